```python
import jax
import jax.numpy as jnp
from jax import lax
import numpy as np

D_MODEL = 1024
BATCH = 8
SEQ = 2048
DEPTH = 4

ATTN_HEADS = 8
HEAD_DIM = 64
ATTN_WIDTH = ATTN_HEADS * HEAD_DIM
IDX_HEADS = 8
IDX_DIM = 64
INDEX_TOPK = 256
Q_BLOCK = 128
ROPE_THETA = 10000.0
CONV_WIDTH = D_MODEL - ATTN_WIDTH
CONV_K = 3
HYB_SIZES = (ATTN_WIDTH, HEAD_DIM, HEAD_DIM, IDX_HEADS * IDX_DIM, IDX_DIM, IDX_HEADS, CONV_WIDTH, CONV_WIDTH, CONV_WIDTH)
HYB_IN = sum(HYB_SIZES)

ML_HEADS = 8
ML_QK_DIM = D_MODEL // (2 * ML_HEADS)
ML_V_DIM = D_MODEL // ML_HEADS
ML_CHUNK = 64
ML_SIZES = (ML_HEADS * ML_QK_DIM, ML_HEADS * ML_QK_DIM, ML_HEADS * ML_V_DIM, ML_HEADS, ML_HEADS, ML_HEADS * ML_V_DIM)
ML_IN = sum(ML_SIZES)

N_GROUPS = 4
EXPERTS_PER_GROUP = 8
N_EXPERTS = N_GROUPS * EXPERTS_PER_GROUP
TOP_K = 2
D_EXPERT = 512
MOE_BLOCK = 256
NORM_EPS = 1e-6

kernel_name = 'hybrid_dsa_conv_mlstm_hmoe'


def _split(a, sizes):
    return jnp.split(a, [int(i) for i in np.cumsum(sizes)[:-1]], axis=-1)


def rms_norm(x, g):
    xf = x.astype(jnp.float32)
    y = xf * lax.rsqrt(jnp.mean(xf * xf, axis=-1, keepdims=True) + NORM_EPS)
    return (y * g.astype(jnp.float32)).astype(x.dtype)


def rope_tables(seq, dim):
    inv = 1.0 / (ROPE_THETA ** (jnp.arange(0, dim, 2, dtype=jnp.float32) / dim))
    ang = jnp.arange(seq, dtype=jnp.float32)[:, None] * inv[None, :]
    return jnp.cos(ang), jnp.sin(ang)


def apply_rope(x, cos, sin):
    half = x.shape[-1] // 2
    xf = x.astype(jnp.float32)
    x1, x2 = xf[..., :half], xf[..., half:]
    c = cos[None, :, None, :]
    s = sin[None, :, None, :]
    return jnp.concatenate([x1 * c - x2 * s, x2 * c + x1 * s], axis=-1).astype(x.dtype)


def dsa_attention(q, k, v, iq, ik, iw):
    B, S, H, d = q.shape
    nb = S // Q_BLOCK
    k_sel = min(INDEX_TOPK, S // 4)
    pos = jnp.arange(S, dtype=jnp.int32)
    ikf = ik.astype(jnp.float32)
    gather = jax.vmap(lambda a, i: a[i])

    def to_blocks(a):
        return jnp.moveaxis(a.reshape((B, nb, Q_BLOCK) + a.shape[2:]), 1, 0)

    def block(args):
        qb, iqb, iwb, pb = args
        s = jnp.einsum('bqhd,bsd->bqhs', iqb.astype(jnp.float32), ikf) * (IDX_DIM ** -0.5)
        score = jnp.einsum('bqhs,bqh->bqs', jax.nn.relu(s), iwb.astype(jnp.float32) * (IDX_HEADS ** -0.5))
        score = jnp.where((pos[None, :] <= pb[:, None])[None], score, -jnp.inf)
        top_val, top_idx = lax.top_k(score, k_sel)
        valid = jnp.isfinite(top_val)
        kg = gather(k, top_idx)
        vg = gather(v, top_idx)
        logits = jnp.einsum('bqhd,bqkd->bqhk', qb, kg).astype(jnp.float32) * (d ** -0.5)
        logits = jnp.where(valid[:, :, None, :], logits, -jnp.inf)
        p = jax.nn.softmax(logits, axis=-1).astype(vg.dtype)
        return jnp.einsum('bqhk,bqkd->bqhd', p, vg)

    out = lax.map(block, (to_blocks(q), to_blocks(iq), to_blocks(iw), pos.reshape(nb, Q_BLOCK)))
    return jnp.moveaxis(out, 0, 1).reshape(B, S, H * d)


def short_gated_conv(bg, cg, u, conv_w):
    S = u.shape[1]
    z = cg * u
    zp = jnp.pad(z, ((0, 0), (CONV_K - 1, 0), (0, 0)))
    y = zp[:, 0:S] * conv_w[0]
    for j in range(1, CONV_K):
        y = y + zp[:, j:j + S] * conv_w[j]
    return bg * y


def hybrid_mixer(h, w_in, q_norm, k_norm, conv_w, w_out, cos, sin):
    B, S, _ = h.shape
    q, k, v, iq, ik, iw, bg, cg, u = _split(h @ w_in, HYB_SIZES)
    q = apply_rope(rms_norm(q.reshape(B, S, ATTN_HEADS, HEAD_DIM), q_norm), cos, sin)
    k = apply_rope(rms_norm(k.reshape(B, S, 1, HEAD_DIM), k_norm), cos, sin)[:, :, 0]
    iq = apply_rope(iq.reshape(B, S, IDX_HEADS, IDX_DIM), cos, sin)
    ik = apply_rope(ik.reshape(B, S, 1, IDX_DIM), cos, sin)[:, :, 0]
    y_attn = dsa_attention(q, k, v, iq, ik, iw)
    y_conv = short_gated_conv(bg, cg, u, conv_w)
    return jnp.concatenate([y_attn, y_conv], axis=-1) @ w_out


def mlstm_chunkwise(q, k, v, i_pre, f_pre):
    B, H, S, dk = q.shape
    dv = v.shape[-1]
    nc = S // ML_CHUNK
    log_f = jax.nn.log_sigmoid(f_pre)
    tril = jnp.tril(jnp.ones((ML_CHUNK, ML_CHUNK), dtype=bool))

    def chunks(a):
        return jnp.moveaxis(a.reshape(a.shape[:2] + (nc, ML_CHUNK) + a.shape[3:]), 2, 0)

    def step(carry, inp):
        C, n, m = carry
        qc, kc, vc, ic, lfc = inp
        b = jnp.cumsum(lfc, axis=-1)
        dmat = b[..., :, None] - b[..., None, :] + ic[..., None, :]
        dmat = jnp.where(tril, dmat, -jnp.inf)
        inter = b + m[..., None]
        m_t = jnp.maximum(inter, jnp.max(dmat, axis=-1))
        w_intra = jnp.exp(dmat - m_t[..., None])
        w_inter = jnp.exp(inter - m_t)
        intra = w_intra * jnp.einsum('bhtd,bhsd->bhts', qc, kc)
        num = w_inter[..., None] * jnp.einsum('bhvd,bhtd->bhtv', C, qc) + jnp.einsum('bhts,bhsv->bhtv', intra, vc)
        den = w_inter * jnp.einsum('bhd,bhtd->bht', n, qc) + jnp.sum(intra, axis=-1)
        hc = num / jnp.maximum(jnp.abs(den), jnp.exp(-m_t))[..., None]
        b_last = b[..., -1]
        g = b_last[..., None] - b + ic
        m_new = jnp.maximum(b_last + m, jnp.max(g, axis=-1))
        decay = jnp.exp(b_last + m - m_new)
        ws = jnp.exp(g - m_new[..., None])
        C_new = decay[..., None, None] * C + jnp.einsum('bhsv,bhsd->bhvd', vc * ws[..., None], kc)
        n_new = decay[..., None] * n + jnp.einsum('bhs,bhsd->bhd', ws, kc)
        return (C_new, n_new, m_new), hc

    init = (jnp.zeros((B, H, dv, dk), jnp.float32), jnp.zeros((B, H, dk), jnp.float32), jnp.zeros((B, H), jnp.float32))
    _, hs = lax.scan(step, init, (chunks(q), chunks(k), chunks(v), chunks(i_pre), chunks(log_f)))
    return jnp.moveaxis(hs, 0, 2).reshape(B, H, S, dv)


def mlstm_mixer(h, w_in, b_gates, out_norm, w_out):
    B, S, _ = h.shape
    q, k, v, ig, fg, og = _split(h @ w_in, ML_SIZES)

    def heads(a, d):
        return jnp.transpose(a.reshape(B, S, ML_HEADS, d), (0, 2, 1, 3)).astype(jnp.float32)

    qh = heads(q, ML_QK_DIM) * (ML_QK_DIM ** -0.5)
    kh = heads(k, ML_QK_DIM)
    vh = heads(v, ML_V_DIM)
    i_pre = jnp.transpose((ig + b_gates[:ML_HEADS]).astype(jnp.float32), (0, 2, 1))
    f_pre = jnp.transpose((fg + b_gates[ML_HEADS:]).astype(jnp.float32), (0, 2, 1))
    hh = jnp.transpose(mlstm_chunkwise(qh, kh, vh, i_pre, f_pre), (0, 2, 1, 3))
    hh = rms_norm(hh, out_norm.reshape(ML_HEADS, ML_V_DIM)).astype(h.dtype).reshape(B, S, ML_HEADS * ML_V_DIM)
    return (jax.nn.sigmoid(og) * hh) @ w_out


def hier_moe(h, w_group, b_group, w_expert, b_expert, w_gate, w_up, w_down):
    B, S, D = h.shape
    T = B * S
    xt = h.reshape(T, D)
    p_group = jax.nn.softmax((xt @ w_group + b_group).astype(jnp.float32), axis=-1)
    g_sel = jnp.argmax(p_group, axis=-1)
    pg = jnp.take_along_axis(p_group, g_sel[:, None], axis=-1)
    e_logits = (xt @ w_expert + b_expert).astype(jnp.float32).reshape(T, N_GROUPS, EXPERTS_PER_GROUP)
    in_group = jnp.take_along_axis(e_logits, g_sel[:, None, None], axis=1)[:, 0]
    top_val, top_idx = lax.top_k(in_group, TOP_K)
    weight = pg * jax.nn.softmax(top_val, axis=-1)
    expert_id = g_sel[:, None].astype(jnp.int32) * EXPERTS_PER_GROUP + top_idx.astype(jnp.int32)
    A = T * TOP_K
    flat_e = expert_id.reshape(A)
    flat_tok = jnp.repeat(jnp.arange(T, dtype=jnp.int32), TOP_K)
    flat_w = weight.reshape(A)
    order = jnp.argsort(flat_e)
    sorted_e = flat_e[order]
    counts = jnp.bincount(flat_e, length=N_EXPERTS)
    blocks_per = (counts + MOE_BLOCK - 1) // MOE_BLOCK
    block_end = jnp.cumsum(blocks_per)
    block_start = block_end - blocks_per
    seg_start = jnp.cumsum(counts) - counts
    rank = jnp.arange(A, dtype=jnp.int32) - seg_start[sorted_e]
    dest = block_start[sorted_e] * MOE_BLOCK + rank
    n_blk = -(-A // MOE_BLOCK) + N_EXPERTS
    R = n_blk * MOE_BLOCK
    row_tok = jnp.zeros((R,), jnp.int32).at[dest].set(flat_tok[order])
    row_w = jnp.zeros((R,), h.dtype).at[dest].set(flat_w[order].astype(h.dtype))
    blk_e = jnp.clip(jnp.searchsorted(block_end, jnp.arange(n_blk), side='right'), 0, N_EXPERTS - 1)
    xs = xt[row_tok].reshape(n_blk, MOE_BLOCK, D)

    def run(args):
        xb, e = args
        return (jax.nn.silu(xb @ w_gate[e]) * (xb @ w_up[e])) @ w_down[e]

    ys = lax.map(run, (xs, blk_e)).reshape(R, D) * row_w[:, None]
    out = jnp.zeros((T, D), h.dtype).at[row_tok].add(ys)
    return out.reshape(B, S, D)


def setup_inputs(seed: int = 0) -> dict:
    key = jax.random.key(seed)
    ks = jax.random.split(key, 24)
    n_even = (DEPTH + 1) // 2
    n_odd = DEPTH // 2
    D = D_MODEL

    def nrm(k, shape, scale):
        return jax.random.normal(k, shape, jnp.float32) * scale

    f_bias = jnp.linspace(3.0, 6.0, ML_HEADS, dtype=jnp.float32)[None, :] + nrm(ks[12], (n_odd, ML_HEADS), 0.1)
    return {
        'x': nrm(ks[0], (BATCH, SEQ, D), 1.0),
        'c': nrm(ks[1], (BATCH, D), 1.0),
        'ada_w': nrm(ks[2], (DEPTH, D, 6 * D), 0.5 * D ** -0.5),
        'ada_b': nrm(ks[3], (DEPTH, 6 * D), 0.02),
        'norm_mix': 1.0 + nrm(ks[4], (DEPTH, D), 0.05),
        'norm_ffn': 1.0 + nrm(ks[5], (DEPTH, D), 0.05),
        'hy_w_in': nrm(ks[6], (n_even, D, HYB_IN), D ** -0.5),
        'hy_q_norm': 1.0 + nrm(ks[7], (n_even, HEAD_DIM), 0.05),
        'hy_k_norm': 1.0 + nrm(ks[8], (n_even, HEAD_DIM), 0.05),
        'hy_conv_w': nrm(ks[9], (n_even, CONV_K, CONV_WIDTH), CONV_K ** -0.5),
        'hy_w_out': nrm(ks[10], (n_even, D, D), D ** -0.5),
        'ml_w_in': nrm(ks[11], (n_odd, D, ML_IN), D ** -0.5),
        'ml_b_gates': jnp.concatenate([nrm(ks[13], (n_odd, ML_HEADS), 0.1), f_bias], axis=-1),
        'ml_out_norm': 1.0 + nrm(ks[14], (n_odd, ML_HEADS * ML_V_DIM), 0.05),
        'ml_w_out': nrm(ks[15], (n_odd, ML_HEADS * ML_V_DIM, D), (ML_HEADS * ML_V_DIM) ** -0.5),
        'moe_w_group': nrm(ks[16], (DEPTH, D, N_GROUPS), D ** -0.5),
        'moe_b_group': nrm(ks[17], (DEPTH, N_GROUPS), 0.01),
        'moe_w_expert': nrm(ks[18], (DEPTH, D, N_EXPERTS), D ** -0.5),
        'moe_b_expert': nrm(ks[19], (DEPTH, N_EXPERTS), 0.01),
        'moe_w_gate': nrm(ks[20], (DEPTH, N_EXPERTS, D, D_EXPERT), D ** -0.5),
        'moe_w_up': nrm(ks[21], (DEPTH, N_EXPERTS, D, D_EXPERT), D ** -0.5),
        'moe_w_down': nrm(ks[22], (DEPTH, N_EXPERTS, D_EXPERT, D), D_EXPERT ** -0.5),
    }


def reference(x, c, ada_w, ada_b, norm_mix, norm_ffn, hy_w_in, hy_q_norm, hy_k_norm, hy_conv_w, hy_w_out,
              ml_w_in, ml_b_gates, ml_out_norm, ml_w_out, moe_w_group, moe_b_group, moe_w_expert, moe_b_expert,
              moe_w_gate, moe_w_up, moe_w_down):
    S = x.shape[1]
    cos, sin = rope_tables(S, HEAD_DIM)
    c_act = jax.nn.silu(c)
    for l in range(DEPTH):
        mod = c_act @ ada_w[l] + ada_b[l]
        sh1, sc1, g1, sh2, sc2, g2 = [m[:, None, :] for m in jnp.split(mod, 6, axis=-1)]
        h = rms_norm(x, norm_mix[l]) * (1 + sc1) + sh1
        j = l // 2
        if l % 2 == 0:
            y = hybrid_mixer(h, hy_w_in[j], hy_q_norm[j], hy_k_norm[j], hy_conv_w[j], hy_w_out[j], cos, sin)
        else:
            y = mlstm_mixer(h, ml_w_in[j], ml_b_gates[j], ml_out_norm[j], ml_w_out[j])
        x = x + g1 * y
        h = rms_norm(x, norm_ffn[l]) * (1 + sc2) + sh2
        x = x + g2 * hier_moe(h, moe_w_group[l], moe_b_group[l], moe_w_expert[l], moe_b_expert[l],
                              moe_w_gate[l], moe_w_up[l], moe_w_down[l])
    return x
```

```python
import functools

import numpy as np
import jax
import jax.numpy as jnp
from jax import lax
from jax.experimental import pallas as pl
from jax.experimental.pallas import tpu as pltpu

F32 = jnp.float32
BF16 = jnp.bfloat16
HIGHEST = lax.Precision.HIGHEST

D_MODEL = 1024
DEPTH = 4
ATTN_HEADS = 8
HEAD_DIM = 64
ATTN_WIDTH = ATTN_HEADS * HEAD_DIM
IDX_HEADS = 8
IDX_DIM = 64
INDEX_TOPK = 256
Q_BLOCK = 128
ROPE_THETA = 10000.0
CONV_WIDTH = D_MODEL - ATTN_WIDTH
CONV_K = 3
ML_HEADS = 8
ML_QK_DIM = 64
ML_V_DIM = 128
N_GROUPS = 4
EXPERTS_PER_GROUP = 8
N_EXPERTS = N_GROUPS * EXPERTS_PER_GROUP
TOP_K = 2
D_EXPERT = 512
MOE_BLOCK = 256
NORM_EPS = 1e-6

LANES = 128
VMEM_LIMIT = 56 * 1024 * 1024
TOKEN_TILE = 256
ML_CHUNK = 128
NEG_INF = float("-inf")


def _cparams(*sem):
    return pltpu.CompilerParams(dimension_semantics=sem, vmem_limit_bytes=VMEM_LIMIT)


def _dot(a, b):
    return jnp.dot(a, b, preferred_element_type=F32)


def _dot_t(a, b):
    return lax.dot_general(a, b, (((1,), (1,)), ((), ())), preferred_element_type=F32)


def _split_dot(a_f32, b_bf16):
    hi = a_f32.astype(BF16)
    lo = (a_f32 - hi.astype(F32)).astype(BF16)
    return _dot(hi, b_bf16) + _dot(lo, b_bf16)


def _ada_kernel(c_ref, w_ref, b_ref, o_ref):
    c = c_ref[...]
    ca = c * jax.nn.sigmoid(c)
    o_ref[0] = jnp.dot(ca, w_ref[0], precision=HIGHEST, preferred_element_type=F32) + b_ref[0]


def _ada_modulation(c, ada_w, ada_b):
    B, D = c.shape
    n_col = ada_w.shape[-1] // D
    return pl.pallas_call(
        _ada_kernel,
        out_shape=jax.ShapeDtypeStruct((DEPTH, B, n_col * D), F32),
        grid=(DEPTH, n_col),
        in_specs=[
            pl.BlockSpec((B, D), lambda l, j: (0, 0)),
            pl.BlockSpec((1, D, D), lambda l, j: (l, 0, j)),
            pl.BlockSpec((1, 1, D), lambda l, j: (l, 0, j)),
        ],
        out_specs=pl.BlockSpec((1, B, D), lambda l, j: (l, 0, j)),
        compiler_params=_cparams("parallel", "parallel"),
        name="ada_modulation",
    )(c, ada_w, ada_b.reshape(DEPTH, 1, n_col * D))


def _modulated_norm(x, gain, scale, shift):
    y = x * lax.rsqrt(jnp.mean(x * x, axis=-1, keepdims=True) + NORM_EPS)
    return y * gain * (1.0 + scale) + shift


def _rope(x, cos, sin_signed, first_half):
    w = x.shape[-1]
    partner = jnp.where(first_half, pltpu.roll(x, w - HEAD_DIM // 2, 1), pltpu.roll(x, HEAD_DIM // 2, 1))
    return x * cos + partner * sin_signed


HYB_COLS = 5 * 512 + 2 * LANES


def _hyb_in_kernel(x_ref, gain_ref, sc_ref, sh_ref, w_ref, cos_ref, sin_ref, qn_ref, kn_ref, grp_ref,
                   q_ref, iq_ref, bcu_ref, kv_ref, ik_ref, iw_ref):
    h = _modulated_norm(x_ref[0], gain_ref[...], sc_ref[0], sh_ref[0])
    p = _dot(h.astype(BF16), w_ref[...])
    cos = cos_ref[...]
    sin = sin_ref[...]
    lane = lax.broadcasted_iota(jnp.int32, (1, ATTN_WIDTH), 1)
    first_half = (lane % HEAD_DIM) < (HEAD_DIM // 2)
    fh128 = first_half[:, :LANES]
    lane128 = lane[:, :LANES]

    q = p[:, 0:512]
    ms = _split_dot(q * q, grp_ref[...]) * (1.0 / HEAD_DIM)
    q = q * lax.rsqrt(ms + NORM_EPS) * qn_ref[...]
    q_ref[0] = (_rope(q, cos, sin, first_half) * (HEAD_DIM ** -0.5)).astype(BF16)

    iq = p[:, 512:1024]
    iq_ref[0] = (_rope(iq, cos, sin, first_half) * (IDX_DIM ** -0.5)).astype(BF16)

    bcu_ref[0] = p[:, 1024:2560]

    kv = p[:, 2560:2688]
    is_k = lane128 < HEAD_DIM
    kk = jnp.where(is_k, kv, 0.0)
    ms_k = jnp.sum(kk * kk, axis=-1, keepdims=True) * (1.0 / HEAD_DIM)
    kn = kv * lax.rsqrt(ms_k + NORM_EPS) * kn_ref[...]
    kr = _rope(kn, cos[:, :LANES], sin[:, :LANES], fh128)
    kv_ref[0] = jnp.where(is_k, kr, kv).astype(BF16)

    sm = p[:, 2688:2816]
    ikr = _rope(sm, cos[:, :LANES], sin[:, :LANES], fh128)
    ik_ref[0] = jnp.where(is_k, ikr, 0.0).astype(BF16)
    iw_ref[0] = sm


def _hyb_in(x, gain, sc, sh, w_pad, cos_t, sin_t, qn_t, kn_t, grp):
    B, S, D = x.shape
    tm = TOKEN_TILE
    row = lambda b, j: (b, j, 0)
    per_b = lambda b, j: (b, 0, 0)
    const2 = lambda b, j: (0, 0)
    tab = lambda b, j: (j, 0)
    return pl.pallas_call(
        _hyb_in_kernel,
        out_shape=(
            jax.ShapeDtypeStruct((B, S, 512), BF16),
            jax.ShapeDtypeStruct((B, S, 512), BF16),
            jax.ShapeDtypeStruct((B, S, 1536), F32),
            jax.ShapeDtypeStruct((B, S, LANES), BF16),
            jax.ShapeDtypeStruct((B, S, LANES), BF16),
            jax.ShapeDtypeStruct((B, S, LANES), F32),
        ),
        grid=(B, S // tm),
        in_specs=[
            pl.BlockSpec((1, tm, D), row),
            pl.BlockSpec((1, D), const2),
            pl.BlockSpec((1, 1, D), per_b),
            pl.BlockSpec((1, 1, D), per_b),
            pl.BlockSpec((D, HYB_COLS), const2),
            pl.BlockSpec((tm, 512), tab),
            pl.BlockSpec((tm, 512), tab),
            pl.BlockSpec((1, 512), const2),
            pl.BlockSpec((1, LANES), const2),
            pl.BlockSpec((512, 512), const2),
        ],
        out_specs=(
            pl.BlockSpec((1, tm, 512), row),
            pl.BlockSpec((1, tm, 512), row),
            pl.BlockSpec((1, tm, 1536), row),
            pl.BlockSpec((1, tm, LANES), row),
            pl.BlockSpec((1, tm, LANES), row),
            pl.BlockSpec((1, tm, LANES), row),
        ),
        compiler_params=_cparams("parallel", "parallel"),
        name="hybrid_in_proj",
    )(x, gain, sc, sh, w_pad, cos_t, sin_t, qn_t, kn_t, grp)


def _count(mask):
    return jnp.sum(jnp.where(mask, 1.0, 0.0), axis=-1, keepdims=True)


def _dsa_kernel(q_ref, iq_ref, iw_ref, kv_ref, ik_ref, o_ref, sc_ref, bias_ref):
    S = kv_ref.shape[1]
    qb = pl.program_id(1)
    kf = float(INDEX_TOPK)

    ik = ik_ref[0][:, :IDX_DIM]
    iw = iw_ref[0]
    acc = jnp.zeros((Q_BLOCK, S), F32)
    for hd in range(IDX_HEADS):
        s_h = _dot_t(iq_ref[0][:, hd * IDX_DIM:(hd + 1) * IDX_DIM], ik)
        w_h = iw[:, IDX_DIM + hd:IDX_DIM + hd + 1] * (IDX_HEADS ** -0.5)
        acc = acc + jnp.maximum(s_h, 0.0) * w_h
    qpos = qb * Q_BLOCK + lax.broadcasted_iota(jnp.int32, (Q_BLOCK, 1), 0)
    kpos = lax.broadcasted_iota(jnp.int32, (1, S), 1)
    causal = kpos <= qpos
    sc_ref[...] = jnp.where(causal, acc, NEG_INF)

    @pl.when(qb * Q_BLOCK + Q_BLOCK <= INDEX_TOPK)
    def _():
        bias_ref[...] = jnp.where(causal, 0.0, NEG_INF)

    @pl.when(qb * Q_BLOCK + Q_BLOCK > INDEX_TOPK)
    def _():
        sc = sc_ref[...]
        row_max = jnp.max(sc, axis=-1, keepdims=True)
        row_min = jnp.min(jnp.where(causal, sc, jnp.inf), axis=-1, keepdims=True)
        top_tied = _count(sc >= row_max) >= kf

        def bisect(_, carry):
            lo, hi = carry
            mid = 0.5 * lo + 0.5 * hi
            ge = _count(sc_ref[...] >= mid) >= kf
            return jnp.where(ge, mid, lo), jnp.where(ge, hi, mid)

        lo, hi = lax.fori_loop(0, 18, bisect, (row_min, row_max))

        def refine_cond(carry):
            it, _, _, done = carry
            return jnp.logical_and(it < S, jnp.min(done) < 0.5)

        def refine(carry):
            it, hi, thr, done = carry
            scv = sc_ref[...]
            m = jnp.max(jnp.where(scv < hi, scv, NEG_INF), axis=-1, keepdims=True)
            hit = _count(scv >= m) >= kf
            fin = done > 0.5
            thr = jnp.where(fin, thr, m)
            hi = jnp.where(jnp.logical_or(fin, hit), hi, m)
            done = jnp.where(hit, 1.0, done)
            return it + 1, hi, thr, done

        done0 = jnp.where(top_tied, 1.0, 0.0)
        _, _, thr, _ = lax.while_loop(refine_cond, refine, (jnp.int32(0), hi, row_max, done0))

        gt = sc > thr
        eq = sc == thr
        need = kf - _count(gt)
        n_eq = _count(eq)
        bias_ref[...] = jnp.where(sc >= thr, 0.0, NEG_INF)

        @pl.when(jnp.max(n_eq - need) > 0.5)
        def _():
            r_i = lax.broadcasted_iota(jnp.int32, (LANES, LANES), 0)
            c_i = lax.broadcasted_iota(jnp.int32, (LANES, LANES), 1)
            upper = jnp.where(r_i <= c_i, 1.0, 0.0).astype(BF16)
            carry_cnt = jnp.zeros((Q_BLOCK, 1), F32)
            for j in range(S // LANES):
                blk = sc_ref[:, j * LANES:(j + 1) * LANES]
                eq_j = blk == thr
                eq_f = jnp.where(eq_j, 1.0, 0.0)
                rank = _dot(eq_f.astype(BF16), upper) + carry_cnt
                keep = jnp.logical_or(blk > thr, jnp.logical_and(eq_j, rank <= need))
                bias_ref[:, j * LANES:(j + 1) * LANES] = jnp.where(keep, 0.0, NEG_INF)
                carry_cnt = carry_cnt + jnp.sum(eq_f, axis=-1, keepdims=True)

    k = kv_ref[0][:, :HEAD_DIM]
    v = kv_ref[0][:, HEAD_DIM:]
    bias = bias_ref[...]
    for hd in range(ATTN_HEADS):
        logits = _dot_t(q_ref[0][:, hd * HEAD_DIM:(hd + 1) * HEAD_DIM], k) + bias
        mx = jnp.max(logits, axis=-1, keepdims=True)
        p = jnp.exp(logits - mx)
        den = jnp.sum(p, axis=-1, keepdims=True)
        o_ref[0, :, hd * HEAD_DIM:(hd + 1) * HEAD_DIM] = _dot(p.astype(BF16), v) / den


def _dsa_attention(q, iq, iw, kv, ik):
    B, S, _ = q.shape
    blk = lambda b, j: (b, j, 0)
    per_b = lambda b, j: (b, 0, 0)
    return pl.pallas_call(
        _dsa_kernel,
        out_shape=jax.ShapeDtypeStruct((B, S, ATTN_WIDTH), F32),
        grid=(B, S // Q_BLOCK),
        in_specs=[
            pl.BlockSpec((1, Q_BLOCK, ATTN_WIDTH), blk),
            pl.BlockSpec((1, Q_BLOCK, IDX_HEADS * IDX_DIM), blk),
            pl.BlockSpec((1, Q_BLOCK, LANES), blk),
            pl.BlockSpec((1, S, LANES), per_b),
            pl.BlockSpec((1, S, LANES), per_b),
        ],
        out_specs=pl.BlockSpec((1, Q_BLOCK, ATTN_WIDTH), blk),
        scratch_shapes=[pltpu.VMEM((Q_BLOCK, S), F32), pltpu.VMEM((Q_BLOCK, S), F32)],
        compiler_params=_cparams("parallel", "parallel"),
        name="dsa_attention",
    )(q, iq, iw, kv, ik)


def _hyb_out_kernel(ya_ref, bcu_ref, halo_ref, cw_ref, w_ref, x_ref, g_ref, o_ref):
    j = pl.program_id(1)
    tm = ya_ref.shape[1]
    bcu = bcu_ref[0]
    bg = bcu[:, 0:512]
    z = bcu[:, 512:1024] * bcu[:, 1024:1536]
    halo = halo_ref[0]
    zh = halo[:, 512:1024] * halo[:, 1024:1536]
    zh = jnp.where(j > 0, zh, 0.0)
    row = lax.broadcasted_iota(jnp.int32, (tm, 1), 0)
    z1 = jnp.where(row >= 1, pltpu.roll(z, 1, 0), zh[7:8, :])
    z2 = jnp.where(row >= 2, pltpu.roll(z, 2, 0), jnp.where(row == 1, zh[7:8, :], zh[6:7, :]))
    cw = cw_ref[...]
    y_conv = bg * (z2 * cw[0:1, :] + z1 * cw[1:2, :] + z * cw[2:3, :])
    y = _dot(ya_ref[0].astype(BF16), w_ref[0:512, :]) + _dot(y_conv.astype(BF16), w_ref[512:1024, :])
    o_ref[0] = x_ref[0] + g_ref[0] * y


def _hyb_out(y_attn, bcu, conv_w, w_out_bf, x, g1):
    B, S, D = x.shape
    tm = TOKEN_TILE
    row = lambda b, j: (b, j, 0)
    per_b = lambda b, j: (b, 0, 0)
    const2 = lambda b, j: (0, 0)
    halo = lambda b, j: (b, jnp.maximum(j * (tm // 8) - 1, 0), 0)
    return pl.pallas_call(
        _hyb_out_kernel,
        out_shape=jax.ShapeDtypeStruct((B, S, D), F32),
        grid=(B, S // tm),
        in_specs=[
            pl.BlockSpec((1, tm, 512), row),
            pl.BlockSpec((1, tm, 1536), row),
            pl.BlockSpec((1, 8, 1536), halo),
            pl.BlockSpec((CONV_K, CONV_WIDTH), const2),
            pl.BlockSpec((D, D), const2),
            pl.BlockSpec((1, tm, D), row),
            pl.BlockSpec((1, 1, D), per_b),
        ],
        out_specs=pl.BlockSpec((1, tm, D), row),
        compiler_params=_cparams("parallel", "parallel"),
        name="hybrid_out_proj",
    )(y_attn, bcu, bcu, conv_w, w_out_bf, x, g1)


ML_COLS = 512 + 512 + 1024 + 1024 + LANES


def _ml_in_kernel(x_ref, gain_ref, sc_ref, sh_ref, w_ref, bias_ref, q_ref, k_ref, v_ref, og_ref, gt_ref):
    h = _modulated_norm(x_ref[0], gain_ref[...], sc_ref[0], sh_ref[0])
    p = _dot(h.astype(BF16), w_ref[...])
    q_ref[0] = (p[:, 0:512] * (ML_QK_DIM ** -0.5)).astype(BF16)
    k_ref[0] = p[:, 512:1024].astype(BF16)
    v_ref[0] = p[:, 1024:2048].astype(BF16)
    og_ref[0] = p[:, 2048:3072]
    gt_ref[0] = p[:, 3072:3200] + bias_ref[...]


def _ml_in(x, gain, sc, sh, w_pad, gate_bias):
    B, S, D = x.shape
    tm = TOKEN_TILE
    row = lambda b, j: (b, j, 0)
    per_b = lambda b, j: (b, 0, 0)
    const2 = lambda b, j: (0, 0)
    return pl.pallas_call(
        _ml_in_kernel,
        out_shape=(
            jax.ShapeDtypeStruct((B, S, 512), BF16),
            jax.ShapeDtypeStruct((B, S, 512), BF16),
            jax.ShapeDtypeStruct((B, S, 1024), BF16),
            jax.ShapeDtypeStruct((B, S, 1024), F32),
            jax.ShapeDtypeStruct((B, S, LANES), F32),
        ),
        grid=(B, S // tm),
        in_specs=[
            pl.BlockSpec((1, tm, D), row),
            pl.BlockSpec((1, D), const2),
            pl.BlockSpec((1, 1, D), per_b),
            pl.BlockSpec((1, 1, D), per_b),
            pl.BlockSpec((D, ML_COLS), const2),
            pl.BlockSpec((1, LANES), const2),
        ],
        out_specs=(
            pl.BlockSpec((1, tm, 512), row),
            pl.BlockSpec((1, tm, 512), row),
            pl.BlockSpec((1, tm, 1024), row),
            pl.BlockSpec((1, tm, 1024), row),
            pl.BlockSpec((1, tm, LANES), row),
        ),
        compiler_params=_cparams("parallel", "parallel"),
        name="mlstm_in_proj",
    )(x, gain, sc, sh, w_pad, gate_bias)


def _log_sigmoid(f):
    return jnp.minimum(f, 0.0) - jnp.log1p(jnp.exp(-jnp.abs(f)))


def _mlstm_kernel(q_ref, k_ref, v_ref, grow_ref, gcol_ref, gain_ref, o_ref, c_ref, n_ref, m_ref):
    L = ML_CHUNK
    S = q_ref.shape[1]
    c_ref[...] = jnp.zeros_like(c_ref)
    n_ref[...] = jnp.zeros_like(n_ref)
    m_ref[...] = jnp.zeros_like(m_ref)
    r_i = lax.broadcasted_iota(jnp.int32, (L, L), 0)
    c_i = lax.broadcasted_iota(jnp.int32, (L, L), 1)
    lower = jnp.where(c_i <= r_i, 1.0, 0.0)
    upper = jnp.where(r_i <= c_i, 1.0, 0.0)
    tril = c_i <= r_i

    def chunk(c, carry):
        r0 = pl.multiple_of(c * L, L)
        rows = grow_ref[0, 0, c]
        cols = gcol_ref[0, 0, c]
        b_rows = jnp.dot(_log_sigmoid(rows), upper, precision=HIGHEST, preferred_element_type=F32)
        b_cols = jnp.dot(lower, _log_sigmoid(cols), precision=HIGHEST, preferred_element_type=F32)
        for hh in range(2):
            q = q_ref[0, pl.ds(r0, L), hh * ML_QK_DIM:(hh + 1) * ML_QK_DIM]
            k = k_ref[0, pl.ds(r0, L), hh * ML_QK_DIM:(hh + 1) * ML_QK_DIM]
            v = v_ref[0, pl.ds(r0, L), hh * ML_V_DIM:(hh + 1) * ML_V_DIM]
            i_row = rows[hh:hh + 1, :]
            i_col = cols[:, hh:hh + 1]
            b_row = b_rows[2 + hh:3 + hh, :]
            b_col = b_cols[:, 2 + hh:3 + hh]
            b_last = b_row[:, L - 1:L]
            m_prev = m_ref[hh]
            ct = c_ref[hh]
            n_row = n_ref[hh]

            dmat = jnp.where(tril, b_col - b_row + i_row, NEG_INF)
            inter = b_col + m_prev
            m_t = jnp.maximum(inter, jnp.max(dmat, axis=-1, keepdims=True))
            w_intra = jnp.exp(dmat - m_t)
            w_inter = jnp.exp(inter - m_t)
            intra = w_intra * _dot_t(q, k)
            num = w_inter * _dot(q, ct.astype(BF16)) + _dot(intra.astype(BF16), v)
            qn = jnp.sum(q.astype(F32) * n_row, axis=-1, keepdims=True)
            den = w_inter * qn + jnp.sum(intra, axis=-1, keepdims=True)
            hc = num / jnp.maximum(jnp.abs(den), jnp.exp(-m_t))
            y = hc * lax.rsqrt(jnp.mean(hc * hc, axis=-1, keepdims=True) + NORM_EPS)
            o_ref[0, pl.ds(r0, L), hh * ML_V_DIM:(hh + 1) * ML_V_DIM] = (
                y * gain_ref[:, hh * ML_V_DIM:(hh + 1) * ML_V_DIM])

            g_row = b_last - b_row + i_row
            g_col = b_last - b_col + i_col
            m_new = jnp.maximum(b_last + m_prev, jnp.max(g_row, axis=-1, keepdims=True))
            decay = jnp.exp(b_last + m_prev - m_new)
            kw = k.astype(F32) * jnp.exp(g_col - m_new)
            c_ref[hh] = decay * ct + lax.dot_general(
                kw.astype(BF16), v, (((0,), (0,)), ((), ())), preferred_element_type=F32)
            n_ref[hh] = decay * n_row + jnp.sum(kw, axis=0, keepdims=True)
            m_ref[hh] = m_new
        return carry

    lax.fori_loop(0, S // L, chunk, 0)


def _mlstm(q, k, v, g_rows, g_cols, out_gain):
    B, S, _ = q.shape
    nc = S // ML_CHUNK
    pairs = ML_HEADS // 2
    return pl.pallas_call(
        _mlstm_kernel,
        out_shape=jax.ShapeDtypeStruct((B, S, ML_HEADS * ML_V_DIM), F32),
        grid=(B, pairs),
        in_specs=[
            pl.BlockSpec((1, S, 2 * ML_QK_DIM), lambda b, p: (b, 0, p)),
            pl.BlockSpec((1, S, 2 * ML_QK_DIM), lambda b, p: (b, 0, p)),
            pl.BlockSpec((1, S, 2 * ML_V_DIM), lambda b, p: (b, 0, p)),
            pl.BlockSpec((1, 1, nc, 8, ML_CHUNK), lambda b, p: (b, p, 0, 0, 0)),
            pl.BlockSpec((1, 1, nc, ML_CHUNK, LANES), lambda b, p: (b, p, 0, 0, 0)),
            pl.BlockSpec((1, 2 * ML_V_DIM), lambda b, p: (0, p)),
        ],
        out_specs=pl.BlockSpec((1, S, 2 * ML_V_DIM), lambda b, p: (b, 0, p)),
        scratch_shapes=[
            pltpu.VMEM((2, ML_QK_DIM, ML_V_DIM), F32),
            pltpu.VMEM((2, 1, ML_QK_DIM), F32),
            pltpu.VMEM((2, 1, 1), F32),
        ],
        compiler_params=_cparams("parallel", "parallel"),
        name="mlstm_chunkwise",
    )(q, k, v, g_rows, g_cols, out_gain)


def _ml_out_kernel(hh_ref, og_ref, w_ref, x_ref, g_ref, o_ref):
    a = jax.nn.sigmoid(og_ref[0]) * hh_ref[0]
    o_ref[0] = x_ref[0] + g_ref[0] * _dot(a.astype(BF16), w_ref[...])


def _ml_out(hh, og, w_out_bf, x, g1):
    B, S, D = x.shape
    tm = TOKEN_TILE
    row = lambda b, j: (b, j, 0)
    per_b = lambda b, j: (b, 0, 0)
    const2 = lambda b, j: (0, 0)
    return pl.pallas_call(
        _ml_out_kernel,
        out_shape=jax.ShapeDtypeStruct((B, S, D), F32),
        grid=(B, S // tm),
        in_specs=[
            pl.BlockSpec((1, tm, D), row),
            pl.BlockSpec((1, tm, D), row),
            pl.BlockSpec((D, D), const2),
            pl.BlockSpec((1, tm, D), row),
            pl.BlockSpec((1, 1, D), per_b),
        ],
        out_specs=pl.BlockSpec((1, tm, D), row),
        compiler_params=_cparams("parallel", "parallel"),
        name="mlstm_out_proj",
    )(hh, og, w_out_bf, x, g1)


def _first_argmax(x, lane, width):
    mx = jnp.max(x, axis=-1, keepdims=True)
    idx = jnp.min(jnp.where(x == mx, lane, width), axis=-1, keepdims=True)
    return mx, idx


def _router_kernel(x_ref, gain_ref, sc_ref, sh_ref, whi_ref, wlo_ref, b_ref, h_ref, r_ref):
    h = _modulated_norm(x_ref[0], gain_ref[...], sc_ref[0], sh_ref[0])
    h_hi = h.astype(BF16)
    h_ref[0] = h_hi
    h_lo = (h - h_hi.astype(F32)).astype(BF16)
    whi = whi_ref[...]
    logits = _dot(h_hi, whi) + _dot(h_lo, whi) + _dot(h_hi, wlo_ref[...]) + b_ref[...]
    lane = lax.broadcasted_iota(jnp.int32, (1, LANES), 1)
    lg = jnp.where(lane < N_GROUPS, logits, NEG_INF)
    g_max, g_sel = _first_argmax(lg, lane, LANES)
    pg = 1.0 / jnp.sum(jnp.exp(lg - g_max), axis=-1, keepdims=True)
    e_lane = lane - N_GROUPS
    in_grp = jnp.logical_and(e_lane >= g_sel * EXPERTS_PER_GROUP, e_lane < (g_sel + 1) * EXPERTS_PER_GROUP)
    le = jnp.where(in_grp, logits, NEG_INF)
    v1, i1 = _first_argmax(le, lane, LANES)
    le2 = jnp.where(lane == i1, NEG_INF, le)
    v2, i2 = _first_argmax(le2, lane, LANES)
    e2 = jnp.exp(v2 - v1)
    w1 = pg / (1.0 + e2)
    w2 = pg * e2 / (1.0 + e2)
    out = jnp.where(lane == 0, (i1 - N_GROUPS).astype(F32), 0.0)
    out = jnp.where(lane == 1, (i2 - N_GROUPS).astype(F32), out)
    out = jnp.where(lane == 2, w1, out)
    out = jnp.where(lane == 3, w2, out)
    r_ref[0] = out


def _router(x, gain, sc, sh, w_hi, w_lo, bias):
    B, S, D = x.shape
    tm = TOKEN_TILE
    row = lambda b, j: (b, j, 0)
    per_b = lambda b, j: (b, 0, 0)
    const2 = lambda b, j: (0, 0)
    return pl.pallas_call(
        _router_kernel,
        out_shape=(
            jax.ShapeDtypeStruct((B, S, D), BF16),
            jax.ShapeDtypeStruct((B, S, LANES), F32),
        ),
        grid=(B, S // tm),
        in_specs=[
            pl.BlockSpec((1, tm, D), row),
            pl.BlockSpec((1, D), const2),
            pl.BlockSpec((1, 1, D), per_b),
            pl.BlockSpec((1, 1, D), per_b),
            pl.BlockSpec((D, LANES), const2),
            pl.BlockSpec((D, LANES), const2),
            pl.BlockSpec((1, LANES), const2),
        ],
        out_specs=(
            pl.BlockSpec((1, tm, D), row),
            pl.BlockSpec((1, tm, LANES), row),
        ),
        compiler_params=_cparams("parallel", "parallel"),
        name="moe_router",
    )(x, gain, sc, sh, w_hi, w_lo, bias)


def _experts_kernel(blk_e_ref, n_used_ref, x_ref, wg_ref, wu_ref, wd_ref, o_ref):
    i = pl.program_id(0)

    @pl.when(i < n_used_ref[0])
    def _():
        x = x_ref[...]
        a = _dot(x, wg_ref[0].astype(BF16))
        u = _dot(x, wu_ref[0].astype(BF16))
        act = a * jax.nn.sigmoid(a) * u
        o_ref[...] = _dot(act.astype(BF16), wd_ref[0].astype(BF16))

    @pl.when(i >= n_used_ref[0])
    def _():
        o_ref[...] = jnp.zeros_like(o_ref)


def _experts(blk_e, n_used, xs, w_gate, w_up, w_down):
    R, D = xs.shape
    n_blk = R // MOE_BLOCK
    grid_spec = pltpu.PrefetchScalarGridSpec(
        num_scalar_prefetch=2,
        grid=(n_blk,),
        in_specs=[
            pl.BlockSpec((MOE_BLOCK, D), lambda i, be, nu: (i, 0)),
            pl.BlockSpec((1, D, D_EXPERT), lambda i, be, nu: (be[i], 0, 0)),
            pl.BlockSpec((1, D, D_EXPERT), lambda i, be, nu: (be[i], 0, 0)),
            pl.BlockSpec((1, D_EXPERT, D), lambda i, be, nu: (be[i], 0, 0)),
        ],
        out_specs=pl.BlockSpec((MOE_BLOCK, D), lambda i, be, nu: (i, 0)),
    )
    return pl.pallas_call(
        _experts_kernel,
        out_shape=jax.ShapeDtypeStruct((R, D), F32),
        grid_spec=grid_spec,
        compiler_params=_cparams("arbitrary"),
        name="moe_experts",
    )(blk_e, n_used, xs, w_gate, w_up, w_down)


def _combine_kernel(x_ref, g_ref, y0_ref, y1_ref, r_ref, o_ref):
    r = r_ref[0]
    y = y0_ref[0] * r[:, 2:3] + y1_ref[0] * r[:, 3:4]
    o_ref[0] = x_ref[0] + g_ref[0] * y


def _combine(x, g2, y0, y1, route):
    B, S, D = x.shape
    tm = TOKEN_TILE
    row = lambda b, j: (b, j, 0)
    per_b = lambda b, j: (b, 0, 0)
    return pl.pallas_call(
        _combine_kernel,
        out_shape=jax.ShapeDtypeStruct((B, S, D), F32),
        grid=(B, S // tm),
        in_specs=[
            pl.BlockSpec((1, tm, D), row),
            pl.BlockSpec((1, 1, D), per_b),
            pl.BlockSpec((1, tm, D), row),
            pl.BlockSpec((1, tm, D), row),
            pl.BlockSpec((1, tm, LANES), row),
        ],
        out_specs=pl.BlockSpec((1, tm, D), row),
        compiler_params=_cparams("parallel", "parallel"),
        name="moe_combine",
    )(x, g2, y0, y1, route)


def _moe_dispatch(route, T):
    A = T * TOP_K
    flat_e = route[:, :TOP_K].astype(jnp.int32).reshape(A)
    order = jnp.argsort(flat_e)
    sorted_e = flat_e[order]
    counts = jnp.bincount(flat_e, length=N_EXPERTS)
    blocks_per = (counts + MOE_BLOCK - 1) // MOE_BLOCK
    block_end = jnp.cumsum(blocks_per)
    block_start = block_end - blocks_per
    seg_start = jnp.cumsum(counts) - counts
    rank = jnp.arange(A, dtype=jnp.int32) - seg_start[sorted_e]
    dest_sorted = (block_start[sorted_e] * MOE_BLOCK + rank).astype(jnp.int32)
    n_blk = -(-A // MOE_BLOCK) + N_EXPERTS
    R = n_blk * MOE_BLOCK
    row_tok = jnp.zeros((R,), jnp.int32).at[dest_sorted].set((order // TOP_K).astype(jnp.int32))
    dest = jnp.zeros((A,), jnp.int32).at[order].set(dest_sorted)
    blk_e = jnp.clip(jnp.searchsorted(block_end, jnp.arange(n_blk), side='right'), 0, N_EXPERTS - 1)
    return row_tok, dest.reshape(T, TOP_K), blk_e.astype(jnp.int32), block_end[-1:].astype(jnp.int32)


def _rope_tables(S):
    inv = 1.0 / (ROPE_THETA ** (jnp.arange(0, HEAD_DIM, 2, dtype=F32) / HEAD_DIM))
    ang = jnp.arange(S, dtype=F32)[:, None] * inv[None, :]
    cos, sin = jnp.cos(ang), jnp.sin(ang)
    cos_h = jnp.concatenate([cos, cos], axis=-1)
    sin_h = jnp.concatenate([-sin, sin], axis=-1)
    return jnp.tile(cos_h, (1, ATTN_HEADS)), jnp.tile(sin_h, (1, ATTN_HEADS))


def _pad_cols(w, width):
    return jnp.pad(w, ((0, 0), (0, width - w.shape[1])))


def kernel(x, c, ada_w, ada_b, norm_mix, norm_ffn, hy_w_in, hy_q_norm, hy_k_norm, hy_conv_w, hy_w_out, ml_w_in, ml_b_gates, ml_out_norm, ml_w_out, moe_w_group, moe_b_group, moe_w_expert, moe_b_expert, moe_w_gate, moe_w_up, moe_w_down):
    B, S, D = x.shape
    T = B * S
    cos_t, sin_t = _rope_tables(S)
    mod = _ada_modulation(c, ada_w, ada_b).reshape(DEPTH, B, 6, 1, D)
    r_i = np.arange(ATTN_WIDTH)
    grp = jnp.asarray((r_i[:, None] // HEAD_DIM) == (r_i[None, :] // HEAD_DIM), dtype=BF16)

    for l in range(DEPTH):
        sh1, sc1, g1, sh2, sc2, g2 = [mod[l, :, i] for i in range(6)]
        gain1 = norm_mix[l].reshape(1, D)
        j = l // 2
        if l % 2 == 0:
            w = hy_w_in[j]
            o = np.cumsum((0,) + (ATTN_WIDTH, HEAD_DIM, HEAD_DIM, IDX_HEADS * IDX_DIM, IDX_DIM, IDX_HEADS,
                                  CONV_WIDTH, CONV_WIDTH, CONV_WIDTH))
            wq, wk, wv, wiq, wik, wiw, wbg, wcg, wu = [w[:, o[i]:o[i + 1]] for i in range(9)]
            w_pad = jnp.concatenate(
                [wq, wiq, wbg, wcg, wu, wk, wv, _pad_cols(jnp.concatenate([wik, wiw], axis=1), LANES)],
                axis=1).astype(BF16)
            qn_t = jnp.tile(hy_q_norm[j], ATTN_HEADS).reshape(1, ATTN_WIDTH)
            kn_t = jnp.tile(hy_k_norm[j], LANES // HEAD_DIM).reshape(1, LANES)
            q, iq, bcu, kv, ik, iw = _hyb_in(x, gain1, sc1, sh1, w_pad, cos_t, sin_t, qn_t, kn_t, grp)
            y_attn = _dsa_attention(q, iq, iw, kv, ik)
            x = _hyb_out(y_attn, bcu, hy_conv_w[j], hy_w_out[j].astype(BF16), x, g1)
        else:
            w = ml_w_in[j]
            hq = ML_HEADS * ML_QK_DIM
            hv = ML_HEADS * ML_V_DIM
            wq, wk, wv = w[:, :hq], w[:, hq:2 * hq], w[:, 2 * hq:2 * hq + hv]
            wg = w[:, 2 * hq + hv:2 * hq + hv + 2 * ML_HEADS]
            wo = w[:, 2 * hq + hv + 2 * ML_HEADS:]
            w_pad = jnp.concatenate([wq, wk, wv, wo, _pad_cols(wg, LANES)], axis=1).astype(BF16)
            gate_bias = jnp.pad(ml_b_gates[j], (0, LANES - 2 * ML_HEADS)).reshape(1, LANES)
            q, k, v, og, gates = _ml_in(x, gain1, sc1, sh1, w_pad, gate_bias)
            nc = S // ML_CHUNK
            pairs = ML_HEADS // 2
            gi = gates[:, :, :ML_HEADS].reshape(B, S, pairs, 2)
            gf = gates[:, :, ML_HEADS:2 * ML_HEADS].reshape(B, S, pairs, 2)
            gp = jnp.concatenate([gi, gf], axis=-1)
            gp = jnp.transpose(gp, (0, 2, 1, 3)).reshape(B, pairs, nc, ML_CHUNK, 4)
            g_cols = jnp.pad(gp, ((0, 0),) * 4 + ((0, LANES - 4),))
            g_rows = jnp.pad(jnp.swapaxes(gp, -1, -2), ((0, 0),) * 3 + ((0, 4), (0, 0)))
            hh = _mlstm(q, k, v, g_rows, g_cols, ml_out_norm[j].reshape(1, hv))
            x = _ml_out(hh, og, ml_w_out[j].astype(BF16), x, g1)

        w_r = _pad_cols(jnp.concatenate([moe_w_group[l], moe_w_expert[l]], axis=1), LANES)
        w_hi = w_r.astype(BF16)
        w_lo = (w_r - w_hi.astype(F32)).astype(BF16)
        b_r = jnp.pad(jnp.concatenate([moe_b_group[l], moe_b_expert[l]]), (0, LANES - N_GROUPS - N_EXPERTS))
        h2, route = _router(x, norm_ffn[l].reshape(1, D), sc2, sh2, w_hi, w_lo, b_r.reshape(1, LANES))
        row_tok, dest, blk_e, n_used = _moe_dispatch(route.reshape(T, LANES), T)
        xs = jnp.take(h2.reshape(T, D), row_tok, axis=0)
        ys = _experts(blk_e, n_used, xs, moe_w_gate[l], moe_w_up[l], moe_w_down[l])
        y0 = jnp.take(ys, dest[:, 0], axis=0).reshape(B, S, D)
        y1 = jnp.take(ys, dest[:, 1], axis=0).reshape(B, S, D)
        x = _combine(x, g2, y0, y1, route)
    return x
```

```python
import functools

import numpy as np
import jax
import jax.numpy as jnp
from jax import lax
from jax.experimental import pallas as pl
from jax.experimental.pallas import tpu as pltpu

F32 = jnp.float32
BF16 = jnp.bfloat16
HIGHEST = lax.Precision.HIGHEST

D_MODEL = 1024
DEPTH = 4
ATTN_HEADS = 8
HEAD_DIM = 64
ATTN_WIDTH = ATTN_HEADS * HEAD_DIM
IDX_HEADS = 8
IDX_DIM = 64
INDEX_TOPK = 256
Q_BLOCK = 128
ROPE_THETA = 10000.0
CONV_WIDTH = D_MODEL - ATTN_WIDTH
CONV_K = 3
ML_HEADS = 8
ML_QK_DIM = 64
ML_V_DIM = 128
N_GROUPS = 4
EXPERTS_PER_GROUP = 8
N_EXPERTS = N_GROUPS * EXPERTS_PER_GROUP
TOP_K = 2
D_EXPERT = 512
MOE_BLOCK = 256
NORM_EPS = 1e-6

LANES = 128
VMEM_LIMIT = 56 * 1024 * 1024
TOKEN_TILE = 256
ML_CHUNK = 128
NEG_INF = float("-inf")


def _cparams(*sem):
    return pltpu.CompilerParams(dimension_semantics=sem, vmem_limit_bytes=VMEM_LIMIT)


def _dot(a, b):
    return jnp.dot(a, b, preferred_element_type=F32)


def _dot_t(a, b):
    return lax.dot_general(a, b, (((1,), (1,)), ((), ())), preferred_element_type=F32)


def _split_dot(a_f32, b_bf16):
    hi = a_f32.astype(BF16)
    lo = (a_f32 - hi.astype(F32)).astype(BF16)
    return _dot(hi, b_bf16) + _dot(lo, b_bf16)


def _ada_kernel(c_ref, w_ref, b_ref, o_ref):
    c = c_ref[...]
    ca = c * jax.nn.sigmoid(c)
    o_ref[0] = jnp.dot(ca, w_ref[0], precision=HIGHEST, preferred_element_type=F32) + b_ref[0]


def _ada_modulation(c, ada_w, ada_b):
    B, D = c.shape
    n_col = ada_w.shape[-1] // D
    return pl.pallas_call(
        _ada_kernel,
        out_shape=jax.ShapeDtypeStruct((DEPTH, B, n_col * D), F32),
        grid=(DEPTH, n_col),
        in_specs=[
            pl.BlockSpec((B, D), lambda l, j: (0, 0)),
            pl.BlockSpec((1, D, D), lambda l, j: (l, 0, j)),
            pl.BlockSpec((1, 1, D), lambda l, j: (l, 0, j)),
        ],
        out_specs=pl.BlockSpec((1, B, D), lambda l, j: (l, 0, j)),
        compiler_params=_cparams("parallel", "parallel"),
        name="ada_modulation",
    )(c, ada_w, ada_b.reshape(DEPTH, 1, n_col * D))


def _modulated_norm(x, gain, scale, shift):
    y = x * lax.rsqrt(jnp.mean(x * x, axis=-1, keepdims=True) + NORM_EPS)
    return y * gain * (1.0 + scale) + shift


def _rope(x, cos, sin_signed, first_half):
    w = x.shape[-1]
    partner = jnp.where(first_half, pltpu.roll(x, w - HEAD_DIM // 2, 1), pltpu.roll(x, HEAD_DIM // 2, 1))
    return x * cos + partner * sin_signed


HYB_COLS = 5 * 512 + 2 * LANES


def _hyb_in_kernel(x_ref, gain_ref, sc_ref, sh_ref, w_ref, cos_ref, sin_ref, qn_ref, kn_ref, grp_ref,
                   q_ref, iq_ref, bcu_ref, kv_ref, ik_ref, iw_ref):
    h = _modulated_norm(x_ref[0], gain_ref[...], sc_ref[0], sh_ref[0])
    p = _dot(h.astype(BF16), w_ref[...])
    cos = cos_ref[...]
    sin = sin_ref[...]
    lane = lax.broadcasted_iota(jnp.int32, (1, ATTN_WIDTH), 1)
    first_half = (lane % HEAD_DIM) < (HEAD_DIM // 2)
    fh128 = first_half[:, :LANES]
    lane128 = lane[:, :LANES]

    q = p[:, 0:512]
    ms = _split_dot(q * q, grp_ref[...]) * (1.0 / HEAD_DIM)
    q = q * lax.rsqrt(ms + NORM_EPS) * qn_ref[...]
    q_ref[0] = (_rope(q, cos, sin, first_half) * (HEAD_DIM ** -0.5)).astype(BF16)

    iq = p[:, 512:1024]
    iq_ref[0] = (_rope(iq, cos, sin, first_half) * (IDX_DIM ** -0.5)).astype(BF16)

    bcu_ref[0] = p[:, 1024:2560]

    kv = p[:, 2560:2688]
    is_k = lane128 < HEAD_DIM
    kk = jnp.where(is_k, kv, 0.0)
    ms_k = jnp.sum(kk * kk, axis=-1, keepdims=True) * (1.0 / HEAD_DIM)
    kn = kv * lax.rsqrt(ms_k + NORM_EPS) * kn_ref[...]
    kr = _rope(kn, cos[:, :LANES], sin[:, :LANES], fh128)
    kv_ref[0] = jnp.where(is_k, kr, kv).astype(BF16)

    sm = p[:, 2688:2816]
    ikr = _rope(sm, cos[:, :LANES], sin[:, :LANES], fh128)
    ik_ref[0] = jnp.where(is_k, ikr, 0.0).astype(BF16)
    iw_ref[0] = sm


def _hyb_in(x, gain, sc, sh, w_pad, cos_t, sin_t, qn_t, kn_t, grp):
    B, S, D = x.shape
    tm = TOKEN_TILE
    row = lambda b, j: (b, j, 0)
    per_b = lambda b, j: (b, 0, 0)
    const2 = lambda b, j: (0, 0)
    tab = lambda b, j: (j, 0)
    return pl.pallas_call(
        _hyb_in_kernel,
        out_shape=(
            jax.ShapeDtypeStruct((B, S, 512), BF16),
            jax.ShapeDtypeStruct((B, S, 512), BF16),
            jax.ShapeDtypeStruct((B, S, 1536), F32),
            jax.ShapeDtypeStruct((B, S, LANES), BF16),
            jax.ShapeDtypeStruct((B, S, LANES), BF16),
            jax.ShapeDtypeStruct((B, S, LANES), F32),
        ),
        grid=(B, S // tm),
        in_specs=[
            pl.BlockSpec((1, tm, D), row),
            pl.BlockSpec((1, D), const2),
            pl.BlockSpec((1, 1, D), per_b),
            pl.BlockSpec((1, 1, D), per_b),
            pl.BlockSpec((D, HYB_COLS), const2),
            pl.BlockSpec((tm, 512), tab),
            pl.BlockSpec((tm, 512), tab),
            pl.BlockSpec((1, 512), const2),
            pl.BlockSpec((1, LANES), const2),
            pl.BlockSpec((512, 512), const2),
        ],
        out_specs=(
            pl.BlockSpec((1, tm, 512), row),
            pl.BlockSpec((1, tm, 512), row),
            pl.BlockSpec((1, tm, 1536), row),
            pl.BlockSpec((1, tm, LANES), row),
            pl.BlockSpec((1, tm, LANES), row),
            pl.BlockSpec((1, tm, LANES), row),
        ),
        compiler_params=_cparams("parallel", "parallel"),
        name="hybrid_in_proj",
    )(x, gain, sc, sh, w_pad, cos_t, sin_t, qn_t, kn_t, grp)


def _count(mask):
    return jnp.sum(jnp.where(mask, 1.0, 0.0), axis=-1, keepdims=True)


def _dsa_kernel(q_ref, iq_ref, iw_ref, kv_ref, ik_ref, o_ref, sc_ref, bias_ref):
    S = kv_ref.shape[1]
    qb = pl.program_id(1)
    kf = float(INDEX_TOPK)

    ik = ik_ref[0][:, :IDX_DIM]
    iw = iw_ref[0]
    acc = jnp.zeros((Q_BLOCK, S), F32)
    for hd in range(IDX_HEADS):
        s_h = _dot_t(iq_ref[0][:, hd * IDX_DIM:(hd + 1) * IDX_DIM], ik)
        w_h = iw[:, IDX_DIM + hd:IDX_DIM + hd + 1] * (IDX_HEADS ** -0.5)
        acc = acc + jnp.maximum(s_h, 0.0) * w_h
    qpos = qb * Q_BLOCK + lax.broadcasted_iota(jnp.int32, (Q_BLOCK, 1), 0)
    kpos = lax.broadcasted_iota(jnp.int32, (1, S), 1)
    causal = kpos <= qpos
    sc_ref[...] = jnp.where(causal, acc, NEG_INF)

    @pl.when(qb * Q_BLOCK + Q_BLOCK <= INDEX_TOPK)
    def _():
        bias_ref[...] = jnp.where(causal, 0.0, NEG_INF)

    @pl.when(qb * Q_BLOCK + Q_BLOCK > INDEX_TOPK)
    def _():
        sc = sc_ref[...]
        row_max = jnp.max(sc, axis=-1, keepdims=True)
        row_min = jnp.min(jnp.where(causal, sc, jnp.inf), axis=-1, keepdims=True)
        top_tied = _count(sc >= row_max) >= kf

        def bisect(_, carry):
            lo, hi = carry
            mid = 0.5 * lo + 0.5 * hi
            ge = _count(sc_ref[...] >= mid) >= kf
            return jnp.where(ge, mid, lo), jnp.where(ge, hi, mid)

        lo, hi = lax.fori_loop(0, 18, bisect, (row_min, row_max))

        def refine_cond(carry):
            it, _, _, done = carry
            return jnp.logical_and(it < S, jnp.min(done) < 0.5)

        def refine(carry):
            it, hi, thr, done = carry
            scv = sc_ref[...]
            m = jnp.max(jnp.where(scv < hi, scv, NEG_INF), axis=-1, keepdims=True)
            hit = _count(scv >= m) >= kf
            fin = done > 0.5
            thr = jnp.where(fin, thr, m)
            hi = jnp.where(jnp.logical_or(fin, hit), hi, m)
            done = jnp.where(hit, 1.0, done)
            return it + 1, hi, thr, done

        done0 = jnp.where(top_tied, 1.0, 0.0)
        _, _, thr, _ = lax.while_loop(refine_cond, refine, (jnp.int32(0), hi, row_max, done0))

        gt = sc > thr
        eq = sc == thr
        need = kf - _count(gt)
        n_eq = _count(eq)
        bias_ref[...] = jnp.where(sc >= thr, 0.0, NEG_INF)

        @pl.when(jnp.max(n_eq - need) > 0.5)
        def _():
            r_i = lax.broadcasted_iota(jnp.int32, (LANES, LANES), 0)
            c_i = lax.broadcasted_iota(jnp.int32, (LANES, LANES), 1)
            upper = jnp.where(r_i <= c_i, 1.0, 0.0).astype(BF16)
            carry_cnt = jnp.zeros((Q_BLOCK, 1), F32)
            for j in range(S // LANES):
                blk = sc_ref[:, j * LANES:(j + 1) * LANES]
                eq_j = blk == thr
                eq_f = jnp.where(eq_j, 1.0, 0.0)
                rank = _dot(eq_f.astype(BF16), upper) + carry_cnt
                keep = jnp.logical_or(blk > thr, jnp.logical_and(eq_j, rank <= need))
                bias_ref[:, j * LANES:(j + 1) * LANES] = jnp.where(keep, 0.0, NEG_INF)
                carry_cnt = carry_cnt + jnp.sum(eq_f, axis=-1, keepdims=True)

    k = kv_ref[0][:, :HEAD_DIM]
    v = kv_ref[0][:, HEAD_DIM:]
    bias = bias_ref[...]
    for hd in range(ATTN_HEADS):
        logits = _dot_t(q_ref[0][:, hd * HEAD_DIM:(hd + 1) * HEAD_DIM], k) + bias
        mx = jnp.max(logits, axis=-1, keepdims=True)
        p = jnp.exp(logits - mx)
        den = jnp.sum(p, axis=-1, keepdims=True)
        o_ref[0, :, hd * HEAD_DIM:(hd + 1) * HEAD_DIM] = _dot(p.astype(BF16), v) / den


def _dsa_attention(q, iq, iw, kv, ik):
    B, S, _ = q.shape
    blk = lambda b, j: (b, j, 0)
    per_b = lambda b, j: (b, 0, 0)
    return pl.pallas_call(
        _dsa_kernel,
        out_shape=jax.ShapeDtypeStruct((B, S, ATTN_WIDTH), F32),
        grid=(B, S // Q_BLOCK),
        in_specs=[
            pl.BlockSpec((1, Q_BLOCK, ATTN_WIDTH), blk),
            pl.BlockSpec((1, Q_BLOCK, IDX_HEADS * IDX_DIM), blk),
            pl.BlockSpec((1, Q_BLOCK, LANES), blk),
            pl.BlockSpec((1, S, LANES), per_b),
            pl.BlockSpec((1, S, LANES), per_b),
        ],
        out_specs=pl.BlockSpec((1, Q_BLOCK, ATTN_WIDTH), blk),
        scratch_shapes=[pltpu.VMEM((Q_BLOCK, S), F32), pltpu.VMEM((Q_BLOCK, S), F32)],
        compiler_params=_cparams("parallel", "parallel"),
        name="dsa_attention",
    )(q, iq, iw, kv, ik)


def _hyb_out_kernel(ya_ref, bcu_ref, halo_ref, cw_ref, w_ref, x_ref, g_ref, o_ref):
    j = pl.program_id(1)
    tm = ya_ref.shape[1]
    bcu = bcu_ref[0]
    bg = bcu[:, 0:512]
    z = bcu[:, 512:1024] * bcu[:, 1024:1536]
    halo = halo_ref[0]
    zh = halo[:, 512:1024] * halo[:, 1024:1536]
    zh = jnp.where(j > 0, zh, 0.0)
    row = lax.broadcasted_iota(jnp.int32, (tm, 1), 0)
    z1 = jnp.where(row >= 1, pltpu.roll(z, 1, 0), zh[7:8, :])
    z2 = jnp.where(row >= 2, pltpu.roll(z, 2, 0), jnp.where(row == 1, zh[7:8, :], zh[6:7, :]))
    cw = cw_ref[...]
    y_conv = bg * (z2 * cw[0:1, :] + z1 * cw[1:2, :] + z * cw[2:3, :])
    y = _dot(ya_ref[0].astype(BF16), w_ref[0:512, :]) + _dot(y_conv.astype(BF16), w_ref[512:1024, :])
    o_ref[0] = x_ref[0] + g_ref[0] * y


def _hyb_out(y_attn, bcu, conv_w, w_out_bf, x, g1):
    B, S, D = x.shape
    tm = TOKEN_TILE
    row = lambda b, j: (b, j, 0)
    per_b = lambda b, j: (b, 0, 0)
    const2 = lambda b, j: (0, 0)
    halo = lambda b, j: (b, jnp.maximum(j * (tm // 8) - 1, 0), 0)
    return pl.pallas_call(
        _hyb_out_kernel,
        out_shape=jax.ShapeDtypeStruct((B, S, D), F32),
        grid=(B, S // tm),
        in_specs=[
            pl.BlockSpec((1, tm, 512), row),
            pl.BlockSpec((1, tm, 1536), row),
            pl.BlockSpec((1, 8, 1536), halo),
            pl.BlockSpec((CONV_K, CONV_WIDTH), const2),
            pl.BlockSpec((D, D), const2),
            pl.BlockSpec((1, tm, D), row),
            pl.BlockSpec((1, 1, D), per_b),
        ],
        out_specs=pl.BlockSpec((1, tm, D), row),
        compiler_params=_cparams("parallel", "parallel"),
        name="hybrid_out_proj",
    )(y_attn, bcu, bcu, conv_w, w_out_bf, x, g1)


ML_COLS = 512 + 512 + 1024 + 1024 + LANES


def _ml_in_kernel(x_ref, gain_ref, sc_ref, sh_ref, w_ref, bias_ref, q_ref, k_ref, v_ref, og_ref, gt_ref):
    h = _modulated_norm(x_ref[0], gain_ref[...], sc_ref[0], sh_ref[0])
    p = _dot(h.astype(BF16), w_ref[...])
    q_ref[0] = (p[:, 0:512] * (ML_QK_DIM ** -0.5)).astype(BF16)
    k_ref[0] = p[:, 512:1024].astype(BF16)
    v_ref[0] = p[:, 1024:2048].astype(BF16)
    og_ref[0] = p[:, 2048:3072]
    gt_ref[0] = p[:, 3072:3200] + bias_ref[...]


def _ml_in(x, gain, sc, sh, w_pad, gate_bias):
    B, S, D = x.shape
    tm = TOKEN_TILE
    row = lambda b, j: (b, j, 0)
    per_b = lambda b, j: (b, 0, 0)
    const2 = lambda b, j: (0, 0)
    return pl.pallas_call(
        _ml_in_kernel,
        out_shape=(
            jax.ShapeDtypeStruct((B, S, 512), BF16),
            jax.ShapeDtypeStruct((B, S, 512), BF16),
            jax.ShapeDtypeStruct((B, S, 1024), BF16),
            jax.ShapeDtypeStruct((B, S, 1024), F32),
            jax.ShapeDtypeStruct((B, S, LANES), F32),
        ),
        grid=(B, S // tm),
        in_specs=[
            pl.BlockSpec((1, tm, D), row),
            pl.BlockSpec((1, D), const2),
            pl.BlockSpec((1, 1, D), per_b),
            pl.BlockSpec((1, 1, D), per_b),
            pl.BlockSpec((D, ML_COLS), const2),
            pl.BlockSpec((1, LANES), const2),
        ],
        out_specs=(
            pl.BlockSpec((1, tm, 512), row),
            pl.BlockSpec((1, tm, 512), row),
            pl.BlockSpec((1, tm, 1024), row),
            pl.BlockSpec((1, tm, 1024), row),
            pl.BlockSpec((1, tm, LANES), row),
        ),
        compiler_params=_cparams("parallel", "parallel"),
        name="mlstm_in_proj",
    )(x, gain, sc, sh, w_pad, gate_bias)


def _log_sigmoid(f):
    return jnp.minimum(f, 0.0) - jnp.log1p(jnp.exp(-jnp.abs(f)))


def _mlstm_kernel(q_ref, k_ref, v_ref, grow_ref, gcol_ref, gain_ref, o_ref, c_ref, n_ref, m_ref):
    L = ML_CHUNK
    S = q_ref.shape[1]
    c_ref[...] = jnp.zeros_like(c_ref)
    n_ref[...] = jnp.zeros_like(n_ref)
    m_ref[...] = jnp.zeros_like(m_ref)
    r_i = lax.broadcasted_iota(jnp.int32, (L, L), 0)
    c_i = lax.broadcasted_iota(jnp.int32, (L, L), 1)
    lower = jnp.where(c_i <= r_i, 1.0, 0.0)
    upper = jnp.where(r_i <= c_i, 1.0, 0.0)
    tril = c_i <= r_i

    def chunk(c, carry):
        r0 = pl.multiple_of(c * L, L)
        rows = grow_ref[0, 0, c]
        cols = gcol_ref[0, 0, c]
        b_rows = jnp.dot(_log_sigmoid(rows), upper, precision=HIGHEST, preferred_element_type=F32)
        b_cols = jnp.dot(lower, _log_sigmoid(cols), precision=HIGHEST, preferred_element_type=F32)
        for hh in range(2):
            q = q_ref[0, pl.ds(r0, L), hh * ML_QK_DIM:(hh + 1) * ML_QK_DIM]
            k = k_ref[0, pl.ds(r0, L), hh * ML_QK_DIM:(hh + 1) * ML_QK_DIM]
            v = v_ref[0, pl.ds(r0, L), hh * ML_V_DIM:(hh + 1) * ML_V_DIM]
            i_row = rows[hh:hh + 1, :]
            i_col = cols[:, hh:hh + 1]
            b_row = b_rows[2 + hh:3 + hh, :]
            b_col = b_cols[:, 2 + hh:3 + hh]
            b_last = b_row[:, L - 1:L]
            m_prev = m_ref[hh]
            ct = c_ref[hh]
            n_row = n_ref[hh]

            dmat = jnp.where(tril, b_col - b_row + i_row, NEG_INF)
            inter = b_col + m_prev
            m_t = jnp.maximum(inter, jnp.max(dmat, axis=-1, keepdims=True))
            w_intra = jnp.exp(dmat - m_t)
            w_inter = jnp.exp(inter - m_t)
            intra = w_intra * _dot_t(q, k)
            num = w_inter * _dot(q, ct.astype(BF16)) + _dot(intra.astype(BF16), v)
            qn = jnp.sum(q.astype(F32) * n_row, axis=-1, keepdims=True)
            den = w_inter * qn + jnp.sum(intra, axis=-1, keepdims=True)
            hc = num / jnp.maximum(jnp.abs(den), jnp.exp(-m_t))
            y = hc * lax.rsqrt(jnp.mean(hc * hc, axis=-1, keepdims=True) + NORM_EPS)
            o_ref[0, pl.ds(r0, L), hh * ML_V_DIM:(hh + 1) * ML_V_DIM] = (
                y * gain_ref[:, hh * ML_V_DIM:(hh + 1) * ML_V_DIM])

            g_row = b_last - b_row + i_row
            g_col = b_last - b_col + i_col
            m_new = jnp.maximum(b_last + m_prev, jnp.max(g_row, axis=-1, keepdims=True))
            decay = jnp.exp(b_last + m_prev - m_new)
            kw = k.astype(F32) * jnp.exp(g_col - m_new)
            c_ref[hh] = decay * ct + lax.dot_general(
                kw.astype(BF16), v, (((0,), (0,)), ((), ())), preferred_element_type=F32)
            n_ref[hh] = decay * n_row + jnp.sum(kw, axis=0, keepdims=True)
            m_ref[hh] = m_new
        return carry

    lax.fori_loop(0, S // L, chunk, 0)


def _mlstm(q, k, v, g_rows, g_cols, out_gain):
    B, S, _ = q.shape
    nc = S // ML_CHUNK
    pairs = ML_HEADS // 2
    return pl.pallas_call(
        _mlstm_kernel,
        out_shape=jax.ShapeDtypeStruct((B, S, ML_HEADS * ML_V_DIM), F32),
        grid=(B, pairs),
        in_specs=[
            pl.BlockSpec((1, S, 2 * ML_QK_DIM), lambda b, p: (b, 0, p)),
            pl.BlockSpec((1, S, 2 * ML_QK_DIM), lambda b, p: (b, 0, p)),
            pl.BlockSpec((1, S, 2 * ML_V_DIM), lambda b, p: (b, 0, p)),
            pl.BlockSpec((1, 1, nc, 8, ML_CHUNK), lambda b, p: (b, p, 0, 0, 0)),
            pl.BlockSpec((1, 1, nc, ML_CHUNK, LANES), lambda b, p: (b, p, 0, 0, 0)),
            pl.BlockSpec((1, 2 * ML_V_DIM), lambda b, p: (0, p)),
        ],
        out_specs=pl.BlockSpec((1, S, 2 * ML_V_DIM), lambda b, p: (b, 0, p)),
        scratch_shapes=[
            pltpu.VMEM((2, ML_QK_DIM, ML_V_DIM), F32),
            pltpu.VMEM((2, 1, ML_QK_DIM), F32),
            pltpu.VMEM((2, 1, 1), F32),
        ],
        compiler_params=_cparams("parallel", "parallel"),
        name="mlstm_chunkwise",
    )(q, k, v, g_rows, g_cols, out_gain)


def _ml_out_kernel(hh_ref, og_ref, w_ref, x_ref, g_ref, o_ref):
    a = jax.nn.sigmoid(og_ref[0]) * hh_ref[0]
    o_ref[0] = x_ref[0] + g_ref[0] * _dot(a.astype(BF16), w_ref[...])


def _ml_out(hh, og, w_out_bf, x, g1):
    B, S, D = x.shape
    tm = TOKEN_TILE
    row = lambda b, j: (b, j, 0)
    per_b = lambda b, j: (b, 0, 0)
    const2 = lambda b, j: (0, 0)
    return pl.pallas_call(
        _ml_out_kernel,
        out_shape=jax.ShapeDtypeStruct((B, S, D), F32),
        grid=(B, S // tm),
        in_specs=[
            pl.BlockSpec((1, tm, D), row),
            pl.BlockSpec((1, tm, D), row),
            pl.BlockSpec((D, D), const2),
            pl.BlockSpec((1, tm, D), row),
            pl.BlockSpec((1, 1, D), per_b),
        ],
        out_specs=pl.BlockSpec((1, tm, D), row),
        compiler_params=_cparams("parallel", "parallel"),
        name="mlstm_out_proj",
    )(hh, og, w_out_bf, x, g1)


def _first_argmax(x, lane, width):
    mx = jnp.max(x, axis=-1, keepdims=True)
    idx = jnp.min(jnp.where(x == mx, lane, width), axis=-1, keepdims=True)
    return mx, idx


def _router_kernel(x_ref, gain_ref, sc_ref, sh_ref, whi_ref, wlo_ref, b_ref, h_ref, r_ref, cnt_ref, run_ref):
    tm = x_ref.shape[1]

    @pl.when(jnp.logical_and(pl.program_id(0) == 0, pl.program_id(1) == 0))
    def _():
        run_ref[...] = jnp.zeros_like(run_ref)

    h = _modulated_norm(x_ref[0], gain_ref[...], sc_ref[0], sh_ref[0])
    h_hi = h.astype(BF16)
    h_ref[0] = h_hi
    h_lo = (h - h_hi.astype(F32)).astype(BF16)
    whi = whi_ref[...]
    logits = _dot(h_hi, whi) + _dot(h_lo, whi) + _dot(h_hi, wlo_ref[...]) + b_ref[...]
    lane = lax.broadcasted_iota(jnp.int32, (1, LANES), 1)
    lg = jnp.where(lane < N_GROUPS, logits, NEG_INF)
    g_max, g_sel = _first_argmax(lg, lane, LANES)
    pg = 1.0 / jnp.sum(jnp.exp(lg - g_max), axis=-1, keepdims=True)
    e_lane = lane - N_GROUPS
    in_grp = jnp.logical_and(e_lane >= g_sel * EXPERTS_PER_GROUP, e_lane < (g_sel + 1) * EXPERTS_PER_GROUP)
    le = jnp.where(in_grp, logits, NEG_INF)
    v1, i1 = _first_argmax(le, lane, LANES)
    le2 = jnp.where(lane == i1, NEG_INF, le)
    v2, i2 = _first_argmax(le2, lane, LANES)
    e2 = jnp.exp(v2 - v1)
    w1 = pg / (1.0 + e2)
    w2 = pg * e2 / (1.0 + e2)
    e1 = i1 - N_GROUPS
    e2 = i2 - N_GROUPS
    hot1 = lane == e1
    hot2 = lane == e2
    onehot = jnp.where(jnp.logical_or(hot1, hot2), 1.0, 0.0)
    r_i = lax.broadcasted_iota(jnp.int32, (tm, tm), 0)
    c_i = lax.broadcasted_iota(jnp.int32, (tm, tm), 1)
    before = jnp.where(c_i < r_i, 1.0, 0.0).astype(BF16)
    seen = _dot(before, onehot.astype(BF16)) + run_ref[...]
    rank1 = jnp.sum(jnp.where(hot1, seen, 0.0), axis=-1, keepdims=True)
    rank2 = jnp.sum(jnp.where(hot2, seen, 0.0), axis=-1, keepdims=True)
    run_ref[...] = run_ref[...] + jnp.sum(onehot, axis=0, keepdims=True)
    cnt_ref[...] = run_ref[...]

    out = jnp.where(lane == 0, e1.astype(F32), 0.0)
    out = jnp.where(lane == 1, e2.astype(F32), out)
    out = jnp.where(lane == 2, w1, out)
    out = jnp.where(lane == 3, w2, out)
    out = jnp.where(lane == 4, rank1, out)
    out = jnp.where(lane == 5, rank2, out)
    r_ref[0] = out


def _router(x, gain, sc, sh, w_hi, w_lo, bias):
    B, S, D = x.shape
    tm = TOKEN_TILE
    row = lambda b, j: (b, j, 0)
    per_b = lambda b, j: (b, 0, 0)
    const2 = lambda b, j: (0, 0)
    return pl.pallas_call(
        _router_kernel,
        out_shape=(
            jax.ShapeDtypeStruct((B, S, D), BF16),
            jax.ShapeDtypeStruct((B, S, LANES), F32),
            jax.ShapeDtypeStruct((1, LANES), F32),
        ),
        grid=(B, S // tm),
        in_specs=[
            pl.BlockSpec((1, tm, D), row),
            pl.BlockSpec((1, D), const2),
            pl.BlockSpec((1, 1, D), per_b),
            pl.BlockSpec((1, 1, D), per_b),
            pl.BlockSpec((D, LANES), const2),
            pl.BlockSpec((D, LANES), const2),
            pl.BlockSpec((1, LANES), const2),
        ],
        out_specs=(
            pl.BlockSpec((1, tm, D), row),
            pl.BlockSpec((1, tm, LANES), row),
            pl.BlockSpec((1, LANES), const2),
        ),
        scratch_shapes=[pltpu.VMEM((1, LANES), F32)],
        compiler_params=_cparams("arbitrary", "arbitrary"),
        name="moe_router",
    )(x, gain, sc, sh, w_hi, w_lo, bias)


def _experts_kernel(blk_e_ref, n_used_ref, x_ref, wg_ref, wu_ref, wd_ref, o_ref):
    i = pl.program_id(0)

    @pl.when(i < n_used_ref[0])
    def _():
        x = x_ref[...]
        a = _dot(x, wg_ref[0].astype(BF16))
        u = _dot(x, wu_ref[0].astype(BF16))
        act = a * jax.nn.sigmoid(a) * u
        o_ref[...] = _dot(act.astype(BF16), wd_ref[0].astype(BF16))

    @pl.when(i >= n_used_ref[0])
    def _():
        o_ref[...] = jnp.zeros_like(o_ref)


def _experts(blk_e, n_used, xs, w_gate, w_up, w_down):
    R, D = xs.shape
    n_blk = R // MOE_BLOCK
    grid_spec = pltpu.PrefetchScalarGridSpec(
        num_scalar_prefetch=2,
        grid=(n_blk,),
        in_specs=[
            pl.BlockSpec((MOE_BLOCK, D), lambda i, be, nu: (i, 0)),
            pl.BlockSpec((1, D, D_EXPERT), lambda i, be, nu: (be[i], 0, 0)),
            pl.BlockSpec((1, D, D_EXPERT), lambda i, be, nu: (be[i], 0, 0)),
            pl.BlockSpec((1, D_EXPERT, D), lambda i, be, nu: (be[i], 0, 0)),
        ],
        out_specs=pl.BlockSpec((MOE_BLOCK, D), lambda i, be, nu: (i, 0)),
    )
    return pl.pallas_call(
        _experts_kernel,
        out_shape=jax.ShapeDtypeStruct((R, D), F32),
        grid_spec=grid_spec,
        compiler_params=_cparams("arbitrary"),
        name="moe_experts",
    )(blk_e, n_used, xs, w_gate, w_up, w_down)


def _combine_kernel(x_ref, g_ref, y0_ref, y1_ref, r_ref, o_ref):
    r = r_ref[0]
    y = y0_ref[0] * r[:, 2:3] + y1_ref[0] * r[:, 3:4]
    o_ref[0] = x_ref[0] + g_ref[0] * y


def _combine(x, g2, y0, y1, route):
    B, S, D = x.shape
    tm = TOKEN_TILE
    row = lambda b, j: (b, j, 0)
    per_b = lambda b, j: (b, 0, 0)
    return pl.pallas_call(
        _combine_kernel,
        out_shape=jax.ShapeDtypeStruct((B, S, D), F32),
        grid=(B, S // tm),
        in_specs=[
            pl.BlockSpec((1, tm, D), row),
            pl.BlockSpec((1, 1, D), per_b),
            pl.BlockSpec((1, tm, D), row),
            pl.BlockSpec((1, tm, D), row),
            pl.BlockSpec((1, tm, LANES), row),
        ],
        out_specs=pl.BlockSpec((1, tm, D), row),
        compiler_params=_cparams("parallel", "parallel"),
        name="moe_combine",
    )(x, g2, y0, y1, route)


def _moe_dispatch(route, counts, T):
    A = T * TOP_K
    counts = counts[0, :N_EXPERTS].astype(jnp.int32)
    blocks_per = (counts + MOE_BLOCK - 1) // MOE_BLOCK
    block_end = jnp.cumsum(blocks_per)
    block_start = block_end - blocks_per
    expert = route[:, :TOP_K].astype(jnp.int32)
    rank = route[:, 4:4 + TOP_K].astype(jnp.int32)
    onehot = expert[:, :, None] == jnp.arange(N_EXPERTS, dtype=jnp.int32)
    start = jnp.sum(jnp.where(onehot, block_start, 0), axis=-1)
    dest = start * MOE_BLOCK + rank
    n_blk = -(-A // MOE_BLOCK) + N_EXPERTS
    R = n_blk * MOE_BLOCK
    tok = jnp.broadcast_to(jnp.arange(T, dtype=jnp.int32)[:, None], (T, TOP_K))
    row_tok = jnp.zeros((R,), jnp.int32).at[dest.reshape(A)].set(
        tok.reshape(A), mode="promise_in_bounds", unique_indices=True)
    blk = jnp.arange(n_blk, dtype=jnp.int32)
    blk_e = jnp.minimum(jnp.sum(blk[:, None] >= block_end[None, :], axis=-1), N_EXPERTS - 1)
    return row_tok, dest, blk_e.astype(jnp.int32), block_end[-1:].astype(jnp.int32)


def _rope_tables(S):
    inv = 1.0 / (ROPE_THETA ** (jnp.arange(0, HEAD_DIM, 2, dtype=F32) / HEAD_DIM))
    ang = jnp.arange(S, dtype=F32)[:, None] * inv[None, :]
    cos, sin = jnp.cos(ang), jnp.sin(ang)
    cos_h = jnp.concatenate([cos, cos], axis=-1)
    sin_h = jnp.concatenate([-sin, sin], axis=-1)
    return jnp.tile(cos_h, (1, ATTN_HEADS)), jnp.tile(sin_h, (1, ATTN_HEADS))


def _pad_cols(w, width):
    return jnp.pad(w, ((0, 0), (0, width - w.shape[1])))


def kernel(x, c, ada_w, ada_b, norm_mix, norm_ffn, hy_w_in, hy_q_norm, hy_k_norm, hy_conv_w, hy_w_out, ml_w_in, ml_b_gates, ml_out_norm, ml_w_out, moe_w_group, moe_b_group, moe_w_expert, moe_b_expert, moe_w_gate, moe_w_up, moe_w_down):
    B, S, D = x.shape
    T = B * S
    cos_t, sin_t = _rope_tables(S)
    mod = _ada_modulation(c, ada_w, ada_b).reshape(DEPTH, B, 6, 1, D)
    r_i = np.arange(ATTN_WIDTH)
    grp = jnp.asarray((r_i[:, None] // HEAD_DIM) == (r_i[None, :] // HEAD_DIM), dtype=BF16)

    for l in range(DEPTH):
        sh1, sc1, g1, sh2, sc2, g2 = [mod[l, :, i] for i in range(6)]
        gain1 = norm_mix[l].reshape(1, D)
        j = l // 2
        if l % 2 == 0:
            w = hy_w_in[j]
            o = np.cumsum((0,) + (ATTN_WIDTH, HEAD_DIM, HEAD_DIM, IDX_HEADS * IDX_DIM, IDX_DIM, IDX_HEADS,
                                  CONV_WIDTH, CONV_WIDTH, CONV_WIDTH))
            wq, wk, wv, wiq, wik, wiw, wbg, wcg, wu = [w[:, o[i]:o[i + 1]] for i in range(9)]
            w_pad = jnp.concatenate(
                [wq, wiq, wbg, wcg, wu, wk, wv, _pad_cols(jnp.concatenate([wik, wiw], axis=1), LANES)],
                axis=1).astype(BF16)
            qn_t = jnp.tile(hy_q_norm[j], ATTN_HEADS).reshape(1, ATTN_WIDTH)
            kn_t = jnp.tile(hy_k_norm[j], LANES // HEAD_DIM).reshape(1, LANES)
            q, iq, bcu, kv, ik, iw = _hyb_in(x, gain1, sc1, sh1, w_pad, cos_t, sin_t, qn_t, kn_t, grp)
            y_attn = _dsa_attention(q, iq, iw, kv, ik)
            x = _hyb_out(y_attn, bcu, hy_conv_w[j], hy_w_out[j].astype(BF16), x, g1)
        else:
            w = ml_w_in[j]
            hq = ML_HEADS * ML_QK_DIM
            hv = ML_HEADS * ML_V_DIM
            wq, wk, wv = w[:, :hq], w[:, hq:2 * hq], w[:, 2 * hq:2 * hq + hv]
            wg = w[:, 2 * hq + hv:2 * hq + hv + 2 * ML_HEADS]
            wo = w[:, 2 * hq + hv + 2 * ML_HEADS:]
            w_pad = jnp.concatenate([wq, wk, wv, wo, _pad_cols(wg, LANES)], axis=1).astype(BF16)
            gate_bias = jnp.pad(ml_b_gates[j], (0, LANES - 2 * ML_HEADS)).reshape(1, LANES)
            q, k, v, og, gates = _ml_in(x, gain1, sc1, sh1, w_pad, gate_bias)
            nc = S // ML_CHUNK
            pairs = ML_HEADS // 2
            gi = gates[:, :, :ML_HEADS].reshape(B, S, pairs, 2)
            gf = gates[:, :, ML_HEADS:2 * ML_HEADS].reshape(B, S, pairs, 2)
            gp = jnp.concatenate([gi, gf], axis=-1)
            gp = jnp.transpose(gp, (0, 2, 1, 3)).reshape(B, pairs, nc, ML_CHUNK, 4)
            g_cols = jnp.pad(gp, ((0, 0),) * 4 + ((0, LANES - 4),))
            g_rows = jnp.pad(jnp.swapaxes(gp, -1, -2), ((0, 0),) * 3 + ((0, 4), (0, 0)))
            hh = _mlstm(q, k, v, g_rows, g_cols, ml_out_norm[j].reshape(1, hv))
            x = _ml_out(hh, og, ml_w_out[j].astype(BF16), x, g1)

        w_r = _pad_cols(jnp.concatenate([moe_w_group[l], moe_w_expert[l]], axis=1), LANES)
        w_hi = w_r.astype(BF16)
        w_lo = (w_r - w_hi.astype(F32)).astype(BF16)
        b_r = jnp.pad(jnp.concatenate([moe_b_group[l], moe_b_expert[l]]), (0, LANES - N_GROUPS - N_EXPERTS))
        h2, route, counts = _router(x, norm_ffn[l].reshape(1, D), sc2, sh2, w_hi, w_lo, b_r.reshape(1, LANES))
        row_tok, dest, blk_e, n_used = _moe_dispatch(route.reshape(T, LANES), counts, T)
        xs = h2.reshape(T, D).at[row_tok].get(mode="promise_in_bounds")
        ys = _experts(blk_e, n_used, xs, moe_w_gate[l], moe_w_up[l], moe_w_down[l])
        y0 = ys.at[dest[:, 0]].get(mode="promise_in_bounds").reshape(B, S, D)
        y1 = ys.at[dest[:, 1]].get(mode="promise_in_bounds").reshape(B, S, D)
        x = _combine(x, g2, y0, y1, route)
    return x
```

```python
import functools

import numpy as np
import jax
import jax.numpy as jnp
from jax import lax
from jax.experimental import pallas as pl
from jax.experimental.pallas import tpu as pltpu

F32 = jnp.float32
BF16 = jnp.bfloat16
HIGHEST = lax.Precision.HIGHEST

D_MODEL = 1024
DEPTH = 4
ATTN_HEADS = 8
HEAD_DIM = 64
ATTN_WIDTH = ATTN_HEADS * HEAD_DIM
IDX_HEADS = 8
IDX_DIM = 64
INDEX_TOPK = 256
Q_BLOCK = 128
ROPE_THETA = 10000.0
CONV_WIDTH = D_MODEL - ATTN_WIDTH
CONV_K = 3
ML_HEADS = 8
ML_QK_DIM = 64
ML_V_DIM = 128
N_GROUPS = 4
EXPERTS_PER_GROUP = 8
N_EXPERTS = N_GROUPS * EXPERTS_PER_GROUP
TOP_K = 2
D_EXPERT = 512
MOE_BLOCK = 256
NORM_EPS = 1e-6

LANES = 128
VMEM_LIMIT = 56 * 1024 * 1024
TOKEN_TILE = 256
ML_CHUNK = 128
NEG_INF = float("-inf")


def _cparams(*sem):
    return pltpu.CompilerParams(dimension_semantics=sem, vmem_limit_bytes=VMEM_LIMIT)


def _dot(a, b):
    return jnp.dot(a, b, preferred_element_type=F32)


def _dot_t(a, b):
    return lax.dot_general(a, b, (((1,), (1,)), ((), ())), preferred_element_type=F32)


def _split_dot(a_f32, b_bf16):
    hi = a_f32.astype(BF16)
    lo = (a_f32 - hi.astype(F32)).astype(BF16)
    return _dot(hi, b_bf16) + _dot(lo, b_bf16)


def _ada_kernel(c_ref, w_ref, b_ref, o_ref):
    c = c_ref[...]
    ca = c * jax.nn.sigmoid(c)
    o_ref[0] = jnp.dot(ca, w_ref[0], precision=HIGHEST, preferred_element_type=F32) + b_ref[0]


def _ada_modulation(c, ada_w, ada_b):
    B, D = c.shape
    n_col = ada_w.shape[-1] // D
    return pl.pallas_call(
        _ada_kernel,
        out_shape=jax.ShapeDtypeStruct((DEPTH, B, n_col * D), F32),
        grid=(DEPTH, n_col),
        in_specs=[
            pl.BlockSpec((B, D), lambda l, j: (0, 0)),
            pl.BlockSpec((1, D, D), lambda l, j: (l, 0, j)),
            pl.BlockSpec((1, 1, D), lambda l, j: (l, 0, j)),
        ],
        out_specs=pl.BlockSpec((1, B, D), lambda l, j: (l, 0, j)),
        compiler_params=_cparams("parallel", "parallel"),
        name="ada_modulation",
    )(c, ada_w, ada_b.reshape(DEPTH, 1, n_col * D))


def _modulated_norm(x, gain, scale, shift):
    y = x * lax.rsqrt(jnp.mean(x * x, axis=-1, keepdims=True) + NORM_EPS)
    return y * gain * (1.0 + scale) + shift


def _rope(x, cos, sin_signed, first_half):
    w = x.shape[-1]
    partner = jnp.where(first_half, pltpu.roll(x, w - HEAD_DIM // 2, 1), pltpu.roll(x, HEAD_DIM // 2, 1))
    return x * cos + partner * sin_signed


HYB_COLS = 5 * 512 + 2 * LANES


def _hyb_in_kernel(x_ref, gain_ref, sc_ref, sh_ref, w_ref, cos_ref, sin_ref, qn_ref, kn_ref, grp_ref,
                   q_ref, iq_ref, bcu_ref, kv_ref, ik_ref, iw_ref):
    h = _modulated_norm(x_ref[0], gain_ref[...], sc_ref[0], sh_ref[0])
    p = _dot(h.astype(BF16), w_ref[...])
    cos = cos_ref[...]
    sin = sin_ref[...]
    lane = lax.broadcasted_iota(jnp.int32, (1, ATTN_WIDTH), 1)
    first_half = (lane % HEAD_DIM) < (HEAD_DIM // 2)
    fh128 = first_half[:, :LANES]
    lane128 = lane[:, :LANES]

    q = p[:, 0:512]
    ms = _split_dot(q * q, grp_ref[...]) * (1.0 / HEAD_DIM)
    q = q * lax.rsqrt(ms + NORM_EPS) * qn_ref[...]
    q_ref[0] = (_rope(q, cos, sin, first_half) * (HEAD_DIM ** -0.5)).astype(BF16)

    iq = p[:, 512:1024]
    iq_ref[0] = (_rope(iq, cos, sin, first_half) * (IDX_DIM ** -0.5)).astype(BF16)

    bcu_ref[0] = p[:, 1024:2560]

    kv = p[:, 2560:2688]
    is_k = lane128 < HEAD_DIM
    kk = jnp.where(is_k, kv, 0.0)
    ms_k = jnp.sum(kk * kk, axis=-1, keepdims=True) * (1.0 / HEAD_DIM)
    kn = kv * lax.rsqrt(ms_k + NORM_EPS) * kn_ref[...]
    kr = _rope(kn, cos[:, :LANES], sin[:, :LANES], fh128)
    kv_ref[0] = jnp.where(is_k, kr, kv).astype(BF16)

    sm = p[:, 2688:2816]
    ikr = _rope(sm, cos[:, :LANES], sin[:, :LANES], fh128)
    ik_ref[0] = jnp.where(is_k, ikr, 0.0).astype(BF16)
    iw_ref[0] = sm


def _hyb_in(x, gain, sc, sh, w_pad, cos_t, sin_t, qn_t, kn_t, grp):
    B, S, D = x.shape
    tm = TOKEN_TILE
    row = lambda b, j: (b, j, 0)
    per_b = lambda b, j: (b, 0, 0)
    const2 = lambda b, j: (0, 0)
    tab = lambda b, j: (j, 0)
    return pl.pallas_call(
        _hyb_in_kernel,
        out_shape=(
            jax.ShapeDtypeStruct((B, S, 512), BF16),
            jax.ShapeDtypeStruct((B, S, 512), BF16),
            jax.ShapeDtypeStruct((B, S, 1536), F32),
            jax.ShapeDtypeStruct((B, S, LANES), BF16),
            jax.ShapeDtypeStruct((B, S, LANES), BF16),
            jax.ShapeDtypeStruct((B, S, LANES), F32),
        ),
        grid=(B, S // tm),
        in_specs=[
            pl.BlockSpec((1, tm, D), row),
            pl.BlockSpec((1, D), const2),
            pl.BlockSpec((1, 1, D), per_b),
            pl.BlockSpec((1, 1, D), per_b),
            pl.BlockSpec((D, HYB_COLS), const2),
            pl.BlockSpec((tm, 512), tab),
            pl.BlockSpec((tm, 512), tab),
            pl.BlockSpec((1, 512), const2),
            pl.BlockSpec((1, LANES), const2),
            pl.BlockSpec((512, 512), const2),
        ],
        out_specs=(
            pl.BlockSpec((1, tm, 512), row),
            pl.BlockSpec((1, tm, 512), row),
            pl.BlockSpec((1, tm, 1536), row),
            pl.BlockSpec((1, tm, LANES), row),
            pl.BlockSpec((1, tm, LANES), row),
            pl.BlockSpec((1, tm, LANES), row),
        ),
        compiler_params=_cparams("parallel", "parallel"),
        name="hybrid_in_proj",
    )(x, gain, sc, sh, w_pad, cos_t, sin_t, qn_t, kn_t, grp)


def _count(mask):
    return jnp.sum(jnp.where(mask, 1.0, 0.0), axis=-1, keepdims=True)


def _dsa_kernel(q_ref, iq_ref, iw_ref, kv_ref, ik_ref, o_ref, sc_ref, bias_ref):
    S = kv_ref.shape[1]
    qb = pl.program_id(1)
    kf = float(INDEX_TOPK)

    ik = ik_ref[0][:, :IDX_DIM]
    iw = iw_ref[0]
    acc = jnp.zeros((Q_BLOCK, S), F32)
    for hd in range(IDX_HEADS):
        s_h = _dot_t(iq_ref[0][:, hd * IDX_DIM:(hd + 1) * IDX_DIM], ik)
        w_h = iw[:, IDX_DIM + hd:IDX_DIM + hd + 1] * (IDX_HEADS ** -0.5)
        acc = acc + jnp.maximum(s_h, 0.0) * w_h
    qpos = qb * Q_BLOCK + lax.broadcasted_iota(jnp.int32, (Q_BLOCK, 1), 0)
    kpos = lax.broadcasted_iota(jnp.int32, (1, S), 1)
    causal = kpos <= qpos
    sc_ref[...] = jnp.where(causal, acc, NEG_INF)

    @pl.when(qb * Q_BLOCK + Q_BLOCK <= INDEX_TOPK)
    def _():
        bias_ref[...] = jnp.where(causal, 0.0, NEG_INF)

    @pl.when(qb * Q_BLOCK + Q_BLOCK > INDEX_TOPK)
    def _():
        sc = sc_ref[...]
        row_max = jnp.max(sc, axis=-1, keepdims=True)
        row_min = jnp.min(jnp.where(causal, sc, jnp.inf), axis=-1, keepdims=True)
        top_tied = _count(sc >= row_max) >= kf

        def bisect(_, carry):
            lo, hi = carry
            mid = 0.5 * lo + 0.5 * hi
            ge = _count(sc_ref[...] >= mid) >= kf
            return jnp.where(ge, mid, lo), jnp.where(ge, hi, mid)

        lo, hi = lax.fori_loop(0, 18, bisect, (row_min, row_max))

        def refine_cond(carry):
            it, _, _, done = carry
            return jnp.logical_and(it < S, jnp.min(done) < 0.5)

        def refine(carry):
            it, hi, thr, done = carry
            scv = sc_ref[...]
            m = jnp.max(jnp.where(scv < hi, scv, NEG_INF), axis=-1, keepdims=True)
            hit = _count(scv >= m) >= kf
            fin = done > 0.5
            thr = jnp.where(fin, thr, m)
            hi = jnp.where(jnp.logical_or(fin, hit), hi, m)
            done = jnp.where(hit, 1.0, done)
            return it + 1, hi, thr, done

        done0 = jnp.where(top_tied, 1.0, 0.0)
        _, _, thr, _ = lax.while_loop(refine_cond, refine, (jnp.int32(0), hi, row_max, done0))

        gt = sc > thr
        eq = sc == thr
        need = kf - _count(gt)
        n_eq = _count(eq)
        bias_ref[...] = jnp.where(sc >= thr, 0.0, NEG_INF)

        @pl.when(jnp.max(n_eq - need) > 0.5)
        def _():
            r_i = lax.broadcasted_iota(jnp.int32, (LANES, LANES), 0)
            c_i = lax.broadcasted_iota(jnp.int32, (LANES, LANES), 1)
            upper = jnp.where(r_i <= c_i, 1.0, 0.0).astype(BF16)
            carry_cnt = jnp.zeros((Q_BLOCK, 1), F32)
            for j in range(S // LANES):
                blk = sc_ref[:, j * LANES:(j + 1) * LANES]
                eq_j = blk == thr
                eq_f = jnp.where(eq_j, 1.0, 0.0)
                rank = _dot(eq_f.astype(BF16), upper) + carry_cnt
                keep = jnp.logical_or(blk > thr, jnp.logical_and(eq_j, rank <= need))
                bias_ref[:, j * LANES:(j + 1) * LANES] = jnp.where(keep, 0.0, NEG_INF)
                carry_cnt = carry_cnt + jnp.sum(eq_f, axis=-1, keepdims=True)

    k = kv_ref[0][:, :HEAD_DIM]
    v = kv_ref[0][:, HEAD_DIM:]
    bias = bias_ref[...]
    for hd in range(ATTN_HEADS):
        logits = _dot_t(q_ref[0][:, hd * HEAD_DIM:(hd + 1) * HEAD_DIM], k) + bias
        mx = jnp.max(logits, axis=-1, keepdims=True)
        p = jnp.exp(logits - mx)
        den = jnp.sum(p, axis=-1, keepdims=True)
        o_ref[0, :, hd * HEAD_DIM:(hd + 1) * HEAD_DIM] = _dot(p.astype(BF16), v) / den


def _dsa_attention(q, iq, iw, kv, ik):
    B, S, _ = q.shape
    blk = lambda b, j: (b, j, 0)
    per_b = lambda b, j: (b, 0, 0)
    return pl.pallas_call(
        _dsa_kernel,
        out_shape=jax.ShapeDtypeStruct((B, S, ATTN_WIDTH), F32),
        grid=(B, S // Q_BLOCK),
        in_specs=[
            pl.BlockSpec((1, Q_BLOCK, ATTN_WIDTH), blk),
            pl.BlockSpec((1, Q_BLOCK, IDX_HEADS * IDX_DIM), blk),
            pl.BlockSpec((1, Q_BLOCK, LANES), blk),
            pl.BlockSpec((1, S, LANES), per_b),
            pl.BlockSpec((1, S, LANES), per_b),
        ],
        out_specs=pl.BlockSpec((1, Q_BLOCK, ATTN_WIDTH), blk),
        scratch_shapes=[pltpu.VMEM((Q_BLOCK, S), F32), pltpu.VMEM((Q_BLOCK, S), F32)],
        compiler_params=_cparams("parallel", "parallel"),
        name="dsa_attention",
    )(q, iq, iw, kv, ik)


def _hyb_out_kernel(ya_ref, bcu_ref, halo_ref, cw_ref, w_ref, x_ref, g_ref, o_ref):
    j = pl.program_id(1)
    tm = ya_ref.shape[1]
    bcu = bcu_ref[0]
    bg = bcu[:, 0:512]
    z = bcu[:, 512:1024] * bcu[:, 1024:1536]
    halo = halo_ref[0]
    zh = halo[:, 512:1024] * halo[:, 1024:1536]
    zh = jnp.where(j > 0, zh, 0.0)
    row = lax.broadcasted_iota(jnp.int32, (tm, 1), 0)
    z1 = jnp.where(row >= 1, pltpu.roll(z, 1, 0), zh[7:8, :])
    z2 = jnp.where(row >= 2, pltpu.roll(z, 2, 0), jnp.where(row == 1, zh[7:8, :], zh[6:7, :]))
    cw = cw_ref[...]
    y_conv = bg * (z2 * cw[0:1, :] + z1 * cw[1:2, :] + z * cw[2:3, :])
    y = _dot(ya_ref[0].astype(BF16), w_ref[0:512, :]) + _dot(y_conv.astype(BF16), w_ref[512:1024, :])
    o_ref[0] = x_ref[0] + g_ref[0] * y


def _hyb_out(y_attn, bcu, conv_w, w_out_bf, x, g1):
    B, S, D = x.shape
    tm = TOKEN_TILE
    row = lambda b, j: (b, j, 0)
    per_b = lambda b, j: (b, 0, 0)
    const2 = lambda b, j: (0, 0)
    halo = lambda b, j: (b, jnp.maximum(j * (tm // 8) - 1, 0), 0)
    return pl.pallas_call(
        _hyb_out_kernel,
        out_shape=jax.ShapeDtypeStruct((B, S, D), F32),
        grid=(B, S // tm),
        in_specs=[
            pl.BlockSpec((1, tm, 512), row),
            pl.BlockSpec((1, tm, 1536), row),
            pl.BlockSpec((1, 8, 1536), halo),
            pl.BlockSpec((CONV_K, CONV_WIDTH), const2),
            pl.BlockSpec((D, D), const2),
            pl.BlockSpec((1, tm, D), row),
            pl.BlockSpec((1, 1, D), per_b),
        ],
        out_specs=pl.BlockSpec((1, tm, D), row),
        compiler_params=_cparams("parallel", "parallel"),
        name="hybrid_out_proj",
    )(y_attn, bcu, bcu, conv_w, w_out_bf, x, g1)


ML_COLS = 512 + 512 + 1024 + 1024 + LANES


def _ml_in_kernel(x_ref, gain_ref, sc_ref, sh_ref, w_ref, bias_ref, q_ref, k_ref, v_ref, og_ref, gt_ref):
    h = _modulated_norm(x_ref[0], gain_ref[...], sc_ref[0], sh_ref[0])
    p = _dot(h.astype(BF16), w_ref[...])
    q_ref[0] = (p[:, 0:512] * (ML_QK_DIM ** -0.5)).astype(BF16)
    k_ref[0] = p[:, 512:1024].astype(BF16)
    v_ref[0] = p[:, 1024:2048].astype(BF16)
    og_ref[0] = p[:, 2048:3072]
    gt_ref[0] = p[:, 3072:3200] + bias_ref[...]


def _ml_in(x, gain, sc, sh, w_pad, gate_bias):
    B, S, D = x.shape
    tm = TOKEN_TILE
    row = lambda b, j: (b, j, 0)
    per_b = lambda b, j: (b, 0, 0)
    const2 = lambda b, j: (0, 0)
    return pl.pallas_call(
        _ml_in_kernel,
        out_shape=(
            jax.ShapeDtypeStruct((B, S, 512), BF16),
            jax.ShapeDtypeStruct((B, S, 512), BF16),
            jax.ShapeDtypeStruct((B, S, 1024), BF16),
            jax.ShapeDtypeStruct((B, S, 1024), F32),
            jax.ShapeDtypeStruct((B, S, LANES), F32),
        ),
        grid=(B, S // tm),
        in_specs=[
            pl.BlockSpec((1, tm, D), row),
            pl.BlockSpec((1, D), const2),
            pl.BlockSpec((1, 1, D), per_b),
            pl.BlockSpec((1, 1, D), per_b),
            pl.BlockSpec((D, ML_COLS), const2),
            pl.BlockSpec((1, LANES), const2),
        ],
        out_specs=(
            pl.BlockSpec((1, tm, 512), row),
            pl.BlockSpec((1, tm, 512), row),
            pl.BlockSpec((1, tm, 1024), row),
            pl.BlockSpec((1, tm, 1024), row),
            pl.BlockSpec((1, tm, LANES), row),
        ),
        compiler_params=_cparams("parallel", "parallel"),
        name="mlstm_in_proj",
    )(x, gain, sc, sh, w_pad, gate_bias)


def _log_sigmoid(f):
    return jnp.minimum(f, 0.0) - jnp.log1p(jnp.exp(-jnp.abs(f)))


def _mlstm_kernel(q_ref, k_ref, v_ref, grow_ref, gcol_ref, gain_ref, o_ref, c_ref, n_ref, m_ref):
    L = ML_CHUNK
    S = q_ref.shape[1]
    c_ref[...] = jnp.zeros_like(c_ref)
    n_ref[...] = jnp.zeros_like(n_ref)
    m_ref[...] = jnp.zeros_like(m_ref)
    r_i = lax.broadcasted_iota(jnp.int32, (L, L), 0)
    c_i = lax.broadcasted_iota(jnp.int32, (L, L), 1)
    lower = jnp.where(c_i <= r_i, 1.0, 0.0)
    upper = jnp.where(r_i <= c_i, 1.0, 0.0)
    tril = c_i <= r_i

    def chunk(c, carry):
        r0 = pl.multiple_of(c * L, L)
        rows = grow_ref[0, 0, c]
        cols = gcol_ref[0, 0, c]
        b_rows = jnp.dot(_log_sigmoid(rows), upper, precision=HIGHEST, preferred_element_type=F32)
        b_cols = jnp.dot(lower, _log_sigmoid(cols), precision=HIGHEST, preferred_element_type=F32)
        for hh in range(2):
            q = q_ref[0, pl.ds(r0, L), hh * ML_QK_DIM:(hh + 1) * ML_QK_DIM]
            k = k_ref[0, pl.ds(r0, L), hh * ML_QK_DIM:(hh + 1) * ML_QK_DIM]
            v = v_ref[0, pl.ds(r0, L), hh * ML_V_DIM:(hh + 1) * ML_V_DIM]
            i_row = rows[hh:hh + 1, :]
            i_col = cols[:, hh:hh + 1]
            b_row = b_rows[2 + hh:3 + hh, :]
            b_col = b_cols[:, 2 + hh:3 + hh]
            b_last = b_row[:, L - 1:L]
            m_prev = m_ref[hh]
            ct = c_ref[hh]
            n_row = n_ref[hh]

            dmat = jnp.where(tril, b_col - b_row + i_row, NEG_INF)
            inter = b_col + m_prev
            m_t = jnp.maximum(inter, jnp.max(dmat, axis=-1, keepdims=True))
            w_intra = jnp.exp(dmat - m_t)
            w_inter = jnp.exp(inter - m_t)
            intra = w_intra * _dot_t(q, k)
            num = w_inter * _dot(q, ct.astype(BF16)) + _dot(intra.astype(BF16), v)
            qn = jnp.sum(q.astype(F32) * n_row, axis=-1, keepdims=True)
            den = w_inter * qn + jnp.sum(intra, axis=-1, keepdims=True)
            hc = num / jnp.maximum(jnp.abs(den), jnp.exp(-m_t))
            y = hc * lax.rsqrt(jnp.mean(hc * hc, axis=-1, keepdims=True) + NORM_EPS)
            o_ref[0, pl.ds(r0, L), hh * ML_V_DIM:(hh + 1) * ML_V_DIM] = (
                y * gain_ref[:, hh * ML_V_DIM:(hh + 1) * ML_V_DIM])

            g_row = b_last - b_row + i_row
            g_col = b_last - b_col + i_col
            m_new = jnp.maximum(b_last + m_prev, jnp.max(g_row, axis=-1, keepdims=True))
            decay = jnp.exp(b_last + m_prev - m_new)
            kw = k.astype(F32) * jnp.exp(g_col - m_new)
            c_ref[hh] = decay * ct + lax.dot_general(
                kw.astype(BF16), v, (((0,), (0,)), ((), ())), preferred_element_type=F32)
            n_ref[hh] = decay * n_row + jnp.sum(kw, axis=0, keepdims=True)
            m_ref[hh] = m_new
        return carry

    lax.fori_loop(0, S // L, chunk, 0)


def _mlstm(q, k, v, g_rows, g_cols, out_gain):
    B, S, _ = q.shape
    nc = S // ML_CHUNK
    pairs = ML_HEADS // 2
    return pl.pallas_call(
        _mlstm_kernel,
        out_shape=jax.ShapeDtypeStruct((B, S, ML_HEADS * ML_V_DIM), F32),
        grid=(B, pairs),
        in_specs=[
            pl.BlockSpec((1, S, 2 * ML_QK_DIM), lambda b, p: (b, 0, p)),
            pl.BlockSpec((1, S, 2 * ML_QK_DIM), lambda b, p: (b, 0, p)),
            pl.BlockSpec((1, S, 2 * ML_V_DIM), lambda b, p: (b, 0, p)),
            pl.BlockSpec((1, 1, nc, 8, ML_CHUNK), lambda b, p: (b, p, 0, 0, 0)),
            pl.BlockSpec((1, 1, nc, ML_CHUNK, LANES), lambda b, p: (b, p, 0, 0, 0)),
            pl.BlockSpec((1, 2 * ML_V_DIM), lambda b, p: (0, p)),
        ],
        out_specs=pl.BlockSpec((1, S, 2 * ML_V_DIM), lambda b, p: (b, 0, p)),
        scratch_shapes=[
            pltpu.VMEM((2, ML_QK_DIM, ML_V_DIM), F32),
            pltpu.VMEM((2, 1, ML_QK_DIM), F32),
            pltpu.VMEM((2, 1, 1), F32),
        ],
        compiler_params=_cparams("parallel", "parallel"),
        name="mlstm_chunkwise",
    )(q, k, v, g_rows, g_cols, out_gain)


def _ml_out_kernel(hh_ref, og_ref, w_ref, x_ref, g_ref, o_ref):
    a = jax.nn.sigmoid(og_ref[0]) * hh_ref[0]
    o_ref[0] = x_ref[0] + g_ref[0] * _dot(a.astype(BF16), w_ref[...])


def _ml_out(hh, og, w_out_bf, x, g1):
    B, S, D = x.shape
    tm = TOKEN_TILE
    row = lambda b, j: (b, j, 0)
    per_b = lambda b, j: (b, 0, 0)
    const2 = lambda b, j: (0, 0)
    return pl.pallas_call(
        _ml_out_kernel,
        out_shape=jax.ShapeDtypeStruct((B, S, D), F32),
        grid=(B, S // tm),
        in_specs=[
            pl.BlockSpec((1, tm, D), row),
            pl.BlockSpec((1, tm, D), row),
            pl.BlockSpec((D, D), const2),
            pl.BlockSpec((1, tm, D), row),
            pl.BlockSpec((1, 1, D), per_b),
        ],
        out_specs=pl.BlockSpec((1, tm, D), row),
        compiler_params=_cparams("parallel", "parallel"),
        name="mlstm_out_proj",
    )(hh, og, w_out_bf, x, g1)


def _first_argmax(x, lane, width):
    mx = jnp.max(x, axis=-1, keepdims=True)
    idx = jnp.min(jnp.where(x == mx, lane, width), axis=-1, keepdims=True)
    return mx, idx


def _router_kernel(x_ref, gain_ref, sc_ref, sh_ref, whi_ref, wlo_ref, b_ref, h_ref, r_ref, cnt_ref, run_ref):
    tm = x_ref.shape[1]

    @pl.when(jnp.logical_and(pl.program_id(0) == 0, pl.program_id(1) == 0))
    def _():
        run_ref[...] = jnp.zeros_like(run_ref)

    h = _modulated_norm(x_ref[0], gain_ref[...], sc_ref[0], sh_ref[0])
    h_hi = h.astype(BF16)
    h_ref[0] = h
    h_lo = (h - h_hi.astype(F32)).astype(BF16)
    whi = whi_ref[...]
    logits = _dot(h_hi, whi) + _dot(h_lo, whi) + _dot(h_hi, wlo_ref[...]) + b_ref[...]
    lane = lax.broadcasted_iota(jnp.int32, (1, LANES), 1)
    lg = jnp.where(lane < N_GROUPS, logits, NEG_INF)
    g_max, g_sel = _first_argmax(lg, lane, LANES)
    pg = 1.0 / jnp.sum(jnp.exp(lg - g_max), axis=-1, keepdims=True)
    e_lane = lane - N_GROUPS
    in_grp = jnp.logical_and(e_lane >= g_sel * EXPERTS_PER_GROUP, e_lane < (g_sel + 1) * EXPERTS_PER_GROUP)
    le = jnp.where(in_grp, logits, NEG_INF)
    v1, i1 = _first_argmax(le, lane, LANES)
    le2 = jnp.where(lane == i1, NEG_INF, le)
    v2, i2 = _first_argmax(le2, lane, LANES)
    e2 = jnp.exp(v2 - v1)
    w1 = pg / (1.0 + e2)
    w2 = pg * e2 / (1.0 + e2)
    e1 = i1 - N_GROUPS
    e2 = i2 - N_GROUPS
    hot1 = lane == e1
    hot2 = lane == e2
    onehot = jnp.where(jnp.logical_or(hot1, hot2), 1.0, 0.0)
    r_i = lax.broadcasted_iota(jnp.int32, (tm, tm), 0)
    c_i = lax.broadcasted_iota(jnp.int32, (tm, tm), 1)
    before = jnp.where(c_i < r_i, 1.0, 0.0).astype(BF16)
    seen = _dot(before, onehot.astype(BF16)) + run_ref[...]
    rank1 = jnp.sum(jnp.where(hot1, seen, 0.0), axis=-1, keepdims=True)
    rank2 = jnp.sum(jnp.where(hot2, seen, 0.0), axis=-1, keepdims=True)
    run_ref[...] = run_ref[...] + jnp.sum(onehot, axis=0, keepdims=True)
    cnt_ref[...] = run_ref[...]

    out = jnp.where(lane == 0, e1.astype(F32), 0.0)
    out = jnp.where(lane == 1, e2.astype(F32), out)
    out = jnp.where(lane == 2, w1, out)
    out = jnp.where(lane == 3, w2, out)
    out = jnp.where(lane == 4, rank1, out)
    out = jnp.where(lane == 5, rank2, out)
    r_ref[0] = out


def _router(x, gain, sc, sh, w_hi, w_lo, bias):
    B, S, D = x.shape
    tm = TOKEN_TILE
    row = lambda b, j: (b, j, 0)
    per_b = lambda b, j: (b, 0, 0)
    const2 = lambda b, j: (0, 0)
    return pl.pallas_call(
        _router_kernel,
        out_shape=(
            jax.ShapeDtypeStruct((B, S, D), F32),
            jax.ShapeDtypeStruct((B, S, LANES), F32),
            jax.ShapeDtypeStruct((1, LANES), F32),
        ),
        grid=(B, S // tm),
        in_specs=[
            pl.BlockSpec((1, tm, D), row),
            pl.BlockSpec((1, D), const2),
            pl.BlockSpec((1, 1, D), per_b),
            pl.BlockSpec((1, 1, D), per_b),
            pl.BlockSpec((D, LANES), const2),
            pl.BlockSpec((D, LANES), const2),
            pl.BlockSpec((1, LANES), const2),
        ],
        out_specs=(
            pl.BlockSpec((1, tm, D), row),
            pl.BlockSpec((1, tm, LANES), row),
            pl.BlockSpec((1, LANES), const2),
        ),
        scratch_shapes=[pltpu.VMEM((1, LANES), F32)],
        compiler_params=_cparams("arbitrary", "arbitrary"),
        name="moe_router",
    )(x, gain, sc, sh, w_hi, w_lo, bias)


def _experts_kernel(blk_e_ref, n_used_ref, x_ref, wg_ref, wu_ref, wd_ref, o_ref, wg_s, wu_s, wd_s):
    i = pl.program_id(0)
    used = i < n_used_ref[0]
    new_expert = jnp.logical_or(i == 0, blk_e_ref[i] != blk_e_ref[jnp.maximum(i - 1, 0)])

    @pl.when(jnp.logical_and(used, new_expert))
    def _():
        wg_s[...] = wg_ref[0, 0].astype(BF16)
        wu_s[...] = wu_ref[0, 0].astype(BF16)
        wd_s[...] = wd_ref[0, 0].astype(BF16)

    @pl.when(used)
    def _():
        x = x_ref[...].astype(BF16)
        a = _dot(x, wg_s[...])
        u = _dot(x, wu_s[...])
        act = a * jax.nn.sigmoid(a) * u
        o_ref[...] = _dot(act.astype(BF16), wd_s[...])

    @pl.when(i >= n_used_ref[0])
    def _():
        o_ref[...] = jnp.zeros_like(o_ref)


def _experts(layer, blk_e, n_used, xs, w_gate, w_up, w_down):
    R, D = xs.shape
    n_blk = R // MOE_BLOCK
    grid_spec = pltpu.PrefetchScalarGridSpec(
        num_scalar_prefetch=2,
        grid=(n_blk,),
        in_specs=[
            pl.BlockSpec((MOE_BLOCK, D), lambda i, be, nu: (i, 0)),
            pl.BlockSpec((1, 1, D, D_EXPERT), lambda i, be, nu: (layer, be[i], 0, 0)),
            pl.BlockSpec((1, 1, D, D_EXPERT), lambda i, be, nu: (layer, be[i], 0, 0)),
            pl.BlockSpec((1, 1, D_EXPERT, D), lambda i, be, nu: (layer, be[i], 0, 0)),
        ],
        out_specs=pl.BlockSpec((MOE_BLOCK, D), lambda i, be, nu: (i, 0)),
        scratch_shapes=[
            pltpu.VMEM((D, D_EXPERT), BF16),
            pltpu.VMEM((D, D_EXPERT), BF16),
            pltpu.VMEM((D_EXPERT, D), BF16),
        ],
    )
    return pl.pallas_call(
        _experts_kernel,
        out_shape=jax.ShapeDtypeStruct((R, D), F32),
        grid_spec=grid_spec,
        compiler_params=_cparams("arbitrary"),
        name="moe_experts",
    )(blk_e, n_used, xs, w_gate, w_up, w_down)


def _combine_kernel(x_ref, g_ref, y0_ref, y1_ref, r_ref, o_ref):
    r = r_ref[0]
    y = y0_ref[0] * r[:, 2:3] + y1_ref[0] * r[:, 3:4]
    o_ref[0] = x_ref[0] + g_ref[0] * y


def _combine(x, g2, y0, y1, route):
    B, S, D = x.shape
    tm = TOKEN_TILE
    row = lambda b, j: (b, j, 0)
    per_b = lambda b, j: (b, 0, 0)
    return pl.pallas_call(
        _combine_kernel,
        out_shape=jax.ShapeDtypeStruct((B, S, D), F32),
        grid=(B, S // tm),
        in_specs=[
            pl.BlockSpec((1, tm, D), row),
            pl.BlockSpec((1, 1, D), per_b),
            pl.BlockSpec((1, tm, D), row),
            pl.BlockSpec((1, tm, D), row),
            pl.BlockSpec((1, tm, LANES), row),
        ],
        out_specs=pl.BlockSpec((1, tm, D), row),
        compiler_params=_cparams("parallel", "parallel"),
        name="moe_combine",
    )(x, g2, y0, y1, route)


def _moe_dispatch(route, counts, T):
    A = T * TOP_K
    counts = counts[0, :N_EXPERTS].astype(jnp.int32)
    blocks_per = (counts + MOE_BLOCK - 1) // MOE_BLOCK
    block_end = jnp.cumsum(blocks_per)
    block_start = block_end - blocks_per
    expert = route[:, :TOP_K].astype(jnp.int32)
    rank = route[:, 4:4 + TOP_K].astype(jnp.int32)
    onehot = expert[:, :, None] == jnp.arange(N_EXPERTS, dtype=jnp.int32)
    start = jnp.sum(jnp.where(onehot, block_start, 0), axis=-1)
    dest = start * MOE_BLOCK + rank
    n_blk = -(-A // MOE_BLOCK) + N_EXPERTS
    R = n_blk * MOE_BLOCK
    tok = jnp.broadcast_to(jnp.arange(T, dtype=jnp.int32)[:, None], (T, TOP_K))
    row_tok = jnp.zeros((R,), jnp.int32).at[dest.reshape(A)].set(
        tok.reshape(A), mode="promise_in_bounds", unique_indices=True)
    blk = jnp.arange(n_blk, dtype=jnp.int32)
    blk_e = jnp.minimum(jnp.sum(blk[:, None] >= block_end[None, :], axis=-1), N_EXPERTS - 1)
    return row_tok, dest, blk_e.astype(jnp.int32), block_end[-1:].astype(jnp.int32)


def _rope_tables(S):
    inv = 1.0 / (ROPE_THETA ** (jnp.arange(0, HEAD_DIM, 2, dtype=F32) / HEAD_DIM))
    ang = jnp.arange(S, dtype=F32)[:, None] * inv[None, :]
    cos, sin = jnp.cos(ang), jnp.sin(ang)
    cos_h = jnp.concatenate([cos, cos], axis=-1)
    sin_h = jnp.concatenate([-sin, sin], axis=-1)
    return jnp.tile(cos_h, (1, ATTN_HEADS)), jnp.tile(sin_h, (1, ATTN_HEADS))


def _pad_cols(w, width):
    return jnp.pad(w, ((0, 0), (0, width - w.shape[1])))


def kernel(x, c, ada_w, ada_b, norm_mix, norm_ffn, hy_w_in, hy_q_norm, hy_k_norm, hy_conv_w, hy_w_out, ml_w_in, ml_b_gates, ml_out_norm, ml_w_out, moe_w_group, moe_b_group, moe_w_expert, moe_b_expert, moe_w_gate, moe_w_up, moe_w_down):
    B, S, D = x.shape
    T = B * S
    cos_t, sin_t = _rope_tables(S)
    mod = _ada_modulation(c, ada_w, ada_b).reshape(DEPTH, B, 6, 1, D)
    r_i = np.arange(ATTN_WIDTH)
    grp = jnp.asarray((r_i[:, None] // HEAD_DIM) == (r_i[None, :] // HEAD_DIM), dtype=BF16)

    for l in range(DEPTH):
        sh1, sc1, g1, sh2, sc2, g2 = [mod[l, :, i] for i in range(6)]
        gain1 = norm_mix[l].reshape(1, D)
        j = l // 2
        if l % 2 == 0:
            w = hy_w_in[j]
            o = np.cumsum((0,) + (ATTN_WIDTH, HEAD_DIM, HEAD_DIM, IDX_HEADS * IDX_DIM, IDX_DIM, IDX_HEADS,
                                  CONV_WIDTH, CONV_WIDTH, CONV_WIDTH))
            wq, wk, wv, wiq, wik, wiw, wbg, wcg, wu = [w[:, o[i]:o[i + 1]] for i in range(9)]
            w_pad = jnp.concatenate(
                [wq, wiq, wbg, wcg, wu, wk, wv, _pad_cols(jnp.concatenate([wik, wiw], axis=1), LANES)],
                axis=1).astype(BF16)
            qn_t = jnp.tile(hy_q_norm[j], ATTN_HEADS).reshape(1, ATTN_WIDTH)
            kn_t = jnp.tile(hy_k_norm[j], LANES // HEAD_DIM).reshape(1, LANES)
            q, iq, bcu, kv, ik, iw = _hyb_in(x, gain1, sc1, sh1, w_pad, cos_t, sin_t, qn_t, kn_t, grp)
            y_attn = _dsa_attention(q, iq, iw, kv, ik)
            x = _hyb_out(y_attn, bcu, hy_conv_w[j], hy_w_out[j].astype(BF16), x, g1)
        else:
            w = ml_w_in[j]
            hq = ML_HEADS * ML_QK_DIM
            hv = ML_HEADS * ML_V_DIM
            wq, wk, wv = w[:, :hq], w[:, hq:2 * hq], w[:, 2 * hq:2 * hq + hv]
            wg = w[:, 2 * hq + hv:2 * hq + hv + 2 * ML_HEADS]
            wo = w[:, 2 * hq + hv + 2 * ML_HEADS:]
            w_pad = jnp.concatenate([wq, wk, wv, wo, _pad_cols(wg, LANES)], axis=1).astype(BF16)
            gate_bias = jnp.pad(ml_b_gates[j], (0, LANES - 2 * ML_HEADS)).reshape(1, LANES)
            q, k, v, og, gates = _ml_in(x, gain1, sc1, sh1, w_pad, gate_bias)
            nc = S // ML_CHUNK
            pairs = ML_HEADS // 2
            gi = gates[:, :, :ML_HEADS].reshape(B, S, pairs, 2)
            gf = gates[:, :, ML_HEADS:2 * ML_HEADS].reshape(B, S, pairs, 2)
            gp = jnp.concatenate([gi, gf], axis=-1)
            gp = jnp.transpose(gp, (0, 2, 1, 3)).reshape(B, pairs, nc, ML_CHUNK, 4)
            g_cols = jnp.pad(gp, ((0, 0),) * 4 + ((0, LANES - 4),))
            g_rows = jnp.pad(jnp.swapaxes(gp, -1, -2), ((0, 0),) * 3 + ((0, 4), (0, 0)))
            hh = _mlstm(q, k, v, g_rows, g_cols, ml_out_norm[j].reshape(1, hv))
            x = _ml_out(hh, og, ml_w_out[j].astype(BF16), x, g1)

        w_r = _pad_cols(jnp.concatenate([moe_w_group[l], moe_w_expert[l]], axis=1), LANES)
        w_hi = w_r.astype(BF16)
        w_lo = (w_r - w_hi.astype(F32)).astype(BF16)
        b_r = jnp.pad(jnp.concatenate([moe_b_group[l], moe_b_expert[l]]), (0, LANES - N_GROUPS - N_EXPERTS))
        h2, route, counts = _router(x, norm_ffn[l].reshape(1, D), sc2, sh2, w_hi, w_lo, b_r.reshape(1, LANES))
        row_tok, dest, blk_e, n_used = _moe_dispatch(route.reshape(T, LANES), counts, T)
        xs = h2.reshape(T, D).at[row_tok].get(mode="promise_in_bounds")
        ys = _experts(l, blk_e, n_used, xs, moe_w_gate, moe_w_up, moe_w_down)
        y0 = ys.at[dest[:, 0]].get(mode="promise_in_bounds").reshape(B, S, D)
        y1 = ys.at[dest[:, 1]].get(mode="promise_in_bounds").reshape(B, S, D)
        x = _combine(x, g2, y0, y1, route)
    return x
```

```python
import functools

import numpy as np
import jax
import jax.numpy as jnp
from jax import lax
from jax.experimental import pallas as pl
from jax.experimental.pallas import tpu as pltpu

F32 = jnp.float32
BF16 = jnp.bfloat16
HIGHEST = lax.Precision.HIGHEST

D_MODEL = 1024
DEPTH = 4
ATTN_HEADS = 8
HEAD_DIM = 64
ATTN_WIDTH = ATTN_HEADS * HEAD_DIM
IDX_HEADS = 8
IDX_DIM = 64
INDEX_TOPK = 256
Q_BLOCK = 128
ROPE_THETA = 10000.0
CONV_WIDTH = D_MODEL - ATTN_WIDTH
CONV_K = 3
ML_HEADS = 8
ML_QK_DIM = 64
ML_V_DIM = 128
N_GROUPS = 4
EXPERTS_PER_GROUP = 8
N_EXPERTS = N_GROUPS * EXPERTS_PER_GROUP
TOP_K = 2
D_EXPERT = 512
MOE_BLOCK = 256
NORM_EPS = 1e-6

LANES = 128
VMEM_LIMIT = 56 * 1024 * 1024
TOKEN_TILE = 256
ML_CHUNK = 256
ML_STEP_HEADS = 4
NEG_INF = float("-inf")


def _cparams(*sem):
    return pltpu.CompilerParams(dimension_semantics=sem, vmem_limit_bytes=VMEM_LIMIT)


def _dot(a, b):
    return jnp.dot(a, b, preferred_element_type=F32)


def _dot_t(a, b):
    return lax.dot_general(a, b, (((1,), (1,)), ((), ())), preferred_element_type=F32)


def _split_dot(a_f32, b_bf16):
    hi = a_f32.astype(BF16)
    lo = (a_f32 - hi.astype(F32)).astype(BF16)
    return _dot(hi, b_bf16) + _dot(lo, b_bf16)


def _ada_kernel(c_ref, w_ref, b_ref, o_ref):
    c = c_ref[...]
    ca = c * jax.nn.sigmoid(c)
    o_ref[0] = jnp.dot(ca, w_ref[0], precision=HIGHEST, preferred_element_type=F32) + b_ref[0]


def _ada_modulation(c, ada_w, ada_b):
    B, D = c.shape
    n_col = ada_w.shape[-1] // D
    return pl.pallas_call(
        _ada_kernel,
        out_shape=jax.ShapeDtypeStruct((DEPTH, B, n_col * D), F32),
        grid=(DEPTH, n_col),
        in_specs=[
            pl.BlockSpec((B, D), lambda l, j: (0, 0)),
            pl.BlockSpec((1, D, D), lambda l, j: (l, 0, j)),
            pl.BlockSpec((1, 1, D), lambda l, j: (l, 0, j)),
        ],
        out_specs=pl.BlockSpec((1, B, D), lambda l, j: (l, 0, j)),
        compiler_params=_cparams("parallel", "parallel"),
        name="ada_modulation",
    )(c, ada_w, ada_b.reshape(DEPTH, 1, n_col * D))


def _modulated_norm(x, gain, scale, shift):
    y = x * lax.rsqrt(jnp.mean(x * x, axis=-1, keepdims=True) + NORM_EPS)
    return y * gain * (1.0 + scale) + shift


def _rope(x, cos, sin_signed, first_half):
    w = x.shape[-1]
    partner = jnp.where(first_half, pltpu.roll(x, w - HEAD_DIM // 2, 1), pltpu.roll(x, HEAD_DIM // 2, 1))
    return x * cos + partner * sin_signed


HYB_COLS = 5 * 512 + 2 * LANES


def _hyb_in_kernel(x_ref, gain_ref, sc_ref, sh_ref, w_ref, cos_ref, sin_ref, qn_ref, kn_ref, grp_ref,
                   q_ref, iq_ref, bcu_ref, kv_ref, ik_ref, iw_ref):
    h = _modulated_norm(x_ref[0], gain_ref[...], sc_ref[0], sh_ref[0])
    p = _dot(h.astype(BF16), w_ref[...])
    cos = cos_ref[...]
    sin = sin_ref[...]
    lane = lax.broadcasted_iota(jnp.int32, (1, ATTN_WIDTH), 1)
    first_half = (lane % HEAD_DIM) < (HEAD_DIM // 2)
    fh128 = first_half[:, :LANES]
    lane128 = lane[:, :LANES]

    q = p[:, 0:512]
    ms = _split_dot(q * q, grp_ref[...]) * (1.0 / HEAD_DIM)
    q = q * lax.rsqrt(ms + NORM_EPS) * qn_ref[...]
    q_ref[0] = (_rope(q, cos, sin, first_half) * (HEAD_DIM ** -0.5)).astype(BF16)

    iq = p[:, 512:1024]
    iq_ref[0] = (_rope(iq, cos, sin, first_half) * (IDX_DIM ** -0.5)).astype(BF16)

    bcu_ref[0] = p[:, 1024:2560]

    kv = p[:, 2560:2688]
    is_k = lane128 < HEAD_DIM
    kk = jnp.where(is_k, kv, 0.0)
    ms_k = jnp.sum(kk * kk, axis=-1, keepdims=True) * (1.0 / HEAD_DIM)
    kn = kv * lax.rsqrt(ms_k + NORM_EPS) * kn_ref[...]
    kr = _rope(kn, cos[:, :LANES], sin[:, :LANES], fh128)
    kv_ref[0] = jnp.where(is_k, kr, kv).astype(BF16)

    sm = p[:, 2688:2816]
    ikr = _rope(sm, cos[:, :LANES], sin[:, :LANES], fh128)
    ik_ref[0] = jnp.where(is_k, ikr, 0.0).astype(BF16)
    iw_ref[0] = sm


def _hyb_in(x, gain, sc, sh, w_pad, cos_t, sin_t, qn_t, kn_t, grp):
    B, S, D = x.shape
    tm = TOKEN_TILE
    row = lambda b, j: (b, j, 0)
    per_b = lambda b, j: (b, 0, 0)
    const2 = lambda b, j: (0, 0)
    tab = lambda b, j: (j, 0)
    return pl.pallas_call(
        _hyb_in_kernel,
        out_shape=(
            jax.ShapeDtypeStruct((B, S, 512), BF16),
            jax.ShapeDtypeStruct((B, S, 512), BF16),
            jax.ShapeDtypeStruct((B, S, 1536), F32),
            jax.ShapeDtypeStruct((B, S, LANES), BF16),
            jax.ShapeDtypeStruct((B, S, LANES), BF16),
            jax.ShapeDtypeStruct((B, S, LANES), F32),
        ),
        grid=(B, S // tm),
        in_specs=[
            pl.BlockSpec((1, tm, D), row),
            pl.BlockSpec((1, D), const2),
            pl.BlockSpec((1, 1, D), per_b),
            pl.BlockSpec((1, 1, D), per_b),
            pl.BlockSpec((D, HYB_COLS), const2),
            pl.BlockSpec((tm, 512), tab),
            pl.BlockSpec((tm, 512), tab),
            pl.BlockSpec((1, 512), const2),
            pl.BlockSpec((1, LANES), const2),
            pl.BlockSpec((512, 512), const2),
        ],
        out_specs=(
            pl.BlockSpec((1, tm, 512), row),
            pl.BlockSpec((1, tm, 512), row),
            pl.BlockSpec((1, tm, 1536), row),
            pl.BlockSpec((1, tm, LANES), row),
            pl.BlockSpec((1, tm, LANES), row),
            pl.BlockSpec((1, tm, LANES), row),
        ),
        compiler_params=_cparams("parallel", "parallel"),
        name="hybrid_in_proj",
    )(x, gain, sc, sh, w_pad, cos_t, sin_t, qn_t, kn_t, grp)


def _count(mask):
    return jnp.sum(jnp.where(mask, 1.0, 0.0), axis=-1, keepdims=True)


DSA_KEY_SPANS = 4


def _dsa_kernel(q_ref, iq_ref, iw_ref, kv_ref, ik_ref, o_ref, sc_ref, bias_ref):
    S = kv_ref.shape[1]
    span = S // DSA_KEY_SPANS
    qb = pl.program_id(1)
    for v in range(DSA_KEY_SPANS):
        sk = (v + 1) * span

        @pl.when(qb // (span // Q_BLOCK) == v)
        def _():
            _dsa_body(q_ref, iq_ref, iw_ref, kv_ref.at[:, :sk, :], ik_ref.at[:, :sk, :], o_ref,
                      sc_ref.at[:, :sk], bias_ref.at[:, :sk])


def _dsa_body(q_ref, iq_ref, iw_ref, kv_ref, ik_ref, o_ref, sc_ref, bias_ref):
    S = kv_ref.shape[1]
    qb = pl.program_id(1)
    kf = float(INDEX_TOPK)

    ik = ik_ref[0][:, :IDX_DIM]
    iw = iw_ref[0]
    acc = jnp.zeros((Q_BLOCK, S), F32)
    for hd in range(IDX_HEADS):
        s_h = _dot_t(iq_ref[0][:, hd * IDX_DIM:(hd + 1) * IDX_DIM], ik)
        w_h = iw[:, IDX_DIM + hd:IDX_DIM + hd + 1] * (IDX_HEADS ** -0.5)
        acc = acc + jnp.maximum(s_h, 0.0) * w_h
    qpos = qb * Q_BLOCK + lax.broadcasted_iota(jnp.int32, (Q_BLOCK, 1), 0)
    kpos = lax.broadcasted_iota(jnp.int32, (1, S), 1)
    causal = kpos <= qpos
    sc_ref[...] = jnp.where(causal, acc, NEG_INF)

    @pl.when(qb * Q_BLOCK + Q_BLOCK <= INDEX_TOPK)
    def _():
        bias_ref[...] = jnp.where(causal, 0.0, NEG_INF)

    @pl.when(qb * Q_BLOCK + Q_BLOCK > INDEX_TOPK)
    def _():
        sc = sc_ref[...]
        row_max = jnp.max(sc, axis=-1, keepdims=True)
        row_min = jnp.min(jnp.where(causal, sc, jnp.inf), axis=-1, keepdims=True)
        top_tied = _count(sc >= row_max) >= kf

        def bisect(_, carry):
            lo, hi = carry
            mid = 0.5 * lo + 0.5 * hi
            ge = _count(sc_ref[...] >= mid) >= kf
            return jnp.where(ge, mid, lo), jnp.where(ge, hi, mid)

        lo, hi = lax.fori_loop(0, 18, bisect, (row_min, row_max))

        def refine_cond(carry):
            it, _, _, done = carry
            return jnp.logical_and(it < S, jnp.min(done) < 0.5)

        def refine(carry):
            it, hi, thr, done = carry
            scv = sc_ref[...]
            m = jnp.max(jnp.where(scv < hi, scv, NEG_INF), axis=-1, keepdims=True)
            hit = _count(scv >= m) >= kf
            fin = done > 0.5
            thr = jnp.where(fin, thr, m)
            hi = jnp.where(jnp.logical_or(fin, hit), hi, m)
            done = jnp.where(hit, 1.0, done)
            return it + 1, hi, thr, done

        done0 = jnp.where(top_tied, 1.0, 0.0)
        _, _, thr, _ = lax.while_loop(refine_cond, refine, (jnp.int32(0), hi, row_max, done0))

        gt = sc > thr
        eq = sc == thr
        need = kf - _count(gt)
        n_eq = _count(eq)
        bias_ref[...] = jnp.where(sc >= thr, 0.0, NEG_INF)

        @pl.when(jnp.max(n_eq - need) > 0.5)
        def _():
            r_i = lax.broadcasted_iota(jnp.int32, (LANES, LANES), 0)
            c_i = lax.broadcasted_iota(jnp.int32, (LANES, LANES), 1)
            upper = jnp.where(r_i <= c_i, 1.0, 0.0).astype(BF16)
            carry_cnt = jnp.zeros((Q_BLOCK, 1), F32)
            for j in range(S // LANES):
                blk = sc_ref[:, j * LANES:(j + 1) * LANES]
                eq_j = blk == thr
                eq_f = jnp.where(eq_j, 1.0, 0.0)
                rank = _dot(eq_f.astype(BF16), upper) + carry_cnt
                keep = jnp.logical_or(blk > thr, jnp.logical_and(eq_j, rank <= need))
                bias_ref[:, j * LANES:(j + 1) * LANES] = jnp.where(keep, 0.0, NEG_INF)
                carry_cnt = carry_cnt + jnp.sum(eq_f, axis=-1, keepdims=True)

    k = kv_ref[0][:, :HEAD_DIM]
    v = kv_ref[0][:, HEAD_DIM:]
    bias = bias_ref[...]
    for hd in range(ATTN_HEADS):
        logits = _dot_t(q_ref[0][:, hd * HEAD_DIM:(hd + 1) * HEAD_DIM], k) + bias
        mx = jnp.max(logits, axis=-1, keepdims=True)
        p = jnp.exp(logits - mx)
        den = jnp.sum(p, axis=-1, keepdims=True)
        o_ref[0, :, hd * HEAD_DIM:(hd + 1) * HEAD_DIM] = _dot(p.astype(BF16), v) / den


def _dsa_attention(q, iq, iw, kv, ik):
    B, S, _ = q.shape
    blk = lambda b, j: (b, j, 0)
    per_b = lambda b, j: (b, 0, 0)
    return pl.pallas_call(
        _dsa_kernel,
        out_shape=jax.ShapeDtypeStruct((B, S, ATTN_WIDTH), F32),
        grid=(B, S // Q_BLOCK),
        in_specs=[
            pl.BlockSpec((1, Q_BLOCK, ATTN_WIDTH), blk),
            pl.BlockSpec((1, Q_BLOCK, IDX_HEADS * IDX_DIM), blk),
            pl.BlockSpec((1, Q_BLOCK, LANES), blk),
            pl.BlockSpec((1, S, LANES), per_b),
            pl.BlockSpec((1, S, LANES), per_b),
        ],
        out_specs=pl.BlockSpec((1, Q_BLOCK, ATTN_WIDTH), blk),
        scratch_shapes=[pltpu.VMEM((Q_BLOCK, S), F32), pltpu.VMEM((Q_BLOCK, S), F32)],
        compiler_params=_cparams("parallel", "parallel"),
        name="dsa_attention",
    )(q, iq, iw, kv, ik)


def _hyb_out_kernel(ya_ref, bcu_ref, halo_ref, cw_ref, w_ref, x_ref, g_ref, o_ref):
    j = pl.program_id(1)
    tm = ya_ref.shape[1]
    bcu = bcu_ref[0]
    bg = bcu[:, 0:512]
    z = bcu[:, 512:1024] * bcu[:, 1024:1536]
    halo = halo_ref[0]
    zh = halo[:, 512:1024] * halo[:, 1024:1536]
    zh = jnp.where(j > 0, zh, 0.0)
    row = lax.broadcasted_iota(jnp.int32, (tm, 1), 0)
    z1 = jnp.where(row >= 1, pltpu.roll(z, 1, 0), zh[7:8, :])
    z2 = jnp.where(row >= 2, pltpu.roll(z, 2, 0), jnp.where(row == 1, zh[7:8, :], zh[6:7, :]))
    cw = cw_ref[...]
    y_conv = bg * (z2 * cw[0:1, :] + z1 * cw[1:2, :] + z * cw[2:3, :])
    y = _dot(ya_ref[0].astype(BF16), w_ref[0:512, :]) + _dot(y_conv.astype(BF16), w_ref[512:1024, :])
    o_ref[0] = x_ref[0] + g_ref[0] * y


def _hyb_out(y_attn, bcu, conv_w, w_out_bf, x, g1):
    B, S, D = x.shape
    tm = TOKEN_TILE
    row = lambda b, j: (b, j, 0)
    per_b = lambda b, j: (b, 0, 0)
    const2 = lambda b, j: (0, 0)
    halo = lambda b, j: (b, jnp.maximum(j * (tm // 8) - 1, 0), 0)
    return pl.pallas_call(
        _hyb_out_kernel,
        out_shape=jax.ShapeDtypeStruct((B, S, D), F32),
        grid=(B, S // tm),
        in_specs=[
            pl.BlockSpec((1, tm, 512), row),
            pl.BlockSpec((1, tm, 1536), row),
            pl.BlockSpec((1, 8, 1536), halo),
            pl.BlockSpec((CONV_K, CONV_WIDTH), const2),
            pl.BlockSpec((D, D), const2),
            pl.BlockSpec((1, tm, D), row),
            pl.BlockSpec((1, 1, D), per_b),
        ],
        out_specs=pl.BlockSpec((1, tm, D), row),
        compiler_params=_cparams("parallel", "parallel"),
        name="hybrid_out_proj",
    )(y_attn, bcu, bcu, conv_w, w_out_bf, x, g1)


ML_COLS = 512 + 512 + 1024 + 1024 + LANES


def _ml_in_kernel(x_ref, gain_ref, sc_ref, sh_ref, w_ref, bias_ref, q_ref, k_ref, v_ref, og_ref, gt_ref):
    h = _modulated_norm(x_ref[0], gain_ref[...], sc_ref[0], sh_ref[0])
    p = _dot(h.astype(BF16), w_ref[...])
    q_ref[0] = (p[:, 0:512] * (ML_QK_DIM ** -0.5)).astype(BF16)
    k_ref[0] = p[:, 512:1024].astype(BF16)
    v_ref[0] = p[:, 1024:2048].astype(BF16)
    og_ref[0] = p[:, 2048:3072]
    gt_ref[0] = p[:, 3072:3200] + bias_ref[...]


def _ml_in(x, gain, sc, sh, w_pad, gate_bias):
    B, S, D = x.shape
    tm = TOKEN_TILE
    row = lambda b, j: (b, j, 0)
    per_b = lambda b, j: (b, 0, 0)
    const2 = lambda b, j: (0, 0)
    return pl.pallas_call(
        _ml_in_kernel,
        out_shape=(
            jax.ShapeDtypeStruct((B, S, 512), BF16),
            jax.ShapeDtypeStruct((B, S, 512), BF16),
            jax.ShapeDtypeStruct((B, S, 1024), BF16),
            jax.ShapeDtypeStruct((B, S, 1024), F32),
            jax.ShapeDtypeStruct((B, S, LANES), F32),
        ),
        grid=(B, S // tm),
        in_specs=[
            pl.BlockSpec((1, tm, D), row),
            pl.BlockSpec((1, D), const2),
            pl.BlockSpec((1, 1, D), per_b),
            pl.BlockSpec((1, 1, D), per_b),
            pl.BlockSpec((D, ML_COLS), const2),
            pl.BlockSpec((1, LANES), const2),
        ],
        out_specs=(
            pl.BlockSpec((1, tm, 512), row),
            pl.BlockSpec((1, tm, 512), row),
            pl.BlockSpec((1, tm, 1024), row),
            pl.BlockSpec((1, tm, 1024), row),
            pl.BlockSpec((1, tm, LANES), row),
        ),
        compiler_params=_cparams("parallel", "parallel"),
        name="mlstm_in_proj",
    )(x, gain, sc, sh, w_pad, gate_bias)


def _log_sigmoid(f):
    return jnp.minimum(f, 0.0) - jnp.log1p(jnp.exp(-jnp.abs(f)))


def _split3(x):
    a = x.astype(BF16)
    r = x - a.astype(F32)
    b = r.astype(BF16)
    c = (r - b.astype(F32)).astype(BF16)
    return a, b, c


def _mlstm_kernel(q_ref, k_ref, v_ref, grow_ref, gcol_ref, gain_ref, o_ref, c_ref, n_ref, m_ref):
    L = ML_CHUNK
    HP = ML_STEP_HEADS
    S = q_ref.shape[1]
    c_ref[...] = jnp.zeros_like(c_ref)
    n_ref[...] = jnp.zeros_like(n_ref)
    m_ref[...] = jnp.zeros_like(m_ref)

    def chunk(c, carry):
        r0 = pl.multiple_of(c * L, L)
        r_i = lax.broadcasted_iota(jnp.int32, (L, L), 0)
        c_i = lax.broadcasted_iota(jnp.int32, (L, L), 1)
        tril = c_i <= r_i
        lower = jnp.where(tril, 1.0, 0.0).astype(BF16)
        upper = jnp.where(r_i <= c_i, 1.0, 0.0).astype(BF16)
        rows = grow_ref[0, 0, c]
        cols = gcol_ref[0, 0, c]
        b_rows = sum(_dot(p, upper) for p in _split3(_log_sigmoid(rows)))
        b_cols = sum(_dot(lower, p) for p in _split3(_log_sigmoid(cols)))
        for hh in range(HP):
            q = q_ref[0, pl.ds(r0, L), hh * ML_QK_DIM:(hh + 1) * ML_QK_DIM]
            k = k_ref[0, pl.ds(r0, L), hh * ML_QK_DIM:(hh + 1) * ML_QK_DIM]
            v = v_ref[0, pl.ds(r0, L), hh * ML_V_DIM:(hh + 1) * ML_V_DIM]
            i_row = rows[hh:hh + 1, :]
            i_col = cols[:, hh:hh + 1]
            b_row = b_rows[HP + hh:HP + hh + 1, :]
            b_col = b_cols[:, HP + hh:HP + hh + 1]
            b_last = b_row[:, L - 1:L]
            m_prev = m_ref[hh]
            ct = c_ref[hh]
            n_row = n_ref[hh]

            dmat = jnp.where(tril, b_col - b_row + i_row, NEG_INF)
            inter = b_col + m_prev
            m_t = jnp.maximum(inter, jnp.max(dmat, axis=-1, keepdims=True))
            w_intra = jnp.exp(dmat - m_t)
            w_inter = jnp.exp(inter - m_t)
            intra = w_intra * _dot_t(q, k)
            num = w_inter * _dot(q, ct.astype(BF16)) + _dot(intra.astype(BF16), v)
            qn = jnp.sum(q.astype(F32) * n_row, axis=-1, keepdims=True)
            den = w_inter * qn + jnp.sum(intra, axis=-1, keepdims=True)
            hc = num / jnp.maximum(jnp.abs(den), jnp.exp(-m_t))
            y = hc * lax.rsqrt(jnp.mean(hc * hc, axis=-1, keepdims=True) + NORM_EPS)
            o_ref[0, pl.ds(r0, L), hh * ML_V_DIM:(hh + 1) * ML_V_DIM] = (
                y * gain_ref[:, hh * ML_V_DIM:(hh + 1) * ML_V_DIM])

            g_row = b_last - b_row + i_row
            g_col = b_last - b_col + i_col
            m_new = jnp.maximum(b_last + m_prev, jnp.max(g_row, axis=-1, keepdims=True))
            decay = jnp.exp(b_last + m_prev - m_new)
            kw = k.astype(F32) * jnp.exp(g_col - m_new)
            c_ref[hh] = decay * ct + lax.dot_general(
                kw.astype(BF16), v, (((0,), (0,)), ((), ())), preferred_element_type=F32)
            n_ref[hh] = decay * n_row + jnp.sum(kw, axis=0, keepdims=True)
            m_ref[hh] = m_new
        return carry

    lax.fori_loop(0, S // L, chunk, 0)


def _mlstm(q, k, v, g_rows, g_cols, out_gain):
    B, S, _ = q.shape
    nc = S // ML_CHUNK
    hp = ML_STEP_HEADS
    return pl.pallas_call(
        _mlstm_kernel,
        out_shape=jax.ShapeDtypeStruct((B, S, ML_HEADS * ML_V_DIM), F32),
        grid=(B, ML_HEADS // hp),
        in_specs=[
            pl.BlockSpec((1, S, hp * ML_QK_DIM), lambda b, p: (b, 0, p)),
            pl.BlockSpec((1, S, hp * ML_QK_DIM), lambda b, p: (b, 0, p)),
            pl.BlockSpec((1, S, hp * ML_V_DIM), lambda b, p: (b, 0, p)),
            pl.BlockSpec((1, 1, nc, 2 * hp, ML_CHUNK), lambda b, p: (b, p, 0, 0, 0)),
            pl.BlockSpec((1, 1, nc, ML_CHUNK, LANES), lambda b, p: (b, p, 0, 0, 0)),
            pl.BlockSpec((1, hp * ML_V_DIM), lambda b, p: (0, p)),
        ],
        out_specs=pl.BlockSpec((1, S, hp * ML_V_DIM), lambda b, p: (b, 0, p)),
        scratch_shapes=[
            pltpu.VMEM((hp, ML_QK_DIM, ML_V_DIM), F32),
            pltpu.VMEM((hp, 1, ML_QK_DIM), F32),
            pltpu.VMEM((hp, 1, 1), F32),
        ],
        compiler_params=_cparams("parallel", "parallel"),
        name="mlstm_chunkwise",
    )(q, k, v, g_rows, g_cols, out_gain)


def _ml_out_kernel(hh_ref, og_ref, w_ref, x_ref, g_ref, o_ref):
    a = jax.nn.sigmoid(og_ref[0]) * hh_ref[0]
    o_ref[0] = x_ref[0] + g_ref[0] * _dot(a.astype(BF16), w_ref[...])


def _ml_out(hh, og, w_out_bf, x, g1):
    B, S, D = x.shape
    tm = TOKEN_TILE
    row = lambda b, j: (b, j, 0)
    per_b = lambda b, j: (b, 0, 0)
    const2 = lambda b, j: (0, 0)
    return pl.pallas_call(
        _ml_out_kernel,
        out_shape=jax.ShapeDtypeStruct((B, S, D), F32),
        grid=(B, S // tm),
        in_specs=[
            pl.BlockSpec((1, tm, D), row),
            pl.BlockSpec((1, tm, D), row),
            pl.BlockSpec((D, D), const2),
            pl.BlockSpec((1, tm, D), row),
            pl.BlockSpec((1, 1, D), per_b),
        ],
        out_specs=pl.BlockSpec((1, tm, D), row),
        compiler_params=_cparams("parallel", "parallel"),
        name="mlstm_out_proj",
    )(hh, og, w_out_bf, x, g1)


def _first_argmax(x, lane, width):
    mx = jnp.max(x, axis=-1, keepdims=True)
    idx = jnp.min(jnp.where(x == mx, lane, width), axis=-1, keepdims=True)
    return mx, idx


def _router_kernel(x_ref, gain_ref, sc_ref, sh_ref, whi_ref, wlo_ref, b_ref, h_ref, r_ref, cnt_ref, run_ref):
    tm = x_ref.shape[1]

    @pl.when(jnp.logical_and(pl.program_id(0) == 0, pl.program_id(1) == 0))
    def _():
        run_ref[...] = jnp.zeros_like(run_ref)

    h = _modulated_norm(x_ref[0], gain_ref[...], sc_ref[0], sh_ref[0])
    h_hi = h.astype(BF16)
    h_ref[0] = h
    h_lo = (h - h_hi.astype(F32)).astype(BF16)
    whi = whi_ref[...]
    logits = _dot(h_hi, whi) + _dot(h_lo, whi) + _dot(h_hi, wlo_ref[...]) + b_ref[...]
    lane = lax.broadcasted_iota(jnp.int32, (1, LANES), 1)
    lg = jnp.where(lane < N_GROUPS, logits, NEG_INF)
    g_max, g_sel = _first_argmax(lg, lane, LANES)
    pg = 1.0 / jnp.sum(jnp.exp(lg - g_max), axis=-1, keepdims=True)
    e_lane = lane - N_GROUPS
    in_grp = jnp.logical_and(e_lane >= g_sel * EXPERTS_PER_GROUP, e_lane < (g_sel + 1) * EXPERTS_PER_GROUP)
    le = jnp.where(in_grp, logits, NEG_INF)
    v1, i1 = _first_argmax(le, lane, LANES)
    le2 = jnp.where(lane == i1, NEG_INF, le)
    v2, i2 = _first_argmax(le2, lane, LANES)
    e2 = jnp.exp(v2 - v1)
    w1 = pg / (1.0 + e2)
    w2 = pg * e2 / (1.0 + e2)
    e1 = i1 - N_GROUPS
    e2 = i2 - N_GROUPS
    hot1 = lane == e1
    hot2 = lane == e2
    onehot = jnp.where(jnp.logical_or(hot1, hot2), 1.0, 0.0)
    r_i = lax.broadcasted_iota(jnp.int32, (tm, tm), 0)
    c_i = lax.broadcasted_iota(jnp.int32, (tm, tm), 1)
    before = jnp.where(c_i < r_i, 1.0, 0.0).astype(BF16)
    seen = _dot(before, onehot.astype(BF16)) + run_ref[...]
    rank1 = jnp.sum(jnp.where(hot1, seen, 0.0), axis=-1, keepdims=True)
    rank2 = jnp.sum(jnp.where(hot2, seen, 0.0), axis=-1, keepdims=True)
    run_ref[...] = run_ref[...] + jnp.sum(onehot, axis=0, keepdims=True)
    cnt_ref[...] = run_ref[...]

    out = jnp.where(lane == 0, e1.astype(F32), 0.0)
    out = jnp.where(lane == 1, e2.astype(F32), out)
    out = jnp.where(lane == 2, w1, out)
    out = jnp.where(lane == 3, w2, out)
    out = jnp.where(lane == 4, rank1, out)
    out = jnp.where(lane == 5, rank2, out)
    r_ref[0] = out


def _router(x, gain, sc, sh, w_hi, w_lo, bias):
    B, S, D = x.shape
    tm = TOKEN_TILE
    row = lambda b, j: (b, j, 0)
    per_b = lambda b, j: (b, 0, 0)
    const2 = lambda b, j: (0, 0)
    return pl.pallas_call(
        _router_kernel,
        out_shape=(
            jax.ShapeDtypeStruct((B, S, D), F32),
            jax.ShapeDtypeStruct((B, S, LANES), F32),
            jax.ShapeDtypeStruct((1, LANES), F32),
        ),
        grid=(B, S // tm),
        in_specs=[
            pl.BlockSpec((1, tm, D), row),
            pl.BlockSpec((1, D), const2),
            pl.BlockSpec((1, 1, D), per_b),
            pl.BlockSpec((1, 1, D), per_b),
            pl.BlockSpec((D, LANES), const2),
            pl.BlockSpec((D, LANES), const2),
            pl.BlockSpec((1, LANES), const2),
        ],
        out_specs=(
            pl.BlockSpec((1, tm, D), row),
            pl.BlockSpec((1, tm, LANES), row),
            pl.BlockSpec((1, LANES), const2),
        ),
        scratch_shapes=[pltpu.VMEM((1, LANES), F32)],
        compiler_params=_cparams("arbitrary", "arbitrary"),
        name="moe_router",
    )(x, gain, sc, sh, w_hi, w_lo, bias)


def _experts_kernel(blk_e_ref, n_used_ref, x_ref, wg_ref, wu_ref, wd_ref, o_ref, wg_s, wu_s, wd_s):
    i = pl.program_id(0)
    used = i < n_used_ref[0]
    new_expert = jnp.logical_or(i == 0, blk_e_ref[i] != blk_e_ref[jnp.maximum(i - 1, 0)])

    @pl.when(jnp.logical_and(used, new_expert))
    def _():
        wg_s[...] = wg_ref[0, 0].astype(BF16)
        wu_s[...] = wu_ref[0, 0].astype(BF16)
        wd_s[...] = wd_ref[0, 0].astype(BF16)

    @pl.when(used)
    def _():
        x = x_ref[...].astype(BF16)
        a = _dot(x, wg_s[...])
        u = _dot(x, wu_s[...])
        act = a * jax.nn.sigmoid(a) * u
        o_ref[...] = _dot(act.astype(BF16), wd_s[...])

    @pl.when(i >= n_used_ref[0])
    def _():
        o_ref[...] = jnp.zeros_like(o_ref)


def _experts(layer, blk_e, n_used, xs, w_gate, w_up, w_down):
    R, D = xs.shape
    n_blk = R // MOE_BLOCK
    grid_spec = pltpu.PrefetchScalarGridSpec(
        num_scalar_prefetch=2,
        grid=(n_blk,),
        in_specs=[
            pl.BlockSpec((MOE_BLOCK, D), lambda i, be, nu: (i, 0)),
            pl.BlockSpec((1, 1, D, D_EXPERT), lambda i, be, nu: (layer, be[i], 0, 0)),
            pl.BlockSpec((1, 1, D, D_EXPERT), lambda i, be, nu: (layer, be[i], 0, 0)),
            pl.BlockSpec((1, 1, D_EXPERT, D), lambda i, be, nu: (layer, be[i], 0, 0)),
        ],
        out_specs=pl.BlockSpec((MOE_BLOCK, D), lambda i, be, nu: (i, 0)),
        scratch_shapes=[
            pltpu.VMEM((D, D_EXPERT), BF16),
            pltpu.VMEM((D, D_EXPERT), BF16),
            pltpu.VMEM((D_EXPERT, D), BF16),
        ],
    )
    return pl.pallas_call(
        _experts_kernel,
        out_shape=jax.ShapeDtypeStruct((R, D), F32),
        grid_spec=grid_spec,
        compiler_params=_cparams("arbitrary"),
        name="moe_experts",
    )(blk_e, n_used, xs, w_gate, w_up, w_down)


def _combine_kernel(x_ref, g_ref, y0_ref, y1_ref, r_ref, o_ref):
    r = r_ref[0]
    y = y0_ref[0] * r[:, 2:3] + y1_ref[0] * r[:, 3:4]
    o_ref[0] = x_ref[0] + g_ref[0] * y


def _combine(x, g2, y0, y1, route):
    B, S, D = x.shape
    tm = TOKEN_TILE
    row = lambda b, j: (b, j, 0)
    per_b = lambda b, j: (b, 0, 0)
    return pl.pallas_call(
        _combine_kernel,
        out_shape=jax.ShapeDtypeStruct((B, S, D), F32),
        grid=(B, S // tm),
        in_specs=[
            pl.BlockSpec((1, tm, D), row),
            pl.BlockSpec((1, 1, D), per_b),
            pl.BlockSpec((1, tm, D), row),
            pl.BlockSpec((1, tm, D), row),
            pl.BlockSpec((1, tm, LANES), row),
        ],
        out_specs=pl.BlockSpec((1, tm, D), row),
        compiler_params=_cparams("parallel", "parallel"),
        name="moe_combine",
    )(x, g2, y0, y1, route)


def _moe_dispatch(route, counts, T):
    A = T * TOP_K
    counts = counts[0, :N_EXPERTS].astype(jnp.int32)
    blocks_per = (counts + MOE_BLOCK - 1) // MOE_BLOCK
    block_end = jnp.cumsum(blocks_per)
    block_start = block_end - blocks_per
    expert = route[:, :TOP_K].astype(jnp.int32)
    rank = route[:, 4:4 + TOP_K].astype(jnp.int32)
    onehot = expert[:, :, None] == jnp.arange(N_EXPERTS, dtype=jnp.int32)
    start = jnp.sum(jnp.where(onehot, block_start, 0), axis=-1)
    dest = start * MOE_BLOCK + rank
    n_blk = -(-A // MOE_BLOCK) + N_EXPERTS
    R = n_blk * MOE_BLOCK
    tok = jnp.broadcast_to(jnp.arange(T, dtype=jnp.int32)[:, None], (T, TOP_K))
    row_tok = (jnp.arange(R, dtype=jnp.int32) % T).at[dest.reshape(A)].set(
        tok.reshape(A), mode="promise_in_bounds", unique_indices=True)
    blk = jnp.arange(n_blk, dtype=jnp.int32)
    blk_e = jnp.minimum(jnp.sum(blk[:, None] >= block_end[None, :], axis=-1), N_EXPERTS - 1)
    return row_tok, dest, blk_e.astype(jnp.int32), block_end[-1:].astype(jnp.int32)


def _rope_tables(S):
    inv = 1.0 / (ROPE_THETA ** (jnp.arange(0, HEAD_DIM, 2, dtype=F32) / HEAD_DIM))
    ang = jnp.arange(S, dtype=F32)[:, None] * inv[None, :]
    cos, sin = jnp.cos(ang), jnp.sin(ang)
    cos_h = jnp.concatenate([cos, cos], axis=-1)
    sin_h = jnp.concatenate([-sin, sin], axis=-1)
    return jnp.tile(cos_h, (1, ATTN_HEADS)), jnp.tile(sin_h, (1, ATTN_HEADS))


def _pad_cols(w, width):
    return jnp.pad(w, ((0, 0), (0, width - w.shape[1])))


def kernel(x, c, ada_w, ada_b, norm_mix, norm_ffn, hy_w_in, hy_q_norm, hy_k_norm, hy_conv_w, hy_w_out, ml_w_in, ml_b_gates, ml_out_norm, ml_w_out, moe_w_group, moe_b_group, moe_w_expert, moe_b_expert, moe_w_gate, moe_w_up, moe_w_down):
    B, S, D = x.shape
    T = B * S
    cos_t, sin_t = _rope_tables(S)
    mod = _ada_modulation(c, ada_w, ada_b).reshape(DEPTH, B, 6, 1, D)
    r_i = np.arange(ATTN_WIDTH)
    grp = jnp.asarray((r_i[:, None] // HEAD_DIM) == (r_i[None, :] // HEAD_DIM), dtype=BF16)

    for l in range(DEPTH):
        sh1, sc1, g1, sh2, sc2, g2 = [mod[l, :, i] for i in range(6)]
        gain1 = norm_mix[l].reshape(1, D)
        j = l // 2
        if l % 2 == 0:
            w = hy_w_in[j]
            o = np.cumsum((0,) + (ATTN_WIDTH, HEAD_DIM, HEAD_DIM, IDX_HEADS * IDX_DIM, IDX_DIM, IDX_HEADS,
                                  CONV_WIDTH, CONV_WIDTH, CONV_WIDTH))
            wq, wk, wv, wiq, wik, wiw, wbg, wcg, wu = [w[:, o[i]:o[i + 1]] for i in range(9)]
            w_pad = jnp.concatenate(
                [wq, wiq, wbg, wcg, wu, wk, wv, _pad_cols(jnp.concatenate([wik, wiw], axis=1), LANES)],
                axis=1).astype(BF16)
            qn_t = jnp.tile(hy_q_norm[j], ATTN_HEADS).reshape(1, ATTN_WIDTH)
            kn_t = jnp.tile(hy_k_norm[j], LANES // HEAD_DIM).reshape(1, LANES)
            q, iq, bcu, kv, ik, iw = _hyb_in(x, gain1, sc1, sh1, w_pad, cos_t, sin_t, qn_t, kn_t, grp)
            y_attn = _dsa_attention(q, iq, iw, kv, ik)
            x = _hyb_out(y_attn, bcu, hy_conv_w[j], hy_w_out[j].astype(BF16), x, g1)
        else:
            w = ml_w_in[j]
            hq = ML_HEADS * ML_QK_DIM
            hv = ML_HEADS * ML_V_DIM
            wq, wk, wv = w[:, :hq], w[:, hq:2 * hq], w[:, 2 * hq:2 * hq + hv]
            wg = w[:, 2 * hq + hv:2 * hq + hv + 2 * ML_HEADS]
            wo = w[:, 2 * hq + hv + 2 * ML_HEADS:]
            w_pad = jnp.concatenate([wq, wk, wv, wo, _pad_cols(wg, LANES)], axis=1).astype(BF16)
            gate_bias = jnp.pad(ml_b_gates[j], (0, LANES - 2 * ML_HEADS)).reshape(1, LANES)
            q, k, v, og, gates = _ml_in(x, gain1, sc1, sh1, w_pad, gate_bias)
            nc = S // ML_CHUNK
            hp = ML_STEP_HEADS
            groups = ML_HEADS // hp
            gi = gates[:, :, :ML_HEADS].reshape(B, S, groups, hp)
            gf = gates[:, :, ML_HEADS:2 * ML_HEADS].reshape(B, S, groups, hp)
            gp = jnp.concatenate([gi, gf], axis=-1)
            gp = jnp.transpose(gp, (0, 2, 1, 3)).reshape(B, groups, nc, ML_CHUNK, 2 * hp)
            g_cols = jnp.pad(gp, ((0, 0),) * 4 + ((0, LANES - 2 * hp),))
            g_rows = jnp.swapaxes(gp, -1, -2)
            hh = _mlstm(q, k, v, g_rows, g_cols, ml_out_norm[j].reshape(1, hv))
            x = _ml_out(hh, og, ml_w_out[j].astype(BF16), x, g1)

        w_r = _pad_cols(jnp.concatenate([moe_w_group[l], moe_w_expert[l]], axis=1), LANES)
        w_hi = w_r.astype(BF16)
        w_lo = (w_r - w_hi.astype(F32)).astype(BF16)
        b_r = jnp.pad(jnp.concatenate([moe_b_group[l], moe_b_expert[l]]), (0, LANES - N_GROUPS - N_EXPERTS))
        h2, route, counts = _router(x, norm_ffn[l].reshape(1, D), sc2, sh2, w_hi, w_lo, b_r.reshape(1, LANES))
        row_tok, dest, blk_e, n_used = _moe_dispatch(route.reshape(T, LANES), counts, T)
        xs = h2.reshape(T, D).at[row_tok].get(mode="promise_in_bounds")
        ys = _experts(l, blk_e, n_used, xs, moe_w_gate, moe_w_up, moe_w_down)
        y0 = ys.at[dest[:, 0]].get(mode="promise_in_bounds").reshape(B, S, D)
        y1 = ys.at[dest[:, 1]].get(mode="promise_in_bounds").reshape(B, S, D)
        x = _combine(x, g2, y0, y1, route)
    return x
```

```python
import functools

import numpy as np
import jax
import jax.numpy as jnp
from jax import lax
from jax.experimental import pallas as pl
from jax.experimental.pallas import tpu as pltpu
from jax.experimental.pallas import tpu_sc as plsc

F32 = jnp.float32
BF16 = jnp.bfloat16
HIGHEST = lax.Precision.HIGHEST

D_MODEL = 1024
DEPTH = 4
ATTN_HEADS = 8
HEAD_DIM = 64
ATTN_WIDTH = ATTN_HEADS * HEAD_DIM
IDX_HEADS = 8
IDX_DIM = 64
INDEX_TOPK = 256
Q_BLOCK = 128
ROPE_THETA = 10000.0
CONV_WIDTH = D_MODEL - ATTN_WIDTH
CONV_K = 3
ML_HEADS = 8
ML_QK_DIM = 64
ML_V_DIM = 128
N_GROUPS = 4
EXPERTS_PER_GROUP = 8
N_EXPERTS = N_GROUPS * EXPERTS_PER_GROUP
TOP_K = 2
D_EXPERT = 512
MOE_BLOCK = 256
NORM_EPS = 1e-6

LANES = 128
VMEM_LIMIT = 56 * 1024 * 1024
TOKEN_TILE = 256
ML_CHUNK = 256
ML_STEP_HEADS = 4
NEG_INF = float("-inf")


def _cparams(*sem):
    return pltpu.CompilerParams(dimension_semantics=sem, vmem_limit_bytes=VMEM_LIMIT)


def _dot(a, b):
    return jnp.dot(a, b, preferred_element_type=F32)


def _dot_t(a, b):
    return lax.dot_general(a, b, (((1,), (1,)), ((), ())), preferred_element_type=F32)


def _split_dot(a_f32, b_bf16):
    hi = a_f32.astype(BF16)
    lo = (a_f32 - hi.astype(F32)).astype(BF16)
    return _dot(hi, b_bf16) + _dot(lo, b_bf16)


def _ada_kernel(c_ref, w_ref, b_ref, o_ref):
    c = c_ref[...]
    ca = c * jax.nn.sigmoid(c)
    o_ref[0] = jnp.dot(ca, w_ref[0], precision=HIGHEST, preferred_element_type=F32) + b_ref[0]


def _ada_modulation(c, ada_w, ada_b):
    B, D = c.shape
    n_col = ada_w.shape[-1] // D
    return pl.pallas_call(
        _ada_kernel,
        out_shape=jax.ShapeDtypeStruct((DEPTH, B, n_col * D), F32),
        grid=(DEPTH, n_col),
        in_specs=[
            pl.BlockSpec((B, D), lambda l, j: (0, 0)),
            pl.BlockSpec((1, D, D), lambda l, j: (l, 0, j)),
            pl.BlockSpec((1, 1, D), lambda l, j: (l, 0, j)),
        ],
        out_specs=pl.BlockSpec((1, B, D), lambda l, j: (l, 0, j)),
        compiler_params=_cparams("parallel", "parallel"),
        name="ada_modulation",
    )(c, ada_w, ada_b.reshape(DEPTH, 1, n_col * D))


def _modulated_norm(x, gain, scale, shift):
    y = x * lax.rsqrt(jnp.mean(x * x, axis=-1, keepdims=True) + NORM_EPS)
    return y * gain * (1.0 + scale) + shift


def _rope(x, cos, sin_signed, first_half):
    w = x.shape[-1]
    partner = jnp.where(first_half, pltpu.roll(x, w - HEAD_DIM // 2, 1), pltpu.roll(x, HEAD_DIM // 2, 1))
    return x * cos + partner * sin_signed


HYB_COLS = 5 * 512 + 2 * LANES


def _hyb_in_kernel(x_ref, gain_ref, sc_ref, sh_ref, w_ref, cos_ref, sin_ref, qn_ref, kn_ref, grp_ref,
                   q_ref, iq_ref, bcu_ref, kv_ref, ik_ref, iw_ref):
    h = _modulated_norm(x_ref[0], gain_ref[...], sc_ref[0], sh_ref[0])
    p = _dot(h.astype(BF16), w_ref[...])
    cos = cos_ref[...]
    sin = sin_ref[...]
    lane = lax.broadcasted_iota(jnp.int32, (1, ATTN_WIDTH), 1)
    first_half = (lane % HEAD_DIM) < (HEAD_DIM // 2)
    fh128 = first_half[:, :LANES]
    lane128 = lane[:, :LANES]

    q = p[:, 0:512]
    ms = _split_dot(q * q, grp_ref[...]) * (1.0 / HEAD_DIM)
    q = q * lax.rsqrt(ms + NORM_EPS) * qn_ref[...]
    q_ref[0] = (_rope(q, cos, sin, first_half) * (HEAD_DIM ** -0.5)).astype(BF16)

    iq = p[:, 512:1024]
    iq_ref[0] = (_rope(iq, cos, sin, first_half) * (IDX_DIM ** -0.5)).astype(BF16)

    bcu_ref[0] = p[:, 1024:2560]

    kv = p[:, 2560:2688]
    is_k = lane128 < HEAD_DIM
    kk = jnp.where(is_k, kv, 0.0)
    ms_k = jnp.sum(kk * kk, axis=-1, keepdims=True) * (1.0 / HEAD_DIM)
    kn = kv * lax.rsqrt(ms_k + NORM_EPS) * kn_ref[...]
    kr = _rope(kn, cos[:, :LANES], sin[:, :LANES], fh128)
    kv_ref[0] = jnp.where(is_k, kr, kv).astype(BF16)

    sm = p[:, 2688:2816]
    ikr = _rope(sm, cos[:, :LANES], sin[:, :LANES], fh128)
    ik_ref[0] = jnp.where(is_k, ikr, 0.0).astype(BF16)
    iw_ref[0] = sm


def _hyb_in(x, gain, sc, sh, w_pad, cos_t, sin_t, qn_t, kn_t, grp):
    B, S, D = x.shape
    tm = TOKEN_TILE
    row = lambda b, j: (b, j, 0)
    per_b = lambda b, j: (b, 0, 0)
    const2 = lambda b, j: (0, 0)
    tab = lambda b, j: (j, 0)
    return pl.pallas_call(
        _hyb_in_kernel,
        out_shape=(
            jax.ShapeDtypeStruct((B, S, 512), BF16),
            jax.ShapeDtypeStruct((B, S, 512), BF16),
            jax.ShapeDtypeStruct((B, S, 1536), F32),
            jax.ShapeDtypeStruct((B, S, LANES), BF16),
            jax.ShapeDtypeStruct((B, S, LANES), BF16),
            jax.ShapeDtypeStruct((B, S, LANES), F32),
        ),
        grid=(B, S // tm),
        in_specs=[
            pl.BlockSpec((1, tm, D), row),
            pl.BlockSpec((1, D), const2),
            pl.BlockSpec((1, 1, D), per_b),
            pl.BlockSpec((1, 1, D), per_b),
            pl.BlockSpec((D, HYB_COLS), const2),
            pl.BlockSpec((tm, 512), tab),
            pl.BlockSpec((tm, 512), tab),
            pl.BlockSpec((1, 512), const2),
            pl.BlockSpec((1, LANES), const2),
            pl.BlockSpec((512, 512), const2),
        ],
        out_specs=(
            pl.BlockSpec((1, tm, 512), row),
            pl.BlockSpec((1, tm, 512), row),
            pl.BlockSpec((1, tm, 1536), row),
            pl.BlockSpec((1, tm, LANES), row),
            pl.BlockSpec((1, tm, LANES), row),
            pl.BlockSpec((1, tm, LANES), row),
        ),
        compiler_params=_cparams("parallel", "parallel"),
        name="hybrid_in_proj",
    )(x, gain, sc, sh, w_pad, cos_t, sin_t, qn_t, kn_t, grp)


def _count(mask):
    return jnp.sum(jnp.where(mask, 1.0, 0.0), axis=-1, keepdims=True)


DSA_KEY_SPANS = 4


def _dsa_kernel(q_ref, iq_ref, iw_ref, kv_ref, ik_ref, o_ref, sc_ref, bias_ref):
    S = kv_ref.shape[1]
    span = S // DSA_KEY_SPANS
    qb = pl.program_id(1)
    for v in range(DSA_KEY_SPANS):
        sk = (v + 1) * span

        @pl.when(qb // (span // Q_BLOCK) == v)
        def _():
            _dsa_body(q_ref, iq_ref, iw_ref, kv_ref.at[:, :sk, :], ik_ref.at[:, :sk, :], o_ref,
                      sc_ref.at[:, :sk], bias_ref.at[:, :sk])


def _dsa_body(q_ref, iq_ref, iw_ref, kv_ref, ik_ref, o_ref, sc_ref, bias_ref):
    S = kv_ref.shape[1]
    qb = pl.program_id(1)
    kf = float(INDEX_TOPK)

    ik = ik_ref[0][:, :IDX_DIM]
    iw = iw_ref[0]
    acc = jnp.zeros((Q_BLOCK, S), F32)
    for hd in range(IDX_HEADS):
        s_h = _dot_t(iq_ref[0][:, hd * IDX_DIM:(hd + 1) * IDX_DIM], ik)
        w_h = iw[:, IDX_DIM + hd:IDX_DIM + hd + 1] * (IDX_HEADS ** -0.5)
        acc = acc + jnp.maximum(s_h, 0.0) * w_h
    qpos = qb * Q_BLOCK + lax.broadcasted_iota(jnp.int32, (Q_BLOCK, 1), 0)
    kpos = lax.broadcasted_iota(jnp.int32, (1, S), 1)
    causal = kpos <= qpos
    sc_ref[...] = jnp.where(causal, acc, NEG_INF)

    @pl.when(qb * Q_BLOCK + Q_BLOCK <= INDEX_TOPK)
    def _():
        bias_ref[...] = jnp.where(causal, 0.0, NEG_INF)

    @pl.when(qb * Q_BLOCK + Q_BLOCK > INDEX_TOPK)
    def _():
        sc = sc_ref[...]
        row_max = jnp.max(sc, axis=-1, keepdims=True)
        row_min = jnp.min(jnp.where(causal, sc, jnp.inf), axis=-1, keepdims=True)
        top_tied = _count(sc >= row_max) >= kf

        def bisect(_, carry):
            lo, hi = carry
            mid = 0.5 * lo + 0.5 * hi
            ge = _count(sc_ref[...] >= mid) >= kf
            return jnp.where(ge, mid, lo), jnp.where(ge, hi, mid)

        lo, hi = lax.fori_loop(0, 18, bisect, (row_min, row_max))

        def refine_cond(carry):
            it, _, _, done = carry
            return jnp.logical_and(it < S, jnp.min(done) < 0.5)

        def refine(carry):
            it, hi, thr, done = carry
            scv = sc_ref[...]
            m = jnp.max(jnp.where(scv < hi, scv, NEG_INF), axis=-1, keepdims=True)
            hit = _count(scv >= m) >= kf
            fin = done > 0.5
            thr = jnp.where(fin, thr, m)
            hi = jnp.where(jnp.logical_or(fin, hit), hi, m)
            done = jnp.where(hit, 1.0, done)
            return it + 1, hi, thr, done

        done0 = jnp.where(top_tied, 1.0, 0.0)
        _, _, thr, _ = lax.while_loop(refine_cond, refine, (jnp.int32(0), hi, row_max, done0))

        gt = sc > thr
        eq = sc == thr
        need = kf - _count(gt)
        n_eq = _count(eq)
        bias_ref[...] = jnp.where(sc >= thr, 0.0, NEG_INF)

        @pl.when(jnp.max(n_eq - need) > 0.5)
        def _():
            r_i = lax.broadcasted_iota(jnp.int32, (LANES, LANES), 0)
            c_i = lax.broadcasted_iota(jnp.int32, (LANES, LANES), 1)
            upper = jnp.where(r_i <= c_i, 1.0, 0.0).astype(BF16)
            carry_cnt = jnp.zeros((Q_BLOCK, 1), F32)
            for j in range(S // LANES):
                blk = sc_ref[:, j * LANES:(j + 1) * LANES]
                eq_j = blk == thr
                eq_f = jnp.where(eq_j, 1.0, 0.0)
                rank = _dot(eq_f.astype(BF16), upper) + carry_cnt
                keep = jnp.logical_or(blk > thr, jnp.logical_and(eq_j, rank <= need))
                bias_ref[:, j * LANES:(j + 1) * LANES] = jnp.where(keep, 0.0, NEG_INF)
                carry_cnt = carry_cnt + jnp.sum(eq_f, axis=-1, keepdims=True)

    k = kv_ref[0][:, :HEAD_DIM]
    v = kv_ref[0][:, HEAD_DIM:]
    bias = bias_ref[...]
    for hd in range(ATTN_HEADS):
        logits = _dot_t(q_ref[0][:, hd * HEAD_DIM:(hd + 1) * HEAD_DIM], k) + bias
        mx = jnp.max(logits, axis=-1, keepdims=True)
        p = jnp.exp(logits - mx)
        den = jnp.sum(p, axis=-1, keepdims=True)
        o_ref[0, :, hd * HEAD_DIM:(hd + 1) * HEAD_DIM] = _dot(p.astype(BF16), v) / den


def _dsa_attention(q, iq, iw, kv, ik):
    B, S, _ = q.shape
    blk = lambda b, j: (b, j, 0)
    per_b = lambda b, j: (b, 0, 0)
    return pl.pallas_call(
        _dsa_kernel,
        out_shape=jax.ShapeDtypeStruct((B, S, ATTN_WIDTH), F32),
        grid=(B, S // Q_BLOCK),
        in_specs=[
            pl.BlockSpec((1, Q_BLOCK, ATTN_WIDTH), blk),
            pl.BlockSpec((1, Q_BLOCK, IDX_HEADS * IDX_DIM), blk),
            pl.BlockSpec((1, Q_BLOCK, LANES), blk),
            pl.BlockSpec((1, S, LANES), per_b),
            pl.BlockSpec((1, S, LANES), per_b),
        ],
        out_specs=pl.BlockSpec((1, Q_BLOCK, ATTN_WIDTH), blk),
        scratch_shapes=[pltpu.VMEM((Q_BLOCK, S), F32), pltpu.VMEM((Q_BLOCK, S), F32)],
        compiler_params=_cparams("parallel", "parallel"),
        name="dsa_attention",
    )(q, iq, iw, kv, ik)


def _hyb_out_kernel(ya_ref, bcu_ref, halo_ref, cw_ref, w_ref, x_ref, g_ref, o_ref):
    j = pl.program_id(1)
    tm = ya_ref.shape[1]
    bcu = bcu_ref[0]
    bg = bcu[:, 0:512]
    z = bcu[:, 512:1024] * bcu[:, 1024:1536]
    halo = halo_ref[0]
    zh = halo[:, 512:1024] * halo[:, 1024:1536]
    zh = jnp.where(j > 0, zh, 0.0)
    row = lax.broadcasted_iota(jnp.int32, (tm, 1), 0)
    z1 = jnp.where(row >= 1, pltpu.roll(z, 1, 0), zh[7:8, :])
    z2 = jnp.where(row >= 2, pltpu.roll(z, 2, 0), jnp.where(row == 1, zh[7:8, :], zh[6:7, :]))
    cw = cw_ref[...]
    y_conv = bg * (z2 * cw[0:1, :] + z1 * cw[1:2, :] + z * cw[2:3, :])
    y = _dot(ya_ref[0].astype(BF16), w_ref[0:512, :]) + _dot(y_conv.astype(BF16), w_ref[512:1024, :])
    o_ref[0] = x_ref[0] + g_ref[0] * y


def _hyb_out(y_attn, bcu, conv_w, w_out_bf, x, g1):
    B, S, D = x.shape
    tm = TOKEN_TILE
    row = lambda b, j: (b, j, 0)
    per_b = lambda b, j: (b, 0, 0)
    const2 = lambda b, j: (0, 0)
    halo = lambda b, j: (b, jnp.maximum(j * (tm // 8) - 1, 0), 0)
    return pl.pallas_call(
        _hyb_out_kernel,
        out_shape=jax.ShapeDtypeStruct((B, S, D), F32),
        grid=(B, S // tm),
        in_specs=[
            pl.BlockSpec((1, tm, 512), row),
            pl.BlockSpec((1, tm, 1536), row),
            pl.BlockSpec((1, 8, 1536), halo),
            pl.BlockSpec((CONV_K, CONV_WIDTH), const2),
            pl.BlockSpec((D, D), const2),
            pl.BlockSpec((1, tm, D), row),
            pl.BlockSpec((1, 1, D), per_b),
        ],
        out_specs=pl.BlockSpec((1, tm, D), row),
        compiler_params=_cparams("parallel", "parallel"),
        name="hybrid_out_proj",
    )(y_attn, bcu, bcu, conv_w, w_out_bf, x, g1)


ML_COLS = 512 + 512 + 1024 + 1024 + LANES


def _ml_in_kernel(x_ref, gain_ref, sc_ref, sh_ref, w_ref, bias_ref, q_ref, k_ref, v_ref, og_ref, gt_ref):
    h = _modulated_norm(x_ref[0], gain_ref[...], sc_ref[0], sh_ref[0])
    p = _dot(h.astype(BF16), w_ref[...])
    q_ref[0] = (p[:, 0:512] * (ML_QK_DIM ** -0.5)).astype(BF16)
    k_ref[0] = p[:, 512:1024].astype(BF16)
    v_ref[0] = p[:, 1024:2048].astype(BF16)
    og_ref[0] = p[:, 2048:3072]
    gt_ref[0] = p[:, 3072:3200] + bias_ref[...]


def _ml_in(x, gain, sc, sh, w_pad, gate_bias):
    B, S, D = x.shape
    tm = TOKEN_TILE
    row = lambda b, j: (b, j, 0)
    per_b = lambda b, j: (b, 0, 0)
    const2 = lambda b, j: (0, 0)
    return pl.pallas_call(
        _ml_in_kernel,
        out_shape=(
            jax.ShapeDtypeStruct((B, S, 512), BF16),
            jax.ShapeDtypeStruct((B, S, 512), BF16),
            jax.ShapeDtypeStruct((B, S, 1024), BF16),
            jax.ShapeDtypeStruct((B, S, 1024), F32),
            jax.ShapeDtypeStruct((B, S, LANES), F32),
        ),
        grid=(B, S // tm),
        in_specs=[
            pl.BlockSpec((1, tm, D), row),
            pl.BlockSpec((1, D), const2),
            pl.BlockSpec((1, 1, D), per_b),
            pl.BlockSpec((1, 1, D), per_b),
            pl.BlockSpec((D, ML_COLS), const2),
            pl.BlockSpec((1, LANES), const2),
        ],
        out_specs=(
            pl.BlockSpec((1, tm, 512), row),
            pl.BlockSpec((1, tm, 512), row),
            pl.BlockSpec((1, tm, 1024), row),
            pl.BlockSpec((1, tm, 1024), row),
            pl.BlockSpec((1, tm, LANES), row),
        ),
        compiler_params=_cparams("parallel", "parallel"),
        name="mlstm_in_proj",
    )(x, gain, sc, sh, w_pad, gate_bias)


def _log_sigmoid(f):
    return jnp.minimum(f, 0.0) - jnp.log1p(jnp.exp(-jnp.abs(f)))


def _split3(x):
    a = x.astype(BF16)
    r = x - a.astype(F32)
    b = r.astype(BF16)
    c = (r - b.astype(F32)).astype(BF16)
    return a, b, c


def _mlstm_kernel(q_ref, k_ref, v_ref, grow_ref, gcol_ref, gain_ref, o_ref, c_ref, n_ref, m_ref):
    L = ML_CHUNK
    HP = ML_STEP_HEADS
    S = q_ref.shape[1]
    c_ref[...] = jnp.zeros_like(c_ref)
    n_ref[...] = jnp.zeros_like(n_ref)
    m_ref[...] = jnp.zeros_like(m_ref)

    def chunk(c, carry):
        r0 = pl.multiple_of(c * L, L)
        r_i = lax.broadcasted_iota(jnp.int32, (L, L), 0)
        c_i = lax.broadcasted_iota(jnp.int32, (L, L), 1)
        tril = c_i <= r_i
        lower = jnp.where(tril, 1.0, 0.0).astype(BF16)
        upper = jnp.where(r_i <= c_i, 1.0, 0.0).astype(BF16)
        rows = grow_ref[0, 0, c]
        cols = gcol_ref[0, 0, c]
        b_rows = sum(_dot(p, upper) for p in _split3(_log_sigmoid(rows)))
        b_cols = sum(_dot(lower, p) for p in _split3(_log_sigmoid(cols)))
        for hh in range(HP):
            q = q_ref[0, pl.ds(r0, L), hh * ML_QK_DIM:(hh + 1) * ML_QK_DIM]
            k = k_ref[0, pl.ds(r0, L), hh * ML_QK_DIM:(hh + 1) * ML_QK_DIM]
            v = v_ref[0, pl.ds(r0, L), hh * ML_V_DIM:(hh + 1) * ML_V_DIM]
            i_row = rows[hh:hh + 1, :]
            i_col = cols[:, hh:hh + 1]
            b_row = b_rows[HP + hh:HP + hh + 1, :]
            b_col = b_cols[:, HP + hh:HP + hh + 1]
            b_last = b_row[:, L - 1:L]
            m_prev = m_ref[hh]
            ct = c_ref[hh]
            n_row = n_ref[hh]

            dmat = jnp.where(tril, b_col - b_row + i_row, NEG_INF)
            inter = b_col + m_prev
            m_t = jnp.maximum(inter, jnp.max(dmat, axis=-1, keepdims=True))
            w_intra = jnp.exp(dmat - m_t)
            w_inter = jnp.exp(inter - m_t)
            intra = w_intra * _dot_t(q, k)
            num = w_inter * _dot(q, ct.astype(BF16)) + _dot(intra.astype(BF16), v)
            qn = jnp.sum(q.astype(F32) * n_row, axis=-1, keepdims=True)
            den = w_inter * qn + jnp.sum(intra, axis=-1, keepdims=True)
            hc = num / jnp.maximum(jnp.abs(den), jnp.exp(-m_t))
            y = hc * lax.rsqrt(jnp.mean(hc * hc, axis=-1, keepdims=True) + NORM_EPS)
            o_ref[0, pl.ds(r0, L), hh * ML_V_DIM:(hh + 1) * ML_V_DIM] = (
                y * gain_ref[:, hh * ML_V_DIM:(hh + 1) * ML_V_DIM])

            g_row = b_last - b_row + i_row
            g_col = b_last - b_col + i_col
            m_new = jnp.maximum(b_last + m_prev, jnp.max(g_row, axis=-1, keepdims=True))
            decay = jnp.exp(b_last + m_prev - m_new)
            kw = k.astype(F32) * jnp.exp(g_col - m_new)
            c_ref[hh] = decay * ct + lax.dot_general(
                kw.astype(BF16), v, (((0,), (0,)), ((), ())), preferred_element_type=F32)
            n_ref[hh] = decay * n_row + jnp.sum(kw, axis=0, keepdims=True)
            m_ref[hh] = m_new
        return carry

    lax.fori_loop(0, S // L, chunk, 0)


def _mlstm(q, k, v, g_rows, g_cols, out_gain):
    B, S, _ = q.shape
    nc = S // ML_CHUNK
    hp = ML_STEP_HEADS
    return pl.pallas_call(
        _mlstm_kernel,
        out_shape=jax.ShapeDtypeStruct((B, S, ML_HEADS * ML_V_DIM), F32),
        grid=(B, ML_HEADS // hp),
        in_specs=[
            pl.BlockSpec((1, S, hp * ML_QK_DIM), lambda b, p: (b, 0, p)),
            pl.BlockSpec((1, S, hp * ML_QK_DIM), lambda b, p: (b, 0, p)),
            pl.BlockSpec((1, S, hp * ML_V_DIM), lambda b, p: (b, 0, p)),
            pl.BlockSpec((1, 1, nc, 2 * hp, ML_CHUNK), lambda b, p: (b, p, 0, 0, 0)),
            pl.BlockSpec((1, 1, nc, ML_CHUNK, LANES), lambda b, p: (b, p, 0, 0, 0)),
            pl.BlockSpec((1, hp * ML_V_DIM), lambda b, p: (0, p)),
        ],
        out_specs=pl.BlockSpec((1, S, hp * ML_V_DIM), lambda b, p: (b, 0, p)),
        scratch_shapes=[
            pltpu.VMEM((hp, ML_QK_DIM, ML_V_DIM), F32),
            pltpu.VMEM((hp, 1, ML_QK_DIM), F32),
            pltpu.VMEM((hp, 1, 1), F32),
        ],
        compiler_params=_cparams("parallel", "parallel"),
        name="mlstm_chunkwise",
    )(q, k, v, g_rows, g_cols, out_gain)


def _ml_out_kernel(hh_ref, og_ref, w_ref, x_ref, g_ref, o_ref):
    a = jax.nn.sigmoid(og_ref[0]) * hh_ref[0]
    o_ref[0] = x_ref[0] + g_ref[0] * _dot(a.astype(BF16), w_ref[...])


def _ml_out(hh, og, w_out_bf, x, g1):
    B, S, D = x.shape
    tm = TOKEN_TILE
    row = lambda b, j: (b, j, 0)
    per_b = lambda b, j: (b, 0, 0)
    const2 = lambda b, j: (0, 0)
    return pl.pallas_call(
        _ml_out_kernel,
        out_shape=jax.ShapeDtypeStruct((B, S, D), F32),
        grid=(B, S // tm),
        in_specs=[
            pl.BlockSpec((1, tm, D), row),
            pl.BlockSpec((1, tm, D), row),
            pl.BlockSpec((D, D), const2),
            pl.BlockSpec((1, tm, D), row),
            pl.BlockSpec((1, 1, D), per_b),
        ],
        out_specs=pl.BlockSpec((1, tm, D), row),
        compiler_params=_cparams("parallel", "parallel"),
        name="mlstm_out_proj",
    )(hh, og, w_out_bf, x, g1)


def _first_argmax(x, lane, width):
    mx = jnp.max(x, axis=-1, keepdims=True)
    idx = jnp.min(jnp.where(x == mx, lane, width), axis=-1, keepdims=True)
    return mx, idx


def _router_kernel(x_ref, gain_ref, sc_ref, sh_ref, whi_ref, wlo_ref, b_ref, h_ref, r_ref, cnt_ref, run_ref):
    tm = x_ref.shape[1]

    @pl.when(jnp.logical_and(pl.program_id(0) == 0, pl.program_id(1) == 0))
    def _():
        run_ref[...] = jnp.zeros_like(run_ref)

    h = _modulated_norm(x_ref[0], gain_ref[...], sc_ref[0], sh_ref[0])
    h_hi = h.astype(BF16)
    h_ref[0] = h
    h_lo = (h - h_hi.astype(F32)).astype(BF16)
    whi = whi_ref[...]
    logits = _dot(h_hi, whi) + _dot(h_lo, whi) + _dot(h_hi, wlo_ref[...]) + b_ref[...]
    lane = lax.broadcasted_iota(jnp.int32, (1, LANES), 1)
    lg = jnp.where(lane < N_GROUPS, logits, NEG_INF)
    g_max, g_sel = _first_argmax(lg, lane, LANES)
    pg = 1.0 / jnp.sum(jnp.exp(lg - g_max), axis=-1, keepdims=True)
    e_lane = lane - N_GROUPS
    in_grp = jnp.logical_and(e_lane >= g_sel * EXPERTS_PER_GROUP, e_lane < (g_sel + 1) * EXPERTS_PER_GROUP)
    le = jnp.where(in_grp, logits, NEG_INF)
    v1, i1 = _first_argmax(le, lane, LANES)
    le2 = jnp.where(lane == i1, NEG_INF, le)
    v2, i2 = _first_argmax(le2, lane, LANES)
    e2 = jnp.exp(v2 - v1)
    w1 = pg / (1.0 + e2)
    w2 = pg * e2 / (1.0 + e2)
    e1 = i1 - N_GROUPS
    e2 = i2 - N_GROUPS
    hot1 = lane == e1
    hot2 = lane == e2
    onehot = jnp.where(jnp.logical_or(hot1, hot2), 1.0, 0.0)
    r_i = lax.broadcasted_iota(jnp.int32, (tm, tm), 0)
    c_i = lax.broadcasted_iota(jnp.int32, (tm, tm), 1)
    before = jnp.where(c_i < r_i, 1.0, 0.0).astype(BF16)
    seen = _dot(before, onehot.astype(BF16)) + run_ref[...]
    rank1 = jnp.sum(jnp.where(hot1, seen, 0.0), axis=-1, keepdims=True)
    rank2 = jnp.sum(jnp.where(hot2, seen, 0.0), axis=-1, keepdims=True)
    run_ref[...] = run_ref[...] + jnp.sum(onehot, axis=0, keepdims=True)
    cnt_ref[...] = run_ref[...]

    out = jnp.where(lane == 0, e1.astype(F32), 0.0)
    out = jnp.where(lane == 1, e2.astype(F32), out)
    out = jnp.where(lane == 2, w1, out)
    out = jnp.where(lane == 3, w2, out)
    out = jnp.where(lane == 4, rank1, out)
    out = jnp.where(lane == 5, rank2, out)
    r_ref[0] = out


def _router(x, gain, sc, sh, w_hi, w_lo, bias):
    B, S, D = x.shape
    tm = TOKEN_TILE
    row = lambda b, j: (b, j, 0)
    per_b = lambda b, j: (b, 0, 0)
    const2 = lambda b, j: (0, 0)
    return pl.pallas_call(
        _router_kernel,
        out_shape=(
            jax.ShapeDtypeStruct((B, S, D), F32),
            jax.ShapeDtypeStruct((B, S, LANES), F32),
            jax.ShapeDtypeStruct((1, LANES), F32),
        ),
        grid=(B, S // tm),
        in_specs=[
            pl.BlockSpec((1, tm, D), row),
            pl.BlockSpec((1, D), const2),
            pl.BlockSpec((1, 1, D), per_b),
            pl.BlockSpec((1, 1, D), per_b),
            pl.BlockSpec((D, LANES), const2),
            pl.BlockSpec((D, LANES), const2),
            pl.BlockSpec((1, LANES), const2),
        ],
        out_specs=(
            pl.BlockSpec((1, tm, D), row),
            pl.BlockSpec((1, tm, LANES), row),
            pl.BlockSpec((1, LANES), const2),
        ),
        scratch_shapes=[pltpu.VMEM((1, LANES), F32)],
        compiler_params=_cparams("arbitrary", "arbitrary"),
        name="moe_router",
    )(x, gain, sc, sh, w_hi, w_lo, bias)


def _experts_kernel(blk_e_ref, n_used_ref, x_ref, wg_ref, wu_ref, wd_ref, o_ref, wg_s, wu_s, wd_s):
    i = pl.program_id(0)
    used = i < n_used_ref[0]
    new_expert = jnp.logical_or(i == 0, blk_e_ref[i] != blk_e_ref[jnp.maximum(i - 1, 0)])

    @pl.when(jnp.logical_and(used, new_expert))
    def _():
        wg_s[...] = wg_ref[0, 0].astype(BF16)
        wu_s[...] = wu_ref[0, 0].astype(BF16)
        wd_s[...] = wd_ref[0, 0].astype(BF16)

    @pl.when(used)
    def _():
        x = x_ref[...].astype(BF16)
        a = _dot(x, wg_s[...])
        u = _dot(x, wu_s[...])
        act = a * jax.nn.sigmoid(a) * u
        o_ref[...] = _dot(act.astype(BF16), wd_s[...])

    @pl.when(i >= n_used_ref[0])
    def _():
        o_ref[...] = jnp.zeros_like(o_ref)


def _experts(layer, blk_e, n_used, xs, w_gate, w_up, w_down):
    R, D = xs.shape
    n_blk = R // MOE_BLOCK
    grid_spec = pltpu.PrefetchScalarGridSpec(
        num_scalar_prefetch=2,
        grid=(n_blk,),
        in_specs=[
            pl.BlockSpec((MOE_BLOCK, D), lambda i, be, nu: (i, 0)),
            pl.BlockSpec((1, 1, D, D_EXPERT), lambda i, be, nu: (layer, be[i], 0, 0)),
            pl.BlockSpec((1, 1, D, D_EXPERT), lambda i, be, nu: (layer, be[i], 0, 0)),
            pl.BlockSpec((1, 1, D_EXPERT, D), lambda i, be, nu: (layer, be[i], 0, 0)),
        ],
        out_specs=pl.BlockSpec((MOE_BLOCK, D), lambda i, be, nu: (i, 0)),
        scratch_shapes=[
            pltpu.VMEM((D, D_EXPERT), BF16),
            pltpu.VMEM((D, D_EXPERT), BF16),
            pltpu.VMEM((D_EXPERT, D), BF16),
        ],
    )
    return pl.pallas_call(
        _experts_kernel,
        out_shape=jax.ShapeDtypeStruct((R, D), F32),
        grid_spec=grid_spec,
        compiler_params=_cparams("arbitrary"),
        name="moe_experts",
    )(blk_e, n_used, xs, w_gate, w_up, w_down)


def _combine_kernel(x_ref, g_ref, y0_ref, y1_ref, r_ref, o_ref):
    r = r_ref[0]
    y = y0_ref[0, 0] * r[:, 2:3] + y1_ref[0, 0] * r[:, 3:4]
    o_ref[0] = x_ref[0] + g_ref[0] * y


def _combine(x, g2, y01, route):
    B, S, D = x.shape
    tm = TOKEN_TILE
    row = lambda b, j: (b, j, 0)
    per_b = lambda b, j: (b, 0, 0)
    return pl.pallas_call(
        _combine_kernel,
        out_shape=jax.ShapeDtypeStruct((B, S, D), F32),
        grid=(B, S // tm),
        in_specs=[
            pl.BlockSpec((1, tm, D), row),
            pl.BlockSpec((1, 1, D), per_b),
            pl.BlockSpec((1, 1, tm, D), lambda b, j: (0, b, j, 0)),
            pl.BlockSpec((1, 1, tm, D), lambda b, j: (1, b, j, 0)),
            pl.BlockSpec((1, tm, LANES), row),
        ],
        out_specs=pl.BlockSpec((1, tm, D), row),
        compiler_params=_cparams("parallel", "parallel"),
        name="moe_combine",
    )(x, g2, y01, y01, route)


SC_CORES = 2
SC_SUBCORES = 16
SC_WORKERS = SC_CORES * SC_SUBCORES
SC_CHUNK = 32


def _sc_mesh():
    return plsc.VectorSubcoreMesh(core_axis_name="c", subcore_axis_name="s",
                                  num_cores=SC_CORES, num_subcores=SC_SUBCORES)


def _sc_scatter_rows(src, idx, n_out):
    T, W = src.shape
    per_w = T // SC_WORKERS
    nch = per_w // SC_CHUNK
    idx4 = idx.reshape(TOP_K, SC_WORKERS, nch, SC_CHUNK)

    @functools.partial(
        pl.kernel, mesh=_sc_mesh(),
        out_type=jax.ShapeDtypeStruct((n_out, W), src.dtype),
        scratch_types=[pltpu.VMEM((TOP_K, nch, SC_CHUNK), jnp.int32), pltpu.VMEM((SC_CHUNK, W), src.dtype)],
        name="sc_scatter_rows",
    )
    def body(src_hbm, idx_hbm, out_hbm, idx_v, rows_v):
        wid = lax.axis_index("s") * SC_CORES + lax.axis_index("c")
        for s in range(TOP_K):
            pltpu.sync_copy(idx_hbm.at[s, wid], idx_v.at[s])

        @pl.loop(0, nch)
        def _(i):
            pltpu.sync_copy(src_hbm.at[pl.ds(wid * per_w + i * SC_CHUNK, SC_CHUNK)], rows_v)
            for s in range(TOP_K):
                pltpu.sync_copy(rows_v, out_hbm.at[idx_v.at[s, i]])

    return body(src, idx4)


def _sc_gather_rows(table, idx):
    N = idx.shape[0]
    W = table.shape[1]
    per_w = N // SC_WORKERS
    nch = per_w // SC_CHUNK
    idx3 = idx.reshape(SC_WORKERS, nch, SC_CHUNK)

    @functools.partial(
        pl.kernel, mesh=_sc_mesh(),
        out_type=jax.ShapeDtypeStruct((N, W), table.dtype),
        scratch_types=[pltpu.VMEM((nch, SC_CHUNK), jnp.int32), pltpu.VMEM((SC_CHUNK, W), table.dtype)],
        name="sc_gather_rows",
    )
    def body(table_hbm, idx_hbm, out_hbm, idx_v, rows_v):
        wid = lax.axis_index("s") * SC_CORES + lax.axis_index("c")
        pltpu.sync_copy(idx_hbm.at[wid], idx_v)

        @pl.loop(0, nch)
        def _(i):
            pltpu.sync_copy(table_hbm.at[idx_v.at[i]], rows_v)
            pltpu.sync_copy(rows_v, out_hbm.at[pl.ds(wid * per_w + i * SC_CHUNK, SC_CHUNK)])

    return body(table, idx3)


def _moe_dispatch(route, counts, T):
    A = T * TOP_K
    counts = counts[0, :N_EXPERTS].astype(jnp.int32)
    blocks_per = (counts + MOE_BLOCK - 1) // MOE_BLOCK
    block_end = jnp.cumsum(blocks_per)
    block_start = block_end - blocks_per
    expert = route[:, :TOP_K].astype(jnp.int32)
    rank = route[:, 4:4 + TOP_K].astype(jnp.int32)
    onehot = expert[:, :, None] == jnp.arange(N_EXPERTS, dtype=jnp.int32)
    start = jnp.sum(jnp.where(onehot, block_start, 0), axis=-1)
    dest = (start * MOE_BLOCK + rank).T
    n_blk = -(-A // MOE_BLOCK) + N_EXPERTS
    blk = jnp.arange(n_blk, dtype=jnp.int32)
    blk_e = jnp.minimum(jnp.sum(blk[:, None] >= block_end[None, :], axis=-1), N_EXPERTS - 1)
    return dest, n_blk * MOE_BLOCK, blk_e.astype(jnp.int32), block_end[-1:].astype(jnp.int32)


def _rope_tables(S):
    inv = 1.0 / (ROPE_THETA ** (jnp.arange(0, HEAD_DIM, 2, dtype=F32) / HEAD_DIM))
    ang = jnp.arange(S, dtype=F32)[:, None] * inv[None, :]
    cos, sin = jnp.cos(ang), jnp.sin(ang)
    cos_h = jnp.concatenate([cos, cos], axis=-1)
    sin_h = jnp.concatenate([-sin, sin], axis=-1)
    return jnp.tile(cos_h, (1, ATTN_HEADS)), jnp.tile(sin_h, (1, ATTN_HEADS))


def _pad_cols(w, width):
    return jnp.pad(w, ((0, 0), (0, width - w.shape[1])))


def kernel(x, c, ada_w, ada_b, norm_mix, norm_ffn, hy_w_in, hy_q_norm, hy_k_norm, hy_conv_w, hy_w_out, ml_w_in, ml_b_gates, ml_out_norm, ml_w_out, moe_w_group, moe_b_group, moe_w_expert, moe_b_expert, moe_w_gate, moe_w_up, moe_w_down):
    B, S, D = x.shape
    T = B * S
    cos_t, sin_t = _rope_tables(S)
    mod = _ada_modulation(c, ada_w, ada_b).reshape(DEPTH, B, 6, 1, D)
    r_i = np.arange(ATTN_WIDTH)
    grp = jnp.asarray((r_i[:, None] // HEAD_DIM) == (r_i[None, :] // HEAD_DIM), dtype=BF16)

    for l in range(DEPTH):
        sh1, sc1, g1, sh2, sc2, g2 = [mod[l, :, i] for i in range(6)]
        gain1 = norm_mix[l].reshape(1, D)
        j = l // 2
        if l % 2 == 0:
            w = hy_w_in[j]
            o = np.cumsum((0,) + (ATTN_WIDTH, HEAD_DIM, HEAD_DIM, IDX_HEADS * IDX_DIM, IDX_DIM, IDX_HEADS,
                                  CONV_WIDTH, CONV_WIDTH, CONV_WIDTH))
            wq, wk, wv, wiq, wik, wiw, wbg, wcg, wu = [w[:, o[i]:o[i + 1]] for i in range(9)]
            w_pad = jnp.concatenate(
                [wq, wiq, wbg, wcg, wu, wk, wv, _pad_cols(jnp.concatenate([wik, wiw], axis=1), LANES)],
                axis=1).astype(BF16)
            qn_t = jnp.tile(hy_q_norm[j], ATTN_HEADS).reshape(1, ATTN_WIDTH)
            kn_t = jnp.tile(hy_k_norm[j], LANES // HEAD_DIM).reshape(1, LANES)
            q, iq, bcu, kv, ik, iw = _hyb_in(x, gain1, sc1, sh1, w_pad, cos_t, sin_t, qn_t, kn_t, grp)
            y_attn = _dsa_attention(q, iq, iw, kv, ik)
            x = _hyb_out(y_attn, bcu, hy_conv_w[j], hy_w_out[j].astype(BF16), x, g1)
        else:
            w = ml_w_in[j]
            hq = ML_HEADS * ML_QK_DIM
            hv = ML_HEADS * ML_V_DIM
            wq, wk, wv = w[:, :hq], w[:, hq:2 * hq], w[:, 2 * hq:2 * hq + hv]
            wg = w[:, 2 * hq + hv:2 * hq + hv + 2 * ML_HEADS]
            wo = w[:, 2 * hq + hv + 2 * ML_HEADS:]
            w_pad = jnp.concatenate([wq, wk, wv, wo, _pad_cols(wg, LANES)], axis=1).astype(BF16)
            gate_bias = jnp.pad(ml_b_gates[j], (0, LANES - 2 * ML_HEADS)).reshape(1, LANES)
            q, k, v, og, gates = _ml_in(x, gain1, sc1, sh1, w_pad, gate_bias)
            nc = S // ML_CHUNK
            hp = ML_STEP_HEADS
            groups = ML_HEADS // hp
            gi = gates[:, :, :ML_HEADS].reshape(B, S, groups, hp)
            gf = gates[:, :, ML_HEADS:2 * ML_HEADS].reshape(B, S, groups, hp)
            gp = jnp.concatenate([gi, gf], axis=-1)
            gp = jnp.transpose(gp, (0, 2, 1, 3)).reshape(B, groups, nc, ML_CHUNK, 2 * hp)
            g_cols = jnp.pad(gp, ((0, 0),) * 4 + ((0, LANES - 2 * hp),))
            g_rows = jnp.swapaxes(gp, -1, -2)
            hh = _mlstm(q, k, v, g_rows, g_cols, ml_out_norm[j].reshape(1, hv))
            x = _ml_out(hh, og, ml_w_out[j].astype(BF16), x, g1)

        w_r = _pad_cols(jnp.concatenate([moe_w_group[l], moe_w_expert[l]], axis=1), LANES)
        w_hi = w_r.astype(BF16)
        w_lo = (w_r - w_hi.astype(F32)).astype(BF16)
        b_r = jnp.pad(jnp.concatenate([moe_b_group[l], moe_b_expert[l]]), (0, LANES - N_GROUPS - N_EXPERTS))
        h2, route, counts = _router(x, norm_ffn[l].reshape(1, D), sc2, sh2, w_hi, w_lo, b_r.reshape(1, LANES))
        dest, n_rows, blk_e, n_used = _moe_dispatch(route.reshape(T, LANES), counts, T)
        xs = _sc_scatter_rows(h2.reshape(T, D), dest, n_rows)
        ys = _experts(l, blk_e, n_used, xs, moe_w_gate, moe_w_up, moe_w_down)
        y01 = _sc_gather_rows(ys, dest.reshape(TOP_K * T)).reshape(TOP_K, B, S, D)
        x = _combine(x, g2, y01, route)
    return x
```

```python
import functools

import numpy as np
import jax
import jax.numpy as jnp
from jax import lax
from jax.experimental import pallas as pl
from jax.experimental.pallas import tpu as pltpu
from jax.experimental.pallas import tpu_sc as plsc

F32 = jnp.float32
BF16 = jnp.bfloat16
HIGHEST = lax.Precision.HIGHEST

D_MODEL = 1024
DEPTH = 4
ATTN_HEADS = 8
HEAD_DIM = 64
ATTN_WIDTH = ATTN_HEADS * HEAD_DIM
IDX_HEADS = 8
IDX_DIM = 64
INDEX_TOPK = 256
Q_BLOCK = 128
ROPE_THETA = 10000.0
CONV_WIDTH = D_MODEL - ATTN_WIDTH
CONV_K = 3
ML_HEADS = 8
ML_QK_DIM = 64
ML_V_DIM = 128
N_GROUPS = 4
EXPERTS_PER_GROUP = 8
N_EXPERTS = N_GROUPS * EXPERTS_PER_GROUP
TOP_K = 2
D_EXPERT = 512
MOE_BLOCK = 256
NORM_EPS = 1e-6

LANES = 128
VMEM_LIMIT = 56 * 1024 * 1024
TOKEN_TILE = 256
ML_CHUNK = 256
ML_STEP_HEADS = 4
NEG_INF = float("-inf")


def _cparams(*sem):
    return pltpu.CompilerParams(dimension_semantics=sem, vmem_limit_bytes=VMEM_LIMIT)


def _dot(a, b):
    return jnp.dot(a, b, preferred_element_type=F32)


def _dot_t(a, b):
    return lax.dot_general(a, b, (((1,), (1,)), ((), ())), preferred_element_type=F32)


def _split_dot(a_f32, b_bf16):
    hi = a_f32.astype(BF16)
    lo = (a_f32 - hi.astype(F32)).astype(BF16)
    return _dot(hi, b_bf16) + _dot(lo, b_bf16)


def _ada_kernel(c_ref, w_ref, b_ref, o_ref):
    c = c_ref[...]
    ca = c * jax.nn.sigmoid(c)
    o_ref[0] = jnp.dot(ca, w_ref[0], precision=HIGHEST, preferred_element_type=F32) + b_ref[0]


def _ada_modulation(c, ada_w, ada_b):
    B, D = c.shape
    n_col = ada_w.shape[-1] // D
    return pl.pallas_call(
        _ada_kernel,
        out_shape=jax.ShapeDtypeStruct((DEPTH, B, n_col * D), F32),
        grid=(DEPTH, n_col),
        in_specs=[
            pl.BlockSpec((B, D), lambda l, j: (0, 0)),
            pl.BlockSpec((1, D, D), lambda l, j: (l, 0, j)),
            pl.BlockSpec((1, 1, D), lambda l, j: (l, 0, j)),
        ],
        out_specs=pl.BlockSpec((1, B, D), lambda l, j: (l, 0, j)),
        compiler_params=_cparams("parallel", "parallel"),
        name="ada_modulation",
    )(c, ada_w, ada_b.reshape(DEPTH, 1, n_col * D))


def _modulated_norm(x, gain, scale, shift):
    y = x * lax.rsqrt(jnp.mean(x * x, axis=-1, keepdims=True) + NORM_EPS)
    return y * gain * (1.0 + scale) + shift


def _rope(x, cos, sin_signed, first_half):
    w = x.shape[-1]
    partner = jnp.where(first_half, pltpu.roll(x, w - HEAD_DIM // 2, 1), pltpu.roll(x, HEAD_DIM // 2, 1))
    return x * cos + partner * sin_signed


HYB_COLS = 5 * 512 + 2 * LANES


def _hyb_in_kernel(x_ref, gain_ref, sc_ref, sh_ref, w_ref, cos_ref, sin_ref, qn_ref, kn_ref, grp_ref,
                   q_ref, iq_ref, bcu_ref, kv_ref, ik_ref, iw_ref):
    h = _modulated_norm(x_ref[0], gain_ref[...], sc_ref[0], sh_ref[0])
    p = _dot(h.astype(BF16), w_ref[...])
    cos = cos_ref[...]
    sin = sin_ref[...]
    lane = lax.broadcasted_iota(jnp.int32, (1, ATTN_WIDTH), 1)
    first_half = (lane % HEAD_DIM) < (HEAD_DIM // 2)
    fh128 = first_half[:, :LANES]
    lane128 = lane[:, :LANES]

    q = p[:, 0:512]
    ms = _split_dot(q * q, grp_ref[...]) * (1.0 / HEAD_DIM)
    q = q * lax.rsqrt(ms + NORM_EPS) * qn_ref[...]
    q_ref[0] = (_rope(q, cos, sin, first_half) * (HEAD_DIM ** -0.5)).astype(BF16)

    iq = p[:, 512:1024]
    iq_ref[0] = (_rope(iq, cos, sin, first_half) * (IDX_DIM ** -0.5)).astype(BF16)

    bcu_ref[0] = p[:, 1024:2560]

    kv = p[:, 2560:2688]
    is_k = lane128 < HEAD_DIM
    kk = jnp.where(is_k, kv, 0.0)
    ms_k = jnp.sum(kk * kk, axis=-1, keepdims=True) * (1.0 / HEAD_DIM)
    kn = kv * lax.rsqrt(ms_k + NORM_EPS) * kn_ref[...]
    kr = _rope(kn, cos[:, :LANES], sin[:, :LANES], fh128)
    kv_ref[0] = jnp.where(is_k, kr, kv).astype(BF16)

    sm = p[:, 2688:2816]
    ikr = _rope(sm, cos[:, :LANES], sin[:, :LANES], fh128)
    ik_ref[0] = jnp.where(is_k, ikr, 0.0).astype(BF16)
    iw_ref[0] = sm


def _hyb_in(x, gain, sc, sh, w_pad, cos_t, sin_t, qn_t, kn_t, grp):
    B, S, D = x.shape
    tm = TOKEN_TILE
    row = lambda b, j: (b, j, 0)
    per_b = lambda b, j: (b, 0, 0)
    const2 = lambda b, j: (0, 0)
    tab = lambda b, j: (j, 0)
    return pl.pallas_call(
        _hyb_in_kernel,
        out_shape=(
            jax.ShapeDtypeStruct((B, S, 512), BF16),
            jax.ShapeDtypeStruct((B, S, 512), BF16),
            jax.ShapeDtypeStruct((B, S, 1536), F32),
            jax.ShapeDtypeStruct((B, S, LANES), BF16),
            jax.ShapeDtypeStruct((B, S, LANES), BF16),
            jax.ShapeDtypeStruct((B, S, LANES), F32),
        ),
        grid=(B, S // tm),
        in_specs=[
            pl.BlockSpec((1, tm, D), row),
            pl.BlockSpec((1, D), const2),
            pl.BlockSpec((1, 1, D), per_b),
            pl.BlockSpec((1, 1, D), per_b),
            pl.BlockSpec((D, HYB_COLS), const2),
            pl.BlockSpec((tm, 512), tab),
            pl.BlockSpec((tm, 512), tab),
            pl.BlockSpec((1, 512), const2),
            pl.BlockSpec((1, LANES), const2),
            pl.BlockSpec((512, 512), const2),
        ],
        out_specs=(
            pl.BlockSpec((1, tm, 512), row),
            pl.BlockSpec((1, tm, 512), row),
            pl.BlockSpec((1, tm, 1536), row),
            pl.BlockSpec((1, tm, LANES), row),
            pl.BlockSpec((1, tm, LANES), row),
            pl.BlockSpec((1, tm, LANES), row),
        ),
        compiler_params=_cparams("parallel", "parallel"),
        name="hybrid_in_proj",
    )(x, gain, sc, sh, w_pad, cos_t, sin_t, qn_t, kn_t, grp)


def _count(mask):
    return jnp.sum(jnp.where(mask, 1.0, 0.0), axis=-1, keepdims=True)


DSA_KEY_SPANS = 8


def _dsa_kernel(q_ref, iq_ref, iw_ref, kv_ref, ik_ref, o_ref, sc_ref, bias_ref):
    S = kv_ref.shape[1]
    span = S // DSA_KEY_SPANS
    qb = pl.program_id(1)
    for v in range(DSA_KEY_SPANS):
        sk = (v + 1) * span

        @pl.when(qb // (span // Q_BLOCK) == v)
        def _():
            _dsa_body(q_ref, iq_ref, iw_ref, kv_ref.at[:, :sk, :], ik_ref.at[:, :sk, :], o_ref,
                      sc_ref.at[:, :sk], bias_ref.at[:, :sk])


def _dsa_body(q_ref, iq_ref, iw_ref, kv_ref, ik_ref, o_ref, sc_ref, bias_ref):
    S = kv_ref.shape[1]
    qb = pl.program_id(1)
    kf = float(INDEX_TOPK)

    ik = ik_ref[0][:, :IDX_DIM]
    iw = iw_ref[0]
    acc = jnp.zeros((Q_BLOCK, S), F32)
    for hd in range(IDX_HEADS):
        s_h = _dot_t(iq_ref[0][:, hd * IDX_DIM:(hd + 1) * IDX_DIM], ik)
        w_h = iw[:, IDX_DIM + hd:IDX_DIM + hd + 1] * (IDX_HEADS ** -0.5)
        acc = acc + jnp.maximum(s_h, 0.0) * w_h
    qpos = qb * Q_BLOCK + lax.broadcasted_iota(jnp.int32, (Q_BLOCK, 1), 0)
    kpos = lax.broadcasted_iota(jnp.int32, (1, S), 1)
    causal = kpos <= qpos
    sc_ref[...] = jnp.where(causal, acc, NEG_INF)

    @pl.when(qb * Q_BLOCK + Q_BLOCK <= INDEX_TOPK)
    def _():
        bias_ref[...] = jnp.where(causal, 0.0, NEG_INF)

    @pl.when(qb * Q_BLOCK + Q_BLOCK > INDEX_TOPK)
    def _():
        sc = sc_ref[...]
        row_max = jnp.max(sc, axis=-1, keepdims=True)
        row_min = jnp.min(jnp.where(causal, sc, jnp.inf), axis=-1, keepdims=True)
        top_tied = _count(sc >= row_max) >= kf

        def bisect(_, carry):
            lo, hi = carry
            mid = 0.5 * lo + 0.5 * hi
            ge = _count(sc_ref[...] >= mid) >= kf
            return jnp.where(ge, mid, lo), jnp.where(ge, hi, mid)

        lo, hi = lax.fori_loop(0, 18, bisect, (row_min, row_max))

        def refine_cond(carry):
            it, _, _, done = carry
            return jnp.logical_and(it < S, jnp.min(done) < 0.5)

        def refine(carry):
            it, hi, thr, done = carry
            scv = sc_ref[...]
            m = jnp.max(jnp.where(scv < hi, scv, NEG_INF), axis=-1, keepdims=True)
            hit = _count(scv >= m) >= kf
            fin = done > 0.5
            thr = jnp.where(fin, thr, m)
            hi = jnp.where(jnp.logical_or(fin, hit), hi, m)
            done = jnp.where(hit, 1.0, done)
            return it + 1, hi, thr, done

        done0 = jnp.where(top_tied, 1.0, 0.0)
        _, _, thr, _ = lax.while_loop(refine_cond, refine, (jnp.int32(0), hi, row_max, done0))

        gt = sc > thr
        eq = sc == thr
        need = kf - _count(gt)
        n_eq = _count(eq)
        bias_ref[...] = jnp.where(sc >= thr, 0.0, NEG_INF)

        @pl.when(jnp.max(n_eq - need) > 0.5)
        def _():
            r_i = lax.broadcasted_iota(jnp.int32, (LANES, LANES), 0)
            c_i = lax.broadcasted_iota(jnp.int32, (LANES, LANES), 1)
            upper = jnp.where(r_i <= c_i, 1.0, 0.0).astype(BF16)
            carry_cnt = jnp.zeros((Q_BLOCK, 1), F32)
            for j in range(S // LANES):
                blk = sc_ref[:, j * LANES:(j + 1) * LANES]
                eq_j = blk == thr
                eq_f = jnp.where(eq_j, 1.0, 0.0)
                rank = _dot(eq_f.astype(BF16), upper) + carry_cnt
                keep = jnp.logical_or(blk > thr, jnp.logical_and(eq_j, rank <= need))
                bias_ref[:, j * LANES:(j + 1) * LANES] = jnp.where(keep, 0.0, NEG_INF)
                carry_cnt = carry_cnt + jnp.sum(eq_f, axis=-1, keepdims=True)

    k = kv_ref[0][:, :HEAD_DIM]
    v = kv_ref[0][:, HEAD_DIM:]
    bias = bias_ref[...]
    for hd in range(ATTN_HEADS):
        logits = _dot_t(q_ref[0][:, hd * HEAD_DIM:(hd + 1) * HEAD_DIM], k) + bias
        mx = jnp.max(logits, axis=-1, keepdims=True)
        p = jnp.exp(logits - mx)
        den = jnp.sum(p, axis=-1, keepdims=True)
        o_ref[0, :, hd * HEAD_DIM:(hd + 1) * HEAD_DIM] = _dot(p.astype(BF16), v) / den


def _dsa_attention(q, iq, iw, kv, ik):
    B, S, _ = q.shape
    blk = lambda b, j: (b, j, 0)
    per_b = lambda b, j: (b, 0, 0)
    return pl.pallas_call(
        _dsa_kernel,
        out_shape=jax.ShapeDtypeStruct((B, S, ATTN_WIDTH), F32),
        grid=(B, S // Q_BLOCK),
        in_specs=[
            pl.BlockSpec((1, Q_BLOCK, ATTN_WIDTH), blk),
            pl.BlockSpec((1, Q_BLOCK, IDX_HEADS * IDX_DIM), blk),
            pl.BlockSpec((1, Q_BLOCK, LANES), blk),
            pl.BlockSpec((1, S, LANES), per_b),
            pl.BlockSpec((1, S, LANES), per_b),
        ],
        out_specs=pl.BlockSpec((1, Q_BLOCK, ATTN_WIDTH), blk),
        scratch_shapes=[pltpu.VMEM((Q_BLOCK, S), F32), pltpu.VMEM((Q_BLOCK, S), F32)],
        compiler_params=_cparams("parallel", "parallel"),
        name="dsa_attention",
    )(q, iq, iw, kv, ik)


def _hyb_out_kernel(ya_ref, bcu_ref, halo_ref, cw_ref, w_ref, x_ref, g_ref, o_ref):
    j = pl.program_id(1)
    tm = ya_ref.shape[1]
    bcu = bcu_ref[0]
    bg = bcu[:, 0:512]
    z = bcu[:, 512:1024] * bcu[:, 1024:1536]
    halo = halo_ref[0]
    zh = halo[:, 512:1024] * halo[:, 1024:1536]
    zh = jnp.where(j > 0, zh, 0.0)
    row = lax.broadcasted_iota(jnp.int32, (tm, 1), 0)
    z1 = jnp.where(row >= 1, pltpu.roll(z, 1, 0), zh[7:8, :])
    z2 = jnp.where(row >= 2, pltpu.roll(z, 2, 0), jnp.where(row == 1, zh[7:8, :], zh[6:7, :]))
    cw = cw_ref[...]
    y_conv = bg * (z2 * cw[0:1, :] + z1 * cw[1:2, :] + z * cw[2:3, :])
    y = _dot(ya_ref[0].astype(BF16), w_ref[0:512, :]) + _dot(y_conv.astype(BF16), w_ref[512:1024, :])
    o_ref[0] = x_ref[0] + g_ref[0] * y


def _hyb_out(y_attn, bcu, conv_w, w_out_bf, x, g1):
    B, S, D = x.shape
    tm = TOKEN_TILE
    row = lambda b, j: (b, j, 0)
    per_b = lambda b, j: (b, 0, 0)
    const2 = lambda b, j: (0, 0)
    halo = lambda b, j: (b, jnp.maximum(j * (tm // 8) - 1, 0), 0)
    return pl.pallas_call(
        _hyb_out_kernel,
        out_shape=jax.ShapeDtypeStruct((B, S, D), F32),
        grid=(B, S // tm),
        in_specs=[
            pl.BlockSpec((1, tm, 512), row),
            pl.BlockSpec((1, tm, 1536), row),
            pl.BlockSpec((1, 8, 1536), halo),
            pl.BlockSpec((CONV_K, CONV_WIDTH), const2),
            pl.BlockSpec((D, D), const2),
            pl.BlockSpec((1, tm, D), row),
            pl.BlockSpec((1, 1, D), per_b),
        ],
        out_specs=pl.BlockSpec((1, tm, D), row),
        compiler_params=_cparams("parallel", "parallel"),
        name="hybrid_out_proj",
    )(y_attn, bcu, bcu, conv_w, w_out_bf, x, g1)


ML_COLS = 512 + 512 + 1024 + 1024 + LANES


def _ml_in_kernel(x_ref, gain_ref, sc_ref, sh_ref, w_ref, bias_ref, q_ref, k_ref, v_ref, og_ref, gt_ref):
    h = _modulated_norm(x_ref[0], gain_ref[...], sc_ref[0], sh_ref[0])
    p = _dot(h.astype(BF16), w_ref[...])
    q_ref[0] = (p[:, 0:512] * (ML_QK_DIM ** -0.5)).astype(BF16)
    k_ref[0, 0] = p[:, 512:1024].T.astype(BF16)
    v_ref[0] = p[:, 1024:2048].astype(BF16)
    og_ref[0] = p[:, 2048:3072]
    gt_ref[0] = p[:, 3072:3200] + bias_ref[...]


def _ml_in(x, gain, sc, sh, w_pad, gate_bias):
    B, S, D = x.shape
    tm = TOKEN_TILE
    assert tm == ML_CHUNK
    row = lambda b, j: (b, j, 0)
    per_b = lambda b, j: (b, 0, 0)
    const2 = lambda b, j: (0, 0)
    return pl.pallas_call(
        _ml_in_kernel,
        out_shape=(
            jax.ShapeDtypeStruct((B, S, 512), BF16),
            jax.ShapeDtypeStruct((B, S // tm, 512, tm), BF16),
            jax.ShapeDtypeStruct((B, S, 1024), BF16),
            jax.ShapeDtypeStruct((B, S, 1024), F32),
            jax.ShapeDtypeStruct((B, S, LANES), F32),
        ),
        grid=(B, S // tm),
        in_specs=[
            pl.BlockSpec((1, tm, D), row),
            pl.BlockSpec((1, D), const2),
            pl.BlockSpec((1, 1, D), per_b),
            pl.BlockSpec((1, 1, D), per_b),
            pl.BlockSpec((D, ML_COLS), const2),
            pl.BlockSpec((1, LANES), const2),
        ],
        out_specs=(
            pl.BlockSpec((1, tm, 512), row),
            pl.BlockSpec((1, 1, 512, tm), lambda b, j: (b, j, 0, 0)),
            pl.BlockSpec((1, tm, 1024), row),
            pl.BlockSpec((1, tm, 1024), row),
            pl.BlockSpec((1, tm, LANES), row),
        ),
        compiler_params=_cparams("parallel", "parallel"),
        name="mlstm_in_proj",
    )(x, gain, sc, sh, w_pad, gate_bias)


def _log_sigmoid(f):
    return jnp.minimum(f, 0.0) - jnp.log1p(jnp.exp(-jnp.abs(f)))


def _split3(x):
    a = x.astype(BF16)
    r = x - a.astype(F32)
    b = r.astype(BF16)
    c = (r - b.astype(F32)).astype(BF16)
    return a, b, c


def _twice(a):
    return jnp.concatenate([a, a], axis=1)


def _mlstm_kernel(q_ref, kt_ref, v_ref, grow_ref, gcol_ref, gain_ref, o_ref, c_ref, m_ref):
    L = ML_CHUNK
    HP = ML_STEP_HEADS
    S = q_ref.shape[1]
    c_ref[...] = jnp.zeros_like(c_ref)
    m_ref[...] = jnp.zeros_like(m_ref)

    def chunk(c, carry):
        r0 = pl.multiple_of(c * L, L)
        r_i = lax.broadcasted_iota(jnp.int32, (L, L), 0)
        c_i = lax.broadcasted_iota(jnp.int32, (L, L), 1)
        tril = c_i <= r_i
        lower = jnp.where(tril, 1.0, 0.0).astype(BF16)
        upper = jnp.where(r_i <= c_i, 1.0, 0.0).astype(BF16)
        e_r = lax.broadcasted_iota(jnp.int32, (LANES, HP * LANES), 0)
        e_c = lax.broadcasted_iota(jnp.int32, (LANES, HP * LANES), 1)
        pick = jnp.where(e_r == HP + e_c // LANES, 1.0, 0.0).astype(BF16)
        rows = grow_ref[0, 0, c]
        cols = gcol_ref[0, 0, c]
        b_rows = sum(_dot(p, upper) for p in _split3(_log_sigmoid(rows)))
        b_cols = sum(_dot(lower, p) for p in _split3(_log_sigmoid(cols)))
        b_colr = sum(_dot(p, pick) for p in _split3(b_cols))
        lane = lax.broadcasted_iota(jnp.int32, (1, L), 1)
        b_last_all = jnp.sum(jnp.where(lane == L - 1, b_rows, 0.0), axis=-1, keepdims=True)
        ones_v = jnp.ones((L, ML_V_DIM), BF16)
        for hh in range(HP):
            q = q_ref[0, pl.ds(r0, L), hh * ML_QK_DIM:(hh + 1) * ML_QK_DIM]
            kt = kt_ref[0, c, hh * ML_QK_DIM:(hh + 1) * ML_QK_DIM, :]
            v = v_ref[0, pl.ds(r0, L), hh * ML_V_DIM:(hh + 1) * ML_V_DIM]
            vx = jnp.concatenate([v, ones_v], axis=1)
            i_row = rows[hh:hh + 1, :]
            b_row = b_rows[HP + hh:HP + hh + 1, :]
            b_last = b_last_all[HP + hh:HP + hh + 1, :]
            b_col = b_colr[:, hh * LANES:(hh + 1) * LANES]
            m_prev = m_ref[hh]
            ctn = c_ref[hh]

            dmat = jnp.where(tril, _twice(b_col) - b_row + i_row, NEG_INF)
            inter = b_col + m_prev
            m_t = jnp.maximum(inter, jnp.max(dmat, axis=-1, keepdims=True))
            w_intra = jnp.exp(dmat - _twice(m_t))
            w_inter = jnp.exp(inter - m_t)
            intra = (w_intra * _dot(q, kt)).astype(BF16)
            tot = _twice(w_inter) * _dot(q, ctn.astype(BF16)) + _dot(intra, vx)
            num = tot[:, :ML_V_DIM]
            den = tot[:, ML_V_DIM:]
            hc = num / jnp.maximum(jnp.abs(den), jnp.exp(-m_t))
            y = hc * lax.rsqrt(jnp.mean(hc * hc, axis=-1, keepdims=True) + NORM_EPS)
            o_ref[0, pl.ds(r0, L), hh * ML_V_DIM:(hh + 1) * ML_V_DIM] = (
                y * gain_ref[:, hh * ML_V_DIM:(hh + 1) * ML_V_DIM])

            g_row = b_last - b_row + i_row
            m_new = jnp.maximum(b_last + m_prev, jnp.max(g_row, axis=-1, keepdims=True))
            decay = jnp.exp(b_last + m_prev - m_new)
            kw = (kt.astype(F32) * jnp.exp(g_row - _twice(m_new))).astype(BF16)
            c_ref[hh] = _twice(decay) * ctn + _dot(kw, vx)
            m_ref[hh] = m_new
        return carry

    lax.fori_loop(0, S // L, chunk, 0)


def _mlstm(q, kt, v, g_rows, g_cols, out_gain):
    B, S, _ = q.shape
    assert ML_CHUNK == 2 * LANES
    nc = S // ML_CHUNK
    hp = ML_STEP_HEADS
    return pl.pallas_call(
        _mlstm_kernel,
        out_shape=jax.ShapeDtypeStruct((B, S, ML_HEADS * ML_V_DIM), F32),
        grid=(B, ML_HEADS // hp),
        in_specs=[
            pl.BlockSpec((1, S, hp * ML_QK_DIM), lambda b, p: (b, 0, p)),
            pl.BlockSpec((1, nc, hp * ML_QK_DIM, ML_CHUNK), lambda b, p: (b, 0, p, 0)),
            pl.BlockSpec((1, S, hp * ML_V_DIM), lambda b, p: (b, 0, p)),
            pl.BlockSpec((1, 1, nc, 2 * hp, ML_CHUNK), lambda b, p: (b, p, 0, 0, 0)),
            pl.BlockSpec((1, 1, nc, ML_CHUNK, LANES), lambda b, p: (b, p, 0, 0, 0)),
            pl.BlockSpec((1, hp * ML_V_DIM), lambda b, p: (0, p)),
        ],
        out_specs=pl.BlockSpec((1, S, hp * ML_V_DIM), lambda b, p: (b, 0, p)),
        scratch_shapes=[
            pltpu.VMEM((hp, ML_QK_DIM, ML_V_DIM + LANES), F32),
            pltpu.VMEM((hp, 1, LANES), F32),
        ],
        compiler_params=_cparams("parallel", "parallel"),
        name="mlstm_chunkwise",
    )(q, kt, v, g_rows, g_cols, out_gain)


def _ml_out_kernel(hh_ref, og_ref, w_ref, x_ref, g_ref, o_ref):
    a = jax.nn.sigmoid(og_ref[0]) * hh_ref[0]
    o_ref[0] = x_ref[0] + g_ref[0] * _dot(a.astype(BF16), w_ref[...])


def _ml_out(hh, og, w_out_bf, x, g1):
    B, S, D = x.shape
    tm = TOKEN_TILE
    row = lambda b, j: (b, j, 0)
    per_b = lambda b, j: (b, 0, 0)
    const2 = lambda b, j: (0, 0)
    return pl.pallas_call(
        _ml_out_kernel,
        out_shape=jax.ShapeDtypeStruct((B, S, D), F32),
        grid=(B, S // tm),
        in_specs=[
            pl.BlockSpec((1, tm, D), row),
            pl.BlockSpec((1, tm, D), row),
            pl.BlockSpec((D, D), const2),
            pl.BlockSpec((1, tm, D), row),
            pl.BlockSpec((1, 1, D), per_b),
        ],
        out_specs=pl.BlockSpec((1, tm, D), row),
        compiler_params=_cparams("parallel", "parallel"),
        name="mlstm_out_proj",
    )(hh, og, w_out_bf, x, g1)


def _first_argmax(x, lane, width):
    mx = jnp.max(x, axis=-1, keepdims=True)
    idx = jnp.min(jnp.where(x == mx, lane, width), axis=-1, keepdims=True)
    return mx, idx


def _router_kernel(x_ref, gain_ref, sc_ref, sh_ref, whi_ref, wlo_ref, b_ref, h_ref, r_ref, cnt_ref, run_ref):
    tm = x_ref.shape[1]

    @pl.when(jnp.logical_and(pl.program_id(0) == 0, pl.program_id(1) == 0))
    def _():
        run_ref[...] = jnp.zeros_like(run_ref)

    h = _modulated_norm(x_ref[0], gain_ref[...], sc_ref[0], sh_ref[0])
    h_hi = h.astype(BF16)
    h_ref[0] = h
    h_lo = (h - h_hi.astype(F32)).astype(BF16)
    whi = whi_ref[...]
    logits = _dot(h_hi, whi) + _dot(h_lo, whi) + _dot(h_hi, wlo_ref[...]) + b_ref[...]
    lane = lax.broadcasted_iota(jnp.int32, (1, LANES), 1)
    lg = jnp.where(lane < N_GROUPS, logits, NEG_INF)
    g_max, g_sel = _first_argmax(lg, lane, LANES)
    pg = 1.0 / jnp.sum(jnp.exp(lg - g_max), axis=-1, keepdims=True)
    e_lane = lane - N_GROUPS
    in_grp = jnp.logical_and(e_lane >= g_sel * EXPERTS_PER_GROUP, e_lane < (g_sel + 1) * EXPERTS_PER_GROUP)
    le = jnp.where(in_grp, logits, NEG_INF)
    v1, i1 = _first_argmax(le, lane, LANES)
    le2 = jnp.where(lane == i1, NEG_INF, le)
    v2, i2 = _first_argmax(le2, lane, LANES)
    e2 = jnp.exp(v2 - v1)
    w1 = pg / (1.0 + e2)
    w2 = pg * e2 / (1.0 + e2)
    e1 = i1 - N_GROUPS
    e2 = i2 - N_GROUPS
    hot1 = lane == e1
    hot2 = lane == e2
    onehot = jnp.where(jnp.logical_or(hot1, hot2), 1.0, 0.0)
    r_i = lax.broadcasted_iota(jnp.int32, (tm, tm), 0)
    c_i = lax.broadcasted_iota(jnp.int32, (tm, tm), 1)
    before = jnp.where(c_i < r_i, 1.0, 0.0).astype(BF16)
    seen = _dot(before, onehot.astype(BF16)) + run_ref[...]
    rank1 = jnp.sum(jnp.where(hot1, seen, 0.0), axis=-1, keepdims=True)
    rank2 = jnp.sum(jnp.where(hot2, seen, 0.0), axis=-1, keepdims=True)
    run_ref[...] = run_ref[...] + jnp.sum(onehot, axis=0, keepdims=True)
    cnt_ref[...] = run_ref[...]

    out = jnp.where(lane == 0, e1.astype(F32), 0.0)
    out = jnp.where(lane == 1, e2.astype(F32), out)
    out = jnp.where(lane == 2, w1, out)
    out = jnp.where(lane == 3, w2, out)
    out = jnp.where(lane == 4, rank1, out)
    out = jnp.where(lane == 5, rank2, out)
    r_ref[0] = out


def _router(x, gain, sc, sh, w_hi, w_lo, bias):
    B, S, D = x.shape
    tm = TOKEN_TILE
    row = lambda b, j: (b, j, 0)
    per_b = lambda b, j: (b, 0, 0)
    const2 = lambda b, j: (0, 0)
    return pl.pallas_call(
        _router_kernel,
        out_shape=(
            jax.ShapeDtypeStruct((B, S, D), F32),
            jax.ShapeDtypeStruct((B, S, LANES), F32),
            jax.ShapeDtypeStruct((1, LANES), F32),
        ),
        grid=(B, S // tm),
        in_specs=[
            pl.BlockSpec((1, tm, D), row),
            pl.BlockSpec((1, D), const2),
            pl.BlockSpec((1, 1, D), per_b),
            pl.BlockSpec((1, 1, D), per_b),
            pl.BlockSpec((D, LANES), const2),
            pl.BlockSpec((D, LANES), const2),
            pl.BlockSpec((1, LANES), const2),
        ],
        out_specs=(
            pl.BlockSpec((1, tm, D), row),
            pl.BlockSpec((1, tm, LANES), row),
            pl.BlockSpec((1, LANES), const2),
        ),
        scratch_shapes=[pltpu.VMEM((1, LANES), F32)],
        compiler_params=_cparams("arbitrary", "arbitrary"),
        name="moe_router",
    )(x, gain, sc, sh, w_hi, w_lo, bias)


def _experts_kernel(blk_e_ref, n_used_ref, x_ref, wg_ref, wu_ref, wd_ref, o_ref, wg_s, wu_s, wd_s):
    i = pl.program_id(0)
    used = i < n_used_ref[0]
    new_expert = jnp.logical_or(i == 0, blk_e_ref[i] != blk_e_ref[jnp.maximum(i - 1, 0)])

    @pl.when(jnp.logical_and(used, new_expert))
    def _():
        wg_s[...] = wg_ref[0, 0].astype(BF16)
        wu_s[...] = wu_ref[0, 0].astype(BF16)
        wd_s[...] = wd_ref[0, 0].astype(BF16)

    @pl.when(used)
    def _():
        x = x_ref[...].astype(BF16)
        a = _dot(x, wg_s[...])
        u = _dot(x, wu_s[...])
        act = a * jax.nn.sigmoid(a) * u
        o_ref[...] = _dot(act.astype(BF16), wd_s[...])

    @pl.when(i >= n_used_ref[0])
    def _():
        o_ref[...] = jnp.zeros_like(o_ref)


def _experts(layer, blk_e, n_used, xs, w_gate, w_up, w_down):
    R, D = xs.shape
    n_blk = R // MOE_BLOCK
    grid_spec = pltpu.PrefetchScalarGridSpec(
        num_scalar_prefetch=2,
        grid=(n_blk,),
        in_specs=[
            pl.BlockSpec((MOE_BLOCK, D), lambda i, be, nu: (i, 0)),
            pl.BlockSpec((1, 1, D, D_EXPERT), lambda i, be, nu: (layer, be[i], 0, 0)),
            pl.BlockSpec((1, 1, D, D_EXPERT), lambda i, be, nu: (layer, be[i], 0, 0)),
            pl.BlockSpec((1, 1, D_EXPERT, D), lambda i, be, nu: (layer, be[i], 0, 0)),
        ],
        out_specs=pl.BlockSpec((MOE_BLOCK, D), lambda i, be, nu: (i, 0)),
        scratch_shapes=[
            pltpu.VMEM((D, D_EXPERT), BF16),
            pltpu.VMEM((D, D_EXPERT), BF16),
            pltpu.VMEM((D_EXPERT, D), BF16),
        ],
    )
    return pl.pallas_call(
        _experts_kernel,
        out_shape=jax.ShapeDtypeStruct((R, D), F32),
        grid_spec=grid_spec,
        compiler_params=_cparams("arbitrary"),
        name="moe_experts",
    )(blk_e, n_used, xs, w_gate, w_up, w_down)


def _combine_kernel(x_ref, g_ref, y0_ref, y1_ref, r_ref, o_ref):
    r = r_ref[0]
    y = y0_ref[0, 0] * r[:, 2:3] + y1_ref[0, 0] * r[:, 3:4]
    o_ref[0] = x_ref[0] + g_ref[0] * y


def _combine(x, g2, y01, route):
    B, S, D = x.shape
    tm = TOKEN_TILE
    row = lambda b, j: (b, j, 0)
    per_b = lambda b, j: (b, 0, 0)
    return pl.pallas_call(
        _combine_kernel,
        out_shape=jax.ShapeDtypeStruct((B, S, D), F32),
        grid=(B, S // tm),
        in_specs=[
            pl.BlockSpec((1, tm, D), row),
            pl.BlockSpec((1, 1, D), per_b),
            pl.BlockSpec((1, 1, tm, D), lambda b, j: (0, b, j, 0)),
            pl.BlockSpec((1, 1, tm, D), lambda b, j: (1, b, j, 0)),
            pl.BlockSpec((1, tm, LANES), row),
        ],
        out_specs=pl.BlockSpec((1, tm, D), row),
        compiler_params=_cparams("parallel", "parallel"),
        name="moe_combine",
    )(x, g2, y01, y01, route)


SC_CORES = 2
SC_SUBCORES = 16
SC_WORKERS = SC_CORES * SC_SUBCORES
SC_CHUNK = 32


def _sc_mesh():
    return plsc.VectorSubcoreMesh(core_axis_name="c", subcore_axis_name="s",
                                  num_cores=SC_CORES, num_subcores=SC_SUBCORES)


def _sc_scatter_rows(src, idx, n_out):
    T, W = src.shape
    per_w = T // SC_WORKERS
    nch = per_w // SC_CHUNK
    idx4 = idx.reshape(TOP_K, SC_WORKERS, nch, SC_CHUNK)

    @functools.partial(
        pl.kernel, mesh=_sc_mesh(),
        out_type=jax.ShapeDtypeStruct((n_out, W), src.dtype),
        scratch_types=[pltpu.VMEM((TOP_K, nch, SC_CHUNK), jnp.int32), pltpu.VMEM((SC_CHUNK, W), src.dtype)],
        name="sc_scatter_rows",
    )
    def body(src_hbm, idx_hbm, out_hbm, idx_v, rows_v):
        wid = lax.axis_index("s") * SC_CORES + lax.axis_index("c")
        for s in range(TOP_K):
            pltpu.sync_copy(idx_hbm.at[s, wid], idx_v.at[s])

        @pl.loop(0, nch)
        def _(i):
            pltpu.sync_copy(src_hbm.at[pl.ds(wid * per_w + i * SC_CHUNK, SC_CHUNK)], rows_v)
            for s in range(TOP_K):
                pltpu.sync_copy(rows_v, out_hbm.at[idx_v.at[s, i]])

    return body(src, idx4)


def _sc_gather_rows(table, idx):
    N = idx.shape[0]
    W = table.shape[1]
    per_w = N // SC_WORKERS
    nch = per_w // SC_CHUNK
    idx3 = idx.reshape(SC_WORKERS, nch, SC_CHUNK)

    @functools.partial(
        pl.kernel, mesh=_sc_mesh(),
        out_type=jax.ShapeDtypeStruct((N, W), table.dtype),
        scratch_types=[pltpu.VMEM((nch, SC_CHUNK), jnp.int32), pltpu.VMEM((SC_CHUNK, W), table.dtype)],
        name="sc_gather_rows",
    )
    def body(table_hbm, idx_hbm, out_hbm, idx_v, rows_v):
        wid = lax.axis_index("s") * SC_CORES + lax.axis_index("c")
        pltpu.sync_copy(idx_hbm.at[wid], idx_v)

        @pl.loop(0, nch)
        def _(i):
            pltpu.sync_copy(table_hbm.at[idx_v.at[i]], rows_v)
            pltpu.sync_copy(rows_v, out_hbm.at[pl.ds(wid * per_w + i * SC_CHUNK, SC_CHUNK)])

    return body(table, idx3)


def _moe_dispatch(route, counts, T):
    A = T * TOP_K
    counts = counts[0, :N_EXPERTS].astype(jnp.int32)
    blocks_per = (counts + MOE_BLOCK - 1) // MOE_BLOCK
    block_end = jnp.cumsum(blocks_per)
    block_start = block_end - blocks_per
    expert = route[:, :TOP_K].astype(jnp.int32)
    rank = route[:, 4:4 + TOP_K].astype(jnp.int32)
    onehot = expert[:, :, None] == jnp.arange(N_EXPERTS, dtype=jnp.int32)
    start = jnp.sum(jnp.where(onehot, block_start, 0), axis=-1)
    dest = (start * MOE_BLOCK + rank).T
    n_blk = -(-A // MOE_BLOCK) + N_EXPERTS
    blk = jnp.arange(n_blk, dtype=jnp.int32)
    blk_e = jnp.minimum(jnp.sum(blk[:, None] >= block_end[None, :], axis=-1), N_EXPERTS - 1)
    return dest, n_blk * MOE_BLOCK, blk_e.astype(jnp.int32), block_end[-1:].astype(jnp.int32)


def _rope_tables(S):
    inv = 1.0 / (ROPE_THETA ** (jnp.arange(0, HEAD_DIM, 2, dtype=F32) / HEAD_DIM))
    ang = jnp.arange(S, dtype=F32)[:, None] * inv[None, :]
    cos, sin = jnp.cos(ang), jnp.sin(ang)
    cos_h = jnp.concatenate([cos, cos], axis=-1)
    sin_h = jnp.concatenate([-sin, sin], axis=-1)
    return jnp.tile(cos_h, (1, ATTN_HEADS)), jnp.tile(sin_h, (1, ATTN_HEADS))


def _pad_cols(w, width):
    return jnp.pad(w, ((0, 0), (0, width - w.shape[1])))


def kernel(x, c, ada_w, ada_b, norm_mix, norm_ffn, hy_w_in, hy_q_norm, hy_k_norm, hy_conv_w, hy_w_out, ml_w_in, ml_b_gates, ml_out_norm, ml_w_out, moe_w_group, moe_b_group, moe_w_expert, moe_b_expert, moe_w_gate, moe_w_up, moe_w_down):
    B, S, D = x.shape
    T = B * S
    cos_t, sin_t = _rope_tables(S)
    mod = _ada_modulation(c, ada_w, ada_b).reshape(DEPTH, B, 6, 1, D)
    r_i = np.arange(ATTN_WIDTH)
    grp = jnp.asarray((r_i[:, None] // HEAD_DIM) == (r_i[None, :] // HEAD_DIM), dtype=BF16)

    for l in range(DEPTH):
        sh1, sc1, g1, sh2, sc2, g2 = [mod[l, :, i] for i in range(6)]
        gain1 = norm_mix[l].reshape(1, D)
        j = l // 2
        if l % 2 == 0:
            w = hy_w_in[j]
            o = np.cumsum((0,) + (ATTN_WIDTH, HEAD_DIM, HEAD_DIM, IDX_HEADS * IDX_DIM, IDX_DIM, IDX_HEADS,
                                  CONV_WIDTH, CONV_WIDTH, CONV_WIDTH))
            wq, wk, wv, wiq, wik, wiw, wbg, wcg, wu = [w[:, o[i]:o[i + 1]] for i in range(9)]
            w_pad = jnp.concatenate(
                [wq, wiq, wbg, wcg, wu, wk, wv, _pad_cols(jnp.concatenate([wik, wiw], axis=1), LANES)],
                axis=1).astype(BF16)
            qn_t = jnp.tile(hy_q_norm[j], ATTN_HEADS).reshape(1, ATTN_WIDTH)
            kn_t = jnp.tile(hy_k_norm[j], LANES // HEAD_DIM).reshape(1, LANES)
            q, iq, bcu, kv, ik, iw = _hyb_in(x, gain1, sc1, sh1, w_pad, cos_t, sin_t, qn_t, kn_t, grp)
            y_attn = _dsa_attention(q, iq, iw, kv, ik)
            x = _hyb_out(y_attn, bcu, hy_conv_w[j], hy_w_out[j].astype(BF16), x, g1)
        else:
            w = ml_w_in[j]
            hq = ML_HEADS * ML_QK_DIM
            hv = ML_HEADS * ML_V_DIM
            wq, wk, wv = w[:, :hq], w[:, hq:2 * hq], w[:, 2 * hq:2 * hq + hv]
            wg = w[:, 2 * hq + hv:2 * hq + hv + 2 * ML_HEADS]
            wo = w[:, 2 * hq + hv + 2 * ML_HEADS:]
            w_pad = jnp.concatenate([wq, wk, wv, wo, _pad_cols(wg, LANES)], axis=1).astype(BF16)
            gate_bias = jnp.pad(ml_b_gates[j], (0, LANES - 2 * ML_HEADS)).reshape(1, LANES)
            q, k, v, og, gates = _ml_in(x, gain1, sc1, sh1, w_pad, gate_bias)
            nc = S // ML_CHUNK
            hp = ML_STEP_HEADS
            groups = ML_HEADS // hp
            gi = gates[:, :, :ML_HEADS].reshape(B, S, groups, hp)
            gf = gates[:, :, ML_HEADS:2 * ML_HEADS].reshape(B, S, groups, hp)
            gp = jnp.concatenate([gi, gf], axis=-1)
            gp = jnp.transpose(gp, (0, 2, 1, 3)).reshape(B, groups, nc, ML_CHUNK, 2 * hp)
            g_cols = jnp.pad(gp, ((0, 0),) * 4 + ((0, LANES - 2 * hp),))
            g_rows = jnp.swapaxes(gp, -1, -2)
            hh = _mlstm(q, k, v, g_rows, g_cols, ml_out_norm[j].reshape(1, hv))
            x = _ml_out(hh, og, ml_w_out[j].astype(BF16), x, g1)

        w_r = _pad_cols(jnp.concatenate([moe_w_group[l], moe_w_expert[l]], axis=1), LANES)
        w_hi = w_r.astype(BF16)
        w_lo = (w_r - w_hi.astype(F32)).astype(BF16)
        b_r = jnp.pad(jnp.concatenate([moe_b_group[l], moe_b_expert[l]]), (0, LANES - N_GROUPS - N_EXPERTS))
        h2, route, counts = _router(x, norm_ffn[l].reshape(1, D), sc2, sh2, w_hi, w_lo, b_r.reshape(1, LANES))
        dest, n_rows, blk_e, n_used = _moe_dispatch(route.reshape(T, LANES), counts, T)
        xs = _sc_scatter_rows(h2.reshape(T, D), dest, n_rows)
        ys = _experts(l, blk_e, n_used, xs, moe_w_gate, moe_w_up, moe_w_down)
        y01 = _sc_gather_rows(ys, dest.reshape(TOP_K * T)).reshape(TOP_K, B, S, D)
        x = _combine(x, g2, y01, route)
    return x
```

```python
import functools

import numpy as np
import jax
import jax.numpy as jnp
from jax import lax
from jax.experimental import pallas as pl
from jax.experimental.pallas import tpu as pltpu
from jax.experimental.pallas import tpu_sc as plsc

F32 = jnp.float32
BF16 = jnp.bfloat16
HIGHEST = lax.Precision.HIGHEST

D_MODEL = 1024
DEPTH = 4
ATTN_HEADS = 8
HEAD_DIM = 64
ATTN_WIDTH = ATTN_HEADS * HEAD_DIM
IDX_HEADS = 8
IDX_DIM = 64
INDEX_TOPK = 256
Q_BLOCK = 256
ROPE_THETA = 10000.0
CONV_WIDTH = D_MODEL - ATTN_WIDTH
CONV_K = 3
ML_HEADS = 8
ML_QK_DIM = 64
ML_V_DIM = 128
N_GROUPS = 4
EXPERTS_PER_GROUP = 8
N_EXPERTS = N_GROUPS * EXPERTS_PER_GROUP
TOP_K = 2
D_EXPERT = 512
MOE_BLOCK = 256
NORM_EPS = 1e-6

LANES = 128
VMEM_LIMIT = 56 * 1024 * 1024
TOKEN_TILE = 256
ML_CHUNK = 256
ML_STEP_HEADS = 4
NEG_INF = float("-inf")


def _cparams(*sem):
    return pltpu.CompilerParams(dimension_semantics=sem, vmem_limit_bytes=VMEM_LIMIT)


def _dot(a, b):
    return jnp.dot(a, b, preferred_element_type=F32)


def _dot_t(a, b):
    return lax.dot_general(a, b, (((1,), (1,)), ((), ())), preferred_element_type=F32)


def _split_dot(a_f32, b_bf16):
    hi = a_f32.astype(BF16)
    lo = (a_f32 - hi.astype(F32)).astype(BF16)
    return _dot(hi, b_bf16) + _dot(lo, b_bf16)


def _ada_kernel(c_ref, w_ref, b_ref, o_ref):
    c = c_ref[...]
    ca = c * jax.nn.sigmoid(c)
    o_ref[0] = jnp.dot(ca, w_ref[0], precision=HIGHEST, preferred_element_type=F32) + b_ref[0]


def _ada_modulation(c, ada_w, ada_b):
    B, D = c.shape
    n_col = ada_w.shape[-1] // D
    return pl.pallas_call(
        _ada_kernel,
        out_shape=jax.ShapeDtypeStruct((DEPTH, B, n_col * D), F32),
        grid=(DEPTH, n_col),
        in_specs=[
            pl.BlockSpec((B, D), lambda l, j: (0, 0)),
            pl.BlockSpec((1, D, D), lambda l, j: (l, 0, j)),
            pl.BlockSpec((1, 1, D), lambda l, j: (l, 0, j)),
        ],
        out_specs=pl.BlockSpec((1, B, D), lambda l, j: (l, 0, j)),
        compiler_params=_cparams("parallel", "parallel"),
        name="ada_modulation",
    )(c, ada_w, ada_b.reshape(DEPTH, 1, n_col * D))


def _modulated_norm(x, gain, scale, shift):
    y = x * lax.rsqrt(jnp.mean(x * x, axis=-1, keepdims=True) + NORM_EPS)
    return y * gain * (1.0 + scale) + shift


def _rope(x, cos, sin_signed, first_half):
    w = x.shape[-1]
    partner = jnp.where(first_half, pltpu.roll(x, w - HEAD_DIM // 2, 1), pltpu.roll(x, HEAD_DIM // 2, 1))
    return x * cos + partner * sin_signed


HYB_COLS = 5 * 512 + 2 * LANES


def _hyb_in_kernel(x_ref, gain_ref, sc_ref, sh_ref, w_ref, cos_ref, sin_ref, qn_ref, kn_ref, grp_ref,
                   q_ref, iq_ref, bcu_ref, kv_ref, ik_ref, iw_ref):
    h = _modulated_norm(x_ref[0], gain_ref[...], sc_ref[0], sh_ref[0])
    p = _dot(h.astype(BF16), w_ref[...])
    cos = cos_ref[...]
    sin = sin_ref[...]
    lane = lax.broadcasted_iota(jnp.int32, (1, ATTN_WIDTH), 1)
    first_half = (lane % HEAD_DIM) < (HEAD_DIM // 2)
    fh128 = first_half[:, :LANES]
    lane128 = lane[:, :LANES]

    q = p[:, 0:512]
    ms = _split_dot(q * q, grp_ref[...]) * (1.0 / HEAD_DIM)
    q = q * lax.rsqrt(ms + NORM_EPS) * qn_ref[...]
    q_ref[0] = (_rope(q, cos, sin, first_half) * (HEAD_DIM ** -0.5)).astype(BF16)

    iq = p[:, 512:1024]
    iq_ref[0] = (_rope(iq, cos, sin, first_half) * (IDX_DIM ** -0.5)).astype(BF16)

    bcu_ref[0] = p[:, 1024:2560]

    kv = p[:, 2560:2688]
    is_k = lane128 < HEAD_DIM
    kk = jnp.where(is_k, kv, 0.0)
    ms_k = jnp.sum(kk * kk, axis=-1, keepdims=True) * (1.0 / HEAD_DIM)
    kn = kv * lax.rsqrt(ms_k + NORM_EPS) * kn_ref[...]
    kr = _rope(kn, cos[:, :LANES], sin[:, :LANES], fh128)
    kv_ref[0] = jnp.where(is_k, kr, kv).astype(BF16)

    sm = p[:, 2688:2816]
    ikr = _rope(sm, cos[:, :LANES], sin[:, :LANES], fh128)
    ik_ref[0] = jnp.where(is_k, ikr, 0.0).astype(BF16)
    iw_ref[0] = sm


def _hyb_in(x, gain, sc, sh, w_pad, cos_t, sin_t, qn_t, kn_t, grp):
    B, S, D = x.shape
    tm = TOKEN_TILE
    row = lambda b, j: (b, j, 0)
    per_b = lambda b, j: (b, 0, 0)
    const2 = lambda b, j: (0, 0)
    tab = lambda b, j: (j, 0)
    return pl.pallas_call(
        _hyb_in_kernel,
        out_shape=(
            jax.ShapeDtypeStruct((B, S, 512), BF16),
            jax.ShapeDtypeStruct((B, S, 512), BF16),
            jax.ShapeDtypeStruct((B, S, 1536), F32),
            jax.ShapeDtypeStruct((B, S, LANES), BF16),
            jax.ShapeDtypeStruct((B, S, LANES), BF16),
            jax.ShapeDtypeStruct((B, S, LANES), F32),
        ),
        grid=(B, S // tm),
        in_specs=[
            pl.BlockSpec((1, tm, D), row),
            pl.BlockSpec((1, D), const2),
            pl.BlockSpec((1, 1, D), per_b),
            pl.BlockSpec((1, 1, D), per_b),
            pl.BlockSpec((D, HYB_COLS), const2),
            pl.BlockSpec((tm, 512), tab),
            pl.BlockSpec((tm, 512), tab),
            pl.BlockSpec((1, 512), const2),
            pl.BlockSpec((1, LANES), const2),
            pl.BlockSpec((512, 512), const2),
        ],
        out_specs=(
            pl.BlockSpec((1, tm, 512), row),
            pl.BlockSpec((1, tm, 512), row),
            pl.BlockSpec((1, tm, 1536), row),
            pl.BlockSpec((1, tm, LANES), row),
            pl.BlockSpec((1, tm, LANES), row),
            pl.BlockSpec((1, tm, LANES), row),
        ),
        compiler_params=_cparams("parallel", "parallel"),
        name="hybrid_in_proj",
    )(x, gain, sc, sh, w_pad, cos_t, sin_t, qn_t, kn_t, grp)


def _count(mask):
    return jnp.sum(jnp.where(mask, 1.0, 0.0), axis=-1, keepdims=True)


DSA_KEY_SPANS = 8


def _dsa_kernel(q_ref, iq_ref, iw_ref, kv_ref, ik_ref, o_ref, sc_ref, bias_ref):
    S = kv_ref.shape[1]
    span = S // DSA_KEY_SPANS
    qb = pl.program_id(1)
    for v in range(DSA_KEY_SPANS):
        sk = (v + 1) * span

        @pl.when(qb // (span // Q_BLOCK) == v)
        def _():
            _dsa_body(q_ref, iq_ref, iw_ref, kv_ref.at[:, :sk, :], ik_ref.at[:, :sk, :], o_ref,
                      sc_ref.at[:, :sk], bias_ref.at[:, :sk])


def _dsa_body(q_ref, iq_ref, iw_ref, kv_ref, ik_ref, o_ref, sc_ref, bias_ref):
    S = kv_ref.shape[1]
    qb = pl.program_id(1)
    kf = float(INDEX_TOPK)

    ik = ik_ref[0][:, :IDX_DIM]
    iw = iw_ref[0]
    acc = jnp.zeros((Q_BLOCK, S), F32)
    for hd in range(IDX_HEADS):
        s_h = _dot_t(iq_ref[0][:, hd * IDX_DIM:(hd + 1) * IDX_DIM], ik)
        w_h = iw[:, IDX_DIM + hd:IDX_DIM + hd + 1] * (IDX_HEADS ** -0.5)
        acc = acc + jnp.maximum(s_h, 0.0) * w_h
    qpos = qb * Q_BLOCK + lax.broadcasted_iota(jnp.int32, (Q_BLOCK, 1), 0)
    kpos = lax.broadcasted_iota(jnp.int32, (1, S), 1)
    causal = kpos <= qpos
    sc_ref[...] = jnp.where(causal, acc, NEG_INF)

    @pl.when(qb * Q_BLOCK + Q_BLOCK <= INDEX_TOPK)
    def _():
        bias_ref[...] = jnp.where(causal, 0.0, NEG_INF)

    @pl.when(qb * Q_BLOCK + Q_BLOCK > INDEX_TOPK)
    def _():
        sc = sc_ref[...]
        row_max = jnp.max(sc, axis=-1, keepdims=True)
        row_min = jnp.min(jnp.where(causal, sc, jnp.inf), axis=-1, keepdims=True)
        top_tied = _count(sc >= row_max) >= kf

        def bisect(_, carry):
            lo, hi = carry
            mid = 0.5 * lo + 0.5 * hi
            ge = _count(sc_ref[...] >= mid) >= kf
            return jnp.where(ge, mid, lo), jnp.where(ge, hi, mid)

        lo, hi = lax.fori_loop(0, 18, bisect, (row_min, row_max))

        def refine_cond(carry):
            it, _, _, done = carry
            return jnp.logical_and(it < S, jnp.min(done) < 0.5)

        def refine(carry):
            it, hi, thr, done = carry
            scv = sc_ref[...]
            m = jnp.max(jnp.where(scv < hi, scv, NEG_INF), axis=-1, keepdims=True)
            hit = _count(scv >= m) >= kf
            fin = done > 0.5
            thr = jnp.where(fin, thr, m)
            hi = jnp.where(jnp.logical_or(fin, hit), hi, m)
            done = jnp.where(hit, 1.0, done)
            return it + 1, hi, thr, done

        done0 = jnp.where(top_tied, 1.0, 0.0)
        _, _, thr, _ = lax.while_loop(refine_cond, refine, (jnp.int32(0), hi, row_max, done0))

        gt = sc > thr
        eq = sc == thr
        need = kf - _count(gt)
        n_eq = _count(eq)
        bias_ref[...] = jnp.where(sc >= thr, 0.0, NEG_INF)

        @pl.when(jnp.max(n_eq - need) > 0.5)
        def _():
            r_i = lax.broadcasted_iota(jnp.int32, (LANES, LANES), 0)
            c_i = lax.broadcasted_iota(jnp.int32, (LANES, LANES), 1)
            upper = jnp.where(r_i <= c_i, 1.0, 0.0).astype(BF16)
            carry_cnt = jnp.zeros((Q_BLOCK, 1), F32)
            for j in range(S // LANES):
                blk = sc_ref[:, j * LANES:(j + 1) * LANES]
                eq_j = blk == thr
                eq_f = jnp.where(eq_j, 1.0, 0.0)
                rank = _dot(eq_f.astype(BF16), upper) + carry_cnt
                keep = jnp.logical_or(blk > thr, jnp.logical_and(eq_j, rank <= need))
                bias_ref[:, j * LANES:(j + 1) * LANES] = jnp.where(keep, 0.0, NEG_INF)
                carry_cnt = carry_cnt + jnp.sum(eq_f, axis=-1, keepdims=True)

    k = kv_ref[0][:, :HEAD_DIM]
    v = kv_ref[0][:, HEAD_DIM:]
    bias = bias_ref[...]
    for hd in range(ATTN_HEADS):
        logits = _dot_t(q_ref[0][:, hd * HEAD_DIM:(hd + 1) * HEAD_DIM], k) + bias
        mx = jnp.max(logits, axis=-1, keepdims=True)
        p = jnp.exp(logits - mx)
        den = jnp.sum(p, axis=-1, keepdims=True)
        o_ref[0, :, hd * HEAD_DIM:(hd + 1) * HEAD_DIM] = _dot(p.astype(BF16), v) / den


def _dsa_attention(q, iq, iw, kv, ik):
    B, S, _ = q.shape
    blk = lambda b, j: (b, j, 0)
    per_b = lambda b, j: (b, 0, 0)
    return pl.pallas_call(
        _dsa_kernel,
        out_shape=jax.ShapeDtypeStruct((B, S, ATTN_WIDTH), F32),
        grid=(B, S // Q_BLOCK),
        in_specs=[
            pl.BlockSpec((1, Q_BLOCK, ATTN_WIDTH), blk),
            pl.BlockSpec((1, Q_BLOCK, IDX_HEADS * IDX_DIM), blk),
            pl.BlockSpec((1, Q_BLOCK, LANES), blk),
            pl.BlockSpec((1, S, LANES), per_b),
            pl.BlockSpec((1, S, LANES), per_b),
        ],
        out_specs=pl.BlockSpec((1, Q_BLOCK, ATTN_WIDTH), blk),
        scratch_shapes=[pltpu.VMEM((Q_BLOCK, S), F32), pltpu.VMEM((Q_BLOCK, S), F32)],
        compiler_params=_cparams("parallel", "parallel"),
        name="dsa_attention",
    )(q, iq, iw, kv, ik)


def _hyb_out_kernel(ya_ref, bcu_ref, halo_ref, cw_ref, w_ref, x_ref, g_ref, o_ref):
    j = pl.program_id(1)
    tm = ya_ref.shape[1]
    bcu = bcu_ref[0]
    bg = bcu[:, 0:512]
    z = bcu[:, 512:1024] * bcu[:, 1024:1536]
    halo = halo_ref[0]
    zh = halo[:, 512:1024] * halo[:, 1024:1536]
    zh = jnp.where(j > 0, zh, 0.0)
    row = lax.broadcasted_iota(jnp.int32, (tm, 1), 0)
    z1 = jnp.where(row >= 1, pltpu.roll(z, 1, 0), zh[7:8, :])
    z2 = jnp.where(row >= 2, pltpu.roll(z, 2, 0), jnp.where(row == 1, zh[7:8, :], zh[6:7, :]))
    cw = cw_ref[...]
    y_conv = bg * (z2 * cw[0:1, :] + z1 * cw[1:2, :] + z * cw[2:3, :])
    y = _dot(ya_ref[0].astype(BF16), w_ref[0:512, :]) + _dot(y_conv.astype(BF16), w_ref[512:1024, :])
    o_ref[0] = x_ref[0] + g_ref[0] * y


def _hyb_out(y_attn, bcu, conv_w, w_out_bf, x, g1):
    B, S, D = x.shape
    tm = TOKEN_TILE
    row = lambda b, j: (b, j, 0)
    per_b = lambda b, j: (b, 0, 0)
    const2 = lambda b, j: (0, 0)
    halo = lambda b, j: (b, jnp.maximum(j * (tm // 8) - 1, 0), 0)
    return pl.pallas_call(
        _hyb_out_kernel,
        out_shape=jax.ShapeDtypeStruct((B, S, D), F32),
        grid=(B, S // tm),
        in_specs=[
            pl.BlockSpec((1, tm, 512), row),
            pl.BlockSpec((1, tm, 1536), row),
            pl.BlockSpec((1, 8, 1536), halo),
            pl.BlockSpec((CONV_K, CONV_WIDTH), const2),
            pl.BlockSpec((D, D), const2),
            pl.BlockSpec((1, tm, D), row),
            pl.BlockSpec((1, 1, D), per_b),
        ],
        out_specs=pl.BlockSpec((1, tm, D), row),
        compiler_params=_cparams("parallel", "parallel"),
        name="hybrid_out_proj",
    )(y_attn, bcu, bcu, conv_w, w_out_bf, x, g1)


ML_COLS = 512 + 512 + 1024 + 1024 + LANES


def _ml_in_kernel(x_ref, gain_ref, sc_ref, sh_ref, w_ref, bias_ref, q_ref, k_ref, v_ref, og_ref, gt_ref):
    h = _modulated_norm(x_ref[0], gain_ref[...], sc_ref[0], sh_ref[0])
    p = _dot(h.astype(BF16), w_ref[...])
    q_ref[0] = (p[:, 0:512] * (ML_QK_DIM ** -0.5)).astype(BF16)
    k_ref[0, 0] = p[:, 512:1024].T.astype(BF16)
    v_ref[0] = p[:, 1024:2048].astype(BF16)
    og_ref[0] = p[:, 2048:3072]
    gt_ref[0] = p[:, 3072:3200] + bias_ref[...]


def _ml_in(x, gain, sc, sh, w_pad, gate_bias):
    B, S, D = x.shape
    tm = TOKEN_TILE
    assert tm == ML_CHUNK
    row = lambda b, j: (b, j, 0)
    per_b = lambda b, j: (b, 0, 0)
    const2 = lambda b, j: (0, 0)
    return pl.pallas_call(
        _ml_in_kernel,
        out_shape=(
            jax.ShapeDtypeStruct((B, S, 512), BF16),
            jax.ShapeDtypeStruct((B, S // tm, 512, tm), BF16),
            jax.ShapeDtypeStruct((B, S, 1024), BF16),
            jax.ShapeDtypeStruct((B, S, 1024), F32),
            jax.ShapeDtypeStruct((B, S, LANES), F32),
        ),
        grid=(B, S // tm),
        in_specs=[
            pl.BlockSpec((1, tm, D), row),
            pl.BlockSpec((1, D), const2),
            pl.BlockSpec((1, 1, D), per_b),
            pl.BlockSpec((1, 1, D), per_b),
            pl.BlockSpec((D, ML_COLS), const2),
            pl.BlockSpec((1, LANES), const2),
        ],
        out_specs=(
            pl.BlockSpec((1, tm, 512), row),
            pl.BlockSpec((1, 1, 512, tm), lambda b, j: (b, j, 0, 0)),
            pl.BlockSpec((1, tm, 1024), row),
            pl.BlockSpec((1, tm, 1024), row),
            pl.BlockSpec((1, tm, LANES), row),
        ),
        compiler_params=_cparams("parallel", "parallel"),
        name="mlstm_in_proj",
    )(x, gain, sc, sh, w_pad, gate_bias)


def _log_sigmoid(f):
    return jnp.minimum(f, 0.0) - jnp.log1p(jnp.exp(-jnp.abs(f)))


def _split3(x):
    a = x.astype(BF16)
    r = x - a.astype(F32)
    b = r.astype(BF16)
    c = (r - b.astype(F32)).astype(BF16)
    return a, b, c


def _twice(a):
    return jnp.concatenate([a, a], axis=1)


def _mlstm_kernel(q_ref, kt_ref, v_ref, grow_ref, gcol_ref, gain_ref, o_ref, c_ref, m_ref):
    L = ML_CHUNK
    HP = ML_STEP_HEADS
    S = q_ref.shape[1]
    c_ref[...] = jnp.zeros_like(c_ref)
    m_ref[...] = jnp.zeros_like(m_ref)

    def chunk(c, carry):
        r0 = pl.multiple_of(c * L, L)
        r_i = lax.broadcasted_iota(jnp.int32, (L, L), 0)
        c_i = lax.broadcasted_iota(jnp.int32, (L, L), 1)
        tril = c_i <= r_i
        lower = jnp.where(tril, 1.0, 0.0).astype(BF16)
        upper = jnp.where(r_i <= c_i, 1.0, 0.0).astype(BF16)
        e_r = lax.broadcasted_iota(jnp.int32, (LANES, HP * LANES), 0)
        e_c = lax.broadcasted_iota(jnp.int32, (LANES, HP * LANES), 1)
        pick = jnp.where(e_r == HP + e_c // LANES, 1.0, 0.0).astype(BF16)
        rows = grow_ref[0, 0, c]
        cols = gcol_ref[0, 0, c]
        b_rows = sum(_dot(p, upper) for p in _split3(_log_sigmoid(rows)))
        b_cols = sum(_dot(lower, p) for p in _split3(_log_sigmoid(cols)))
        b_colr = sum(_dot(p, pick) for p in _split3(b_cols))
        lane = lax.broadcasted_iota(jnp.int32, (1, L), 1)
        b_last_all = jnp.sum(jnp.where(lane == L - 1, b_rows, 0.0), axis=-1, keepdims=True)
        ones_v = jnp.ones((L, ML_V_DIM), BF16)
        for hh in range(HP):
            q = q_ref[0, pl.ds(r0, L), hh * ML_QK_DIM:(hh + 1) * ML_QK_DIM]
            kt = kt_ref[0, c, hh * ML_QK_DIM:(hh + 1) * ML_QK_DIM, :]
            v = v_ref[0, pl.ds(r0, L), hh * ML_V_DIM:(hh + 1) * ML_V_DIM]
            vx = jnp.concatenate([v, ones_v], axis=1)
            i_row = rows[hh:hh + 1, :]
            b_row = b_rows[HP + hh:HP + hh + 1, :]
            b_last = b_last_all[HP + hh:HP + hh + 1, :]
            b_col = b_colr[:, hh * LANES:(hh + 1) * LANES]
            m_prev = m_ref[hh]
            ctn = c_ref[hh]

            dmat = jnp.where(tril, _twice(b_col) - b_row + i_row, NEG_INF)
            inter = b_col + m_prev
            m_t = jnp.maximum(inter, jnp.max(dmat, axis=-1, keepdims=True))
            w_intra = jnp.exp(dmat - _twice(m_t))
            w_inter = jnp.exp(inter - m_t)
            intra = (w_intra * _dot(q, kt)).astype(BF16)
            tot = _twice(w_inter) * _dot(q, ctn.astype(BF16)) + _dot(intra, vx)
            num = tot[:, :ML_V_DIM]
            den = tot[:, ML_V_DIM:]
            hc = num / jnp.maximum(jnp.abs(den), jnp.exp(-m_t))
            y = hc * lax.rsqrt(jnp.mean(hc * hc, axis=-1, keepdims=True) + NORM_EPS)
            o_ref[0, pl.ds(r0, L), hh * ML_V_DIM:(hh + 1) * ML_V_DIM] = (
                y * gain_ref[:, hh * ML_V_DIM:(hh + 1) * ML_V_DIM])

            g_row = b_last - b_row + i_row
            m_new = jnp.maximum(b_last + m_prev, jnp.max(g_row, axis=-1, keepdims=True))
            decay = jnp.exp(b_last + m_prev - m_new)
            kw = (kt.astype(F32) * jnp.exp(g_row - _twice(m_new))).astype(BF16)
            c_ref[hh] = _twice(decay) * ctn + _dot(kw, vx)
            m_ref[hh] = m_new
        return carry

    lax.fori_loop(0, S // L, chunk, 0)


def _mlstm(q, kt, v, g_rows, g_cols, out_gain):
    B, S, _ = q.shape
    assert ML_CHUNK == 2 * LANES
    nc = S // ML_CHUNK
    hp = ML_STEP_HEADS
    return pl.pallas_call(
        _mlstm_kernel,
        out_shape=jax.ShapeDtypeStruct((B, S, ML_HEADS * ML_V_DIM), F32),
        grid=(B, ML_HEADS // hp),
        in_specs=[
            pl.BlockSpec((1, S, hp * ML_QK_DIM), lambda b, p: (b, 0, p)),
            pl.BlockSpec((1, nc, hp * ML_QK_DIM, ML_CHUNK), lambda b, p: (b, 0, p, 0)),
            pl.BlockSpec((1, S, hp * ML_V_DIM), lambda b, p: (b, 0, p)),
            pl.BlockSpec((1, 1, nc, 2 * hp, ML_CHUNK), lambda b, p: (b, p, 0, 0, 0)),
            pl.BlockSpec((1, 1, nc, ML_CHUNK, LANES), lambda b, p: (b, p, 0, 0, 0)),
            pl.BlockSpec((1, hp * ML_V_DIM), lambda b, p: (0, p)),
        ],
        out_specs=pl.BlockSpec((1, S, hp * ML_V_DIM), lambda b, p: (b, 0, p)),
        scratch_shapes=[
            pltpu.VMEM((hp, ML_QK_DIM, ML_V_DIM + LANES), F32),
            pltpu.VMEM((hp, 1, LANES), F32),
        ],
        compiler_params=_cparams("parallel", "parallel"),
        name="mlstm_chunkwise",
    )(q, kt, v, g_rows, g_cols, out_gain)


def _ml_out_kernel(hh_ref, og_ref, w_ref, x_ref, g_ref, o_ref):
    a = jax.nn.sigmoid(og_ref[0]) * hh_ref[0]
    o_ref[0] = x_ref[0] + g_ref[0] * _dot(a.astype(BF16), w_ref[...])


def _ml_out(hh, og, w_out_bf, x, g1):
    B, S, D = x.shape
    tm = TOKEN_TILE
    row = lambda b, j: (b, j, 0)
    per_b = lambda b, j: (b, 0, 0)
    const2 = lambda b, j: (0, 0)
    return pl.pallas_call(
        _ml_out_kernel,
        out_shape=jax.ShapeDtypeStruct((B, S, D), F32),
        grid=(B, S // tm),
        in_specs=[
            pl.BlockSpec((1, tm, D), row),
            pl.BlockSpec((1, tm, D), row),
            pl.BlockSpec((D, D), const2),
            pl.BlockSpec((1, tm, D), row),
            pl.BlockSpec((1, 1, D), per_b),
        ],
        out_specs=pl.BlockSpec((1, tm, D), row),
        compiler_params=_cparams("parallel", "parallel"),
        name="mlstm_out_proj",
    )(hh, og, w_out_bf, x, g1)


def _first_argmax(x, lane, width):
    mx = jnp.max(x, axis=-1, keepdims=True)
    idx = jnp.min(jnp.where(x == mx, lane, width), axis=-1, keepdims=True)
    return mx, idx


def _router_kernel(x_ref, gain_ref, sc_ref, sh_ref, whi_ref, wlo_ref, b_ref, h_ref, r_ref, cnt_ref, run_ref):
    tm = x_ref.shape[1]

    @pl.when(jnp.logical_and(pl.program_id(0) == 0, pl.program_id(1) == 0))
    def _():
        run_ref[...] = jnp.zeros_like(run_ref)

    h = _modulated_norm(x_ref[0], gain_ref[...], sc_ref[0], sh_ref[0])
    h_hi = h.astype(BF16)
    bits = lax.bitcast_convert_type(h_hi.astype(F32), jnp.uint32)
    half = bits.shape[1] // 2
    packed = (bits[:, :half] >> 16) | (bits[:, half:] & jnp.uint32(0xFFFF0000))
    h_ref[0] = lax.bitcast_convert_type(packed, jnp.int32)
    h_lo = (h - h_hi.astype(F32)).astype(BF16)
    whi = whi_ref[...]
    logits = _dot(h_hi, whi) + _dot(h_lo, whi) + _dot(h_hi, wlo_ref[...]) + b_ref[...]
    lane = lax.broadcasted_iota(jnp.int32, (1, LANES), 1)
    lg = jnp.where(lane < N_GROUPS, logits, NEG_INF)
    g_max, g_sel = _first_argmax(lg, lane, LANES)
    pg = 1.0 / jnp.sum(jnp.exp(lg - g_max), axis=-1, keepdims=True)
    e_lane = lane - N_GROUPS
    in_grp = jnp.logical_and(e_lane >= g_sel * EXPERTS_PER_GROUP, e_lane < (g_sel + 1) * EXPERTS_PER_GROUP)
    le = jnp.where(in_grp, logits, NEG_INF)
    v1, i1 = _first_argmax(le, lane, LANES)
    le2 = jnp.where(lane == i1, NEG_INF, le)
    v2, i2 = _first_argmax(le2, lane, LANES)
    e2 = jnp.exp(v2 - v1)
    w1 = pg / (1.0 + e2)
    w2 = pg * e2 / (1.0 + e2)
    e1 = i1 - N_GROUPS
    e2 = i2 - N_GROUPS
    hot1 = lane == e1
    hot2 = lane == e2
    onehot = jnp.where(jnp.logical_or(hot1, hot2), 1.0, 0.0)
    r_i = lax.broadcasted_iota(jnp.int32, (tm, tm), 0)
    c_i = lax.broadcasted_iota(jnp.int32, (tm, tm), 1)
    before = jnp.where(c_i < r_i, 1.0, 0.0).astype(BF16)
    seen = _dot(before, onehot.astype(BF16)) + run_ref[...]
    rank1 = jnp.sum(jnp.where(hot1, seen, 0.0), axis=-1, keepdims=True)
    rank2 = jnp.sum(jnp.where(hot2, seen, 0.0), axis=-1, keepdims=True)
    run_ref[...] = run_ref[...] + jnp.sum(onehot, axis=0, keepdims=True)
    cnt_ref[...] = run_ref[...]

    out = jnp.where(lane == 0, e1.astype(F32), 0.0)
    out = jnp.where(lane == 1, e2.astype(F32), out)
    out = jnp.where(lane == 2, w1, out)
    out = jnp.where(lane == 3, w2, out)
    out = jnp.where(lane == 4, rank1, out)
    out = jnp.where(lane == 5, rank2, out)
    r_ref[0] = out


def _router(x, gain, sc, sh, w_hi, w_lo, bias):
    B, S, D = x.shape
    tm = TOKEN_TILE
    row = lambda b, j: (b, j, 0)
    per_b = lambda b, j: (b, 0, 0)
    const2 = lambda b, j: (0, 0)
    return pl.pallas_call(
        _router_kernel,
        out_shape=(
            jax.ShapeDtypeStruct((B, S, D // 2), jnp.int32),
            jax.ShapeDtypeStruct((B, S, LANES), F32),
            jax.ShapeDtypeStruct((1, LANES), F32),
        ),
        grid=(B, S // tm),
        in_specs=[
            pl.BlockSpec((1, tm, D), row),
            pl.BlockSpec((1, D), const2),
            pl.BlockSpec((1, 1, D), per_b),
            pl.BlockSpec((1, 1, D), per_b),
            pl.BlockSpec((D, LANES), const2),
            pl.BlockSpec((D, LANES), const2),
            pl.BlockSpec((1, LANES), const2),
        ],
        out_specs=(
            pl.BlockSpec((1, tm, D // 2), row),
            pl.BlockSpec((1, tm, LANES), row),
            pl.BlockSpec((1, LANES), const2),
        ),
        scratch_shapes=[pltpu.VMEM((1, LANES), F32)],
        compiler_params=_cparams("arbitrary", "arbitrary"),
        name="moe_router",
    )(x, gain, sc, sh, w_hi, w_lo, bias)


def _experts_kernel(blk_e_ref, n_used_ref, x_ref, wg_ref, wu_ref, wd_ref, o_ref, wg_s, wu_s, wd_s):
    i = pl.program_id(0)
    used = i < n_used_ref[0]
    new_expert = jnp.logical_or(i == 0, blk_e_ref[i] != blk_e_ref[jnp.maximum(i - 1, 0)])

    @pl.when(jnp.logical_and(used, new_expert))
    def _():
        wg_s[...] = wg_ref[0, 0].astype(BF16)
        wu_s[...] = wu_ref[0, 0].astype(BF16)
        wd_s[...] = wd_ref[0, 0].astype(BF16)

    @pl.when(used)
    def _():
        words = lax.bitcast_convert_type(x_ref[...], jnp.uint32)
        x = jnp.concatenate(
            [lax.bitcast_convert_type(words << 16, F32),
             lax.bitcast_convert_type(words & jnp.uint32(0xFFFF0000), F32)], axis=1).astype(BF16)
        a = _dot(x, wg_s[...])
        u = _dot(x, wu_s[...])
        act = a * jax.nn.sigmoid(a) * u
        o_ref[...] = _dot(act.astype(BF16), wd_s[...])

    @pl.when(i >= n_used_ref[0])
    def _():
        o_ref[...] = jnp.zeros_like(o_ref)


def _experts(layer, blk_e, n_used, xs, w_gate, w_up, w_down):
    R = xs.shape[0]
    D = 2 * xs.shape[1]
    n_blk = R // MOE_BLOCK
    grid_spec = pltpu.PrefetchScalarGridSpec(
        num_scalar_prefetch=2,
        grid=(n_blk,),
        in_specs=[
            pl.BlockSpec((MOE_BLOCK, D // 2), lambda i, be, nu: (i, 0)),
            pl.BlockSpec((1, 1, D, D_EXPERT), lambda i, be, nu: (layer, be[i], 0, 0)),
            pl.BlockSpec((1, 1, D, D_EXPERT), lambda i, be, nu: (layer, be[i], 0, 0)),
            pl.BlockSpec((1, 1, D_EXPERT, D), lambda i, be, nu: (layer, be[i], 0, 0)),
        ],
        out_specs=pl.BlockSpec((MOE_BLOCK, D), lambda i, be, nu: (i, 0)),
        scratch_shapes=[
            pltpu.VMEM((D, D_EXPERT), BF16),
            pltpu.VMEM((D, D_EXPERT), BF16),
            pltpu.VMEM((D_EXPERT, D), BF16),
        ],
    )
    return pl.pallas_call(
        _experts_kernel,
        out_shape=jax.ShapeDtypeStruct((R, D), F32),
        grid_spec=grid_spec,
        compiler_params=_cparams("arbitrary"),
        name="moe_experts",
    )(blk_e, n_used, xs, w_gate, w_up, w_down)


def _combine_kernel(x_ref, g_ref, y0_ref, y1_ref, r_ref, o_ref):
    r = r_ref[0]
    y = y0_ref[0, 0] * r[:, 2:3] + y1_ref[0, 0] * r[:, 3:4]
    o_ref[0] = x_ref[0] + g_ref[0] * y


def _combine(x, g2, y01, route):
    B, S, D = x.shape
    tm = TOKEN_TILE
    row = lambda b, j: (b, j, 0)
    per_b = lambda b, j: (b, 0, 0)
    return pl.pallas_call(
        _combine_kernel,
        out_shape=jax.ShapeDtypeStruct((B, S, D), F32),
        grid=(B, S // tm),
        in_specs=[
            pl.BlockSpec((1, tm, D), row),
            pl.BlockSpec((1, 1, D), per_b),
            pl.BlockSpec((1, 1, tm, D), lambda b, j: (0, b, j, 0)),
            pl.BlockSpec((1, 1, tm, D), lambda b, j: (1, b, j, 0)),
            pl.BlockSpec((1, tm, LANES), row),
        ],
        out_specs=pl.BlockSpec((1, tm, D), row),
        compiler_params=_cparams("parallel", "parallel"),
        name="moe_combine",
    )(x, g2, y01, y01, route)


SC_CORES = 2
SC_SUBCORES = 16
SC_WORKERS = SC_CORES * SC_SUBCORES
SC_CHUNK = 32


def _sc_mesh():
    return plsc.VectorSubcoreMesh(core_axis_name="c", subcore_axis_name="s",
                                  num_cores=SC_CORES, num_subcores=SC_SUBCORES)


def _sc_scatter_rows(src, idx, n_out):
    T, W = src.shape
    per_w = T // SC_WORKERS
    nch = per_w // SC_CHUNK
    idx4 = idx.reshape(TOP_K, SC_WORKERS, nch, SC_CHUNK)

    @functools.partial(
        pl.kernel, mesh=_sc_mesh(),
        out_type=jax.ShapeDtypeStruct((n_out, W), src.dtype),
        scratch_types=[pltpu.VMEM((TOP_K, nch, SC_CHUNK), jnp.int32), pltpu.VMEM((SC_CHUNK, W), src.dtype)],
        name="sc_scatter_rows",
    )
    def body(src_hbm, idx_hbm, out_hbm, idx_v, rows_v):
        wid = lax.axis_index("s") * SC_CORES + lax.axis_index("c")
        for s in range(TOP_K):
            pltpu.sync_copy(idx_hbm.at[s, wid], idx_v.at[s])

        @pl.loop(0, nch)
        def _(i):
            pltpu.sync_copy(src_hbm.at[pl.ds(wid * per_w + i * SC_CHUNK, SC_CHUNK)], rows_v)
            for s in range(TOP_K):
                pltpu.sync_copy(rows_v, out_hbm.at[idx_v.at[s, i]])

    return body(src, idx4)


def _sc_gather_rows(table, idx):
    N = idx.shape[0]
    W = table.shape[1]
    per_w = N // SC_WORKERS
    nch = per_w // SC_CHUNK
    idx3 = idx.reshape(SC_WORKERS, nch, SC_CHUNK)

    @functools.partial(
        pl.kernel, mesh=_sc_mesh(),
        out_type=jax.ShapeDtypeStruct((N, W), table.dtype),
        scratch_types=[pltpu.VMEM((nch, SC_CHUNK), jnp.int32), pltpu.VMEM((SC_CHUNK, W), table.dtype)],
        name="sc_gather_rows",
    )
    def body(table_hbm, idx_hbm, out_hbm, idx_v, rows_v):
        wid = lax.axis_index("s") * SC_CORES + lax.axis_index("c")
        pltpu.sync_copy(idx_hbm.at[wid], idx_v)

        @pl.loop(0, nch)
        def _(i):
            pltpu.sync_copy(table_hbm.at[idx_v.at[i]], rows_v)
            pltpu.sync_copy(rows_v, out_hbm.at[pl.ds(wid * per_w + i * SC_CHUNK, SC_CHUNK)])

    return body(table, idx3)


def _moe_dispatch(route, counts, T):
    A = T * TOP_K
    counts = counts[0, :N_EXPERTS].astype(jnp.int32)
    blocks_per = (counts + MOE_BLOCK - 1) // MOE_BLOCK
    block_end = jnp.cumsum(blocks_per)
    block_start = block_end - blocks_per
    expert = route[:, :TOP_K].astype(jnp.int32)
    rank = route[:, 4:4 + TOP_K].astype(jnp.int32)
    onehot = expert[:, :, None] == jnp.arange(N_EXPERTS, dtype=jnp.int32)
    start = jnp.sum(jnp.where(onehot, block_start, 0), axis=-1)
    dest = (start * MOE_BLOCK + rank).T
    n_blk = -(-A // MOE_BLOCK) + N_EXPERTS
    blk = jnp.arange(n_blk, dtype=jnp.int32)
    blk_e = jnp.minimum(jnp.sum(blk[:, None] >= block_end[None, :], axis=-1), N_EXPERTS - 1)
    return dest, n_blk * MOE_BLOCK, blk_e.astype(jnp.int32), block_end[-1:].astype(jnp.int32)


def _rope_tables(S):
    inv = 1.0 / (ROPE_THETA ** (jnp.arange(0, HEAD_DIM, 2, dtype=F32) / HEAD_DIM))
    ang = jnp.arange(S, dtype=F32)[:, None] * inv[None, :]
    cos, sin = jnp.cos(ang), jnp.sin(ang)
    cos_h = jnp.concatenate([cos, cos], axis=-1)
    sin_h = jnp.concatenate([-sin, sin], axis=-1)
    return jnp.tile(cos_h, (1, ATTN_HEADS)), jnp.tile(sin_h, (1, ATTN_HEADS))


def _pad_cols(w, width):
    return jnp.pad(w, ((0, 0), (0, width - w.shape[1])))


def kernel(x, c, ada_w, ada_b, norm_mix, norm_ffn, hy_w_in, hy_q_norm, hy_k_norm, hy_conv_w, hy_w_out, ml_w_in, ml_b_gates, ml_out_norm, ml_w_out, moe_w_group, moe_b_group, moe_w_expert, moe_b_expert, moe_w_gate, moe_w_up, moe_w_down):
    B, S, D = x.shape
    T = B * S
    cos_t, sin_t = _rope_tables(S)
    mod = _ada_modulation(c, ada_w, ada_b).reshape(DEPTH, B, 6, 1, D)
    r_i = np.arange(ATTN_WIDTH)
    grp = jnp.asarray((r_i[:, None] // HEAD_DIM) == (r_i[None, :] // HEAD_DIM), dtype=BF16)

    for l in range(DEPTH):
        sh1, sc1, g1, sh2, sc2, g2 = [mod[l, :, i] for i in range(6)]
        gain1 = norm_mix[l].reshape(1, D)
        j = l // 2
        if l % 2 == 0:
            w = hy_w_in[j]
            o = np.cumsum((0,) + (ATTN_WIDTH, HEAD_DIM, HEAD_DIM, IDX_HEADS * IDX_DIM, IDX_DIM, IDX_HEADS,
                                  CONV_WIDTH, CONV_WIDTH, CONV_WIDTH))
            wq, wk, wv, wiq, wik, wiw, wbg, wcg, wu = [w[:, o[i]:o[i + 1]] for i in range(9)]
            w_pad = jnp.concatenate(
                [wq, wiq, wbg, wcg, wu, wk, wv, _pad_cols(jnp.concatenate([wik, wiw], axis=1), LANES)],
                axis=1).astype(BF16)
            qn_t = jnp.tile(hy_q_norm[j], ATTN_HEADS).reshape(1, ATTN_WIDTH)
            kn_t = jnp.tile(hy_k_norm[j], LANES // HEAD_DIM).reshape(1, LANES)
            q, iq, bcu, kv, ik, iw = _hyb_in(x, gain1, sc1, sh1, w_pad, cos_t, sin_t, qn_t, kn_t, grp)
            y_attn = _dsa_attention(q, iq, iw, kv, ik)
            x = _hyb_out(y_attn, bcu, hy_conv_w[j], hy_w_out[j].astype(BF16), x, g1)
        else:
            w = ml_w_in[j]
            hq = ML_HEADS * ML_QK_DIM
            hv = ML_HEADS * ML_V_DIM
            wq, wk, wv = w[:, :hq], w[:, hq:2 * hq], w[:, 2 * hq:2 * hq + hv]
            wg = w[:, 2 * hq + hv:2 * hq + hv + 2 * ML_HEADS]
            wo = w[:, 2 * hq + hv + 2 * ML_HEADS:]
            w_pad = jnp.concatenate([wq, wk, wv, wo, _pad_cols(wg, LANES)], axis=1).astype(BF16)
            gate_bias = jnp.pad(ml_b_gates[j], (0, LANES - 2 * ML_HEADS)).reshape(1, LANES)
            q, k, v, og, gates = _ml_in(x, gain1, sc1, sh1, w_pad, gate_bias)
            nc = S // ML_CHUNK
            hp = ML_STEP_HEADS
            groups = ML_HEADS // hp
            gi = gates[:, :, :ML_HEADS].reshape(B, S, groups, hp)
            gf = gates[:, :, ML_HEADS:2 * ML_HEADS].reshape(B, S, groups, hp)
            gp = jnp.concatenate([gi, gf], axis=-1)
            gp = jnp.transpose(gp, (0, 2, 1, 3)).reshape(B, groups, nc, ML_CHUNK, 2 * hp)
            g_cols = jnp.pad(gp, ((0, 0),) * 4 + ((0, LANES - 2 * hp),))
            g_rows = jnp.swapaxes(gp, -1, -2)
            hh = _mlstm(q, k, v, g_rows, g_cols, ml_out_norm[j].reshape(1, hv))
            x = _ml_out(hh, og, ml_w_out[j].astype(BF16), x, g1)

        w_r = _pad_cols(jnp.concatenate([moe_w_group[l], moe_w_expert[l]], axis=1), LANES)
        w_hi = w_r.astype(BF16)
        w_lo = (w_r - w_hi.astype(F32)).astype(BF16)
        b_r = jnp.pad(jnp.concatenate([moe_b_group[l], moe_b_expert[l]]), (0, LANES - N_GROUPS - N_EXPERTS))
        h2, route, counts = _router(x, norm_ffn[l].reshape(1, D), sc2, sh2, w_hi, w_lo, b_r.reshape(1, LANES))
        dest, n_rows, blk_e, n_used = _moe_dispatch(route.reshape(T, LANES), counts, T)
        xs = _sc_scatter_rows(h2.reshape(T, D // 2), dest, n_rows)
        ys = _experts(l, blk_e, n_used, xs, moe_w_gate, moe_w_up, moe_w_down)
        y01 = _sc_gather_rows(ys, dest.reshape(TOP_K * T)).reshape(TOP_K, B, S, D)
        x = _combine(x, g2, y01, route)
    return x
```

```python
import functools

import numpy as np
import jax
import jax.numpy as jnp
from jax import lax
from jax.experimental import pallas as pl
from jax.experimental.pallas import tpu as pltpu
from jax.experimental.pallas import tpu_sc as plsc

F32 = jnp.float32
BF16 = jnp.bfloat16
HIGHEST = lax.Precision.HIGHEST

D_MODEL = 1024
DEPTH = 4
ATTN_HEADS = 8
HEAD_DIM = 64
ATTN_WIDTH = ATTN_HEADS * HEAD_DIM
IDX_HEADS = 8
IDX_DIM = 64
INDEX_TOPK = 256
Q_BLOCK = 128
ROPE_THETA = 10000.0
CONV_WIDTH = D_MODEL - ATTN_WIDTH
CONV_K = 3
ML_HEADS = 8
ML_QK_DIM = 64
ML_V_DIM = 128
N_GROUPS = 4
EXPERTS_PER_GROUP = 8
N_EXPERTS = N_GROUPS * EXPERTS_PER_GROUP
TOP_K = 2
D_EXPERT = 512
MOE_BLOCK = 256
NORM_EPS = 1e-6

LANES = 128
VMEM_LIMIT = 56 * 1024 * 1024
TOKEN_TILE = 256
ML_CHUNK = 256
ML_STEP_HEADS = 4
NEG_INF = float("-inf")


def _cparams(*sem):
    return pltpu.CompilerParams(dimension_semantics=sem, vmem_limit_bytes=VMEM_LIMIT)


def _dot(a, b):
    return jnp.dot(a, b, preferred_element_type=F32)


def _dot_t(a, b):
    return lax.dot_general(a, b, (((1,), (1,)), ((), ())), preferred_element_type=F32)


def _split_dot(a_f32, b_bf16):
    hi = a_f32.astype(BF16)
    lo = (a_f32 - hi.astype(F32)).astype(BF16)
    return _dot(hi, b_bf16) + _dot(lo, b_bf16)


def _ada_kernel(c_ref, w_ref, b_ref, o_ref):
    c = c_ref[...]
    ca = c * jax.nn.sigmoid(c)
    o_ref[0] = jnp.dot(ca, w_ref[0], precision=HIGHEST, preferred_element_type=F32) + b_ref[0]


def _ada_modulation(c, ada_w, ada_b):
    B, D = c.shape
    n_col = ada_w.shape[-1] // D
    return pl.pallas_call(
        _ada_kernel,
        out_shape=jax.ShapeDtypeStruct((DEPTH, B, n_col * D), F32),
        grid=(DEPTH, n_col),
        in_specs=[
            pl.BlockSpec((B, D), lambda l, j: (0, 0)),
            pl.BlockSpec((1, D, D), lambda l, j: (l, 0, j)),
            pl.BlockSpec((1, 1, D), lambda l, j: (l, 0, j)),
        ],
        out_specs=pl.BlockSpec((1, B, D), lambda l, j: (l, 0, j)),
        compiler_params=_cparams("parallel", "parallel"),
        name="ada_modulation",
    )(c, ada_w, ada_b.reshape(DEPTH, 1, n_col * D))


def _modulated_norm(x, gain, scale, shift):
    y = x * lax.rsqrt(jnp.mean(x * x, axis=-1, keepdims=True) + NORM_EPS)
    return y * gain * (1.0 + scale) + shift


def _rope(x, cos, sin_signed, first_half):
    w = x.shape[-1]
    partner = jnp.where(first_half, pltpu.roll(x, w - HEAD_DIM // 2, 1), pltpu.roll(x, HEAD_DIM // 2, 1))
    return x * cos + partner * sin_signed


HYB_COLS = 5 * 512 + 2 * LANES


def _hyb_in_kernel(x_ref, gain_ref, sc_ref, sh_ref, w_ref, cos_ref, sin_ref, qn_ref, kn_ref, grp_ref,
                   q_ref, iq_ref, bcu_ref, kv_ref, ik_ref, iw_ref):
    h = _modulated_norm(x_ref[0], gain_ref[...], sc_ref[0], sh_ref[0])
    p = _dot(h.astype(BF16), w_ref[...])
    cos = cos_ref[...]
    sin = sin_ref[...]
    lane = lax.broadcasted_iota(jnp.int32, (1, ATTN_WIDTH), 1)
    first_half = (lane % HEAD_DIM) < (HEAD_DIM // 2)
    fh128 = first_half[:, :LANES]
    lane128 = lane[:, :LANES]

    q = p[:, 0:512]
    ms = _split_dot(q * q, grp_ref[...]) * (1.0 / HEAD_DIM)
    q = q * lax.rsqrt(ms + NORM_EPS) * qn_ref[...]
    q_ref[0] = (_rope(q, cos, sin, first_half) * (HEAD_DIM ** -0.5)).astype(BF16)

    iq = p[:, 512:1024]
    iq_ref[0] = (_rope(iq, cos, sin, first_half) * (IDX_DIM ** -0.5)).astype(BF16)

    bcu_ref[0] = p[:, 1024:2560]

    kv = p[:, 2560:2688]
    is_k = lane128 < HEAD_DIM
    kk = jnp.where(is_k, kv, 0.0)
    ms_k = jnp.sum(kk * kk, axis=-1, keepdims=True) * (1.0 / HEAD_DIM)
    kn = kv * lax.rsqrt(ms_k + NORM_EPS) * kn_ref[...]
    kr = _rope(kn, cos[:, :LANES], sin[:, :LANES], fh128)
    kv_ref[0] = jnp.where(is_k, kr, kv).astype(BF16)

    sm = p[:, 2688:2816]
    ikr = _rope(sm, cos[:, :LANES], sin[:, :LANES], fh128)
    ik_ref[0] = jnp.where(is_k, ikr, 0.0).astype(BF16)
    iw_ref[0] = sm


def _hyb_in(x, gain, sc, sh, w_pad, cos_t, sin_t, qn_t, kn_t, grp):
    B, S, D = x.shape
    tm = TOKEN_TILE
    row = lambda b, j: (b, j, 0)
    per_b = lambda b, j: (b, 0, 0)
    const2 = lambda b, j: (0, 0)
    tab = lambda b, j: (j, 0)
    return pl.pallas_call(
        _hyb_in_kernel,
        out_shape=(
            jax.ShapeDtypeStruct((B, S, 512), BF16),
            jax.ShapeDtypeStruct((B, S, 512), BF16),
            jax.ShapeDtypeStruct((B, S, 1536), F32),
            jax.ShapeDtypeStruct((B, S, LANES), BF16),
            jax.ShapeDtypeStruct((B, S, LANES), BF16),
            jax.ShapeDtypeStruct((B, S, LANES), F32),
        ),
        grid=(B, S // tm),
        in_specs=[
            pl.BlockSpec((1, tm, D), row),
            pl.BlockSpec((1, D), const2),
            pl.BlockSpec((1, 1, D), per_b),
            pl.BlockSpec((1, 1, D), per_b),
            pl.BlockSpec((D, HYB_COLS), const2),
            pl.BlockSpec((tm, 512), tab),
            pl.BlockSpec((tm, 512), tab),
            pl.BlockSpec((1, 512), const2),
            pl.BlockSpec((1, LANES), const2),
            pl.BlockSpec((512, 512), const2),
        ],
        out_specs=(
            pl.BlockSpec((1, tm, 512), row),
            pl.BlockSpec((1, tm, 512), row),
            pl.BlockSpec((1, tm, 1536), row),
            pl.BlockSpec((1, tm, LANES), row),
            pl.BlockSpec((1, tm, LANES), row),
            pl.BlockSpec((1, tm, LANES), row),
        ),
        compiler_params=_cparams("parallel", "parallel"),
        name="hybrid_in_proj",
    )(x, gain, sc, sh, w_pad, cos_t, sin_t, qn_t, kn_t, grp)


def _count(mask):
    return jnp.sum(jnp.where(mask, 1.0, 0.0), axis=-1, keepdims=True)


DSA_KEY_SPANS = 8


def _dsa_kernel(q_ref, iq_ref, iw_ref, kv_ref, ik_ref, o_ref, sc_ref, bias_ref):
    S = kv_ref.shape[1]
    span = S // DSA_KEY_SPANS
    qb = pl.program_id(1)
    for v in range(DSA_KEY_SPANS):
        sk = (v + 1) * span

        @pl.when(qb // (span // Q_BLOCK) == v)
        def _():
            _dsa_body(q_ref, iq_ref, iw_ref, kv_ref.at[:, :sk, :], ik_ref.at[:, :sk, :], o_ref,
                      sc_ref.at[:, :sk], bias_ref.at[:, :sk])


def _dsa_body(q_ref, iq_ref, iw_ref, kv_ref, ik_ref, o_ref, sc_ref, bias_ref):
    S = kv_ref.shape[1]
    qb = pl.program_id(1)
    kf = float(INDEX_TOPK)

    ik = ik_ref[0][:, :IDX_DIM]
    iw = iw_ref[0]
    acc = jnp.zeros((Q_BLOCK, S), F32)
    for hd in range(IDX_HEADS):
        s_h = _dot_t(iq_ref[0][:, hd * IDX_DIM:(hd + 1) * IDX_DIM], ik)
        w_h = iw[:, IDX_DIM + hd:IDX_DIM + hd + 1] * (IDX_HEADS ** -0.5)
        acc = acc + jnp.maximum(s_h, 0.0) * w_h
    qpos = qb * Q_BLOCK + lax.broadcasted_iota(jnp.int32, (Q_BLOCK, 1), 0)
    kpos = lax.broadcasted_iota(jnp.int32, (1, S), 1)
    causal = kpos <= qpos
    sc_ref[...] = jnp.where(causal, acc, NEG_INF)

    @pl.when(qb * Q_BLOCK + Q_BLOCK <= INDEX_TOPK)
    def _():
        bias_ref[...] = jnp.where(causal, 0.0, NEG_INF)

    @pl.when(qb * Q_BLOCK + Q_BLOCK > INDEX_TOPK)
    def _():
        sc = sc_ref[...]
        row_max = jnp.max(sc, axis=-1, keepdims=True)
        row_min = jnp.min(jnp.where(causal, sc, jnp.inf), axis=-1, keepdims=True)
        top_tied = _count(sc >= row_max) >= kf

        def bisect(_, carry):
            lo, hi = carry
            mid = 0.5 * lo + 0.5 * hi
            ge = _count(sc_ref[...] >= mid) >= kf
            return jnp.where(ge, mid, lo), jnp.where(ge, hi, mid)

        lo, hi = lax.fori_loop(0, 18, bisect, (row_min, row_max))

        def refine_cond(carry):
            it, _, _, done = carry
            return jnp.logical_and(it < S, jnp.min(done) < 0.5)

        def refine(carry):
            it, hi, thr, done = carry
            scv = sc_ref[...]
            m = jnp.max(jnp.where(scv < hi, scv, NEG_INF), axis=-1, keepdims=True)
            hit = _count(scv >= m) >= kf
            fin = done > 0.5
            thr = jnp.where(fin, thr, m)
            hi = jnp.where(jnp.logical_or(fin, hit), hi, m)
            done = jnp.where(hit, 1.0, done)
            return it + 1, hi, thr, done

        done0 = jnp.where(top_tied, 1.0, 0.0)
        _, _, thr, _ = lax.while_loop(refine_cond, refine, (jnp.int32(0), hi, row_max, done0))

        gt = sc > thr
        eq = sc == thr
        need = kf - _count(gt)
        n_eq = _count(eq)
        bias_ref[...] = jnp.where(sc >= thr, 0.0, NEG_INF)

        @pl.when(jnp.max(n_eq - need) > 0.5)
        def _():
            r_i = lax.broadcasted_iota(jnp.int32, (LANES, LANES), 0)
            c_i = lax.broadcasted_iota(jnp.int32, (LANES, LANES), 1)
            upper = jnp.where(r_i <= c_i, 1.0, 0.0).astype(BF16)
            carry_cnt = jnp.zeros((Q_BLOCK, 1), F32)
            for j in range(S // LANES):
                blk = sc_ref[:, j * LANES:(j + 1) * LANES]
                eq_j = blk == thr
                eq_f = jnp.where(eq_j, 1.0, 0.0)
                rank = _dot(eq_f.astype(BF16), upper) + carry_cnt
                keep = jnp.logical_or(blk > thr, jnp.logical_and(eq_j, rank <= need))
                bias_ref[:, j * LANES:(j + 1) * LANES] = jnp.where(keep, 0.0, NEG_INF)
                carry_cnt = carry_cnt + jnp.sum(eq_f, axis=-1, keepdims=True)

    k = kv_ref[0][:, :HEAD_DIM]
    v = kv_ref[0][:, HEAD_DIM:]
    bias = bias_ref[...]
    for hd in range(ATTN_HEADS):
        logits = _dot_t(q_ref[0][:, hd * HEAD_DIM:(hd + 1) * HEAD_DIM], k) + bias
        mx = jnp.max(logits, axis=-1, keepdims=True)
        p = jnp.exp(logits - mx)
        den = jnp.sum(p, axis=-1, keepdims=True)
        o_ref[0, :, hd * HEAD_DIM:(hd + 1) * HEAD_DIM] = _dot(p.astype(BF16), v) / den


def _dsa_attention(q, iq, iw, kv, ik):
    B, S, _ = q.shape
    blk = lambda b, j: (b, j, 0)
    per_b = lambda b, j: (b, 0, 0)
    return pl.pallas_call(
        _dsa_kernel,
        out_shape=jax.ShapeDtypeStruct((B, S, ATTN_WIDTH), F32),
        grid=(B, S // Q_BLOCK),
        in_specs=[
            pl.BlockSpec((1, Q_BLOCK, ATTN_WIDTH), blk),
            pl.BlockSpec((1, Q_BLOCK, IDX_HEADS * IDX_DIM), blk),
            pl.BlockSpec((1, Q_BLOCK, LANES), blk),
            pl.BlockSpec((1, S, LANES), per_b),
            pl.BlockSpec((1, S, LANES), per_b),
        ],
        out_specs=pl.BlockSpec((1, Q_BLOCK, ATTN_WIDTH), blk),
        scratch_shapes=[pltpu.VMEM((Q_BLOCK, S), F32), pltpu.VMEM((Q_BLOCK, S), F32)],
        compiler_params=_cparams("parallel", "parallel"),
        name="dsa_attention",
    )(q, iq, iw, kv, ik)


def _hyb_out_kernel(ya_ref, bcu_ref, halo_ref, cw_ref, w_ref, x_ref, g_ref, o_ref):
    j = pl.program_id(1)
    tm = ya_ref.shape[1]
    bcu = bcu_ref[0]
    bg = bcu[:, 0:512]
    z = bcu[:, 512:1024] * bcu[:, 1024:1536]
    halo = halo_ref[0]
    zh = halo[:, 512:1024] * halo[:, 1024:1536]
    zh = jnp.where(j > 0, zh, 0.0)
    row = lax.broadcasted_iota(jnp.int32, (tm, 1), 0)
    z1 = jnp.where(row >= 1, pltpu.roll(z, 1, 0), zh[7:8, :])
    z2 = jnp.where(row >= 2, pltpu.roll(z, 2, 0), jnp.where(row == 1, zh[7:8, :], zh[6:7, :]))
    cw = cw_ref[...]
    y_conv = bg * (z2 * cw[0:1, :] + z1 * cw[1:2, :] + z * cw[2:3, :])
    y = _dot(ya_ref[0].astype(BF16), w_ref[0:512, :]) + _dot(y_conv.astype(BF16), w_ref[512:1024, :])
    o_ref[0] = x_ref[0] + g_ref[0] * y


def _hyb_out(y_attn, bcu, conv_w, w_out_bf, x, g1):
    B, S, D = x.shape
    tm = TOKEN_TILE
    row = lambda b, j: (b, j, 0)
    per_b = lambda b, j: (b, 0, 0)
    const2 = lambda b, j: (0, 0)
    halo = lambda b, j: (b, jnp.maximum(j * (tm // 8) - 1, 0), 0)
    return pl.pallas_call(
        _hyb_out_kernel,
        out_shape=jax.ShapeDtypeStruct((B, S, D), F32),
        grid=(B, S // tm),
        in_specs=[
            pl.BlockSpec((1, tm, 512), row),
            pl.BlockSpec((1, tm, 1536), row),
            pl.BlockSpec((1, 8, 1536), halo),
            pl.BlockSpec((CONV_K, CONV_WIDTH), const2),
            pl.BlockSpec((D, D), const2),
            pl.BlockSpec((1, tm, D), row),
            pl.BlockSpec((1, 1, D), per_b),
        ],
        out_specs=pl.BlockSpec((1, tm, D), row),
        compiler_params=_cparams("parallel", "parallel"),
        name="hybrid_out_proj",
    )(y_attn, bcu, bcu, conv_w, w_out_bf, x, g1)


ML_COLS = 512 + 512 + 1024 + 1024 + LANES


def _ml_in_kernel(x_ref, gain_ref, sc_ref, sh_ref, w_ref, bias_ref, q_ref, k_ref, v_ref, og_ref, gt_ref):
    h = _modulated_norm(x_ref[0], gain_ref[...], sc_ref[0], sh_ref[0])
    p = _dot(h.astype(BF16), w_ref[...])
    q_ref[0] = (p[:, 0:512] * (ML_QK_DIM ** -0.5)).astype(BF16)
    k_ref[0, 0] = p[:, 512:1024].T.astype(BF16)
    v_ref[0] = p[:, 1024:2048].astype(BF16)
    og_ref[0] = p[:, 2048:3072]
    gt_ref[0] = p[:, 3072:3200] + bias_ref[...]


def _ml_in(x, gain, sc, sh, w_pad, gate_bias):
    B, S, D = x.shape
    tm = TOKEN_TILE
    assert tm == ML_CHUNK
    row = lambda b, j: (b, j, 0)
    per_b = lambda b, j: (b, 0, 0)
    const2 = lambda b, j: (0, 0)
    return pl.pallas_call(
        _ml_in_kernel,
        out_shape=(
            jax.ShapeDtypeStruct((B, S, 512), BF16),
            jax.ShapeDtypeStruct((B, S // tm, 512, tm), BF16),
            jax.ShapeDtypeStruct((B, S, 1024), BF16),
            jax.ShapeDtypeStruct((B, S, 1024), F32),
            jax.ShapeDtypeStruct((B, S, LANES), F32),
        ),
        grid=(B, S // tm),
        in_specs=[
            pl.BlockSpec((1, tm, D), row),
            pl.BlockSpec((1, D), const2),
            pl.BlockSpec((1, 1, D), per_b),
            pl.BlockSpec((1, 1, D), per_b),
            pl.BlockSpec((D, ML_COLS), const2),
            pl.BlockSpec((1, LANES), const2),
        ],
        out_specs=(
            pl.BlockSpec((1, tm, 512), row),
            pl.BlockSpec((1, 1, 512, tm), lambda b, j: (b, j, 0, 0)),
            pl.BlockSpec((1, tm, 1024), row),
            pl.BlockSpec((1, tm, 1024), row),
            pl.BlockSpec((1, tm, LANES), row),
        ),
        compiler_params=_cparams("parallel", "parallel"),
        name="mlstm_in_proj",
    )(x, gain, sc, sh, w_pad, gate_bias)


def _log_sigmoid(f):
    return jnp.minimum(f, 0.0) - jnp.log1p(jnp.exp(-jnp.abs(f)))


def _split3(x):
    a = x.astype(BF16)
    r = x - a.astype(F32)
    b = r.astype(BF16)
    c = (r - b.astype(F32)).astype(BF16)
    return a, b, c


def _twice(a):
    return jnp.concatenate([a, a], axis=1)


def _mlstm_kernel(q_ref, kt_ref, v_ref, grow_ref, gcol_ref, gain_ref, o_ref, c_ref, m_ref):
    L = ML_CHUNK
    HP = ML_STEP_HEADS
    S = q_ref.shape[1]
    c_ref[...] = jnp.zeros_like(c_ref)
    m_ref[...] = jnp.zeros_like(m_ref)

    def chunk(c, carry):
        r0 = pl.multiple_of(c * L, L)
        r_i = lax.broadcasted_iota(jnp.int32, (L, L), 0)
        c_i = lax.broadcasted_iota(jnp.int32, (L, L), 1)
        tril = c_i <= r_i
        lower = jnp.where(tril, 1.0, 0.0).astype(BF16)
        upper = jnp.where(r_i <= c_i, 1.0, 0.0).astype(BF16)
        e_r = lax.broadcasted_iota(jnp.int32, (LANES, HP * LANES), 0)
        e_c = lax.broadcasted_iota(jnp.int32, (LANES, HP * LANES), 1)
        pick = jnp.where(e_r == HP + e_c // LANES, 1.0, 0.0).astype(BF16)
        rows = grow_ref[0, 0, c]
        cols = gcol_ref[0, 0, c]
        b_rows = sum(_dot(p, upper) for p in _split3(_log_sigmoid(rows)))
        b_cols = sum(_dot(lower, p) for p in _split3(_log_sigmoid(cols)))
        b_colr = sum(_dot(p, pick) for p in _split3(b_cols))
        lane = lax.broadcasted_iota(jnp.int32, (1, L), 1)
        b_last_all = jnp.sum(jnp.where(lane == L - 1, b_rows, 0.0), axis=-1, keepdims=True)
        ones_v = jnp.ones((L, ML_V_DIM), BF16)
        for hh in range(HP):
            q = q_ref[0, pl.ds(r0, L), hh * ML_QK_DIM:(hh + 1) * ML_QK_DIM]
            kt = kt_ref[0, c, hh * ML_QK_DIM:(hh + 1) * ML_QK_DIM, :]
            v = v_ref[0, pl.ds(r0, L), hh * ML_V_DIM:(hh + 1) * ML_V_DIM]
            vx = jnp.concatenate([v, ones_v], axis=1)
            i_row = rows[hh:hh + 1, :]
            b_row = b_rows[HP + hh:HP + hh + 1, :]
            b_last = b_last_all[HP + hh:HP + hh + 1, :]
            b_col = b_colr[:, hh * LANES:(hh + 1) * LANES]
            m_prev = m_ref[hh]
            ctn = c_ref[hh]

            dmat = jnp.where(tril, _twice(b_col) - b_row + i_row, NEG_INF)
            inter = b_col + m_prev
            m_t = jnp.maximum(inter, jnp.max(dmat, axis=-1, keepdims=True))
            w_intra = jnp.exp(dmat - _twice(m_t))
            w_inter = jnp.exp(inter - m_t)
            intra = (w_intra * _dot(q, kt)).astype(BF16)
            tot = _twice(w_inter) * _dot(q, ctn.astype(BF16)) + _dot(intra, vx)
            num = tot[:, :ML_V_DIM]
            den = tot[:, ML_V_DIM:]
            hc = num / jnp.maximum(jnp.abs(den), jnp.exp(-m_t))
            y = hc * lax.rsqrt(jnp.mean(hc * hc, axis=-1, keepdims=True) + NORM_EPS)
            o_ref[0, pl.ds(r0, L), hh * ML_V_DIM:(hh + 1) * ML_V_DIM] = (
                y * gain_ref[:, hh * ML_V_DIM:(hh + 1) * ML_V_DIM])

            g_row = b_last - b_row + i_row
            m_new = jnp.maximum(b_last + m_prev, jnp.max(g_row, axis=-1, keepdims=True))
            decay = jnp.exp(b_last + m_prev - m_new)
            kw = (kt.astype(F32) * jnp.exp(g_row - _twice(m_new))).astype(BF16)
            c_ref[hh] = _twice(decay) * ctn + _dot(kw, vx)
            m_ref[hh] = m_new
        return carry

    lax.fori_loop(0, S // L, chunk, 0)


def _mlstm(q, kt, v, g_rows, g_cols, out_gain):
    B, S, _ = q.shape
    assert ML_CHUNK == 2 * LANES
    nc = S // ML_CHUNK
    hp = ML_STEP_HEADS
    return pl.pallas_call(
        _mlstm_kernel,
        out_shape=jax.ShapeDtypeStruct((B, S, ML_HEADS * ML_V_DIM), F32),
        grid=(B, ML_HEADS // hp),
        in_specs=[
            pl.BlockSpec((1, S, hp * ML_QK_DIM), lambda b, p: (b, 0, p)),
            pl.BlockSpec((1, nc, hp * ML_QK_DIM, ML_CHUNK), lambda b, p: (b, 0, p, 0)),
            pl.BlockSpec((1, S, hp * ML_V_DIM), lambda b, p: (b, 0, p)),
            pl.BlockSpec((1, 1, nc, 2 * hp, ML_CHUNK), lambda b, p: (b, p, 0, 0, 0)),
            pl.BlockSpec((1, 1, nc, ML_CHUNK, LANES), lambda b, p: (b, p, 0, 0, 0)),
            pl.BlockSpec((1, hp * ML_V_DIM), lambda b, p: (0, p)),
        ],
        out_specs=pl.BlockSpec((1, S, hp * ML_V_DIM), lambda b, p: (b, 0, p)),
        scratch_shapes=[
            pltpu.VMEM((hp, ML_QK_DIM, ML_V_DIM + LANES), F32),
            pltpu.VMEM((hp, 1, LANES), F32),
        ],
        compiler_params=_cparams("parallel", "parallel"),
        name="mlstm_chunkwise",
    )(q, kt, v, g_rows, g_cols, out_gain)


def _ml_out_kernel(hh_ref, og_ref, w_ref, x_ref, g_ref, o_ref):
    a = jax.nn.sigmoid(og_ref[0]) * hh_ref[0]
    o_ref[0] = x_ref[0] + g_ref[0] * _dot(a.astype(BF16), w_ref[...])


def _ml_out(hh, og, w_out_bf, x, g1):
    B, S, D = x.shape
    tm = TOKEN_TILE
    row = lambda b, j: (b, j, 0)
    per_b = lambda b, j: (b, 0, 0)
    const2 = lambda b, j: (0, 0)
    return pl.pallas_call(
        _ml_out_kernel,
        out_shape=jax.ShapeDtypeStruct((B, S, D), F32),
        grid=(B, S // tm),
        in_specs=[
            pl.BlockSpec((1, tm, D), row),
            pl.BlockSpec((1, tm, D), row),
            pl.BlockSpec((D, D), const2),
            pl.BlockSpec((1, tm, D), row),
            pl.BlockSpec((1, 1, D), per_b),
        ],
        out_specs=pl.BlockSpec((1, tm, D), row),
        compiler_params=_cparams("parallel", "parallel"),
        name="mlstm_out_proj",
    )(hh, og, w_out_bf, x, g1)


def _first_argmax(x, lane, width):
    mx = jnp.max(x, axis=-1, keepdims=True)
    idx = jnp.min(jnp.where(x == mx, lane, width), axis=-1, keepdims=True)
    return mx, idx


def _router_kernel(x_ref, gain_ref, sc_ref, sh_ref, whi_ref, wlo_ref, b_ref, h_ref, r_ref, cnt_ref, run_ref):
    tm = x_ref.shape[1]

    @pl.when(jnp.logical_and(pl.program_id(0) == 0, pl.program_id(1) == 0))
    def _():
        run_ref[...] = jnp.zeros_like(run_ref)

    h = _modulated_norm(x_ref[0], gain_ref[...], sc_ref[0], sh_ref[0])
    h_hi = h.astype(BF16)
    bits = lax.bitcast_convert_type(h_hi.astype(F32), jnp.uint32)
    half = bits.shape[1] // 2
    packed = (bits[:, :half] >> 16) | (bits[:, half:] & jnp.uint32(0xFFFF0000))
    h_ref[0] = lax.bitcast_convert_type(packed, jnp.int32)
    h_lo = (h - h_hi.astype(F32)).astype(BF16)
    whi = whi_ref[...]
    logits = _dot(h_hi, whi) + _dot(h_lo, whi) + _dot(h_hi, wlo_ref[...]) + b_ref[...]
    lane = lax.broadcasted_iota(jnp.int32, (1, LANES), 1)
    lg = jnp.where(lane < N_GROUPS, logits, NEG_INF)
    g_max, g_sel = _first_argmax(lg, lane, LANES)
    pg = 1.0 / jnp.sum(jnp.exp(lg - g_max), axis=-1, keepdims=True)
    e_lane = lane - N_GROUPS
    in_grp = jnp.logical_and(e_lane >= g_sel * EXPERTS_PER_GROUP, e_lane < (g_sel + 1) * EXPERTS_PER_GROUP)
    le = jnp.where(in_grp, logits, NEG_INF)
    v1, i1 = _first_argmax(le, lane, LANES)
    le2 = jnp.where(lane == i1, NEG_INF, le)
    v2, i2 = _first_argmax(le2, lane, LANES)
    e2 = jnp.exp(v2 - v1)
    w1 = pg / (1.0 + e2)
    w2 = pg * e2 / (1.0 + e2)
    e1 = i1 - N_GROUPS
    e2 = i2 - N_GROUPS
    hot1 = lane == e1
    hot2 = lane == e2
    onehot = jnp.where(jnp.logical_or(hot1, hot2), 1.0, 0.0)
    r_i = lax.broadcasted_iota(jnp.int32, (tm, tm), 0)
    c_i = lax.broadcasted_iota(jnp.int32, (tm, tm), 1)
    before = jnp.where(c_i < r_i, 1.0, 0.0).astype(BF16)
    seen = _dot(before, onehot.astype(BF16)) + run_ref[...]
    rank1 = jnp.sum(jnp.where(hot1, seen, 0.0), axis=-1, keepdims=True)
    rank2 = jnp.sum(jnp.where(hot2, seen, 0.0), axis=-1, keepdims=True)
    run_ref[...] = run_ref[...] + jnp.sum(onehot, axis=0, keepdims=True)
    cnt_ref[...] = run_ref[...]

    out = jnp.where(lane == 0, e1.astype(F32), 0.0)
    out = jnp.where(lane == 1, e2.astype(F32), out)
    out = jnp.where(lane == 2, w1, out)
    out = jnp.where(lane == 3, w2, out)
    out = jnp.where(lane == 4, rank1, out)
    out = jnp.where(lane == 5, rank2, out)
    r_ref[0] = out


def _router(x, gain, sc, sh, w_hi, w_lo, bias):
    B, S, D = x.shape
    tm = TOKEN_TILE
    row = lambda b, j: (b, j, 0)
    per_b = lambda b, j: (b, 0, 0)
    const2 = lambda b, j: (0, 0)
    return pl.pallas_call(
        _router_kernel,
        out_shape=(
            jax.ShapeDtypeStruct((B, S, D // 2), jnp.int32),
            jax.ShapeDtypeStruct((B, S, LANES), F32),
            jax.ShapeDtypeStruct((1, LANES), F32),
        ),
        grid=(B, S // tm),
        in_specs=[
            pl.BlockSpec((1, tm, D), row),
            pl.BlockSpec((1, D), const2),
            pl.BlockSpec((1, 1, D), per_b),
            pl.BlockSpec((1, 1, D), per_b),
            pl.BlockSpec((D, LANES), const2),
            pl.BlockSpec((D, LANES), const2),
            pl.BlockSpec((1, LANES), const2),
        ],
        out_specs=(
            pl.BlockSpec((1, tm, D // 2), row),
            pl.BlockSpec((1, tm, LANES), row),
            pl.BlockSpec((1, LANES), const2),
        ),
        scratch_shapes=[pltpu.VMEM((1, LANES), F32)],
        compiler_params=_cparams("arbitrary", "arbitrary"),
        name="moe_router",
    )(x, gain, sc, sh, w_hi, w_lo, bias)


def _experts_kernel(blk_e_ref, n_used_ref, x_ref, wg_ref, wu_ref, wd_ref, o_ref, wg_s, wu_s, wd_s):
    i = pl.program_id(0)
    used = i < n_used_ref[0]
    new_expert = jnp.logical_or(i == 0, blk_e_ref[i] != blk_e_ref[jnp.maximum(i - 1, 0)])

    @pl.when(jnp.logical_and(used, new_expert))
    def _():
        wg_s[...] = wg_ref[0, 0].astype(BF16)
        wu_s[...] = wu_ref[0, 0].astype(BF16)
        wd_s[...] = wd_ref[0, 0].astype(BF16)

    @pl.when(used)
    def _():
        words = lax.bitcast_convert_type(x_ref[...], jnp.uint32)
        x = jnp.concatenate(
            [lax.bitcast_convert_type(words << 16, F32),
             lax.bitcast_convert_type(words & jnp.uint32(0xFFFF0000), F32)], axis=1).astype(BF16)
        a = _dot(x, wg_s[...])
        u = _dot(x, wu_s[...])
        act = a * jax.nn.sigmoid(a) * u
        o_ref[...] = _dot(act.astype(BF16), wd_s[...])

    @pl.when(i >= n_used_ref[0])
    def _():
        o_ref[...] = jnp.zeros_like(o_ref)


def _experts(layer, blk_e, n_used, xs, w_gate, w_up, w_down):
    R = xs.shape[0]
    D = 2 * xs.shape[1]
    n_blk = R // MOE_BLOCK
    grid_spec = pltpu.PrefetchScalarGridSpec(
        num_scalar_prefetch=2,
        grid=(n_blk,),
        in_specs=[
            pl.BlockSpec((MOE_BLOCK, D // 2), lambda i, be, nu: (i, 0)),
            pl.BlockSpec((1, 1, D, D_EXPERT), lambda i, be, nu: (layer, be[i], 0, 0)),
            pl.BlockSpec((1, 1, D, D_EXPERT), lambda i, be, nu: (layer, be[i], 0, 0)),
            pl.BlockSpec((1, 1, D_EXPERT, D), lambda i, be, nu: (layer, be[i], 0, 0)),
        ],
        out_specs=pl.BlockSpec((MOE_BLOCK, D), lambda i, be, nu: (i, 0)),
        scratch_shapes=[
            pltpu.VMEM((D, D_EXPERT), BF16),
            pltpu.VMEM((D, D_EXPERT), BF16),
            pltpu.VMEM((D_EXPERT, D), BF16),
        ],
    )
    return pl.pallas_call(
        _experts_kernel,
        out_shape=jax.ShapeDtypeStruct((R, D), F32),
        grid_spec=grid_spec,
        compiler_params=_cparams("arbitrary"),
        name="moe_experts",
    )(blk_e, n_used, xs, w_gate, w_up, w_down)


def _combine_kernel(x_ref, g_ref, y0_ref, y1_ref, r_ref, o_ref):
    r = r_ref[0]
    y = y0_ref[0, 0] * r[:, 2:3] + y1_ref[0, 0] * r[:, 3:4]
    o_ref[0] = x_ref[0] + g_ref[0] * y


def _combine(x, g2, y01, route):
    B, S, D = x.shape
    tm = TOKEN_TILE
    row = lambda b, j: (b, j, 0)
    per_b = lambda b, j: (b, 0, 0)
    return pl.pallas_call(
        _combine_kernel,
        out_shape=jax.ShapeDtypeStruct((B, S, D), F32),
        grid=(B, S // tm),
        in_specs=[
            pl.BlockSpec((1, tm, D), row),
            pl.BlockSpec((1, 1, D), per_b),
            pl.BlockSpec((1, 1, tm, D), lambda b, j: (0, b, j, 0)),
            pl.BlockSpec((1, 1, tm, D), lambda b, j: (1, b, j, 0)),
            pl.BlockSpec((1, tm, LANES), row),
        ],
        out_specs=pl.BlockSpec((1, tm, D), row),
        compiler_params=_cparams("parallel", "parallel"),
        name="moe_combine",
    )(x, g2, y01, y01, route)


SC_CORES = 2
SC_SUBCORES = 16
SC_WORKERS = SC_CORES * SC_SUBCORES
SC_CHUNK = 32


def _sc_mesh():
    return plsc.VectorSubcoreMesh(core_axis_name="c", subcore_axis_name="s",
                                  num_cores=SC_CORES, num_subcores=SC_SUBCORES)


def _sc_scatter_rows(src, idx, n_out):
    T, W = src.shape
    per_w = T // SC_WORKERS
    nch = per_w // SC_CHUNK
    idx4 = idx.reshape(TOP_K, SC_WORKERS, nch, SC_CHUNK)

    @functools.partial(
        pl.kernel, mesh=_sc_mesh(),
        out_type=jax.ShapeDtypeStruct((n_out, W), src.dtype),
        scratch_types=[pltpu.VMEM((TOP_K, nch, SC_CHUNK), jnp.int32), pltpu.VMEM((SC_CHUNK, W), src.dtype)],
        name="sc_scatter_rows",
    )
    def body(src_hbm, idx_hbm, out_hbm, idx_v, rows_v):
        wid = lax.axis_index("s") * SC_CORES + lax.axis_index("c")
        for s in range(TOP_K):
            pltpu.sync_copy(idx_hbm.at[s, wid], idx_v.at[s])

        @pl.loop(0, nch)
        def _(i):
            pltpu.sync_copy(src_hbm.at[pl.ds(wid * per_w + i * SC_CHUNK, SC_CHUNK)], rows_v)
            for s in range(TOP_K):
                pltpu.sync_copy(rows_v, out_hbm.at[idx_v.at[s, i]])

    return body(src, idx4)


def _sc_gather_rows(table, idx):
    N = idx.shape[0]
    W = table.shape[1]
    per_w = N // SC_WORKERS
    nch = per_w // SC_CHUNK
    idx3 = idx.reshape(SC_WORKERS, nch, SC_CHUNK)

    @functools.partial(
        pl.kernel, mesh=_sc_mesh(),
        out_type=jax.ShapeDtypeStruct((N, W), table.dtype),
        scratch_types=[
            pltpu.VMEM((nch, SC_CHUNK), jnp.int32),
            pltpu.VMEM((2, SC_CHUNK, W), table.dtype),
            pltpu.SemaphoreType.DMA((2,)),
            pltpu.SemaphoreType.DMA((2,)),
        ],
        name="sc_gather_rows",
    )
    def body(table_hbm, idx_hbm, out_hbm, idx_v, rows_v, gather_sem, write_sem):
        wid = lax.axis_index("s") * SC_CORES + lax.axis_index("c")
        pltpu.sync_copy(idx_hbm.at[wid], idx_v)

        def gather(j, b):
            return pltpu.make_async_copy(table_hbm.at[idx_v.at[j]], rows_v.at[b], gather_sem.at[b])

        def write(j, b):
            return pltpu.make_async_copy(
                rows_v.at[b], out_hbm.at[pl.ds(wid * per_w + j * SC_CHUNK, SC_CHUNK)], write_sem.at[b])

        gather(0, 0).start()

        @pl.loop(0, nch, step=2)
        def _(i):
            for b in range(2):
                j = i + b

                @pl.when(j >= 1)
                def _():
                    write(j - 1, 1 - b).wait()

                @pl.when(j + 1 < nch)
                def _():
                    gather(j + 1, 1 - b).start()

                gather(j, b).wait()
                write(j, b).start()

        write(nch - 1, (nch - 1) % 2).wait()

    assert nch % 2 == 0
    return body(table, idx3)


def _moe_dispatch(route, counts, T):
    A = T * TOP_K
    counts = counts[0, :N_EXPERTS].astype(jnp.int32)
    blocks_per = (counts + MOE_BLOCK - 1) // MOE_BLOCK
    block_end = jnp.cumsum(blocks_per)
    block_start = block_end - blocks_per
    expert = route[:, :TOP_K].astype(jnp.int32)
    rank = route[:, 4:4 + TOP_K].astype(jnp.int32)
    onehot = expert[:, :, None] == jnp.arange(N_EXPERTS, dtype=jnp.int32)
    start = jnp.sum(jnp.where(onehot, block_start, 0), axis=-1)
    dest = (start * MOE_BLOCK + rank).T
    n_blk = -(-A // MOE_BLOCK) + N_EXPERTS
    blk = jnp.arange(n_blk, dtype=jnp.int32)
    blk_e = jnp.minimum(jnp.sum(blk[:, None] >= block_end[None, :], axis=-1), N_EXPERTS - 1)
    return dest, n_blk * MOE_BLOCK, blk_e.astype(jnp.int32), block_end[-1:].astype(jnp.int32)


def _rope_tables(S):
    inv = 1.0 / (ROPE_THETA ** (jnp.arange(0, HEAD_DIM, 2, dtype=F32) / HEAD_DIM))
    ang = jnp.arange(S, dtype=F32)[:, None] * inv[None, :]
    cos, sin = jnp.cos(ang), jnp.sin(ang)
    cos_h = jnp.concatenate([cos, cos], axis=-1)
    sin_h = jnp.concatenate([-sin, sin], axis=-1)
    return jnp.tile(cos_h, (1, ATTN_HEADS)), jnp.tile(sin_h, (1, ATTN_HEADS))


def _pad_cols(w, width):
    return jnp.pad(w, ((0, 0), (0, width - w.shape[1])))


def kernel(x, c, ada_w, ada_b, norm_mix, norm_ffn, hy_w_in, hy_q_norm, hy_k_norm, hy_conv_w, hy_w_out, ml_w_in, ml_b_gates, ml_out_norm, ml_w_out, moe_w_group, moe_b_group, moe_w_expert, moe_b_expert, moe_w_gate, moe_w_up, moe_w_down):
    B, S, D = x.shape
    T = B * S
    cos_t, sin_t = _rope_tables(S)
    mod = _ada_modulation(c, ada_w, ada_b).reshape(DEPTH, B, 6, 1, D)
    r_i = np.arange(ATTN_WIDTH)
    grp = jnp.asarray((r_i[:, None] // HEAD_DIM) == (r_i[None, :] // HEAD_DIM), dtype=BF16)

    for l in range(DEPTH):
        sh1, sc1, g1, sh2, sc2, g2 = [mod[l, :, i] for i in range(6)]
        gain1 = norm_mix[l].reshape(1, D)
        j = l // 2
        if l % 2 == 0:
            w = hy_w_in[j]
            o = np.cumsum((0,) + (ATTN_WIDTH, HEAD_DIM, HEAD_DIM, IDX_HEADS * IDX_DIM, IDX_DIM, IDX_HEADS,
                                  CONV_WIDTH, CONV_WIDTH, CONV_WIDTH))
            wq, wk, wv, wiq, wik, wiw, wbg, wcg, wu = [w[:, o[i]:o[i + 1]] for i in range(9)]
            w_pad = jnp.concatenate(
                [wq, wiq, wbg, wcg, wu, wk, wv, _pad_cols(jnp.concatenate([wik, wiw], axis=1), LANES)],
                axis=1).astype(BF16)
            qn_t = jnp.tile(hy_q_norm[j], ATTN_HEADS).reshape(1, ATTN_WIDTH)
            kn_t = jnp.tile(hy_k_norm[j], LANES // HEAD_DIM).reshape(1, LANES)
            q, iq, bcu, kv, ik, iw = _hyb_in(x, gain1, sc1, sh1, w_pad, cos_t, sin_t, qn_t, kn_t, grp)
            y_attn = _dsa_attention(q, iq, iw, kv, ik)
            x = _hyb_out(y_attn, bcu, hy_conv_w[j], hy_w_out[j].astype(BF16), x, g1)
        else:
            w = ml_w_in[j]
            hq = ML_HEADS * ML_QK_DIM
            hv = ML_HEADS * ML_V_DIM
            wq, wk, wv = w[:, :hq], w[:, hq:2 * hq], w[:, 2 * hq:2 * hq + hv]
            wg = w[:, 2 * hq + hv:2 * hq + hv + 2 * ML_HEADS]
            wo = w[:, 2 * hq + hv + 2 * ML_HEADS:]
            w_pad = jnp.concatenate([wq, wk, wv, wo, _pad_cols(wg, LANES)], axis=1).astype(BF16)
            gate_bias = jnp.pad(ml_b_gates[j], (0, LANES - 2 * ML_HEADS)).reshape(1, LANES)
            q, k, v, og, gates = _ml_in(x, gain1, sc1, sh1, w_pad, gate_bias)
            nc = S // ML_CHUNK
            hp = ML_STEP_HEADS
            groups = ML_HEADS // hp
            gi = gates[:, :, :ML_HEADS].reshape(B, S, groups, hp)
            gf = gates[:, :, ML_HEADS:2 * ML_HEADS].reshape(B, S, groups, hp)
            gp = jnp.concatenate([gi, gf], axis=-1)
            gp = jnp.transpose(gp, (0, 2, 1, 3)).reshape(B, groups, nc, ML_CHUNK, 2 * hp)
            g_cols = jnp.pad(gp, ((0, 0),) * 4 + ((0, LANES - 2 * hp),))
            g_rows = jnp.swapaxes(gp, -1, -2)
            hh = _mlstm(q, k, v, g_rows, g_cols, ml_out_norm[j].reshape(1, hv))
            x = _ml_out(hh, og, ml_w_out[j].astype(BF16), x, g1)

        w_r = _pad_cols(jnp.concatenate([moe_w_group[l], moe_w_expert[l]], axis=1), LANES)
        w_hi = w_r.astype(BF16)
        w_lo = (w_r - w_hi.astype(F32)).astype(BF16)
        b_r = jnp.pad(jnp.concatenate([moe_b_group[l], moe_b_expert[l]]), (0, LANES - N_GROUPS - N_EXPERTS))
        h2, route, counts = _router(x, norm_ffn[l].reshape(1, D), sc2, sh2, w_hi, w_lo, b_r.reshape(1, LANES))
        dest, n_rows, blk_e, n_used = _moe_dispatch(route.reshape(T, LANES), counts, T)
        xs = _sc_scatter_rows(h2.reshape(T, D // 2), dest, n_rows)
        ys = _experts(l, blk_e, n_used, xs, moe_w_gate, moe_w_up, moe_w_down)
        y01 = _sc_gather_rows(ys, dest.reshape(TOP_K * T)).reshape(TOP_K, B, S, D)
        x = _combine(x, g2, y01, route)
    return x
```

```python
import functools

import numpy as np
import jax
import jax.numpy as jnp
from jax import lax
from jax.experimental import pallas as pl
from jax.experimental.pallas import tpu as pltpu
from jax.experimental.pallas import tpu_sc as plsc

F32 = jnp.float32
BF16 = jnp.bfloat16
HIGHEST = lax.Precision.HIGHEST

D_MODEL = 1024
DEPTH = 4
ATTN_HEADS = 8
HEAD_DIM = 64
ATTN_WIDTH = ATTN_HEADS * HEAD_DIM
IDX_HEADS = 8
IDX_DIM = 64
INDEX_TOPK = 256
Q_BLOCK = 128
ROPE_THETA = 10000.0
CONV_WIDTH = D_MODEL - ATTN_WIDTH
CONV_K = 3
ML_HEADS = 8
ML_QK_DIM = 64
ML_V_DIM = 128
N_GROUPS = 4
EXPERTS_PER_GROUP = 8
N_EXPERTS = N_GROUPS * EXPERTS_PER_GROUP
TOP_K = 2
D_EXPERT = 512
MOE_BLOCK = 256
NORM_EPS = 1e-6

LANES = 128
VMEM_LIMIT = 56 * 1024 * 1024
TOKEN_TILE = 256
ML_CHUNK = 256
ML_STEP_HEADS = 4
NEG_INF = float("-inf")


def _cparams(*sem):
    return pltpu.CompilerParams(dimension_semantics=sem, vmem_limit_bytes=VMEM_LIMIT)


def _dot(a, b):
    return jnp.dot(a, b, preferred_element_type=F32)


def _split_dot(a_f32, b_bf16):
    hi = a_f32.astype(BF16)
    lo = (a_f32 - hi.astype(F32)).astype(BF16)
    return _dot(hi, b_bf16) + _dot(lo, b_bf16)


def _ada_kernel(c_ref, w_ref, b_ref, o_ref):
    c = c_ref[...]
    ca = c * jax.nn.sigmoid(c)
    o_ref[0] = jnp.dot(ca, w_ref[0], precision=HIGHEST, preferred_element_type=F32) + b_ref[0]


def _ada_modulation(c, ada_w, ada_b):
    B, D = c.shape
    n_col = ada_w.shape[-1] // D
    return pl.pallas_call(
        _ada_kernel,
        out_shape=jax.ShapeDtypeStruct((DEPTH, B, n_col * D), F32),
        grid=(DEPTH, n_col),
        in_specs=[
            pl.BlockSpec((B, D), lambda l, j: (0, 0)),
            pl.BlockSpec((1, D, D), lambda l, j: (l, 0, j)),
            pl.BlockSpec((1, 1, D), lambda l, j: (l, 0, j)),
        ],
        out_specs=pl.BlockSpec((1, B, D), lambda l, j: (l, 0, j)),
        compiler_params=_cparams("parallel", "parallel"),
        name="ada_modulation",
    )(c, ada_w, ada_b.reshape(DEPTH, 1, n_col * D))


def _modulated_norm(x, gain, scale, shift):
    y = x * lax.rsqrt(jnp.mean(x * x, axis=-1, keepdims=True) + NORM_EPS)
    return y * gain * (1.0 + scale) + shift


def _rope(x, cos, sin_signed, first_half):
    w = x.shape[-1]
    partner = jnp.where(first_half, pltpu.roll(x, w - HEAD_DIM // 2, 1), pltpu.roll(x, HEAD_DIM // 2, 1))
    return x * cos + partner * sin_signed


HYB_COLS = 5 * 512 + 2 * LANES


def _hyb_in_kernel(x_ref, gain_ref, sc_ref, sh_ref, w_ref, cos_ref, sin_ref, qn_ref, kn_ref, grp_ref,
                   qt_ref, iqt_ref, bcu_ref, kv_ref, kvt_ref, ik_ref, iwt_ref):
    h = _modulated_norm(x_ref[0], gain_ref[...], sc_ref[0], sh_ref[0])
    p = _dot(h.astype(BF16), w_ref[...])
    cos = cos_ref[...]
    sin = sin_ref[...]
    lane = lax.broadcasted_iota(jnp.int32, (1, ATTN_WIDTH), 1)
    first_half = (lane % HEAD_DIM) < (HEAD_DIM // 2)
    fh128 = first_half[:, :LANES]
    lane128 = lane[:, :LANES]

    q = p[:, 0:512]
    ms = _split_dot(q * q, grp_ref[...]) * (1.0 / HEAD_DIM)
    q = q * lax.rsqrt(ms + NORM_EPS) * qn_ref[...]
    qt_ref[0] = (_rope(q, cos, sin, first_half) * (HEAD_DIM ** -0.5)).T.astype(BF16)

    iq = p[:, 512:1024]
    iqt_ref[0] = (_rope(iq, cos, sin, first_half) * (IDX_DIM ** -0.5)).T.astype(BF16)

    bcu_ref[0] = p[:, 1024:2560]

    kv = p[:, 2560:2688]
    is_k = lane128 < HEAD_DIM
    kk = jnp.where(is_k, kv, 0.0)
    ms_k = jnp.sum(kk * kk, axis=-1, keepdims=True) * (1.0 / HEAD_DIM)
    kn = kv * lax.rsqrt(ms_k + NORM_EPS) * kn_ref[...]
    kr = _rope(kn, cos[:, :LANES], sin[:, :LANES], fh128)
    kv = jnp.where(is_k, kr, kv)
    kv_ref[0] = kv.astype(BF16)
    kvt_ref[0, 0] = kv.T.astype(BF16)

    sm = p[:, 2688:2816]
    ikr = _rope(sm, cos[:, :LANES], sin[:, :LANES], fh128)
    ik_ref[0] = jnp.where(is_k, ikr, 0.0).astype(BF16)
    iwt_ref[0] = sm.T


def _hyb_in(x, gain, sc, sh, w_pad, cos_t, sin_t, qn_t, kn_t, grp):
    B, S, D = x.shape
    tm = TOKEN_TILE
    row = lambda b, j: (b, j, 0)
    per_b = lambda b, j: (b, 0, 0)
    const2 = lambda b, j: (0, 0)
    tab = lambda b, j: (j, 0)
    col = lambda b, j: (b, 0, j)
    assert tm == DSA_KEY_CHUNK
    return pl.pallas_call(
        _hyb_in_kernel,
        out_shape=(
            jax.ShapeDtypeStruct((B, 512, S), BF16),
            jax.ShapeDtypeStruct((B, 512, S), BF16),
            jax.ShapeDtypeStruct((B, S, 1536), F32),
            jax.ShapeDtypeStruct((B, S, LANES), BF16),
            jax.ShapeDtypeStruct((B, S // tm, LANES, tm), BF16),
            jax.ShapeDtypeStruct((B, S, LANES), BF16),
            jax.ShapeDtypeStruct((B, LANES, S), F32),
        ),
        grid=(B, S // tm),
        in_specs=[
            pl.BlockSpec((1, tm, D), row),
            pl.BlockSpec((1, D), const2),
            pl.BlockSpec((1, 1, D), per_b),
            pl.BlockSpec((1, 1, D), per_b),
            pl.BlockSpec((D, HYB_COLS), const2),
            pl.BlockSpec((tm, 512), tab),
            pl.BlockSpec((tm, 512), tab),
            pl.BlockSpec((1, 512), const2),
            pl.BlockSpec((1, LANES), const2),
            pl.BlockSpec((512, 512), const2),
        ],
        out_specs=(
            pl.BlockSpec((1, 512, tm), col),
            pl.BlockSpec((1, 512, tm), col),
            pl.BlockSpec((1, tm, 1536), row),
            pl.BlockSpec((1, tm, LANES), row),
            pl.BlockSpec((1, 1, LANES, tm), lambda b, j: (b, j, 0, 0)),
            pl.BlockSpec((1, tm, LANES), row),
            pl.BlockSpec((1, LANES, tm), col),
        ),
        compiler_params=_cparams("parallel", "parallel"),
        name="hybrid_in_proj",
    )(x, gain, sc, sh, w_pad, cos_t, sin_t, qn_t, kn_t, grp)


DSA_KEY_CHUNK = 256
F32_LOWEST = float(np.finfo(np.float32).min)


def _fold8(x, op):
    parts = x.reshape(x.shape[0] // 8, 8, x.shape[1])
    while parts.shape[0] > 1:
        half = parts.shape[0] // 2
        assert parts.shape[0] == 2 * half
        parts = op(parts[:half], parts[half:])
    return parts[0]


def _col_reduce(x, op):
    t = _fold8(x, op)
    for shift in (4, 2, 1):
        t = op(t, pltpu.roll(t, shift, 0))
    return t[0:1, :]


def _dsa_t_kernel(qt_ref, iqt_ref, iwt_ref, kv_ref, kvt_ref, ik_ref, o_ref, sc_ref, bias_ref, acc_ref):
    CK = DSA_KEY_CHUNK
    QB = Q_BLOCK
    qb = pl.program_id(1)
    nk = (qb * QB + QB + CK - 1) // CK
    kf = float(INDEX_TOPK)
    qpos = qb * QB + lax.broadcasted_iota(jnp.int32, (1, QB), 1)
    krow = lax.broadcasted_iota(jnp.int32, (CK, 1), 0)
    w_idx = iwt_ref[0, IDX_DIM:IDX_DIM + IDX_HEADS, :] * (IDX_HEADS ** -0.5)

    def rows(c):
        return pl.ds(pl.multiple_of(c * CK, CK), CK)

    def heads_on_lanes(ref, width):
        return jnp.concatenate([ref[0, hd * width:(hd + 1) * width, :] for hd in range(ref.shape[1] // width)], axis=1)

    def head_lanes(hd):
        return slice(hd * QB, (hd + 1) * QB)

    iq_wide = heads_on_lanes(iqt_ref, IDX_DIM)
    w_wide = jnp.concatenate([w_idx[hd:hd + 1, :] for hd in range(IDX_HEADS)], axis=1)

    def score_chunk(c, carry):
        mx, mn = carry
        ikc = ik_ref[0, rows(c), :][:, :IDX_DIM]
        s_all = jnp.maximum(_dot(ikc, iq_wide), 0.0) * w_wide
        acc = s_all[:, head_lanes(0)]
        for hd in range(1, IDX_HEADS):
            acc = acc + s_all[:, head_lanes(hd)]
        causal = (c * CK + krow) <= qpos
        sc_ref[rows(c), :] = jnp.where(causal, acc, NEG_INF)
        mx = jnp.maximum(mx, _fold8(jnp.where(causal, acc, NEG_INF), jnp.maximum))
        mn = jnp.minimum(mn, _fold8(jnp.where(causal, acc, jnp.inf), jnp.minimum))
        return mx, mn

    mx8, mn8 = lax.fori_loop(0, nk, score_chunk,
                             (jnp.full((8, QB), NEG_INF, F32), jnp.full((8, QB), jnp.inf, F32)))
    row_max = jnp.max(mx8, axis=0, keepdims=True)
    row_min = jnp.min(mn8, axis=0, keepdims=True)

    @pl.when(nk % 2 == 1)
    def _():
        sc_ref[rows(nk), :] = jnp.full((CK, QB), NEG_INF, F32)

    n_pairs = (nk + 1) // 2

    def pair_rows(c):
        return pl.ds(pl.multiple_of(c * (2 * CK), 2 * CK), 2 * CK)

    def count(pred):
        def body(c, part):
            return part + _fold8(jnp.where(pred(sc_ref[pair_rows(c), :]), 1.0, 0.0), jnp.add)
        part = lax.fori_loop(0, n_pairs, body, jnp.zeros((8, QB), F32))
        return jnp.sum(part, axis=0, keepdims=True)

    @pl.when(qb * QB + QB <= INDEX_TOPK)
    def _():
        def body(c, carry):
            bias_ref[rows(c), :] = jnp.where(sc_ref[rows(c), :] > NEG_INF, 0.0, NEG_INF)
            return carry
        lax.fori_loop(0, nk, body, 0)

    @pl.when(qb * QB + QB > INDEX_TOPK)
    def _():
        top_tied = count(lambda x: x >= row_max) >= kf

        def bisect(_, carry):
            lo, hi = carry
            mid = 0.5 * lo + 0.5 * hi
            ge = count(lambda x: x >= mid) >= kf
            return jnp.where(ge, mid, lo), jnp.where(ge, hi, mid)

        lo, hi = lax.fori_loop(0, 18, bisect, (row_min, row_max))

        def refine_cond(carry):
            it, _, _, done = carry
            return jnp.logical_and(it < nk * CK, jnp.min(done) < 0.5)

        def refine(carry):
            it, hi, thr, done = carry

            def below(c, part):
                x = sc_ref[pair_rows(c), :]
                return jnp.maximum(part, _fold8(jnp.where(x < hi, x, NEG_INF), jnp.maximum))

            m = jnp.max(lax.fori_loop(0, n_pairs, below, jnp.full((8, QB), NEG_INF, F32)), axis=0, keepdims=True)
            hit = count(lambda x: x >= m) >= kf
            fin = done > 0.5
            thr = jnp.where(fin, thr, m)
            hi = jnp.where(jnp.logical_or(fin, hit), hi, m)
            done = jnp.where(hit, 1.0, done)
            return it + 1, hi, thr, done

        done0 = jnp.where(top_tied, 1.0, 0.0)
        _, _, thr, _ = lax.while_loop(refine_cond, refine, (jnp.int32(0), hi, row_max, done0))

        need = kf - count(lambda x: x > thr)
        n_eq = count(lambda x: x == thr)
        tied = jnp.max(n_eq - need) > 0.5

        @pl.when(jnp.logical_not(tied))
        def _():
            def body(c, carry):
                bias_ref[rows(c), :] = jnp.where(sc_ref[rows(c), :] >= thr, 0.0, NEG_INF)
                return carry
            lax.fori_loop(0, nk, body, 0)

        @pl.when(tied)
        def _():
            r_i = lax.broadcasted_iota(jnp.int32, (CK, CK), 0)
            c_i = lax.broadcasted_iota(jnp.int32, (CK, CK), 1)
            lower = jnp.where(c_i <= r_i, 1.0, 0.0).astype(BF16)

            def body(c, seen):
                x = sc_ref[rows(c), :]
                eq = x == thr
                eq_f = jnp.where(eq, 1.0, 0.0)
                rank = _dot(lower, eq_f.astype(BF16)) + seen
                keep = jnp.logical_or(x > thr, jnp.logical_and(eq, rank <= need))
                bias_ref[rows(c), :] = jnp.where(keep, 0.0, NEG_INF)
                return seen + jnp.sum(eq_f, axis=0, keepdims=True)

            lax.fori_loop(0, nk, body, jnp.zeros((1, QB), F32))

    acc_ref[...] = jnp.zeros_like(acc_ref)
    q_wide = heads_on_lanes(qt_ref, HEAD_DIM)

    def attend_chunk(c, carry):
        m_old, l_old = carry
        kc = kv_ref[0, rows(c), :][:, :HEAD_DIM]
        vt = kvt_ref[0, c, HEAD_DIM:, :]
        bias = bias_ref[rows(c), :]
        logits = _dot(kc, q_wide) + jnp.concatenate([bias] * ATTN_HEADS, axis=1)
        m_new = jnp.maximum(m_old, _col_reduce(logits, jnp.maximum))
        alpha = jnp.exp(m_old - m_new)
        p = jnp.exp(logits - m_new)
        l_new = alpha * l_old + _col_reduce(p, jnp.add)
        acc_ref[...] = alpha * acc_ref[...] + _dot(vt, p.astype(BF16))
        return m_new, l_new

    _, l_all = lax.fori_loop(0, nk, attend_chunk, (jnp.full((1, ATTN_HEADS * QB), -1e30, F32),
                                                   jnp.zeros((1, ATTN_HEADS * QB), F32)))
    out_t = acc_ref[...] / l_all
    o_ref[0] = jnp.concatenate([out_t[:, head_lanes(hd)] for hd in range(ATTN_HEADS)], axis=0).T


def _dsa_attention_t(qt, iqt, iwt, kv, kvt, ik):
    B, _, S = qt.shape
    col = lambda b, j: (b, 0, j)
    per_b = lambda b, j: (b, 0, 0)
    return pl.pallas_call(
        _dsa_t_kernel,
        out_shape=jax.ShapeDtypeStruct((B, S, ATTN_WIDTH), F32),
        grid=(B, S // Q_BLOCK),
        in_specs=[
            pl.BlockSpec((1, ATTN_WIDTH, Q_BLOCK), col),
            pl.BlockSpec((1, IDX_HEADS * IDX_DIM, Q_BLOCK), col),
            pl.BlockSpec((1, LANES, Q_BLOCK), col),
            pl.BlockSpec((1, S, LANES), per_b),
            pl.BlockSpec((1, S // DSA_KEY_CHUNK, LANES, DSA_KEY_CHUNK), lambda b, j: (b, 0, 0, 0)),
            pl.BlockSpec((1, S, LANES), per_b),
        ],
        out_specs=pl.BlockSpec((1, Q_BLOCK, ATTN_WIDTH), lambda b, j: (b, j, 0)),
        scratch_shapes=[
            pltpu.VMEM((S, Q_BLOCK), F32),
            pltpu.VMEM((S, Q_BLOCK), F32),
            pltpu.VMEM((HEAD_DIM, ATTN_HEADS * Q_BLOCK), F32),
        ],
        compiler_params=_cparams("parallel", "parallel"),
        name="dsa_attention",
    )(qt, iqt, iwt, kv, kvt, ik)


def _hyb_out_kernel(ya_ref, bcu_ref, halo_ref, cw_ref, w_ref, x_ref, g_ref, o_ref):
    j = pl.program_id(1)
    tm = ya_ref.shape[1]
    bcu = bcu_ref[0]
    bg = bcu[:, 0:512]
    z = bcu[:, 512:1024] * bcu[:, 1024:1536]
    halo = halo_ref[0]
    zh = halo[:, 512:1024] * halo[:, 1024:1536]
    zh = jnp.where(j > 0, zh, 0.0)
    row = lax.broadcasted_iota(jnp.int32, (tm, 1), 0)
    z1 = jnp.where(row >= 1, pltpu.roll(z, 1, 0), zh[7:8, :])
    z2 = jnp.where(row >= 2, pltpu.roll(z, 2, 0), jnp.where(row == 1, zh[7:8, :], zh[6:7, :]))
    cw = cw_ref[...]
    y_conv = bg * (z2 * cw[0:1, :] + z1 * cw[1:2, :] + z * cw[2:3, :])
    y = _dot(ya_ref[0].astype(BF16), w_ref[0:512, :]) + _dot(y_conv.astype(BF16), w_ref[512:1024, :])
    o_ref[0] = x_ref[0] + g_ref[0] * y


def _hyb_out(y_attn, bcu, conv_w, w_out_bf, x, g1):
    B, S, D = x.shape
    tm = TOKEN_TILE
    row = lambda b, j: (b, j, 0)
    per_b = lambda b, j: (b, 0, 0)
    const2 = lambda b, j: (0, 0)
    halo = lambda b, j: (b, jnp.maximum(j * (tm // 8) - 1, 0), 0)
    return pl.pallas_call(
        _hyb_out_kernel,
        out_shape=jax.ShapeDtypeStruct((B, S, D), F32),
        grid=(B, S // tm),
        in_specs=[
            pl.BlockSpec((1, tm, 512), row),
            pl.BlockSpec((1, tm, 1536), row),
            pl.BlockSpec((1, 8, 1536), halo),
            pl.BlockSpec((CONV_K, CONV_WIDTH), const2),
            pl.BlockSpec((D, D), const2),
            pl.BlockSpec((1, tm, D), row),
            pl.BlockSpec((1, 1, D), per_b),
        ],
        out_specs=pl.BlockSpec((1, tm, D), row),
        compiler_params=_cparams("parallel", "parallel"),
        name="hybrid_out_proj",
    )(y_attn, bcu, bcu, conv_w, w_out_bf, x, g1)


ML_COLS = 512 + 512 + 1024 + 1024 + LANES


def _ml_in_kernel(x_ref, gain_ref, sc_ref, sh_ref, w_ref, bias_ref, q_ref, k_ref, v_ref, og_ref, gt_ref):
    h = _modulated_norm(x_ref[0], gain_ref[...], sc_ref[0], sh_ref[0])
    p = _dot(h.astype(BF16), w_ref[...])
    q_ref[0] = (p[:, 0:512] * (ML_QK_DIM ** -0.5)).astype(BF16)
    k_ref[0, 0] = p[:, 512:1024].T.astype(BF16)
    v_ref[0] = p[:, 1024:2048].astype(BF16)
    og_ref[0] = p[:, 2048:3072]
    gt_ref[0] = p[:, 3072:3200] + bias_ref[...]


def _ml_in(x, gain, sc, sh, w_pad, gate_bias):
    B, S, D = x.shape
    tm = TOKEN_TILE
    assert tm == ML_CHUNK
    row = lambda b, j: (b, j, 0)
    per_b = lambda b, j: (b, 0, 0)
    const2 = lambda b, j: (0, 0)
    return pl.pallas_call(
        _ml_in_kernel,
        out_shape=(
            jax.ShapeDtypeStruct((B, S, 512), BF16),
            jax.ShapeDtypeStruct((B, S // tm, 512, tm), BF16),
            jax.ShapeDtypeStruct((B, S, 1024), BF16),
            jax.ShapeDtypeStruct((B, S, 1024), F32),
            jax.ShapeDtypeStruct((B, S, LANES), F32),
        ),
        grid=(B, S // tm),
        in_specs=[
            pl.BlockSpec((1, tm, D), row),
            pl.BlockSpec((1, D), const2),
            pl.BlockSpec((1, 1, D), per_b),
            pl.BlockSpec((1, 1, D), per_b),
            pl.BlockSpec((D, ML_COLS), const2),
            pl.BlockSpec((1, LANES), const2),
        ],
        out_specs=(
            pl.BlockSpec((1, tm, 512), row),
            pl.BlockSpec((1, 1, 512, tm), lambda b, j: (b, j, 0, 0)),
            pl.BlockSpec((1, tm, 1024), row),
            pl.BlockSpec((1, tm, 1024), row),
            pl.BlockSpec((1, tm, LANES), row),
        ),
        compiler_params=_cparams("parallel", "parallel"),
        name="mlstm_in_proj",
    )(x, gain, sc, sh, w_pad, gate_bias)


def _log_sigmoid(f):
    return jnp.minimum(f, 0.0) - jnp.log1p(jnp.exp(-jnp.abs(f)))


def _split3(x):
    a = x.astype(BF16)
    r = x - a.astype(F32)
    b = r.astype(BF16)
    c = (r - b.astype(F32)).astype(BF16)
    return a, b, c


def _twice(a):
    return jnp.concatenate([a, a], axis=1)


def _mlstm_kernel(q_ref, kt_ref, v_ref, grow_ref, gcol_ref, gain_ref, o_ref, c_ref, m_ref):
    L = ML_CHUNK
    HP = ML_STEP_HEADS
    S = q_ref.shape[1]
    c_ref[...] = jnp.zeros_like(c_ref)
    m_ref[...] = jnp.zeros_like(m_ref)

    def chunk(c, carry):
        r0 = pl.multiple_of(c * L, L)
        r_i = lax.broadcasted_iota(jnp.int32, (L, L), 0)
        c_i = lax.broadcasted_iota(jnp.int32, (L, L), 1)
        tril = c_i <= r_i
        lower = jnp.where(tril, 1.0, 0.0).astype(BF16)
        upper = jnp.where(r_i <= c_i, 1.0, 0.0).astype(BF16)
        e_r = lax.broadcasted_iota(jnp.int32, (LANES, HP * LANES), 0)
        e_c = lax.broadcasted_iota(jnp.int32, (LANES, HP * LANES), 1)
        pick = jnp.where(e_r == HP + e_c // LANES, 1.0, 0.0).astype(BF16)
        rows = grow_ref[0, 0, c]
        cols = gcol_ref[0, 0, c]
        b_rows = sum(_dot(p, upper) for p in _split3(_log_sigmoid(rows)))
        b_cols = sum(_dot(lower, p) for p in _split3(_log_sigmoid(cols)))
        b_colr = sum(_dot(p, pick) for p in _split3(b_cols))
        lane = lax.broadcasted_iota(jnp.int32, (1, L), 1)
        b_last_all = jnp.sum(jnp.where(lane == L - 1, b_rows, 0.0), axis=-1, keepdims=True)
        ones_v = jnp.ones((L, ML_V_DIM), BF16)
        for hh in range(HP):
            q = q_ref[0, pl.ds(r0, L), hh * ML_QK_DIM:(hh + 1) * ML_QK_DIM]
            kt = kt_ref[0, c, hh * ML_QK_DIM:(hh + 1) * ML_QK_DIM, :]
            v = v_ref[0, pl.ds(r0, L), hh * ML_V_DIM:(hh + 1) * ML_V_DIM]
            vx = jnp.concatenate([v, ones_v], axis=1)
            i_row = rows[hh:hh + 1, :]
            b_row = b_rows[HP + hh:HP + hh + 1, :]
            b_last = b_last_all[HP + hh:HP + hh + 1, :]
            b_col = b_colr[:, hh * LANES:(hh + 1) * LANES]
            m_prev = m_ref[hh]
            ctn = c_ref[hh]

            dmat = jnp.where(tril, _twice(b_col) - b_row + i_row, NEG_INF)
            inter = b_col + m_prev
            m_t = jnp.maximum(inter, jnp.max(dmat, axis=-1, keepdims=True))
            w_intra = jnp.exp(dmat - _twice(m_t))
            w_inter = jnp.exp(inter - m_t)
            intra = (w_intra * _dot(q, kt)).astype(BF16)
            tot = _twice(w_inter) * _dot(q, ctn.astype(BF16)) + _dot(intra, vx)
            num = tot[:, :ML_V_DIM]
            den = tot[:, ML_V_DIM:]
            hc = num / jnp.maximum(jnp.abs(den), jnp.exp(-m_t))
            y = hc * lax.rsqrt(jnp.mean(hc * hc, axis=-1, keepdims=True) + NORM_EPS)
            o_ref[0, pl.ds(r0, L), hh * ML_V_DIM:(hh + 1) * ML_V_DIM] = (
                y * gain_ref[:, hh * ML_V_DIM:(hh + 1) * ML_V_DIM])

            g_row = b_last - b_row + i_row
            m_new = jnp.maximum(b_last + m_prev, jnp.max(g_row, axis=-1, keepdims=True))
            decay = jnp.exp(b_last + m_prev - m_new)
            kw = (kt.astype(F32) * jnp.exp(g_row - _twice(m_new))).astype(BF16)
            c_ref[hh] = _twice(decay) * ctn + _dot(kw, vx)
            m_ref[hh] = m_new
        return carry

    lax.fori_loop(0, S // L, chunk, 0)


def _mlstm(q, kt, v, g_rows, g_cols, out_gain):
    B, S, _ = q.shape
    assert ML_CHUNK == 2 * LANES
    nc = S // ML_CHUNK
    hp = ML_STEP_HEADS
    return pl.pallas_call(
        _mlstm_kernel,
        out_shape=jax.ShapeDtypeStruct((B, S, ML_HEADS * ML_V_DIM), F32),
        grid=(B, ML_HEADS // hp),
        in_specs=[
            pl.BlockSpec((1, S, hp * ML_QK_DIM), lambda b, p: (b, 0, p)),
            pl.BlockSpec((1, nc, hp * ML_QK_DIM, ML_CHUNK), lambda b, p: (b, 0, p, 0)),
            pl.BlockSpec((1, S, hp * ML_V_DIM), lambda b, p: (b, 0, p)),
            pl.BlockSpec((1, 1, nc, 2 * hp, ML_CHUNK), lambda b, p: (b, p, 0, 0, 0)),
            pl.BlockSpec((1, 1, nc, ML_CHUNK, LANES), lambda b, p: (b, p, 0, 0, 0)),
            pl.BlockSpec((1, hp * ML_V_DIM), lambda b, p: (0, p)),
        ],
        out_specs=pl.BlockSpec((1, S, hp * ML_V_DIM), lambda b, p: (b, 0, p)),
        scratch_shapes=[
            pltpu.VMEM((hp, ML_QK_DIM, ML_V_DIM + LANES), F32),
            pltpu.VMEM((hp, 1, LANES), F32),
        ],
        compiler_params=_cparams("parallel", "parallel"),
        name="mlstm_chunkwise",
    )(q, kt, v, g_rows, g_cols, out_gain)


def _ml_out_kernel(hh_ref, og_ref, w_ref, x_ref, g_ref, o_ref):
    a = jax.nn.sigmoid(og_ref[0]) * hh_ref[0]
    o_ref[0] = x_ref[0] + g_ref[0] * _dot(a.astype(BF16), w_ref[...])


def _ml_out(hh, og, w_out_bf, x, g1):
    B, S, D = x.shape
    tm = TOKEN_TILE
    row = lambda b, j: (b, j, 0)
    per_b = lambda b, j: (b, 0, 0)
    const2 = lambda b, j: (0, 0)
    return pl.pallas_call(
        _ml_out_kernel,
        out_shape=jax.ShapeDtypeStruct((B, S, D), F32),
        grid=(B, S // tm),
        in_specs=[
            pl.BlockSpec((1, tm, D), row),
            pl.BlockSpec((1, tm, D), row),
            pl.BlockSpec((D, D), const2),
            pl.BlockSpec((1, tm, D), row),
            pl.BlockSpec((1, 1, D), per_b),
        ],
        out_specs=pl.BlockSpec((1, tm, D), row),
        compiler_params=_cparams("parallel", "parallel"),
        name="mlstm_out_proj",
    )(hh, og, w_out_bf, x, g1)


def _first_argmax(x, lane, width):
    mx = jnp.max(x, axis=-1, keepdims=True)
    idx = jnp.min(jnp.where(x == mx, lane, width), axis=-1, keepdims=True)
    return mx, idx


def _router_kernel(x_ref, gain_ref, sc_ref, sh_ref, whi_ref, wlo_ref, b_ref, h_ref, r_ref, cnt_ref, run_ref):
    tm = x_ref.shape[1]

    @pl.when(jnp.logical_and(pl.program_id(0) == 0, pl.program_id(1) == 0))
    def _():
        run_ref[...] = jnp.zeros_like(run_ref)

    h = _modulated_norm(x_ref[0], gain_ref[...], sc_ref[0], sh_ref[0])
    h_hi = h.astype(BF16)
    bits = lax.bitcast_convert_type(h_hi.astype(F32), jnp.uint32)
    half = bits.shape[1] // 2
    packed = (bits[:, :half] >> 16) | (bits[:, half:] & jnp.uint32(0xFFFF0000))
    h_ref[0] = lax.bitcast_convert_type(packed, jnp.int32)
    h_lo = (h - h_hi.astype(F32)).astype(BF16)
    whi = whi_ref[...]
    logits = _dot(h_hi, whi) + _dot(h_lo, whi) + _dot(h_hi, wlo_ref[...]) + b_ref[...]
    lane = lax.broadcasted_iota(jnp.int32, (1, LANES), 1)
    lg = jnp.where(lane < N_GROUPS, logits, NEG_INF)
    g_max, g_sel = _first_argmax(lg, lane, LANES)
    pg = 1.0 / jnp.sum(jnp.exp(lg - g_max), axis=-1, keepdims=True)
    e_lane = lane - N_GROUPS
    in_grp = jnp.logical_and(e_lane >= g_sel * EXPERTS_PER_GROUP, e_lane < (g_sel + 1) * EXPERTS_PER_GROUP)
    le = jnp.where(in_grp, logits, NEG_INF)
    v1, i1 = _first_argmax(le, lane, LANES)
    le2 = jnp.where(lane == i1, NEG_INF, le)
    v2, i2 = _first_argmax(le2, lane, LANES)
    e2 = jnp.exp(v2 - v1)
    w1 = pg / (1.0 + e2)
    w2 = pg * e2 / (1.0 + e2)
    e1 = i1 - N_GROUPS
    e2 = i2 - N_GROUPS
    hot1 = lane == e1
    hot2 = lane == e2
    onehot = jnp.where(jnp.logical_or(hot1, hot2), 1.0, 0.0)
    r_i = lax.broadcasted_iota(jnp.int32, (tm, tm), 0)
    c_i = lax.broadcasted_iota(jnp.int32, (tm, tm), 1)
    before = jnp.where(c_i < r_i, 1.0, 0.0).astype(BF16)
    seen = _dot(before, onehot.astype(BF16)) + run_ref[...]
    rank1 = jnp.sum(jnp.where(hot1, seen, 0.0), axis=-1, keepdims=True)
    rank2 = jnp.sum(jnp.where(hot2, seen, 0.0), axis=-1, keepdims=True)
    run_ref[...] = run_ref[...] + jnp.sum(onehot, axis=0, keepdims=True)
    cnt_ref[...] = run_ref[...]

    out = jnp.where(lane == 0, e1.astype(F32), 0.0)
    out = jnp.where(lane == 1, e2.astype(F32), out)
    out = jnp.where(lane == 2, w1, out)
    out = jnp.where(lane == 3, w2, out)
    out = jnp.where(lane == 4, rank1, out)
    out = jnp.where(lane == 5, rank2, out)
    r_ref[0] = out


def _router(x, gain, sc, sh, w_hi, w_lo, bias):
    B, S, D = x.shape
    tm = TOKEN_TILE
    row = lambda b, j: (b, j, 0)
    per_b = lambda b, j: (b, 0, 0)
    const2 = lambda b, j: (0, 0)
    return pl.pallas_call(
        _router_kernel,
        out_shape=(
            jax.ShapeDtypeStruct((B, S, D // 2), jnp.int32),
            jax.ShapeDtypeStruct((B, S, LANES), F32),
            jax.ShapeDtypeStruct((1, LANES), F32),
        ),
        grid=(B, S // tm),
        in_specs=[
            pl.BlockSpec((1, tm, D), row),
            pl.BlockSpec((1, D), const2),
            pl.BlockSpec((1, 1, D), per_b),
            pl.BlockSpec((1, 1, D), per_b),
            pl.BlockSpec((D, LANES), const2),
            pl.BlockSpec((D, LANES), const2),
            pl.BlockSpec((1, LANES), const2),
        ],
        out_specs=(
            pl.BlockSpec((1, tm, D // 2), row),
            pl.BlockSpec((1, tm, LANES), row),
            pl.BlockSpec((1, LANES), const2),
        ),
        scratch_shapes=[pltpu.VMEM((1, LANES), F32)],
        compiler_params=_cparams("arbitrary", "arbitrary"),
        name="moe_router",
    )(x, gain, sc, sh, w_hi, w_lo, bias)


def _experts_kernel(blk_e_ref, n_used_ref, x_ref, wg_ref, wu_ref, wd_ref, o_ref, wg_s, wu_s, wd_s):
    i = pl.program_id(0)
    used = i < n_used_ref[0]
    new_expert = jnp.logical_or(i == 0, blk_e_ref[i] != blk_e_ref[jnp.maximum(i - 1, 0)])

    @pl.when(jnp.logical_and(used, new_expert))
    def _():
        wg_s[...] = wg_ref[0, 0].astype(BF16)
        wu_s[...] = wu_ref[0, 0].astype(BF16)
        wd_s[...] = wd_ref[0, 0].astype(BF16)

    @pl.when(used)
    def _():
        words = lax.bitcast_convert_type(x_ref[...], jnp.uint32)
        x = jnp.concatenate(
            [lax.bitcast_convert_type(words << 16, F32),
             lax.bitcast_convert_type(words & jnp.uint32(0xFFFF0000), F32)], axis=1).astype(BF16)
        a = _dot(x, wg_s[...])
        u = _dot(x, wu_s[...])
        act = a * jax.nn.sigmoid(a) * u
        o_ref[...] = _dot(act.astype(BF16), wd_s[...])

    @pl.when(i >= n_used_ref[0])
    def _():
        o_ref[...] = jnp.zeros_like(o_ref)


def _experts(layer, blk_e, n_used, xs, w_gate, w_up, w_down):
    R = xs.shape[0]
    D = 2 * xs.shape[1]
    n_blk = R // MOE_BLOCK
    grid_spec = pltpu.PrefetchScalarGridSpec(
        num_scalar_prefetch=2,
        grid=(n_blk,),
        in_specs=[
            pl.BlockSpec((MOE_BLOCK, D // 2), lambda i, be, nu: (i, 0)),
            pl.BlockSpec((1, 1, D, D_EXPERT), lambda i, be, nu: (layer, be[i], 0, 0)),
            pl.BlockSpec((1, 1, D, D_EXPERT), lambda i, be, nu: (layer, be[i], 0, 0)),
            pl.BlockSpec((1, 1, D_EXPERT, D), lambda i, be, nu: (layer, be[i], 0, 0)),
        ],
        out_specs=pl.BlockSpec((MOE_BLOCK, D), lambda i, be, nu: (i, 0)),
        scratch_shapes=[
            pltpu.VMEM((D, D_EXPERT), BF16),
            pltpu.VMEM((D, D_EXPERT), BF16),
            pltpu.VMEM((D_EXPERT, D), BF16),
        ],
    )
    return pl.pallas_call(
        _experts_kernel,
        out_shape=jax.ShapeDtypeStruct((R, D), F32),
        grid_spec=grid_spec,
        compiler_params=_cparams("arbitrary"),
        name="moe_experts",
    )(blk_e, n_used, xs, w_gate, w_up, w_down)


def _combine_kernel(x_ref, g_ref, y0_ref, y1_ref, r_ref, o_ref):
    r = r_ref[0]
    y = y0_ref[0, 0] * r[:, 2:3] + y1_ref[0, 0] * r[:, 3:4]
    o_ref[0] = x_ref[0] + g_ref[0] * y


def _combine(x, g2, y01, route):
    B, S, D = x.shape
    tm = TOKEN_TILE
    row = lambda b, j: (b, j, 0)
    per_b = lambda b, j: (b, 0, 0)
    return pl.pallas_call(
        _combine_kernel,
        out_shape=jax.ShapeDtypeStruct((B, S, D), F32),
        grid=(B, S // tm),
        in_specs=[
            pl.BlockSpec((1, tm, D), row),
            pl.BlockSpec((1, 1, D), per_b),
            pl.BlockSpec((1, 1, tm, D), lambda b, j: (0, b, j, 0)),
            pl.BlockSpec((1, 1, tm, D), lambda b, j: (1, b, j, 0)),
            pl.BlockSpec((1, tm, LANES), row),
        ],
        out_specs=pl.BlockSpec((1, tm, D), row),
        compiler_params=_cparams("parallel", "parallel"),
        name="moe_combine",
    )(x, g2, y01, y01, route)


SC_CORES = 2
SC_SUBCORES = 16
SC_WORKERS = SC_CORES * SC_SUBCORES
SC_CHUNK = 32


def _sc_mesh():
    return plsc.VectorSubcoreMesh(core_axis_name="c", subcore_axis_name="s",
                                  num_cores=SC_CORES, num_subcores=SC_SUBCORES)


def _sc_scatter_rows(src, idx, n_out):
    T, W = src.shape
    per_w = T // SC_WORKERS
    nch = per_w // SC_CHUNK
    idx4 = idx.reshape(TOP_K, SC_WORKERS, nch, SC_CHUNK)

    @functools.partial(
        pl.kernel, mesh=_sc_mesh(),
        out_type=jax.ShapeDtypeStruct((n_out, W), src.dtype),
        scratch_types=[pltpu.VMEM((TOP_K, nch, SC_CHUNK), jnp.int32), pltpu.VMEM((SC_CHUNK, W), src.dtype)],
        name="sc_scatter_rows",
    )
    def body(src_hbm, idx_hbm, out_hbm, idx_v, rows_v):
        wid = lax.axis_index("s") * SC_CORES + lax.axis_index("c")
        for s in range(TOP_K):
            pltpu.sync_copy(idx_hbm.at[s, wid], idx_v.at[s])

        @pl.loop(0, nch)
        def _(i):
            pltpu.sync_copy(src_hbm.at[pl.ds(wid * per_w + i * SC_CHUNK, SC_CHUNK)], rows_v)
            for s in range(TOP_K):
                pltpu.sync_copy(rows_v, out_hbm.at[idx_v.at[s, i]])

    return body(src, idx4)


def _sc_gather_rows(table, idx):
    N = idx.shape[0]
    W = table.shape[1]
    per_w = N // SC_WORKERS
    nch = per_w // SC_CHUNK
    idx3 = idx.reshape(SC_WORKERS, nch, SC_CHUNK)

    @functools.partial(
        pl.kernel, mesh=_sc_mesh(),
        out_type=jax.ShapeDtypeStruct((N, W), table.dtype),
        scratch_types=[
            pltpu.VMEM((nch, SC_CHUNK), jnp.int32),
            pltpu.VMEM((2, SC_CHUNK, W), table.dtype),
            pltpu.SemaphoreType.DMA((2,)),
            pltpu.SemaphoreType.DMA((2,)),
        ],
        name="sc_gather_rows",
    )
    def body(table_hbm, idx_hbm, out_hbm, idx_v, rows_v, gather_sem, write_sem):
        wid = lax.axis_index("s") * SC_CORES + lax.axis_index("c")
        pltpu.sync_copy(idx_hbm.at[wid], idx_v)

        def gather(j, b):
            return pltpu.make_async_copy(table_hbm.at[idx_v.at[j]], rows_v.at[b], gather_sem.at[b])

        def write(j, b):
            return pltpu.make_async_copy(
                rows_v.at[b], out_hbm.at[pl.ds(wid * per_w + j * SC_CHUNK, SC_CHUNK)], write_sem.at[b])

        gather(0, 0).start()

        @pl.loop(0, nch, step=2)
        def _(i):
            for b in range(2):
                j = i + b

                @pl.when(j >= 1)
                def _():
                    write(j - 1, 1 - b).wait()

                @pl.when(j + 1 < nch)
                def _():
                    gather(j + 1, 1 - b).start()

                gather(j, b).wait()
                write(j, b).start()

        write(nch - 1, (nch - 1) % 2).wait()

    assert nch % 2 == 0
    return body(table, idx3)


def _moe_dispatch(route, counts, T):
    A = T * TOP_K
    counts = counts[0, :N_EXPERTS].astype(jnp.int32)
    blocks_per = (counts + MOE_BLOCK - 1) // MOE_BLOCK
    block_end = jnp.cumsum(blocks_per)
    block_start = block_end - blocks_per
    expert = route[:, :TOP_K].astype(jnp.int32)
    rank = route[:, 4:4 + TOP_K].astype(jnp.int32)
    onehot = expert[:, :, None] == jnp.arange(N_EXPERTS, dtype=jnp.int32)
    start = jnp.sum(jnp.where(onehot, block_start, 0), axis=-1)
    dest = (start * MOE_BLOCK + rank).T
    n_blk = -(-A // MOE_BLOCK) + N_EXPERTS
    blk = jnp.arange(n_blk, dtype=jnp.int32)
    blk_e = jnp.minimum(jnp.sum(blk[:, None] >= block_end[None, :], axis=-1), N_EXPERTS - 1)
    return dest, n_blk * MOE_BLOCK, blk_e.astype(jnp.int32), block_end[-1:].astype(jnp.int32)


def _rope_tables(S):
    inv = 1.0 / (ROPE_THETA ** (jnp.arange(0, HEAD_DIM, 2, dtype=F32) / HEAD_DIM))
    ang = jnp.arange(S, dtype=F32)[:, None] * inv[None, :]
    cos, sin = jnp.cos(ang), jnp.sin(ang)
    cos_h = jnp.concatenate([cos, cos], axis=-1)
    sin_h = jnp.concatenate([-sin, sin], axis=-1)
    return jnp.tile(cos_h, (1, ATTN_HEADS)), jnp.tile(sin_h, (1, ATTN_HEADS))


def _pad_cols(w, width):
    return jnp.pad(w, ((0, 0), (0, width - w.shape[1])))


def kernel(x, c, ada_w, ada_b, norm_mix, norm_ffn, hy_w_in, hy_q_norm, hy_k_norm, hy_conv_w, hy_w_out, ml_w_in, ml_b_gates, ml_out_norm, ml_w_out, moe_w_group, moe_b_group, moe_w_expert, moe_b_expert, moe_w_gate, moe_w_up, moe_w_down):
    B, S, D = x.shape
    T = B * S
    cos_t, sin_t = _rope_tables(S)
    mod = _ada_modulation(c, ada_w, ada_b).reshape(DEPTH, B, 6, 1, D)
    r_i = np.arange(ATTN_WIDTH)
    grp = jnp.asarray((r_i[:, None] // HEAD_DIM) == (r_i[None, :] // HEAD_DIM), dtype=BF16)

    for l in range(DEPTH):
        sh1, sc1, g1, sh2, sc2, g2 = [mod[l, :, i] for i in range(6)]
        gain1 = norm_mix[l].reshape(1, D)
        j = l // 2
        if l % 2 == 0:
            w = hy_w_in[j]
            o = np.cumsum((0,) + (ATTN_WIDTH, HEAD_DIM, HEAD_DIM, IDX_HEADS * IDX_DIM, IDX_DIM, IDX_HEADS,
                                  CONV_WIDTH, CONV_WIDTH, CONV_WIDTH))
            wq, wk, wv, wiq, wik, wiw, wbg, wcg, wu = [w[:, o[i]:o[i + 1]] for i in range(9)]
            w_pad = jnp.concatenate(
                [wq, wiq, wbg, wcg, wu, wk, wv, _pad_cols(jnp.concatenate([wik, wiw], axis=1), LANES)],
                axis=1).astype(BF16)
            qn_t = jnp.tile(hy_q_norm[j], ATTN_HEADS).reshape(1, ATTN_WIDTH)
            kn_t = jnp.tile(hy_k_norm[j], LANES // HEAD_DIM).reshape(1, LANES)
            qt, iqt, bcu, kv, kvt, ik, iwt = _hyb_in(x, gain1, sc1, sh1, w_pad, cos_t, sin_t, qn_t, kn_t, grp)
            y_attn = _dsa_attention_t(qt, iqt, iwt, kv, kvt, ik)
            x = _hyb_out(y_attn, bcu, hy_conv_w[j], hy_w_out[j].astype(BF16), x, g1)
        else:
            w = ml_w_in[j]
            hq = ML_HEADS * ML_QK_DIM
            hv = ML_HEADS * ML_V_DIM
            wq, wk, wv = w[:, :hq], w[:, hq:2 * hq], w[:, 2 * hq:2 * hq + hv]
            wg = w[:, 2 * hq + hv:2 * hq + hv + 2 * ML_HEADS]
            wo = w[:, 2 * hq + hv + 2 * ML_HEADS:]
            w_pad = jnp.concatenate([wq, wk, wv, wo, _pad_cols(wg, LANES)], axis=1).astype(BF16)
            gate_bias = jnp.pad(ml_b_gates[j], (0, LANES - 2 * ML_HEADS)).reshape(1, LANES)
            q, k, v, og, gates = _ml_in(x, gain1, sc1, sh1, w_pad, gate_bias)
            nc = S // ML_CHUNK
            hp = ML_STEP_HEADS
            groups = ML_HEADS // hp
            gi = gates[:, :, :ML_HEADS].reshape(B, S, groups, hp)
            gf = gates[:, :, ML_HEADS:2 * ML_HEADS].reshape(B, S, groups, hp)
            gp = jnp.concatenate([gi, gf], axis=-1)
            gp = jnp.transpose(gp, (0, 2, 1, 3)).reshape(B, groups, nc, ML_CHUNK, 2 * hp)
            g_cols = jnp.pad(gp, ((0, 0),) * 4 + ((0, LANES - 2 * hp),))
            g_rows = jnp.swapaxes(gp, -1, -2)
            hh = _mlstm(q, k, v, g_rows, g_cols, ml_out_norm[j].reshape(1, hv))
            x = _ml_out(hh, og, ml_w_out[j].astype(BF16), x, g1)

        w_r = _pad_cols(jnp.concatenate([moe_w_group[l], moe_w_expert[l]], axis=1), LANES)
        w_hi = w_r.astype(BF16)
        w_lo = (w_r - w_hi.astype(F32)).astype(BF16)
        b_r = jnp.pad(jnp.concatenate([moe_b_group[l], moe_b_expert[l]]), (0, LANES - N_GROUPS - N_EXPERTS))
        h2, route, counts = _router(x, norm_ffn[l].reshape(1, D), sc2, sh2, w_hi, w_lo, b_r.reshape(1, LANES))
        dest, n_rows, blk_e, n_used = _moe_dispatch(route.reshape(T, LANES), counts, T)
        xs = _sc_scatter_rows(h2.reshape(T, D // 2), dest, n_rows)
        ys = _experts(l, blk_e, n_used, xs, moe_w_gate, moe_w_up, moe_w_down)
        y01 = _sc_gather_rows(ys, dest.reshape(TOP_K * T)).reshape(TOP_K, B, S, D)
        x = _combine(x, g2, y01, route)
    return x
```

```python
import functools

import numpy as np
import jax
import jax.numpy as jnp
from jax import lax
from jax.experimental import pallas as pl
from jax.experimental.pallas import tpu as pltpu
from jax.experimental.pallas import tpu_sc as plsc

F32 = jnp.float32
BF16 = jnp.bfloat16
HIGHEST = lax.Precision.HIGHEST

D_MODEL = 1024
DEPTH = 4
ATTN_HEADS = 8
HEAD_DIM = 64
ATTN_WIDTH = ATTN_HEADS * HEAD_DIM
IDX_HEADS = 8
IDX_DIM = 64
INDEX_TOPK = 256
Q_BLOCK = 256
ROPE_THETA = 10000.0
CONV_WIDTH = D_MODEL - ATTN_WIDTH
CONV_K = 3
ML_HEADS = 8
ML_QK_DIM = 64
ML_V_DIM = 128
N_GROUPS = 4
EXPERTS_PER_GROUP = 8
N_EXPERTS = N_GROUPS * EXPERTS_PER_GROUP
TOP_K = 2
D_EXPERT = 512
MOE_BLOCK = 256
NORM_EPS = 1e-6

LANES = 128
VMEM_LIMIT = 56 * 1024 * 1024
TOKEN_TILE = 256
ML_CHUNK = 256
ML_STEP_HEADS = 4
NEG_INF = float("-inf")


def _cparams(*sem):
    return pltpu.CompilerParams(dimension_semantics=sem, vmem_limit_bytes=VMEM_LIMIT)


def _dot(a, b):
    return jnp.dot(a, b, preferred_element_type=F32)


def _split_dot(a_f32, b_bf16):
    hi = a_f32.astype(BF16)
    lo = (a_f32 - hi.astype(F32)).astype(BF16)
    return _dot(hi, b_bf16) + _dot(lo, b_bf16)


def _ada_kernel(c_ref, w_ref, b_ref, o_ref):
    c = c_ref[...]
    ca = c * jax.nn.sigmoid(c)
    o_ref[0] = jnp.dot(ca, w_ref[0], precision=HIGHEST, preferred_element_type=F32) + b_ref[0]


def _ada_modulation(c, ada_w, ada_b):
    B, D = c.shape
    n_col = ada_w.shape[-1] // D
    return pl.pallas_call(
        _ada_kernel,
        out_shape=jax.ShapeDtypeStruct((DEPTH, B, n_col * D), F32),
        grid=(DEPTH, n_col),
        in_specs=[
            pl.BlockSpec((B, D), lambda l, j: (0, 0)),
            pl.BlockSpec((1, D, D), lambda l, j: (l, 0, j)),
            pl.BlockSpec((1, 1, D), lambda l, j: (l, 0, j)),
        ],
        out_specs=pl.BlockSpec((1, B, D), lambda l, j: (l, 0, j)),
        compiler_params=_cparams("parallel", "parallel"),
        name="ada_modulation",
    )(c, ada_w, ada_b.reshape(DEPTH, 1, n_col * D))


def _modulated_norm(x, gain, scale, shift):
    y = x * lax.rsqrt(jnp.mean(x * x, axis=-1, keepdims=True) + NORM_EPS)
    return y * gain * (1.0 + scale) + shift


def _rope(x, cos, sin_signed, first_half):
    w = x.shape[-1]
    partner = jnp.where(first_half, pltpu.roll(x, w - HEAD_DIM // 2, 1), pltpu.roll(x, HEAD_DIM // 2, 1))
    return x * cos + partner * sin_signed


HYB_COLS = 5 * 512 + 2 * LANES


def _hyb_in_kernel(x_ref, gain_ref, sc_ref, sh_ref, w_ref, cos_ref, sin_ref, qn_ref, kn_ref, grp_ref,
                   qt_ref, iqt_ref, bcu_ref, kv_ref, kvt_ref, ik_ref, iwt_ref):
    h = _modulated_norm(x_ref[0], gain_ref[...], sc_ref[0], sh_ref[0])
    p = _dot(h.astype(BF16), w_ref[...])
    cos = cos_ref[...]
    sin = sin_ref[...]
    lane = lax.broadcasted_iota(jnp.int32, (1, ATTN_WIDTH), 1)
    first_half = (lane % HEAD_DIM) < (HEAD_DIM // 2)
    fh128 = first_half[:, :LANES]
    lane128 = lane[:, :LANES]

    q = p[:, 0:512]
    ms = _split_dot(q * q, grp_ref[...]) * (1.0 / HEAD_DIM)
    q = q * lax.rsqrt(ms + NORM_EPS) * qn_ref[...]
    qt_ref[0] = (_rope(q, cos, sin, first_half) * (HEAD_DIM ** -0.5)).T.astype(BF16)

    iq = p[:, 512:1024]
    iqt_ref[0] = (_rope(iq, cos, sin, first_half) * (IDX_DIM ** -0.5)).T.astype(BF16)

    bcu_ref[0] = p[:, 1024:2560]

    kv = p[:, 2560:2688]
    is_k = lane128 < HEAD_DIM
    kk = jnp.where(is_k, kv, 0.0)
    ms_k = jnp.sum(kk * kk, axis=-1, keepdims=True) * (1.0 / HEAD_DIM)
    kn = kv * lax.rsqrt(ms_k + NORM_EPS) * kn_ref[...]
    kr = _rope(kn, cos[:, :LANES], sin[:, :LANES], fh128)
    kv = jnp.where(is_k, kr, kv)
    kv_ref[0] = kv.astype(BF16)
    kvt_ref[0, 0] = kv.T.astype(BF16)

    sm = p[:, 2688:2816]
    ikr = _rope(sm, cos[:, :LANES], sin[:, :LANES], fh128)
    ik_ref[0] = jnp.where(is_k, ikr, 0.0).astype(BF16)
    iwt_ref[0] = sm.T


def _hyb_in(x, gain, sc, sh, w_pad, cos_t, sin_t, qn_t, kn_t, grp):
    B, S, D = x.shape
    tm = TOKEN_TILE
    row = lambda b, j: (b, j, 0)
    per_b = lambda b, j: (b, 0, 0)
    const2 = lambda b, j: (0, 0)
    tab = lambda b, j: (j, 0)
    col = lambda b, j: (b, 0, j)
    assert tm == DSA_KEY_CHUNK
    return pl.pallas_call(
        _hyb_in_kernel,
        out_shape=(
            jax.ShapeDtypeStruct((B, 512, S), BF16),
            jax.ShapeDtypeStruct((B, 512, S), BF16),
            jax.ShapeDtypeStruct((B, S, 1536), F32),
            jax.ShapeDtypeStruct((B, S, LANES), BF16),
            jax.ShapeDtypeStruct((B, S // tm, LANES, tm), BF16),
            jax.ShapeDtypeStruct((B, S, LANES), BF16),
            jax.ShapeDtypeStruct((B, LANES, S), F32),
        ),
        grid=(B, S // tm),
        in_specs=[
            pl.BlockSpec((1, tm, D), row),
            pl.BlockSpec((1, D), const2),
            pl.BlockSpec((1, 1, D), per_b),
            pl.BlockSpec((1, 1, D), per_b),
            pl.BlockSpec((D, HYB_COLS), const2),
            pl.BlockSpec((tm, 512), tab),
            pl.BlockSpec((tm, 512), tab),
            pl.BlockSpec((1, 512), const2),
            pl.BlockSpec((1, LANES), const2),
            pl.BlockSpec((512, 512), const2),
        ],
        out_specs=(
            pl.BlockSpec((1, 512, tm), col),
            pl.BlockSpec((1, 512, tm), col),
            pl.BlockSpec((1, tm, 1536), row),
            pl.BlockSpec((1, tm, LANES), row),
            pl.BlockSpec((1, 1, LANES, tm), lambda b, j: (b, j, 0, 0)),
            pl.BlockSpec((1, tm, LANES), row),
            pl.BlockSpec((1, LANES, tm), col),
        ),
        compiler_params=_cparams("parallel", "parallel"),
        name="hybrid_in_proj",
    )(x, gain, sc, sh, w_pad, cos_t, sin_t, qn_t, kn_t, grp)


DSA_KEY_CHUNK = 256
F32_LOWEST = float(np.finfo(np.float32).min)


def _fold8(x, op):
    parts = x.reshape(x.shape[0] // 8, 8, x.shape[1])
    while parts.shape[0] > 1:
        half = parts.shape[0] // 2
        assert parts.shape[0] == 2 * half
        parts = op(parts[:half], parts[half:])
    return parts[0]


def _col_reduce(x, op):
    t = _fold8(x, op)
    for shift in (4, 2, 1):
        t = op(t, pltpu.roll(t, shift, 0))
    return t[0:1, :]


def _dsa_t_kernel(qt_ref, iqt_ref, iwt_ref, kv_ref, kvt_ref, ik_ref, o_ref, sc_ref, bias_ref, acc_ref):
    CK = DSA_KEY_CHUNK
    QB = Q_BLOCK
    qb = pl.program_id(1)
    nk = (qb * QB + QB + CK - 1) // CK
    kf = float(INDEX_TOPK)
    qpos = qb * QB + lax.broadcasted_iota(jnp.int32, (1, QB), 1)
    krow = lax.broadcasted_iota(jnp.int32, (CK, 1), 0)
    w_idx = iwt_ref[0, IDX_DIM:IDX_DIM + IDX_HEADS, :] * (IDX_HEADS ** -0.5)

    def rows(c):
        return pl.ds(pl.multiple_of(c * CK, CK), CK)

    def heads_on_lanes(ref, width):
        return jnp.concatenate([ref[0, hd * width:(hd + 1) * width, :] for hd in range(ref.shape[1] // width)], axis=1)

    def head_lanes(hd):
        return slice(hd * QB, (hd + 1) * QB)

    iq_wide = heads_on_lanes(iqt_ref, IDX_DIM)
    w_wide = jnp.concatenate([w_idx[hd:hd + 1, :] for hd in range(IDX_HEADS)], axis=1)

    def score_chunk(c, carry):
        mx, mn = carry
        ikc = ik_ref[0, rows(c), :][:, :IDX_DIM]
        s_all = jnp.maximum(_dot(ikc, iq_wide), 0.0) * w_wide
        acc = s_all[:, head_lanes(0)]
        for hd in range(1, IDX_HEADS):
            acc = acc + s_all[:, head_lanes(hd)]
        causal = (c * CK + krow) <= qpos
        sc_ref[rows(c), :] = jnp.where(causal, acc, NEG_INF)
        mx = jnp.maximum(mx, _fold8(jnp.where(causal, acc, NEG_INF), jnp.maximum))
        mn = jnp.minimum(mn, _fold8(jnp.where(causal, acc, jnp.inf), jnp.minimum))
        return mx, mn

    mx8, mn8 = lax.fori_loop(0, nk, score_chunk,
                             (jnp.full((8, QB), NEG_INF, F32), jnp.full((8, QB), jnp.inf, F32)))
    row_max = jnp.max(mx8, axis=0, keepdims=True)
    row_min = jnp.min(mn8, axis=0, keepdims=True)

    @pl.when(nk % 2 == 1)
    def _():
        sc_ref[rows(nk), :] = jnp.full((CK, QB), NEG_INF, F32)

    n_pairs = (nk + 1) // 2

    def pair_rows(c):
        return pl.ds(pl.multiple_of(c * (2 * CK), 2 * CK), 2 * CK)

    def count(pred):
        def body(c, part):
            return part + _fold8(jnp.where(pred(sc_ref[pair_rows(c), :]), 1.0, 0.0), jnp.add)
        part = lax.fori_loop(0, n_pairs, body, jnp.zeros((8, QB), F32))
        return jnp.sum(part, axis=0, keepdims=True)

    @pl.when(qb * QB + QB <= INDEX_TOPK)
    def _():
        def body(c, carry):
            bias_ref[rows(c), :] = jnp.where(sc_ref[rows(c), :] > NEG_INF, 0.0, NEG_INF)
            return carry
        lax.fori_loop(0, nk, body, 0)

    @pl.when(qb * QB + QB > INDEX_TOPK)
    def _():
        top_tied = count(lambda x: x >= row_max) >= kf

        def bisect(_, carry):
            lo, hi = carry
            mid = 0.5 * lo + 0.5 * hi
            ge = count(lambda x: x >= mid) >= kf
            return jnp.where(ge, mid, lo), jnp.where(ge, hi, mid)

        lo, hi = lax.fori_loop(0, 18, bisect, (row_min, row_max))

        def refine_cond(carry):
            it, _, _, done = carry
            return jnp.logical_and(it < nk * CK, jnp.min(done) < 0.5)

        def refine(carry):
            it, hi, thr, done = carry

            def below(c, part):
                x = sc_ref[pair_rows(c), :]
                return jnp.maximum(part, _fold8(jnp.where(x < hi, x, NEG_INF), jnp.maximum))

            m = jnp.max(lax.fori_loop(0, n_pairs, below, jnp.full((8, QB), NEG_INF, F32)), axis=0, keepdims=True)
            hit = count(lambda x: x >= m) >= kf
            fin = done > 0.5
            thr = jnp.where(fin, thr, m)
            hi = jnp.where(jnp.logical_or(fin, hit), hi, m)
            done = jnp.where(hit, 1.0, done)
            return it + 1, hi, thr, done

        done0 = jnp.where(top_tied, 1.0, 0.0)
        _, _, thr, _ = lax.while_loop(refine_cond, refine, (jnp.int32(0), hi, row_max, done0))

        need = kf - count(lambda x: x > thr)
        n_eq = count(lambda x: x == thr)
        tied = jnp.max(n_eq - need) > 0.5

        @pl.when(jnp.logical_not(tied))
        def _():
            def body(c, carry):
                bias_ref[rows(c), :] = jnp.where(sc_ref[rows(c), :] >= thr, 0.0, NEG_INF)
                return carry
            lax.fori_loop(0, nk, body, 0)

        @pl.when(tied)
        def _():
            r_i = lax.broadcasted_iota(jnp.int32, (CK, CK), 0)
            c_i = lax.broadcasted_iota(jnp.int32, (CK, CK), 1)
            lower = jnp.where(c_i <= r_i, 1.0, 0.0).astype(BF16)

            def body(c, seen):
                x = sc_ref[rows(c), :]
                eq = x == thr
                eq_f = jnp.where(eq, 1.0, 0.0)
                rank = _dot(lower, eq_f.astype(BF16)) + seen
                keep = jnp.logical_or(x > thr, jnp.logical_and(eq, rank <= need))
                bias_ref[rows(c), :] = jnp.where(keep, 0.0, NEG_INF)
                return seen + jnp.sum(eq_f, axis=0, keepdims=True)

            lax.fori_loop(0, nk, body, jnp.zeros((1, QB), F32))

    acc_ref[...] = jnp.zeros_like(acc_ref)
    q_wide = heads_on_lanes(qt_ref, HEAD_DIM)

    def attend_chunk(c, carry):
        m_old, l_old = carry
        kc = kv_ref[0, rows(c), :][:, :HEAD_DIM]
        vt = kvt_ref[0, c, HEAD_DIM:, :]
        bias = bias_ref[rows(c), :]
        logits = _dot(kc, q_wide) + jnp.concatenate([bias] * ATTN_HEADS, axis=1)
        m_new = jnp.maximum(m_old, _col_reduce(logits, jnp.maximum))
        alpha = jnp.exp(m_old - m_new)
        p = jnp.exp(logits - m_new)
        l_new = alpha * l_old + _col_reduce(p, jnp.add)
        acc_ref[...] = alpha * acc_ref[...] + _dot(vt, p.astype(BF16))
        return m_new, l_new

    _, l_all = lax.fori_loop(0, nk, attend_chunk, (jnp.full((1, ATTN_HEADS * QB), -1e30, F32),
                                                   jnp.zeros((1, ATTN_HEADS * QB), F32)))
    out_t = acc_ref[...] / l_all
    o_ref[0] = jnp.concatenate([out_t[:, head_lanes(hd)] for hd in range(ATTN_HEADS)], axis=0).T


def _dsa_attention_t(qt, iqt, iwt, kv, kvt, ik):
    B, _, S = qt.shape
    col = lambda b, j: (b, 0, j)
    per_b = lambda b, j: (b, 0, 0)
    return pl.pallas_call(
        _dsa_t_kernel,
        out_shape=jax.ShapeDtypeStruct((B, S, ATTN_WIDTH), F32),
        grid=(B, S // Q_BLOCK),
        in_specs=[
            pl.BlockSpec((1, ATTN_WIDTH, Q_BLOCK), col),
            pl.BlockSpec((1, IDX_HEADS * IDX_DIM, Q_BLOCK), col),
            pl.BlockSpec((1, LANES, Q_BLOCK), col),
            pl.BlockSpec((1, S, LANES), per_b),
            pl.BlockSpec((1, S // DSA_KEY_CHUNK, LANES, DSA_KEY_CHUNK), lambda b, j: (b, 0, 0, 0)),
            pl.BlockSpec((1, S, LANES), per_b),
        ],
        out_specs=pl.BlockSpec((1, Q_BLOCK, ATTN_WIDTH), lambda b, j: (b, j, 0)),
        scratch_shapes=[
            pltpu.VMEM((S, Q_BLOCK), F32),
            pltpu.VMEM((S, Q_BLOCK), F32),
            pltpu.VMEM((HEAD_DIM, ATTN_HEADS * Q_BLOCK), F32),
        ],
        compiler_params=_cparams("parallel", "parallel"),
        name="dsa_attention",
    )(qt, iqt, iwt, kv, kvt, ik)


def _hyb_out_kernel(ya_ref, bcu_ref, halo_ref, cw_ref, w_ref, x_ref, g_ref, o_ref):
    j = pl.program_id(1)
    tm = ya_ref.shape[1]
    bcu = bcu_ref[0]
    bg = bcu[:, 0:512]
    z = bcu[:, 512:1024] * bcu[:, 1024:1536]
    halo = halo_ref[0]
    zh = halo[:, 512:1024] * halo[:, 1024:1536]
    zh = jnp.where(j > 0, zh, 0.0)
    row = lax.broadcasted_iota(jnp.int32, (tm, 1), 0)
    z1 = jnp.where(row >= 1, pltpu.roll(z, 1, 0), zh[7:8, :])
    z2 = jnp.where(row >= 2, pltpu.roll(z, 2, 0), jnp.where(row == 1, zh[7:8, :], zh[6:7, :]))
    cw = cw_ref[...]
    y_conv = bg * (z2 * cw[0:1, :] + z1 * cw[1:2, :] + z * cw[2:3, :])
    y = _dot(ya_ref[0].astype(BF16), w_ref[0:512, :]) + _dot(y_conv.astype(BF16), w_ref[512:1024, :])
    o_ref[0] = x_ref[0] + g_ref[0] * y


def _hyb_out(y_attn, bcu, conv_w, w_out_bf, x, g1):
    B, S, D = x.shape
    tm = TOKEN_TILE
    row = lambda b, j: (b, j, 0)
    per_b = lambda b, j: (b, 0, 0)
    const2 = lambda b, j: (0, 0)
    halo = lambda b, j: (b, jnp.maximum(j * (tm // 8) - 1, 0), 0)
    return pl.pallas_call(
        _hyb_out_kernel,
        out_shape=jax.ShapeDtypeStruct((B, S, D), F32),
        grid=(B, S // tm),
        in_specs=[
            pl.BlockSpec((1, tm, 512), row),
            pl.BlockSpec((1, tm, 1536), row),
            pl.BlockSpec((1, 8, 1536), halo),
            pl.BlockSpec((CONV_K, CONV_WIDTH), const2),
            pl.BlockSpec((D, D), const2),
            pl.BlockSpec((1, tm, D), row),
            pl.BlockSpec((1, 1, D), per_b),
        ],
        out_specs=pl.BlockSpec((1, tm, D), row),
        compiler_params=_cparams("parallel", "parallel"),
        name="hybrid_out_proj",
    )(y_attn, bcu, bcu, conv_w, w_out_bf, x, g1)


ML_COLS = 512 + 512 + 1024 + 1024 + LANES


def _ml_in_kernel(x_ref, gain_ref, sc_ref, sh_ref, w_ref, bias_ref, q_ref, k_ref, v_ref, og_ref, gt_ref):
    h = _modulated_norm(x_ref[0], gain_ref[...], sc_ref[0], sh_ref[0])
    p = _dot(h.astype(BF16), w_ref[...])
    q_ref[0] = (p[:, 0:512] * (ML_QK_DIM ** -0.5)).astype(BF16)
    k_ref[0, 0] = p[:, 512:1024].T.astype(BF16)
    v_ref[0] = p[:, 1024:2048].astype(BF16)
    og_ref[0] = p[:, 2048:3072]
    gt_ref[0] = p[:, 3072:3200] + bias_ref[...]


def _ml_in(x, gain, sc, sh, w_pad, gate_bias):
    B, S, D = x.shape
    tm = TOKEN_TILE
    assert tm == ML_CHUNK
    row = lambda b, j: (b, j, 0)
    per_b = lambda b, j: (b, 0, 0)
    const2 = lambda b, j: (0, 0)
    return pl.pallas_call(
        _ml_in_kernel,
        out_shape=(
            jax.ShapeDtypeStruct((B, S, 512), BF16),
            jax.ShapeDtypeStruct((B, S // tm, 512, tm), BF16),
            jax.ShapeDtypeStruct((B, S, 1024), BF16),
            jax.ShapeDtypeStruct((B, S, 1024), F32),
            jax.ShapeDtypeStruct((B, S, LANES), F32),
        ),
        grid=(B, S // tm),
        in_specs=[
            pl.BlockSpec((1, tm, D), row),
            pl.BlockSpec((1, D), const2),
            pl.BlockSpec((1, 1, D), per_b),
            pl.BlockSpec((1, 1, D), per_b),
            pl.BlockSpec((D, ML_COLS), const2),
            pl.BlockSpec((1, LANES), const2),
        ],
        out_specs=(
            pl.BlockSpec((1, tm, 512), row),
            pl.BlockSpec((1, 1, 512, tm), lambda b, j: (b, j, 0, 0)),
            pl.BlockSpec((1, tm, 1024), row),
            pl.BlockSpec((1, tm, 1024), row),
            pl.BlockSpec((1, tm, LANES), row),
        ),
        compiler_params=_cparams("parallel", "parallel"),
        name="mlstm_in_proj",
    )(x, gain, sc, sh, w_pad, gate_bias)


def _log_sigmoid(f):
    return jnp.minimum(f, 0.0) - jnp.log1p(jnp.exp(-jnp.abs(f)))


def _split3(x):
    a = x.astype(BF16)
    r = x - a.astype(F32)
    b = r.astype(BF16)
    c = (r - b.astype(F32)).astype(BF16)
    return a, b, c


def _twice(a):
    return jnp.concatenate([a, a], axis=1)


def _mlstm_kernel(q_ref, kt_ref, v_ref, grow_ref, gcol_ref, gain_ref, o_ref, c_ref, m_ref):
    L = ML_CHUNK
    HP = ML_STEP_HEADS
    S = q_ref.shape[1]
    c_ref[...] = jnp.zeros_like(c_ref)
    m_ref[...] = jnp.zeros_like(m_ref)

    def chunk(c, carry):
        r0 = pl.multiple_of(c * L, L)
        r_i = lax.broadcasted_iota(jnp.int32, (L, L), 0)
        c_i = lax.broadcasted_iota(jnp.int32, (L, L), 1)
        tril = c_i <= r_i
        lower = jnp.where(tril, 1.0, 0.0).astype(BF16)
        upper = jnp.where(r_i <= c_i, 1.0, 0.0).astype(BF16)
        e_r = lax.broadcasted_iota(jnp.int32, (LANES, HP * LANES), 0)
        e_c = lax.broadcasted_iota(jnp.int32, (LANES, HP * LANES), 1)
        pick = jnp.where(e_r == HP + e_c // LANES, 1.0, 0.0).astype(BF16)
        rows = grow_ref[0, 0, c]
        cols = gcol_ref[0, 0, c]
        b_rows = sum(_dot(p, upper) for p in _split3(_log_sigmoid(rows)))
        b_cols = sum(_dot(lower, p) for p in _split3(_log_sigmoid(cols)))
        b_colr = sum(_dot(p, pick) for p in _split3(b_cols))
        lane = lax.broadcasted_iota(jnp.int32, (1, L), 1)
        b_last_all = jnp.sum(jnp.where(lane == L - 1, b_rows, 0.0), axis=-1, keepdims=True)
        ones_v = jnp.ones((L, ML_V_DIM), BF16)
        for hh in range(HP):
            q = q_ref[0, pl.ds(r0, L), hh * ML_QK_DIM:(hh + 1) * ML_QK_DIM]
            kt = kt_ref[0, c, hh * ML_QK_DIM:(hh + 1) * ML_QK_DIM, :]
            v = v_ref[0, pl.ds(r0, L), hh * ML_V_DIM:(hh + 1) * ML_V_DIM]
            vx = jnp.concatenate([v, ones_v], axis=1)
            i_row = rows[hh:hh + 1, :]
            b_row = b_rows[HP + hh:HP + hh + 1, :]
            b_last = b_last_all[HP + hh:HP + hh + 1, :]
            b_col = b_colr[:, hh * LANES:(hh + 1) * LANES]
            m_prev = m_ref[hh]
            ctn = c_ref[hh]

            dmat = jnp.where(tril, _twice(b_col) - b_row + i_row, NEG_INF)
            inter = b_col + m_prev
            m_t = jnp.maximum(inter, jnp.max(dmat, axis=-1, keepdims=True))
            w_intra = jnp.exp(dmat - _twice(m_t))
            w_inter = jnp.exp(inter - m_t)
            intra = (w_intra * _dot(q, kt)).astype(BF16)
            tot = _twice(w_inter) * _dot(q, ctn.astype(BF16)) + _dot(intra, vx)
            num = tot[:, :ML_V_DIM]
            den = tot[:, ML_V_DIM:]
            hc = num / jnp.maximum(jnp.abs(den), jnp.exp(-m_t))
            y = hc * lax.rsqrt(jnp.mean(hc * hc, axis=-1, keepdims=True) + NORM_EPS)
            o_ref[0, pl.ds(r0, L), hh * ML_V_DIM:(hh + 1) * ML_V_DIM] = (
                y * gain_ref[:, hh * ML_V_DIM:(hh + 1) * ML_V_DIM])

            g_row = b_last - b_row + i_row
            m_new = jnp.maximum(b_last + m_prev, jnp.max(g_row, axis=-1, keepdims=True))
            decay = jnp.exp(b_last + m_prev - m_new)
            kw = (kt.astype(F32) * jnp.exp(g_row - _twice(m_new))).astype(BF16)
            c_ref[hh] = _twice(decay) * ctn + _dot(kw, vx)
            m_ref[hh] = m_new
        return carry

    lax.fori_loop(0, S // L, chunk, 0)


def _mlstm(q, kt, v, g_rows, g_cols, out_gain):
    B, S, _ = q.shape
    assert ML_CHUNK == 2 * LANES
    nc = S // ML_CHUNK
    hp = ML_STEP_HEADS
    return pl.pallas_call(
        _mlstm_kernel,
        out_shape=jax.ShapeDtypeStruct((B, S, ML_HEADS * ML_V_DIM), F32),
        grid=(B, ML_HEADS // hp),
        in_specs=[
            pl.BlockSpec((1, S, hp * ML_QK_DIM), lambda b, p: (b, 0, p)),
            pl.BlockSpec((1, nc, hp * ML_QK_DIM, ML_CHUNK), lambda b, p: (b, 0, p, 0)),
            pl.BlockSpec((1, S, hp * ML_V_DIM), lambda b, p: (b, 0, p)),
            pl.BlockSpec((1, 1, nc, 2 * hp, ML_CHUNK), lambda b, p: (b, p, 0, 0, 0)),
            pl.BlockSpec((1, 1, nc, ML_CHUNK, LANES), lambda b, p: (b, p, 0, 0, 0)),
            pl.BlockSpec((1, hp * ML_V_DIM), lambda b, p: (0, p)),
        ],
        out_specs=pl.BlockSpec((1, S, hp * ML_V_DIM), lambda b, p: (b, 0, p)),
        scratch_shapes=[
            pltpu.VMEM((hp, ML_QK_DIM, ML_V_DIM + LANES), F32),
            pltpu.VMEM((hp, 1, LANES), F32),
        ],
        compiler_params=_cparams("parallel", "parallel"),
        name="mlstm_chunkwise",
    )(q, kt, v, g_rows, g_cols, out_gain)


def _ml_out_kernel(hh_ref, og_ref, w_ref, x_ref, g_ref, o_ref):
    a = jax.nn.sigmoid(og_ref[0]) * hh_ref[0]
    o_ref[0] = x_ref[0] + g_ref[0] * _dot(a.astype(BF16), w_ref[...])


def _ml_out(hh, og, w_out_bf, x, g1):
    B, S, D = x.shape
    tm = TOKEN_TILE
    row = lambda b, j: (b, j, 0)
    per_b = lambda b, j: (b, 0, 0)
    const2 = lambda b, j: (0, 0)
    return pl.pallas_call(
        _ml_out_kernel,
        out_shape=jax.ShapeDtypeStruct((B, S, D), F32),
        grid=(B, S // tm),
        in_specs=[
            pl.BlockSpec((1, tm, D), row),
            pl.BlockSpec((1, tm, D), row),
            pl.BlockSpec((D, D), const2),
            pl.BlockSpec((1, tm, D), row),
            pl.BlockSpec((1, 1, D), per_b),
        ],
        out_specs=pl.BlockSpec((1, tm, D), row),
        compiler_params=_cparams("parallel", "parallel"),
        name="mlstm_out_proj",
    )(hh, og, w_out_bf, x, g1)


def _first_argmax(x, lane, width):
    mx = jnp.max(x, axis=-1, keepdims=True)
    idx = jnp.min(jnp.where(x == mx, lane, width), axis=-1, keepdims=True)
    return mx, idx


def _router_kernel(x_ref, gain_ref, sc_ref, sh_ref, whi_ref, wlo_ref, b_ref, h_ref, r_ref, cnt_ref, run_ref):
    tm = x_ref.shape[1]

    @pl.when(jnp.logical_and(pl.program_id(0) == 0, pl.program_id(1) == 0))
    def _():
        run_ref[...] = jnp.zeros_like(run_ref)

    h = _modulated_norm(x_ref[0], gain_ref[...], sc_ref[0], sh_ref[0])
    h_hi = h.astype(BF16)
    bits = lax.bitcast_convert_type(h_hi.astype(F32), jnp.uint32)
    half = bits.shape[1] // 2
    packed = (bits[:, :half] >> 16) | (bits[:, half:] & jnp.uint32(0xFFFF0000))
    h_ref[0] = lax.bitcast_convert_type(packed, jnp.int32)
    h_lo = (h - h_hi.astype(F32)).astype(BF16)
    whi = whi_ref[...]
    logits = _dot(h_hi, whi) + _dot(h_lo, whi) + _dot(h_hi, wlo_ref[...]) + b_ref[...]
    lane = lax.broadcasted_iota(jnp.int32, (1, LANES), 1)
    lg = jnp.where(lane < N_GROUPS, logits, NEG_INF)
    g_max, g_sel = _first_argmax(lg, lane, LANES)
    pg = 1.0 / jnp.sum(jnp.exp(lg - g_max), axis=-1, keepdims=True)
    e_lane = lane - N_GROUPS
    in_grp = jnp.logical_and(e_lane >= g_sel * EXPERTS_PER_GROUP, e_lane < (g_sel + 1) * EXPERTS_PER_GROUP)
    le = jnp.where(in_grp, logits, NEG_INF)
    v1, i1 = _first_argmax(le, lane, LANES)
    le2 = jnp.where(lane == i1, NEG_INF, le)
    v2, i2 = _first_argmax(le2, lane, LANES)
    e2 = jnp.exp(v2 - v1)
    w1 = pg / (1.0 + e2)
    w2 = pg * e2 / (1.0 + e2)
    e1 = i1 - N_GROUPS
    e2 = i2 - N_GROUPS
    hot1 = lane == e1
    hot2 = lane == e2
    onehot = jnp.where(jnp.logical_or(hot1, hot2), 1.0, 0.0)
    r_i = lax.broadcasted_iota(jnp.int32, (tm, tm), 0)
    c_i = lax.broadcasted_iota(jnp.int32, (tm, tm), 1)
    before = jnp.where(c_i < r_i, 1.0, 0.0).astype(BF16)
    seen = _dot(before, onehot.astype(BF16)) + run_ref[...]
    rank1 = jnp.sum(jnp.where(hot1, seen, 0.0), axis=-1, keepdims=True)
    rank2 = jnp.sum(jnp.where(hot2, seen, 0.0), axis=-1, keepdims=True)
    run_ref[...] = run_ref[...] + jnp.sum(onehot, axis=0, keepdims=True)
    cnt_ref[...] = run_ref[...]

    out = jnp.where(lane == 0, e1.astype(F32), 0.0)
    out = jnp.where(lane == 1, e2.astype(F32), out)
    out = jnp.where(lane == 2, w1, out)
    out = jnp.where(lane == 3, w2, out)
    out = jnp.where(lane == 4, rank1, out)
    out = jnp.where(lane == 5, rank2, out)
    r_ref[0] = out


def _router(x, gain, sc, sh, w_hi, w_lo, bias):
    B, S, D = x.shape
    tm = TOKEN_TILE
    row = lambda b, j: (b, j, 0)
    per_b = lambda b, j: (b, 0, 0)
    const2 = lambda b, j: (0, 0)
    return pl.pallas_call(
        _router_kernel,
        out_shape=(
            jax.ShapeDtypeStruct((B, S, D // 2), jnp.int32),
            jax.ShapeDtypeStruct((B, S, LANES), F32),
            jax.ShapeDtypeStruct((1, LANES), F32),
        ),
        grid=(B, S // tm),
        in_specs=[
            pl.BlockSpec((1, tm, D), row),
            pl.BlockSpec((1, D), const2),
            pl.BlockSpec((1, 1, D), per_b),
            pl.BlockSpec((1, 1, D), per_b),
            pl.BlockSpec((D, LANES), const2),
            pl.BlockSpec((D, LANES), const2),
            pl.BlockSpec((1, LANES), const2),
        ],
        out_specs=(
            pl.BlockSpec((1, tm, D // 2), row),
            pl.BlockSpec((1, tm, LANES), row),
            pl.BlockSpec((1, LANES), const2),
        ),
        scratch_shapes=[pltpu.VMEM((1, LANES), F32)],
        compiler_params=_cparams("arbitrary", "arbitrary"),
        name="moe_router",
    )(x, gain, sc, sh, w_hi, w_lo, bias)


def _experts_kernel(blk_e_ref, n_used_ref, x_ref, wg_ref, wu_ref, wd_ref, o_ref, wg_s, wu_s, wd_s):
    i = pl.program_id(0)
    used = i < n_used_ref[0]
    new_expert = jnp.logical_or(i == 0, blk_e_ref[i] != blk_e_ref[jnp.maximum(i - 1, 0)])

    @pl.when(jnp.logical_and(used, new_expert))
    def _():
        wg_s[...] = wg_ref[0, 0].astype(BF16)
        wu_s[...] = wu_ref[0, 0].astype(BF16)
        wd_s[...] = wd_ref[0, 0].astype(BF16)

    @pl.when(used)
    def _():
        words = lax.bitcast_convert_type(x_ref[...], jnp.uint32)
        x = jnp.concatenate(
            [lax.bitcast_convert_type(words << 16, F32),
             lax.bitcast_convert_type(words & jnp.uint32(0xFFFF0000), F32)], axis=1).astype(BF16)
        a = _dot(x, wg_s[...])
        u = _dot(x, wu_s[...])
        act = a * jax.nn.sigmoid(a) * u
        o_ref[...] = _dot(act.astype(BF16), wd_s[...])

    @pl.when(i >= n_used_ref[0])
    def _():
        o_ref[...] = jnp.zeros_like(o_ref)


def _experts(layer, blk_e, n_used, xs, w_gate, w_up, w_down):
    R = xs.shape[0]
    D = 2 * xs.shape[1]
    n_blk = R // MOE_BLOCK
    grid_spec = pltpu.PrefetchScalarGridSpec(
        num_scalar_prefetch=2,
        grid=(n_blk,),
        in_specs=[
            pl.BlockSpec((MOE_BLOCK, D // 2), lambda i, be, nu: (i, 0)),
            pl.BlockSpec((1, 1, D, D_EXPERT), lambda i, be, nu: (layer, be[i], 0, 0)),
            pl.BlockSpec((1, 1, D, D_EXPERT), lambda i, be, nu: (layer, be[i], 0, 0)),
            pl.BlockSpec((1, 1, D_EXPERT, D), lambda i, be, nu: (layer, be[i], 0, 0)),
        ],
        out_specs=pl.BlockSpec((MOE_BLOCK, D), lambda i, be, nu: (i, 0)),
        scratch_shapes=[
            pltpu.VMEM((D, D_EXPERT), BF16),
            pltpu.VMEM((D, D_EXPERT), BF16),
            pltpu.VMEM((D_EXPERT, D), BF16),
        ],
    )
    return pl.pallas_call(
        _experts_kernel,
        out_shape=jax.ShapeDtypeStruct((R, D), F32),
        grid_spec=grid_spec,
        compiler_params=_cparams("arbitrary"),
        name="moe_experts",
    )(blk_e, n_used, xs, w_gate, w_up, w_down)


def _combine_kernel(x_ref, g_ref, y0_ref, y1_ref, r_ref, o_ref):
    r = r_ref[0]
    y = y0_ref[0, 0] * r[:, 2:3] + y1_ref[0, 0] * r[:, 3:4]
    o_ref[0] = x_ref[0] + g_ref[0] * y


def _combine(x, g2, y01, route):
    B, S, D = x.shape
    tm = TOKEN_TILE
    row = lambda b, j: (b, j, 0)
    per_b = lambda b, j: (b, 0, 0)
    return pl.pallas_call(
        _combine_kernel,
        out_shape=jax.ShapeDtypeStruct((B, S, D), F32),
        grid=(B, S // tm),
        in_specs=[
            pl.BlockSpec((1, tm, D), row),
            pl.BlockSpec((1, 1, D), per_b),
            pl.BlockSpec((1, 1, tm, D), lambda b, j: (0, b, j, 0)),
            pl.BlockSpec((1, 1, tm, D), lambda b, j: (1, b, j, 0)),
            pl.BlockSpec((1, tm, LANES), row),
        ],
        out_specs=pl.BlockSpec((1, tm, D), row),
        compiler_params=_cparams("parallel", "parallel"),
        name="moe_combine",
    )(x, g2, y01, y01, route)


SC_CORES = 2
SC_SUBCORES = 16
SC_WORKERS = SC_CORES * SC_SUBCORES
SC_CHUNK = 32


def _sc_mesh():
    return plsc.VectorSubcoreMesh(core_axis_name="c", subcore_axis_name="s",
                                  num_cores=SC_CORES, num_subcores=SC_SUBCORES)


def _sc_scatter_rows(src, idx, n_out):
    T, W = src.shape
    per_w = T // SC_WORKERS
    nch = per_w // SC_CHUNK
    idx4 = idx.reshape(TOP_K, SC_WORKERS, nch, SC_CHUNK)

    @functools.partial(
        pl.kernel, mesh=_sc_mesh(),
        out_type=jax.ShapeDtypeStruct((n_out, W), src.dtype),
        scratch_types=[pltpu.VMEM((TOP_K, nch, SC_CHUNK), jnp.int32), pltpu.VMEM((SC_CHUNK, W), src.dtype)],
        name="sc_scatter_rows",
    )
    def body(src_hbm, idx_hbm, out_hbm, idx_v, rows_v):
        wid = lax.axis_index("s") * SC_CORES + lax.axis_index("c")
        for s in range(TOP_K):
            pltpu.sync_copy(idx_hbm.at[s, wid], idx_v.at[s])

        @pl.loop(0, nch)
        def _(i):
            pltpu.sync_copy(src_hbm.at[pl.ds(wid * per_w + i * SC_CHUNK, SC_CHUNK)], rows_v)
            for s in range(TOP_K):
                pltpu.sync_copy(rows_v, out_hbm.at[idx_v.at[s, i]])

    return body(src, idx4)


def _sc_gather_rows(table, idx):
    N = idx.shape[0]
    W = table.shape[1]
    per_w = N // SC_WORKERS
    nch = per_w // SC_CHUNK
    idx3 = idx.reshape(SC_WORKERS, nch, SC_CHUNK)

    @functools.partial(
        pl.kernel, mesh=_sc_mesh(),
        out_type=jax.ShapeDtypeStruct((N, W), table.dtype),
        scratch_types=[
            pltpu.VMEM((nch, SC_CHUNK), jnp.int32),
            pltpu.VMEM((2, SC_CHUNK, W), table.dtype),
            pltpu.SemaphoreType.DMA((2,)),
            pltpu.SemaphoreType.DMA((2,)),
        ],
        name="sc_gather_rows",
    )
    def body(table_hbm, idx_hbm, out_hbm, idx_v, rows_v, gather_sem, write_sem):
        wid = lax.axis_index("s") * SC_CORES + lax.axis_index("c")
        pltpu.sync_copy(idx_hbm.at[wid], idx_v)

        def gather(j, b):
            return pltpu.make_async_copy(table_hbm.at[idx_v.at[j]], rows_v.at[b], gather_sem.at[b])

        def write(j, b):
            return pltpu.make_async_copy(
                rows_v.at[b], out_hbm.at[pl.ds(wid * per_w + j * SC_CHUNK, SC_CHUNK)], write_sem.at[b])

        gather(0, 0).start()

        @pl.loop(0, nch, step=2)
        def _(i):
            for b in range(2):
                j = i + b

                @pl.when(j >= 1)
                def _():
                    write(j - 1, 1 - b).wait()

                @pl.when(j + 1 < nch)
                def _():
                    gather(j + 1, 1 - b).start()

                gather(j, b).wait()
                write(j, b).start()

        write(nch - 1, (nch - 1) % 2).wait()

    assert nch % 2 == 0
    return body(table, idx3)


def _moe_dispatch(route, counts, T):
    A = T * TOP_K
    counts = counts[0, :N_EXPERTS].astype(jnp.int32)
    blocks_per = (counts + MOE_BLOCK - 1) // MOE_BLOCK
    block_end = jnp.cumsum(blocks_per)
    block_start = block_end - blocks_per
    expert = route[:, :TOP_K].astype(jnp.int32)
    rank = route[:, 4:4 + TOP_K].astype(jnp.int32)
    onehot = expert[:, :, None] == jnp.arange(N_EXPERTS, dtype=jnp.int32)
    start = jnp.sum(jnp.where(onehot, block_start, 0), axis=-1)
    dest = (start * MOE_BLOCK + rank).T
    n_blk = -(-A // MOE_BLOCK) + N_EXPERTS
    blk = jnp.arange(n_blk, dtype=jnp.int32)
    blk_e = jnp.minimum(jnp.sum(blk[:, None] >= block_end[None, :], axis=-1), N_EXPERTS - 1)
    return dest, n_blk * MOE_BLOCK, blk_e.astype(jnp.int32), block_end[-1:].astype(jnp.int32)


def _rope_tables(S):
    inv = 1.0 / (ROPE_THETA ** (jnp.arange(0, HEAD_DIM, 2, dtype=F32) / HEAD_DIM))
    ang = jnp.arange(S, dtype=F32)[:, None] * inv[None, :]
    cos, sin = jnp.cos(ang), jnp.sin(ang)
    cos_h = jnp.concatenate([cos, cos], axis=-1)
    sin_h = jnp.concatenate([-sin, sin], axis=-1)
    return jnp.tile(cos_h, (1, ATTN_HEADS)), jnp.tile(sin_h, (1, ATTN_HEADS))


def _pad_cols(w, width):
    return jnp.pad(w, ((0, 0), (0, width - w.shape[1])))


def kernel(x, c, ada_w, ada_b, norm_mix, norm_ffn, hy_w_in, hy_q_norm, hy_k_norm, hy_conv_w, hy_w_out, ml_w_in, ml_b_gates, ml_out_norm, ml_w_out, moe_w_group, moe_b_group, moe_w_expert, moe_b_expert, moe_w_gate, moe_w_up, moe_w_down):
    B, S, D = x.shape
    T = B * S
    cos_t, sin_t = _rope_tables(S)
    mod = _ada_modulation(c, ada_w, ada_b).reshape(DEPTH, B, 6, 1, D)
    r_i = np.arange(ATTN_WIDTH)
    grp = jnp.asarray((r_i[:, None] // HEAD_DIM) == (r_i[None, :] // HEAD_DIM), dtype=BF16)

    for l in range(DEPTH):
        sh1, sc1, g1, sh2, sc2, g2 = [mod[l, :, i] for i in range(6)]
        gain1 = norm_mix[l].reshape(1, D)
        j = l // 2
        if l % 2 == 0:
            w = hy_w_in[j]
            o = np.cumsum((0,) + (ATTN_WIDTH, HEAD_DIM, HEAD_DIM, IDX_HEADS * IDX_DIM, IDX_DIM, IDX_HEADS,
                                  CONV_WIDTH, CONV_WIDTH, CONV_WIDTH))
            wq, wk, wv, wiq, wik, wiw, wbg, wcg, wu = [w[:, o[i]:o[i + 1]] for i in range(9)]
            w_pad = jnp.concatenate(
                [wq, wiq, wbg, wcg, wu, wk, wv, _pad_cols(jnp.concatenate([wik, wiw], axis=1), LANES)],
                axis=1).astype(BF16)
            qn_t = jnp.tile(hy_q_norm[j], ATTN_HEADS).reshape(1, ATTN_WIDTH)
            kn_t = jnp.tile(hy_k_norm[j], LANES // HEAD_DIM).reshape(1, LANES)
            qt, iqt, bcu, kv, kvt, ik, iwt = _hyb_in(x, gain1, sc1, sh1, w_pad, cos_t, sin_t, qn_t, kn_t, grp)
            y_attn = _dsa_attention_t(qt, iqt, iwt, kv, kvt, ik)
            x = _hyb_out(y_attn, bcu, hy_conv_w[j], hy_w_out[j].astype(BF16), x, g1)
        else:
            w = ml_w_in[j]
            hq = ML_HEADS * ML_QK_DIM
            hv = ML_HEADS * ML_V_DIM
            wq, wk, wv = w[:, :hq], w[:, hq:2 * hq], w[:, 2 * hq:2 * hq + hv]
            wg = w[:, 2 * hq + hv:2 * hq + hv + 2 * ML_HEADS]
            wo = w[:, 2 * hq + hv + 2 * ML_HEADS:]
            w_pad = jnp.concatenate([wq, wk, wv, wo, _pad_cols(wg, LANES)], axis=1).astype(BF16)
            gate_bias = jnp.pad(ml_b_gates[j], (0, LANES - 2 * ML_HEADS)).reshape(1, LANES)
            q, k, v, og, gates = _ml_in(x, gain1, sc1, sh1, w_pad, gate_bias)
            nc = S // ML_CHUNK
            hp = ML_STEP_HEADS
            groups = ML_HEADS // hp
            gi = gates[:, :, :ML_HEADS].reshape(B, S, groups, hp)
            gf = gates[:, :, ML_HEADS:2 * ML_HEADS].reshape(B, S, groups, hp)
            gp = jnp.concatenate([gi, gf], axis=-1)
            gp = jnp.transpose(gp, (0, 2, 1, 3)).reshape(B, groups, nc, ML_CHUNK, 2 * hp)
            g_cols = jnp.pad(gp, ((0, 0),) * 4 + ((0, LANES - 2 * hp),))
            g_rows = jnp.swapaxes(gp, -1, -2)
            hh = _mlstm(q, k, v, g_rows, g_cols, ml_out_norm[j].reshape(1, hv))
            x = _ml_out(hh, og, ml_w_out[j].astype(BF16), x, g1)

        w_r = _pad_cols(jnp.concatenate([moe_w_group[l], moe_w_expert[l]], axis=1), LANES)
        w_hi = w_r.astype(BF16)
        w_lo = (w_r - w_hi.astype(F32)).astype(BF16)
        b_r = jnp.pad(jnp.concatenate([moe_b_group[l], moe_b_expert[l]]), (0, LANES - N_GROUPS - N_EXPERTS))
        h2, route, counts = _router(x, norm_ffn[l].reshape(1, D), sc2, sh2, w_hi, w_lo, b_r.reshape(1, LANES))
        dest, n_rows, blk_e, n_used = _moe_dispatch(route.reshape(T, LANES), counts, T)
        xs = _sc_scatter_rows(h2.reshape(T, D // 2), dest, n_rows)
        ys = _experts(l, blk_e, n_used, xs, moe_w_gate, moe_w_up, moe_w_down)
        y01 = _sc_gather_rows(ys, dest.reshape(TOP_K * T)).reshape(TOP_K, B, S, D)
        x = _combine(x, g2, y01, route)
    return x
```

```python
import functools

import numpy as np
import jax
import jax.numpy as jnp
from jax import lax
from jax.experimental import pallas as pl
from jax.experimental.pallas import tpu as pltpu
from jax.experimental.pallas import tpu_sc as plsc

F32 = jnp.float32
BF16 = jnp.bfloat16
HIGHEST = lax.Precision.HIGHEST

D_MODEL = 1024
DEPTH = 4
ATTN_HEADS = 8
HEAD_DIM = 64
ATTN_WIDTH = ATTN_HEADS * HEAD_DIM
IDX_HEADS = 8
IDX_DIM = 64
INDEX_TOPK = 256
Q_BLOCK = 256
ROPE_THETA = 10000.0
CONV_WIDTH = D_MODEL - ATTN_WIDTH
CONV_K = 3
ML_HEADS = 8
ML_QK_DIM = 64
ML_V_DIM = 128
N_GROUPS = 4
EXPERTS_PER_GROUP = 8
N_EXPERTS = N_GROUPS * EXPERTS_PER_GROUP
TOP_K = 2
D_EXPERT = 512
MOE_BLOCK = 256
NORM_EPS = 1e-6

LANES = 128
VMEM_LIMIT = 56 * 1024 * 1024
TOKEN_TILE = 256
ML_CHUNK = 256
ML_STEP_HEADS = 8
NEG_INF = float("-inf")


def _cparams(*sem):
    return pltpu.CompilerParams(dimension_semantics=sem, vmem_limit_bytes=VMEM_LIMIT)


def _dot(a, b):
    return jnp.dot(a, b, preferred_element_type=F32)


def _pack_bf16_pairs(x):
    bits = lax.bitcast_convert_type(x.astype(BF16).astype(F32), jnp.uint32)
    half = bits.shape[1] // 2
    packed = (bits[:, :half] >> 16) | (bits[:, half:] & jnp.uint32(0xFFFF0000))
    return lax.bitcast_convert_type(packed, jnp.int32)


def _unpack_bf16_pairs(words):
    words = lax.bitcast_convert_type(words, jnp.uint32)
    return jnp.concatenate(
        [lax.bitcast_convert_type(words << 16, F32),
         lax.bitcast_convert_type(words & jnp.uint32(0xFFFF0000), F32)], axis=1)


def _split_dot(a_f32, b_bf16):
    hi = a_f32.astype(BF16)
    lo = (a_f32 - hi.astype(F32)).astype(BF16)
    return _dot(hi, b_bf16) + _dot(lo, b_bf16)


def _ada_kernel(c_ref, w_ref, b_ref, o_ref):
    c = c_ref[...]
    ca = c * jax.nn.sigmoid(c)
    o_ref[0] = jnp.dot(ca, w_ref[0], precision=HIGHEST, preferred_element_type=F32) + b_ref[0]


def _ada_modulation(c, ada_w, ada_b):
    B, D = c.shape
    n_col = ada_w.shape[-1] // D
    return pl.pallas_call(
        _ada_kernel,
        out_shape=jax.ShapeDtypeStruct((DEPTH, B, n_col * D), F32),
        grid=(DEPTH, n_col),
        in_specs=[
            pl.BlockSpec((B, D), lambda l, j: (0, 0)),
            pl.BlockSpec((1, D, D), lambda l, j: (l, 0, j)),
            pl.BlockSpec((1, 1, D), lambda l, j: (l, 0, j)),
        ],
        out_specs=pl.BlockSpec((1, B, D), lambda l, j: (l, 0, j)),
        compiler_params=_cparams("parallel", "parallel"),
        name="ada_modulation",
    )(c, ada_w, ada_b.reshape(DEPTH, 1, n_col * D))


def _modulated_norm(x, gain, scale, shift):
    y = x * lax.rsqrt(jnp.mean(x * x, axis=-1, keepdims=True) + NORM_EPS)
    return y * gain * (1.0 + scale) + shift


def _rope(x, cos, sin_signed, first_half):
    w = x.shape[-1]
    partner = jnp.where(first_half, pltpu.roll(x, w - HEAD_DIM // 2, 1), pltpu.roll(x, HEAD_DIM // 2, 1))
    return x * cos + partner * sin_signed


HYB_COLS = 5 * 512 + 2 * LANES


def _hyb_in_kernel(x_ref, gain_ref, sc_ref, sh_ref, w_ref, cos_ref, sin_ref, qn_ref, kn_ref, grp_ref,
                   qt_ref, iqt_ref, bcu_ref, kv_ref, kvt_ref, ik_ref, iwt_ref):
    h = _modulated_norm(x_ref[0], gain_ref[...], sc_ref[0], sh_ref[0])
    p = _dot(h.astype(BF16), w_ref[...])
    cos = cos_ref[...]
    sin = sin_ref[...]
    lane = lax.broadcasted_iota(jnp.int32, (1, ATTN_WIDTH), 1)
    first_half = (lane % HEAD_DIM) < (HEAD_DIM // 2)
    fh128 = first_half[:, :LANES]
    lane128 = lane[:, :LANES]

    q = p[:, 0:512]
    ms = _split_dot(q * q, grp_ref[...]) * (1.0 / HEAD_DIM)
    q = q * lax.rsqrt(ms + NORM_EPS) * qn_ref[...]
    qt_ref[0] = (_rope(q, cos, sin, first_half) * (HEAD_DIM ** -0.5)).T.astype(BF16)

    iq = p[:, 512:1024]
    iqt_ref[0] = (_rope(iq, cos, sin, first_half) * (IDX_DIM ** -0.5)).T.astype(BF16)

    bcu_ref[0] = p[:, 1024:2560]

    kv = p[:, 2560:2688]
    is_k = lane128 < HEAD_DIM
    kk = jnp.where(is_k, kv, 0.0)
    ms_k = jnp.sum(kk * kk, axis=-1, keepdims=True) * (1.0 / HEAD_DIM)
    kn = kv * lax.rsqrt(ms_k + NORM_EPS) * kn_ref[...]
    kr = _rope(kn, cos[:, :LANES], sin[:, :LANES], fh128)
    kv = jnp.where(is_k, kr, kv)
    kv_ref[0] = kv.astype(BF16)
    kvt_ref[0, 0] = kv.T.astype(BF16)

    sm = p[:, 2688:2816]
    ikr = _rope(sm, cos[:, :LANES], sin[:, :LANES], fh128)
    ik_ref[0] = jnp.where(is_k, ikr, 0.0).astype(BF16)
    iwt_ref[0] = sm.T


def _hyb_in(x, gain, sc, sh, w_pad, cos_t, sin_t, qn_t, kn_t, grp):
    B, S, D = x.shape
    tm = TOKEN_TILE
    row = lambda b, j: (b, j, 0)
    per_b = lambda b, j: (b, 0, 0)
    const2 = lambda b, j: (0, 0)
    tab = lambda b, j: (j, 0)
    col = lambda b, j: (b, 0, j)
    assert tm == DSA_KEY_CHUNK
    return pl.pallas_call(
        _hyb_in_kernel,
        out_shape=(
            jax.ShapeDtypeStruct((B, 512, S), BF16),
            jax.ShapeDtypeStruct((B, 512, S), BF16),
            jax.ShapeDtypeStruct((B, S, 1536), F32),
            jax.ShapeDtypeStruct((B, S, LANES), BF16),
            jax.ShapeDtypeStruct((B, S // tm, LANES, tm), BF16),
            jax.ShapeDtypeStruct((B, S, LANES), BF16),
            jax.ShapeDtypeStruct((B, LANES, S), F32),
        ),
        grid=(B, S // tm),
        in_specs=[
            pl.BlockSpec((1, tm, D), row),
            pl.BlockSpec((1, D), const2),
            pl.BlockSpec((1, 1, D), per_b),
            pl.BlockSpec((1, 1, D), per_b),
            pl.BlockSpec((D, HYB_COLS), const2),
            pl.BlockSpec((tm, 512), tab),
            pl.BlockSpec((tm, 512), tab),
            pl.BlockSpec((1, 512), const2),
            pl.BlockSpec((1, LANES), const2),
            pl.BlockSpec((512, 512), const2),
        ],
        out_specs=(
            pl.BlockSpec((1, 512, tm), col),
            pl.BlockSpec((1, 512, tm), col),
            pl.BlockSpec((1, tm, 1536), row),
            pl.BlockSpec((1, tm, LANES), row),
            pl.BlockSpec((1, 1, LANES, tm), lambda b, j: (b, j, 0, 0)),
            pl.BlockSpec((1, tm, LANES), row),
            pl.BlockSpec((1, LANES, tm), col),
        ),
        compiler_params=_cparams("parallel", "parallel"),
        name="hybrid_in_proj",
    )(x, gain, sc, sh, w_pad, cos_t, sin_t, qn_t, kn_t, grp)


DSA_KEY_CHUNK = 256
F32_LOWEST = float(np.finfo(np.float32).min)


def _fold8(x, op):
    parts = x.reshape(x.shape[0] // 8, 8, x.shape[1])
    while parts.shape[0] > 1:
        half = parts.shape[0] // 2
        assert parts.shape[0] == 2 * half
        parts = op(parts[:half], parts[half:])
    return parts[0]


def _col_reduce(x, op):
    t = _fold8(x, op)
    for shift in (4, 2, 1):
        t = op(t, pltpu.roll(t, shift, 0))
    return t[0:1, :]


def _dsa_t_kernel(qt_ref, iqt_ref, iwt_ref, kv_ref, kvt_ref, ik_ref, o_ref, sc_ref, bias_ref, acc_ref):
    CK = DSA_KEY_CHUNK
    QB = Q_BLOCK
    qb = pl.program_id(1)
    nk = (qb * QB + QB + CK - 1) // CK
    kf = float(INDEX_TOPK)
    qpos = qb * QB + lax.broadcasted_iota(jnp.int32, (1, QB), 1)
    krow = lax.broadcasted_iota(jnp.int32, (CK, 1), 0)
    w_idx = iwt_ref[0, IDX_DIM:IDX_DIM + IDX_HEADS, :] * (IDX_HEADS ** -0.5)

    def rows(c):
        return pl.ds(pl.multiple_of(c * CK, CK), CK)

    def heads_on_lanes(ref, width):
        return jnp.concatenate([ref[0, hd * width:(hd + 1) * width, :] for hd in range(ref.shape[1] // width)], axis=1)

    def head_lanes(hd):
        return slice(hd * QB, (hd + 1) * QB)

    iq_wide = heads_on_lanes(iqt_ref, IDX_DIM)
    w_wide = jnp.concatenate([w_idx[hd:hd + 1, :] for hd in range(IDX_HEADS)], axis=1)

    def score_chunk(c, carry):
        mx, mn = carry
        ikc = ik_ref[0, rows(c), :][:, :IDX_DIM]
        s_all = jnp.maximum(_dot(ikc, iq_wide), 0.0) * w_wide
        acc = s_all[:, head_lanes(0)]
        for hd in range(1, IDX_HEADS):
            acc = acc + s_all[:, head_lanes(hd)]
        causal = (c * CK + krow) <= qpos
        sc_ref[rows(c), :] = jnp.where(causal, acc, NEG_INF)
        mx = jnp.maximum(mx, _fold8(jnp.where(causal, acc, NEG_INF), jnp.maximum))
        mn = jnp.minimum(mn, _fold8(jnp.where(causal, acc, jnp.inf), jnp.minimum))
        return mx, mn

    mx8, mn8 = lax.fori_loop(0, nk, score_chunk,
                             (jnp.full((8, QB), NEG_INF, F32), jnp.full((8, QB), jnp.inf, F32)))
    row_max = jnp.max(mx8, axis=0, keepdims=True)
    row_min = jnp.min(mn8, axis=0, keepdims=True)

    @pl.when(nk % 2 == 1)
    def _():
        sc_ref[rows(nk), :] = jnp.full((CK, QB), NEG_INF, F32)

    n_pairs = (nk + 1) // 2

    def pair_rows(c):
        return pl.ds(pl.multiple_of(c * (2 * CK), 2 * CK), 2 * CK)

    def count(pred):
        def body(c, part):
            return part + _fold8(jnp.where(pred(sc_ref[pair_rows(c), :]), 1.0, 0.0), jnp.add)
        part = lax.fori_loop(0, n_pairs, body, jnp.zeros((8, QB), F32))
        return jnp.sum(part, axis=0, keepdims=True)

    @pl.when(qb * QB + QB <= INDEX_TOPK)
    def _():
        def body(c, carry):
            bias_ref[rows(c), :] = jnp.where(sc_ref[rows(c), :] > NEG_INF, 0.0, NEG_INF)
            return carry
        lax.fori_loop(0, nk, body, 0)

    @pl.when(qb * QB + QB > INDEX_TOPK)
    def _():
        top_tied = count(lambda x: x >= row_max) >= kf

        def bisect(_, carry):
            lo, hi = carry
            mid = 0.5 * lo + 0.5 * hi
            ge = count(lambda x: x >= mid) >= kf
            return jnp.where(ge, mid, lo), jnp.where(ge, hi, mid)

        lo, hi = lax.fori_loop(0, 18, bisect, (row_min, row_max))

        def refine_cond(carry):
            it, _, _, done = carry
            return jnp.logical_and(it < nk * CK, jnp.min(done) < 0.5)

        def refine(carry):
            it, hi, thr, done = carry

            def below(c, part):
                x = sc_ref[pair_rows(c), :]
                return jnp.maximum(part, _fold8(jnp.where(x < hi, x, NEG_INF), jnp.maximum))

            m = jnp.max(lax.fori_loop(0, n_pairs, below, jnp.full((8, QB), NEG_INF, F32)), axis=0, keepdims=True)
            hit = count(lambda x: x >= m) >= kf
            fin = done > 0.5
            thr = jnp.where(fin, thr, m)
            hi = jnp.where(jnp.logical_or(fin, hit), hi, m)
            done = jnp.where(hit, 1.0, done)
            return it + 1, hi, thr, done

        done0 = jnp.where(top_tied, 1.0, 0.0)
        _, _, thr, _ = lax.while_loop(refine_cond, refine, (jnp.int32(0), hi, row_max, done0))

        need = kf - count(lambda x: x > thr)
        n_eq = count(lambda x: x == thr)
        tied = jnp.max(n_eq - need) > 0.5

        @pl.when(jnp.logical_not(tied))
        def _():
            def body(c, carry):
                bias_ref[rows(c), :] = jnp.where(sc_ref[rows(c), :] >= thr, 0.0, NEG_INF)
                return carry
            lax.fori_loop(0, nk, body, 0)

        @pl.when(tied)
        def _():
            r_i = lax.broadcasted_iota(jnp.int32, (CK, CK), 0)
            c_i = lax.broadcasted_iota(jnp.int32, (CK, CK), 1)
            lower = jnp.where(c_i <= r_i, 1.0, 0.0).astype(BF16)

            def body(c, seen):
                x = sc_ref[rows(c), :]
                eq = x == thr
                eq_f = jnp.where(eq, 1.0, 0.0)
                rank = _dot(lower, eq_f.astype(BF16)) + seen
                keep = jnp.logical_or(x > thr, jnp.logical_and(eq, rank <= need))
                bias_ref[rows(c), :] = jnp.where(keep, 0.0, NEG_INF)
                return seen + jnp.sum(eq_f, axis=0, keepdims=True)

            lax.fori_loop(0, nk, body, jnp.zeros((1, QB), F32))

    acc_ref[...] = jnp.zeros_like(acc_ref)
    q_wide = heads_on_lanes(qt_ref, HEAD_DIM)

    def attend_chunk(c, carry):
        m_old, l_old = carry
        kc = kv_ref[0, rows(c), :][:, :HEAD_DIM]
        vt = kvt_ref[0, c, HEAD_DIM:, :]
        bias = bias_ref[rows(c), :]
        logits = _dot(kc, q_wide) + jnp.concatenate([bias] * ATTN_HEADS, axis=1)
        m_new = jnp.maximum(m_old, _col_reduce(logits, jnp.maximum))
        alpha = jnp.exp(m_old - m_new)
        p = jnp.exp(logits - m_new)
        l_new = alpha * l_old + _col_reduce(p, jnp.add)
        acc_ref[...] = alpha * acc_ref[...] + _dot(vt, p.astype(BF16))
        return m_new, l_new

    _, l_all = lax.fori_loop(0, nk, attend_chunk, (jnp.full((1, ATTN_HEADS * QB), -1e30, F32),
                                                   jnp.zeros((1, ATTN_HEADS * QB), F32)))
    out_t = acc_ref[...] / l_all
    o_ref[0] = jnp.concatenate([out_t[:, head_lanes(hd)] for hd in range(ATTN_HEADS)], axis=0).T


def _dsa_attention_t(qt, iqt, iwt, kv, kvt, ik):
    B, _, S = qt.shape
    col = lambda b, j: (b, 0, j)
    per_b = lambda b, j: (b, 0, 0)
    return pl.pallas_call(
        _dsa_t_kernel,
        out_shape=jax.ShapeDtypeStruct((B, S, ATTN_WIDTH), F32),
        grid=(B, S // Q_BLOCK),
        in_specs=[
            pl.BlockSpec((1, ATTN_WIDTH, Q_BLOCK), col),
            pl.BlockSpec((1, IDX_HEADS * IDX_DIM, Q_BLOCK), col),
            pl.BlockSpec((1, LANES, Q_BLOCK), col),
            pl.BlockSpec((1, S, LANES), per_b),
            pl.BlockSpec((1, S // DSA_KEY_CHUNK, LANES, DSA_KEY_CHUNK), lambda b, j: (b, 0, 0, 0)),
            pl.BlockSpec((1, S, LANES), per_b),
        ],
        out_specs=pl.BlockSpec((1, Q_BLOCK, ATTN_WIDTH), lambda b, j: (b, j, 0)),
        scratch_shapes=[
            pltpu.VMEM((S, Q_BLOCK), F32),
            pltpu.VMEM((S, Q_BLOCK), F32),
            pltpu.VMEM((HEAD_DIM, ATTN_HEADS * Q_BLOCK), F32),
        ],
        compiler_params=_cparams("parallel", "parallel"),
        name="dsa_attention",
    )(qt, iqt, iwt, kv, kvt, ik)


def _hyb_out_kernel(ya_ref, bcu_ref, halo_ref, cw_ref, w_ref, x_ref, g_ref, o_ref):
    j = pl.program_id(1)
    tm = ya_ref.shape[1]
    bcu = bcu_ref[0]
    bg = bcu[:, 0:512]
    z = bcu[:, 512:1024] * bcu[:, 1024:1536]
    halo = halo_ref[0]
    zh = halo[:, 512:1024] * halo[:, 1024:1536]
    zh = jnp.where(j > 0, zh, 0.0)
    row = lax.broadcasted_iota(jnp.int32, (tm, 1), 0)
    z1 = jnp.where(row >= 1, pltpu.roll(z, 1, 0), zh[7:8, :])
    z2 = jnp.where(row >= 2, pltpu.roll(z, 2, 0), jnp.where(row == 1, zh[7:8, :], zh[6:7, :]))
    cw = cw_ref[...]
    y_conv = bg * (z2 * cw[0:1, :] + z1 * cw[1:2, :] + z * cw[2:3, :])
    y = _dot(ya_ref[0].astype(BF16), w_ref[0:512, :]) + _dot(y_conv.astype(BF16), w_ref[512:1024, :])
    o_ref[0] = x_ref[0] + g_ref[0] * y


def _hyb_out(y_attn, bcu, conv_w, w_out_bf, x, g1):
    B, S, D = x.shape
    tm = TOKEN_TILE
    row = lambda b, j: (b, j, 0)
    per_b = lambda b, j: (b, 0, 0)
    const2 = lambda b, j: (0, 0)
    halo = lambda b, j: (b, jnp.maximum(j * (tm // 8) - 1, 0), 0)
    return pl.pallas_call(
        _hyb_out_kernel,
        out_shape=jax.ShapeDtypeStruct((B, S, D), F32),
        grid=(B, S // tm),
        in_specs=[
            pl.BlockSpec((1, tm, 512), row),
            pl.BlockSpec((1, tm, 1536), row),
            pl.BlockSpec((1, 8, 1536), halo),
            pl.BlockSpec((CONV_K, CONV_WIDTH), const2),
            pl.BlockSpec((D, D), const2),
            pl.BlockSpec((1, tm, D), row),
            pl.BlockSpec((1, 1, D), per_b),
        ],
        out_specs=pl.BlockSpec((1, tm, D), row),
        compiler_params=_cparams("parallel", "parallel"),
        name="hybrid_out_proj",
    )(y_attn, bcu, bcu, conv_w, w_out_bf, x, g1)


ML_COLS = 512 + 512 + 1024 + 1024 + LANES


def _ml_in_kernel(x_ref, gain_ref, sc_ref, sh_ref, w_ref, bias_ref, q_ref, k_ref, v_ref, og_ref, gt_ref):
    h = _modulated_norm(x_ref[0], gain_ref[...], sc_ref[0], sh_ref[0])
    p = _dot(h.astype(BF16), w_ref[...])
    q_ref[0] = (p[:, 0:512] * (ML_QK_DIM ** -0.5)).astype(BF16)
    k_ref[0, 0] = p[:, 512:1024].T.astype(BF16)
    v_ref[0] = p[:, 1024:2048].astype(BF16)
    og_ref[0] = p[:, 2048:3072]
    gt_ref[0] = p[:, 3072:3200] + bias_ref[...]


def _ml_in(x, gain, sc, sh, w_pad, gate_bias):
    B, S, D = x.shape
    tm = TOKEN_TILE
    assert tm == ML_CHUNK
    row = lambda b, j: (b, j, 0)
    per_b = lambda b, j: (b, 0, 0)
    const2 = lambda b, j: (0, 0)
    return pl.pallas_call(
        _ml_in_kernel,
        out_shape=(
            jax.ShapeDtypeStruct((B, S, 512), BF16),
            jax.ShapeDtypeStruct((B, S // tm, 512, tm), BF16),
            jax.ShapeDtypeStruct((B, S, 1024), BF16),
            jax.ShapeDtypeStruct((B, S, 1024), F32),
            jax.ShapeDtypeStruct((B, S, LANES), F32),
        ),
        grid=(B, S // tm),
        in_specs=[
            pl.BlockSpec((1, tm, D), row),
            pl.BlockSpec((1, D), const2),
            pl.BlockSpec((1, 1, D), per_b),
            pl.BlockSpec((1, 1, D), per_b),
            pl.BlockSpec((D, ML_COLS), const2),
            pl.BlockSpec((1, LANES), const2),
        ],
        out_specs=(
            pl.BlockSpec((1, tm, 512), row),
            pl.BlockSpec((1, 1, 512, tm), lambda b, j: (b, j, 0, 0)),
            pl.BlockSpec((1, tm, 1024), row),
            pl.BlockSpec((1, tm, 1024), row),
            pl.BlockSpec((1, tm, LANES), row),
        ),
        compiler_params=_cparams("parallel", "parallel"),
        name="mlstm_in_proj",
    )(x, gain, sc, sh, w_pad, gate_bias)


def _log_sigmoid(f):
    return jnp.minimum(f, 0.0) - jnp.log1p(jnp.exp(-jnp.abs(f)))


def _split3(x):
    a = x.astype(BF16)
    r = x - a.astype(F32)
    b = r.astype(BF16)
    c = (r - b.astype(F32)).astype(BF16)
    return a, b, c


def _twice(a):
    return jnp.concatenate([a, a], axis=1)


def _mlstm_kernel(q_ref, kt_ref, v_ref, grow_ref, gcol_ref, gain_ref, o_ref, c_ref, m_ref):
    L = ML_CHUNK
    HP = ML_STEP_HEADS
    S = q_ref.shape[1]
    c_ref[...] = jnp.zeros_like(c_ref)
    m_ref[...] = jnp.zeros_like(m_ref)

    def chunk(c, carry):
        r0 = pl.multiple_of(c * L, L)
        r_i = lax.broadcasted_iota(jnp.int32, (L, L), 0)
        c_i = lax.broadcasted_iota(jnp.int32, (L, L), 1)
        tril = c_i <= r_i
        lower = jnp.where(tril, 1.0, 0.0).astype(BF16)
        upper = jnp.where(r_i <= c_i, 1.0, 0.0).astype(BF16)
        e_r = lax.broadcasted_iota(jnp.int32, (LANES, HP * LANES), 0)
        e_c = lax.broadcasted_iota(jnp.int32, (LANES, HP * LANES), 1)
        pick = jnp.where(e_r == HP + e_c // LANES, 1.0, 0.0).astype(BF16)
        rows = grow_ref[0, 0, c]
        cols = gcol_ref[0, 0, c]
        b_rows = sum(_dot(p, upper) for p in _split3(_log_sigmoid(rows)))
        b_cols = sum(_dot(lower, p) for p in _split3(_log_sigmoid(cols)))
        b_colr = sum(_dot(p, pick) for p in _split3(b_cols))
        lane = lax.broadcasted_iota(jnp.int32, (1, L), 1)
        b_last_all = jnp.sum(jnp.where(lane == L - 1, b_rows, 0.0), axis=-1, keepdims=True)
        ones_v = jnp.ones((L, ML_V_DIM), BF16)
        for hh in range(HP):
            q = q_ref[0, pl.ds(r0, L), hh * ML_QK_DIM:(hh + 1) * ML_QK_DIM]
            kt = kt_ref[0, c, hh * ML_QK_DIM:(hh + 1) * ML_QK_DIM, :]
            v = v_ref[0, pl.ds(r0, L), hh * ML_V_DIM:(hh + 1) * ML_V_DIM]
            vx = jnp.concatenate([v, ones_v], axis=1)
            i_row = rows[hh:hh + 1, :]
            b_row = b_rows[HP + hh:HP + hh + 1, :]
            b_last = b_last_all[HP + hh:HP + hh + 1, :]
            b_col = b_colr[:, hh * LANES:(hh + 1) * LANES]
            m_prev = m_ref[hh]
            ctn = c_ref[hh]

            dmat = jnp.where(tril, _twice(b_col) - b_row + i_row, NEG_INF)
            inter = b_col + m_prev
            m_t = jnp.maximum(inter, jnp.max(dmat, axis=-1, keepdims=True))
            w_intra = jnp.exp(dmat - _twice(m_t))
            w_inter = jnp.exp(inter - m_t)
            intra = (w_intra * _dot(q, kt)).astype(BF16)
            tot = _twice(w_inter) * _dot(q, ctn.astype(BF16)) + _dot(intra, vx)
            num = tot[:, :ML_V_DIM]
            den = tot[:, ML_V_DIM:]
            hc = num / jnp.maximum(jnp.abs(den), jnp.exp(-m_t))
            y = hc * lax.rsqrt(jnp.mean(hc * hc, axis=-1, keepdims=True) + NORM_EPS)
            o_ref[0, pl.ds(r0, L), hh * ML_V_DIM:(hh + 1) * ML_V_DIM] = (
                y * gain_ref[:, hh * ML_V_DIM:(hh + 1) * ML_V_DIM])

            g_row = b_last - b_row + i_row
            m_new = jnp.maximum(b_last + m_prev, jnp.max(g_row, axis=-1, keepdims=True))
            decay = jnp.exp(b_last + m_prev - m_new)
            kw = (kt.astype(F32) * jnp.exp(g_row - _twice(m_new))).astype(BF16)
            c_ref[hh] = _twice(decay) * ctn + _dot(kw, vx)
            m_ref[hh] = m_new
        return carry

    lax.fori_loop(0, S // L, chunk, 0)


def _mlstm(q, kt, v, g_rows, g_cols, out_gain):
    B, S, _ = q.shape
    assert ML_CHUNK == 2 * LANES
    nc = S // ML_CHUNK
    hp = ML_STEP_HEADS
    return pl.pallas_call(
        _mlstm_kernel,
        out_shape=jax.ShapeDtypeStruct((B, S, ML_HEADS * ML_V_DIM), F32),
        grid=(B, ML_HEADS // hp),
        in_specs=[
            pl.BlockSpec((1, S, hp * ML_QK_DIM), lambda b, p: (b, 0, p)),
            pl.BlockSpec((1, nc, hp * ML_QK_DIM, ML_CHUNK), lambda b, p: (b, 0, p, 0)),
            pl.BlockSpec((1, S, hp * ML_V_DIM), lambda b, p: (b, 0, p)),
            pl.BlockSpec((1, 1, nc, 2 * hp, ML_CHUNK), lambda b, p: (b, p, 0, 0, 0)),
            pl.BlockSpec((1, 1, nc, ML_CHUNK, LANES), lambda b, p: (b, p, 0, 0, 0)),
            pl.BlockSpec((1, hp * ML_V_DIM), lambda b, p: (0, p)),
        ],
        out_specs=pl.BlockSpec((1, S, hp * ML_V_DIM), lambda b, p: (b, 0, p)),
        scratch_shapes=[
            pltpu.VMEM((hp, ML_QK_DIM, ML_V_DIM + LANES), F32),
            pltpu.VMEM((hp, 1, LANES), F32),
        ],
        compiler_params=_cparams("parallel", "parallel"),
        name="mlstm_chunkwise",
    )(q, kt, v, g_rows, g_cols, out_gain)


def _ml_out_kernel(hh_ref, og_ref, w_ref, x_ref, g_ref, o_ref):
    a = jax.nn.sigmoid(og_ref[0]) * hh_ref[0]
    o_ref[0] = x_ref[0] + g_ref[0] * _dot(a.astype(BF16), w_ref[...])


def _ml_out(hh, og, w_out_bf, x, g1):
    B, S, D = x.shape
    tm = TOKEN_TILE
    row = lambda b, j: (b, j, 0)
    per_b = lambda b, j: (b, 0, 0)
    const2 = lambda b, j: (0, 0)
    return pl.pallas_call(
        _ml_out_kernel,
        out_shape=jax.ShapeDtypeStruct((B, S, D), F32),
        grid=(B, S // tm),
        in_specs=[
            pl.BlockSpec((1, tm, D), row),
            pl.BlockSpec((1, tm, D), row),
            pl.BlockSpec((D, D), const2),
            pl.BlockSpec((1, tm, D), row),
            pl.BlockSpec((1, 1, D), per_b),
        ],
        out_specs=pl.BlockSpec((1, tm, D), row),
        compiler_params=_cparams("parallel", "parallel"),
        name="mlstm_out_proj",
    )(hh, og, w_out_bf, x, g1)


def _first_argmax(x, lane, width):
    mx = jnp.max(x, axis=-1, keepdims=True)
    idx = jnp.min(jnp.where(x == mx, lane, width), axis=-1, keepdims=True)
    return mx, idx


def _router_kernel(x_ref, gain_ref, sc_ref, sh_ref, whi_ref, wlo_ref, b_ref, h_ref, r_ref, cnt_ref, run_ref):
    tm = x_ref.shape[1]

    @pl.when(jnp.logical_and(pl.program_id(0) == 0, pl.program_id(1) == 0))
    def _():
        run_ref[...] = jnp.zeros_like(run_ref)

    h = _modulated_norm(x_ref[0], gain_ref[...], sc_ref[0], sh_ref[0])
    h_hi = h.astype(BF16)
    h_ref[0] = _pack_bf16_pairs(h)
    h_lo = (h - h_hi.astype(F32)).astype(BF16)
    whi = whi_ref[...]
    logits = _dot(h_hi, whi) + _dot(h_lo, whi) + _dot(h_hi, wlo_ref[...]) + b_ref[...]
    lane = lax.broadcasted_iota(jnp.int32, (1, LANES), 1)
    lg = jnp.where(lane < N_GROUPS, logits, NEG_INF)
    g_max, g_sel = _first_argmax(lg, lane, LANES)
    pg = 1.0 / jnp.sum(jnp.exp(lg - g_max), axis=-1, keepdims=True)
    e_lane = lane - N_GROUPS
    in_grp = jnp.logical_and(e_lane >= g_sel * EXPERTS_PER_GROUP, e_lane < (g_sel + 1) * EXPERTS_PER_GROUP)
    le = jnp.where(in_grp, logits, NEG_INF)
    v1, i1 = _first_argmax(le, lane, LANES)
    le2 = jnp.where(lane == i1, NEG_INF, le)
    v2, i2 = _first_argmax(le2, lane, LANES)
    e2 = jnp.exp(v2 - v1)
    w1 = pg / (1.0 + e2)
    w2 = pg * e2 / (1.0 + e2)
    e1 = i1 - N_GROUPS
    e2 = i2 - N_GROUPS
    hot1 = lane == e1
    hot2 = lane == e2
    onehot = jnp.where(jnp.logical_or(hot1, hot2), 1.0, 0.0)
    r_i = lax.broadcasted_iota(jnp.int32, (tm, tm), 0)
    c_i = lax.broadcasted_iota(jnp.int32, (tm, tm), 1)
    before = jnp.where(c_i < r_i, 1.0, 0.0).astype(BF16)
    seen = _dot(before, onehot.astype(BF16)) + run_ref[...]
    rank1 = jnp.sum(jnp.where(hot1, seen, 0.0), axis=-1, keepdims=True)
    rank2 = jnp.sum(jnp.where(hot2, seen, 0.0), axis=-1, keepdims=True)
    run_ref[...] = run_ref[...] + jnp.sum(onehot, axis=0, keepdims=True)
    cnt_ref[...] = run_ref[...]

    out = jnp.where(lane == 0, e1.astype(F32), 0.0)
    out = jnp.where(lane == 1, e2.astype(F32), out)
    out = jnp.where(lane == 2, w1, out)
    out = jnp.where(lane == 3, w2, out)
    out = jnp.where(lane == 4, rank1, out)
    out = jnp.where(lane == 5, rank2, out)
    r_ref[0] = out


def _router(x, gain, sc, sh, w_hi, w_lo, bias):
    B, S, D = x.shape
    tm = TOKEN_TILE
    row = lambda b, j: (b, j, 0)
    per_b = lambda b, j: (b, 0, 0)
    const2 = lambda b, j: (0, 0)
    return pl.pallas_call(
        _router_kernel,
        out_shape=(
            jax.ShapeDtypeStruct((B, S, D // 2), jnp.int32),
            jax.ShapeDtypeStruct((B, S, LANES), F32),
            jax.ShapeDtypeStruct((1, LANES), F32),
        ),
        grid=(B, S // tm),
        in_specs=[
            pl.BlockSpec((1, tm, D), row),
            pl.BlockSpec((1, D), const2),
            pl.BlockSpec((1, 1, D), per_b),
            pl.BlockSpec((1, 1, D), per_b),
            pl.BlockSpec((D, LANES), const2),
            pl.BlockSpec((D, LANES), const2),
            pl.BlockSpec((1, LANES), const2),
        ],
        out_specs=(
            pl.BlockSpec((1, tm, D // 2), row),
            pl.BlockSpec((1, tm, LANES), row),
            pl.BlockSpec((1, LANES), const2),
        ),
        scratch_shapes=[pltpu.VMEM((1, LANES), F32)],
        compiler_params=_cparams("arbitrary", "arbitrary"),
        name="moe_router",
    )(x, gain, sc, sh, w_hi, w_lo, bias)


def _experts_kernel(blk_e_ref, n_used_ref, x_ref, wg_ref, wu_ref, wd_ref, o_ref, wg_s, wu_s, wd_s):
    i = pl.program_id(0)
    used = i < n_used_ref[0]
    new_expert = jnp.logical_or(i == 0, blk_e_ref[i] != blk_e_ref[jnp.maximum(i - 1, 0)])

    @pl.when(jnp.logical_and(used, new_expert))
    def _():
        wg_s[...] = wg_ref[0, 0].astype(BF16)
        wu_s[...] = wu_ref[0, 0].astype(BF16)
        wd_s[...] = wd_ref[0, 0].astype(BF16)

    @pl.when(used)
    def _():
        x = _unpack_bf16_pairs(x_ref[...]).astype(BF16)
        a = _dot(x, wg_s[...])
        u = _dot(x, wu_s[...])
        act = a * jax.nn.sigmoid(a) * u
        o_ref[...] = _pack_bf16_pairs(_dot(act.astype(BF16), wd_s[...]))

    @pl.when(i >= n_used_ref[0])
    def _():
        o_ref[...] = jnp.zeros_like(o_ref)


def _experts(layer, blk_e, n_used, xs, w_gate, w_up, w_down):
    R = xs.shape[0]
    D = 2 * xs.shape[1]
    n_blk = R // MOE_BLOCK
    grid_spec = pltpu.PrefetchScalarGridSpec(
        num_scalar_prefetch=2,
        grid=(n_blk,),
        in_specs=[
            pl.BlockSpec((MOE_BLOCK, D // 2), lambda i, be, nu: (i, 0)),
            pl.BlockSpec((1, 1, D, D_EXPERT), lambda i, be, nu: (layer, be[i], 0, 0)),
            pl.BlockSpec((1, 1, D, D_EXPERT), lambda i, be, nu: (layer, be[i], 0, 0)),
            pl.BlockSpec((1, 1, D_EXPERT, D), lambda i, be, nu: (layer, be[i], 0, 0)),
        ],
        out_specs=pl.BlockSpec((MOE_BLOCK, D // 2), lambda i, be, nu: (i, 0)),
        scratch_shapes=[
            pltpu.VMEM((D, D_EXPERT), BF16),
            pltpu.VMEM((D, D_EXPERT), BF16),
            pltpu.VMEM((D_EXPERT, D), BF16),
        ],
    )
    return pl.pallas_call(
        _experts_kernel,
        out_shape=jax.ShapeDtypeStruct((R, D // 2), jnp.int32),
        grid_spec=grid_spec,
        compiler_params=_cparams("arbitrary"),
        name="moe_experts",
    )(blk_e, n_used, xs, w_gate, w_up, w_down)


def _combine_kernel(x_ref, g_ref, y0_ref, y1_ref, r_ref, o_ref):
    r = r_ref[0]
    y = _unpack_bf16_pairs(y0_ref[0, 0]) * r[:, 2:3] + _unpack_bf16_pairs(y1_ref[0, 0]) * r[:, 3:4]
    o_ref[0] = x_ref[0] + g_ref[0] * y


def _combine(x, g2, y01, route):
    B, S, D = x.shape
    tm = TOKEN_TILE
    row = lambda b, j: (b, j, 0)
    per_b = lambda b, j: (b, 0, 0)
    return pl.pallas_call(
        _combine_kernel,
        out_shape=jax.ShapeDtypeStruct((B, S, D), F32),
        grid=(B, S // tm),
        in_specs=[
            pl.BlockSpec((1, tm, D), row),
            pl.BlockSpec((1, 1, D), per_b),
            pl.BlockSpec((1, 1, tm, D // 2), lambda b, j: (0, b, j, 0)),
            pl.BlockSpec((1, 1, tm, D // 2), lambda b, j: (1, b, j, 0)),
            pl.BlockSpec((1, tm, LANES), row),
        ],
        out_specs=pl.BlockSpec((1, tm, D), row),
        compiler_params=_cparams("parallel", "parallel"),
        name="moe_combine",
    )(x, g2, y01, y01, route)


SC_CORES = 2
SC_SUBCORES = 16
SC_WORKERS = SC_CORES * SC_SUBCORES
SC_CHUNK = 64


def _sc_mesh():
    return plsc.VectorSubcoreMesh(core_axis_name="c", subcore_axis_name="s",
                                  num_cores=SC_CORES, num_subcores=SC_SUBCORES)


def _sc_scatter_rows(src, idx, n_out):
    T, W = src.shape
    per_w = T // SC_WORKERS
    nch = per_w // SC_CHUNK
    idx4 = idx.reshape(TOP_K, SC_WORKERS, nch, SC_CHUNK)

    @functools.partial(
        pl.kernel, mesh=_sc_mesh(),
        out_type=jax.ShapeDtypeStruct((n_out, W), src.dtype),
        scratch_types=[pltpu.VMEM((TOP_K, nch, SC_CHUNK), jnp.int32), pltpu.VMEM((SC_CHUNK, W), src.dtype)],
        name="sc_scatter_rows",
    )
    def body(src_hbm, idx_hbm, out_hbm, idx_v, rows_v):
        wid = lax.axis_index("s") * SC_CORES + lax.axis_index("c")
        for s in range(TOP_K):
            pltpu.sync_copy(idx_hbm.at[s, wid], idx_v.at[s])

        @pl.loop(0, nch)
        def _(i):
            pltpu.sync_copy(src_hbm.at[pl.ds(wid * per_w + i * SC_CHUNK, SC_CHUNK)], rows_v)
            for s in range(TOP_K):
                pltpu.sync_copy(rows_v, out_hbm.at[idx_v.at[s, i]])

    return body(src, idx4)


def _sc_gather_rows(table, idx):
    N = idx.shape[0]
    W = table.shape[1]
    per_w = N // SC_WORKERS
    nch = per_w // SC_CHUNK
    idx3 = idx.reshape(SC_WORKERS, nch, SC_CHUNK)

    @functools.partial(
        pl.kernel, mesh=_sc_mesh(),
        out_type=jax.ShapeDtypeStruct((N, W), table.dtype),
        scratch_types=[
            pltpu.VMEM((nch, SC_CHUNK), jnp.int32),
            pltpu.VMEM((2, SC_CHUNK, W), table.dtype),
            pltpu.SemaphoreType.DMA((2,)),
            pltpu.SemaphoreType.DMA((2,)),
        ],
        name="sc_gather_rows",
    )
    def body(table_hbm, idx_hbm, out_hbm, idx_v, rows_v, gather_sem, write_sem):
        wid = lax.axis_index("s") * SC_CORES + lax.axis_index("c")
        pltpu.sync_copy(idx_hbm.at[wid], idx_v)

        def gather(j, b):
            return pltpu.make_async_copy(table_hbm.at[idx_v.at[j]], rows_v.at[b], gather_sem.at[b])

        def write(j, b):
            return pltpu.make_async_copy(
                rows_v.at[b], out_hbm.at[pl.ds(wid * per_w + j * SC_CHUNK, SC_CHUNK)], write_sem.at[b])

        gather(0, 0).start()

        @pl.loop(0, nch, step=2)
        def _(i):
            for b in range(2):
                j = i + b

                @pl.when(j >= 1)
                def _():
                    write(j - 1, 1 - b).wait()

                @pl.when(j + 1 < nch)
                def _():
                    gather(j + 1, 1 - b).start()

                gather(j, b).wait()
                write(j, b).start()

        write(nch - 1, (nch - 1) % 2).wait()

    assert nch % 2 == 0
    return body(table, idx3)


def _moe_dispatch(route, counts, T):
    A = T * TOP_K
    counts = counts[0, :N_EXPERTS].astype(jnp.int32)
    blocks_per = (counts + MOE_BLOCK - 1) // MOE_BLOCK
    block_end = jnp.cumsum(blocks_per)
    block_start = block_end - blocks_per
    expert = route[:, :TOP_K].astype(jnp.int32)
    rank = route[:, 4:4 + TOP_K].astype(jnp.int32)
    onehot = expert[:, :, None] == jnp.arange(N_EXPERTS, dtype=jnp.int32)
    start = jnp.sum(jnp.where(onehot, block_start, 0), axis=-1)
    dest = (start * MOE_BLOCK + rank).T
    n_blk = -(-A // MOE_BLOCK) + N_EXPERTS
    blk = jnp.arange(n_blk, dtype=jnp.int32)
    blk_e = jnp.minimum(jnp.sum(blk[:, None] >= block_end[None, :], axis=-1), N_EXPERTS - 1)
    return dest, n_blk * MOE_BLOCK, blk_e.astype(jnp.int32), block_end[-1:].astype(jnp.int32)


def _rope_tables(S):
    inv = 1.0 / (ROPE_THETA ** (jnp.arange(0, HEAD_DIM, 2, dtype=F32) / HEAD_DIM))
    ang = jnp.arange(S, dtype=F32)[:, None] * inv[None, :]
    cos, sin = jnp.cos(ang), jnp.sin(ang)
    cos_h = jnp.concatenate([cos, cos], axis=-1)
    sin_h = jnp.concatenate([-sin, sin], axis=-1)
    return jnp.tile(cos_h, (1, ATTN_HEADS)), jnp.tile(sin_h, (1, ATTN_HEADS))


def _pad_cols(w, width):
    return jnp.pad(w, ((0, 0), (0, width - w.shape[1])))


def kernel(x, c, ada_w, ada_b, norm_mix, norm_ffn, hy_w_in, hy_q_norm, hy_k_norm, hy_conv_w, hy_w_out, ml_w_in, ml_b_gates, ml_out_norm, ml_w_out, moe_w_group, moe_b_group, moe_w_expert, moe_b_expert, moe_w_gate, moe_w_up, moe_w_down):
    B, S, D = x.shape
    T = B * S
    cos_t, sin_t = _rope_tables(S)
    mod = _ada_modulation(c, ada_w, ada_b).reshape(DEPTH, B, 6, 1, D)
    r_i = np.arange(ATTN_WIDTH)
    grp = jnp.asarray((r_i[:, None] // HEAD_DIM) == (r_i[None, :] // HEAD_DIM), dtype=BF16)

    for l in range(DEPTH):
        sh1, sc1, g1, sh2, sc2, g2 = [mod[l, :, i] for i in range(6)]
        gain1 = norm_mix[l].reshape(1, D)
        j = l // 2
        if l % 2 == 0:
            w = hy_w_in[j]
            o = np.cumsum((0,) + (ATTN_WIDTH, HEAD_DIM, HEAD_DIM, IDX_HEADS * IDX_DIM, IDX_DIM, IDX_HEADS,
                                  CONV_WIDTH, CONV_WIDTH, CONV_WIDTH))
            wq, wk, wv, wiq, wik, wiw, wbg, wcg, wu = [w[:, o[i]:o[i + 1]] for i in range(9)]
            w_pad = jnp.concatenate(
                [wq, wiq, wbg, wcg, wu, wk, wv, _pad_cols(jnp.concatenate([wik, wiw], axis=1), LANES)],
                axis=1).astype(BF16)
            qn_t = jnp.tile(hy_q_norm[j], ATTN_HEADS).reshape(1, ATTN_WIDTH)
            kn_t = jnp.tile(hy_k_norm[j], LANES // HEAD_DIM).reshape(1, LANES)
            qt, iqt, bcu, kv, kvt, ik, iwt = _hyb_in(x, gain1, sc1, sh1, w_pad, cos_t, sin_t, qn_t, kn_t, grp)
            y_attn = _dsa_attention_t(qt, iqt, iwt, kv, kvt, ik)
            x = _hyb_out(y_attn, bcu, hy_conv_w[j], hy_w_out[j].astype(BF16), x, g1)
        else:
            w = ml_w_in[j]
            hq = ML_HEADS * ML_QK_DIM
            hv = ML_HEADS * ML_V_DIM
            wq, wk, wv = w[:, :hq], w[:, hq:2 * hq], w[:, 2 * hq:2 * hq + hv]
            wg = w[:, 2 * hq + hv:2 * hq + hv + 2 * ML_HEADS]
            wo = w[:, 2 * hq + hv + 2 * ML_HEADS:]
            w_pad = jnp.concatenate([wq, wk, wv, wo, _pad_cols(wg, LANES)], axis=1).astype(BF16)
            gate_bias = jnp.pad(ml_b_gates[j], (0, LANES - 2 * ML_HEADS)).reshape(1, LANES)
            q, k, v, og, gates = _ml_in(x, gain1, sc1, sh1, w_pad, gate_bias)
            nc = S // ML_CHUNK
            hp = ML_STEP_HEADS
            groups = ML_HEADS // hp
            gi = gates[:, :, :ML_HEADS].reshape(B, S, groups, hp)
            gf = gates[:, :, ML_HEADS:2 * ML_HEADS].reshape(B, S, groups, hp)
            gp = jnp.concatenate([gi, gf], axis=-1)
            gp = jnp.transpose(gp, (0, 2, 1, 3)).reshape(B, groups, nc, ML_CHUNK, 2 * hp)
            g_cols = jnp.pad(gp, ((0, 0),) * 4 + ((0, LANES - 2 * hp),))
            g_rows = jnp.swapaxes(gp, -1, -2)
            hh = _mlstm(q, k, v, g_rows, g_cols, ml_out_norm[j].reshape(1, hv))
            x = _ml_out(hh, og, ml_w_out[j].astype(BF16), x, g1)

        w_r = _pad_cols(jnp.concatenate([moe_w_group[l], moe_w_expert[l]], axis=1), LANES)
        w_hi = w_r.astype(BF16)
        w_lo = (w_r - w_hi.astype(F32)).astype(BF16)
        b_r = jnp.pad(jnp.concatenate([moe_b_group[l], moe_b_expert[l]]), (0, LANES - N_GROUPS - N_EXPERTS))
        h2, route, counts = _router(x, norm_ffn[l].reshape(1, D), sc2, sh2, w_hi, w_lo, b_r.reshape(1, LANES))
        dest, n_rows, blk_e, n_used = _moe_dispatch(route.reshape(T, LANES), counts, T)
        xs = _sc_scatter_rows(h2.reshape(T, D // 2), dest, n_rows)
        ys = _experts(l, blk_e, n_used, xs, moe_w_gate, moe_w_up, moe_w_down)
        y01 = _sc_gather_rows(ys, dest.reshape(TOP_K * T)).reshape(TOP_K, B, S, D // 2)
        x = _combine(x, g2, y01, route)
    return x
```

```python
import functools

import numpy as np
import jax
import jax.numpy as jnp
from jax import lax
from jax.experimental import pallas as pl
from jax.experimental.pallas import tpu as pltpu
from jax.experimental.pallas import tpu_sc as plsc

F32 = jnp.float32
BF16 = jnp.bfloat16
HIGHEST = lax.Precision.HIGHEST

D_MODEL = 1024
DEPTH = 4
ATTN_HEADS = 8
HEAD_DIM = 64
ATTN_WIDTH = ATTN_HEADS * HEAD_DIM
IDX_HEADS = 8
IDX_DIM = 64
INDEX_TOPK = 256
Q_BLOCK = 256
ROPE_THETA = 10000.0
CONV_WIDTH = D_MODEL - ATTN_WIDTH
CONV_K = 3
ML_HEADS = 8
ML_QK_DIM = 64
ML_V_DIM = 128
N_GROUPS = 4
EXPERTS_PER_GROUP = 8
N_EXPERTS = N_GROUPS * EXPERTS_PER_GROUP
TOP_K = 2
D_EXPERT = 512
MOE_BLOCK = 256
NORM_EPS = 1e-6

LANES = 128
VMEM_LIMIT = 56 * 1024 * 1024
TOKEN_TILE = 256
ML_CHUNK = 256
ML_STEP_HEADS = 8
NEG_INF = float("-inf")


def _cparams(*sem):
    return pltpu.CompilerParams(dimension_semantics=sem, vmem_limit_bytes=VMEM_LIMIT)


def _dot(a, b):
    return jnp.dot(a, b, preferred_element_type=F32)


def _pack_bf16_pairs(x):
    bits = lax.bitcast_convert_type(x.astype(BF16).astype(F32), jnp.uint32)
    half = bits.shape[1] // 2
    packed = (bits[:, :half] >> 16) | (bits[:, half:] & jnp.uint32(0xFFFF0000))
    return lax.bitcast_convert_type(packed, jnp.int32)


def _unpack_bf16_pairs(words):
    words = lax.bitcast_convert_type(words, jnp.uint32)
    return jnp.concatenate(
        [lax.bitcast_convert_type(words << 16, F32),
         lax.bitcast_convert_type(words & jnp.uint32(0xFFFF0000), F32)], axis=1)


def _split_dot(a_f32, b_bf16):
    hi = a_f32.astype(BF16)
    lo = (a_f32 - hi.astype(F32)).astype(BF16)
    return _dot(hi, b_bf16) + _dot(lo, b_bf16)


def _ada_kernel(c_ref, w_ref, b_ref, o_ref):
    c = c_ref[...]
    ca = c * jax.nn.sigmoid(c)
    o_ref[0] = jnp.dot(ca, w_ref[0], precision=HIGHEST, preferred_element_type=F32) + b_ref[0]


def _ada_modulation(c, ada_w, ada_b):
    B, D = c.shape
    n_col = ada_w.shape[-1] // D
    return pl.pallas_call(
        _ada_kernel,
        out_shape=jax.ShapeDtypeStruct((DEPTH, B, n_col * D), F32),
        grid=(DEPTH, n_col),
        in_specs=[
            pl.BlockSpec((B, D), lambda l, j: (0, 0)),
            pl.BlockSpec((1, D, D), lambda l, j: (l, 0, j)),
            pl.BlockSpec((1, 1, D), lambda l, j: (l, 0, j)),
        ],
        out_specs=pl.BlockSpec((1, B, D), lambda l, j: (l, 0, j)),
        compiler_params=_cparams("parallel", "parallel"),
        name="ada_modulation",
    )(c, ada_w, ada_b.reshape(DEPTH, 1, n_col * D))


def _modulated_norm(x, gain, scale, shift):
    y = x * lax.rsqrt(jnp.mean(x * x, axis=-1, keepdims=True) + NORM_EPS)
    return y * gain * (1.0 + scale) + shift


def _rope(x, cos, sin_signed, first_half):
    w = x.shape[-1]
    partner = jnp.where(first_half, pltpu.roll(x, w - HEAD_DIM // 2, 1), pltpu.roll(x, HEAD_DIM // 2, 1))
    return x * cos + partner * sin_signed


HYB_COLS = 5 * 512 + 2 * LANES


def _hyb_in_kernel(x_ref, gain_ref, sc_ref, sh_ref, w_ref, cos_ref, sin_ref, qn_ref, kn_ref, grp_ref,
                   qt_ref, iqt_ref, bcu_ref, kv_ref, kvt_ref, ik_ref, iwt_ref):
    h = _modulated_norm(x_ref[0], gain_ref[...], sc_ref[0], sh_ref[0])
    p = _dot(h.astype(BF16), w_ref[...])
    cos = cos_ref[...]
    sin = sin_ref[...]
    lane = lax.broadcasted_iota(jnp.int32, (1, ATTN_WIDTH), 1)
    first_half = (lane % HEAD_DIM) < (HEAD_DIM // 2)
    fh128 = first_half[:, :LANES]
    lane128 = lane[:, :LANES]

    q = p[:, 0:512]
    ms = _split_dot(q * q, grp_ref[...]) * (1.0 / HEAD_DIM)
    q = q * lax.rsqrt(ms + NORM_EPS) * qn_ref[...]
    qt_ref[0] = (_rope(q, cos, sin, first_half) * (HEAD_DIM ** -0.5)).T.astype(BF16)

    iq = p[:, 512:1024]
    iqt_ref[0] = (_rope(iq, cos, sin, first_half) * (IDX_DIM ** -0.5)).T.astype(BF16)

    bcu_ref[0] = p[:, 1024:2560]

    kv = p[:, 2560:2688]
    is_k = lane128 < HEAD_DIM
    kk = jnp.where(is_k, kv, 0.0)
    ms_k = jnp.sum(kk * kk, axis=-1, keepdims=True) * (1.0 / HEAD_DIM)
    kn = kv * lax.rsqrt(ms_k + NORM_EPS) * kn_ref[...]
    kr = _rope(kn, cos[:, :LANES], sin[:, :LANES], fh128)
    kv = jnp.where(is_k, kr, kv)
    kv_ref[0] = kv.astype(BF16)
    kvt_ref[0, 0] = kv.T.astype(BF16)

    sm = p[:, 2688:2816]
    ikr = _rope(sm, cos[:, :LANES], sin[:, :LANES], fh128)
    ik_ref[0] = jnp.where(is_k, ikr, 0.0).astype(BF16)
    iwt_ref[0] = sm.T


def _hyb_in(x, gain, sc, sh, w_pad, cos_t, sin_t, qn_t, kn_t, grp):
    B, S, D = x.shape
    tm = TOKEN_TILE
    row = lambda b, j: (b, j, 0)
    per_b = lambda b, j: (b, 0, 0)
    const2 = lambda b, j: (0, 0)
    tab = lambda b, j: (j, 0)
    col = lambda b, j: (b, 0, j)
    assert tm == DSA_KEY_CHUNK
    return pl.pallas_call(
        _hyb_in_kernel,
        out_shape=(
            jax.ShapeDtypeStruct((B, 512, S), BF16),
            jax.ShapeDtypeStruct((B, 512, S), BF16),
            jax.ShapeDtypeStruct((B, S, 1536), F32),
            jax.ShapeDtypeStruct((B, S, LANES), BF16),
            jax.ShapeDtypeStruct((B, S // tm, LANES, tm), BF16),
            jax.ShapeDtypeStruct((B, S, LANES), BF16),
            jax.ShapeDtypeStruct((B, LANES, S), F32),
        ),
        grid=(B, S // tm),
        in_specs=[
            pl.BlockSpec((1, tm, D), row),
            pl.BlockSpec((1, D), const2),
            pl.BlockSpec((1, 1, D), per_b),
            pl.BlockSpec((1, 1, D), per_b),
            pl.BlockSpec((D, HYB_COLS), const2),
            pl.BlockSpec((tm, 512), tab),
            pl.BlockSpec((tm, 512), tab),
            pl.BlockSpec((1, 512), const2),
            pl.BlockSpec((1, LANES), const2),
            pl.BlockSpec((512, 512), const2),
        ],
        out_specs=(
            pl.BlockSpec((1, 512, tm), col),
            pl.BlockSpec((1, 512, tm), col),
            pl.BlockSpec((1, tm, 1536), row),
            pl.BlockSpec((1, tm, LANES), row),
            pl.BlockSpec((1, 1, LANES, tm), lambda b, j: (b, j, 0, 0)),
            pl.BlockSpec((1, tm, LANES), row),
            pl.BlockSpec((1, LANES, tm), col),
        ),
        compiler_params=_cparams("parallel", "parallel"),
        name="hybrid_in_proj",
    )(x, gain, sc, sh, w_pad, cos_t, sin_t, qn_t, kn_t, grp)


DSA_KEY_CHUNK = 256
F32_LOWEST = float(np.finfo(np.float32).min)


def _fold8(x, op):
    parts = x.reshape(x.shape[0] // 8, 8, x.shape[1])
    while parts.shape[0] > 1:
        half = parts.shape[0] // 2
        assert parts.shape[0] == 2 * half
        parts = op(parts[:half], parts[half:])
    return parts[0]


def _col_reduce(x, op):
    t = _fold8(x, op)
    for shift in (4, 2, 1):
        t = op(t, pltpu.roll(t, shift, 0))
    return t[0:1, :]


def _dsa_t_kernel(qt_ref, iqt_ref, iwt_ref, kv_ref, kvt_ref, ik_ref, o_ref, sc_ref, bias_ref, acc_ref):
    CK = DSA_KEY_CHUNK
    QB = Q_BLOCK
    qb = pl.program_id(1)
    nk = (qb * QB + QB + CK - 1) // CK
    kf = float(INDEX_TOPK)
    qpos = qb * QB + lax.broadcasted_iota(jnp.int32, (1, QB), 1)
    krow = lax.broadcasted_iota(jnp.int32, (CK, 1), 0)
    w_idx = iwt_ref[0, IDX_DIM:IDX_DIM + IDX_HEADS, :] * (IDX_HEADS ** -0.5)

    def rows(c):
        return pl.ds(pl.multiple_of(c * CK, CK), CK)

    def heads_on_lanes(ref, width):
        return jnp.concatenate([ref[0, hd * width:(hd + 1) * width, :] for hd in range(ref.shape[1] // width)], axis=1)

    def head_lanes(hd):
        return slice(hd * QB, (hd + 1) * QB)

    iq_wide = heads_on_lanes(iqt_ref, IDX_DIM)
    w_wide = jnp.concatenate([w_idx[hd:hd + 1, :] for hd in range(IDX_HEADS)], axis=1)

    def score_chunk(c, carry):
        mx, mn = carry
        ikc = ik_ref[0, rows(c), :][:, :IDX_DIM]
        s_all = jnp.maximum(_dot(ikc, iq_wide), 0.0) * w_wide
        acc = s_all[:, head_lanes(0)]
        for hd in range(1, IDX_HEADS):
            acc = acc + s_all[:, head_lanes(hd)]
        causal = (c * CK + krow) <= qpos
        sc_ref[rows(c), :] = jnp.where(causal, acc, NEG_INF)
        mx = jnp.maximum(mx, _fold8(jnp.where(causal, acc, NEG_INF), jnp.maximum))
        mn = jnp.minimum(mn, _fold8(jnp.where(causal, acc, jnp.inf), jnp.minimum))
        return mx, mn

    mx8, mn8 = lax.fori_loop(0, nk, score_chunk,
                             (jnp.full((8, QB), NEG_INF, F32), jnp.full((8, QB), jnp.inf, F32)))
    row_max = jnp.max(mx8, axis=0, keepdims=True)
    row_min = jnp.min(mn8, axis=0, keepdims=True)

    @pl.when(nk % 2 == 1)
    def _():
        sc_ref[rows(nk), :] = jnp.full((CK, QB), NEG_INF, F32)

    n_pairs = (nk + 1) // 2

    def pair_rows(c):
        return pl.ds(pl.multiple_of(c * (2 * CK), 2 * CK), 2 * CK)

    def count(pred):
        def body(c, part):
            return part + _fold8(jnp.where(pred(sc_ref[pair_rows(c), :]), 1.0, 0.0), jnp.add)
        part = lax.fori_loop(0, n_pairs, body, jnp.zeros((8, QB), F32))
        return jnp.sum(part, axis=0, keepdims=True)

    @pl.when(qb * QB + QB <= INDEX_TOPK)
    def _():
        def body(c, carry):
            bias_ref[rows(c), :] = jnp.where(sc_ref[rows(c), :] > NEG_INF, 0.0, NEG_INF)
            return carry
        lax.fori_loop(0, nk, body, 0)

    @pl.when(qb * QB + QB > INDEX_TOPK)
    def _():
        top_tied = count(lambda x: x >= row_max) >= kf

        def bisect(_, carry):
            lo, hi = carry
            mid = 0.5 * lo + 0.5 * hi
            ge = count(lambda x: x >= mid) >= kf
            return jnp.where(ge, mid, lo), jnp.where(ge, hi, mid)

        lo, hi = lax.fori_loop(0, 18, bisect, (row_min, row_max))

        def refine_cond(carry):
            it, _, _, done = carry
            return jnp.logical_and(it < nk * CK, jnp.min(done) < 0.5)

        def refine(carry):
            it, hi, thr, done = carry

            def below(c, part):
                x = sc_ref[pair_rows(c), :]
                return jnp.maximum(part, _fold8(jnp.where(x < hi, x, NEG_INF), jnp.maximum))

            m = jnp.max(lax.fori_loop(0, n_pairs, below, jnp.full((8, QB), NEG_INF, F32)), axis=0, keepdims=True)
            hit = count(lambda x: x >= m) >= kf
            fin = done > 0.5
            thr = jnp.where(fin, thr, m)
            hi = jnp.where(jnp.logical_or(fin, hit), hi, m)
            done = jnp.where(hit, 1.0, done)
            return it + 1, hi, thr, done

        done0 = jnp.where(top_tied, 1.0, 0.0)
        _, _, thr, _ = lax.while_loop(refine_cond, refine, (jnp.int32(0), hi, row_max, done0))

        need = kf - count(lambda x: x > thr)
        n_eq = count(lambda x: x == thr)
        tied = jnp.max(n_eq - need) > 0.5

        @pl.when(jnp.logical_not(tied))
        def _():
            def body(c, carry):
                bias_ref[rows(c), :] = jnp.where(sc_ref[rows(c), :] >= thr, 0.0, NEG_INF)
                return carry
            lax.fori_loop(0, nk, body, 0)

        @pl.when(tied)
        def _():
            r_i = lax.broadcasted_iota(jnp.int32, (CK, CK), 0)
            c_i = lax.broadcasted_iota(jnp.int32, (CK, CK), 1)
            lower = jnp.where(c_i <= r_i, 1.0, 0.0).astype(BF16)

            def body(c, seen):
                x = sc_ref[rows(c), :]
                eq = x == thr
                eq_f = jnp.where(eq, 1.0, 0.0)
                rank = _dot(lower, eq_f.astype(BF16)) + seen
                keep = jnp.logical_or(x > thr, jnp.logical_and(eq, rank <= need))
                bias_ref[rows(c), :] = jnp.where(keep, 0.0, NEG_INF)
                return seen + jnp.sum(eq_f, axis=0, keepdims=True)

            lax.fori_loop(0, nk, body, jnp.zeros((1, QB), F32))

    acc_ref[...] = jnp.zeros_like(acc_ref)
    q_wide = heads_on_lanes(qt_ref, HEAD_DIM)

    def attend_chunk(c, carry):
        m_old, l_old = carry
        kc = kv_ref[0, rows(c), :][:, :HEAD_DIM]
        vt = kvt_ref[0, c, HEAD_DIM:, :]
        bias = bias_ref[rows(c), :]
        logits = _dot(kc, q_wide) + jnp.concatenate([bias] * ATTN_HEADS, axis=1)
        m_new = jnp.maximum(m_old, _col_reduce(logits, jnp.maximum))
        alpha = jnp.exp(m_old - m_new)
        p = jnp.exp(logits - m_new)
        l_new = alpha * l_old + _col_reduce(p, jnp.add)
        acc_ref[...] = alpha * acc_ref[...] + _dot(vt, p.astype(BF16))
        return m_new, l_new

    _, l_all = lax.fori_loop(0, nk, attend_chunk, (jnp.full((1, ATTN_HEADS * QB), -1e30, F32),
                                                   jnp.zeros((1, ATTN_HEADS * QB), F32)))
    out_t = acc_ref[...] / l_all
    o_ref[0] = jnp.concatenate([out_t[:, head_lanes(hd)] for hd in range(ATTN_HEADS)], axis=0).T


def _dsa_attention_t(qt, iqt, iwt, kv, kvt, ik):
    B, _, S = qt.shape
    col = lambda b, j: (b, 0, j)
    per_b = lambda b, j: (b, 0, 0)
    return pl.pallas_call(
        _dsa_t_kernel,
        out_shape=jax.ShapeDtypeStruct((B, S, ATTN_WIDTH), F32),
        grid=(B, S // Q_BLOCK),
        in_specs=[
            pl.BlockSpec((1, ATTN_WIDTH, Q_BLOCK), col),
            pl.BlockSpec((1, IDX_HEADS * IDX_DIM, Q_BLOCK), col),
            pl.BlockSpec((1, LANES, Q_BLOCK), col),
            pl.BlockSpec((1, S, LANES), per_b),
            pl.BlockSpec((1, S // DSA_KEY_CHUNK, LANES, DSA_KEY_CHUNK), lambda b, j: (b, 0, 0, 0)),
            pl.BlockSpec((1, S, LANES), per_b),
        ],
        out_specs=pl.BlockSpec((1, Q_BLOCK, ATTN_WIDTH), lambda b, j: (b, j, 0)),
        scratch_shapes=[
            pltpu.VMEM((S, Q_BLOCK), F32),
            pltpu.VMEM((S, Q_BLOCK), F32),
            pltpu.VMEM((HEAD_DIM, ATTN_HEADS * Q_BLOCK), F32),
        ],
        compiler_params=_cparams("parallel", "parallel"),
        name="dsa_attention",
    )(qt, iqt, iwt, kv, kvt, ik)


def _hyb_out_kernel(ya_ref, bcu_ref, halo_ref, cw_ref, w_ref, x_ref, g_ref, o_ref):
    j = pl.program_id(1)
    tm = ya_ref.shape[1]
    bcu = bcu_ref[0]
    bg = bcu[:, 0:512]
    z = bcu[:, 512:1024] * bcu[:, 1024:1536]
    halo = halo_ref[0]
    zh = halo[:, 512:1024] * halo[:, 1024:1536]
    zh = jnp.where(j > 0, zh, 0.0)
    row = lax.broadcasted_iota(jnp.int32, (tm, 1), 0)
    z1 = jnp.where(row >= 1, pltpu.roll(z, 1, 0), zh[7:8, :])
    z2 = jnp.where(row >= 2, pltpu.roll(z, 2, 0), jnp.where(row == 1, zh[7:8, :], zh[6:7, :]))
    cw = cw_ref[...]
    y_conv = bg * (z2 * cw[0:1, :] + z1 * cw[1:2, :] + z * cw[2:3, :])
    y = _dot(ya_ref[0].astype(BF16), w_ref[0:512, :]) + _dot(y_conv.astype(BF16), w_ref[512:1024, :])
    o_ref[0] = x_ref[0] + g_ref[0] * y


def _hyb_out(y_attn, bcu, conv_w, w_out_bf, x, g1):
    B, S, D = x.shape
    tm = TOKEN_TILE
    row = lambda b, j: (b, j, 0)
    per_b = lambda b, j: (b, 0, 0)
    const2 = lambda b, j: (0, 0)
    halo = lambda b, j: (b, jnp.maximum(j * (tm // 8) - 1, 0), 0)
    return pl.pallas_call(
        _hyb_out_kernel,
        out_shape=jax.ShapeDtypeStruct((B, S, D), F32),
        grid=(B, S // tm),
        in_specs=[
            pl.BlockSpec((1, tm, 512), row),
            pl.BlockSpec((1, tm, 1536), row),
            pl.BlockSpec((1, 8, 1536), halo),
            pl.BlockSpec((CONV_K, CONV_WIDTH), const2),
            pl.BlockSpec((D, D), const2),
            pl.BlockSpec((1, tm, D), row),
            pl.BlockSpec((1, 1, D), per_b),
        ],
        out_specs=pl.BlockSpec((1, tm, D), row),
        compiler_params=_cparams("parallel", "parallel"),
        name="hybrid_out_proj",
    )(y_attn, bcu, bcu, conv_w, w_out_bf, x, g1)


ML_COLS = 512 + 512 + 1024 + 1024 + LANES


def _ml_in_kernel(x_ref, gain_ref, sc_ref, sh_ref, w_ref, bias_ref, q_ref, k_ref, v_ref, og_ref, gt_ref):
    h = _modulated_norm(x_ref[0], gain_ref[...], sc_ref[0], sh_ref[0])
    p = _dot(h.astype(BF16), w_ref[...])
    q_ref[0] = (p[:, 0:512] * (ML_QK_DIM ** -0.5)).astype(BF16)
    k_ref[0, 0] = p[:, 512:1024].T.astype(BF16)
    v_ref[0] = p[:, 1024:2048].astype(BF16)
    og_ref[0] = p[:, 2048:3072]
    gt_ref[0] = p[:, 3072:3200] + bias_ref[...]


def _ml_in(x, gain, sc, sh, w_pad, gate_bias):
    B, S, D = x.shape
    tm = TOKEN_TILE
    assert tm == ML_CHUNK
    row = lambda b, j: (b, j, 0)
    per_b = lambda b, j: (b, 0, 0)
    const2 = lambda b, j: (0, 0)
    return pl.pallas_call(
        _ml_in_kernel,
        out_shape=(
            jax.ShapeDtypeStruct((B, S, 512), BF16),
            jax.ShapeDtypeStruct((B, S // tm, 512, tm), BF16),
            jax.ShapeDtypeStruct((B, S, 1024), BF16),
            jax.ShapeDtypeStruct((B, S, 1024), F32),
            jax.ShapeDtypeStruct((B, S, LANES), F32),
        ),
        grid=(B, S // tm),
        in_specs=[
            pl.BlockSpec((1, tm, D), row),
            pl.BlockSpec((1, D), const2),
            pl.BlockSpec((1, 1, D), per_b),
            pl.BlockSpec((1, 1, D), per_b),
            pl.BlockSpec((D, ML_COLS), const2),
            pl.BlockSpec((1, LANES), const2),
        ],
        out_specs=(
            pl.BlockSpec((1, tm, 512), row),
            pl.BlockSpec((1, 1, 512, tm), lambda b, j: (b, j, 0, 0)),
            pl.BlockSpec((1, tm, 1024), row),
            pl.BlockSpec((1, tm, 1024), row),
            pl.BlockSpec((1, tm, LANES), row),
        ),
        compiler_params=_cparams("parallel", "parallel"),
        name="mlstm_in_proj",
    )(x, gain, sc, sh, w_pad, gate_bias)


def _log_sigmoid(f):
    return jnp.minimum(f, 0.0) - jnp.log1p(jnp.exp(-jnp.abs(f)))


def _split3(x):
    a = x.astype(BF16)
    r = x - a.astype(F32)
    b = r.astype(BF16)
    c = (r - b.astype(F32)).astype(BF16)
    return a, b, c


def _twice(a):
    return jnp.concatenate([a, a], axis=1)


def _mlstm_kernel(q_ref, kt_ref, v_ref, grow_ref, gcol_ref, gain_ref, o_ref, c_ref, m_ref):
    L = ML_CHUNK
    HP = ML_STEP_HEADS
    S = q_ref.shape[1]
    c_ref[...] = jnp.zeros_like(c_ref)
    m_ref[...] = jnp.zeros_like(m_ref)

    def chunk(c, carry):
        r0 = pl.multiple_of(c * L, L)
        r_i = lax.broadcasted_iota(jnp.int32, (L, L), 0)
        c_i = lax.broadcasted_iota(jnp.int32, (L, L), 1)
        tril = c_i <= r_i
        lower = jnp.where(tril, 1.0, 0.0).astype(BF16)
        upper = jnp.where(r_i <= c_i, 1.0, 0.0).astype(BF16)
        e_r = lax.broadcasted_iota(jnp.int32, (LANES, HP * LANES), 0)
        e_c = lax.broadcasted_iota(jnp.int32, (LANES, HP * LANES), 1)
        pick = jnp.where(e_r == HP + e_c // LANES, 1.0, 0.0).astype(BF16)
        rows = grow_ref[0, 0, c]
        cols = gcol_ref[0, 0, c]
        b_rows = sum(_dot(p, upper) for p in _split3(_log_sigmoid(rows)))
        b_cols = sum(_dot(lower, p) for p in _split3(_log_sigmoid(cols)))
        b_colr = sum(_dot(p, pick) for p in _split3(b_cols))
        lane = lax.broadcasted_iota(jnp.int32, (1, L), 1)
        b_last_all = jnp.sum(jnp.where(lane == L - 1, b_rows, 0.0), axis=-1, keepdims=True)
        ones_v = jnp.ones((L, ML_V_DIM), BF16)
        for hh in range(HP):
            q = q_ref[0, pl.ds(r0, L), hh * ML_QK_DIM:(hh + 1) * ML_QK_DIM]
            kt = kt_ref[0, c, hh * ML_QK_DIM:(hh + 1) * ML_QK_DIM, :]
            v = v_ref[0, pl.ds(r0, L), hh * ML_V_DIM:(hh + 1) * ML_V_DIM]
            vx = jnp.concatenate([v, ones_v], axis=1)
            i_row = rows[hh:hh + 1, :]
            b_row = b_rows[HP + hh:HP + hh + 1, :]
            b_last = b_last_all[HP + hh:HP + hh + 1, :]
            b_col = b_colr[:, hh * LANES:(hh + 1) * LANES]
            m_prev = m_ref[hh]
            ctn = c_ref[hh]

            dmat = jnp.where(tril, _twice(b_col) - b_row + i_row, NEG_INF)
            inter = b_col + m_prev
            m_t = jnp.maximum(inter, jnp.max(dmat, axis=-1, keepdims=True))
            w_intra = jnp.exp(dmat - _twice(m_t))
            w_inter = jnp.exp(inter - m_t)
            intra = (w_intra * _dot(q, kt)).astype(BF16)
            tot = _twice(w_inter) * _dot(q, ctn.astype(BF16)) + _dot(intra, vx)
            num = tot[:, :ML_V_DIM]
            den = tot[:, ML_V_DIM:]
            hc = num / jnp.maximum(jnp.abs(den), jnp.exp(-m_t))
            y = hc * lax.rsqrt(jnp.mean(hc * hc, axis=-1, keepdims=True) + NORM_EPS)
            o_ref[0, pl.ds(r0, L), hh * ML_V_DIM:(hh + 1) * ML_V_DIM] = (
                y * gain_ref[:, hh * ML_V_DIM:(hh + 1) * ML_V_DIM])

            g_row = b_last - b_row + i_row
            m_new = jnp.maximum(b_last + m_prev, jnp.max(g_row, axis=-1, keepdims=True))
            decay = jnp.exp(b_last + m_prev - m_new)
            kw = (kt.astype(F32) * jnp.exp(g_row - _twice(m_new))).astype(BF16)
            c_ref[hh] = _twice(decay) * ctn + _dot(kw, vx)
            m_ref[hh] = m_new
        return carry

    lax.fori_loop(0, S // L, chunk, 0)


def _mlstm(q, kt, v, g_rows, g_cols, out_gain):
    B, S, _ = q.shape
    assert ML_CHUNK == 2 * LANES
    nc = S // ML_CHUNK
    hp = ML_STEP_HEADS
    return pl.pallas_call(
        _mlstm_kernel,
        out_shape=jax.ShapeDtypeStruct((B, S, ML_HEADS * ML_V_DIM), F32),
        grid=(B, ML_HEADS // hp),
        in_specs=[
            pl.BlockSpec((1, S, hp * ML_QK_DIM), lambda b, p: (b, 0, p)),
            pl.BlockSpec((1, nc, hp * ML_QK_DIM, ML_CHUNK), lambda b, p: (b, 0, p, 0)),
            pl.BlockSpec((1, S, hp * ML_V_DIM), lambda b, p: (b, 0, p)),
            pl.BlockSpec((1, 1, nc, 2 * hp, ML_CHUNK), lambda b, p: (b, p, 0, 0, 0)),
            pl.BlockSpec((1, 1, nc, ML_CHUNK, LANES), lambda b, p: (b, p, 0, 0, 0)),
            pl.BlockSpec((1, hp * ML_V_DIM), lambda b, p: (0, p)),
        ],
        out_specs=pl.BlockSpec((1, S, hp * ML_V_DIM), lambda b, p: (b, 0, p)),
        scratch_shapes=[
            pltpu.VMEM((hp, ML_QK_DIM, ML_V_DIM + LANES), F32),
            pltpu.VMEM((hp, 1, LANES), F32),
        ],
        compiler_params=_cparams("parallel", "parallel"),
        name="mlstm_chunkwise",
    )(q, kt, v, g_rows, g_cols, out_gain)


def _ml_out_kernel(hh_ref, og_ref, w_ref, x_ref, g_ref, o_ref):
    a = jax.nn.sigmoid(og_ref[0]) * hh_ref[0]
    o_ref[0] = x_ref[0] + g_ref[0] * _dot(a.astype(BF16), w_ref[...])


def _ml_out(hh, og, w_out_bf, x, g1):
    B, S, D = x.shape
    tm = TOKEN_TILE
    row = lambda b, j: (b, j, 0)
    per_b = lambda b, j: (b, 0, 0)
    const2 = lambda b, j: (0, 0)
    return pl.pallas_call(
        _ml_out_kernel,
        out_shape=jax.ShapeDtypeStruct((B, S, D), F32),
        grid=(B, S // tm),
        in_specs=[
            pl.BlockSpec((1, tm, D), row),
            pl.BlockSpec((1, tm, D), row),
            pl.BlockSpec((D, D), const2),
            pl.BlockSpec((1, tm, D), row),
            pl.BlockSpec((1, 1, D), per_b),
        ],
        out_specs=pl.BlockSpec((1, tm, D), row),
        compiler_params=_cparams("parallel", "parallel"),
        name="mlstm_out_proj",
    )(hh, og, w_out_bf, x, g1)


def _first_argmax(x, lane, width):
    mx = jnp.max(x, axis=-1, keepdims=True)
    idx = jnp.min(jnp.where(x == mx, lane, width), axis=-1, keepdims=True)
    return mx, idx


def _router_kernel(x_ref, gain_ref, sc_ref, sh_ref, whi_ref, wlo_ref, b_ref, h_ref, r_ref, cnt_ref, run_ref):
    tm = x_ref.shape[1]

    @pl.when(jnp.logical_and(pl.program_id(0) == 0, pl.program_id(1) == 0))
    def _():
        run_ref[...] = jnp.zeros_like(run_ref)

    h = _modulated_norm(x_ref[0], gain_ref[...], sc_ref[0], sh_ref[0])
    h_hi = h.astype(BF16)
    h_ref[0] = _pack_bf16_pairs(h)
    h_lo = (h - h_hi.astype(F32)).astype(BF16)
    whi = whi_ref[...]
    logits = _dot(h_hi, whi) + _dot(h_lo, whi) + _dot(h_hi, wlo_ref[...]) + b_ref[...]
    lane = lax.broadcasted_iota(jnp.int32, (1, LANES), 1)
    lg = jnp.where(lane < N_GROUPS, logits, NEG_INF)
    g_max, g_sel = _first_argmax(lg, lane, LANES)
    pg = 1.0 / jnp.sum(jnp.exp(lg - g_max), axis=-1, keepdims=True)
    e_lane = lane - N_GROUPS
    in_grp = jnp.logical_and(e_lane >= g_sel * EXPERTS_PER_GROUP, e_lane < (g_sel + 1) * EXPERTS_PER_GROUP)
    le = jnp.where(in_grp, logits, NEG_INF)
    v1, i1 = _first_argmax(le, lane, LANES)
    le2 = jnp.where(lane == i1, NEG_INF, le)
    v2, i2 = _first_argmax(le2, lane, LANES)
    e2 = jnp.exp(v2 - v1)
    w1 = pg / (1.0 + e2)
    w2 = pg * e2 / (1.0 + e2)
    e1 = i1 - N_GROUPS
    e2 = i2 - N_GROUPS
    hot1 = lane == e1
    hot2 = lane == e2
    onehot = jnp.where(jnp.logical_or(hot1, hot2), 1.0, 0.0)
    r_i = lax.broadcasted_iota(jnp.int32, (tm, tm), 0)
    c_i = lax.broadcasted_iota(jnp.int32, (tm, tm), 1)
    before = jnp.where(c_i < r_i, 1.0, 0.0).astype(BF16)
    seen = _dot(before, onehot.astype(BF16)) + run_ref[...]
    rank1 = jnp.sum(jnp.where(hot1, seen, 0.0), axis=-1, keepdims=True)
    rank2 = jnp.sum(jnp.where(hot2, seen, 0.0), axis=-1, keepdims=True)
    run_ref[...] = run_ref[...] + jnp.sum(onehot, axis=0, keepdims=True)
    cnt_ref[...] = run_ref[...]

    out = jnp.where(lane == 0, e1.astype(F32), 0.0)
    out = jnp.where(lane == 1, e2.astype(F32), out)
    out = jnp.where(lane == 2, w1, out)
    out = jnp.where(lane == 3, w2, out)
    out = jnp.where(lane == 4, rank1, out)
    out = jnp.where(lane == 5, rank2, out)
    r_ref[0] = out


def _router(x, gain, sc, sh, w_hi, w_lo, bias):
    B, S, D = x.shape
    tm = TOKEN_TILE
    row = lambda b, j: (b, j, 0)
    per_b = lambda b, j: (b, 0, 0)
    const2 = lambda b, j: (0, 0)
    return pl.pallas_call(
        _router_kernel,
        out_shape=(
            jax.ShapeDtypeStruct((B, S, D // 2), jnp.int32),
            jax.ShapeDtypeStruct((B, S, LANES), F32),
            jax.ShapeDtypeStruct((1, LANES), F32),
        ),
        grid=(B, S // tm),
        in_specs=[
            pl.BlockSpec((1, tm, D), row),
            pl.BlockSpec((1, D), const2),
            pl.BlockSpec((1, 1, D), per_b),
            pl.BlockSpec((1, 1, D), per_b),
            pl.BlockSpec((D, LANES), const2),
            pl.BlockSpec((D, LANES), const2),
            pl.BlockSpec((1, LANES), const2),
        ],
        out_specs=(
            pl.BlockSpec((1, tm, D // 2), row),
            pl.BlockSpec((1, tm, LANES), row),
            pl.BlockSpec((1, LANES), const2),
        ),
        scratch_shapes=[pltpu.VMEM((1, LANES), F32)],
        compiler_params=_cparams("arbitrary", "arbitrary"),
        name="moe_router",
    )(x, gain, sc, sh, w_hi, w_lo, bias)


def _experts_kernel(blk_e_ref, n_used_ref, next_e_ref, x_ref, wg_hbm, wu_hbm, wd_hbm, o_ref,
                    wg_f, wu_f, wd_f, wg_s, wu_s, wd_s, sem, *, layer):
    i = pl.program_id(0)
    used = i < n_used_ref[0]
    e = blk_e_ref[i]
    new_expert = jnp.logical_or(i == 0, e != blk_e_ref[jnp.maximum(i - 1, 0)])

    def fetch(expert):
        return (pltpu.make_async_copy(wg_hbm.at[layer, expert], wg_f, sem.at[0]),
                pltpu.make_async_copy(wu_hbm.at[layer, expert], wu_f, sem.at[1]),
                pltpu.make_async_copy(wd_hbm.at[layer, expert], wd_f, sem.at[2]))

    @pl.when(i == 0)
    def _():
        for cp in fetch(e):
            cp.start()

    @pl.when(jnp.logical_and(used, new_expert))
    def _():
        for cp in fetch(e):
            cp.wait()
        wg_s[...] = wg_f[...].astype(BF16)
        wu_s[...] = wu_f[...].astype(BF16)
        wd_s[...] = wd_f[...].astype(BF16)

        @pl.when(next_e_ref[i] >= 0)
        def _():
            for cp in fetch(next_e_ref[i]):
                cp.start()

    @pl.when(used)
    def _():
        x = _unpack_bf16_pairs(x_ref[...]).astype(BF16)
        a = _dot(x, wg_s[...])
        u = _dot(x, wu_s[...])
        act = a * jax.nn.sigmoid(a) * u
        o_ref[...] = _pack_bf16_pairs(_dot(act.astype(BF16), wd_s[...]))

    @pl.when(i >= n_used_ref[0])
    def _():
        o_ref[...] = jnp.zeros_like(o_ref)


def _experts(layer, blk_e, n_used, next_e, xs, w_gate, w_up, w_down):
    R = xs.shape[0]
    D = 2 * xs.shape[1]
    n_blk = R // MOE_BLOCK
    rows = lambda i, be, nu, ne: (i, 0)
    grid_spec = pltpu.PrefetchScalarGridSpec(
        num_scalar_prefetch=3,
        grid=(n_blk,),
        in_specs=[
            pl.BlockSpec((MOE_BLOCK, D // 2), rows),
            pl.BlockSpec(memory_space=pl.ANY),
            pl.BlockSpec(memory_space=pl.ANY),
            pl.BlockSpec(memory_space=pl.ANY),
        ],
        out_specs=pl.BlockSpec((MOE_BLOCK, D // 2), rows),
        scratch_shapes=[
            pltpu.VMEM((D, D_EXPERT), F32),
            pltpu.VMEM((D, D_EXPERT), F32),
            pltpu.VMEM((D_EXPERT, D), F32),
            pltpu.VMEM((D, D_EXPERT), BF16),
            pltpu.VMEM((D, D_EXPERT), BF16),
            pltpu.VMEM((D_EXPERT, D), BF16),
            pltpu.SemaphoreType.DMA((3,)),
        ],
    )
    return pl.pallas_call(
        functools.partial(_experts_kernel, layer=layer),
        out_shape=jax.ShapeDtypeStruct((R, D // 2), jnp.int32),
        grid_spec=grid_spec,
        compiler_params=_cparams("arbitrary"),
        name="moe_experts",
    )(blk_e, n_used, next_e, xs, w_gate, w_up, w_down)


def _combine_kernel(x_ref, g_ref, y0_ref, y1_ref, r_ref, o_ref):
    r = r_ref[0]
    y = _unpack_bf16_pairs(y0_ref[0, 0]) * r[:, 2:3] + _unpack_bf16_pairs(y1_ref[0, 0]) * r[:, 3:4]
    o_ref[0] = x_ref[0] + g_ref[0] * y


def _combine(x, g2, y01, route):
    B, S, D = x.shape
    tm = TOKEN_TILE
    row = lambda b, j: (b, j, 0)
    per_b = lambda b, j: (b, 0, 0)
    return pl.pallas_call(
        _combine_kernel,
        out_shape=jax.ShapeDtypeStruct((B, S, D), F32),
        grid=(B, S // tm),
        in_specs=[
            pl.BlockSpec((1, tm, D), row),
            pl.BlockSpec((1, 1, D), per_b),
            pl.BlockSpec((1, 1, tm, D // 2), lambda b, j: (0, b, j, 0)),
            pl.BlockSpec((1, 1, tm, D // 2), lambda b, j: (1, b, j, 0)),
            pl.BlockSpec((1, tm, LANES), row),
        ],
        out_specs=pl.BlockSpec((1, tm, D), row),
        compiler_params=_cparams("parallel", "parallel"),
        name="moe_combine",
    )(x, g2, y01, y01, route)


SC_CORES = 2
SC_SUBCORES = 16
SC_WORKERS = SC_CORES * SC_SUBCORES
SC_CHUNK = 64


def _sc_mesh():
    return plsc.VectorSubcoreMesh(core_axis_name="c", subcore_axis_name="s",
                                  num_cores=SC_CORES, num_subcores=SC_SUBCORES)


def _sc_scatter_rows(src, idx, n_out):
    T, W = src.shape
    per_w = T // SC_WORKERS
    nch = per_w // SC_CHUNK
    idx4 = idx.reshape(TOP_K, SC_WORKERS, nch, SC_CHUNK)

    @functools.partial(
        pl.kernel, mesh=_sc_mesh(),
        out_type=jax.ShapeDtypeStruct((n_out, W), src.dtype),
        scratch_types=[pltpu.VMEM((TOP_K, nch, SC_CHUNK), jnp.int32), pltpu.VMEM((SC_CHUNK, W), src.dtype)],
        name="sc_scatter_rows",
    )
    def body(src_hbm, idx_hbm, out_hbm, idx_v, rows_v):
        wid = lax.axis_index("s") * SC_CORES + lax.axis_index("c")
        for s in range(TOP_K):
            pltpu.sync_copy(idx_hbm.at[s, wid], idx_v.at[s])

        @pl.loop(0, nch)
        def _(i):
            pltpu.sync_copy(src_hbm.at[pl.ds(wid * per_w + i * SC_CHUNK, SC_CHUNK)], rows_v)
            for s in range(TOP_K):
                pltpu.sync_copy(rows_v, out_hbm.at[idx_v.at[s, i]])

    return body(src, idx4)


def _sc_gather_rows(table, idx):
    N = idx.shape[0]
    W = table.shape[1]
    per_w = N // SC_WORKERS
    nch = per_w // SC_CHUNK
    idx3 = idx.reshape(SC_WORKERS, nch, SC_CHUNK)

    @functools.partial(
        pl.kernel, mesh=_sc_mesh(),
        out_type=jax.ShapeDtypeStruct((N, W), table.dtype),
        scratch_types=[
            pltpu.VMEM((nch, SC_CHUNK), jnp.int32),
            pltpu.VMEM((2, SC_CHUNK, W), table.dtype),
            pltpu.SemaphoreType.DMA((2,)),
            pltpu.SemaphoreType.DMA((2,)),
        ],
        name="sc_gather_rows",
    )
    def body(table_hbm, idx_hbm, out_hbm, idx_v, rows_v, gather_sem, write_sem):
        wid = lax.axis_index("s") * SC_CORES + lax.axis_index("c")
        pltpu.sync_copy(idx_hbm.at[wid], idx_v)

        def gather(j, b):
            return pltpu.make_async_copy(table_hbm.at[idx_v.at[j]], rows_v.at[b], gather_sem.at[b])

        def write(j, b):
            return pltpu.make_async_copy(
                rows_v.at[b], out_hbm.at[pl.ds(wid * per_w + j * SC_CHUNK, SC_CHUNK)], write_sem.at[b])

        gather(0, 0).start()

        @pl.loop(0, nch, step=2)
        def _(i):
            for b in range(2):
                j = i + b

                @pl.when(j >= 1)
                def _():
                    write(j - 1, 1 - b).wait()

                @pl.when(j + 1 < nch)
                def _():
                    gather(j + 1, 1 - b).start()

                gather(j, b).wait()
                write(j, b).start()

        write(nch - 1, (nch - 1) % 2).wait()

    assert nch % 2 == 0
    return body(table, idx3)


def _moe_dispatch(route, counts, T):
    A = T * TOP_K
    counts = counts[0, :N_EXPERTS].astype(jnp.int32)
    blocks_per = (counts + MOE_BLOCK - 1) // MOE_BLOCK
    block_end = jnp.cumsum(blocks_per)
    block_start = block_end - blocks_per
    expert = route[:, :TOP_K].astype(jnp.int32)
    rank = route[:, 4:4 + TOP_K].astype(jnp.int32)
    onehot = expert[:, :, None] == jnp.arange(N_EXPERTS, dtype=jnp.int32)
    start = jnp.sum(jnp.where(onehot, block_start, 0), axis=-1)
    dest = (start * MOE_BLOCK + rank).T
    n_blk = -(-A // MOE_BLOCK) + N_EXPERTS
    blk = jnp.arange(n_blk, dtype=jnp.int32)
    blk_e = jnp.minimum(jnp.sum(blk[:, None] >= block_end[None, :], axis=-1), N_EXPERTS - 1).astype(jnp.int32)
    n_used = block_end[-1]
    first = jnp.logical_and(blk < n_used, jnp.logical_or(blk == 0, blk_e != jnp.roll(blk_e, 1)))
    first_pos = jnp.where(first, blk, n_blk)
    next_pos = jnp.concatenate([lax.cummin(first_pos, axis=0, reverse=True)[1:], jnp.full((1,), n_blk, jnp.int32)])
    next_e = jnp.where(next_pos < n_blk, blk_e[jnp.minimum(next_pos, n_blk - 1)], -1).astype(jnp.int32)
    return dest, n_blk * MOE_BLOCK, blk_e, n_used.reshape(1).astype(jnp.int32), next_e


def _rope_tables(S):
    inv = 1.0 / (ROPE_THETA ** (jnp.arange(0, HEAD_DIM, 2, dtype=F32) / HEAD_DIM))
    ang = jnp.arange(S, dtype=F32)[:, None] * inv[None, :]
    cos, sin = jnp.cos(ang), jnp.sin(ang)
    cos_h = jnp.concatenate([cos, cos], axis=-1)
    sin_h = jnp.concatenate([-sin, sin], axis=-1)
    return jnp.tile(cos_h, (1, ATTN_HEADS)), jnp.tile(sin_h, (1, ATTN_HEADS))


def _pad_cols(w, width):
    return jnp.pad(w, ((0, 0), (0, width - w.shape[1])))


def kernel(x, c, ada_w, ada_b, norm_mix, norm_ffn, hy_w_in, hy_q_norm, hy_k_norm, hy_conv_w, hy_w_out, ml_w_in, ml_b_gates, ml_out_norm, ml_w_out, moe_w_group, moe_b_group, moe_w_expert, moe_b_expert, moe_w_gate, moe_w_up, moe_w_down):
    B, S, D = x.shape
    T = B * S
    cos_t, sin_t = _rope_tables(S)
    mod = _ada_modulation(c, ada_w, ada_b).reshape(DEPTH, B, 6, 1, D)
    r_i = np.arange(ATTN_WIDTH)
    grp = jnp.asarray((r_i[:, None] // HEAD_DIM) == (r_i[None, :] // HEAD_DIM), dtype=BF16)

    for l in range(DEPTH):
        sh1, sc1, g1, sh2, sc2, g2 = [mod[l, :, i] for i in range(6)]
        gain1 = norm_mix[l].reshape(1, D)
        j = l // 2
        if l % 2 == 0:
            w = hy_w_in[j]
            o = np.cumsum((0,) + (ATTN_WIDTH, HEAD_DIM, HEAD_DIM, IDX_HEADS * IDX_DIM, IDX_DIM, IDX_HEADS,
                                  CONV_WIDTH, CONV_WIDTH, CONV_WIDTH))
            wq, wk, wv, wiq, wik, wiw, wbg, wcg, wu = [w[:, o[i]:o[i + 1]] for i in range(9)]
            w_pad = jnp.concatenate(
                [wq, wiq, wbg, wcg, wu, wk, wv, _pad_cols(jnp.concatenate([wik, wiw], axis=1), LANES)],
                axis=1).astype(BF16)
            qn_t = jnp.tile(hy_q_norm[j], ATTN_HEADS).reshape(1, ATTN_WIDTH)
            kn_t = jnp.tile(hy_k_norm[j], LANES // HEAD_DIM).reshape(1, LANES)
            qt, iqt, bcu, kv, kvt, ik, iwt = _hyb_in(x, gain1, sc1, sh1, w_pad, cos_t, sin_t, qn_t, kn_t, grp)
            y_attn = _dsa_attention_t(qt, iqt, iwt, kv, kvt, ik)
            x = _hyb_out(y_attn, bcu, hy_conv_w[j], hy_w_out[j].astype(BF16), x, g1)
        else:
            w = ml_w_in[j]
            hq = ML_HEADS * ML_QK_DIM
            hv = ML_HEADS * ML_V_DIM
            wq, wk, wv = w[:, :hq], w[:, hq:2 * hq], w[:, 2 * hq:2 * hq + hv]
            wg = w[:, 2 * hq + hv:2 * hq + hv + 2 * ML_HEADS]
            wo = w[:, 2 * hq + hv + 2 * ML_HEADS:]
            w_pad = jnp.concatenate([wq, wk, wv, wo, _pad_cols(wg, LANES)], axis=1).astype(BF16)
            gate_bias = jnp.pad(ml_b_gates[j], (0, LANES - 2 * ML_HEADS)).reshape(1, LANES)
            q, k, v, og, gates = _ml_in(x, gain1, sc1, sh1, w_pad, gate_bias)
            nc = S // ML_CHUNK
            hp = ML_STEP_HEADS
            groups = ML_HEADS // hp
            gi = gates[:, :, :ML_HEADS].reshape(B, S, groups, hp)
            gf = gates[:, :, ML_HEADS:2 * ML_HEADS].reshape(B, S, groups, hp)
            gp = jnp.concatenate([gi, gf], axis=-1)
            gp = jnp.transpose(gp, (0, 2, 1, 3)).reshape(B, groups, nc, ML_CHUNK, 2 * hp)
            g_cols = jnp.pad(gp, ((0, 0),) * 4 + ((0, LANES - 2 * hp),))
            g_rows = jnp.swapaxes(gp, -1, -2)
            hh = _mlstm(q, k, v, g_rows, g_cols, ml_out_norm[j].reshape(1, hv))
            x = _ml_out(hh, og, ml_w_out[j].astype(BF16), x, g1)

        w_r = _pad_cols(jnp.concatenate([moe_w_group[l], moe_w_expert[l]], axis=1), LANES)
        w_hi = w_r.astype(BF16)
        w_lo = (w_r - w_hi.astype(F32)).astype(BF16)
        b_r = jnp.pad(jnp.concatenate([moe_b_group[l], moe_b_expert[l]]), (0, LANES - N_GROUPS - N_EXPERTS))
        h2, route, counts = _router(x, norm_ffn[l].reshape(1, D), sc2, sh2, w_hi, w_lo, b_r.reshape(1, LANES))
        dest, n_rows, blk_e, n_used, next_e = _moe_dispatch(route.reshape(T, LANES), counts, T)
        xs = _sc_scatter_rows(h2.reshape(T, D // 2), dest, n_rows)
        ys = _experts(l, blk_e, n_used, next_e, xs, moe_w_gate, moe_w_up, moe_w_down)
        y01 = _sc_gather_rows(ys, dest.reshape(TOP_K * T)).reshape(TOP_K, B, S, D // 2)
        x = _combine(x, g2, y01, route)
    return x
```

```python
import functools

import numpy as np
import jax
import jax.numpy as jnp
from jax import lax
from jax.experimental import pallas as pl
from jax.experimental.pallas import tpu as pltpu
from jax.experimental.pallas import tpu_sc as plsc

F32 = jnp.float32
BF16 = jnp.bfloat16
HIGHEST = lax.Precision.HIGHEST

D_MODEL = 1024
DEPTH = 4
ATTN_HEADS = 8
HEAD_DIM = 64
ATTN_WIDTH = ATTN_HEADS * HEAD_DIM
IDX_HEADS = 8
IDX_DIM = 64
INDEX_TOPK = 256
Q_BLOCK = 256
ROPE_THETA = 10000.0
CONV_WIDTH = D_MODEL - ATTN_WIDTH
CONV_K = 3
ML_HEADS = 8
ML_QK_DIM = 64
ML_V_DIM = 128
N_GROUPS = 4
EXPERTS_PER_GROUP = 8
N_EXPERTS = N_GROUPS * EXPERTS_PER_GROUP
TOP_K = 2
D_EXPERT = 512
MOE_BLOCK = 512
NORM_EPS = 1e-6

LANES = 128
VMEM_LIMIT = 56 * 1024 * 1024
TOKEN_TILE = 256
ML_CHUNK = 256
ML_STEP_HEADS = 8
NEG_INF = float("-inf")


def _cparams(*sem):
    return pltpu.CompilerParams(dimension_semantics=sem, vmem_limit_bytes=VMEM_LIMIT)


def _dot(a, b):
    return jnp.dot(a, b, preferred_element_type=F32)


def _pack_bf16_pairs(x):
    bits = lax.bitcast_convert_type(x.astype(BF16).astype(F32), jnp.uint32)
    half = bits.shape[1] // 2
    packed = (bits[:, :half] >> 16) | (bits[:, half:] & jnp.uint32(0xFFFF0000))
    return lax.bitcast_convert_type(packed, jnp.int32)


def _unpack_bf16_pairs(words):
    words = lax.bitcast_convert_type(words, jnp.uint32)
    return jnp.concatenate(
        [lax.bitcast_convert_type(words << 16, F32),
         lax.bitcast_convert_type(words & jnp.uint32(0xFFFF0000), F32)], axis=1)


def _split_dot(a_f32, b_bf16):
    hi = a_f32.astype(BF16)
    lo = (a_f32 - hi.astype(F32)).astype(BF16)
    return _dot(hi, b_bf16) + _dot(lo, b_bf16)


def _ada_kernel(c_ref, w_ref, b_ref, o_ref):
    c = c_ref[...]
    ca = c * jax.nn.sigmoid(c)
    o_ref[0] = jnp.dot(ca, w_ref[0], precision=HIGHEST, preferred_element_type=F32) + b_ref[0]


def _ada_modulation(c, ada_w, ada_b):
    B, D = c.shape
    n_col = ada_w.shape[-1] // D
    return pl.pallas_call(
        _ada_kernel,
        out_shape=jax.ShapeDtypeStruct((DEPTH, B, n_col * D), F32),
        grid=(DEPTH, n_col),
        in_specs=[
            pl.BlockSpec((B, D), lambda l, j: (0, 0)),
            pl.BlockSpec((1, D, D), lambda l, j: (l, 0, j)),
            pl.BlockSpec((1, 1, D), lambda l, j: (l, 0, j)),
        ],
        out_specs=pl.BlockSpec((1, B, D), lambda l, j: (l, 0, j)),
        compiler_params=_cparams("parallel", "parallel"),
        name="ada_modulation",
    )(c, ada_w, ada_b.reshape(DEPTH, 1, n_col * D))


def _modulated_norm(x, gain, scale, shift):
    y = x * lax.rsqrt(jnp.mean(x * x, axis=-1, keepdims=True) + NORM_EPS)
    return y * gain * (1.0 + scale) + shift


def _rope(x, cos, sin_signed, first_half):
    w = x.shape[-1]
    partner = jnp.where(first_half, pltpu.roll(x, w - HEAD_DIM // 2, 1), pltpu.roll(x, HEAD_DIM // 2, 1))
    return x * cos + partner * sin_signed


HYB_COLS = 5 * 512 + 2 * LANES


def _hyb_in_kernel(x_ref, gain_ref, sc_ref, sh_ref, w_ref, cos_ref, sin_ref, qn_ref, kn_ref, grp_ref,
                   qt_ref, iqt_ref, bcu_ref, kv_ref, kvt_ref, ik_ref, iwt_ref):
    h = _modulated_norm(x_ref[0], gain_ref[...], sc_ref[0], sh_ref[0])
    p = _dot(h.astype(BF16), w_ref[...])
    cos = cos_ref[...]
    sin = sin_ref[...]
    lane = lax.broadcasted_iota(jnp.int32, (1, ATTN_WIDTH), 1)
    first_half = (lane % HEAD_DIM) < (HEAD_DIM // 2)
    fh128 = first_half[:, :LANES]
    lane128 = lane[:, :LANES]

    q = p[:, 0:512]
    ms = _split_dot(q * q, grp_ref[...]) * (1.0 / HEAD_DIM)
    q = q * lax.rsqrt(ms + NORM_EPS) * qn_ref[...]
    qt_ref[0] = (_rope(q, cos, sin, first_half) * (HEAD_DIM ** -0.5)).T.astype(BF16)

    iq = p[:, 512:1024]
    iqt_ref[0] = (_rope(iq, cos, sin, first_half) * (IDX_DIM ** -0.5)).T.astype(BF16)

    bcu_ref[0] = p[:, 1024:2560]

    kv = p[:, 2560:2688]
    is_k = lane128 < HEAD_DIM
    kk = jnp.where(is_k, kv, 0.0)
    ms_k = jnp.sum(kk * kk, axis=-1, keepdims=True) * (1.0 / HEAD_DIM)
    kn = kv * lax.rsqrt(ms_k + NORM_EPS) * kn_ref[...]
    kr = _rope(kn, cos[:, :LANES], sin[:, :LANES], fh128)
    kv = jnp.where(is_k, kr, kv)
    kv_ref[0] = kv.astype(BF16)
    kvt_ref[0, 0] = kv.T.astype(BF16)

    sm = p[:, 2688:2816]
    ikr = _rope(sm, cos[:, :LANES], sin[:, :LANES], fh128)
    ik_ref[0] = jnp.where(is_k, ikr, 0.0).astype(BF16)
    iwt_ref[0] = sm.T


def _hyb_in(x, gain, sc, sh, w_pad, cos_t, sin_t, qn_t, kn_t, grp):
    B, S, D = x.shape
    tm = TOKEN_TILE
    row = lambda b, j: (b, j, 0)
    per_b = lambda b, j: (b, 0, 0)
    const2 = lambda b, j: (0, 0)
    tab = lambda b, j: (j, 0)
    col = lambda b, j: (b, 0, j)
    assert tm == DSA_KEY_CHUNK
    return pl.pallas_call(
        _hyb_in_kernel,
        out_shape=(
            jax.ShapeDtypeStruct((B, 512, S), BF16),
            jax.ShapeDtypeStruct((B, 512, S), BF16),
            jax.ShapeDtypeStruct((B, S, 1536), F32),
            jax.ShapeDtypeStruct((B, S, LANES), BF16),
            jax.ShapeDtypeStruct((B, S // tm, LANES, tm), BF16),
            jax.ShapeDtypeStruct((B, S, LANES), BF16),
            jax.ShapeDtypeStruct((B, LANES, S), F32),
        ),
        grid=(B, S // tm),
        in_specs=[
            pl.BlockSpec((1, tm, D), row),
            pl.BlockSpec((1, D), const2),
            pl.BlockSpec((1, 1, D), per_b),
            pl.BlockSpec((1, 1, D), per_b),
            pl.BlockSpec((D, HYB_COLS), const2),
            pl.BlockSpec((tm, 512), tab),
            pl.BlockSpec((tm, 512), tab),
            pl.BlockSpec((1, 512), const2),
            pl.BlockSpec((1, LANES), const2),
            pl.BlockSpec((512, 512), const2),
        ],
        out_specs=(
            pl.BlockSpec((1, 512, tm), col),
            pl.BlockSpec((1, 512, tm), col),
            pl.BlockSpec((1, tm, 1536), row),
            pl.BlockSpec((1, tm, LANES), row),
            pl.BlockSpec((1, 1, LANES, tm), lambda b, j: (b, j, 0, 0)),
            pl.BlockSpec((1, tm, LANES), row),
            pl.BlockSpec((1, LANES, tm), col),
        ),
        compiler_params=_cparams("parallel", "parallel"),
        name="hybrid_in_proj",
    )(x, gain, sc, sh, w_pad, cos_t, sin_t, qn_t, kn_t, grp)


DSA_KEY_CHUNK = 256
F32_LOWEST = float(np.finfo(np.float32).min)


def _fold8(x, op):
    parts = x.reshape(x.shape[0] // 8, 8, x.shape[1])
    while parts.shape[0] > 1:
        half = parts.shape[0] // 2
        assert parts.shape[0] == 2 * half
        parts = op(parts[:half], parts[half:])
    return parts[0]


def _col_reduce(x, op):
    t = _fold8(x, op)
    for shift in (4, 2, 1):
        t = op(t, pltpu.roll(t, shift, 0))
    return t[0:1, :]


def _dsa_t_kernel(qt_ref, iqt_ref, iwt_ref, kv_ref, kvt_ref, ik_ref, o_ref, sc_ref, bias_ref, acc_ref):
    CK = DSA_KEY_CHUNK
    QB = Q_BLOCK
    qb = pl.program_id(1)
    nk = (qb * QB + QB + CK - 1) // CK
    kf = float(INDEX_TOPK)
    qpos = qb * QB + lax.broadcasted_iota(jnp.int32, (1, QB), 1)
    krow = lax.broadcasted_iota(jnp.int32, (CK, 1), 0)
    w_idx = iwt_ref[0, IDX_DIM:IDX_DIM + IDX_HEADS, :] * (IDX_HEADS ** -0.5)

    def rows(c):
        return pl.ds(pl.multiple_of(c * CK, CK), CK)

    def heads_on_lanes(ref, width):
        return jnp.concatenate([ref[0, hd * width:(hd + 1) * width, :] for hd in range(ref.shape[1] // width)], axis=1)

    def head_lanes(hd):
        return slice(hd * QB, (hd + 1) * QB)

    iq_wide = heads_on_lanes(iqt_ref, IDX_DIM)
    w_wide = jnp.concatenate([w_idx[hd:hd + 1, :] for hd in range(IDX_HEADS)], axis=1)

    def score_chunk(c, carry):
        mx, mn = carry
        ikc = ik_ref[0, rows(c), :][:, :IDX_DIM]
        s_all = jnp.maximum(_dot(ikc, iq_wide), 0.0) * w_wide
        acc = s_all[:, head_lanes(0)]
        for hd in range(1, IDX_HEADS):
            acc = acc + s_all[:, head_lanes(hd)]
        causal = (c * CK + krow) <= qpos
        sc_ref[rows(c), :] = jnp.where(causal, acc, NEG_INF)
        mx = jnp.maximum(mx, _fold8(jnp.where(causal, acc, NEG_INF), jnp.maximum))
        mn = jnp.minimum(mn, _fold8(jnp.where(causal, acc, jnp.inf), jnp.minimum))
        return mx, mn

    mx8, mn8 = lax.fori_loop(0, nk, score_chunk,
                             (jnp.full((8, QB), NEG_INF, F32), jnp.full((8, QB), jnp.inf, F32)))
    row_max = jnp.max(mx8, axis=0, keepdims=True)
    row_min = jnp.min(mn8, axis=0, keepdims=True)

    @pl.when(nk % 2 == 1)
    def _():
        sc_ref[rows(nk), :] = jnp.full((CK, QB), NEG_INF, F32)

    n_pairs = (nk + 1) // 2

    def pair_rows(c):
        return pl.ds(pl.multiple_of(c * (2 * CK), 2 * CK), 2 * CK)

    def count(pred):
        def body(c, part):
            return part + _fold8(jnp.where(pred(sc_ref[pair_rows(c), :]), 1.0, 0.0), jnp.add)
        part = lax.fori_loop(0, n_pairs, body, jnp.zeros((8, QB), F32))
        return jnp.sum(part, axis=0, keepdims=True)

    @pl.when(qb * QB + QB <= INDEX_TOPK)
    def _():
        def body(c, carry):
            bias_ref[rows(c), :] = jnp.where(sc_ref[rows(c), :] > NEG_INF, 0.0, NEG_INF)
            return carry
        lax.fori_loop(0, nk, body, 0)

    @pl.when(qb * QB + QB > INDEX_TOPK)
    def _():
        top_tied = count(lambda x: x >= row_max) >= kf

        def bisect(_, carry):
            lo, hi = carry
            mid = 0.5 * lo + 0.5 * hi
            ge = count(lambda x: x >= mid) >= kf
            return jnp.where(ge, mid, lo), jnp.where(ge, hi, mid)

        lo, hi = lax.fori_loop(0, 18, bisect, (row_min, row_max))

        def refine_cond(carry):
            it, _, _, done = carry
            return jnp.logical_and(it < nk * CK, jnp.min(done) < 0.5)

        def refine(carry):
            it, hi, thr, done = carry

            def below(c, part):
                x = sc_ref[pair_rows(c), :]
                return jnp.maximum(part, _fold8(jnp.where(x < hi, x, NEG_INF), jnp.maximum))

            m = jnp.max(lax.fori_loop(0, n_pairs, below, jnp.full((8, QB), NEG_INF, F32)), axis=0, keepdims=True)
            hit = count(lambda x: x >= m) >= kf
            fin = done > 0.5
            thr = jnp.where(fin, thr, m)
            hi = jnp.where(jnp.logical_or(fin, hit), hi, m)
            done = jnp.where(hit, 1.0, done)
            return it + 1, hi, thr, done

        done0 = jnp.where(top_tied, 1.0, 0.0)
        _, _, thr, _ = lax.while_loop(refine_cond, refine, (jnp.int32(0), hi, row_max, done0))

        need = kf - count(lambda x: x > thr)
        n_eq = count(lambda x: x == thr)
        tied = jnp.max(n_eq - need) > 0.5

        @pl.when(jnp.logical_not(tied))
        def _():
            def body(c, carry):
                bias_ref[rows(c), :] = jnp.where(sc_ref[rows(c), :] >= thr, 0.0, NEG_INF)
                return carry
            lax.fori_loop(0, nk, body, 0)

        @pl.when(tied)
        def _():
            r_i = lax.broadcasted_iota(jnp.int32, (CK, CK), 0)
            c_i = lax.broadcasted_iota(jnp.int32, (CK, CK), 1)
            lower = jnp.where(c_i <= r_i, 1.0, 0.0).astype(BF16)

            def body(c, seen):
                x = sc_ref[rows(c), :]
                eq = x == thr
                eq_f = jnp.where(eq, 1.0, 0.0)
                rank = _dot(lower, eq_f.astype(BF16)) + seen
                keep = jnp.logical_or(x > thr, jnp.logical_and(eq, rank <= need))
                bias_ref[rows(c), :] = jnp.where(keep, 0.0, NEG_INF)
                return seen + jnp.sum(eq_f, axis=0, keepdims=True)

            lax.fori_loop(0, nk, body, jnp.zeros((1, QB), F32))

    acc_ref[...] = jnp.zeros_like(acc_ref)
    q_wide = heads_on_lanes(qt_ref, HEAD_DIM)

    def attend_chunk(c, carry):
        m_old, l_old = carry
        kc = kv_ref[0, rows(c), :][:, :HEAD_DIM]
        vt = kvt_ref[0, c, HEAD_DIM:, :]
        bias = bias_ref[rows(c), :]
        logits = _dot(kc, q_wide) + jnp.concatenate([bias] * ATTN_HEADS, axis=1)
        m_new = jnp.maximum(m_old, _col_reduce(logits, jnp.maximum))
        alpha = jnp.exp(m_old - m_new)
        p = jnp.exp(logits - m_new)
        l_new = alpha * l_old + _col_reduce(p, jnp.add)
        acc_ref[...] = alpha * acc_ref[...] + _dot(vt, p.astype(BF16))
        return m_new, l_new

    _, l_all = lax.fori_loop(0, nk, attend_chunk, (jnp.full((1, ATTN_HEADS * QB), -1e30, F32),
                                                   jnp.zeros((1, ATTN_HEADS * QB), F32)))
    out_t = acc_ref[...] / l_all
    o_ref[0] = jnp.concatenate([out_t[:, head_lanes(hd)] for hd in range(ATTN_HEADS)], axis=0).T


def _dsa_attention_t(qt, iqt, iwt, kv, kvt, ik):
    B, _, S = qt.shape
    col = lambda b, j: (b, 0, j)
    per_b = lambda b, j: (b, 0, 0)
    return pl.pallas_call(
        _dsa_t_kernel,
        out_shape=jax.ShapeDtypeStruct((B, S, ATTN_WIDTH), F32),
        grid=(B, S // Q_BLOCK),
        in_specs=[
            pl.BlockSpec((1, ATTN_WIDTH, Q_BLOCK), col),
            pl.BlockSpec((1, IDX_HEADS * IDX_DIM, Q_BLOCK), col),
            pl.BlockSpec((1, LANES, Q_BLOCK), col),
            pl.BlockSpec((1, S, LANES), per_b),
            pl.BlockSpec((1, S // DSA_KEY_CHUNK, LANES, DSA_KEY_CHUNK), lambda b, j: (b, 0, 0, 0)),
            pl.BlockSpec((1, S, LANES), per_b),
        ],
        out_specs=pl.BlockSpec((1, Q_BLOCK, ATTN_WIDTH), lambda b, j: (b, j, 0)),
        scratch_shapes=[
            pltpu.VMEM((S, Q_BLOCK), F32),
            pltpu.VMEM((S, Q_BLOCK), F32),
            pltpu.VMEM((HEAD_DIM, ATTN_HEADS * Q_BLOCK), F32),
        ],
        compiler_params=_cparams("parallel", "parallel"),
        name="dsa_attention",
    )(qt, iqt, iwt, kv, kvt, ik)


def _hyb_out_kernel(ya_ref, bcu_ref, halo_ref, cw_ref, w_ref, x_ref, g_ref,
                    gain2_ref, sc2_ref, sh2_ref, wr_ref, br_ref, o_ref, h_ref, r_ref, cnt_ref, run_ref):
    j = pl.program_id(1)
    tm = ya_ref.shape[1]
    bcu = bcu_ref[0]
    bg = bcu[:, 0:512]
    z = bcu[:, 512:1024] * bcu[:, 1024:1536]
    halo = halo_ref[0]
    zh = halo[:, 512:1024] * halo[:, 1024:1536]
    zh = jnp.where(j > 0, zh, 0.0)
    row = lax.broadcasted_iota(jnp.int32, (tm, 1), 0)
    z1 = jnp.where(row >= 1, pltpu.roll(z, 1, 0), zh[7:8, :])
    z2 = jnp.where(row >= 2, pltpu.roll(z, 2, 0), jnp.where(row == 1, zh[7:8, :], zh[6:7, :]))
    cw = cw_ref[...]
    y_conv = bg * (z2 * cw[0:1, :] + z1 * cw[1:2, :] + z * cw[2:3, :])
    y = _dot(ya_ref[0].astype(BF16), w_ref[0:512, :]) + _dot(y_conv.astype(BF16), w_ref[512:1024, :])
    x_new = x_ref[0] + g_ref[0] * y
    o_ref[0] = x_new
    _route_tile(x_new, gain2_ref, sc2_ref, sh2_ref, wr_ref, br_ref, h_ref, r_ref, cnt_ref, run_ref)


def _hyb_out(y_attn, bcu, conv_w, w_out_bf, x, g1, route_args):
    B, S, D = x.shape
    tm = TOKEN_TILE
    row = lambda b, j: (b, j, 0)
    per_b = lambda b, j: (b, 0, 0)
    const2 = lambda b, j: (0, 0)
    halo = lambda b, j: (b, jnp.maximum(j * (tm // 8) - 1, 0), 0)
    r_in, r_shape, r_out, r_scratch = _route_specs(B, S, D, tm)
    return pl.pallas_call(
        _hyb_out_kernel,
        out_shape=(jax.ShapeDtypeStruct((B, S, D), F32),) + r_shape,
        grid=(B, S // tm),
        in_specs=[
            pl.BlockSpec((1, tm, 512), row),
            pl.BlockSpec((1, tm, 1536), row),
            pl.BlockSpec((1, 8, 1536), halo),
            pl.BlockSpec((CONV_K, CONV_WIDTH), const2),
            pl.BlockSpec((D, D), const2),
            pl.BlockSpec((1, tm, D), row),
            pl.BlockSpec((1, 1, D), per_b),
        ] + r_in,
        out_specs=(pl.BlockSpec((1, tm, D), row),) + r_out,
        scratch_shapes=r_scratch,
        compiler_params=_cparams("arbitrary", "arbitrary"),
        name="hybrid_out_proj",
    )(y_attn, bcu, bcu, conv_w, w_out_bf, x, g1, *route_args)


ML_COLS = 512 + 512 + 1024 + 1024 + LANES


def _ml_in_kernel(x_ref, gain_ref, sc_ref, sh_ref, w_ref, bias_ref, q_ref, k_ref, v_ref, og_ref, gt_ref):
    h = _modulated_norm(x_ref[0], gain_ref[...], sc_ref[0], sh_ref[0])
    p = _dot(h.astype(BF16), w_ref[...])
    q_ref[0] = (p[:, 0:512] * (ML_QK_DIM ** -0.5)).astype(BF16)
    k_ref[0, 0] = p[:, 512:1024].T.astype(BF16)
    v_ref[0] = p[:, 1024:2048].astype(BF16)
    og_ref[0] = p[:, 2048:3072]
    gt_ref[0] = p[:, 3072:3200] + bias_ref[...]


def _ml_in(x, gain, sc, sh, w_pad, gate_bias):
    B, S, D = x.shape
    tm = TOKEN_TILE
    assert tm == ML_CHUNK
    row = lambda b, j: (b, j, 0)
    per_b = lambda b, j: (b, 0, 0)
    const2 = lambda b, j: (0, 0)
    return pl.pallas_call(
        _ml_in_kernel,
        out_shape=(
            jax.ShapeDtypeStruct((B, S, 512), BF16),
            jax.ShapeDtypeStruct((B, S // tm, 512, tm), BF16),
            jax.ShapeDtypeStruct((B, S, 1024), BF16),
            jax.ShapeDtypeStruct((B, S, 1024), F32),
            jax.ShapeDtypeStruct((B, S, LANES), F32),
        ),
        grid=(B, S // tm),
        in_specs=[
            pl.BlockSpec((1, tm, D), row),
            pl.BlockSpec((1, D), const2),
            pl.BlockSpec((1, 1, D), per_b),
            pl.BlockSpec((1, 1, D), per_b),
            pl.BlockSpec((D, ML_COLS), const2),
            pl.BlockSpec((1, LANES), const2),
        ],
        out_specs=(
            pl.BlockSpec((1, tm, 512), row),
            pl.BlockSpec((1, 1, 512, tm), lambda b, j: (b, j, 0, 0)),
            pl.BlockSpec((1, tm, 1024), row),
            pl.BlockSpec((1, tm, 1024), row),
            pl.BlockSpec((1, tm, LANES), row),
        ),
        compiler_params=_cparams("parallel", "parallel"),
        name="mlstm_in_proj",
    )(x, gain, sc, sh, w_pad, gate_bias)


def _log_sigmoid(f):
    return jnp.minimum(f, 0.0) - jnp.log1p(jnp.exp(-jnp.abs(f)))


def _split3(x):
    a = x.astype(BF16)
    r = x - a.astype(F32)
    b = r.astype(BF16)
    c = (r - b.astype(F32)).astype(BF16)
    return a, b, c


def _twice(a):
    return jnp.concatenate([a, a], axis=1)


def _mlstm_kernel(q_ref, kt_ref, v_ref, grow_ref, gcol_ref, gain_ref, o_ref, c_ref, m_ref):
    L = ML_CHUNK
    HP = ML_STEP_HEADS
    S = q_ref.shape[1]
    c_ref[...] = jnp.zeros_like(c_ref)
    m_ref[...] = jnp.zeros_like(m_ref)

    def chunk(c, carry):
        r0 = pl.multiple_of(c * L, L)
        r_i = lax.broadcasted_iota(jnp.int32, (L, L), 0)
        c_i = lax.broadcasted_iota(jnp.int32, (L, L), 1)
        tril = c_i <= r_i
        lower = jnp.where(tril, 1.0, 0.0).astype(BF16)
        upper = jnp.where(r_i <= c_i, 1.0, 0.0).astype(BF16)
        e_r = lax.broadcasted_iota(jnp.int32, (LANES, HP * LANES), 0)
        e_c = lax.broadcasted_iota(jnp.int32, (LANES, HP * LANES), 1)
        pick = jnp.where(e_r == HP + e_c // LANES, 1.0, 0.0).astype(BF16)
        rows = grow_ref[0, 0, c]
        cols = gcol_ref[0, 0, c]
        b_rows = sum(_dot(p, upper) for p in _split3(_log_sigmoid(rows)))
        b_cols = sum(_dot(lower, p) for p in _split3(_log_sigmoid(cols)))
        b_colr = sum(_dot(p, pick) for p in _split3(b_cols))
        lane = lax.broadcasted_iota(jnp.int32, (1, L), 1)
        b_last_all = jnp.sum(jnp.where(lane == L - 1, b_rows, 0.0), axis=-1, keepdims=True)
        ones_v = jnp.ones((L, ML_V_DIM), BF16)
        for hh in range(HP):
            q = q_ref[0, pl.ds(r0, L), hh * ML_QK_DIM:(hh + 1) * ML_QK_DIM]
            kt = kt_ref[0, c, hh * ML_QK_DIM:(hh + 1) * ML_QK_DIM, :]
            v = v_ref[0, pl.ds(r0, L), hh * ML_V_DIM:(hh + 1) * ML_V_DIM]
            vx = jnp.concatenate([v, ones_v], axis=1)
            i_row = rows[hh:hh + 1, :]
            b_row = b_rows[HP + hh:HP + hh + 1, :]
            b_last = b_last_all[HP + hh:HP + hh + 1, :]
            b_col = b_colr[:, hh * LANES:(hh + 1) * LANES]
            m_prev = m_ref[hh]
            ctn = c_ref[hh]

            dmat = jnp.where(tril, _twice(b_col) - b_row + i_row, NEG_INF)
            inter = b_col + m_prev
            m_t = jnp.maximum(inter, jnp.max(dmat, axis=-1, keepdims=True))
            w_intra = jnp.exp(dmat - _twice(m_t))
            w_inter = jnp.exp(inter - m_t)
            intra = (w_intra * _dot(q, kt)).astype(BF16)
            tot = _twice(w_inter) * _dot(q, ctn.astype(BF16)) + _dot(intra, vx)
            num = tot[:, :ML_V_DIM]
            den = tot[:, ML_V_DIM:]
            hc = num / jnp.maximum(jnp.abs(den), jnp.exp(-m_t))
            y = hc * lax.rsqrt(jnp.mean(hc * hc, axis=-1, keepdims=True) + NORM_EPS)
            o_ref[0, pl.ds(r0, L), hh * ML_V_DIM:(hh + 1) * ML_V_DIM] = (
                y * gain_ref[:, hh * ML_V_DIM:(hh + 1) * ML_V_DIM])

            g_row = b_last - b_row + i_row
            m_new = jnp.maximum(b_last + m_prev, jnp.max(g_row, axis=-1, keepdims=True))
            decay = jnp.exp(b_last + m_prev - m_new)
            kw = (kt.astype(F32) * jnp.exp(g_row - _twice(m_new))).astype(BF16)
            c_ref[hh] = _twice(decay) * ctn + _dot(kw, vx)
            m_ref[hh] = m_new
        return carry

    lax.fori_loop(0, S // L, chunk, 0)


def _mlstm(q, kt, v, g_rows, g_cols, out_gain):
    B, S, _ = q.shape
    assert ML_CHUNK == 2 * LANES
    nc = S // ML_CHUNK
    hp = ML_STEP_HEADS
    return pl.pallas_call(
        _mlstm_kernel,
        out_shape=jax.ShapeDtypeStruct((B, S, ML_HEADS * ML_V_DIM), F32),
        grid=(B, ML_HEADS // hp),
        in_specs=[
            pl.BlockSpec((1, S, hp * ML_QK_DIM), lambda b, p: (b, 0, p)),
            pl.BlockSpec((1, nc, hp * ML_QK_DIM, ML_CHUNK), lambda b, p: (b, 0, p, 0)),
            pl.BlockSpec((1, S, hp * ML_V_DIM), lambda b, p: (b, 0, p)),
            pl.BlockSpec((1, 1, nc, 2 * hp, ML_CHUNK), lambda b, p: (b, p, 0, 0, 0)),
            pl.BlockSpec((1, 1, nc, ML_CHUNK, LANES), lambda b, p: (b, p, 0, 0, 0)),
            pl.BlockSpec((1, hp * ML_V_DIM), lambda b, p: (0, p)),
        ],
        out_specs=pl.BlockSpec((1, S, hp * ML_V_DIM), lambda b, p: (b, 0, p)),
        scratch_shapes=[
            pltpu.VMEM((hp, ML_QK_DIM, ML_V_DIM + LANES), F32),
            pltpu.VMEM((hp, 1, LANES), F32),
        ],
        compiler_params=_cparams("parallel", "parallel"),
        name="mlstm_chunkwise",
    )(q, kt, v, g_rows, g_cols, out_gain)


def _ml_out_kernel(hh_ref, og_ref, w_ref, x_ref, g_ref,
                   gain2_ref, sc2_ref, sh2_ref, wr_ref, br_ref, o_ref, h_ref, r_ref, cnt_ref, run_ref):
    a = jax.nn.sigmoid(og_ref[0]) * hh_ref[0]
    x_new = x_ref[0] + g_ref[0] * _dot(a.astype(BF16), w_ref[...])
    o_ref[0] = x_new
    _route_tile(x_new, gain2_ref, sc2_ref, sh2_ref, wr_ref, br_ref, h_ref, r_ref, cnt_ref, run_ref)


def _ml_out(hh, og, w_out_bf, x, g1, route_args):
    B, S, D = x.shape
    tm = TOKEN_TILE
    row = lambda b, j: (b, j, 0)
    per_b = lambda b, j: (b, 0, 0)
    const2 = lambda b, j: (0, 0)
    r_in, r_shape, r_out, r_scratch = _route_specs(B, S, D, tm)
    return pl.pallas_call(
        _ml_out_kernel,
        out_shape=(jax.ShapeDtypeStruct((B, S, D), F32),) + r_shape,
        grid=(B, S // tm),
        in_specs=[
            pl.BlockSpec((1, tm, D), row),
            pl.BlockSpec((1, tm, D), row),
            pl.BlockSpec((D, D), const2),
            pl.BlockSpec((1, tm, D), row),
            pl.BlockSpec((1, 1, D), per_b),
        ] + r_in,
        out_specs=(pl.BlockSpec((1, tm, D), row),) + r_out,
        scratch_shapes=r_scratch,
        compiler_params=_cparams("arbitrary", "arbitrary"),
        name="mlstm_out_proj",
    )(hh, og, w_out_bf, x, g1, *route_args)


def _first_argmax(x, lane, width):
    mx = jnp.max(x, axis=-1, keepdims=True)
    idx = jnp.min(jnp.where(x == mx, lane, width), axis=-1, keepdims=True)
    return mx, idx


def _route_tile(x, gain_ref, sc_ref, sh_ref, w_ref, b_ref, h_ref, r_ref, cnt_ref, run_ref):
    tm = x.shape[0]

    @pl.when(jnp.logical_and(pl.program_id(0) == 0, pl.program_id(1) == 0))
    def _():
        run_ref[...] = jnp.zeros_like(run_ref)

    h = _modulated_norm(x, gain_ref[...], sc_ref[0], sh_ref[0])
    h_ref[0] = _pack_bf16_pairs(h)
    logits = _dot(h.astype(BF16), w_ref[...]) + b_ref[...]
    lane = lax.broadcasted_iota(jnp.int32, (1, LANES), 1)
    lg = jnp.where(lane < N_GROUPS, logits, NEG_INF)
    g_max, g_sel = _first_argmax(lg, lane, LANES)
    pg = 1.0 / jnp.sum(jnp.exp(lg - g_max), axis=-1, keepdims=True)
    e_lane = lane - N_GROUPS
    in_grp = jnp.logical_and(e_lane >= g_sel * EXPERTS_PER_GROUP, e_lane < (g_sel + 1) * EXPERTS_PER_GROUP)
    le = jnp.where(in_grp, logits, NEG_INF)
    v1, i1 = _first_argmax(le, lane, LANES)
    le2 = jnp.where(lane == i1, NEG_INF, le)
    v2, i2 = _first_argmax(le2, lane, LANES)
    e2 = jnp.exp(v2 - v1)
    w1 = pg / (1.0 + e2)
    w2 = pg * e2 / (1.0 + e2)
    e1 = i1 - N_GROUPS
    e2 = i2 - N_GROUPS
    hot1 = lane == e1
    hot2 = lane == e2
    onehot = jnp.where(jnp.logical_or(hot1, hot2), 1.0, 0.0)
    r_i = lax.broadcasted_iota(jnp.int32, (tm, tm), 0)
    c_i = lax.broadcasted_iota(jnp.int32, (tm, tm), 1)
    before = jnp.where(c_i < r_i, 1.0, 0.0).astype(BF16)
    seen = _dot(before, onehot.astype(BF16)) + run_ref[...]
    rank1 = jnp.sum(jnp.where(hot1, seen, 0.0), axis=-1, keepdims=True)
    rank2 = jnp.sum(jnp.where(hot2, seen, 0.0), axis=-1, keepdims=True)
    run_ref[...] = run_ref[...] + jnp.sum(onehot, axis=0, keepdims=True)
    cnt_ref[...] = run_ref[...]

    out = jnp.where(lane == 0, e1.astype(F32), 0.0)
    out = jnp.where(lane == 1, e2.astype(F32), out)
    out = jnp.where(lane == 2, w1, out)
    out = jnp.where(lane == 3, w2, out)
    out = jnp.where(lane == 4, rank1, out)
    out = jnp.where(lane == 5, rank2, out)
    r_ref[0] = out


def _route_specs(B, S, D, tm):
    row = lambda b, j: (b, j, 0)
    per_b = lambda b, j: (b, 0, 0)
    const2 = lambda b, j: (0, 0)
    in_specs = [
        pl.BlockSpec((1, D), const2),
        pl.BlockSpec((1, 1, D), per_b),
        pl.BlockSpec((1, 1, D), per_b),
        pl.BlockSpec((D, LANES), const2),
        pl.BlockSpec((1, LANES), const2),
    ]
    out_shape = (
        jax.ShapeDtypeStruct((B, S, D // 2), jnp.int32),
        jax.ShapeDtypeStruct((B, S, LANES), F32),
        jax.ShapeDtypeStruct((1, LANES), F32),
    )
    out_specs = (
        pl.BlockSpec((1, tm, D // 2), row),
        pl.BlockSpec((1, tm, LANES), row),
        pl.BlockSpec((1, LANES), const2),
    )
    return in_specs, out_shape, out_specs, [pltpu.VMEM((1, LANES), F32)]


def _experts_kernel(blk_e_ref, n_used_ref, next_e_ref, x_ref, wg_hbm, wu_hbm, wd_hbm, o_ref,
                    wg_f, wu_f, wd_f, wg_s, wu_s, wd_s, sem, *, layer):
    i = pl.program_id(0)
    used = i < n_used_ref[0]
    e = blk_e_ref[i]
    new_expert = jnp.logical_or(i == 0, e != blk_e_ref[jnp.maximum(i - 1, 0)])

    def fetch(expert):
        return (pltpu.make_async_copy(wg_hbm.at[layer, expert], wg_f, sem.at[0]),
                pltpu.make_async_copy(wu_hbm.at[layer, expert], wu_f, sem.at[1]),
                pltpu.make_async_copy(wd_hbm.at[layer, expert], wd_f, sem.at[2]))

    @pl.when(i == 0)
    def _():
        for cp in fetch(e):
            cp.start()

    @pl.when(jnp.logical_and(used, new_expert))
    def _():
        for cp in fetch(e):
            cp.wait()
        wg_s[...] = wg_f[...].astype(BF16)
        wu_s[...] = wu_f[...].astype(BF16)
        wd_s[...] = wd_f[...].astype(BF16)

        @pl.when(next_e_ref[i] >= 0)
        def _():
            for cp in fetch(next_e_ref[i]):
                cp.start()

    @pl.when(used)
    def _():
        x = _unpack_bf16_pairs(x_ref[...]).astype(BF16)
        a = _dot(x, wg_s[...])
        u = _dot(x, wu_s[...])
        act = a * jax.nn.sigmoid(a) * u
        o_ref[...] = _pack_bf16_pairs(_dot(act.astype(BF16), wd_s[...]))

    @pl.when(i >= n_used_ref[0])
    def _():
        o_ref[...] = jnp.zeros_like(o_ref)


def _experts(layer, blk_e, n_used, next_e, xs, w_gate, w_up, w_down):
    R = xs.shape[0]
    D = 2 * xs.shape[1]
    n_blk = R // MOE_BLOCK
    rows = lambda i, be, nu, ne: (i, 0)
    grid_spec = pltpu.PrefetchScalarGridSpec(
        num_scalar_prefetch=3,
        grid=(n_blk,),
        in_specs=[
            pl.BlockSpec((MOE_BLOCK, D // 2), rows),
            pl.BlockSpec(memory_space=pl.ANY),
            pl.BlockSpec(memory_space=pl.ANY),
            pl.BlockSpec(memory_space=pl.ANY),
        ],
        out_specs=pl.BlockSpec((MOE_BLOCK, D // 2), rows),
        scratch_shapes=[
            pltpu.VMEM((D, D_EXPERT), F32),
            pltpu.VMEM((D, D_EXPERT), F32),
            pltpu.VMEM((D_EXPERT, D), F32),
            pltpu.VMEM((D, D_EXPERT), BF16),
            pltpu.VMEM((D, D_EXPERT), BF16),
            pltpu.VMEM((D_EXPERT, D), BF16),
            pltpu.SemaphoreType.DMA((3,)),
        ],
    )
    return pl.pallas_call(
        functools.partial(_experts_kernel, layer=layer),
        out_shape=jax.ShapeDtypeStruct((R, D // 2), jnp.int32),
        grid_spec=grid_spec,
        compiler_params=_cparams("arbitrary"),
        name="moe_experts",
    )(blk_e, n_used, next_e, xs, w_gate, w_up, w_down)


def _combine_kernel(x_ref, g_ref, y0_ref, y1_ref, r_ref, o_ref):
    r = r_ref[0]
    y = _unpack_bf16_pairs(y0_ref[0, 0]) * r[:, 2:3] + _unpack_bf16_pairs(y1_ref[0, 0]) * r[:, 3:4]
    o_ref[0] = x_ref[0] + g_ref[0] * y


def _combine(x, g2, y01, route):
    B, S, D = x.shape
    tm = TOKEN_TILE
    row = lambda b, j: (b, j, 0)
    per_b = lambda b, j: (b, 0, 0)
    return pl.pallas_call(
        _combine_kernel,
        out_shape=jax.ShapeDtypeStruct((B, S, D), F32),
        grid=(B, S // tm),
        in_specs=[
            pl.BlockSpec((1, tm, D), row),
            pl.BlockSpec((1, 1, D), per_b),
            pl.BlockSpec((1, 1, tm, D // 2), lambda b, j: (0, b, j, 0)),
            pl.BlockSpec((1, 1, tm, D // 2), lambda b, j: (1, b, j, 0)),
            pl.BlockSpec((1, tm, LANES), row),
        ],
        out_specs=pl.BlockSpec((1, tm, D), row),
        compiler_params=_cparams("parallel", "parallel"),
        name="moe_combine",
    )(x, g2, y01, y01, route)


SC_CORES = 2
SC_SUBCORES = 16
SC_WORKERS = SC_CORES * SC_SUBCORES
SC_CHUNK = 64


def _sc_mesh():
    return plsc.VectorSubcoreMesh(core_axis_name="c", subcore_axis_name="s",
                                  num_cores=SC_CORES, num_subcores=SC_SUBCORES)


def _sc_scatter_rows(src, idx, n_out):
    T, W = src.shape
    per_w = T // SC_WORKERS
    nch = per_w // SC_CHUNK
    idx4 = idx.reshape(TOP_K, SC_WORKERS, nch, SC_CHUNK)

    @functools.partial(
        pl.kernel, mesh=_sc_mesh(),
        out_type=jax.ShapeDtypeStruct((n_out, W), src.dtype),
        scratch_types=[pltpu.VMEM((TOP_K, nch, SC_CHUNK), jnp.int32), pltpu.VMEM((SC_CHUNK, W), src.dtype)],
        name="sc_scatter_rows",
    )
    def body(src_hbm, idx_hbm, out_hbm, idx_v, rows_v):
        wid = lax.axis_index("s") * SC_CORES + lax.axis_index("c")
        for s in range(TOP_K):
            pltpu.sync_copy(idx_hbm.at[s, wid], idx_v.at[s])

        @pl.loop(0, nch)
        def _(i):
            pltpu.sync_copy(src_hbm.at[pl.ds(wid * per_w + i * SC_CHUNK, SC_CHUNK)], rows_v)
            for s in range(TOP_K):
                pltpu.sync_copy(rows_v, out_hbm.at[idx_v.at[s, i]])

    return body(src, idx4)


def _sc_gather_rows(table, idx):
    N = idx.shape[0]
    W = table.shape[1]
    per_w = N // SC_WORKERS
    nch = per_w // SC_CHUNK
    idx3 = idx.reshape(SC_WORKERS, nch, SC_CHUNK)

    @functools.partial(
        pl.kernel, mesh=_sc_mesh(),
        out_type=jax.ShapeDtypeStruct((N, W), table.dtype),
        scratch_types=[
            pltpu.VMEM((nch, SC_CHUNK), jnp.int32),
            pltpu.VMEM((2, SC_CHUNK, W), table.dtype),
            pltpu.SemaphoreType.DMA((2,)),
            pltpu.SemaphoreType.DMA((2,)),
        ],
        name="sc_gather_rows",
    )
    def body(table_hbm, idx_hbm, out_hbm, idx_v, rows_v, gather_sem, write_sem):
        wid = lax.axis_index("s") * SC_CORES + lax.axis_index("c")
        pltpu.sync_copy(idx_hbm.at[wid], idx_v)

        def gather(j, b):
            return pltpu.make_async_copy(table_hbm.at[idx_v.at[j]], rows_v.at[b], gather_sem.at[b])

        def write(j, b):
            return pltpu.make_async_copy(
                rows_v.at[b], out_hbm.at[pl.ds(wid * per_w + j * SC_CHUNK, SC_CHUNK)], write_sem.at[b])

        gather(0, 0).start()

        @pl.loop(0, nch, step=2)
        def _(i):
            for b in range(2):
                j = i + b

                @pl.when(j >= 1)
                def _():
                    write(j - 1, 1 - b).wait()

                @pl.when(j + 1 < nch)
                def _():
                    gather(j + 1, 1 - b).start()

                gather(j, b).wait()
                write(j, b).start()

        write(nch - 1, (nch - 1) % 2).wait()

    assert nch % 2 == 0
    return body(table, idx3)


def _moe_dispatch(route, counts, T):
    A = T * TOP_K
    counts = counts[0, :N_EXPERTS].astype(jnp.int32)
    blocks_per = (counts + MOE_BLOCK - 1) // MOE_BLOCK
    block_end = jnp.cumsum(blocks_per)
    block_start = block_end - blocks_per
    expert = route[:, :TOP_K].astype(jnp.int32)
    rank = route[:, 4:4 + TOP_K].astype(jnp.int32)
    onehot = expert[:, :, None] == jnp.arange(N_EXPERTS, dtype=jnp.int32)
    start = jnp.sum(jnp.where(onehot, block_start, 0), axis=-1)
    dest = (start * MOE_BLOCK + rank).T
    n_blk = -(-A // MOE_BLOCK) + N_EXPERTS
    blk = jnp.arange(n_blk, dtype=jnp.int32)
    blk_e = jnp.minimum(jnp.sum(blk[:, None] >= block_end[None, :], axis=-1), N_EXPERTS - 1).astype(jnp.int32)
    n_used = block_end[-1]
    first = jnp.logical_and(blk < n_used, jnp.logical_or(blk == 0, blk_e != jnp.roll(blk_e, 1)))
    first_pos = jnp.where(first, blk, n_blk)
    next_pos = jnp.concatenate([lax.cummin(first_pos, axis=0, reverse=True)[1:], jnp.full((1,), n_blk, jnp.int32)])
    next_e = jnp.where(next_pos < n_blk, blk_e[jnp.minimum(next_pos, n_blk - 1)], -1).astype(jnp.int32)
    return dest, n_blk * MOE_BLOCK, blk_e, n_used.reshape(1).astype(jnp.int32), next_e


def _rope_tables(S):
    inv = 1.0 / (ROPE_THETA ** (jnp.arange(0, HEAD_DIM, 2, dtype=F32) / HEAD_DIM))
    ang = jnp.arange(S, dtype=F32)[:, None] * inv[None, :]
    cos, sin = jnp.cos(ang), jnp.sin(ang)
    cos_h = jnp.concatenate([cos, cos], axis=-1)
    sin_h = jnp.concatenate([-sin, sin], axis=-1)
    return jnp.tile(cos_h, (1, ATTN_HEADS)), jnp.tile(sin_h, (1, ATTN_HEADS))


def _pad_cols(w, width):
    return jnp.pad(w, ((0, 0), (0, width - w.shape[1])))


def kernel(x, c, ada_w, ada_b, norm_mix, norm_ffn, hy_w_in, hy_q_norm, hy_k_norm, hy_conv_w, hy_w_out, ml_w_in, ml_b_gates, ml_out_norm, ml_w_out, moe_w_group, moe_b_group, moe_w_expert, moe_b_expert, moe_w_gate, moe_w_up, moe_w_down):
    B, S, D = x.shape
    T = B * S
    cos_t, sin_t = _rope_tables(S)
    mod = _ada_modulation(c, ada_w, ada_b).reshape(DEPTH, B, 6, 1, D)
    r_i = np.arange(ATTN_WIDTH)
    grp = jnp.asarray((r_i[:, None] // HEAD_DIM) == (r_i[None, :] // HEAD_DIM), dtype=BF16)

    for l in range(DEPTH):
        sh1, sc1, g1, sh2, sc2, g2 = [mod[l, :, i] for i in range(6)]
        gain1 = norm_mix[l].reshape(1, D)
        w_r = _pad_cols(jnp.concatenate([moe_w_group[l], moe_w_expert[l]], axis=1), LANES).astype(BF16)
        b_r = jnp.pad(jnp.concatenate([moe_b_group[l], moe_b_expert[l]]), (0, LANES - N_GROUPS - N_EXPERTS))
        route_args = (norm_ffn[l].reshape(1, D), sc2, sh2, w_r, b_r.reshape(1, LANES))
        j = l // 2
        if l % 2 == 0:
            w = hy_w_in[j]
            o = np.cumsum((0,) + (ATTN_WIDTH, HEAD_DIM, HEAD_DIM, IDX_HEADS * IDX_DIM, IDX_DIM, IDX_HEADS,
                                  CONV_WIDTH, CONV_WIDTH, CONV_WIDTH))
            wq, wk, wv, wiq, wik, wiw, wbg, wcg, wu = [w[:, o[i]:o[i + 1]] for i in range(9)]
            w_pad = jnp.concatenate(
                [wq, wiq, wbg, wcg, wu, wk, wv, _pad_cols(jnp.concatenate([wik, wiw], axis=1), LANES)],
                axis=1).astype(BF16)
            qn_t = jnp.tile(hy_q_norm[j], ATTN_HEADS).reshape(1, ATTN_WIDTH)
            kn_t = jnp.tile(hy_k_norm[j], LANES // HEAD_DIM).reshape(1, LANES)
            qt, iqt, bcu, kv, kvt, ik, iwt = _hyb_in(x, gain1, sc1, sh1, w_pad, cos_t, sin_t, qn_t, kn_t, grp)
            y_attn = _dsa_attention_t(qt, iqt, iwt, kv, kvt, ik)
            x, h2, route, counts = _hyb_out(y_attn, bcu, hy_conv_w[j], hy_w_out[j].astype(BF16), x, g1, route_args)
        else:
            w = ml_w_in[j]
            hq = ML_HEADS * ML_QK_DIM
            hv = ML_HEADS * ML_V_DIM
            wq, wk, wv = w[:, :hq], w[:, hq:2 * hq], w[:, 2 * hq:2 * hq + hv]
            wg = w[:, 2 * hq + hv:2 * hq + hv + 2 * ML_HEADS]
            wo = w[:, 2 * hq + hv + 2 * ML_HEADS:]
            w_pad = jnp.concatenate([wq, wk, wv, wo, _pad_cols(wg, LANES)], axis=1).astype(BF16)
            gate_bias = jnp.pad(ml_b_gates[j], (0, LANES - 2 * ML_HEADS)).reshape(1, LANES)
            q, k, v, og, gates = _ml_in(x, gain1, sc1, sh1, w_pad, gate_bias)
            nc = S // ML_CHUNK
            hp = ML_STEP_HEADS
            groups = ML_HEADS // hp
            gi = gates[:, :, :ML_HEADS].reshape(B, S, groups, hp)
            gf = gates[:, :, ML_HEADS:2 * ML_HEADS].reshape(B, S, groups, hp)
            gp = jnp.concatenate([gi, gf], axis=-1)
            gp = jnp.transpose(gp, (0, 2, 1, 3)).reshape(B, groups, nc, ML_CHUNK, 2 * hp)
            g_cols = jnp.pad(gp, ((0, 0),) * 4 + ((0, LANES - 2 * hp),))
            g_rows = jnp.swapaxes(gp, -1, -2)
            hh = _mlstm(q, k, v, g_rows, g_cols, ml_out_norm[j].reshape(1, hv))
            x, h2, route, counts = _ml_out(hh, og, ml_w_out[j].astype(BF16), x, g1, route_args)

        dest, n_rows, blk_e, n_used, next_e = _moe_dispatch(route.reshape(T, LANES), counts, T)
        xs = _sc_scatter_rows(h2.reshape(T, D // 2), dest, n_rows)
        ys = _experts(l, blk_e, n_used, next_e, xs, moe_w_gate, moe_w_up, moe_w_down)
        y01 = _sc_gather_rows(ys, dest.reshape(TOP_K * T)).reshape(TOP_K, B, S, D // 2)
        x = _combine(x, g2, y01, route)
    return x
```

```python
import functools

import numpy as np
import jax
import jax.numpy as jnp
from jax import lax
from jax.experimental import pallas as pl
from jax.experimental.pallas import tpu as pltpu
from jax.experimental.pallas import tpu_sc as plsc

F32 = jnp.float32
BF16 = jnp.bfloat16
HIGHEST = lax.Precision.HIGHEST

D_MODEL = 1024
DEPTH = 4
ATTN_HEADS = 8
HEAD_DIM = 64
ATTN_WIDTH = ATTN_HEADS * HEAD_DIM
IDX_HEADS = 8
IDX_DIM = 64
INDEX_TOPK = 256
Q_BLOCK = 256
ROPE_THETA = 10000.0
CONV_WIDTH = D_MODEL - ATTN_WIDTH
CONV_K = 3
ML_HEADS = 8
ML_QK_DIM = 64
ML_V_DIM = 128
N_GROUPS = 4
EXPERTS_PER_GROUP = 8
N_EXPERTS = N_GROUPS * EXPERTS_PER_GROUP
TOP_K = 2
D_EXPERT = 512
MOE_BLOCK = 512
NORM_EPS = 1e-6

LANES = 128
VMEM_LIMIT = 56 * 1024 * 1024
TOKEN_TILE = 256
ML_CHUNK = 256
ML_STEP_HEADS = 8
NEG_INF = float("-inf")


def _cparams(*sem):
    return pltpu.CompilerParams(dimension_semantics=sem, vmem_limit_bytes=VMEM_LIMIT)


def _dot(a, b):
    return jnp.dot(a, b, preferred_element_type=F32)


def _pack_bf16_pairs(x):
    bits = lax.bitcast_convert_type(x.astype(BF16).astype(F32), jnp.uint32)
    half = bits.shape[1] // 2
    packed = (bits[:, :half] >> 16) | (bits[:, half:] & jnp.uint32(0xFFFF0000))
    return lax.bitcast_convert_type(packed, jnp.int32)


def _unpack_bf16_pairs(words):
    words = lax.bitcast_convert_type(words, jnp.uint32)
    return jnp.concatenate(
        [lax.bitcast_convert_type(words << 16, F32),
         lax.bitcast_convert_type(words & jnp.uint32(0xFFFF0000), F32)], axis=1)


def _split_dot(a_f32, b_bf16):
    hi = a_f32.astype(BF16)
    lo = (a_f32 - hi.astype(F32)).astype(BF16)
    return _dot(hi, b_bf16) + _dot(lo, b_bf16)


def _ada_kernel(c_ref, w_ref, b_ref, o_ref):
    c = c_ref[...]
    ca = c * jax.nn.sigmoid(c)
    o_ref[0] = jnp.dot(ca, w_ref[0], precision=HIGHEST, preferred_element_type=F32) + b_ref[0]


def _ada_modulation(c, ada_w, ada_b):
    B, D = c.shape
    n_col = ada_w.shape[-1] // D
    return pl.pallas_call(
        _ada_kernel,
        out_shape=jax.ShapeDtypeStruct((DEPTH, B, n_col * D), F32),
        grid=(DEPTH, n_col),
        in_specs=[
            pl.BlockSpec((B, D), lambda l, j: (0, 0)),
            pl.BlockSpec((1, D, D), lambda l, j: (l, 0, j)),
            pl.BlockSpec((1, 1, D), lambda l, j: (l, 0, j)),
        ],
        out_specs=pl.BlockSpec((1, B, D), lambda l, j: (l, 0, j)),
        compiler_params=_cparams("parallel", "parallel"),
        name="ada_modulation",
    )(c, ada_w, ada_b.reshape(DEPTH, 1, n_col * D))


def _modulated_norm(x, gain, scale, shift):
    y = x * lax.rsqrt(jnp.mean(x * x, axis=-1, keepdims=True) + NORM_EPS)
    return y * gain * (1.0 + scale) + shift


def _rope(x, cos, sin_signed, first_half):
    w = x.shape[-1]
    partner = jnp.where(first_half, pltpu.roll(x, w - HEAD_DIM // 2, 1), pltpu.roll(x, HEAD_DIM // 2, 1))
    return x * cos + partner * sin_signed


HYB_COLS = 5 * 512 + 2 * LANES


def _hyb_in_kernel(x_ref, gain_ref, sc_ref, sh_ref, w_ref, cos_ref, sin_ref, qn_ref, kn_ref, grp_ref,
                   qt_ref, iqt_ref, bcu_ref, kv_ref, kvt_ref, ik_ref, iwt_ref):
    h = _modulated_norm(x_ref[0], gain_ref[...], sc_ref[0], sh_ref[0])
    p = _dot(h.astype(BF16), w_ref[...])
    cos = cos_ref[...]
    sin = sin_ref[...]
    lane = lax.broadcasted_iota(jnp.int32, (1, ATTN_WIDTH), 1)
    first_half = (lane % HEAD_DIM) < (HEAD_DIM // 2)
    fh128 = first_half[:, :LANES]
    lane128 = lane[:, :LANES]

    q = p[:, 0:512]
    ms = _split_dot(q * q, grp_ref[...]) * (1.0 / HEAD_DIM)
    q = q * lax.rsqrt(ms + NORM_EPS) * qn_ref[...]
    qt_ref[0] = (_rope(q, cos, sin, first_half) * (HEAD_DIM ** -0.5)).T.astype(BF16)

    iq = p[:, 512:1024]
    iqt_ref[0] = (_rope(iq, cos, sin, first_half) * (IDX_DIM ** -0.5)).T.astype(BF16)

    bcu_ref[0] = p[:, 1024:2560]

    kv = p[:, 2560:2688]
    is_k = lane128 < HEAD_DIM
    kk = jnp.where(is_k, kv, 0.0)
    ms_k = jnp.sum(kk * kk, axis=-1, keepdims=True) * (1.0 / HEAD_DIM)
    kn = kv * lax.rsqrt(ms_k + NORM_EPS) * kn_ref[...]
    kr = _rope(kn, cos[:, :LANES], sin[:, :LANES], fh128)
    kv = jnp.where(is_k, kr, kv)
    kv_ref[0] = kv.astype(BF16)
    kvt_ref[0, 0] = kv.T.astype(BF16)

    sm = p[:, 2688:2816]
    ikr = _rope(sm, cos[:, :LANES], sin[:, :LANES], fh128)
    ik_ref[0] = jnp.where(is_k, ikr, 0.0).astype(BF16)
    iwt_ref[0] = sm.T


def _hyb_in(x, gain, sc, sh, w_pad, cos_t, sin_t, qn_t, kn_t, grp):
    B, S, D = x.shape
    tm = TOKEN_TILE
    row = lambda b, j: (b, j, 0)
    per_b = lambda b, j: (b, 0, 0)
    const2 = lambda b, j: (0, 0)
    tab = lambda b, j: (j, 0)
    col = lambda b, j: (b, 0, j)
    assert tm == DSA_KEY_CHUNK
    return pl.pallas_call(
        _hyb_in_kernel,
        out_shape=(
            jax.ShapeDtypeStruct((B, 512, S), BF16),
            jax.ShapeDtypeStruct((B, 512, S), BF16),
            jax.ShapeDtypeStruct((B, S, 1536), F32),
            jax.ShapeDtypeStruct((B, S, LANES), BF16),
            jax.ShapeDtypeStruct((B, S // tm, LANES, tm), BF16),
            jax.ShapeDtypeStruct((B, S, LANES), BF16),
            jax.ShapeDtypeStruct((B, LANES, S), F32),
        ),
        grid=(B, S // tm),
        in_specs=[
            pl.BlockSpec((1, tm, D), row),
            pl.BlockSpec((1, D), const2),
            pl.BlockSpec((1, 1, D), per_b),
            pl.BlockSpec((1, 1, D), per_b),
            pl.BlockSpec((D, HYB_COLS), const2),
            pl.BlockSpec((tm, 512), tab),
            pl.BlockSpec((tm, 512), tab),
            pl.BlockSpec((1, 512), const2),
            pl.BlockSpec((1, LANES), const2),
            pl.BlockSpec((512, 512), const2),
        ],
        out_specs=(
            pl.BlockSpec((1, 512, tm), col),
            pl.BlockSpec((1, 512, tm), col),
            pl.BlockSpec((1, tm, 1536), row),
            pl.BlockSpec((1, tm, LANES), row),
            pl.BlockSpec((1, 1, LANES, tm), lambda b, j: (b, j, 0, 0)),
            pl.BlockSpec((1, tm, LANES), row),
            pl.BlockSpec((1, LANES, tm), col),
        ),
        compiler_params=_cparams("parallel", "parallel"),
        name="hybrid_in_proj",
    )(x, gain, sc, sh, w_pad, cos_t, sin_t, qn_t, kn_t, grp)


DSA_KEY_CHUNK = 256
F32_LOWEST = float(np.finfo(np.float32).min)


def _fold8(x, op):
    parts = x.reshape(x.shape[0] // 8, 8, x.shape[1])
    while parts.shape[0] > 1:
        half = parts.shape[0] // 2
        assert parts.shape[0] == 2 * half
        parts = op(parts[:half], parts[half:])
    return parts[0]


def _col_reduce(x, op):
    t = _fold8(x, op)
    for shift in (4, 2, 1):
        t = op(t, pltpu.roll(t, shift, 0))
    return t[0:1, :]


def _dsa_t_kernel(qt_ref, iqt_ref, iwt_ref, kv_ref, kvt_ref, ik_ref, o_ref, sc_ref, bias_ref, acc_ref):
    CK = DSA_KEY_CHUNK
    QB = Q_BLOCK
    qb = pl.program_id(1)
    nk = (qb * QB + QB + CK - 1) // CK
    kf = float(INDEX_TOPK)
    qpos = qb * QB + lax.broadcasted_iota(jnp.int32, (1, QB), 1)
    krow = lax.broadcasted_iota(jnp.int32, (CK, 1), 0)
    w_idx = iwt_ref[0, IDX_DIM:IDX_DIM + IDX_HEADS, :] * (IDX_HEADS ** -0.5)

    def rows(c):
        return pl.ds(pl.multiple_of(c * CK, CK), CK)

    def heads_on_lanes(ref, width):
        return jnp.concatenate([ref[0, hd * width:(hd + 1) * width, :] for hd in range(ref.shape[1] // width)], axis=1)

    def head_lanes(hd):
        return slice(hd * QB, (hd + 1) * QB)

    iq_wide = heads_on_lanes(iqt_ref, IDX_DIM)
    w_wide = jnp.concatenate([w_idx[hd:hd + 1, :] for hd in range(IDX_HEADS)], axis=1)

    def score_chunk(c, carry):
        mx, mn = carry
        ikc = ik_ref[0, rows(c), :][:, :IDX_DIM]
        s_all = jnp.maximum(_dot(ikc, iq_wide), 0.0) * w_wide
        acc = s_all[:, head_lanes(0)]
        for hd in range(1, IDX_HEADS):
            acc = acc + s_all[:, head_lanes(hd)]
        causal = (c * CK + krow) <= qpos
        sc_ref[rows(c), :] = jnp.where(causal, acc, NEG_INF)
        mx = jnp.maximum(mx, _fold8(jnp.where(causal, acc, NEG_INF), jnp.maximum))
        mn = jnp.minimum(mn, _fold8(jnp.where(causal, acc, jnp.inf), jnp.minimum))
        return mx, mn

    mx8, mn8 = lax.fori_loop(0, nk, score_chunk,
                             (jnp.full((8, QB), NEG_INF, F32), jnp.full((8, QB), jnp.inf, F32)))
    row_max = jnp.max(mx8, axis=0, keepdims=True)
    row_min = jnp.min(mn8, axis=0, keepdims=True)

    @pl.when(nk % 2 == 1)
    def _():
        sc_ref[rows(nk), :] = jnp.full((CK, QB), NEG_INF, F32)

    n_pairs = (nk + 1) // 2

    def pair_rows(c):
        return pl.ds(pl.multiple_of(c * (2 * CK), 2 * CK), 2 * CK)

    def count(pred):
        def body(c, part):
            return part + _fold8(jnp.where(pred(sc_ref[pair_rows(c), :]), 1.0, 0.0), jnp.add)
        part = lax.fori_loop(0, n_pairs, body, jnp.zeros((8, QB), F32))
        return jnp.sum(part, axis=0, keepdims=True)

    @pl.when(qb * QB + QB <= INDEX_TOPK)
    def _():
        def body(c, carry):
            bias_ref[rows(c), :] = jnp.where(sc_ref[rows(c), :] > NEG_INF, 0.0, NEG_INF)
            return carry
        lax.fori_loop(0, nk, body, 0)

    @pl.when(qb * QB + QB > INDEX_TOPK)
    def _():
        top_tied = count(lambda x: x >= row_max) >= kf

        def bisect(_, carry):
            lo, hi = carry
            mid = 0.5 * lo + 0.5 * hi
            ge = count(lambda x: x >= mid) >= kf
            return jnp.where(ge, mid, lo), jnp.where(ge, hi, mid)

        lo, hi = lax.fori_loop(0, 18, bisect, (row_min, row_max))

        def refine_cond(carry):
            it, _, _, done = carry
            return jnp.logical_and(it < nk * CK, jnp.min(done) < 0.5)

        def refine(carry):
            it, hi, thr, done = carry

            def below(c, part):
                x = sc_ref[pair_rows(c), :]
                return jnp.maximum(part, _fold8(jnp.where(x < hi, x, NEG_INF), jnp.maximum))

            m = jnp.max(lax.fori_loop(0, n_pairs, below, jnp.full((8, QB), NEG_INF, F32)), axis=0, keepdims=True)
            hit = count(lambda x: x >= m) >= kf
            fin = done > 0.5
            thr = jnp.where(fin, thr, m)
            hi = jnp.where(jnp.logical_or(fin, hit), hi, m)
            done = jnp.where(hit, 1.0, done)
            return it + 1, hi, thr, done

        done0 = jnp.where(top_tied, 1.0, 0.0)
        _, _, thr, _ = lax.while_loop(refine_cond, refine, (jnp.int32(0), hi, row_max, done0))

        need = kf - count(lambda x: x > thr)
        n_eq = count(lambda x: x == thr)
        tied = jnp.max(n_eq - need) > 0.5

        @pl.when(jnp.logical_not(tied))
        def _():
            def body(c, carry):
                bias_ref[rows(c), :] = jnp.where(sc_ref[rows(c), :] >= thr, 0.0, NEG_INF)
                return carry
            lax.fori_loop(0, nk, body, 0)

        @pl.when(tied)
        def _():
            r_i = lax.broadcasted_iota(jnp.int32, (CK, CK), 0)
            c_i = lax.broadcasted_iota(jnp.int32, (CK, CK), 1)
            lower = jnp.where(c_i <= r_i, 1.0, 0.0).astype(BF16)

            def body(c, seen):
                x = sc_ref[rows(c), :]
                eq = x == thr
                eq_f = jnp.where(eq, 1.0, 0.0)
                rank = _dot(lower, eq_f.astype(BF16)) + seen
                keep = jnp.logical_or(x > thr, jnp.logical_and(eq, rank <= need))
                bias_ref[rows(c), :] = jnp.where(keep, 0.0, NEG_INF)
                return seen + jnp.sum(eq_f, axis=0, keepdims=True)

            lax.fori_loop(0, nk, body, jnp.zeros((1, QB), F32))

    acc_ref[...] = jnp.zeros_like(acc_ref)
    q_wide = heads_on_lanes(qt_ref, HEAD_DIM)

    def attend_chunk(c, carry):
        m_old, l_old = carry
        kc = kv_ref[0, rows(c), :][:, :HEAD_DIM]
        vt = kvt_ref[0, c, HEAD_DIM:, :]
        bias = bias_ref[rows(c), :]
        logits = _dot(kc, q_wide) + jnp.concatenate([bias] * ATTN_HEADS, axis=1)
        m_new = jnp.maximum(m_old, _col_reduce(logits, jnp.maximum))
        alpha = jnp.exp(m_old - m_new)
        p = jnp.exp(logits - m_new)
        l_new = alpha * l_old + _col_reduce(p, jnp.add)
        acc_ref[...] = alpha * acc_ref[...] + _dot(vt, p.astype(BF16))
        return m_new, l_new

    _, l_all = lax.fori_loop(0, nk, attend_chunk, (jnp.full((1, ATTN_HEADS * QB), -1e30, F32),
                                                   jnp.zeros((1, ATTN_HEADS * QB), F32)))
    out_t = acc_ref[...] / l_all
    o_ref[0] = jnp.concatenate([out_t[:, head_lanes(hd)] for hd in range(ATTN_HEADS)], axis=0).T


def _dsa_attention_t(qt, iqt, iwt, kv, kvt, ik):
    B, _, S = qt.shape
    col = lambda b, j: (b, 0, j)
    per_b = lambda b, j: (b, 0, 0)
    return pl.pallas_call(
        _dsa_t_kernel,
        out_shape=jax.ShapeDtypeStruct((B, S, ATTN_WIDTH), F32),
        grid=(B, S // Q_BLOCK),
        in_specs=[
            pl.BlockSpec((1, ATTN_WIDTH, Q_BLOCK), col),
            pl.BlockSpec((1, IDX_HEADS * IDX_DIM, Q_BLOCK), col),
            pl.BlockSpec((1, LANES, Q_BLOCK), col),
            pl.BlockSpec((1, S, LANES), per_b),
            pl.BlockSpec((1, S // DSA_KEY_CHUNK, LANES, DSA_KEY_CHUNK), lambda b, j: (b, 0, 0, 0)),
            pl.BlockSpec((1, S, LANES), per_b),
        ],
        out_specs=pl.BlockSpec((1, Q_BLOCK, ATTN_WIDTH), lambda b, j: (b, j, 0)),
        scratch_shapes=[
            pltpu.VMEM((S, Q_BLOCK), F32),
            pltpu.VMEM((S, Q_BLOCK), F32),
            pltpu.VMEM((HEAD_DIM, ATTN_HEADS * Q_BLOCK), F32),
        ],
        compiler_params=_cparams("parallel", "parallel"),
        name="dsa_attention",
    )(qt, iqt, iwt, kv, kvt, ik)


def _hyb_out_kernel(ya_ref, bcu_ref, halo_ref, cw_ref, w_ref, x_ref, g_ref,
                    gain2_ref, sc2_ref, sh2_ref, wr_ref, br_ref, o_ref, h_ref, r_ref, rt_ref, cnt_ref, run_ref):
    j = pl.program_id(1)
    tm = ya_ref.shape[1]
    bcu = bcu_ref[0]
    bg = bcu[:, 0:512]
    z = bcu[:, 512:1024] * bcu[:, 1024:1536]
    halo = halo_ref[0]
    zh = halo[:, 512:1024] * halo[:, 1024:1536]
    zh = jnp.where(j > 0, zh, 0.0)
    row = lax.broadcasted_iota(jnp.int32, (tm, 1), 0)
    z1 = jnp.where(row >= 1, pltpu.roll(z, 1, 0), zh[7:8, :])
    z2 = jnp.where(row >= 2, pltpu.roll(z, 2, 0), jnp.where(row == 1, zh[7:8, :], zh[6:7, :]))
    cw = cw_ref[...]
    y_conv = bg * (z2 * cw[0:1, :] + z1 * cw[1:2, :] + z * cw[2:3, :])
    y = _dot(ya_ref[0].astype(BF16), w_ref[0:512, :]) + _dot(y_conv.astype(BF16), w_ref[512:1024, :])
    x_new = x_ref[0] + g_ref[0] * y
    o_ref[0] = x_new
    _route_tile(x_new, gain2_ref, sc2_ref, sh2_ref, wr_ref, br_ref, h_ref, r_ref, rt_ref, cnt_ref, run_ref)


def _hyb_out(y_attn, bcu, conv_w, w_out_bf, x, g1, route_args):
    B, S, D = x.shape
    tm = TOKEN_TILE
    row = lambda b, j: (b, j, 0)
    per_b = lambda b, j: (b, 0, 0)
    const2 = lambda b, j: (0, 0)
    halo = lambda b, j: (b, jnp.maximum(j * (tm // 8) - 1, 0), 0)
    r_in, r_shape, r_out, r_scratch = _route_specs(B, S, D, tm)
    return pl.pallas_call(
        _hyb_out_kernel,
        out_shape=(jax.ShapeDtypeStruct((B, S, D), F32),) + r_shape,
        grid=(B, S // tm),
        in_specs=[
            pl.BlockSpec((1, tm, 512), row),
            pl.BlockSpec((1, tm, 1536), row),
            pl.BlockSpec((1, 8, 1536), halo),
            pl.BlockSpec((CONV_K, CONV_WIDTH), const2),
            pl.BlockSpec((D, D), const2),
            pl.BlockSpec((1, tm, D), row),
            pl.BlockSpec((1, 1, D), per_b),
        ] + r_in,
        out_specs=(pl.BlockSpec((1, tm, D), row),) + r_out,
        scratch_shapes=r_scratch,
        compiler_params=_cparams("arbitrary", "arbitrary"),
        name="hybrid_out_proj",
    )(y_attn, bcu, bcu, conv_w, w_out_bf, x, g1, *route_args)


ML_COLS = 512 + 512 + 1024 + 1024 + LANES


def _ml_in_kernel(x_ref, gain_ref, sc_ref, sh_ref, w_ref, bias_ref, q_ref, k_ref, v_ref, og_ref, gt_ref, gtt_ref):
    h = _modulated_norm(x_ref[0], gain_ref[...], sc_ref[0], sh_ref[0])
    p = _dot(h.astype(BF16), w_ref[...])
    q_ref[0] = (p[:, 0:512] * (ML_QK_DIM ** -0.5)).astype(BF16)
    k_ref[0, 0] = p[:, 512:1024].T.astype(BF16)
    v_ref[0] = p[:, 1024:2048].astype(BF16)
    og_ref[0] = p[:, 2048:3072]
    gates = p[:, 3072:3200] + bias_ref[...]
    gt_ref[0, 0] = gates
    gtt_ref[0, 0] = gates.T[:2 * ML_HEADS, :]


def _ml_in(x, gain, sc, sh, w_pad, gate_bias):
    B, S, D = x.shape
    tm = TOKEN_TILE
    assert tm == ML_CHUNK
    row = lambda b, j: (b, j, 0)
    per_b = lambda b, j: (b, 0, 0)
    const2 = lambda b, j: (0, 0)
    return pl.pallas_call(
        _ml_in_kernel,
        out_shape=(
            jax.ShapeDtypeStruct((B, S, 512), BF16),
            jax.ShapeDtypeStruct((B, S // tm, 512, tm), BF16),
            jax.ShapeDtypeStruct((B, S, 1024), BF16),
            jax.ShapeDtypeStruct((B, S, 1024), F32),
            jax.ShapeDtypeStruct((B, S // tm, tm, LANES), F32),
            jax.ShapeDtypeStruct((B, S // tm, 2 * ML_HEADS, tm), F32),
        ),
        grid=(B, S // tm),
        in_specs=[
            pl.BlockSpec((1, tm, D), row),
            pl.BlockSpec((1, D), const2),
            pl.BlockSpec((1, 1, D), per_b),
            pl.BlockSpec((1, 1, D), per_b),
            pl.BlockSpec((D, ML_COLS), const2),
            pl.BlockSpec((1, LANES), const2),
        ],
        out_specs=(
            pl.BlockSpec((1, tm, 512), row),
            pl.BlockSpec((1, 1, 512, tm), lambda b, j: (b, j, 0, 0)),
            pl.BlockSpec((1, tm, 1024), row),
            pl.BlockSpec((1, tm, 1024), row),
            pl.BlockSpec((1, 1, tm, LANES), lambda b, j: (b, j, 0, 0)),
            pl.BlockSpec((1, 1, 2 * ML_HEADS, tm), lambda b, j: (b, j, 0, 0)),
        ),
        compiler_params=_cparams("parallel", "parallel"),
        name="mlstm_in_proj",
    )(x, gain, sc, sh, w_pad, gate_bias)


def _log_sigmoid(f):
    return jnp.minimum(f, 0.0) - jnp.log1p(jnp.exp(-jnp.abs(f)))


def _split3(x):
    a = x.astype(BF16)
    r = x - a.astype(F32)
    b = r.astype(BF16)
    c = (r - b.astype(F32)).astype(BF16)
    return a, b, c


def _twice(a):
    return jnp.concatenate([a, a], axis=1)


def _mlstm_kernel(q_ref, kt_ref, v_ref, grow_ref, gcol_ref, gain_ref, o_ref, c_ref, m_ref):
    L = ML_CHUNK
    HP = ML_STEP_HEADS
    S = q_ref.shape[1]
    c_ref[...] = jnp.zeros_like(c_ref)
    m_ref[...] = jnp.zeros_like(m_ref)

    def chunk(c, carry):
        r0 = pl.multiple_of(c * L, L)
        r_i = lax.broadcasted_iota(jnp.int32, (L, L), 0)
        c_i = lax.broadcasted_iota(jnp.int32, (L, L), 1)
        tril = c_i <= r_i
        lower = jnp.where(tril, 1.0, 0.0).astype(BF16)
        upper = jnp.where(r_i <= c_i, 1.0, 0.0).astype(BF16)
        e_r = lax.broadcasted_iota(jnp.int32, (LANES, HP * LANES), 0)
        e_c = lax.broadcasted_iota(jnp.int32, (LANES, HP * LANES), 1)
        pick = jnp.where(e_r == HP + e_c // LANES, 1.0, 0.0).astype(BF16)
        rows = grow_ref[0, 0, c]
        cols = gcol_ref[0, 0, c]
        b_rows = sum(_dot(p, upper) for p in _split3(_log_sigmoid(rows)))
        b_cols = sum(_dot(lower, p) for p in _split3(_log_sigmoid(cols)))
        b_colr = sum(_dot(p, pick) for p in _split3(b_cols))
        lane = lax.broadcasted_iota(jnp.int32, (1, L), 1)
        b_last_all = jnp.sum(jnp.where(lane == L - 1, b_rows, 0.0), axis=-1, keepdims=True)
        ones_v = jnp.ones((L, ML_V_DIM), BF16)
        for hh in range(HP):
            q = q_ref[0, pl.ds(r0, L), hh * ML_QK_DIM:(hh + 1) * ML_QK_DIM]
            kt = kt_ref[0, c, hh * ML_QK_DIM:(hh + 1) * ML_QK_DIM, :]
            v = v_ref[0, pl.ds(r0, L), hh * ML_V_DIM:(hh + 1) * ML_V_DIM]
            vx = jnp.concatenate([v, ones_v], axis=1)
            i_row = rows[hh:hh + 1, :]
            b_row = b_rows[HP + hh:HP + hh + 1, :]
            b_last = b_last_all[HP + hh:HP + hh + 1, :]
            b_col = b_colr[:, hh * LANES:(hh + 1) * LANES]
            m_prev = m_ref[hh]
            ctn = c_ref[hh]

            dmat = jnp.where(tril, _twice(b_col) - b_row + i_row, NEG_INF)
            inter = b_col + m_prev
            m_t = jnp.maximum(inter, jnp.max(dmat, axis=-1, keepdims=True))
            w_intra = jnp.exp(dmat - _twice(m_t))
            w_inter = jnp.exp(inter - m_t)
            intra = (w_intra * _dot(q, kt)).astype(BF16)
            tot = _twice(w_inter) * _dot(q, ctn.astype(BF16)) + _dot(intra, vx)
            num = tot[:, :ML_V_DIM]
            den = tot[:, ML_V_DIM:]
            hc = num / jnp.maximum(jnp.abs(den), jnp.exp(-m_t))
            y = hc * lax.rsqrt(jnp.mean(hc * hc, axis=-1, keepdims=True) + NORM_EPS)
            o_ref[0, pl.ds(r0, L), hh * ML_V_DIM:(hh + 1) * ML_V_DIM] = (
                y * gain_ref[:, hh * ML_V_DIM:(hh + 1) * ML_V_DIM])

            g_row = b_last - b_row + i_row
            m_new = jnp.maximum(b_last + m_prev, jnp.max(g_row, axis=-1, keepdims=True))
            decay = jnp.exp(b_last + m_prev - m_new)
            kw = (kt.astype(F32) * jnp.exp(g_row - _twice(m_new))).astype(BF16)
            c_ref[hh] = _twice(decay) * ctn + _dot(kw, vx)
            m_ref[hh] = m_new
        return carry

    lax.fori_loop(0, S // L, chunk, 0)


def _mlstm(q, kt, v, g_rows, g_cols, out_gain):
    B, S, _ = q.shape
    assert ML_CHUNK == 2 * LANES
    nc = S // ML_CHUNK
    hp = ML_STEP_HEADS
    return pl.pallas_call(
        _mlstm_kernel,
        out_shape=jax.ShapeDtypeStruct((B, S, ML_HEADS * ML_V_DIM), F32),
        grid=(B, ML_HEADS // hp),
        in_specs=[
            pl.BlockSpec((1, S, hp * ML_QK_DIM), lambda b, p: (b, 0, p)),
            pl.BlockSpec((1, nc, hp * ML_QK_DIM, ML_CHUNK), lambda b, p: (b, 0, p, 0)),
            pl.BlockSpec((1, S, hp * ML_V_DIM), lambda b, p: (b, 0, p)),
            pl.BlockSpec((1, 1, nc, 2 * hp, ML_CHUNK), lambda b, p: (b, p, 0, 0, 0)),
            pl.BlockSpec((1, 1, nc, ML_CHUNK, LANES), lambda b, p: (b, p, 0, 0, 0)),
            pl.BlockSpec((1, hp * ML_V_DIM), lambda b, p: (0, p)),
        ],
        out_specs=pl.BlockSpec((1, S, hp * ML_V_DIM), lambda b, p: (b, 0, p)),
        scratch_shapes=[
            pltpu.VMEM((hp, ML_QK_DIM, ML_V_DIM + LANES), F32),
            pltpu.VMEM((hp, 1, LANES), F32),
        ],
        compiler_params=_cparams("parallel", "parallel"),
        name="mlstm_chunkwise",
    )(q, kt, v, g_rows, g_cols, out_gain)


def _ml_out_kernel(hh_ref, og_ref, w_ref, x_ref, g_ref,
                   gain2_ref, sc2_ref, sh2_ref, wr_ref, br_ref, o_ref, h_ref, r_ref, rt_ref, cnt_ref, run_ref):
    a = jax.nn.sigmoid(og_ref[0]) * hh_ref[0]
    x_new = x_ref[0] + g_ref[0] * _dot(a.astype(BF16), w_ref[...])
    o_ref[0] = x_new
    _route_tile(x_new, gain2_ref, sc2_ref, sh2_ref, wr_ref, br_ref, h_ref, r_ref, rt_ref, cnt_ref, run_ref)


def _ml_out(hh, og, w_out_bf, x, g1, route_args):
    B, S, D = x.shape
    tm = TOKEN_TILE
    row = lambda b, j: (b, j, 0)
    per_b = lambda b, j: (b, 0, 0)
    const2 = lambda b, j: (0, 0)
    r_in, r_shape, r_out, r_scratch = _route_specs(B, S, D, tm)
    return pl.pallas_call(
        _ml_out_kernel,
        out_shape=(jax.ShapeDtypeStruct((B, S, D), F32),) + r_shape,
        grid=(B, S // tm),
        in_specs=[
            pl.BlockSpec((1, tm, D), row),
            pl.BlockSpec((1, tm, D), row),
            pl.BlockSpec((D, D), const2),
            pl.BlockSpec((1, tm, D), row),
            pl.BlockSpec((1, 1, D), per_b),
        ] + r_in,
        out_specs=(pl.BlockSpec((1, tm, D), row),) + r_out,
        scratch_shapes=r_scratch,
        compiler_params=_cparams("arbitrary", "arbitrary"),
        name="mlstm_out_proj",
    )(hh, og, w_out_bf, x, g1, *route_args)


def _first_argmax(x, lane, width):
    mx = jnp.max(x, axis=-1, keepdims=True)
    idx = jnp.min(jnp.where(x == mx, lane, width), axis=-1, keepdims=True)
    return mx, idx


def _route_tile(x, gain_ref, sc_ref, sh_ref, w_ref, b_ref, h_ref, r_ref, rt_ref, cnt_ref, run_ref):
    tm = x.shape[0]

    @pl.when(jnp.logical_and(pl.program_id(0) == 0, pl.program_id(1) == 0))
    def _():
        run_ref[...] = jnp.zeros_like(run_ref)

    h = _modulated_norm(x, gain_ref[...], sc_ref[0], sh_ref[0])
    h_ref[0] = _pack_bf16_pairs(h)
    logits = _dot(h.astype(BF16), w_ref[...]) + b_ref[...]
    lane = lax.broadcasted_iota(jnp.int32, (1, LANES), 1)
    lg = jnp.where(lane < N_GROUPS, logits, NEG_INF)
    g_max, g_sel = _first_argmax(lg, lane, LANES)
    pg = 1.0 / jnp.sum(jnp.exp(lg - g_max), axis=-1, keepdims=True)
    e_lane = lane - N_GROUPS
    in_grp = jnp.logical_and(e_lane >= g_sel * EXPERTS_PER_GROUP, e_lane < (g_sel + 1) * EXPERTS_PER_GROUP)
    le = jnp.where(in_grp, logits, NEG_INF)
    v1, i1 = _first_argmax(le, lane, LANES)
    le2 = jnp.where(lane == i1, NEG_INF, le)
    v2, i2 = _first_argmax(le2, lane, LANES)
    e2 = jnp.exp(v2 - v1)
    w1 = pg / (1.0 + e2)
    w2 = pg * e2 / (1.0 + e2)
    e1 = i1 - N_GROUPS
    e2 = i2 - N_GROUPS
    hot1 = lane == e1
    hot2 = lane == e2
    onehot = jnp.where(jnp.logical_or(hot1, hot2), 1.0, 0.0)
    r_i = lax.broadcasted_iota(jnp.int32, (tm, tm), 0)
    c_i = lax.broadcasted_iota(jnp.int32, (tm, tm), 1)
    before = jnp.where(c_i < r_i, 1.0, 0.0).astype(BF16)
    seen = _dot(before, onehot.astype(BF16)) + run_ref[...]
    rank1 = jnp.sum(jnp.where(hot1, seen, 0.0), axis=-1, keepdims=True)
    rank2 = jnp.sum(jnp.where(hot2, seen, 0.0), axis=-1, keepdims=True)
    run_ref[...] = run_ref[...] + jnp.sum(onehot, axis=0, keepdims=True)
    cnt_ref[...] = run_ref[...]

    out = jnp.where(lane == 0, e1.astype(F32), 0.0)
    out = jnp.where(lane == 1, e2.astype(F32), out)
    out = jnp.where(lane == 2, w1, out)
    out = jnp.where(lane == 3, w2, out)
    out = jnp.where(lane == 4, rank1, out)
    out = jnp.where(lane == 5, rank2, out)
    r_ref[0] = out
    rt_ref[0] = out.T[:8, :]


def _route_specs(B, S, D, tm):
    row = lambda b, j: (b, j, 0)
    per_b = lambda b, j: (b, 0, 0)
    const2 = lambda b, j: (0, 0)
    in_specs = [
        pl.BlockSpec((1, D), const2),
        pl.BlockSpec((1, 1, D), per_b),
        pl.BlockSpec((1, 1, D), per_b),
        pl.BlockSpec((D, LANES), const2),
        pl.BlockSpec((1, LANES), const2),
    ]
    out_shape = (
        jax.ShapeDtypeStruct((B, S, D // 2), jnp.int32),
        jax.ShapeDtypeStruct((B, S, LANES), F32),
        jax.ShapeDtypeStruct((B, 8, S), F32),
        jax.ShapeDtypeStruct((1, LANES), F32),
    )
    out_specs = (
        pl.BlockSpec((1, tm, D // 2), row),
        pl.BlockSpec((1, tm, LANES), row),
        pl.BlockSpec((1, 8, tm), lambda b, j: (b, 0, j)),
        pl.BlockSpec((1, LANES), const2),
    )
    return in_specs, out_shape, out_specs, [pltpu.VMEM((1, LANES), F32)]


def _experts_kernel(blk_e_ref, n_used_ref, next_e_ref, x_ref, wg_hbm, wu_hbm, wd_hbm, o_ref,
                    wg_f, wu_f, wd_f, wg_s, wu_s, wd_s, sem, *, layer):
    i = pl.program_id(0)
    used = i < n_used_ref[0]
    e = blk_e_ref[i]
    new_expert = jnp.logical_or(i == 0, e != blk_e_ref[jnp.maximum(i - 1, 0)])

    def fetch(expert):
        return (pltpu.make_async_copy(wg_hbm.at[layer, expert], wg_f, sem.at[0]),
                pltpu.make_async_copy(wu_hbm.at[layer, expert], wu_f, sem.at[1]),
                pltpu.make_async_copy(wd_hbm.at[layer, expert], wd_f, sem.at[2]))

    @pl.when(i == 0)
    def _():
        for cp in fetch(e):
            cp.start()

    @pl.when(jnp.logical_and(used, new_expert))
    def _():
        for cp in fetch(e):
            cp.wait()
        wg_s[...] = wg_f[...].astype(BF16)
        wu_s[...] = wu_f[...].astype(BF16)
        wd_s[...] = wd_f[...].astype(BF16)

        @pl.when(next_e_ref[i] >= 0)
        def _():
            for cp in fetch(next_e_ref[i]):
                cp.start()

    @pl.when(used)
    def _():
        x = _unpack_bf16_pairs(x_ref[...]).astype(BF16)
        a = _dot(x, wg_s[...])
        u = _dot(x, wu_s[...])
        act = a * jax.nn.sigmoid(a) * u
        o_ref[...] = _pack_bf16_pairs(_dot(act.astype(BF16), wd_s[...]))

    @pl.when(i >= n_used_ref[0])
    def _():
        o_ref[...] = jnp.zeros_like(o_ref)


def _experts(layer, blk_e, n_used, next_e, xs, w_gate, w_up, w_down):
    R = xs.shape[0]
    D = 2 * xs.shape[1]
    n_blk = R // MOE_BLOCK
    rows = lambda i, be, nu, ne: (i, 0)
    grid_spec = pltpu.PrefetchScalarGridSpec(
        num_scalar_prefetch=3,
        grid=(n_blk,),
        in_specs=[
            pl.BlockSpec((MOE_BLOCK, D // 2), rows),
            pl.BlockSpec(memory_space=pl.ANY),
            pl.BlockSpec(memory_space=pl.ANY),
            pl.BlockSpec(memory_space=pl.ANY),
        ],
        out_specs=pl.BlockSpec((MOE_BLOCK, D // 2), rows),
        scratch_shapes=[
            pltpu.VMEM((D, D_EXPERT), F32),
            pltpu.VMEM((D, D_EXPERT), F32),
            pltpu.VMEM((D_EXPERT, D), F32),
            pltpu.VMEM((D, D_EXPERT), BF16),
            pltpu.VMEM((D, D_EXPERT), BF16),
            pltpu.VMEM((D_EXPERT, D), BF16),
            pltpu.SemaphoreType.DMA((3,)),
        ],
    )
    return pl.pallas_call(
        functools.partial(_experts_kernel, layer=layer),
        out_shape=jax.ShapeDtypeStruct((R, D // 2), jnp.int32),
        grid_spec=grid_spec,
        compiler_params=_cparams("arbitrary"),
        name="moe_experts",
    )(blk_e, n_used, next_e, xs, w_gate, w_up, w_down)


def _combine_kernel(x_ref, g_ref, y0_ref, y1_ref, r_ref, o_ref):
    r = r_ref[0]
    y = _unpack_bf16_pairs(y0_ref[0, 0]) * r[:, 2:3] + _unpack_bf16_pairs(y1_ref[0, 0]) * r[:, 3:4]
    o_ref[0] = x_ref[0] + g_ref[0] * y


def _combine(x, g2, y01, route):
    B, S, D = x.shape
    tm = TOKEN_TILE
    row = lambda b, j: (b, j, 0)
    per_b = lambda b, j: (b, 0, 0)
    return pl.pallas_call(
        _combine_kernel,
        out_shape=jax.ShapeDtypeStruct((B, S, D), F32),
        grid=(B, S // tm),
        in_specs=[
            pl.BlockSpec((1, tm, D), row),
            pl.BlockSpec((1, 1, D), per_b),
            pl.BlockSpec((1, 1, tm, D // 2), lambda b, j: (0, b, j, 0)),
            pl.BlockSpec((1, 1, tm, D // 2), lambda b, j: (1, b, j, 0)),
            pl.BlockSpec((1, tm, LANES), row),
        ],
        out_specs=pl.BlockSpec((1, tm, D), row),
        compiler_params=_cparams("parallel", "parallel"),
        name="moe_combine",
    )(x, g2, y01, y01, route)


SC_CORES = 2
SC_SUBCORES = 16
SC_WORKERS = SC_CORES * SC_SUBCORES
SC_CHUNK = 64


def _sc_mesh():
    return plsc.VectorSubcoreMesh(core_axis_name="c", subcore_axis_name="s",
                                  num_cores=SC_CORES, num_subcores=SC_SUBCORES)


def _sc_scatter_rows(src, idx, n_out):
    T, W = src.shape
    per_w = T // SC_WORKERS
    nch = per_w // SC_CHUNK
    idx4 = idx.reshape(TOP_K, SC_WORKERS, nch, SC_CHUNK)

    @functools.partial(
        pl.kernel, mesh=_sc_mesh(),
        out_type=jax.ShapeDtypeStruct((n_out, W), src.dtype),
        scratch_types=[pltpu.VMEM((TOP_K, nch, SC_CHUNK), jnp.int32), pltpu.VMEM((SC_CHUNK, W), src.dtype)],
        name="sc_scatter_rows",
    )
    def body(src_hbm, idx_hbm, out_hbm, idx_v, rows_v):
        wid = lax.axis_index("s") * SC_CORES + lax.axis_index("c")
        for s in range(TOP_K):
            pltpu.sync_copy(idx_hbm.at[s, wid], idx_v.at[s])

        @pl.loop(0, nch)
        def _(i):
            pltpu.sync_copy(src_hbm.at[pl.ds(wid * per_w + i * SC_CHUNK, SC_CHUNK)], rows_v)
            for s in range(TOP_K):
                pltpu.sync_copy(rows_v, out_hbm.at[idx_v.at[s, i]])

    return body(src, idx4)


def _sc_gather_rows(table, idx):
    N = idx.shape[0]
    W = table.shape[1]
    per_w = N // SC_WORKERS
    nch = per_w // SC_CHUNK
    idx3 = idx.reshape(SC_WORKERS, nch, SC_CHUNK)

    @functools.partial(
        pl.kernel, mesh=_sc_mesh(),
        out_type=jax.ShapeDtypeStruct((N, W), table.dtype),
        scratch_types=[
            pltpu.VMEM((nch, SC_CHUNK), jnp.int32),
            pltpu.VMEM((2, SC_CHUNK, W), table.dtype),
            pltpu.SemaphoreType.DMA((2,)),
            pltpu.SemaphoreType.DMA((2,)),
        ],
        name="sc_gather_rows",
    )
    def body(table_hbm, idx_hbm, out_hbm, idx_v, rows_v, gather_sem, write_sem):
        wid = lax.axis_index("s") * SC_CORES + lax.axis_index("c")
        pltpu.sync_copy(idx_hbm.at[wid], idx_v)

        def gather(j, b):
            return pltpu.make_async_copy(table_hbm.at[idx_v.at[j]], rows_v.at[b], gather_sem.at[b])

        def write(j, b):
            return pltpu.make_async_copy(
                rows_v.at[b], out_hbm.at[pl.ds(wid * per_w + j * SC_CHUNK, SC_CHUNK)], write_sem.at[b])

        gather(0, 0).start()

        @pl.loop(0, nch, step=2)
        def _(i):
            for b in range(2):
                j = i + b

                @pl.when(j >= 1)
                def _():
                    write(j - 1, 1 - b).wait()

                @pl.when(j + 1 < nch)
                def _():
                    gather(j + 1, 1 - b).start()

                gather(j, b).wait()
                write(j, b).start()

        write(nch - 1, (nch - 1) % 2).wait()

    assert nch % 2 == 0
    return body(table, idx3)


def _moe_dispatch(route_t, counts, T):
    A = T * TOP_K
    counts = counts[0, :N_EXPERTS].astype(jnp.int32)
    blocks_per = (counts + MOE_BLOCK - 1) // MOE_BLOCK
    block_end = jnp.cumsum(blocks_per)
    block_start = block_end - blocks_per
    expert = jnp.swapaxes(route_t[:, :TOP_K, :], 0, 1).reshape(TOP_K, T).astype(jnp.int32)
    rank = jnp.swapaxes(route_t[:, 4:4 + TOP_K, :], 0, 1).reshape(TOP_K, T).astype(jnp.int32)
    onehot = expert[None] == jnp.arange(N_EXPERTS, dtype=jnp.int32)[:, None, None]
    start = jnp.sum(jnp.where(onehot, block_start[:, None, None], 0), axis=0)
    dest = start * MOE_BLOCK + rank
    n_blk = -(-A // MOE_BLOCK) + N_EXPERTS
    blk = jnp.arange(n_blk, dtype=jnp.int32)
    blk_e = jnp.minimum(jnp.sum(blk[:, None] >= block_end[None, :], axis=-1), N_EXPERTS - 1).astype(jnp.int32)
    n_used = block_end[-1]
    first = jnp.logical_and(blk < n_used, jnp.logical_or(blk == 0, blk_e != jnp.roll(blk_e, 1)))
    first_pos = jnp.where(first, blk, n_blk)
    next_pos = jnp.concatenate([lax.cummin(first_pos, axis=0, reverse=True)[1:], jnp.full((1,), n_blk, jnp.int32)])
    next_e = jnp.where(next_pos < n_blk, blk_e[jnp.minimum(next_pos, n_blk - 1)], -1).astype(jnp.int32)
    return dest, n_blk * MOE_BLOCK, blk_e, n_used.reshape(1).astype(jnp.int32), next_e


def _rope_tables(S):
    inv = 1.0 / (ROPE_THETA ** (jnp.arange(0, HEAD_DIM, 2, dtype=F32) / HEAD_DIM))
    ang = jnp.arange(S, dtype=F32)[:, None] * inv[None, :]
    cos, sin = jnp.cos(ang), jnp.sin(ang)
    cos_h = jnp.concatenate([cos, cos], axis=-1)
    sin_h = jnp.concatenate([-sin, sin], axis=-1)
    return jnp.tile(cos_h, (1, ATTN_HEADS)), jnp.tile(sin_h, (1, ATTN_HEADS))


def _pad_cols(w, width):
    return jnp.pad(w, ((0, 0), (0, width - w.shape[1])))


def kernel(x, c, ada_w, ada_b, norm_mix, norm_ffn, hy_w_in, hy_q_norm, hy_k_norm, hy_conv_w, hy_w_out, ml_w_in, ml_b_gates, ml_out_norm, ml_w_out, moe_w_group, moe_b_group, moe_w_expert, moe_b_expert, moe_w_gate, moe_w_up, moe_w_down):
    B, S, D = x.shape
    T = B * S
    cos_t, sin_t = _rope_tables(S)
    mod = _ada_modulation(c, ada_w, ada_b).reshape(DEPTH, B, 6, 1, D)
    r_i = np.arange(ATTN_WIDTH)
    grp = jnp.asarray((r_i[:, None] // HEAD_DIM) == (r_i[None, :] // HEAD_DIM), dtype=BF16)

    for l in range(DEPTH):
        sh1, sc1, g1, sh2, sc2, g2 = [mod[l, :, i] for i in range(6)]
        gain1 = norm_mix[l].reshape(1, D)
        w_r = _pad_cols(jnp.concatenate([moe_w_group[l], moe_w_expert[l]], axis=1), LANES).astype(BF16)
        b_r = jnp.pad(jnp.concatenate([moe_b_group[l], moe_b_expert[l]]), (0, LANES - N_GROUPS - N_EXPERTS))
        route_args = (norm_ffn[l].reshape(1, D), sc2, sh2, w_r, b_r.reshape(1, LANES))
        j = l // 2
        if l % 2 == 0:
            w = hy_w_in[j]
            o = np.cumsum((0,) + (ATTN_WIDTH, HEAD_DIM, HEAD_DIM, IDX_HEADS * IDX_DIM, IDX_DIM, IDX_HEADS,
                                  CONV_WIDTH, CONV_WIDTH, CONV_WIDTH))
            wq, wk, wv, wiq, wik, wiw, wbg, wcg, wu = [w[:, o[i]:o[i + 1]] for i in range(9)]
            w_pad = jnp.concatenate(
                [wq, wiq, wbg, wcg, wu, wk, wv, _pad_cols(jnp.concatenate([wik, wiw], axis=1), LANES)],
                axis=1).astype(BF16)
            qn_t = jnp.tile(hy_q_norm[j], ATTN_HEADS).reshape(1, ATTN_WIDTH)
            kn_t = jnp.tile(hy_k_norm[j], LANES // HEAD_DIM).reshape(1, LANES)
            qt, iqt, bcu, kv, kvt, ik, iwt = _hyb_in(x, gain1, sc1, sh1, w_pad, cos_t, sin_t, qn_t, kn_t, grp)
            y_attn = _dsa_attention_t(qt, iqt, iwt, kv, kvt, ik)
            x, h2, route, route_t, counts =_hyb_out(y_attn, bcu, hy_conv_w[j], hy_w_out[j].astype(BF16), x, g1, route_args)
        else:
            w = ml_w_in[j]
            hq = ML_HEADS * ML_QK_DIM
            hv = ML_HEADS * ML_V_DIM
            wq, wk, wv = w[:, :hq], w[:, hq:2 * hq], w[:, 2 * hq:2 * hq + hv]
            wg = w[:, 2 * hq + hv:2 * hq + hv + 2 * ML_HEADS]
            wo = w[:, 2 * hq + hv + 2 * ML_HEADS:]
            w_pad = jnp.concatenate([wq, wk, wv, wo, _pad_cols(wg, LANES)], axis=1).astype(BF16)
            gate_bias = jnp.pad(ml_b_gates[j], (0, LANES - 2 * ML_HEADS)).reshape(1, LANES)
            q, k, v, og, g_cols, g_rows = _ml_in(x, gain1, sc1, sh1, w_pad, gate_bias)
            assert ML_STEP_HEADS == ML_HEADS
            hh = _mlstm(q, k, v, g_rows[:, None], g_cols[:, None], ml_out_norm[j].reshape(1, hv))
            x, h2, route, route_t, counts =_ml_out(hh, og, ml_w_out[j].astype(BF16), x, g1, route_args)

        dest, n_rows, blk_e, n_used, next_e = _moe_dispatch(route_t, counts, T)
        xs = _sc_scatter_rows(h2.reshape(T, D // 2), dest, n_rows)
        ys = _experts(l, blk_e, n_used, next_e, xs, moe_w_gate, moe_w_up, moe_w_down)
        y01 = _sc_gather_rows(ys, dest.reshape(TOP_K * T)).reshape(TOP_K, B, S, D // 2)
        x = _combine(x, g2, y01, route)
    return x
```

```python
import functools

import numpy as np
import jax
import jax.numpy as jnp
from jax import lax
from jax.experimental import pallas as pl
from jax.experimental.pallas import tpu as pltpu
from jax.experimental.pallas import tpu_sc as plsc

F32 = jnp.float32
BF16 = jnp.bfloat16
HIGHEST = lax.Precision.HIGHEST

D_MODEL = 1024
DEPTH = 4
ATTN_HEADS = 8
HEAD_DIM = 64
ATTN_WIDTH = ATTN_HEADS * HEAD_DIM
IDX_HEADS = 8
IDX_DIM = 64
INDEX_TOPK = 256
Q_BLOCK = 256
ROPE_THETA = 10000.0
CONV_WIDTH = D_MODEL - ATTN_WIDTH
CONV_K = 3
ML_HEADS = 8
ML_QK_DIM = 64
ML_V_DIM = 128
N_GROUPS = 4
EXPERTS_PER_GROUP = 8
N_EXPERTS = N_GROUPS * EXPERTS_PER_GROUP
TOP_K = 2
D_EXPERT = 512
MOE_BLOCK = 512
NORM_EPS = 1e-6

LANES = 128
VMEM_LIMIT = 56 * 1024 * 1024
TOKEN_TILE = 256
ML_CHUNK = 256
ML_STEP_HEADS = 8
NEG_INF = float("-inf")


def _cparams(*sem):
    return pltpu.CompilerParams(dimension_semantics=sem, vmem_limit_bytes=VMEM_LIMIT)


def _dot(a, b):
    return jnp.dot(a, b, preferred_element_type=F32)


def _pack_bf16_pairs(x):
    bits = lax.bitcast_convert_type(x.astype(BF16).astype(F32), jnp.uint32)
    half = bits.shape[1] // 2
    packed = (bits[:, :half] >> 16) | (bits[:, half:] & jnp.uint32(0xFFFF0000))
    return lax.bitcast_convert_type(packed, jnp.int32)


def _unpack_bf16_pairs(words):
    words = lax.bitcast_convert_type(words, jnp.uint32)
    return jnp.concatenate(
        [lax.bitcast_convert_type(words << 16, F32),
         lax.bitcast_convert_type(words & jnp.uint32(0xFFFF0000), F32)], axis=1)


def _split_dot(a_f32, b_bf16):
    hi = a_f32.astype(BF16)
    lo = (a_f32 - hi.astype(F32)).astype(BF16)
    return _dot(hi, b_bf16) + _dot(lo, b_bf16)


def _ada_kernel(c_ref, w_ref, b_ref, o_ref):
    c = c_ref[...]
    ca = c * jax.nn.sigmoid(c)
    o_ref[0] = jnp.dot(ca, w_ref[0], precision=HIGHEST, preferred_element_type=F32) + b_ref[0]


def _ada_modulation(c, ada_w, ada_b):
    B, D = c.shape
    n_col = ada_w.shape[-1] // D
    return pl.pallas_call(
        _ada_kernel,
        out_shape=jax.ShapeDtypeStruct((DEPTH, B, n_col * D), F32),
        grid=(DEPTH, n_col),
        in_specs=[
            pl.BlockSpec((B, D), lambda l, j: (0, 0)),
            pl.BlockSpec((1, D, D), lambda l, j: (l, 0, j)),
            pl.BlockSpec((1, 1, D), lambda l, j: (l, 0, j)),
        ],
        out_specs=pl.BlockSpec((1, B, D), lambda l, j: (l, 0, j)),
        compiler_params=_cparams("parallel", "parallel"),
        name="ada_modulation",
    )(c, ada_w, ada_b.reshape(DEPTH, 1, n_col * D))


def _modulated_norm(x, gain, scale, shift):
    y = x * lax.rsqrt(jnp.mean(x * x, axis=-1, keepdims=True) + NORM_EPS)
    return y * gain * (1.0 + scale) + shift


def _rope(x, cos, sin_signed, first_half):
    w = x.shape[-1]
    partner = jnp.where(first_half, pltpu.roll(x, w - HEAD_DIM // 2, 1), pltpu.roll(x, HEAD_DIM // 2, 1))
    return x * cos + partner * sin_signed


HYB_COLS = 5 * 512 + 2 * LANES


def _hyb_in_kernel(*refs, n_pending):
    x_ref, pending_refs, refs = refs[0], refs[1:1 + n_pending], refs[1 + n_pending:]
    (gain_ref, sc_ref, sh_ref, w_ref, cos_ref, sin_ref, qn_ref, kn_ref, grp_ref,
     qt_ref, iqt_ref, bcu_ref, kv_ref, kvt_ref, ik_ref, iwt_ref) = refs[:16]
    x = _residual_tile(x_ref, pending_refs)
    if n_pending:
        refs[16][0] = x
    h = _modulated_norm(x, gain_ref[...], sc_ref[0], sh_ref[0])
    p = _dot(h.astype(BF16), w_ref[...])
    cos = cos_ref[...]
    sin = sin_ref[...]
    lane = lax.broadcasted_iota(jnp.int32, (1, ATTN_WIDTH), 1)
    first_half = (lane % HEAD_DIM) < (HEAD_DIM // 2)
    fh128 = first_half[:, :LANES]
    lane128 = lane[:, :LANES]

    q = p[:, 0:512]
    ms = _split_dot(q * q, grp_ref[...]) * (1.0 / HEAD_DIM)
    q = q * lax.rsqrt(ms + NORM_EPS) * qn_ref[...]
    qt_ref[0] = (_rope(q, cos, sin, first_half) * (HEAD_DIM ** -0.5)).T.astype(BF16)

    iq = p[:, 512:1024]
    iqt_ref[0] = (_rope(iq, cos, sin, first_half) * (IDX_DIM ** -0.5)).T.astype(BF16)

    bcu_ref[0] = p[:, 1024:2560]

    kv = p[:, 2560:2688]
    is_k = lane128 < HEAD_DIM
    kk = jnp.where(is_k, kv, 0.0)
    ms_k = jnp.sum(kk * kk, axis=-1, keepdims=True) * (1.0 / HEAD_DIM)
    kn = kv * lax.rsqrt(ms_k + NORM_EPS) * kn_ref[...]
    kr = _rope(kn, cos[:, :LANES], sin[:, :LANES], fh128)
    kv = jnp.where(is_k, kr, kv)
    kv_ref[0] = kv.astype(BF16)
    kvt_ref[0, 0] = kv.T.astype(BF16)

    sm = p[:, 2688:2816]
    ikr = _rope(sm, cos[:, :LANES], sin[:, :LANES], fh128)
    ik_ref[0] = jnp.where(is_k, ikr, 0.0).astype(BF16)
    iwt_ref[0] = sm.T


def _hyb_in(x, pending, gain, sc, sh, w_pad, cos_t, sin_t, qn_t, kn_t, grp):
    B, S, D = x.shape
    tm = TOKEN_TILE
    row = lambda b, j: (b, j, 0)
    per_b = lambda b, j: (b, 0, 0)
    const2 = lambda b, j: (0, 0)
    tab = lambda b, j: (j, 0)
    col = lambda b, j: (b, 0, j)
    assert tm == DSA_KEY_CHUNK
    x_out_shape = (jax.ShapeDtypeStruct((B, S, D), F32),) if pending else ()
    x_out_spec = (pl.BlockSpec((1, tm, D), row),) if pending else ()
    return pl.pallas_call(
        functools.partial(_hyb_in_kernel, n_pending=len(pending)),
        out_shape=(
            jax.ShapeDtypeStruct((B, 512, S), BF16),
            jax.ShapeDtypeStruct((B, 512, S), BF16),
            jax.ShapeDtypeStruct((B, S, 1536), F32),
            jax.ShapeDtypeStruct((B, S, LANES), BF16),
            jax.ShapeDtypeStruct((B, S // tm, LANES, tm), BF16),
            jax.ShapeDtypeStruct((B, S, LANES), BF16),
            jax.ShapeDtypeStruct((B, LANES, S), F32),
        ) + x_out_shape,
        grid=(B, S // tm),
        in_specs=[pl.BlockSpec((1, tm, D), row)] + (_pending_specs(D, tm) if pending else []) + [
            pl.BlockSpec((1, D), const2),
            pl.BlockSpec((1, 1, D), per_b),
            pl.BlockSpec((1, 1, D), per_b),
            pl.BlockSpec((D, HYB_COLS), const2),
            pl.BlockSpec((tm, 512), tab),
            pl.BlockSpec((tm, 512), tab),
            pl.BlockSpec((1, 512), const2),
            pl.BlockSpec((1, LANES), const2),
            pl.BlockSpec((512, 512), const2),
        ],
        out_specs=(
            pl.BlockSpec((1, 512, tm), col),
            pl.BlockSpec((1, 512, tm), col),
            pl.BlockSpec((1, tm, 1536), row),
            pl.BlockSpec((1, tm, LANES), row),
            pl.BlockSpec((1, 1, LANES, tm), lambda b, j: (b, j, 0, 0)),
            pl.BlockSpec((1, tm, LANES), row),
            pl.BlockSpec((1, LANES, tm), col),
        ) + x_out_spec,
        compiler_params=_cparams("parallel", "parallel"),
        name="hybrid_in_proj",
    )(x, *pending, gain, sc, sh, w_pad, cos_t, sin_t, qn_t, kn_t, grp)


DSA_KEY_CHUNK = 256
F32_LOWEST = float(np.finfo(np.float32).min)


def _fold8(x, op):
    parts = x.reshape(x.shape[0] // 8, 8, x.shape[1])
    while parts.shape[0] > 1:
        half = parts.shape[0] // 2
        assert parts.shape[0] == 2 * half
        parts = op(parts[:half], parts[half:])
    return parts[0]


def _col_reduce(x, op):
    t = _fold8(x, op)
    for shift in (4, 2, 1):
        t = op(t, pltpu.roll(t, shift, 0))
    return t[0:1, :]


def _dsa_t_kernel(qt_ref, iqt_ref, iwt_ref, kv_ref, kvt_ref, ik_ref, o_ref, sc_ref, bias_ref, acc_ref):
    CK = DSA_KEY_CHUNK
    QB = Q_BLOCK
    qb = pl.program_id(1)
    nk = (qb * QB + QB + CK - 1) // CK
    kf = float(INDEX_TOPK)
    qpos = qb * QB + lax.broadcasted_iota(jnp.int32, (1, QB), 1)
    krow = lax.broadcasted_iota(jnp.int32, (CK, 1), 0)
    w_idx = iwt_ref[0, IDX_DIM:IDX_DIM + IDX_HEADS, :] * (IDX_HEADS ** -0.5)

    def rows(c):
        return pl.ds(pl.multiple_of(c * CK, CK), CK)

    def heads_on_lanes(ref, width):
        return jnp.concatenate([ref[0, hd * width:(hd + 1) * width, :] for hd in range(ref.shape[1] // width)], axis=1)

    def head_lanes(hd):
        return slice(hd * QB, (hd + 1) * QB)

    iq_wide = heads_on_lanes(iqt_ref, IDX_DIM)
    w_wide = jnp.concatenate([w_idx[hd:hd + 1, :] for hd in range(IDX_HEADS)], axis=1)

    def score_chunk(c, carry):
        mx, mn = carry
        ikc = ik_ref[0, rows(c), :][:, :IDX_DIM]
        s_all = jnp.maximum(_dot(ikc, iq_wide), 0.0) * w_wide
        acc = s_all[:, head_lanes(0)]
        for hd in range(1, IDX_HEADS):
            acc = acc + s_all[:, head_lanes(hd)]
        causal = (c * CK + krow) <= qpos
        sc_ref[rows(c), :] = jnp.where(causal, acc, NEG_INF)
        mx = jnp.maximum(mx, _fold8(jnp.where(causal, acc, NEG_INF), jnp.maximum))
        mn = jnp.minimum(mn, _fold8(jnp.where(causal, acc, jnp.inf), jnp.minimum))
        return mx, mn

    mx8, mn8 = lax.fori_loop(0, nk, score_chunk,
                             (jnp.full((8, QB), NEG_INF, F32), jnp.full((8, QB), jnp.inf, F32)))
    row_max = jnp.max(mx8, axis=0, keepdims=True)
    row_min = jnp.min(mn8, axis=0, keepdims=True)

    @pl.when(nk % 2 == 1)
    def _():
        sc_ref[rows(nk), :] = jnp.full((CK, QB), NEG_INF, F32)

    n_pairs = (nk + 1) // 2

    def pair_rows(c):
        return pl.ds(pl.multiple_of(c * (2 * CK), 2 * CK), 2 * CK)

    def count(pred):
        def body(c, part):
            return part + _fold8(jnp.where(pred(sc_ref[pair_rows(c), :]), 1.0, 0.0), jnp.add)
        part = lax.fori_loop(0, n_pairs, body, jnp.zeros((8, QB), F32))
        return jnp.sum(part, axis=0, keepdims=True)

    @pl.when(qb * QB + QB <= INDEX_TOPK)
    def _():
        def body(c, carry):
            bias_ref[rows(c), :] = jnp.where(sc_ref[rows(c), :] > NEG_INF, 0.0, NEG_INF)
            return carry
        lax.fori_loop(0, nk, body, 0)

    @pl.when(qb * QB + QB > INDEX_TOPK)
    def _():
        top_tied = count(lambda x: x >= row_max) >= kf

        def bisect(_, carry):
            lo, hi = carry
            mid = 0.5 * lo + 0.5 * hi
            ge = count(lambda x: x >= mid) >= kf
            return jnp.where(ge, mid, lo), jnp.where(ge, hi, mid)

        lo, hi = lax.fori_loop(0, 18, bisect, (row_min, row_max))

        def refine_cond(carry):
            it, _, _, done = carry
            return jnp.logical_and(it < nk * CK, jnp.min(done) < 0.5)

        def refine(carry):
            it, hi, thr, done = carry

            def below(c, part):
                x = sc_ref[pair_rows(c), :]
                return jnp.maximum(part, _fold8(jnp.where(x < hi, x, NEG_INF), jnp.maximum))

            m = jnp.max(lax.fori_loop(0, n_pairs, below, jnp.full((8, QB), NEG_INF, F32)), axis=0, keepdims=True)
            hit = count(lambda x: x >= m) >= kf
            fin = done > 0.5
            thr = jnp.where(fin, thr, m)
            hi = jnp.where(jnp.logical_or(fin, hit), hi, m)
            done = jnp.where(hit, 1.0, done)
            return it + 1, hi, thr, done

        done0 = jnp.where(top_tied, 1.0, 0.0)
        _, _, thr, _ = lax.while_loop(refine_cond, refine, (jnp.int32(0), hi, row_max, done0))

        need = kf - count(lambda x: x > thr)
        n_eq = count(lambda x: x == thr)
        tied = jnp.max(n_eq - need) > 0.5

        @pl.when(jnp.logical_not(tied))
        def _():
            def body(c, carry):
                bias_ref[rows(c), :] = jnp.where(sc_ref[rows(c), :] >= thr, 0.0, NEG_INF)
                return carry
            lax.fori_loop(0, nk, body, 0)

        @pl.when(tied)
        def _():
            r_i = lax.broadcasted_iota(jnp.int32, (CK, CK), 0)
            c_i = lax.broadcasted_iota(jnp.int32, (CK, CK), 1)
            lower = jnp.where(c_i <= r_i, 1.0, 0.0).astype(BF16)

            def body(c, seen):
                x = sc_ref[rows(c), :]
                eq = x == thr
                eq_f = jnp.where(eq, 1.0, 0.0)
                rank = _dot(lower, eq_f.astype(BF16)) + seen
                keep = jnp.logical_or(x > thr, jnp.logical_and(eq, rank <= need))
                bias_ref[rows(c), :] = jnp.where(keep, 0.0, NEG_INF)
                return seen + jnp.sum(eq_f, axis=0, keepdims=True)

            lax.fori_loop(0, nk, body, jnp.zeros((1, QB), F32))

    acc_ref[...] = jnp.zeros_like(acc_ref)
    q_wide = heads_on_lanes(qt_ref, HEAD_DIM)

    def attend_chunk(c, carry):
        m_old, l_old = carry
        kc = kv_ref[0, rows(c), :][:, :HEAD_DIM]
        vt = kvt_ref[0, c, HEAD_DIM:, :]
        bias = bias_ref[rows(c), :]
        logits = _dot(kc, q_wide) + jnp.concatenate([bias] * ATTN_HEADS, axis=1)
        m_new = jnp.maximum(m_old, _col_reduce(logits, jnp.maximum))
        alpha = jnp.exp(m_old - m_new)
        p = jnp.exp(logits - m_new)
        l_new = alpha * l_old + _col_reduce(p, jnp.add)
        acc_ref[...] = alpha * acc_ref[...] + _dot(vt, p.astype(BF16))
        return m_new, l_new

    _, l_all = lax.fori_loop(0, nk, attend_chunk, (jnp.full((1, ATTN_HEADS * QB), -1e30, F32),
                                                   jnp.zeros((1, ATTN_HEADS * QB), F32)))
    out_t = acc_ref[...] / l_all
    o_ref[0] = jnp.concatenate([out_t[:, head_lanes(hd)] for hd in range(ATTN_HEADS)], axis=0).T


def _dsa_attention_t(qt, iqt, iwt, kv, kvt, ik):
    B, _, S = qt.shape
    col = lambda b, j: (b, 0, j)
    per_b = lambda b, j: (b, 0, 0)
    return pl.pallas_call(
        _dsa_t_kernel,
        out_shape=jax.ShapeDtypeStruct((B, S, ATTN_WIDTH), F32),
        grid=(B, S // Q_BLOCK),
        in_specs=[
            pl.BlockSpec((1, ATTN_WIDTH, Q_BLOCK), col),
            pl.BlockSpec((1, IDX_HEADS * IDX_DIM, Q_BLOCK), col),
            pl.BlockSpec((1, LANES, Q_BLOCK), col),
            pl.BlockSpec((1, S, LANES), per_b),
            pl.BlockSpec((1, S // DSA_KEY_CHUNK, LANES, DSA_KEY_CHUNK), lambda b, j: (b, 0, 0, 0)),
            pl.BlockSpec((1, S, LANES), per_b),
        ],
        out_specs=pl.BlockSpec((1, Q_BLOCK, ATTN_WIDTH), lambda b, j: (b, j, 0)),
        scratch_shapes=[
            pltpu.VMEM((S, Q_BLOCK), F32),
            pltpu.VMEM((S, Q_BLOCK), F32),
            pltpu.VMEM((HEAD_DIM, ATTN_HEADS * Q_BLOCK), F32),
        ],
        compiler_params=_cparams("parallel", "parallel"),
        name="dsa_attention",
    )(qt, iqt, iwt, kv, kvt, ik)


def _hyb_out_kernel(ya_ref, bcu_ref, halo_ref, cw_ref, w_ref, x_ref, g_ref,
                    gain2_ref, sc2_ref, sh2_ref, wr_ref, br_ref, o_ref, h_ref, r_ref, rt_ref, cnt_ref, run_ref):
    j = pl.program_id(1)
    tm = ya_ref.shape[1]
    bcu = bcu_ref[0]
    bg = bcu[:, 0:512]
    z = bcu[:, 512:1024] * bcu[:, 1024:1536]
    halo = halo_ref[0]
    zh = halo[:, 512:1024] * halo[:, 1024:1536]
    zh = jnp.where(j > 0, zh, 0.0)
    row = lax.broadcasted_iota(jnp.int32, (tm, 1), 0)
    z1 = jnp.where(row >= 1, pltpu.roll(z, 1, 0), zh[7:8, :])
    z2 = jnp.where(row >= 2, pltpu.roll(z, 2, 0), jnp.where(row == 1, zh[7:8, :], zh[6:7, :]))
    cw = cw_ref[...]
    y_conv = bg * (z2 * cw[0:1, :] + z1 * cw[1:2, :] + z * cw[2:3, :])
    y = _dot(ya_ref[0].astype(BF16), w_ref[0:512, :]) + _dot(y_conv.astype(BF16), w_ref[512:1024, :])
    x_new = x_ref[0] + g_ref[0] * y
    o_ref[0] = x_new
    _route_tile(x_new, gain2_ref, sc2_ref, sh2_ref, wr_ref, br_ref, h_ref, r_ref, rt_ref, cnt_ref, run_ref)


def _hyb_out(y_attn, bcu, conv_w, w_out_bf, x, g1, route_args):
    B, S, D = x.shape
    tm = TOKEN_TILE
    row = lambda b, j: (b, j, 0)
    per_b = lambda b, j: (b, 0, 0)
    const2 = lambda b, j: (0, 0)
    halo = lambda b, j: (b, jnp.maximum(j * (tm // 8) - 1, 0), 0)
    r_in, r_shape, r_out, r_scratch = _route_specs(B, S, D, tm)
    return pl.pallas_call(
        _hyb_out_kernel,
        out_shape=(jax.ShapeDtypeStruct((B, S, D), F32),) + r_shape,
        grid=(B, S // tm),
        in_specs=[
            pl.BlockSpec((1, tm, 512), row),
            pl.BlockSpec((1, tm, 1536), row),
            pl.BlockSpec((1, 8, 1536), halo),
            pl.BlockSpec((CONV_K, CONV_WIDTH), const2),
            pl.BlockSpec((D, D), const2),
            pl.BlockSpec((1, tm, D), row),
            pl.BlockSpec((1, 1, D), per_b),
        ] + r_in,
        out_specs=(pl.BlockSpec((1, tm, D), row),) + r_out,
        scratch_shapes=r_scratch,
        compiler_params=_cparams("arbitrary", "arbitrary"),
        name="hybrid_out_proj",
    )(y_attn, bcu, bcu, conv_w, w_out_bf, x, g1, *route_args)


ML_COLS = 512 + 512 + 1024 + 1024 + LANES


def _ml_in_kernel(x_ref, g2_ref, y0_ref, y1_ref, r_ref, gain_ref, sc_ref, sh_ref, w_ref, bias_ref,
                  q_ref, k_ref, v_ref, og_ref, gt_ref, gtt_ref, xo_ref):
    x = _residual_tile(x_ref, (g2_ref, y0_ref, y1_ref, r_ref))
    xo_ref[0] = x
    h = _modulated_norm(x, gain_ref[...], sc_ref[0], sh_ref[0])
    p = _dot(h.astype(BF16), w_ref[...])
    q_ref[0] = (p[:, 0:512] * (ML_QK_DIM ** -0.5)).astype(BF16)
    k_ref[0, 0] = p[:, 512:1024].T.astype(BF16)
    v_ref[0] = p[:, 1024:2048].astype(BF16)
    og_ref[0] = p[:, 2048:3072]
    gates = p[:, 3072:3200] + bias_ref[...]
    gt_ref[0, 0] = gates
    gtt_ref[0, 0] = gates.T[:2 * ML_HEADS, :]


def _ml_in(x, pending, gain, sc, sh, w_pad, gate_bias):
    B, S, D = x.shape
    tm = TOKEN_TILE
    assert tm == ML_CHUNK
    row = lambda b, j: (b, j, 0)
    per_b = lambda b, j: (b, 0, 0)
    const2 = lambda b, j: (0, 0)
    return pl.pallas_call(
        _ml_in_kernel,
        out_shape=(
            jax.ShapeDtypeStruct((B, S, 512), BF16),
            jax.ShapeDtypeStruct((B, S // tm, 512, tm), BF16),
            jax.ShapeDtypeStruct((B, S, 1024), BF16),
            jax.ShapeDtypeStruct((B, S, 1024), F32),
            jax.ShapeDtypeStruct((B, S // tm, tm, LANES), F32),
            jax.ShapeDtypeStruct((B, S // tm, 2 * ML_HEADS, tm), F32),
            jax.ShapeDtypeStruct((B, S, D), F32),
        ),
        grid=(B, S // tm),
        in_specs=[pl.BlockSpec((1, tm, D), row)] + _pending_specs(D, tm) + [
            pl.BlockSpec((1, D), const2),
            pl.BlockSpec((1, 1, D), per_b),
            pl.BlockSpec((1, 1, D), per_b),
            pl.BlockSpec((D, ML_COLS), const2),
            pl.BlockSpec((1, LANES), const2),
        ],
        out_specs=(
            pl.BlockSpec((1, tm, 512), row),
            pl.BlockSpec((1, 1, 512, tm), lambda b, j: (b, j, 0, 0)),
            pl.BlockSpec((1, tm, 1024), row),
            pl.BlockSpec((1, tm, 1024), row),
            pl.BlockSpec((1, 1, tm, LANES), lambda b, j: (b, j, 0, 0)),
            pl.BlockSpec((1, 1, 2 * ML_HEADS, tm), lambda b, j: (b, j, 0, 0)),
            pl.BlockSpec((1, tm, D), row),
        ),
        compiler_params=_cparams("parallel", "parallel"),
        name="mlstm_in_proj",
    )(x, *pending, gain, sc, sh, w_pad, gate_bias)


def _log_sigmoid(f):
    return jnp.minimum(f, 0.0) - jnp.log1p(jnp.exp(-jnp.abs(f)))


def _split3(x):
    a = x.astype(BF16)
    r = x - a.astype(F32)
    b = r.astype(BF16)
    c = (r - b.astype(F32)).astype(BF16)
    return a, b, c


def _twice(a):
    return jnp.concatenate([a, a], axis=1)


def _mlstm_kernel(q_ref, kt_ref, v_ref, grow_ref, gcol_ref, gain_ref, o_ref, c_ref, m_ref):
    L = ML_CHUNK
    HP = ML_STEP_HEADS
    S = q_ref.shape[1]
    c_ref[...] = jnp.zeros_like(c_ref)
    m_ref[...] = jnp.zeros_like(m_ref)

    def chunk(c, carry):
        r0 = pl.multiple_of(c * L, L)
        r_i = lax.broadcasted_iota(jnp.int32, (L, L), 0)
        c_i = lax.broadcasted_iota(jnp.int32, (L, L), 1)
        tril = c_i <= r_i
        lower = jnp.where(tril, 1.0, 0.0).astype(BF16)
        upper = jnp.where(r_i <= c_i, 1.0, 0.0).astype(BF16)
        e_r = lax.broadcasted_iota(jnp.int32, (LANES, HP * LANES), 0)
        e_c = lax.broadcasted_iota(jnp.int32, (LANES, HP * LANES), 1)
        pick = jnp.where(e_r == HP + e_c // LANES, 1.0, 0.0).astype(BF16)
        rows = grow_ref[0, 0, c]
        cols = gcol_ref[0, 0, c]
        b_rows = sum(_dot(p, upper) for p in _split3(_log_sigmoid(rows)))
        b_cols = sum(_dot(lower, p) for p in _split3(_log_sigmoid(cols)))
        b_colr = sum(_dot(p, pick) for p in _split3(b_cols))
        lane = lax.broadcasted_iota(jnp.int32, (1, L), 1)
        b_last_all = jnp.sum(jnp.where(lane == L - 1, b_rows, 0.0), axis=-1, keepdims=True)
        ones_v = jnp.ones((L, ML_V_DIM), BF16)
        for hh in range(HP):
            q = q_ref[0, pl.ds(r0, L), hh * ML_QK_DIM:(hh + 1) * ML_QK_DIM]
            kt = kt_ref[0, c, hh * ML_QK_DIM:(hh + 1) * ML_QK_DIM, :]
            v = v_ref[0, pl.ds(r0, L), hh * ML_V_DIM:(hh + 1) * ML_V_DIM]
            vx = jnp.concatenate([v, ones_v], axis=1)
            i_row = rows[hh:hh + 1, :]
            b_row = b_rows[HP + hh:HP + hh + 1, :]
            b_last = b_last_all[HP + hh:HP + hh + 1, :]
            b_col = b_colr[:, hh * LANES:(hh + 1) * LANES]
            m_prev = m_ref[hh]
            ctn = c_ref[hh]

            dmat = jnp.where(tril, _twice(b_col) - b_row + i_row, NEG_INF)
            inter = b_col + m_prev
            m_t = jnp.maximum(inter, jnp.max(dmat, axis=-1, keepdims=True))
            w_intra = jnp.exp(dmat - _twice(m_t))
            w_inter = jnp.exp(inter - m_t)
            intra = (w_intra * _dot(q, kt)).astype(BF16)
            tot = _twice(w_inter) * _dot(q, ctn.astype(BF16)) + _dot(intra, vx)
            num = tot[:, :ML_V_DIM]
            den = tot[:, ML_V_DIM:]
            hc = num / jnp.maximum(jnp.abs(den), jnp.exp(-m_t))
            y = hc * lax.rsqrt(jnp.mean(hc * hc, axis=-1, keepdims=True) + NORM_EPS)
            o_ref[0, pl.ds(r0, L), hh * ML_V_DIM:(hh + 1) * ML_V_DIM] = (
                y * gain_ref[:, hh * ML_V_DIM:(hh + 1) * ML_V_DIM])

            g_row = b_last - b_row + i_row
            m_new = jnp.maximum(b_last + m_prev, jnp.max(g_row, axis=-1, keepdims=True))
            decay = jnp.exp(b_last + m_prev - m_new)
            kw = (kt.astype(F32) * jnp.exp(g_row - _twice(m_new))).astype(BF16)
            c_ref[hh] = _twice(decay) * ctn + _dot(kw, vx)
            m_ref[hh] = m_new
        return carry

    lax.fori_loop(0, S // L, chunk, 0)


def _mlstm(q, kt, v, g_rows, g_cols, out_gain):
    B, S, _ = q.shape
    assert ML_CHUNK == 2 * LANES
    nc = S // ML_CHUNK
    hp = ML_STEP_HEADS
    return pl.pallas_call(
        _mlstm_kernel,
        out_shape=jax.ShapeDtypeStruct((B, S, ML_HEADS * ML_V_DIM), F32),
        grid=(B, ML_HEADS // hp),
        in_specs=[
            pl.BlockSpec((1, S, hp * ML_QK_DIM), lambda b, p: (b, 0, p)),
            pl.BlockSpec((1, nc, hp * ML_QK_DIM, ML_CHUNK), lambda b, p: (b, 0, p, 0)),
            pl.BlockSpec((1, S, hp * ML_V_DIM), lambda b, p: (b, 0, p)),
            pl.BlockSpec((1, 1, nc, 2 * hp, ML_CHUNK), lambda b, p: (b, p, 0, 0, 0)),
            pl.BlockSpec((1, 1, nc, ML_CHUNK, LANES), lambda b, p: (b, p, 0, 0, 0)),
            pl.BlockSpec((1, hp * ML_V_DIM), lambda b, p: (0, p)),
        ],
        out_specs=pl.BlockSpec((1, S, hp * ML_V_DIM), lambda b, p: (b, 0, p)),
        scratch_shapes=[
            pltpu.VMEM((hp, ML_QK_DIM, ML_V_DIM + LANES), F32),
            pltpu.VMEM((hp, 1, LANES), F32),
        ],
        compiler_params=_cparams("parallel", "parallel"),
        name="mlstm_chunkwise",
    )(q, kt, v, g_rows, g_cols, out_gain)


def _ml_out_kernel(hh_ref, og_ref, w_ref, x_ref, g_ref,
                   gain2_ref, sc2_ref, sh2_ref, wr_ref, br_ref, o_ref, h_ref, r_ref, rt_ref, cnt_ref, run_ref):
    a = jax.nn.sigmoid(og_ref[0]) * hh_ref[0]
    x_new = x_ref[0] + g_ref[0] * _dot(a.astype(BF16), w_ref[...])
    o_ref[0] = x_new
    _route_tile(x_new, gain2_ref, sc2_ref, sh2_ref, wr_ref, br_ref, h_ref, r_ref, rt_ref, cnt_ref, run_ref)


def _ml_out(hh, og, w_out_bf, x, g1, route_args):
    B, S, D = x.shape
    tm = TOKEN_TILE
    row = lambda b, j: (b, j, 0)
    per_b = lambda b, j: (b, 0, 0)
    const2 = lambda b, j: (0, 0)
    r_in, r_shape, r_out, r_scratch = _route_specs(B, S, D, tm)
    return pl.pallas_call(
        _ml_out_kernel,
        out_shape=(jax.ShapeDtypeStruct((B, S, D), F32),) + r_shape,
        grid=(B, S // tm),
        in_specs=[
            pl.BlockSpec((1, tm, D), row),
            pl.BlockSpec((1, tm, D), row),
            pl.BlockSpec((D, D), const2),
            pl.BlockSpec((1, tm, D), row),
            pl.BlockSpec((1, 1, D), per_b),
        ] + r_in,
        out_specs=(pl.BlockSpec((1, tm, D), row),) + r_out,
        scratch_shapes=r_scratch,
        compiler_params=_cparams("arbitrary", "arbitrary"),
        name="mlstm_out_proj",
    )(hh, og, w_out_bf, x, g1, *route_args)


def _first_argmax(x, lane, width):
    mx = jnp.max(x, axis=-1, keepdims=True)
    idx = jnp.min(jnp.where(x == mx, lane, width), axis=-1, keepdims=True)
    return mx, idx


def _route_tile(x, gain_ref, sc_ref, sh_ref, w_ref, b_ref, h_ref, r_ref, rt_ref, cnt_ref, run_ref):
    tm = x.shape[0]

    @pl.when(jnp.logical_and(pl.program_id(0) == 0, pl.program_id(1) == 0))
    def _():
        run_ref[...] = jnp.zeros_like(run_ref)

    h = _modulated_norm(x, gain_ref[...], sc_ref[0], sh_ref[0])
    h_ref[0] = _pack_bf16_pairs(h)
    logits = _dot(h.astype(BF16), w_ref[...]) + b_ref[...]
    lane = lax.broadcasted_iota(jnp.int32, (1, LANES), 1)
    lg = jnp.where(lane < N_GROUPS, logits, NEG_INF)
    g_max, g_sel = _first_argmax(lg, lane, LANES)
    pg = 1.0 / jnp.sum(jnp.exp(lg - g_max), axis=-1, keepdims=True)
    e_lane = lane - N_GROUPS
    in_grp = jnp.logical_and(e_lane >= g_sel * EXPERTS_PER_GROUP, e_lane < (g_sel + 1) * EXPERTS_PER_GROUP)
    le = jnp.where(in_grp, logits, NEG_INF)
    v1, i1 = _first_argmax(le, lane, LANES)
    le2 = jnp.where(lane == i1, NEG_INF, le)
    v2, i2 = _first_argmax(le2, lane, LANES)
    e2 = jnp.exp(v2 - v1)
    w1 = pg / (1.0 + e2)
    w2 = pg * e2 / (1.0 + e2)
    e1 = i1 - N_GROUPS
    e2 = i2 - N_GROUPS
    hot1 = lane == e1
    hot2 = lane == e2
    onehot = jnp.where(jnp.logical_or(hot1, hot2), 1.0, 0.0)
    r_i = lax.broadcasted_iota(jnp.int32, (tm, tm), 0)
    c_i = lax.broadcasted_iota(jnp.int32, (tm, tm), 1)
    before = jnp.where(c_i < r_i, 1.0, 0.0).astype(BF16)
    seen = _dot(before, onehot.astype(BF16)) + run_ref[...]
    rank1 = jnp.sum(jnp.where(hot1, seen, 0.0), axis=-1, keepdims=True)
    rank2 = jnp.sum(jnp.where(hot2, seen, 0.0), axis=-1, keepdims=True)
    run_ref[...] = run_ref[...] + jnp.sum(onehot, axis=0, keepdims=True)
    cnt_ref[...] = run_ref[...]

    out = jnp.where(lane == 0, e1.astype(F32), 0.0)
    out = jnp.where(lane == 1, e2.astype(F32), out)
    out = jnp.where(lane == 2, w1, out)
    out = jnp.where(lane == 3, w2, out)
    out = jnp.where(lane == 4, rank1, out)
    out = jnp.where(lane == 5, rank2, out)
    r_ref[0] = out
    rt_ref[0] = out.T[:8, :]


def _route_specs(B, S, D, tm):
    row = lambda b, j: (b, j, 0)
    per_b = lambda b, j: (b, 0, 0)
    const2 = lambda b, j: (0, 0)
    in_specs = [
        pl.BlockSpec((1, D), const2),
        pl.BlockSpec((1, 1, D), per_b),
        pl.BlockSpec((1, 1, D), per_b),
        pl.BlockSpec((D, LANES), const2),
        pl.BlockSpec((1, LANES), const2),
    ]
    out_shape = (
        jax.ShapeDtypeStruct((B, S, D // 2), jnp.int32),
        jax.ShapeDtypeStruct((B, S, LANES), F32),
        jax.ShapeDtypeStruct((B, 8, S), F32),
        jax.ShapeDtypeStruct((1, LANES), F32),
    )
    out_specs = (
        pl.BlockSpec((1, tm, D // 2), row),
        pl.BlockSpec((1, tm, LANES), row),
        pl.BlockSpec((1, 8, tm), lambda b, j: (b, 0, j)),
        pl.BlockSpec((1, LANES), const2),
    )
    return in_specs, out_shape, out_specs, [pltpu.VMEM((1, LANES), F32)]


def _experts_kernel(blk_e_ref, n_used_ref, next_e_ref, x_ref, wg_hbm, wu_hbm, wd_hbm, o_ref,
                    wg_f, wu_f, wd_f, wg_s, wu_s, wd_s, sem, *, layer):
    i = pl.program_id(0)
    used = i < n_used_ref[0]
    e = blk_e_ref[i]
    new_expert = jnp.logical_or(i == 0, e != blk_e_ref[jnp.maximum(i - 1, 0)])

    def fetch(expert):
        return (pltpu.make_async_copy(wg_hbm.at[layer, expert], wg_f, sem.at[0]),
                pltpu.make_async_copy(wu_hbm.at[layer, expert], wu_f, sem.at[1]),
                pltpu.make_async_copy(wd_hbm.at[layer, expert], wd_f, sem.at[2]))

    @pl.when(i == 0)
    def _():
        for cp in fetch(e):
            cp.start()

    @pl.when(jnp.logical_and(used, new_expert))
    def _():
        for cp in fetch(e):
            cp.wait()
        wg_s[...] = wg_f[...].astype(BF16)
        wu_s[...] = wu_f[...].astype(BF16)
        wd_s[...] = wd_f[...].astype(BF16)

        @pl.when(next_e_ref[i] >= 0)
        def _():
            for cp in fetch(next_e_ref[i]):
                cp.start()

    @pl.when(used)
    def _():
        x = _unpack_bf16_pairs(x_ref[...]).astype(BF16)
        a = _dot(x, wg_s[...])
        u = _dot(x, wu_s[...])
        act = a * jax.nn.sigmoid(a) * u
        o_ref[...] = _pack_bf16_pairs(_dot(act.astype(BF16), wd_s[...]))

    @pl.when(i >= n_used_ref[0])
    def _():
        o_ref[...] = jnp.zeros_like(o_ref)


def _experts(layer, blk_e, n_used, next_e, xs, w_gate, w_up, w_down):
    R = xs.shape[0]
    D = 2 * xs.shape[1]
    n_blk = R // MOE_BLOCK
    rows = lambda i, be, nu, ne: (i, 0)
    grid_spec = pltpu.PrefetchScalarGridSpec(
        num_scalar_prefetch=3,
        grid=(n_blk,),
        in_specs=[
            pl.BlockSpec((MOE_BLOCK, D // 2), rows),
            pl.BlockSpec(memory_space=pl.ANY),
            pl.BlockSpec(memory_space=pl.ANY),
            pl.BlockSpec(memory_space=pl.ANY),
        ],
        out_specs=pl.BlockSpec((MOE_BLOCK, D // 2), rows),
        scratch_shapes=[
            pltpu.VMEM((D, D_EXPERT), F32),
            pltpu.VMEM((D, D_EXPERT), F32),
            pltpu.VMEM((D_EXPERT, D), F32),
            pltpu.VMEM((D, D_EXPERT), BF16),
            pltpu.VMEM((D, D_EXPERT), BF16),
            pltpu.VMEM((D_EXPERT, D), BF16),
            pltpu.SemaphoreType.DMA((3,)),
        ],
    )
    return pl.pallas_call(
        functools.partial(_experts_kernel, layer=layer),
        out_shape=jax.ShapeDtypeStruct((R, D // 2), jnp.int32),
        grid_spec=grid_spec,
        compiler_params=_cparams("arbitrary"),
        name="moe_experts",
    )(blk_e, n_used, next_e, xs, w_gate, w_up, w_down)


def _residual_tile(x_ref, pending_refs):
    if not pending_refs:
        return x_ref[0]
    g_ref, y0_ref, y1_ref, r_ref = pending_refs
    r = r_ref[0]
    y = _unpack_bf16_pairs(y0_ref[0, 0]) * r[:, 2:3] + _unpack_bf16_pairs(y1_ref[0, 0]) * r[:, 3:4]
    return x_ref[0] + g_ref[0] * y


def _pending_specs(D, tm):
    return [
        pl.BlockSpec((1, 1, D), lambda b, j: (b, 0, 0)),
        pl.BlockSpec((1, 1, tm, D // 2), lambda b, j: (0, b, j, 0)),
        pl.BlockSpec((1, 1, tm, D // 2), lambda b, j: (1, b, j, 0)),
        pl.BlockSpec((1, tm, LANES), lambda b, j: (b, j, 0)),
    ]


def _combine_kernel(x_ref, g_ref, y0_ref, y1_ref, r_ref, o_ref):
    o_ref[0] = _residual_tile(x_ref, (g_ref, y0_ref, y1_ref, r_ref))


def _combine(x, pending):
    B, S, D = x.shape
    tm = TOKEN_TILE
    row = lambda b, j: (b, j, 0)
    return pl.pallas_call(
        _combine_kernel,
        out_shape=jax.ShapeDtypeStruct((B, S, D), F32),
        grid=(B, S // tm),
        in_specs=[pl.BlockSpec((1, tm, D), row)] + _pending_specs(D, tm),
        out_specs=pl.BlockSpec((1, tm, D), row),
        compiler_params=_cparams("parallel", "parallel"),
        name="moe_combine",
    )(x, *pending)


SC_CORES = 2
SC_SUBCORES = 16
SC_WORKERS = SC_CORES * SC_SUBCORES
SC_CHUNK = 64


def _sc_mesh():
    return plsc.VectorSubcoreMesh(core_axis_name="c", subcore_axis_name="s",
                                  num_cores=SC_CORES, num_subcores=SC_SUBCORES)


def _sc_scatter_rows(src, idx, n_out):
    T, W = src.shape
    per_w = T // SC_WORKERS
    nch = per_w // SC_CHUNK
    idx4 = idx.reshape(TOP_K, SC_WORKERS, nch, SC_CHUNK)

    @functools.partial(
        pl.kernel, mesh=_sc_mesh(),
        out_type=jax.ShapeDtypeStruct((n_out, W), src.dtype),
        scratch_types=[pltpu.VMEM((TOP_K, nch, SC_CHUNK), jnp.int32), pltpu.VMEM((SC_CHUNK, W), src.dtype)],
        name="sc_scatter_rows",
    )
    def body(src_hbm, idx_hbm, out_hbm, idx_v, rows_v):
        wid = lax.axis_index("s") * SC_CORES + lax.axis_index("c")
        for s in range(TOP_K):
            pltpu.sync_copy(idx_hbm.at[s, wid], idx_v.at[s])

        @pl.loop(0, nch)
        def _(i):
            pltpu.sync_copy(src_hbm.at[pl.ds(wid * per_w + i * SC_CHUNK, SC_CHUNK)], rows_v)
            for s in range(TOP_K):
                pltpu.sync_copy(rows_v, out_hbm.at[idx_v.at[s, i]])

    return body(src, idx4)


def _sc_gather_rows(table, idx):
    N = idx.shape[0]
    W = table.shape[1]
    per_w = N // SC_WORKERS
    nch = per_w // SC_CHUNK
    idx3 = idx.reshape(SC_WORKERS, nch, SC_CHUNK)

    @functools.partial(
        pl.kernel, mesh=_sc_mesh(),
        out_type=jax.ShapeDtypeStruct((N, W), table.dtype),
        scratch_types=[
            pltpu.VMEM((nch, SC_CHUNK), jnp.int32),
            pltpu.VMEM((2, SC_CHUNK, W), table.dtype),
            pltpu.SemaphoreType.DMA((2,)),
            pltpu.SemaphoreType.DMA((2,)),
        ],
        name="sc_gather_rows",
    )
    def body(table_hbm, idx_hbm, out_hbm, idx_v, rows_v, gather_sem, write_sem):
        wid = lax.axis_index("s") * SC_CORES + lax.axis_index("c")
        pltpu.sync_copy(idx_hbm.at[wid], idx_v)

        def gather(j, b):
            return pltpu.make_async_copy(table_hbm.at[idx_v.at[j]], rows_v.at[b], gather_sem.at[b])

        def write(j, b):
            return pltpu.make_async_copy(
                rows_v.at[b], out_hbm.at[pl.ds(wid * per_w + j * SC_CHUNK, SC_CHUNK)], write_sem.at[b])

        gather(0, 0).start()

        @pl.loop(0, nch, step=2)
        def _(i):
            for b in range(2):
                j = i + b

                @pl.when(j >= 1)
                def _():
                    write(j - 1, 1 - b).wait()

                @pl.when(j + 1 < nch)
                def _():
                    gather(j + 1, 1 - b).start()

                gather(j, b).wait()
                write(j, b).start()

        write(nch - 1, (nch - 1) % 2).wait()

    assert nch % 2 == 0
    return body(table, idx3)


def _moe_dispatch(route_t, counts, T):
    A = T * TOP_K
    counts = counts[0, :N_EXPERTS].astype(jnp.int32)
    blocks_per = (counts + MOE_BLOCK - 1) // MOE_BLOCK
    block_end = jnp.cumsum(blocks_per)
    block_start = block_end - blocks_per
    expert = jnp.swapaxes(route_t[:, :TOP_K, :], 0, 1).reshape(TOP_K, T).astype(jnp.int32)
    rank = jnp.swapaxes(route_t[:, 4:4 + TOP_K, :], 0, 1).reshape(TOP_K, T).astype(jnp.int32)
    onehot = expert[None] == jnp.arange(N_EXPERTS, dtype=jnp.int32)[:, None, None]
    start = jnp.sum(jnp.where(onehot, block_start[:, None, None], 0), axis=0)
    dest = start * MOE_BLOCK + rank
    n_blk = -(-A // MOE_BLOCK) + N_EXPERTS
    blk = jnp.arange(n_blk, dtype=jnp.int32)
    blk_e = jnp.minimum(jnp.sum(blk[:, None] >= block_end[None, :], axis=-1), N_EXPERTS - 1).astype(jnp.int32)
    n_used = block_end[-1]
    first = jnp.logical_and(blk < n_used, jnp.logical_or(blk == 0, blk_e != jnp.roll(blk_e, 1)))
    first_pos = jnp.where(first, blk, n_blk)
    next_pos = jnp.concatenate([lax.cummin(first_pos, axis=0, reverse=True)[1:], jnp.full((1,), n_blk, jnp.int32)])
    next_e = jnp.where(next_pos < n_blk, blk_e[jnp.minimum(next_pos, n_blk - 1)], -1).astype(jnp.int32)
    return dest, n_blk * MOE_BLOCK, blk_e, n_used.reshape(1).astype(jnp.int32), next_e


def _rope_tables(S):
    inv = 1.0 / (ROPE_THETA ** (jnp.arange(0, HEAD_DIM, 2, dtype=F32) / HEAD_DIM))
    ang = jnp.arange(S, dtype=F32)[:, None] * inv[None, :]
    cos, sin = jnp.cos(ang), jnp.sin(ang)
    cos_h = jnp.concatenate([cos, cos], axis=-1)
    sin_h = jnp.concatenate([-sin, sin], axis=-1)
    return jnp.tile(cos_h, (1, ATTN_HEADS)), jnp.tile(sin_h, (1, ATTN_HEADS))


def _pad_cols(w, width):
    return jnp.pad(w, ((0, 0), (0, width - w.shape[1])))


def kernel(x, c, ada_w, ada_b, norm_mix, norm_ffn, hy_w_in, hy_q_norm, hy_k_norm, hy_conv_w, hy_w_out, ml_w_in, ml_b_gates, ml_out_norm, ml_w_out, moe_w_group, moe_b_group, moe_w_expert, moe_b_expert, moe_w_gate, moe_w_up, moe_w_down):
    B, S, D = x.shape
    T = B * S
    cos_t, sin_t = _rope_tables(S)
    mod = _ada_modulation(c, ada_w, ada_b).reshape(DEPTH, B, 6, 1, D)
    r_i = np.arange(ATTN_WIDTH)
    grp = jnp.asarray((r_i[:, None] // HEAD_DIM) == (r_i[None, :] // HEAD_DIM), dtype=BF16)

    pending = ()
    for l in range(DEPTH):
        sh1, sc1, g1, sh2, sc2, g2 = [mod[l, :, i] for i in range(6)]
        gain1 = norm_mix[l].reshape(1, D)
        w_r = _pad_cols(jnp.concatenate([moe_w_group[l], moe_w_expert[l]], axis=1), LANES).astype(BF16)
        b_r = jnp.pad(jnp.concatenate([moe_b_group[l], moe_b_expert[l]]), (0, LANES - N_GROUPS - N_EXPERTS))
        route_args = (norm_ffn[l].reshape(1, D), sc2, sh2, w_r, b_r.reshape(1, LANES))
        j = l // 2
        if l % 2 == 0:
            w = hy_w_in[j]
            o = np.cumsum((0,) + (ATTN_WIDTH, HEAD_DIM, HEAD_DIM, IDX_HEADS * IDX_DIM, IDX_DIM, IDX_HEADS,
                                  CONV_WIDTH, CONV_WIDTH, CONV_WIDTH))
            wq, wk, wv, wiq, wik, wiw, wbg, wcg, wu = [w[:, o[i]:o[i + 1]] for i in range(9)]
            w_pad = jnp.concatenate(
                [wq, wiq, wbg, wcg, wu, wk, wv, _pad_cols(jnp.concatenate([wik, wiw], axis=1), LANES)],
                axis=1).astype(BF16)
            qn_t = jnp.tile(hy_q_norm[j], ATTN_HEADS).reshape(1, ATTN_WIDTH)
            kn_t = jnp.tile(hy_k_norm[j], LANES // HEAD_DIM).reshape(1, LANES)
            outs = _hyb_in(x, pending, gain1, sc1, sh1, w_pad, cos_t, sin_t, qn_t, kn_t, grp)
            qt, iqt, bcu, kv, kvt, ik, iwt = outs[:7]
            if pending:
                x = outs[7]
            y_attn = _dsa_attention_t(qt, iqt, iwt, kv, kvt, ik)
            x, h2, route, route_t, counts =_hyb_out(y_attn, bcu, hy_conv_w[j], hy_w_out[j].astype(BF16), x, g1, route_args)
        else:
            w = ml_w_in[j]
            hq = ML_HEADS * ML_QK_DIM
            hv = ML_HEADS * ML_V_DIM
            wq, wk, wv = w[:, :hq], w[:, hq:2 * hq], w[:, 2 * hq:2 * hq + hv]
            wg = w[:, 2 * hq + hv:2 * hq + hv + 2 * ML_HEADS]
            wo = w[:, 2 * hq + hv + 2 * ML_HEADS:]
            w_pad = jnp.concatenate([wq, wk, wv, wo, _pad_cols(wg, LANES)], axis=1).astype(BF16)
            gate_bias = jnp.pad(ml_b_gates[j], (0, LANES - 2 * ML_HEADS)).reshape(1, LANES)
            q, k, v, og, g_cols, g_rows, x = _ml_in(x, pending, gain1, sc1, sh1, w_pad, gate_bias)
            assert ML_STEP_HEADS == ML_HEADS
            hh = _mlstm(q, k, v, g_rows[:, None], g_cols[:, None], ml_out_norm[j].reshape(1, hv))
            x, h2, route, route_t, counts =_ml_out(hh, og, ml_w_out[j].astype(BF16), x, g1, route_args)

        dest, n_rows, blk_e, n_used, next_e = _moe_dispatch(route_t, counts, T)
        xs = _sc_scatter_rows(h2.reshape(T, D // 2), dest, n_rows)
        ys = _experts(l, blk_e, n_used, next_e, xs, moe_w_gate, moe_w_up, moe_w_down)
        y01 = _sc_gather_rows(ys, dest.reshape(TOP_K * T)).reshape(TOP_K, B, S, D // 2)
        pending = (g2, y01, y01, route)
    return _combine(x, pending)
```

```python
import functools

import numpy as np
import jax
import jax.numpy as jnp
from jax import lax
from jax.experimental import pallas as pl
from jax.experimental.pallas import tpu as pltpu
from jax.experimental.pallas import tpu_sc as plsc

F32 = jnp.float32
BF16 = jnp.bfloat16
HIGHEST = lax.Precision.HIGHEST

D_MODEL = 1024
DEPTH = 4
ATTN_HEADS = 8
HEAD_DIM = 64
ATTN_WIDTH = ATTN_HEADS * HEAD_DIM
IDX_HEADS = 8
IDX_DIM = 64
INDEX_TOPK = 256
Q_BLOCK = 256
ROPE_THETA = 10000.0
CONV_WIDTH = D_MODEL - ATTN_WIDTH
CONV_K = 3
ML_HEADS = 8
ML_QK_DIM = 64
ML_V_DIM = 128
N_GROUPS = 4
EXPERTS_PER_GROUP = 8
N_EXPERTS = N_GROUPS * EXPERTS_PER_GROUP
TOP_K = 2
D_EXPERT = 512
MOE_BLOCK = 512
NORM_EPS = 1e-6

LANES = 128
VMEM_LIMIT = 56 * 1024 * 1024
TOKEN_TILE = 256
ML_CHUNK = 256
ML_STEP_HEADS = 8
NEG_INF = float("-inf")


def _cparams(*sem):
    return pltpu.CompilerParams(dimension_semantics=sem, vmem_limit_bytes=VMEM_LIMIT)


def _dot(a, b):
    return jnp.dot(a, b, preferred_element_type=F32)


def _pack_bf16_pairs(x):
    bits = lax.bitcast_convert_type(x.astype(BF16).astype(F32), jnp.uint32)
    half = bits.shape[1] // 2
    packed = (bits[:, :half] >> 16) | (bits[:, half:] & jnp.uint32(0xFFFF0000))
    return lax.bitcast_convert_type(packed, jnp.int32)


def _unpack_bf16_pairs(words):
    words = lax.bitcast_convert_type(words, jnp.uint32)
    return jnp.concatenate(
        [lax.bitcast_convert_type(words << 16, F32),
         lax.bitcast_convert_type(words & jnp.uint32(0xFFFF0000), F32)], axis=1)


def _split_dot(a_f32, b_bf16):
    hi = a_f32.astype(BF16)
    lo = (a_f32 - hi.astype(F32)).astype(BF16)
    return _dot(hi, b_bf16) + _dot(lo, b_bf16)


def _ada_kernel(c_ref, w_ref, b_ref, o_ref):
    c = c_ref[...]
    ca = c * jax.nn.sigmoid(c)
    o_ref[0] = jnp.dot(ca, w_ref[0], precision=HIGHEST, preferred_element_type=F32) + b_ref[0]


def _ada_modulation(c, ada_w, ada_b):
    B, D = c.shape
    n_col = ada_w.shape[-1] // D
    return pl.pallas_call(
        _ada_kernel,
        out_shape=jax.ShapeDtypeStruct((DEPTH, B, n_col * D), F32),
        grid=(DEPTH, n_col),
        in_specs=[
            pl.BlockSpec((B, D), lambda l, j: (0, 0)),
            pl.BlockSpec((1, D, D), lambda l, j: (l, 0, j)),
            pl.BlockSpec((1, 1, D), lambda l, j: (l, 0, j)),
        ],
        out_specs=pl.BlockSpec((1, B, D), lambda l, j: (l, 0, j)),
        compiler_params=_cparams("parallel", "parallel"),
        name="ada_modulation",
    )(c, ada_w, ada_b.reshape(DEPTH, 1, n_col * D))


def _modulated_norm(x, gain, scale, shift):
    y = x * lax.rsqrt(jnp.mean(x * x, axis=-1, keepdims=True) + NORM_EPS)
    return y * gain * (1.0 + scale) + shift


def _rope(x, cos, sin_signed, first_half):
    w = x.shape[-1]
    partner = jnp.where(first_half, pltpu.roll(x, w - HEAD_DIM // 2, 1), pltpu.roll(x, HEAD_DIM // 2, 1))
    return x * cos + partner * sin_signed


HYB_COLS = 5 * 512 + 2 * LANES


def _hyb_in_kernel(*refs, n_pending):
    x_ref, pending_refs, refs = refs[0], refs[1:1 + n_pending], refs[1 + n_pending:]
    (gain_ref, sc_ref, sh_ref, w_ref, cos_ref, sin_ref, qn_ref, kn_ref, grp_ref,
     qt_ref, iqt_ref, bcu_ref, kv_ref, kvt_ref, ik_ref, iwt_ref) = refs[:16]
    x = _residual_tile(x_ref, pending_refs)
    if n_pending:
        refs[16][0] = x
    h = _modulated_norm(x, gain_ref[...], sc_ref[0], sh_ref[0])
    hb = h.astype(BF16)
    p_small = _dot(hb, w_ref[:, 0:2 * LANES])
    p_q = _dot(hb, w_ref[:, 2 * LANES:2 * LANES + 512])
    p_iq = _dot(hb, w_ref[:, 2 * LANES + 512:2 * LANES + 1024])
    cos = cos_ref[...]
    sin = sin_ref[...]
    lane = lax.broadcasted_iota(jnp.int32, (1, ATTN_WIDTH), 1)
    first_half = (lane % HEAD_DIM) < (HEAD_DIM // 2)
    fh128 = first_half[:, :LANES]
    lane128 = lane[:, :LANES]

    kv = p_small[:, :LANES]
    is_k = lane128 < HEAD_DIM
    kk = jnp.where(is_k, kv, 0.0)
    ms_k = jnp.sum(kk * kk, axis=-1, keepdims=True) * (1.0 / HEAD_DIM)
    kn = kv * lax.rsqrt(ms_k + NORM_EPS) * kn_ref[...]
    kr = _rope(kn, cos[:, :LANES], sin[:, :LANES], fh128)
    kv = jnp.where(is_k, kr, kv)
    kv_ref[0] = kv.astype(BF16)
    kvt_ref[0, 0] = kv.T.astype(BF16)

    sm = p_small[:, LANES:]
    ikr = _rope(sm, cos[:, :LANES], sin[:, :LANES], fh128)
    ik_ref[0] = jnp.where(is_k, ikr, 0.0).astype(BF16)
    iwt_ref[0] = sm.T

    ms = _split_dot(p_q * p_q, grp_ref[...]) * (1.0 / HEAD_DIM)
    q = p_q * lax.rsqrt(ms + NORM_EPS) * qn_ref[...]
    qt_ref[0] = (_rope(q, cos, sin, first_half) * (HEAD_DIM ** -0.5)).T.astype(BF16)
    iqt_ref[0] = (_rope(p_iq, cos, sin, first_half) * (IDX_DIM ** -0.5)).T.astype(BF16)

    bcu_ref[0] = _dot(hb, w_ref[:, 2 * LANES + 1024:])


def _hyb_in(x, pending, gain, sc, sh, w_pad, cos_t, sin_t, qn_t, kn_t, grp):
    B, S, D = x.shape
    tm = TOKEN_TILE
    row = lambda b, j: (b, j, 0)
    per_b = lambda b, j: (b, 0, 0)
    const2 = lambda b, j: (0, 0)
    tab = lambda b, j: (j, 0)
    col = lambda b, j: (b, 0, j)
    assert tm == DSA_KEY_CHUNK
    x_out_shape = (jax.ShapeDtypeStruct((B, S, D), F32),) if pending else ()
    x_out_spec = (pl.BlockSpec((1, tm, D), row),) if pending else ()
    return pl.pallas_call(
        functools.partial(_hyb_in_kernel, n_pending=len(pending)),
        out_shape=(
            jax.ShapeDtypeStruct((B, 512, S), BF16),
            jax.ShapeDtypeStruct((B, 512, S), BF16),
            jax.ShapeDtypeStruct((B, S, 1536), F32),
            jax.ShapeDtypeStruct((B, S, LANES), BF16),
            jax.ShapeDtypeStruct((B, S // tm, LANES, tm), BF16),
            jax.ShapeDtypeStruct((B, S, LANES), BF16),
            jax.ShapeDtypeStruct((B, LANES, S), F32),
        ) + x_out_shape,
        grid=(B, S // tm),
        in_specs=[pl.BlockSpec((1, tm, D), row)] + (_pending_specs(D, tm) if pending else []) + [
            pl.BlockSpec((1, D), const2),
            pl.BlockSpec((1, 1, D), per_b),
            pl.BlockSpec((1, 1, D), per_b),
            pl.BlockSpec((D, HYB_COLS), const2),
            pl.BlockSpec((tm, 512), tab),
            pl.BlockSpec((tm, 512), tab),
            pl.BlockSpec((1, 512), const2),
            pl.BlockSpec((1, LANES), const2),
            pl.BlockSpec((512, 512), const2),
        ],
        out_specs=(
            pl.BlockSpec((1, 512, tm), col),
            pl.BlockSpec((1, 512, tm), col),
            pl.BlockSpec((1, tm, 1536), row),
            pl.BlockSpec((1, tm, LANES), row),
            pl.BlockSpec((1, 1, LANES, tm), lambda b, j: (b, j, 0, 0)),
            pl.BlockSpec((1, tm, LANES), row),
            pl.BlockSpec((1, LANES, tm), col),
        ) + x_out_spec,
        compiler_params=_cparams("parallel", "parallel"),
        name="hybrid_in_proj",
    )(x, *pending, gain, sc, sh, w_pad, cos_t, sin_t, qn_t, kn_t, grp)


DSA_KEY_CHUNK = 256
F32_LOWEST = float(np.finfo(np.float32).min)


def _fold8(x, op):
    parts = x.reshape(x.shape[0] // 8, 8, x.shape[1])
    while parts.shape[0] > 1:
        half = parts.shape[0] // 2
        assert parts.shape[0] == 2 * half
        parts = op(parts[:half], parts[half:])
    return parts[0]


def _col_reduce(x, op):
    t = _fold8(x, op)
    for shift in (4, 2, 1):
        t = op(t, pltpu.roll(t, shift, 0))
    return t[0:1, :]


def _dsa_t_kernel(qt_ref, iqt_ref, iwt_ref, kv_ref, kvt_ref, ik_ref, o_ref, sc_ref, bias_ref, acc_ref):
    CK = DSA_KEY_CHUNK
    QB = Q_BLOCK
    qb = pl.program_id(1)
    nk = (qb * QB + QB + CK - 1) // CK
    kf = float(INDEX_TOPK)
    qpos = qb * QB + lax.broadcasted_iota(jnp.int32, (1, QB), 1)
    krow = lax.broadcasted_iota(jnp.int32, (CK, 1), 0)
    w_idx = iwt_ref[0, IDX_DIM:IDX_DIM + IDX_HEADS, :] * (IDX_HEADS ** -0.5)

    def rows(c):
        return pl.ds(pl.multiple_of(c * CK, CK), CK)

    def heads_on_lanes(ref, width):
        return jnp.concatenate([ref[0, hd * width:(hd + 1) * width, :] for hd in range(ref.shape[1] // width)], axis=1)

    def head_lanes(hd):
        return slice(hd * QB, (hd + 1) * QB)

    iq_wide = heads_on_lanes(iqt_ref, IDX_DIM)
    w_wide = jnp.concatenate([w_idx[hd:hd + 1, :] for hd in range(IDX_HEADS)], axis=1)

    def score_chunk(c, carry):
        mx, mn = carry
        ikc = ik_ref[0, rows(c), :][:, :IDX_DIM]
        s_all = jnp.maximum(_dot(ikc, iq_wide), 0.0) * w_wide
        acc = s_all[:, head_lanes(0)]
        for hd in range(1, IDX_HEADS):
            acc = acc + s_all[:, head_lanes(hd)]
        causal = (c * CK + krow) <= qpos
        sc_ref[rows(c), :] = jnp.where(causal, acc, NEG_INF)
        mx = jnp.maximum(mx, _fold8(jnp.where(causal, acc, NEG_INF), jnp.maximum))
        mn = jnp.minimum(mn, _fold8(jnp.where(causal, acc, jnp.inf), jnp.minimum))
        return mx, mn

    mx8, mn8 = lax.fori_loop(0, nk, score_chunk,
                             (jnp.full((8, QB), NEG_INF, F32), jnp.full((8, QB), jnp.inf, F32)))
    row_max = jnp.max(mx8, axis=0, keepdims=True)
    row_min = jnp.min(mn8, axis=0, keepdims=True)

    @pl.when(nk % 2 == 1)
    def _():
        sc_ref[rows(nk), :] = jnp.full((CK, QB), NEG_INF, F32)

    n_pairs = (nk + 1) // 2

    def pair_rows(c):
        return pl.ds(pl.multiple_of(c * (2 * CK), 2 * CK), 2 * CK)

    def count(pred):
        def body(c, part):
            return part + _fold8(jnp.where(pred(sc_ref[pair_rows(c), :]), 1.0, 0.0), jnp.add)
        part = lax.fori_loop(0, n_pairs, body, jnp.zeros((8, QB), F32))
        return jnp.sum(part, axis=0, keepdims=True)

    @pl.when(qb * QB + QB <= INDEX_TOPK)
    def _():
        def body(c, carry):
            bias_ref[rows(c), :] = jnp.where(sc_ref[rows(c), :] > NEG_INF, 0.0, NEG_INF)
            return carry
        lax.fori_loop(0, nk, body, 0)

    @pl.when(qb * QB + QB > INDEX_TOPK)
    def _():
        top_tied = count(lambda x: x >= row_max) >= kf

        def bisect(_, carry):
            lo, hi = carry
            mid = 0.5 * lo + 0.5 * hi
            ge = count(lambda x: x >= mid) >= kf
            return jnp.where(ge, mid, lo), jnp.where(ge, hi, mid)

        lo, hi = lax.fori_loop(0, 18, bisect, (row_min, row_max))

        def refine_cond(carry):
            it, _, _, done = carry
            return jnp.logical_and(it < nk * CK, jnp.min(done) < 0.5)

        def refine(carry):
            it, hi, thr, done = carry

            def below(c, part):
                x = sc_ref[pair_rows(c), :]
                return jnp.maximum(part, _fold8(jnp.where(x < hi, x, NEG_INF), jnp.maximum))

            m = jnp.max(lax.fori_loop(0, n_pairs, below, jnp.full((8, QB), NEG_INF, F32)), axis=0, keepdims=True)
            hit = count(lambda x: x >= m) >= kf
            fin = done > 0.5
            thr = jnp.where(fin, thr, m)
            hi = jnp.where(jnp.logical_or(fin, hit), hi, m)
            done = jnp.where(hit, 1.0, done)
            return it + 1, hi, thr, done

        done0 = jnp.where(top_tied, 1.0, 0.0)
        _, _, thr, _ = lax.while_loop(refine_cond, refine, (jnp.int32(0), hi, row_max, done0))

        need = kf - count(lambda x: x > thr)
        n_eq = count(lambda x: x == thr)
        tied = jnp.max(n_eq - need) > 0.5

        @pl.when(jnp.logical_not(tied))
        def _():
            def body(c, carry):
                bias_ref[rows(c), :] = jnp.where(sc_ref[rows(c), :] >= thr, 0.0, NEG_INF)
                return carry
            lax.fori_loop(0, nk, body, 0)

        @pl.when(tied)
        def _():
            r_i = lax.broadcasted_iota(jnp.int32, (CK, CK), 0)
            c_i = lax.broadcasted_iota(jnp.int32, (CK, CK), 1)
            lower = jnp.where(c_i <= r_i, 1.0, 0.0).astype(BF16)

            def body(c, seen):
                x = sc_ref[rows(c), :]
                eq = x == thr
                eq_f = jnp.where(eq, 1.0, 0.0)
                rank = _dot(lower, eq_f.astype(BF16)) + seen
                keep = jnp.logical_or(x > thr, jnp.logical_and(eq, rank <= need))
                bias_ref[rows(c), :] = jnp.where(keep, 0.0, NEG_INF)
                return seen + jnp.sum(eq_f, axis=0, keepdims=True)

            lax.fori_loop(0, nk, body, jnp.zeros((1, QB), F32))

    acc_ref[...] = jnp.zeros_like(acc_ref)
    q_wide = heads_on_lanes(qt_ref, HEAD_DIM)

    def attend_chunk(c, carry):
        m_old, l_old = carry
        kc = kv_ref[0, rows(c), :][:, :HEAD_DIM]
        vt = kvt_ref[0, c, HEAD_DIM:, :]
        bias = bias_ref[rows(c), :]
        logits = _dot(kc, q_wide) + jnp.concatenate([bias] * ATTN_HEADS, axis=1)
        m_new = jnp.maximum(m_old, _col_reduce(logits, jnp.maximum))
        alpha = jnp.exp(m_old - m_new)
        p = jnp.exp(logits - m_new)
        l_new = alpha * l_old + _col_reduce(p, jnp.add)
        acc_ref[...] = alpha * acc_ref[...] + _dot(vt, p.astype(BF16))
        return m_new, l_new

    _, l_all = lax.fori_loop(0, nk, attend_chunk, (jnp.full((1, ATTN_HEADS * QB), -1e30, F32),
                                                   jnp.zeros((1, ATTN_HEADS * QB), F32)))
    out_t = acc_ref[...] / l_all
    o_ref[0] = jnp.concatenate([out_t[:, head_lanes(hd)] for hd in range(ATTN_HEADS)], axis=0).T


def _dsa_attention_t(qt, iqt, iwt, kv, kvt, ik):
    B, _, S = qt.shape
    col = lambda b, j: (b, 0, j)
    per_b = lambda b, j: (b, 0, 0)
    return pl.pallas_call(
        _dsa_t_kernel,
        out_shape=jax.ShapeDtypeStruct((B, S, ATTN_WIDTH), F32),
        grid=(B, S // Q_BLOCK),
        in_specs=[
            pl.BlockSpec((1, ATTN_WIDTH, Q_BLOCK), col),
            pl.BlockSpec((1, IDX_HEADS * IDX_DIM, Q_BLOCK), col),
            pl.BlockSpec((1, LANES, Q_BLOCK), col),
            pl.BlockSpec((1, S, LANES), per_b),
            pl.BlockSpec((1, S // DSA_KEY_CHUNK, LANES, DSA_KEY_CHUNK), lambda b, j: (b, 0, 0, 0)),
            pl.BlockSpec((1, S, LANES), per_b),
        ],
        out_specs=pl.BlockSpec((1, Q_BLOCK, ATTN_WIDTH), lambda b, j: (b, j, 0)),
        scratch_shapes=[
            pltpu.VMEM((S, Q_BLOCK), F32),
            pltpu.VMEM((S, Q_BLOCK), F32),
            pltpu.VMEM((HEAD_DIM, ATTN_HEADS * Q_BLOCK), F32),
        ],
        compiler_params=_cparams("parallel", "parallel"),
        name="dsa_attention",
    )(qt, iqt, iwt, kv, kvt, ik)


def _hyb_out_kernel(ya_ref, bcu_ref, halo_ref, cw_ref, w_ref, x_ref, g_ref,
                    gain2_ref, sc2_ref, sh2_ref, wr_ref, br_ref, o_ref, h_ref, r_ref, rt_ref, cnt_ref, run_ref):
    j = pl.program_id(1)
    tm = ya_ref.shape[1]
    bcu = bcu_ref[0]
    bg = bcu[:, 0:512]
    z = bcu[:, 512:1024] * bcu[:, 1024:1536]
    halo = halo_ref[0]
    zh = halo[:, 512:1024] * halo[:, 1024:1536]
    zh = jnp.where(j > 0, zh, 0.0)
    row = lax.broadcasted_iota(jnp.int32, (tm, 1), 0)
    z1 = jnp.where(row >= 1, pltpu.roll(z, 1, 0), zh[7:8, :])
    z2 = jnp.where(row >= 2, pltpu.roll(z, 2, 0), jnp.where(row == 1, zh[7:8, :], zh[6:7, :]))
    cw = cw_ref[...]
    y_conv = bg * (z2 * cw[0:1, :] + z1 * cw[1:2, :] + z * cw[2:3, :])
    y = _dot(ya_ref[0].astype(BF16), w_ref[0:512, :]) + _dot(y_conv.astype(BF16), w_ref[512:1024, :])
    x_new = x_ref[0] + g_ref[0] * y
    o_ref[0] = x_new
    _route_tile(x_new, gain2_ref, sc2_ref, sh2_ref, wr_ref, br_ref, h_ref, r_ref, rt_ref, cnt_ref, run_ref)


def _hyb_out(y_attn, bcu, conv_w, w_out_bf, x, g1, route_args):
    B, S, D = x.shape
    tm = TOKEN_TILE
    row = lambda b, j: (b, j, 0)
    per_b = lambda b, j: (b, 0, 0)
    const2 = lambda b, j: (0, 0)
    halo = lambda b, j: (b, jnp.maximum(j * (tm // 8) - 1, 0), 0)
    r_in, r_shape, r_out, r_scratch = _route_specs(B, S, D, tm)
    return pl.pallas_call(
        _hyb_out_kernel,
        out_shape=(jax.ShapeDtypeStruct((B, S, D), F32),) + r_shape,
        grid=(B, S // tm),
        in_specs=[
            pl.BlockSpec((1, tm, 512), row),
            pl.BlockSpec((1, tm, 1536), row),
            pl.BlockSpec((1, 8, 1536), halo),
            pl.BlockSpec((CONV_K, CONV_WIDTH), const2),
            pl.BlockSpec((D, D), const2),
            pl.BlockSpec((1, tm, D), row),
            pl.BlockSpec((1, 1, D), per_b),
        ] + r_in,
        out_specs=(pl.BlockSpec((1, tm, D), row),) + r_out,
        scratch_shapes=r_scratch,
        compiler_params=_cparams("arbitrary", "arbitrary"),
        name="hybrid_out_proj",
    )(y_attn, bcu, bcu, conv_w, w_out_bf, x, g1, *route_args)


ML_COLS = 512 + 512 + 1024 + 1024 + LANES


def _ml_in_kernel(x_ref, g2_ref, y0_ref, y1_ref, r_ref, gain_ref, sc_ref, sh_ref, w_ref, bias_ref,
                  q_ref, k_ref, v_ref, og_ref, gt_ref, gtt_ref, xo_ref):
    x = _residual_tile(x_ref, (g2_ref, y0_ref, y1_ref, r_ref))
    xo_ref[0] = x
    h = _modulated_norm(x, gain_ref[...], sc_ref[0], sh_ref[0])
    hb = h.astype(BF16)
    gates = _dot(hb, w_ref[:, 0:LANES]) + bias_ref[...]
    gt_ref[0, 0] = gates
    gtt_ref[0, 0] = gates.T[:2 * ML_HEADS, :]
    k_ref[0, 0] = _dot(hb, w_ref[:, LANES:LANES + 512]).T.astype(BF16)
    q_ref[0] = (_dot(hb, w_ref[:, LANES + 512:LANES + 1024]) * (ML_QK_DIM ** -0.5)).astype(BF16)
    v_ref[0] = _dot(hb, w_ref[:, LANES + 1024:LANES + 2048]).astype(BF16)
    og_ref[0] = _dot(hb, w_ref[:, LANES + 2048:])


def _ml_in(x, pending, gain, sc, sh, w_pad, gate_bias):
    B, S, D = x.shape
    tm = TOKEN_TILE
    assert tm == ML_CHUNK
    row = lambda b, j: (b, j, 0)
    per_b = lambda b, j: (b, 0, 0)
    const2 = lambda b, j: (0, 0)
    return pl.pallas_call(
        _ml_in_kernel,
        out_shape=(
            jax.ShapeDtypeStruct((B, S, 512), BF16),
            jax.ShapeDtypeStruct((B, S // tm, 512, tm), BF16),
            jax.ShapeDtypeStruct((B, S, 1024), BF16),
            jax.ShapeDtypeStruct((B, S, 1024), F32),
            jax.ShapeDtypeStruct((B, S // tm, tm, LANES), F32),
            jax.ShapeDtypeStruct((B, S // tm, 2 * ML_HEADS, tm), F32),
            jax.ShapeDtypeStruct((B, S, D), F32),
        ),
        grid=(B, S // tm),
        in_specs=[pl.BlockSpec((1, tm, D), row)] + _pending_specs(D, tm) + [
            pl.BlockSpec((1, D), const2),
            pl.BlockSpec((1, 1, D), per_b),
            pl.BlockSpec((1, 1, D), per_b),
            pl.BlockSpec((D, ML_COLS), const2),
            pl.BlockSpec((1, LANES), const2),
        ],
        out_specs=(
            pl.BlockSpec((1, tm, 512), row),
            pl.BlockSpec((1, 1, 512, tm), lambda b, j: (b, j, 0, 0)),
            pl.BlockSpec((1, tm, 1024), row),
            pl.BlockSpec((1, tm, 1024), row),
            pl.BlockSpec((1, 1, tm, LANES), lambda b, j: (b, j, 0, 0)),
            pl.BlockSpec((1, 1, 2 * ML_HEADS, tm), lambda b, j: (b, j, 0, 0)),
            pl.BlockSpec((1, tm, D), row),
        ),
        compiler_params=_cparams("parallel", "parallel"),
        name="mlstm_in_proj",
    )(x, *pending, gain, sc, sh, w_pad, gate_bias)


def _log_sigmoid(f):
    return jnp.minimum(f, 0.0) - jnp.log1p(jnp.exp(-jnp.abs(f)))


def _split3(x):
    a = x.astype(BF16)
    r = x - a.astype(F32)
    b = r.astype(BF16)
    c = (r - b.astype(F32)).astype(BF16)
    return a, b, c


def _twice(a):
    return jnp.concatenate([a, a], axis=1)


def _mlstm_kernel(q_ref, kt_ref, v_ref, grow_ref, gcol_ref, gain_ref, o_ref, c_ref, m_ref):
    L = ML_CHUNK
    HP = ML_STEP_HEADS
    S = q_ref.shape[1]
    c_ref[...] = jnp.zeros_like(c_ref)
    m_ref[...] = jnp.zeros_like(m_ref)

    def chunk(c, carry):
        r0 = pl.multiple_of(c * L, L)
        r_i = lax.broadcasted_iota(jnp.int32, (L, L), 0)
        c_i = lax.broadcasted_iota(jnp.int32, (L, L), 1)
        tril = c_i <= r_i
        lower = jnp.where(tril, 1.0, 0.0).astype(BF16)
        upper = jnp.where(r_i <= c_i, 1.0, 0.0).astype(BF16)
        e_r = lax.broadcasted_iota(jnp.int32, (LANES, HP * LANES), 0)
        e_c = lax.broadcasted_iota(jnp.int32, (LANES, HP * LANES), 1)
        pick = jnp.where(e_r == HP + e_c // LANES, 1.0, 0.0).astype(BF16)
        rows = grow_ref[0, 0, c]
        cols = gcol_ref[0, 0, c]
        b_rows = sum(_dot(p, upper) for p in _split3(_log_sigmoid(rows)))
        b_cols = sum(_dot(lower, p) for p in _split3(_log_sigmoid(cols)))
        b_colr = sum(_dot(p, pick) for p in _split3(b_cols))
        lane = lax.broadcasted_iota(jnp.int32, (1, L), 1)
        b_last_all = jnp.sum(jnp.where(lane == L - 1, b_rows, 0.0), axis=-1, keepdims=True)
        ones_v = jnp.ones((L, ML_V_DIM), BF16)
        for hh in range(HP):
            q = q_ref[0, pl.ds(r0, L), hh * ML_QK_DIM:(hh + 1) * ML_QK_DIM]
            kt = kt_ref[0, c, hh * ML_QK_DIM:(hh + 1) * ML_QK_DIM, :]
            v = v_ref[0, pl.ds(r0, L), hh * ML_V_DIM:(hh + 1) * ML_V_DIM]
            vx = jnp.concatenate([v, ones_v], axis=1)
            i_row = rows[hh:hh + 1, :]
            b_row = b_rows[HP + hh:HP + hh + 1, :]
            b_last = b_last_all[HP + hh:HP + hh + 1, :]
            b_col = b_colr[:, hh * LANES:(hh + 1) * LANES]
            m_prev = m_ref[hh]
            ctn = c_ref[hh]

            dmat = jnp.where(tril, _twice(b_col) - b_row + i_row, NEG_INF)
            inter = b_col + m_prev
            m_t = jnp.maximum(inter, jnp.max(dmat, axis=-1, keepdims=True))
            w_intra = jnp.exp(dmat - _twice(m_t))
            w_inter = jnp.exp(inter - m_t)
            intra = (w_intra * _dot(q, kt)).astype(BF16)
            tot = _twice(w_inter) * _dot(q, ctn.astype(BF16)) + _dot(intra, vx)
            num = tot[:, :ML_V_DIM]
            den = tot[:, ML_V_DIM:]
            hc = num / jnp.maximum(jnp.abs(den), jnp.exp(-m_t))
            y = hc * lax.rsqrt(jnp.mean(hc * hc, axis=-1, keepdims=True) + NORM_EPS)
            o_ref[0, pl.ds(r0, L), hh * ML_V_DIM:(hh + 1) * ML_V_DIM] = (
                y * gain_ref[:, hh * ML_V_DIM:(hh + 1) * ML_V_DIM])

            g_row = b_last - b_row + i_row
            m_new = jnp.maximum(b_last + m_prev, jnp.max(g_row, axis=-1, keepdims=True))
            decay = jnp.exp(b_last + m_prev - m_new)
            kw = (kt.astype(F32) * jnp.exp(g_row - _twice(m_new))).astype(BF16)
            c_ref[hh] = _twice(decay) * ctn + _dot(kw, vx)
            m_ref[hh] = m_new
        return carry

    lax.fori_loop(0, S // L, chunk, 0)


def _mlstm(q, kt, v, g_rows, g_cols, out_gain):
    B, S, _ = q.shape
    assert ML_CHUNK == 2 * LANES
    nc = S // ML_CHUNK
    hp = ML_STEP_HEADS
    return pl.pallas_call(
        _mlstm_kernel,
        out_shape=jax.ShapeDtypeStruct((B, S, ML_HEADS * ML_V_DIM), F32),
        grid=(B, ML_HEADS // hp),
        in_specs=[
            pl.BlockSpec((1, S, hp * ML_QK_DIM), lambda b, p: (b, 0, p)),
            pl.BlockSpec((1, nc, hp * ML_QK_DIM, ML_CHUNK), lambda b, p: (b, 0, p, 0)),
            pl.BlockSpec((1, S, hp * ML_V_DIM), lambda b, p: (b, 0, p)),
            pl.BlockSpec((1, 1, nc, 2 * hp, ML_CHUNK), lambda b, p: (b, p, 0, 0, 0)),
            pl.BlockSpec((1, 1, nc, ML_CHUNK, LANES), lambda b, p: (b, p, 0, 0, 0)),
            pl.BlockSpec((1, hp * ML_V_DIM), lambda b, p: (0, p)),
        ],
        out_specs=pl.BlockSpec((1, S, hp * ML_V_DIM), lambda b, p: (b, 0, p)),
        scratch_shapes=[
            pltpu.VMEM((hp, ML_QK_DIM, ML_V_DIM + LANES), F32),
            pltpu.VMEM((hp, 1, LANES), F32),
        ],
        compiler_params=_cparams("parallel", "parallel"),
        name="mlstm_chunkwise",
    )(q, kt, v, g_rows, g_cols, out_gain)


def _ml_out_kernel(hh_ref, og_ref, w_ref, x_ref, g_ref,
                   gain2_ref, sc2_ref, sh2_ref, wr_ref, br_ref, o_ref, h_ref, r_ref, rt_ref, cnt_ref, run_ref):
    a = jax.nn.sigmoid(og_ref[0]) * hh_ref[0]
    x_new = x_ref[0] + g_ref[0] * _dot(a.astype(BF16), w_ref[...])
    o_ref[0] = x_new
    _route_tile(x_new, gain2_ref, sc2_ref, sh2_ref, wr_ref, br_ref, h_ref, r_ref, rt_ref, cnt_ref, run_ref)


def _ml_out(hh, og, w_out_bf, x, g1, route_args):
    B, S, D = x.shape
    tm = TOKEN_TILE
    row = lambda b, j: (b, j, 0)
    per_b = lambda b, j: (b, 0, 0)
    const2 = lambda b, j: (0, 0)
    r_in, r_shape, r_out, r_scratch = _route_specs(B, S, D, tm)
    return pl.pallas_call(
        _ml_out_kernel,
        out_shape=(jax.ShapeDtypeStruct((B, S, D), F32),) + r_shape,
        grid=(B, S // tm),
        in_specs=[
            pl.BlockSpec((1, tm, D), row),
            pl.BlockSpec((1, tm, D), row),
            pl.BlockSpec((D, D), const2),
            pl.BlockSpec((1, tm, D), row),
            pl.BlockSpec((1, 1, D), per_b),
        ] + r_in,
        out_specs=(pl.BlockSpec((1, tm, D), row),) + r_out,
        scratch_shapes=r_scratch,
        compiler_params=_cparams("arbitrary", "arbitrary"),
        name="mlstm_out_proj",
    )(hh, og, w_out_bf, x, g1, *route_args)


def _first_argmax(x, lane, width):
    mx = jnp.max(x, axis=-1, keepdims=True)
    idx = jnp.min(jnp.where(x == mx, lane, width), axis=-1, keepdims=True)
    return mx, idx


def _route_tile(x, gain_ref, sc_ref, sh_ref, w_ref, b_ref, h_ref, r_ref, rt_ref, cnt_ref, run_ref):
    tm = x.shape[0]

    @pl.when(jnp.logical_and(pl.program_id(0) == 0, pl.program_id(1) == 0))
    def _():
        run_ref[...] = jnp.zeros_like(run_ref)

    h = _modulated_norm(x, gain_ref[...], sc_ref[0], sh_ref[0])
    h_ref[0] = _pack_bf16_pairs(h)
    logits = _dot(h.astype(BF16), w_ref[...]) + b_ref[...]
    lane = lax.broadcasted_iota(jnp.int32, (1, LANES), 1)

    def pick(lgt):
        lg = jnp.where(lane < N_GROUPS, lgt, NEG_INF)
        g_max, g_sel = _first_argmax(lg, lane, LANES)
        pg = 1.0 / jnp.sum(jnp.exp(lg - g_max), axis=-1, keepdims=True)
        e_lane = lane - N_GROUPS
        in_grp = jnp.logical_and(e_lane >= g_sel * EXPERTS_PER_GROUP, e_lane < (g_sel + 1) * EXPERTS_PER_GROUP)
        le = jnp.where(in_grp, lgt, NEG_INF)
        v1, i1 = _first_argmax(le, lane, LANES)
        v2, i2 = _first_argmax(jnp.where(lane == i1, NEG_INF, le), lane, LANES)
        ratio = jnp.exp(v2 - v1)
        return i1 - N_GROUPS, i2 - N_GROUPS, pg / (1.0 + ratio), pg * ratio / (1.0 + ratio)

    e1, e2, w1, w2 = pick(logits)
    hot1 = lane == e1
    hot2 = lane == e2
    onehot = jnp.where(jnp.logical_or(hot1, hot2), 1.0, 0.0)
    r_i = lax.broadcasted_iota(jnp.int32, (tm, tm), 0)
    c_i = lax.broadcasted_iota(jnp.int32, (tm, tm), 1)
    before = jnp.where(c_i < r_i, 1.0, 0.0).astype(BF16)
    seen = _dot(before, onehot.astype(BF16)) + run_ref[...]
    rank1 = jnp.sum(jnp.where(hot1, seen, 0.0), axis=-1, keepdims=True)
    rank2 = jnp.sum(jnp.where(hot2, seen, 0.0), axis=-1, keepdims=True)
    run_ref[...] = run_ref[...] + jnp.sum(onehot, axis=0, keepdims=True)
    cnt_ref[...] = run_ref[...]

    out = jnp.where(lane == 0, e1.astype(F32), 0.0)
    out = jnp.where(lane == 1, e2.astype(F32), out)
    out = jnp.where(lane == 2, w1, out)
    out = jnp.where(lane == 3, w2, out)
    out = jnp.where(lane == 4, rank1, out)
    out = jnp.where(lane == 5, rank2, out)
    r_ref[0] = out
    rt_ref[0] = out.T[:8, :]


def _route_specs(B, S, D, tm):
    row = lambda b, j: (b, j, 0)
    per_b = lambda b, j: (b, 0, 0)
    const2 = lambda b, j: (0, 0)
    in_specs = [
        pl.BlockSpec((1, D), const2),
        pl.BlockSpec((1, 1, D), per_b),
        pl.BlockSpec((1, 1, D), per_b),
        pl.BlockSpec((D, LANES), const2),
        pl.BlockSpec((1, LANES), const2),
    ]
    out_shape = (
        jax.ShapeDtypeStruct((B, S, D // 2), jnp.int32),
        jax.ShapeDtypeStruct((B, S, LANES), F32),
        jax.ShapeDtypeStruct((B, 8, S), F32),
        jax.ShapeDtypeStruct((1, LANES), F32),
    )
    out_specs = (
        pl.BlockSpec((1, tm, D // 2), row),
        pl.BlockSpec((1, tm, LANES), row),
        pl.BlockSpec((1, 8, tm), lambda b, j: (b, 0, j)),
        pl.BlockSpec((1, LANES), const2),
    )
    return in_specs, out_shape, out_specs, [pltpu.VMEM((1, LANES), F32)]


def _experts_kernel(blk_e_ref, n_used_ref, next_e_ref, x_ref, wg_hbm, wu_hbm, wd_hbm, o_ref,
                    wg_f, wu_f, wd_f, wg_s, wu_s, wd_s, sem, *, layer):
    i = pl.program_id(0)
    used = i < n_used_ref[0]
    e = blk_e_ref[i]
    new_expert = jnp.logical_or(i == 0, e != blk_e_ref[jnp.maximum(i - 1, 0)])

    def fetch(expert):
        return (pltpu.make_async_copy(wg_hbm.at[layer, expert], wg_f, sem.at[0]),
                pltpu.make_async_copy(wu_hbm.at[layer, expert], wu_f, sem.at[1]),
                pltpu.make_async_copy(wd_hbm.at[layer, expert], wd_f, sem.at[2]))

    @pl.when(i == 0)
    def _():
        for cp in fetch(e):
            cp.start()

    @pl.when(jnp.logical_and(used, new_expert))
    def _():
        for cp in fetch(e):
            cp.wait()
        wg_s[...] = wg_f[...].astype(BF16)
        wu_s[...] = wu_f[...].astype(BF16)
        wd_s[...] = wd_f[...].astype(BF16)

        @pl.when(next_e_ref[i] >= 0)
        def _():
            for cp in fetch(next_e_ref[i]):
                cp.start()

    @pl.when(used)
    def _():
        x = _unpack_bf16_pairs(x_ref[...]).astype(BF16)
        a = _dot(x, wg_s[...])
        u = _dot(x, wu_s[...])
        act = a * jax.nn.sigmoid(a) * u
        o_ref[...] = _pack_bf16_pairs(_dot(act.astype(BF16), wd_s[...]))

    @pl.when(i >= n_used_ref[0])
    def _():
        o_ref[...] = jnp.zeros_like(o_ref)


def _experts(layer, blk_e, n_used, next_e, xs, w_gate, w_up, w_down):
    R = xs.shape[0]
    D = 2 * xs.shape[1]
    n_blk = R // MOE_BLOCK
    rows = lambda i, be, nu, ne: (i, 0)
    grid_spec = pltpu.PrefetchScalarGridSpec(
        num_scalar_prefetch=3,
        grid=(n_blk,),
        in_specs=[
            pl.BlockSpec((MOE_BLOCK, D // 2), rows),
            pl.BlockSpec(memory_space=pl.ANY),
            pl.BlockSpec(memory_space=pl.ANY),
            pl.BlockSpec(memory_space=pl.ANY),
        ],
        out_specs=pl.BlockSpec((MOE_BLOCK, D // 2), rows),
        scratch_shapes=[
            pltpu.VMEM((D, D_EXPERT), F32),
            pltpu.VMEM((D, D_EXPERT), F32),
            pltpu.VMEM((D_EXPERT, D), F32),
            pltpu.VMEM((D, D_EXPERT), BF16),
            pltpu.VMEM((D, D_EXPERT), BF16),
            pltpu.VMEM((D_EXPERT, D), BF16),
            pltpu.SemaphoreType.DMA((3,)),
        ],
    )
    return pl.pallas_call(
        functools.partial(_experts_kernel, layer=layer),
        out_shape=jax.ShapeDtypeStruct((R, D // 2), jnp.int32),
        grid_spec=grid_spec,
        compiler_params=_cparams("arbitrary"),
        name="moe_experts",
    )(blk_e, n_used, next_e, xs, w_gate, w_up, w_down)


def _residual_tile(x_ref, pending_refs):
    if not pending_refs:
        return x_ref[0]
    g_ref, y0_ref, y1_ref, r_ref = pending_refs
    r = r_ref[0]
    y = _unpack_bf16_pairs(y0_ref[0, 0]) * r[:, 2:3] + _unpack_bf16_pairs(y1_ref[0, 0]) * r[:, 3:4]
    return x_ref[0] + g_ref[0] * y


def _pending_specs(D, tm):
    return [
        pl.BlockSpec((1, 1, D), lambda b, j: (b, 0, 0)),
        pl.BlockSpec((1, 1, tm, D // 2), lambda b, j: (0, b, j, 0)),
        pl.BlockSpec((1, 1, tm, D // 2), lambda b, j: (1, b, j, 0)),
        pl.BlockSpec((1, tm, LANES), lambda b, j: (b, j, 0)),
    ]


def _combine_kernel(x_ref, g_ref, y0_ref, y1_ref, r_ref, o_ref):
    o_ref[0] = _residual_tile(x_ref, (g_ref, y0_ref, y1_ref, r_ref))


def _combine(x, pending):
    B, S, D = x.shape
    tm = TOKEN_TILE
    row = lambda b, j: (b, j, 0)
    return pl.pallas_call(
        _combine_kernel,
        out_shape=jax.ShapeDtypeStruct((B, S, D), F32),
        grid=(B, S // tm),
        in_specs=[pl.BlockSpec((1, tm, D), row)] + _pending_specs(D, tm),
        out_specs=pl.BlockSpec((1, tm, D), row),
        compiler_params=_cparams("parallel", "parallel"),
        name="moe_combine",
    )(x, *pending)


SC_CORES = 2
SC_SUBCORES = 16
SC_WORKERS = SC_CORES * SC_SUBCORES
SC_CHUNK = 64


def _sc_mesh():
    return plsc.VectorSubcoreMesh(core_axis_name="c", subcore_axis_name="s",
                                  num_cores=SC_CORES, num_subcores=SC_SUBCORES)


def _sc_scatter_rows(src, idx, n_out):
    T, W = src.shape
    per_w = T // SC_WORKERS
    nch = per_w // SC_CHUNK
    idx4 = idx.reshape(TOP_K, SC_WORKERS, nch, SC_CHUNK)

    @functools.partial(
        pl.kernel, mesh=_sc_mesh(),
        out_type=jax.ShapeDtypeStruct((n_out, W), src.dtype),
        scratch_types=[pltpu.VMEM((TOP_K, nch, SC_CHUNK), jnp.int32), pltpu.VMEM((SC_CHUNK, W), src.dtype)],
        name="sc_scatter_rows",
    )
    def body(src_hbm, idx_hbm, out_hbm, idx_v, rows_v):
        wid = lax.axis_index("s") * SC_CORES + lax.axis_index("c")
        for s in range(TOP_K):
            pltpu.sync_copy(idx_hbm.at[s, wid], idx_v.at[s])

        @pl.loop(0, nch)
        def _(i):
            pltpu.sync_copy(src_hbm.at[pl.ds(wid * per_w + i * SC_CHUNK, SC_CHUNK)], rows_v)
            for s in range(TOP_K):
                pltpu.sync_copy(rows_v, out_hbm.at[idx_v.at[s, i]])

    return body(src, idx4)


def _sc_gather_rows(table, idx):
    N = idx.shape[0]
    W = table.shape[1]
    per_w = N // SC_WORKERS
    nch = per_w // SC_CHUNK
    idx3 = idx.reshape(SC_WORKERS, nch, SC_CHUNK)

    @functools.partial(
        pl.kernel, mesh=_sc_mesh(),
        out_type=jax.ShapeDtypeStruct((N, W), table.dtype),
        scratch_types=[
            pltpu.VMEM((nch, SC_CHUNK), jnp.int32),
            pltpu.VMEM((2, SC_CHUNK, W), table.dtype),
            pltpu.SemaphoreType.DMA((2,)),
            pltpu.SemaphoreType.DMA((2,)),
        ],
        name="sc_gather_rows",
    )
    def body(table_hbm, idx_hbm, out_hbm, idx_v, rows_v, gather_sem, write_sem):
        wid = lax.axis_index("s") * SC_CORES + lax.axis_index("c")
        pltpu.sync_copy(idx_hbm.at[wid], idx_v)

        def gather(j, b):
            return pltpu.make_async_copy(table_hbm.at[idx_v.at[j]], rows_v.at[b], gather_sem.at[b])

        def write(j, b):
            return pltpu.make_async_copy(
                rows_v.at[b], out_hbm.at[pl.ds(wid * per_w + j * SC_CHUNK, SC_CHUNK)], write_sem.at[b])

        gather(0, 0).start()

        @pl.loop(0, nch, step=2)
        def _(i):
            for b in range(2):
                j = i + b

                @pl.when(j >= 1)
                def _():
                    write(j - 1, 1 - b).wait()

                @pl.when(j + 1 < nch)
                def _():
                    gather(j + 1, 1 - b).start()

                gather(j, b).wait()
                write(j, b).start()

        write(nch - 1, (nch - 1) % 2).wait()

    assert nch % 2 == 0
    return body(table, idx3)


def _moe_dispatch(route_t, counts, T):
    A = T * TOP_K
    counts = counts[0, :N_EXPERTS].astype(jnp.int32)
    blocks_per = (counts + MOE_BLOCK - 1) // MOE_BLOCK
    block_end = jnp.cumsum(blocks_per)
    block_start = block_end - blocks_per
    expert = jnp.swapaxes(route_t[:, :TOP_K, :], 0, 1).reshape(TOP_K, T).astype(jnp.int32)
    rank = jnp.swapaxes(route_t[:, 4:4 + TOP_K, :], 0, 1).reshape(TOP_K, T).astype(jnp.int32)
    onehot = expert[None] == jnp.arange(N_EXPERTS, dtype=jnp.int32)[:, None, None]
    start = jnp.sum(jnp.where(onehot, block_start[:, None, None], 0), axis=0)
    dest = start * MOE_BLOCK + rank
    n_blk = -(-A // MOE_BLOCK) + N_EXPERTS
    blk = jnp.arange(n_blk, dtype=jnp.int32)
    blk_e = jnp.minimum(jnp.sum(blk[:, None] >= block_end[None, :], axis=-1), N_EXPERTS - 1).astype(jnp.int32)
    n_used = block_end[-1]
    first = jnp.logical_and(blk < n_used, jnp.logical_or(blk == 0, blk_e != jnp.roll(blk_e, 1)))
    first_pos = jnp.where(first, blk, n_blk)
    next_pos = jnp.concatenate([lax.cummin(first_pos, axis=0, reverse=True)[1:], jnp.full((1,), n_blk, jnp.int32)])
    next_e = jnp.where(next_pos < n_blk, blk_e[jnp.minimum(next_pos, n_blk - 1)], -1).astype(jnp.int32)
    return dest, n_blk * MOE_BLOCK, blk_e, n_used.reshape(1).astype(jnp.int32), next_e


def _rope_tables(S):
    inv = 1.0 / (ROPE_THETA ** (jnp.arange(0, HEAD_DIM, 2, dtype=F32) / HEAD_DIM))
    ang = jnp.arange(S, dtype=F32)[:, None] * inv[None, :]
    cos, sin = jnp.cos(ang), jnp.sin(ang)
    cos_h = jnp.concatenate([cos, cos], axis=-1)
    sin_h = jnp.concatenate([-sin, sin], axis=-1)
    return jnp.tile(cos_h, (1, ATTN_HEADS)), jnp.tile(sin_h, (1, ATTN_HEADS))


def _pad_cols(w, width):
    return jnp.pad(w, ((0, 0), (0, width - w.shape[1])))


def kernel(x, c, ada_w, ada_b, norm_mix, norm_ffn, hy_w_in, hy_q_norm, hy_k_norm, hy_conv_w, hy_w_out, ml_w_in, ml_b_gates, ml_out_norm, ml_w_out, moe_w_group, moe_b_group, moe_w_expert, moe_b_expert, moe_w_gate, moe_w_up, moe_w_down):
    B, S, D = x.shape
    T = B * S
    cos_t, sin_t = _rope_tables(S)
    mod = _ada_modulation(c, ada_w, ada_b).reshape(DEPTH, B, 6, 1, D)
    r_i = np.arange(ATTN_WIDTH)
    grp = jnp.asarray((r_i[:, None] // HEAD_DIM) == (r_i[None, :] // HEAD_DIM), dtype=BF16)

    pending = ()
    for l in range(DEPTH):
        sh1, sc1, g1, sh2, sc2, g2 = [mod[l, :, i] for i in range(6)]
        gain1 = norm_mix[l].reshape(1, D)
        w_r = _pad_cols(jnp.concatenate([moe_w_group[l], moe_w_expert[l]], axis=1), LANES).astype(BF16)
        b_r = jnp.pad(jnp.concatenate([moe_b_group[l], moe_b_expert[l]]), (0, LANES - N_GROUPS - N_EXPERTS))
        route_args = (norm_ffn[l].reshape(1, D), sc2, sh2, w_r, b_r.reshape(1, LANES))
        j = l // 2
        if l % 2 == 0:
            w = hy_w_in[j]
            o = np.cumsum((0,) + (ATTN_WIDTH, HEAD_DIM, HEAD_DIM, IDX_HEADS * IDX_DIM, IDX_DIM, IDX_HEADS,
                                  CONV_WIDTH, CONV_WIDTH, CONV_WIDTH))
            wq, wk, wv, wiq, wik, wiw, wbg, wcg, wu = [w[:, o[i]:o[i + 1]] for i in range(9)]
            w_pad = jnp.concatenate(
                [wk, wv, _pad_cols(jnp.concatenate([wik, wiw], axis=1), LANES), wq, wiq, wbg, wcg, wu],
                axis=1).astype(BF16)
            qn_t = jnp.tile(hy_q_norm[j], ATTN_HEADS).reshape(1, ATTN_WIDTH)
            kn_t = jnp.tile(hy_k_norm[j], LANES // HEAD_DIM).reshape(1, LANES)
            outs = _hyb_in(x, pending, gain1, sc1, sh1, w_pad, cos_t, sin_t, qn_t, kn_t, grp)
            qt, iqt, bcu, kv, kvt, ik, iwt = outs[:7]
            if pending:
                x = outs[7]
            y_attn = _dsa_attention_t(qt, iqt, iwt, kv, kvt, ik)
            x, h2, route, route_t, counts =_hyb_out(y_attn, bcu, hy_conv_w[j], hy_w_out[j].astype(BF16), x, g1, route_args)
        else:
            w = ml_w_in[j]
            hq = ML_HEADS * ML_QK_DIM
            hv = ML_HEADS * ML_V_DIM
            wq, wk, wv = w[:, :hq], w[:, hq:2 * hq], w[:, 2 * hq:2 * hq + hv]
            wg = w[:, 2 * hq + hv:2 * hq + hv + 2 * ML_HEADS]
            wo = w[:, 2 * hq + hv + 2 * ML_HEADS:]
            w_pad = jnp.concatenate([_pad_cols(wg, LANES), wk, wq, wv, wo], axis=1).astype(BF16)
            gate_bias = jnp.pad(ml_b_gates[j], (0, LANES - 2 * ML_HEADS)).reshape(1, LANES)
            q, k, v, og, g_cols, g_rows, x = _ml_in(x, pending, gain1, sc1, sh1, w_pad, gate_bias)
            assert ML_STEP_HEADS == ML_HEADS
            hh = _mlstm(q, k, v, g_rows[:, None], g_cols[:, None], ml_out_norm[j].reshape(1, hv))
            x, h2, route, route_t, counts =_ml_out(hh, og, ml_w_out[j].astype(BF16), x, g1, route_args)

        dest, n_rows, blk_e, n_used, next_e = _moe_dispatch(route_t, counts, T)
        xs = _sc_scatter_rows(h2.reshape(T, D // 2), dest, n_rows)
        ys = _experts(l, blk_e, n_used, next_e, xs, moe_w_gate, moe_w_up, moe_w_down)
        y01 = _sc_gather_rows(ys, dest.reshape(TOP_K * T)).reshape(TOP_K, B, S, D // 2)
        pending = (g2, y01, y01, route)
    return _combine(x, pending)
```

```python
import functools

import numpy as np
import jax
import jax.numpy as jnp
from jax import lax
from jax.experimental import pallas as pl
from jax.experimental.pallas import tpu as pltpu
from jax.experimental.pallas import tpu_sc as plsc

F32 = jnp.float32
BF16 = jnp.bfloat16
HIGHEST = lax.Precision.HIGHEST

D_MODEL = 1024
DEPTH = 4
ATTN_HEADS = 8
HEAD_DIM = 64
ATTN_WIDTH = ATTN_HEADS * HEAD_DIM
IDX_HEADS = 8
IDX_DIM = 64
INDEX_TOPK = 256
Q_BLOCK = 256
ROPE_THETA = 10000.0
CONV_WIDTH = D_MODEL - ATTN_WIDTH
CONV_K = 3
ML_HEADS = 8
ML_QK_DIM = 64
ML_V_DIM = 128
N_GROUPS = 4
EXPERTS_PER_GROUP = 8
N_EXPERTS = N_GROUPS * EXPERTS_PER_GROUP
TOP_K = 2
D_EXPERT = 512
MOE_BLOCK = 512
NORM_EPS = 1e-6

LANES = 128
VMEM_LIMIT = 56 * 1024 * 1024
TOKEN_TILE = 256
OUT_TILE = 512
ML_CHUNK = 256
ML_STEP_HEADS = 8
NEG_INF = float("-inf")


def _cparams(*sem):
    return pltpu.CompilerParams(dimension_semantics=sem, vmem_limit_bytes=VMEM_LIMIT)


def _dot(a, b):
    return jnp.dot(a, b, preferred_element_type=F32)


def _pack_bf16_pairs(x):
    bits = lax.bitcast_convert_type(x.astype(BF16).astype(F32), jnp.uint32)
    half = bits.shape[1] // 2
    packed = (bits[:, :half] >> 16) | (bits[:, half:] & jnp.uint32(0xFFFF0000))
    return lax.bitcast_convert_type(packed, jnp.int32)


def _unpack_bf16_pairs(words):
    words = lax.bitcast_convert_type(words, jnp.uint32)
    return jnp.concatenate(
        [lax.bitcast_convert_type(words << 16, F32),
         lax.bitcast_convert_type(words & jnp.uint32(0xFFFF0000), F32)], axis=1)


def _split_dot(a_f32, b_bf16):
    hi = a_f32.astype(BF16)
    lo = (a_f32 - hi.astype(F32)).astype(BF16)
    return _dot(hi, b_bf16) + _dot(lo, b_bf16)


def _ada_kernel(c_ref, w_ref, b_ref, o_ref):
    c = c_ref[...]
    ca = c * jax.nn.sigmoid(c)
    o_ref[0] = jnp.dot(ca, w_ref[0], precision=HIGHEST, preferred_element_type=F32) + b_ref[0]


def _ada_modulation(c, ada_w, ada_b):
    B, D = c.shape
    n_col = ada_w.shape[-1] // D
    return pl.pallas_call(
        _ada_kernel,
        out_shape=jax.ShapeDtypeStruct((DEPTH, B, n_col * D), F32),
        grid=(DEPTH, n_col),
        in_specs=[
            pl.BlockSpec((B, D), lambda l, j: (0, 0)),
            pl.BlockSpec((1, D, D), lambda l, j: (l, 0, j)),
            pl.BlockSpec((1, 1, D), lambda l, j: (l, 0, j)),
        ],
        out_specs=pl.BlockSpec((1, B, D), lambda l, j: (l, 0, j)),
        compiler_params=_cparams("parallel", "parallel"),
        name="ada_modulation",
    )(c, ada_w, ada_b.reshape(DEPTH, 1, n_col * D))


def _modulated_norm(x, gain, scale, shift):
    y = x * lax.rsqrt(jnp.mean(x * x, axis=-1, keepdims=True) + NORM_EPS)
    return y * gain * (1.0 + scale) + shift


def _rope(x, cos, sin_signed, first_half):
    w = x.shape[-1]
    partner = jnp.where(first_half, pltpu.roll(x, w - HEAD_DIM // 2, 1), pltpu.roll(x, HEAD_DIM // 2, 1))
    return x * cos + partner * sin_signed


HYB_COLS = 5 * 512 + 2 * LANES


def _hyb_in_kernel(*refs, n_pending):
    x_ref, pending_refs, refs = refs[0], refs[1:1 + n_pending], refs[1 + n_pending:]
    (gain_ref, sc_ref, sh_ref, w_ref, cos_ref, sin_ref, qn_ref, kn_ref, grp_ref,
     qt_ref, iqt_ref, bcu_ref, kv_ref, kvt_ref, ik_ref, iwt_ref) = refs[:16]
    x = _residual_tile(x_ref, pending_refs)
    if n_pending:
        refs[16][0] = x
    h = _modulated_norm(x, gain_ref[...], sc_ref[0], sh_ref[0])
    hb = h.astype(BF16)
    p_small = _dot(hb, w_ref[:, 0:2 * LANES])
    p_q = _dot(hb, w_ref[:, 2 * LANES:2 * LANES + 512])
    p_iq = _dot(hb, w_ref[:, 2 * LANES + 512:2 * LANES + 1024])
    cos = cos_ref[...]
    sin = sin_ref[...]
    lane = lax.broadcasted_iota(jnp.int32, (1, ATTN_WIDTH), 1)
    first_half = (lane % HEAD_DIM) < (HEAD_DIM // 2)
    fh128 = first_half[:, :LANES]
    lane128 = lane[:, :LANES]

    kv = p_small[:, :LANES]
    is_k = lane128 < HEAD_DIM
    kk = jnp.where(is_k, kv, 0.0)
    ms_k = jnp.sum(kk * kk, axis=-1, keepdims=True) * (1.0 / HEAD_DIM)
    kn = kv * lax.rsqrt(ms_k + NORM_EPS) * kn_ref[...]
    kr = _rope(kn, cos[:, :LANES], sin[:, :LANES], fh128)
    kv = jnp.where(is_k, kr, kv)
    kv_ref[0] = kv.astype(BF16)
    kvt_ref[0, 0] = kv.T.astype(BF16)

    sm = p_small[:, LANES:]
    ikr = _rope(sm, cos[:, :LANES], sin[:, :LANES], fh128)
    ik_ref[0] = jnp.where(is_k, ikr, 0.0).astype(BF16)
    iwt_ref[0] = sm.T

    ms = _split_dot(p_q * p_q, grp_ref[...]) * (1.0 / HEAD_DIM)
    q = p_q * lax.rsqrt(ms + NORM_EPS) * qn_ref[...]
    qt_ref[0] = (_rope(q, cos, sin, first_half) * (HEAD_DIM ** -0.5)).T.astype(BF16)
    iqt_ref[0] = (_rope(p_iq, cos, sin, first_half) * (IDX_DIM ** -0.5)).T.astype(BF16)

    bcu_ref[0] = _dot(hb, w_ref[:, 2 * LANES + 1024:])


def _hyb_in(x, pending, gain, sc, sh, w_pad, cos_t, sin_t, qn_t, kn_t, grp):
    B, S, D = x.shape
    tm = TOKEN_TILE
    row = lambda b, j: (b, j, 0)
    per_b = lambda b, j: (b, 0, 0)
    const2 = lambda b, j: (0, 0)
    tab = lambda b, j: (j, 0)
    col = lambda b, j: (b, 0, j)
    assert tm == DSA_KEY_CHUNK
    x_out_shape = (jax.ShapeDtypeStruct((B, S, D), F32),) if pending else ()
    x_out_spec = (pl.BlockSpec((1, tm, D), row),) if pending else ()
    return pl.pallas_call(
        functools.partial(_hyb_in_kernel, n_pending=len(pending)),
        out_shape=(
            jax.ShapeDtypeStruct((B, 512, S), BF16),
            jax.ShapeDtypeStruct((B, 512, S), BF16),
            jax.ShapeDtypeStruct((B, S, 1536), F32),
            jax.ShapeDtypeStruct((B, S, LANES), BF16),
            jax.ShapeDtypeStruct((B, S // tm, LANES, tm), BF16),
            jax.ShapeDtypeStruct((B, S, LANES), BF16),
            jax.ShapeDtypeStruct((B, LANES, S), F32),
        ) + x_out_shape,
        grid=(B, S // tm),
        in_specs=[pl.BlockSpec((1, tm, D), row)] + (_pending_specs(D, tm) if pending else []) + [
            pl.BlockSpec((1, D), const2),
            pl.BlockSpec((1, 1, D), per_b),
            pl.BlockSpec((1, 1, D), per_b),
            pl.BlockSpec((D, HYB_COLS), const2),
            pl.BlockSpec((tm, 512), tab),
            pl.BlockSpec((tm, 512), tab),
            pl.BlockSpec((1, 512), const2),
            pl.BlockSpec((1, LANES), const2),
            pl.BlockSpec((512, 512), const2),
        ],
        out_specs=(
            pl.BlockSpec((1, 512, tm), col),
            pl.BlockSpec((1, 512, tm), col),
            pl.BlockSpec((1, tm, 1536), row),
            pl.BlockSpec((1, tm, LANES), row),
            pl.BlockSpec((1, 1, LANES, tm), lambda b, j: (b, j, 0, 0)),
            pl.BlockSpec((1, tm, LANES), row),
            pl.BlockSpec((1, LANES, tm), col),
        ) + x_out_spec,
        compiler_params=_cparams("parallel", "parallel"),
        name="hybrid_in_proj",
    )(x, *pending, gain, sc, sh, w_pad, cos_t, sin_t, qn_t, kn_t, grp)


DSA_KEY_CHUNK = 256
F32_LOWEST = float(np.finfo(np.float32).min)


def _fold8(x, op):
    parts = x.reshape(x.shape[0] // 8, 8, x.shape[1])
    while parts.shape[0] > 1:
        half = parts.shape[0] // 2
        assert parts.shape[0] == 2 * half
        parts = op(parts[:half], parts[half:])
    return parts[0]


def _col_reduce(x, op):
    t = _fold8(x, op)
    for shift in (4, 2, 1):
        t = op(t, pltpu.roll(t, shift, 0))
    return t[0:1, :]


def _dsa_t_kernel(qt_ref, iqt_ref, iwt_ref, kv_ref, kvt_ref, ik_ref, o_ref, sc_ref, bias_ref, acc_ref):
    CK = DSA_KEY_CHUNK
    QB = Q_BLOCK
    qb = pl.program_id(1)
    nk = (qb * QB + QB + CK - 1) // CK
    kf = float(INDEX_TOPK)
    qpos = qb * QB + lax.broadcasted_iota(jnp.int32, (1, QB), 1)
    krow = lax.broadcasted_iota(jnp.int32, (CK, 1), 0)
    w_idx = iwt_ref[0, IDX_DIM:IDX_DIM + IDX_HEADS, :] * (IDX_HEADS ** -0.5)

    def rows(c):
        return pl.ds(pl.multiple_of(c * CK, CK), CK)

    def heads_on_lanes(ref, width):
        return jnp.concatenate([ref[0, hd * width:(hd + 1) * width, :] for hd in range(ref.shape[1] // width)], axis=1)

    def head_lanes(hd):
        return slice(hd * QB, (hd + 1) * QB)

    iq_wide = heads_on_lanes(iqt_ref, IDX_DIM)
    w_wide = jnp.concatenate([w_idx[hd:hd + 1, :] for hd in range(IDX_HEADS)], axis=1)

    def score_chunk(c, carry):
        mx, mn = carry
        ikc = ik_ref[0, rows(c), :][:, :IDX_DIM]
        s_all = jnp.maximum(_dot(ikc, iq_wide), 0.0) * w_wide
        acc = s_all[:, head_lanes(0)]
        for hd in range(1, IDX_HEADS):
            acc = acc + s_all[:, head_lanes(hd)]
        causal = (c * CK + krow) <= qpos
        sc_ref[rows(c), :] = jnp.where(causal, acc, NEG_INF)
        mx = jnp.maximum(mx, _fold8(jnp.where(causal, acc, NEG_INF), jnp.maximum))
        mn = jnp.minimum(mn, _fold8(jnp.where(causal, acc, jnp.inf), jnp.minimum))
        return mx, mn

    mx8, mn8 = lax.fori_loop(0, nk, score_chunk,
                             (jnp.full((8, QB), NEG_INF, F32), jnp.full((8, QB), jnp.inf, F32)))
    row_max = jnp.max(mx8, axis=0, keepdims=True)
    row_min = jnp.min(mn8, axis=0, keepdims=True)

    @pl.when(nk % 2 == 1)
    def _():
        sc_ref[rows(nk), :] = jnp.full((CK, QB), NEG_INF, F32)

    n_pairs = (nk + 1) // 2

    def pair_rows(c):
        return pl.ds(pl.multiple_of(c * (2 * CK), 2 * CK), 2 * CK)

    def count(pred):
        def body(c, part):
            return part + _fold8(jnp.where(pred(sc_ref[pair_rows(c), :]), 1.0, 0.0), jnp.add)
        part = lax.fori_loop(0, n_pairs, body, jnp.zeros((8, QB), F32))
        return jnp.sum(part, axis=0, keepdims=True)

    @pl.when(qb * QB + QB <= INDEX_TOPK)
    def _():
        def body(c, carry):
            bias_ref[rows(c), :] = jnp.where(sc_ref[rows(c), :] > NEG_INF, 0.0, NEG_INF)
            return carry
        lax.fori_loop(0, nk, body, 0)

    @pl.when(qb * QB + QB > INDEX_TOPK)
    def _():
        top_tied = count(lambda x: x >= row_max) >= kf

        def bisect(_, carry):
            lo, hi = carry
            mid = 0.5 * lo + 0.5 * hi
            ge = count(lambda x: x >= mid) >= kf
            return jnp.where(ge, mid, lo), jnp.where(ge, hi, mid)

        lo, hi = lax.fori_loop(0, 18, bisect, (row_min, row_max))

        def refine_cond(carry):
            it, _, _, done = carry
            return jnp.logical_and(it < nk * CK, jnp.min(done) < 0.5)

        def refine(carry):
            it, hi, thr, done = carry

            def below(c, part):
                x = sc_ref[pair_rows(c), :]
                return jnp.maximum(part, _fold8(jnp.where(x < hi, x, NEG_INF), jnp.maximum))

            m = jnp.max(lax.fori_loop(0, n_pairs, below, jnp.full((8, QB), NEG_INF, F32)), axis=0, keepdims=True)
            hit = count(lambda x: x >= m) >= kf
            fin = done > 0.5
            thr = jnp.where(fin, thr, m)
            hi = jnp.where(jnp.logical_or(fin, hit), hi, m)
            done = jnp.where(hit, 1.0, done)
            return it + 1, hi, thr, done

        done0 = jnp.where(top_tied, 1.0, 0.0)
        _, _, thr, _ = lax.while_loop(refine_cond, refine, (jnp.int32(0), hi, row_max, done0))

        need = kf - count(lambda x: x > thr)
        n_eq = count(lambda x: x == thr)
        tied = jnp.max(n_eq - need) > 0.5

        @pl.when(jnp.logical_not(tied))
        def _():
            def body(c, carry):
                bias_ref[rows(c), :] = jnp.where(sc_ref[rows(c), :] >= thr, 0.0, NEG_INF)
                return carry
            lax.fori_loop(0, nk, body, 0)

        @pl.when(tied)
        def _():
            r_i = lax.broadcasted_iota(jnp.int32, (CK, CK), 0)
            c_i = lax.broadcasted_iota(jnp.int32, (CK, CK), 1)
            lower = jnp.where(c_i <= r_i, 1.0, 0.0).astype(BF16)

            def body(c, seen):
                x = sc_ref[rows(c), :]
                eq = x == thr
                eq_f = jnp.where(eq, 1.0, 0.0)
                rank = _dot(lower, eq_f.astype(BF16)) + seen
                keep = jnp.logical_or(x > thr, jnp.logical_and(eq, rank <= need))
                bias_ref[rows(c), :] = jnp.where(keep, 0.0, NEG_INF)
                return seen + jnp.sum(eq_f, axis=0, keepdims=True)

            lax.fori_loop(0, nk, body, jnp.zeros((1, QB), F32))

    acc_ref[...] = jnp.zeros_like(acc_ref)
    q_wide = heads_on_lanes(qt_ref, HEAD_DIM)

    def attend_chunk(c, carry):
        m_old, l_old = carry
        kc = kv_ref[0, rows(c), :][:, :HEAD_DIM]
        vt = kvt_ref[0, c, HEAD_DIM:, :]
        bias = bias_ref[rows(c), :]
        logits = _dot(kc, q_wide) + jnp.concatenate([bias] * ATTN_HEADS, axis=1)
        m_new = jnp.maximum(m_old, _col_reduce(logits, jnp.maximum))
        alpha = jnp.exp(m_old - m_new)
        p = jnp.exp(logits - m_new)
        l_new = alpha * l_old + _col_reduce(p, jnp.add)
        acc_ref[...] = alpha * acc_ref[...] + _dot(vt, p.astype(BF16))
        return m_new, l_new

    _, l_all = lax.fori_loop(0, nk, attend_chunk, (jnp.full((1, ATTN_HEADS * QB), -1e30, F32),
                                                   jnp.zeros((1, ATTN_HEADS * QB), F32)))
    out_t = acc_ref[...] / l_all
    o_ref[0] = jnp.concatenate([out_t[:, head_lanes(hd)] for hd in range(ATTN_HEADS)], axis=0).T


def _dsa_attention_t(qt, iqt, iwt, kv, kvt, ik):
    B, _, S = qt.shape
    col = lambda b, j: (b, 0, j)
    per_b = lambda b, j: (b, 0, 0)
    return pl.pallas_call(
        _dsa_t_kernel,
        out_shape=jax.ShapeDtypeStruct((B, S, ATTN_WIDTH), F32),
        grid=(B, S // Q_BLOCK),
        in_specs=[
            pl.BlockSpec((1, ATTN_WIDTH, Q_BLOCK), col),
            pl.BlockSpec((1, IDX_HEADS * IDX_DIM, Q_BLOCK), col),
            pl.BlockSpec((1, LANES, Q_BLOCK), col),
            pl.BlockSpec((1, S, LANES), per_b),
            pl.BlockSpec((1, S // DSA_KEY_CHUNK, LANES, DSA_KEY_CHUNK), lambda b, j: (b, 0, 0, 0)),
            pl.BlockSpec((1, S, LANES), per_b),
        ],
        out_specs=pl.BlockSpec((1, Q_BLOCK, ATTN_WIDTH), lambda b, j: (b, j, 0)),
        scratch_shapes=[
            pltpu.VMEM((S, Q_BLOCK), F32),
            pltpu.VMEM((S, Q_BLOCK), F32),
            pltpu.VMEM((HEAD_DIM, ATTN_HEADS * Q_BLOCK), F32),
        ],
        compiler_params=_cparams("parallel", "parallel"),
        name="dsa_attention",
    )(qt, iqt, iwt, kv, kvt, ik)


def _hyb_out_kernel(ya_ref, bcu_ref, halo_ref, cw_ref, w_ref, x_ref, g_ref,
                    gain2_ref, sc2_ref, sh2_ref, wr_ref, br_ref, o_ref, h_ref, r_ref, rt_ref, cnt_ref, run_ref):
    j = pl.program_id(1)
    tm = ya_ref.shape[1]
    bcu = bcu_ref[0]
    bg = bcu[:, 0:512]
    z = bcu[:, 512:1024] * bcu[:, 1024:1536]
    halo = halo_ref[0]
    zh = halo[:, 512:1024] * halo[:, 1024:1536]
    zh = jnp.where(j > 0, zh, 0.0)
    row = lax.broadcasted_iota(jnp.int32, (tm, 1), 0)
    z1 = jnp.where(row >= 1, pltpu.roll(z, 1, 0), zh[7:8, :])
    z2 = jnp.where(row >= 2, pltpu.roll(z, 2, 0), jnp.where(row == 1, zh[7:8, :], zh[6:7, :]))
    cw = cw_ref[...]
    y_conv = bg * (z2 * cw[0:1, :] + z1 * cw[1:2, :] + z * cw[2:3, :])
    y = _dot(ya_ref[0].astype(BF16), w_ref[0:512, :]) + _dot(y_conv.astype(BF16), w_ref[512:1024, :])
    x_new = x_ref[0] + g_ref[0] * y
    o_ref[0] = x_new
    _route_tile(x_new, gain2_ref, sc2_ref, sh2_ref, wr_ref, br_ref, h_ref, r_ref, rt_ref, cnt_ref, run_ref)


def _hyb_out(y_attn, bcu, conv_w, w_out_bf, x, g1, route_args):
    B, S, D = x.shape
    tm = OUT_TILE
    row = lambda b, j: (b, j, 0)
    per_b = lambda b, j: (b, 0, 0)
    const2 = lambda b, j: (0, 0)
    halo = lambda b, j: (b, jnp.maximum(j * (tm // 8) - 1, 0), 0)
    r_in, r_shape, r_out, r_scratch = _route_specs(B, S, D, tm)
    return pl.pallas_call(
        _hyb_out_kernel,
        out_shape=(jax.ShapeDtypeStruct((B, S, D), F32),) + r_shape,
        grid=(B, S // tm),
        in_specs=[
            pl.BlockSpec((1, tm, 512), row),
            pl.BlockSpec((1, tm, 1536), row),
            pl.BlockSpec((1, 8, 1536), halo),
            pl.BlockSpec((CONV_K, CONV_WIDTH), const2),
            pl.BlockSpec((D, D), const2),
            pl.BlockSpec((1, tm, D), row),
            pl.BlockSpec((1, 1, D), per_b),
        ] + r_in,
        out_specs=(pl.BlockSpec((1, tm, D), row),) + r_out,
        scratch_shapes=r_scratch,
        compiler_params=_cparams("arbitrary", "arbitrary"),
        name="hybrid_out_proj",
    )(y_attn, bcu, bcu, conv_w, w_out_bf, x, g1, *route_args)


ML_COLS = 512 + 512 + 1024 + 1024 + LANES


def _ml_in_kernel(x_ref, g2_ref, y0_ref, y1_ref, r_ref, gain_ref, sc_ref, sh_ref, w_ref, bias_ref,
                  q_ref, k_ref, v_ref, og_ref, gt_ref, gtt_ref, xo_ref):
    x = _residual_tile(x_ref, (g2_ref, y0_ref, y1_ref, r_ref))
    xo_ref[0] = x
    h = _modulated_norm(x, gain_ref[...], sc_ref[0], sh_ref[0])
    hb = h.astype(BF16)
    gates = _dot(hb, w_ref[:, 0:LANES]) + bias_ref[...]
    gt_ref[0, 0] = gates
    gtt_ref[0, 0] = gates.T[:2 * ML_HEADS, :]
    k_ref[0, 0] = _dot(hb, w_ref[:, LANES:LANES + 512]).T.astype(BF16)
    q_ref[0] = (_dot(hb, w_ref[:, LANES + 512:LANES + 1024]) * (ML_QK_DIM ** -0.5)).astype(BF16)
    v_ref[0] = _dot(hb, w_ref[:, LANES + 1024:LANES + 2048]).astype(BF16)
    og_ref[0] = _dot(hb, w_ref[:, LANES + 2048:])


def _ml_in(x, pending, gain, sc, sh, w_pad, gate_bias):
    B, S, D = x.shape
    tm = TOKEN_TILE
    assert tm == ML_CHUNK
    row = lambda b, j: (b, j, 0)
    per_b = lambda b, j: (b, 0, 0)
    const2 = lambda b, j: (0, 0)
    return pl.pallas_call(
        _ml_in_kernel,
        out_shape=(
            jax.ShapeDtypeStruct((B, S, 512), BF16),
            jax.ShapeDtypeStruct((B, S // tm, 512, tm), BF16),
            jax.ShapeDtypeStruct((B, S, 1024), BF16),
            jax.ShapeDtypeStruct((B, S, 1024), F32),
            jax.ShapeDtypeStruct((B, S // tm, tm, LANES), F32),
            jax.ShapeDtypeStruct((B, S // tm, 2 * ML_HEADS, tm), F32),
            jax.ShapeDtypeStruct((B, S, D), F32),
        ),
        grid=(B, S // tm),
        in_specs=[pl.BlockSpec((1, tm, D), row)] + _pending_specs(D, tm) + [
            pl.BlockSpec((1, D), const2),
            pl.BlockSpec((1, 1, D), per_b),
            pl.BlockSpec((1, 1, D), per_b),
            pl.BlockSpec((D, ML_COLS), const2),
            pl.BlockSpec((1, LANES), const2),
        ],
        out_specs=(
            pl.BlockSpec((1, tm, 512), row),
            pl.BlockSpec((1, 1, 512, tm), lambda b, j: (b, j, 0, 0)),
            pl.BlockSpec((1, tm, 1024), row),
            pl.BlockSpec((1, tm, 1024), row),
            pl.BlockSpec((1, 1, tm, LANES), lambda b, j: (b, j, 0, 0)),
            pl.BlockSpec((1, 1, 2 * ML_HEADS, tm), lambda b, j: (b, j, 0, 0)),
            pl.BlockSpec((1, tm, D), row),
        ),
        compiler_params=_cparams("parallel", "parallel"),
        name="mlstm_in_proj",
    )(x, *pending, gain, sc, sh, w_pad, gate_bias)


def _log_sigmoid(f):
    return jnp.minimum(f, 0.0) - jnp.log1p(jnp.exp(-jnp.abs(f)))


def _split3(x):
    a = x.astype(BF16)
    r = x - a.astype(F32)
    b = r.astype(BF16)
    c = (r - b.astype(F32)).astype(BF16)
    return a, b, c


def _twice(a):
    return jnp.concatenate([a, a], axis=1)


def _mlstm_kernel(q_ref, kt_ref, v_ref, grow_ref, gcol_ref, gain_ref, o_ref, c_ref, m_ref):
    L = ML_CHUNK
    HP = ML_STEP_HEADS
    S = q_ref.shape[1]
    c_ref[...] = jnp.zeros_like(c_ref)
    m_ref[...] = jnp.zeros_like(m_ref)

    def chunk(c, carry):
        r0 = pl.multiple_of(c * L, L)
        r_i = lax.broadcasted_iota(jnp.int32, (L, L), 0)
        c_i = lax.broadcasted_iota(jnp.int32, (L, L), 1)
        tril = c_i <= r_i
        lower = jnp.where(tril, 1.0, 0.0).astype(BF16)
        upper = jnp.where(r_i <= c_i, 1.0, 0.0).astype(BF16)
        e_r = lax.broadcasted_iota(jnp.int32, (LANES, HP * LANES), 0)
        e_c = lax.broadcasted_iota(jnp.int32, (LANES, HP * LANES), 1)
        pick = jnp.where(e_r == HP + e_c // LANES, 1.0, 0.0).astype(BF16)
        rows = grow_ref[0, 0, c]
        cols = gcol_ref[0, 0, c]
        b_rows = sum(_dot(p, upper) for p in _split3(_log_sigmoid(rows)))
        b_cols = sum(_dot(lower, p) for p in _split3(_log_sigmoid(cols)))
        b_colr = sum(_dot(p, pick) for p in _split3(b_cols))
        lane = lax.broadcasted_iota(jnp.int32, (1, L), 1)
        b_last_all = jnp.sum(jnp.where(lane == L - 1, b_rows, 0.0), axis=-1, keepdims=True)
        ones_v = jnp.ones((L, ML_V_DIM), BF16)
        for hh in range(HP):
            q = q_ref[0, pl.ds(r0, L), hh * ML_QK_DIM:(hh + 1) * ML_QK_DIM]
            kt = kt_ref[0, c, hh * ML_QK_DIM:(hh + 1) * ML_QK_DIM, :]
            v = v_ref[0, pl.ds(r0, L), hh * ML_V_DIM:(hh + 1) * ML_V_DIM]
            vx = jnp.concatenate([v, ones_v], axis=1)
            i_row = rows[hh:hh + 1, :]
            b_row = b_rows[HP + hh:HP + hh + 1, :]
            b_last = b_last_all[HP + hh:HP + hh + 1, :]
            b_col = b_colr[:, hh * LANES:(hh + 1) * LANES]
            m_prev = m_ref[hh]
            ctn = c_ref[hh]

            dmat = jnp.where(tril, _twice(b_col) - b_row + i_row, NEG_INF)
            inter = b_col + m_prev
            m_t = jnp.maximum(inter, jnp.max(dmat, axis=-1, keepdims=True))
            w_intra = jnp.exp(dmat - _twice(m_t))
            w_inter = jnp.exp(inter - m_t)
            intra = (w_intra * _dot(q, kt)).astype(BF16)
            tot = _twice(w_inter) * _dot(q, ctn.astype(BF16)) + _dot(intra, vx)
            num = tot[:, :ML_V_DIM]
            den = tot[:, ML_V_DIM:]
            hc = num / jnp.maximum(jnp.abs(den), jnp.exp(-m_t))
            y = hc * lax.rsqrt(jnp.mean(hc * hc, axis=-1, keepdims=True) + NORM_EPS)
            o_ref[0, pl.ds(r0, L), hh * ML_V_DIM:(hh + 1) * ML_V_DIM] = (
                y * gain_ref[:, hh * ML_V_DIM:(hh + 1) * ML_V_DIM])

            g_row = b_last - b_row + i_row
            m_new = jnp.maximum(b_last + m_prev, jnp.max(g_row, axis=-1, keepdims=True))
            decay = jnp.exp(b_last + m_prev - m_new)
            kw = (kt.astype(F32) * jnp.exp(g_row - _twice(m_new))).astype(BF16)
            c_ref[hh] = _twice(decay) * ctn + _dot(kw, vx)
            m_ref[hh] = m_new
        return carry

    lax.fori_loop(0, S // L, chunk, 0)


def _mlstm(q, kt, v, g_rows, g_cols, out_gain):
    B, S, _ = q.shape
    assert ML_CHUNK == 2 * LANES
    nc = S // ML_CHUNK
    hp = ML_STEP_HEADS
    return pl.pallas_call(
        _mlstm_kernel,
        out_shape=jax.ShapeDtypeStruct((B, S, ML_HEADS * ML_V_DIM), F32),
        grid=(B, ML_HEADS // hp),
        in_specs=[
            pl.BlockSpec((1, S, hp * ML_QK_DIM), lambda b, p: (b, 0, p)),
            pl.BlockSpec((1, nc, hp * ML_QK_DIM, ML_CHUNK), lambda b, p: (b, 0, p, 0)),
            pl.BlockSpec((1, S, hp * ML_V_DIM), lambda b, p: (b, 0, p)),
            pl.BlockSpec((1, 1, nc, 2 * hp, ML_CHUNK), lambda b, p: (b, p, 0, 0, 0)),
            pl.BlockSpec((1, 1, nc, ML_CHUNK, LANES), lambda b, p: (b, p, 0, 0, 0)),
            pl.BlockSpec((1, hp * ML_V_DIM), lambda b, p: (0, p)),
        ],
        out_specs=pl.BlockSpec((1, S, hp * ML_V_DIM), lambda b, p: (b, 0, p)),
        scratch_shapes=[
            pltpu.VMEM((hp, ML_QK_DIM, ML_V_DIM + LANES), F32),
            pltpu.VMEM((hp, 1, LANES), F32),
        ],
        compiler_params=_cparams("parallel", "parallel"),
        name="mlstm_chunkwise",
    )(q, kt, v, g_rows, g_cols, out_gain)


def _ml_out_kernel(hh_ref, og_ref, w_ref, x_ref, g_ref,
                   gain2_ref, sc2_ref, sh2_ref, wr_ref, br_ref, o_ref, h_ref, r_ref, rt_ref, cnt_ref, run_ref):
    a = jax.nn.sigmoid(og_ref[0]) * hh_ref[0]
    x_new = x_ref[0] + g_ref[0] * _dot(a.astype(BF16), w_ref[...])
    o_ref[0] = x_new
    _route_tile(x_new, gain2_ref, sc2_ref, sh2_ref, wr_ref, br_ref, h_ref, r_ref, rt_ref, cnt_ref, run_ref)


def _ml_out(hh, og, w_out_bf, x, g1, route_args):
    B, S, D = x.shape
    tm = OUT_TILE
    row = lambda b, j: (b, j, 0)
    per_b = lambda b, j: (b, 0, 0)
    const2 = lambda b, j: (0, 0)
    r_in, r_shape, r_out, r_scratch = _route_specs(B, S, D, tm)
    return pl.pallas_call(
        _ml_out_kernel,
        out_shape=(jax.ShapeDtypeStruct((B, S, D), F32),) + r_shape,
        grid=(B, S // tm),
        in_specs=[
            pl.BlockSpec((1, tm, D), row),
            pl.BlockSpec((1, tm, D), row),
            pl.BlockSpec((D, D), const2),
            pl.BlockSpec((1, tm, D), row),
            pl.BlockSpec((1, 1, D), per_b),
        ] + r_in,
        out_specs=(pl.BlockSpec((1, tm, D), row),) + r_out,
        scratch_shapes=r_scratch,
        compiler_params=_cparams("arbitrary", "arbitrary"),
        name="mlstm_out_proj",
    )(hh, og, w_out_bf, x, g1, *route_args)


def _first_argmax(x, lane, width):
    mx = jnp.max(x, axis=-1, keepdims=True)
    idx = jnp.min(jnp.where(x == mx, lane, width), axis=-1, keepdims=True)
    return mx, idx


def _route_tile(x, gain_ref, sc_ref, sh_ref, w_ref, b_ref, h_ref, r_ref, rt_ref, cnt_ref, run_ref):
    tm = x.shape[0]

    @pl.when(jnp.logical_and(pl.program_id(0) == 0, pl.program_id(1) == 0))
    def _():
        run_ref[...] = jnp.zeros_like(run_ref)

    h = _modulated_norm(x, gain_ref[...], sc_ref[0], sh_ref[0])
    h_ref[0] = _pack_bf16_pairs(h)
    logits = _dot(h.astype(BF16), w_ref[...]) + b_ref[...]
    lane = lax.broadcasted_iota(jnp.int32, (1, LANES), 1)

    def pick(lgt):
        lg = jnp.where(lane < N_GROUPS, lgt, NEG_INF)
        g_max, g_sel = _first_argmax(lg, lane, LANES)
        pg = 1.0 / jnp.sum(jnp.exp(lg - g_max), axis=-1, keepdims=True)
        e_lane = lane - N_GROUPS
        in_grp = jnp.logical_and(e_lane >= g_sel * EXPERTS_PER_GROUP, e_lane < (g_sel + 1) * EXPERTS_PER_GROUP)
        le = jnp.where(in_grp, lgt, NEG_INF)
        v1, i1 = _first_argmax(le, lane, LANES)
        v2, i2 = _first_argmax(jnp.where(lane == i1, NEG_INF, le), lane, LANES)
        ratio = jnp.exp(v2 - v1)
        return i1 - N_GROUPS, i2 - N_GROUPS, pg / (1.0 + ratio), pg * ratio / (1.0 + ratio)

    e1, e2, w1, w2 = pick(logits)
    hot1 = lane == e1
    hot2 = lane == e2
    onehot = jnp.where(jnp.logical_or(hot1, hot2), 1.0, 0.0)
    r_i = lax.broadcasted_iota(jnp.int32, (tm, tm), 0)
    c_i = lax.broadcasted_iota(jnp.int32, (tm, tm), 1)
    before = jnp.where(c_i < r_i, 1.0, 0.0).astype(BF16)
    seen = _dot(before, onehot.astype(BF16)) + run_ref[...]
    rank1 = jnp.sum(jnp.where(hot1, seen, 0.0), axis=-1, keepdims=True)
    rank2 = jnp.sum(jnp.where(hot2, seen, 0.0), axis=-1, keepdims=True)
    run_ref[...] = run_ref[...] + jnp.sum(onehot, axis=0, keepdims=True)
    cnt_ref[...] = run_ref[...]

    out = jnp.where(lane == 0, e1.astype(F32), 0.0)
    out = jnp.where(lane == 1, e2.astype(F32), out)
    out = jnp.where(lane == 2, w1, out)
    out = jnp.where(lane == 3, w2, out)
    out = jnp.where(lane == 4, rank1, out)
    out = jnp.where(lane == 5, rank2, out)
    r_ref[0] = out
    rt_ref[0] = out.T[:8, :]


def _route_specs(B, S, D, tm):
    row = lambda b, j: (b, j, 0)
    per_b = lambda b, j: (b, 0, 0)
    const2 = lambda b, j: (0, 0)
    in_specs = [
        pl.BlockSpec((1, D), const2),
        pl.BlockSpec((1, 1, D), per_b),
        pl.BlockSpec((1, 1, D), per_b),
        pl.BlockSpec((D, LANES), const2),
        pl.BlockSpec((1, LANES), const2),
    ]
    out_shape = (
        jax.ShapeDtypeStruct((B, S, D // 2), jnp.int32),
        jax.ShapeDtypeStruct((B, S, LANES), F32),
        jax.ShapeDtypeStruct((B, 8, S), F32),
        jax.ShapeDtypeStruct((1, LANES), F32),
    )
    out_specs = (
        pl.BlockSpec((1, tm, D // 2), row),
        pl.BlockSpec((1, tm, LANES), row),
        pl.BlockSpec((1, 8, tm), lambda b, j: (b, 0, j)),
        pl.BlockSpec((1, LANES), const2),
    )
    return in_specs, out_shape, out_specs, [pltpu.VMEM((1, LANES), F32)]


def _experts_kernel(blk_e_ref, n_used_ref, next_e_ref, x_ref, wg_hbm, wu_hbm, wd_hbm, o_ref,
                    wg_f, wu_f, wd_f, wg_s, wu_s, wd_s, sem, *, layer):
    i = pl.program_id(0)
    used = i < n_used_ref[0]
    e = blk_e_ref[i]
    new_expert = jnp.logical_or(i == 0, e != blk_e_ref[jnp.maximum(i - 1, 0)])

    def fetch(expert):
        return (pltpu.make_async_copy(wg_hbm.at[layer, expert], wg_f, sem.at[0]),
                pltpu.make_async_copy(wu_hbm.at[layer, expert], wu_f, sem.at[1]),
                pltpu.make_async_copy(wd_hbm.at[layer, expert], wd_f, sem.at[2]))

    @pl.when(i == 0)
    def _():
        for cp in fetch(e):
            cp.start()

    @pl.when(jnp.logical_and(used, new_expert))
    def _():
        for cp in fetch(e):
            cp.wait()
        wg_s[...] = wg_f[...].astype(BF16)
        wu_s[...] = wu_f[...].astype(BF16)
        wd_s[...] = wd_f[...].astype(BF16)

        @pl.when(next_e_ref[i] >= 0)
        def _():
            for cp in fetch(next_e_ref[i]):
                cp.start()

    @pl.when(used)
    def _():
        x = _unpack_bf16_pairs(x_ref[...]).astype(BF16)
        a = _dot(x, wg_s[...])
        u = _dot(x, wu_s[...])
        act = a * jax.nn.sigmoid(a) * u
        o_ref[...] = _pack_bf16_pairs(_dot(act.astype(BF16), wd_s[...]))

    @pl.when(i >= n_used_ref[0])
    def _():
        o_ref[...] = jnp.zeros_like(o_ref)


def _experts(layer, blk_e, n_used, next_e, xs, w_gate, w_up, w_down):
    R = xs.shape[0]
    D = 2 * xs.shape[1]
    n_blk = R // MOE_BLOCK
    rows = lambda i, be, nu, ne: (i, 0)
    grid_spec = pltpu.PrefetchScalarGridSpec(
        num_scalar_prefetch=3,
        grid=(n_blk,),
        in_specs=[
            pl.BlockSpec((MOE_BLOCK, D // 2), rows),
            pl.BlockSpec(memory_space=pl.ANY),
            pl.BlockSpec(memory_space=pl.ANY),
            pl.BlockSpec(memory_space=pl.ANY),
        ],
        out_specs=pl.BlockSpec((MOE_BLOCK, D // 2), rows),
        scratch_shapes=[
            pltpu.VMEM((D, D_EXPERT), F32),
            pltpu.VMEM((D, D_EXPERT), F32),
            pltpu.VMEM((D_EXPERT, D), F32),
            pltpu.VMEM((D, D_EXPERT), BF16),
            pltpu.VMEM((D, D_EXPERT), BF16),
            pltpu.VMEM((D_EXPERT, D), BF16),
            pltpu.SemaphoreType.DMA((3,)),
        ],
    )
    return pl.pallas_call(
        functools.partial(_experts_kernel, layer=layer),
        out_shape=jax.ShapeDtypeStruct((R, D // 2), jnp.int32),
        grid_spec=grid_spec,
        compiler_params=_cparams("arbitrary"),
        name="moe_experts",
    )(blk_e, n_used, next_e, xs, w_gate, w_up, w_down)


def _residual_tile(x_ref, pending_refs):
    if not pending_refs:
        return x_ref[0]
    g_ref, y0_ref, y1_ref, r_ref = pending_refs
    r = r_ref[0]
    y = _unpack_bf16_pairs(y0_ref[0, 0]) * r[:, 2:3] + _unpack_bf16_pairs(y1_ref[0, 0]) * r[:, 3:4]
    return x_ref[0] + g_ref[0] * y


def _pending_specs(D, tm):
    return [
        pl.BlockSpec((1, 1, D), lambda b, j: (b, 0, 0)),
        pl.BlockSpec((1, 1, tm, D // 2), lambda b, j: (0, b, j, 0)),
        pl.BlockSpec((1, 1, tm, D // 2), lambda b, j: (1, b, j, 0)),
        pl.BlockSpec((1, tm, LANES), lambda b, j: (b, j, 0)),
    ]


def _combine_kernel(x_ref, g_ref, y0_ref, y1_ref, r_ref, o_ref):
    o_ref[0] = _residual_tile(x_ref, (g_ref, y0_ref, y1_ref, r_ref))


def _combine(x, pending):
    B, S, D = x.shape
    tm = TOKEN_TILE
    row = lambda b, j: (b, j, 0)
    return pl.pallas_call(
        _combine_kernel,
        out_shape=jax.ShapeDtypeStruct((B, S, D), F32),
        grid=(B, S // tm),
        in_specs=[pl.BlockSpec((1, tm, D), row)] + _pending_specs(D, tm),
        out_specs=pl.BlockSpec((1, tm, D), row),
        compiler_params=_cparams("parallel", "parallel"),
        name="moe_combine",
    )(x, *pending)


SC_CORES = 2
SC_SUBCORES = 16
SC_WORKERS = SC_CORES * SC_SUBCORES
SC_CHUNK = 64


def _sc_mesh():
    return plsc.VectorSubcoreMesh(core_axis_name="c", subcore_axis_name="s",
                                  num_cores=SC_CORES, num_subcores=SC_SUBCORES)


def _sc_scatter_rows(src, idx, n_out):
    T, W = src.shape
    per_w = T // SC_WORKERS
    nch = per_w // SC_CHUNK
    idx4 = idx.reshape(TOP_K, SC_WORKERS, nch, SC_CHUNK)

    @functools.partial(
        pl.kernel, mesh=_sc_mesh(),
        out_type=jax.ShapeDtypeStruct((n_out, W), src.dtype),
        scratch_types=[pltpu.VMEM((TOP_K, nch, SC_CHUNK), jnp.int32), pltpu.VMEM((SC_CHUNK, W), src.dtype)],
        name="sc_scatter_rows",
    )
    def body(src_hbm, idx_hbm, out_hbm, idx_v, rows_v):
        wid = lax.axis_index("s") * SC_CORES + lax.axis_index("c")
        for s in range(TOP_K):
            pltpu.sync_copy(idx_hbm.at[s, wid], idx_v.at[s])

        @pl.loop(0, nch)
        def _(i):
            pltpu.sync_copy(src_hbm.at[pl.ds(wid * per_w + i * SC_CHUNK, SC_CHUNK)], rows_v)
            for s in range(TOP_K):
                pltpu.sync_copy(rows_v, out_hbm.at[idx_v.at[s, i]])

    return body(src, idx4)


def _sc_gather_rows(table, idx):
    N = idx.shape[0]
    W = table.shape[1]
    per_w = N // SC_WORKERS
    nch = per_w // SC_CHUNK
    idx3 = idx.reshape(SC_WORKERS, nch, SC_CHUNK)

    @functools.partial(
        pl.kernel, mesh=_sc_mesh(),
        out_type=jax.ShapeDtypeStruct((N, W), table.dtype),
        scratch_types=[
            pltpu.VMEM((nch, SC_CHUNK), jnp.int32),
            pltpu.VMEM((2, SC_CHUNK, W), table.dtype),
            pltpu.SemaphoreType.DMA((2,)),
            pltpu.SemaphoreType.DMA((2,)),
        ],
        name="sc_gather_rows",
    )
    def body(table_hbm, idx_hbm, out_hbm, idx_v, rows_v, gather_sem, write_sem):
        wid = lax.axis_index("s") * SC_CORES + lax.axis_index("c")
        pltpu.sync_copy(idx_hbm.at[wid], idx_v)

        def gather(j, b):
            return pltpu.make_async_copy(table_hbm.at[idx_v.at[j]], rows_v.at[b], gather_sem.at[b])

        def write(j, b):
            return pltpu.make_async_copy(
                rows_v.at[b], out_hbm.at[pl.ds(wid * per_w + j * SC_CHUNK, SC_CHUNK)], write_sem.at[b])

        gather(0, 0).start()

        @pl.loop(0, nch, step=2)
        def _(i):
            for b in range(2):
                j = i + b

                @pl.when(j >= 1)
                def _():
                    write(j - 1, 1 - b).wait()

                @pl.when(j + 1 < nch)
                def _():
                    gather(j + 1, 1 - b).start()

                gather(j, b).wait()
                write(j, b).start()

        write(nch - 1, (nch - 1) % 2).wait()

    assert nch % 2 == 0
    return body(table, idx3)


def _moe_dispatch(route_t, counts, T):
    A = T * TOP_K
    counts = counts[0, :N_EXPERTS].astype(jnp.int32)
    blocks_per = (counts + MOE_BLOCK - 1) // MOE_BLOCK
    block_end = jnp.cumsum(blocks_per)
    block_start = block_end - blocks_per
    expert = jnp.swapaxes(route_t[:, :TOP_K, :], 0, 1).reshape(TOP_K, T).astype(jnp.int32)
    rank = jnp.swapaxes(route_t[:, 4:4 + TOP_K, :], 0, 1).reshape(TOP_K, T).astype(jnp.int32)
    onehot = expert[None] == jnp.arange(N_EXPERTS, dtype=jnp.int32)[:, None, None]
    start = jnp.sum(jnp.where(onehot, block_start[:, None, None], 0), axis=0)
    dest = start * MOE_BLOCK + rank
    n_blk = -(-A // MOE_BLOCK) + N_EXPERTS
    blk = jnp.arange(n_blk, dtype=jnp.int32)
    blk_e = jnp.minimum(jnp.sum(blk[:, None] >= block_end[None, :], axis=-1), N_EXPERTS - 1).astype(jnp.int32)
    n_used = block_end[-1]
    first = jnp.logical_and(blk < n_used, jnp.logical_or(blk == 0, blk_e != jnp.roll(blk_e, 1)))
    first_pos = jnp.where(first, blk, n_blk)
    next_pos = jnp.concatenate([lax.cummin(first_pos, axis=0, reverse=True)[1:], jnp.full((1,), n_blk, jnp.int32)])
    next_e = jnp.where(next_pos < n_blk, blk_e[jnp.minimum(next_pos, n_blk - 1)], -1).astype(jnp.int32)
    return dest, n_blk * MOE_BLOCK, blk_e, n_used.reshape(1).astype(jnp.int32), next_e


def _rope_tables(S):
    inv = 1.0 / (ROPE_THETA ** (jnp.arange(0, HEAD_DIM, 2, dtype=F32) / HEAD_DIM))
    ang = jnp.arange(S, dtype=F32)[:, None] * inv[None, :]
    cos, sin = jnp.cos(ang), jnp.sin(ang)
    cos_h = jnp.concatenate([cos, cos], axis=-1)
    sin_h = jnp.concatenate([-sin, sin], axis=-1)
    return jnp.tile(cos_h, (1, ATTN_HEADS)), jnp.tile(sin_h, (1, ATTN_HEADS))


def _pad_cols(w, width):
    return jnp.pad(w, ((0, 0), (0, width - w.shape[1])))


def kernel(x, c, ada_w, ada_b, norm_mix, norm_ffn, hy_w_in, hy_q_norm, hy_k_norm, hy_conv_w, hy_w_out, ml_w_in, ml_b_gates, ml_out_norm, ml_w_out, moe_w_group, moe_b_group, moe_w_expert, moe_b_expert, moe_w_gate, moe_w_up, moe_w_down):
    B, S, D = x.shape
    T = B * S
    cos_t, sin_t = _rope_tables(S)
    mod = _ada_modulation(c, ada_w, ada_b).reshape(DEPTH, B, 6, 1, D)
    r_i = np.arange(ATTN_WIDTH)
    grp = jnp.asarray((r_i[:, None] // HEAD_DIM) == (r_i[None, :] // HEAD_DIM), dtype=BF16)

    pending = ()
    for l in range(DEPTH):
        sh1, sc1, g1, sh2, sc2, g2 = [mod[l, :, i] for i in range(6)]
        gain1 = norm_mix[l].reshape(1, D)
        w_r = _pad_cols(jnp.concatenate([moe_w_group[l], moe_w_expert[l]], axis=1), LANES).astype(BF16)
        b_r = jnp.pad(jnp.concatenate([moe_b_group[l], moe_b_expert[l]]), (0, LANES - N_GROUPS - N_EXPERTS))
        route_args = (norm_ffn[l].reshape(1, D), sc2, sh2, w_r, b_r.reshape(1, LANES))
        j = l // 2
        if l % 2 == 0:
            w = hy_w_in[j]
            o = np.cumsum((0,) + (ATTN_WIDTH, HEAD_DIM, HEAD_DIM, IDX_HEADS * IDX_DIM, IDX_DIM, IDX_HEADS,
                                  CONV_WIDTH, CONV_WIDTH, CONV_WIDTH))
            wq, wk, wv, wiq, wik, wiw, wbg, wcg, wu = [w[:, o[i]:o[i + 1]] for i in range(9)]
            w_pad = jnp.concatenate(
                [wk, wv, _pad_cols(jnp.concatenate([wik, wiw], axis=1), LANES), wq, wiq, wbg, wcg, wu],
                axis=1).astype(BF16)
            qn_t = jnp.tile(hy_q_norm[j], ATTN_HEADS).reshape(1, ATTN_WIDTH)
            kn_t = jnp.tile(hy_k_norm[j], LANES // HEAD_DIM).reshape(1, LANES)
            outs = _hyb_in(x, pending, gain1, sc1, sh1, w_pad, cos_t, sin_t, qn_t, kn_t, grp)
            qt, iqt, bcu, kv, kvt, ik, iwt = outs[:7]
            if pending:
                x = outs[7]
            y_attn = _dsa_attention_t(qt, iqt, iwt, kv, kvt, ik)
            x, h2, route, route_t, counts =_hyb_out(y_attn, bcu, hy_conv_w[j], hy_w_out[j].astype(BF16), x, g1, route_args)
        else:
            w = ml_w_in[j]
            hq = ML_HEADS * ML_QK_DIM
            hv = ML_HEADS * ML_V_DIM
            wq, wk, wv = w[:, :hq], w[:, hq:2 * hq], w[:, 2 * hq:2 * hq + hv]
            wg = w[:, 2 * hq + hv:2 * hq + hv + 2 * ML_HEADS]
            wo = w[:, 2 * hq + hv + 2 * ML_HEADS:]
            w_pad = jnp.concatenate([_pad_cols(wg, LANES), wk, wq, wv, wo], axis=1).astype(BF16)
            gate_bias = jnp.pad(ml_b_gates[j], (0, LANES - 2 * ML_HEADS)).reshape(1, LANES)
            q, k, v, og, g_cols, g_rows, x = _ml_in(x, pending, gain1, sc1, sh1, w_pad, gate_bias)
            assert ML_STEP_HEADS == ML_HEADS
            hh = _mlstm(q, k, v, g_rows[:, None], g_cols[:, None], ml_out_norm[j].reshape(1, hv))
            x, h2, route, route_t, counts =_ml_out(hh, og, ml_w_out[j].astype(BF16), x, g1, route_args)

        dest, n_rows, blk_e, n_used, next_e = _moe_dispatch(route_t, counts, T)
        xs = _sc_scatter_rows(h2.reshape(T, D // 2), dest, n_rows)
        ys = _experts(l, blk_e, n_used, next_e, xs, moe_w_gate, moe_w_up, moe_w_down)
        y01 = _sc_gather_rows(ys, dest.reshape(TOP_K * T)).reshape(TOP_K, B, S, D // 2)
        pending = (g2, y01, y01, route)
    return _combine(x, pending)
```

```python
import functools

import numpy as np
import jax
import jax.numpy as jnp
from jax import lax
from jax.experimental import pallas as pl
from jax.experimental.pallas import tpu as pltpu
from jax.experimental.pallas import tpu_sc as plsc

F32 = jnp.float32
BF16 = jnp.bfloat16
HIGHEST = lax.Precision.HIGHEST

D_MODEL = 1024
DEPTH = 4
ATTN_HEADS = 8
HEAD_DIM = 64
ATTN_WIDTH = ATTN_HEADS * HEAD_DIM
IDX_HEADS = 8
IDX_DIM = 64
INDEX_TOPK = 256
Q_BLOCK = 256
ROPE_THETA = 10000.0
CONV_WIDTH = D_MODEL - ATTN_WIDTH
CONV_K = 3
ML_HEADS = 8
ML_QK_DIM = 64
ML_V_DIM = 128
N_GROUPS = 4
EXPERTS_PER_GROUP = 8
N_EXPERTS = N_GROUPS * EXPERTS_PER_GROUP
TOP_K = 2
D_EXPERT = 512
MOE_BLOCK = 512
NORM_EPS = 1e-6

LANES = 128
VMEM_LIMIT = 56 * 1024 * 1024
TOKEN_TILE = 256
OUT_TILE = 512
ML_CHUNK = 256
ML_STEP_HEADS = 8
NEG_INF = float("-inf")


def _cparams(*sem):
    return pltpu.CompilerParams(dimension_semantics=sem, vmem_limit_bytes=VMEM_LIMIT)


def _dot(a, b):
    return jnp.dot(a, b, preferred_element_type=F32)


def _pack_bf16_pairs(x):
    bits = lax.bitcast_convert_type(x.astype(BF16).astype(F32), jnp.uint32)
    half = bits.shape[1] // 2
    packed = (bits[:, :half] >> 16) | (bits[:, half:] & jnp.uint32(0xFFFF0000))
    return lax.bitcast_convert_type(packed, jnp.int32)


def _unpack_bf16_pairs(words):
    words = lax.bitcast_convert_type(words, jnp.uint32)
    return jnp.concatenate(
        [lax.bitcast_convert_type(words << 16, F32),
         lax.bitcast_convert_type(words & jnp.uint32(0xFFFF0000), F32)], axis=1)


def _split_dot(a_f32, b_bf16):
    hi = a_f32.astype(BF16)
    lo = (a_f32 - hi.astype(F32)).astype(BF16)
    return _dot(hi, b_bf16) + _dot(lo, b_bf16)


def _ada_kernel(c_ref, w_ref, b_ref, o_ref):
    c = c_ref[...]
    ca = c * jax.nn.sigmoid(c)
    o_ref[0] = jnp.dot(ca, w_ref[0], precision=HIGHEST, preferred_element_type=F32) + b_ref[0]


def _ada_modulation(c, ada_w, ada_b):
    B, D = c.shape
    n_col = ada_w.shape[-1] // D
    return pl.pallas_call(
        _ada_kernel,
        out_shape=jax.ShapeDtypeStruct((DEPTH, B, n_col * D), F32),
        grid=(DEPTH, n_col),
        in_specs=[
            pl.BlockSpec((B, D), lambda l, j: (0, 0)),
            pl.BlockSpec((1, D, D), lambda l, j: (l, 0, j)),
            pl.BlockSpec((1, 1, D), lambda l, j: (l, 0, j)),
        ],
        out_specs=pl.BlockSpec((1, B, D), lambda l, j: (l, 0, j)),
        compiler_params=_cparams("parallel", "parallel"),
        name="ada_modulation",
    )(c, ada_w, ada_b.reshape(DEPTH, 1, n_col * D))


def _modulated_norm(x, gain, scale, shift):
    y = x * lax.rsqrt(jnp.mean(x * x, axis=-1, keepdims=True) + NORM_EPS)
    return y * gain * (1.0 + scale) + shift


def _rope(x, cos, sin_signed, first_half):
    w = x.shape[-1]
    partner = jnp.where(first_half, pltpu.roll(x, w - HEAD_DIM // 2, 1), pltpu.roll(x, HEAD_DIM // 2, 1))
    return x * cos + partner * sin_signed


HYB_COLS = 5 * 512 + 2 * LANES


def _hyb_in_kernel(*refs, n_pending):
    x_ref, pending_refs, refs = refs[0], refs[1:1 + n_pending], refs[1 + n_pending:]
    (gain_ref, sc_ref, sh_ref, w_ref, cos_ref, sin_ref, qn_ref, kn_ref, grp_ref,
     qt_ref, iqt_ref, bcu_ref, kv_ref, kvt_ref, ik_ref, iwt_ref) = refs[:16]
    x = _residual_tile(x_ref, pending_refs)
    if n_pending:
        refs[16][0] = x
    h = _modulated_norm(x, gain_ref[...], sc_ref[0], sh_ref[0])
    hb = h.astype(BF16)
    p_small = _dot(hb, w_ref[:, 0:2 * LANES])
    p_q = _dot(hb, w_ref[:, 2 * LANES:2 * LANES + 512])
    p_iq = _dot(hb, w_ref[:, 2 * LANES + 512:2 * LANES + 1024])
    cos = cos_ref[...]
    sin = sin_ref[...]
    lane = lax.broadcasted_iota(jnp.int32, (1, ATTN_WIDTH), 1)
    first_half = (lane % HEAD_DIM) < (HEAD_DIM // 2)
    fh128 = first_half[:, :LANES]
    lane128 = lane[:, :LANES]

    kv = p_small[:, :LANES]
    is_k = lane128 < HEAD_DIM
    kk = jnp.where(is_k, kv, 0.0)
    ms_k = jnp.sum(kk * kk, axis=-1, keepdims=True) * (1.0 / HEAD_DIM)
    kn = kv * lax.rsqrt(ms_k + NORM_EPS) * kn_ref[...]
    kr = _rope(kn, cos[:, :LANES], sin[:, :LANES], fh128)
    kv = jnp.where(is_k, kr, kv)
    kv_ref[0] = kv.astype(BF16)
    kvt_ref[0, 0] = kv.T.astype(BF16)

    sm = p_small[:, LANES:]
    ikr = _rope(sm, cos[:, :LANES], sin[:, :LANES], fh128)
    ik_ref[0] = jnp.where(is_k, ikr, 0.0).astype(BF16)
    iwt_ref[0] = sm.T

    ms = _split_dot(p_q * p_q, grp_ref[...]) * (1.0 / HEAD_DIM)
    q = p_q * lax.rsqrt(ms + NORM_EPS) * qn_ref[...]
    qt_ref[0] = (_rope(q, cos, sin, first_half) * (HEAD_DIM ** -0.5)).T.astype(BF16)
    iqt_ref[0] = (_rope(p_iq, cos, sin, first_half) * (IDX_DIM ** -0.5)).T.astype(BF16)

    bcu_ref[0] = _dot(hb, w_ref[:, 2 * LANES + 1024:])


def _hyb_in(x, pending, gain, sc, sh, w_pad, cos_t, sin_t, qn_t, kn_t, grp):
    B, S, D = x.shape
    tm = TOKEN_TILE
    row = lambda b, j: (b, j, 0)
    per_b = lambda b, j: (b, 0, 0)
    const2 = lambda b, j: (0, 0)
    tab = lambda b, j: (j, 0)
    col = lambda b, j: (b, 0, j)
    assert tm == DSA_KEY_CHUNK
    x_out_shape = (jax.ShapeDtypeStruct((B, S, D), F32),) if pending else ()
    x_out_spec = (pl.BlockSpec((1, tm, D), row),) if pending else ()
    return pl.pallas_call(
        functools.partial(_hyb_in_kernel, n_pending=len(pending)),
        out_shape=(
            jax.ShapeDtypeStruct((B, 512, S), BF16),
            jax.ShapeDtypeStruct((B, 512, S), BF16),
            jax.ShapeDtypeStruct((B, S, 1536), F32),
            jax.ShapeDtypeStruct((B, S, LANES), BF16),
            jax.ShapeDtypeStruct((B, S // tm, LANES, tm), BF16),
            jax.ShapeDtypeStruct((B, S, LANES), BF16),
            jax.ShapeDtypeStruct((B, LANES, S), F32),
        ) + x_out_shape,
        grid=(B, S // tm),
        in_specs=[pl.BlockSpec((1, tm, D), row)] + (_pending_specs(D, tm) if pending else []) + [
            pl.BlockSpec((1, D), const2),
            pl.BlockSpec((1, 1, D), per_b),
            pl.BlockSpec((1, 1, D), per_b),
            pl.BlockSpec((D, HYB_COLS), const2),
            pl.BlockSpec((tm, 512), tab),
            pl.BlockSpec((tm, 512), tab),
            pl.BlockSpec((1, 512), const2),
            pl.BlockSpec((1, LANES), const2),
            pl.BlockSpec((512, 512), const2),
        ],
        out_specs=(
            pl.BlockSpec((1, 512, tm), col),
            pl.BlockSpec((1, 512, tm), col),
            pl.BlockSpec((1, tm, 1536), row),
            pl.BlockSpec((1, tm, LANES), row),
            pl.BlockSpec((1, 1, LANES, tm), lambda b, j: (b, j, 0, 0)),
            pl.BlockSpec((1, tm, LANES), row),
            pl.BlockSpec((1, LANES, tm), col),
        ) + x_out_spec,
        compiler_params=_cparams("parallel", "parallel"),
        name="hybrid_in_proj",
    )(x, *pending, gain, sc, sh, w_pad, cos_t, sin_t, qn_t, kn_t, grp)


DSA_KEY_CHUNK = 256
F32_LOWEST = float(np.finfo(np.float32).min)


def _fold8(x, op):
    parts = x.reshape(x.shape[0] // 8, 8, x.shape[1])
    while parts.shape[0] > 1:
        half = parts.shape[0] // 2
        assert parts.shape[0] == 2 * half
        parts = op(parts[:half], parts[half:])
    return parts[0]


def _col_reduce(x, op):
    t = _fold8(x, op)
    for shift in (4, 2, 1):
        t = op(t, pltpu.roll(t, shift, 0))
    return t[0:1, :]


def _dsa_t_kernel(qt_ref, iqt_ref, iwt_ref, kv_ref, kvt_ref, ik_ref, o_ref, sc_ref, bias_ref, acc_ref):
    CK = DSA_KEY_CHUNK
    QB = Q_BLOCK
    qb = pl.program_id(1)
    nk = (qb * QB + QB + CK - 1) // CK
    kf = float(INDEX_TOPK)
    qpos = qb * QB + lax.broadcasted_iota(jnp.int32, (1, QB), 1)
    krow = lax.broadcasted_iota(jnp.int32, (CK, 1), 0)
    w_idx = iwt_ref[0, IDX_DIM:IDX_DIM + IDX_HEADS, :] * (IDX_HEADS ** -0.5)

    def rows(c):
        return pl.ds(pl.multiple_of(c * CK, CK), CK)

    def heads_on_lanes(ref, width):
        return jnp.concatenate([ref[0, hd * width:(hd + 1) * width, :] for hd in range(ref.shape[1] // width)], axis=1)

    def head_lanes(hd):
        return slice(hd * QB, (hd + 1) * QB)

    iq_wide = heads_on_lanes(iqt_ref, IDX_DIM)
    w_wide = jnp.concatenate([w_idx[hd:hd + 1, :] for hd in range(IDX_HEADS)], axis=1)

    def score_chunk(c, carry):
        mx, mn = carry
        ikc = ik_ref[0, rows(c), :][:, :IDX_DIM]
        s_all = jnp.maximum(_dot(ikc, iq_wide), 0.0) * w_wide
        acc = s_all[:, head_lanes(0)]
        for hd in range(1, IDX_HEADS):
            acc = acc + s_all[:, head_lanes(hd)]
        causal = (c * CK + krow) <= qpos
        sc_ref[rows(c), :] = jnp.where(causal, acc, NEG_INF)
        mx = jnp.maximum(mx, _fold8(jnp.where(causal, acc, NEG_INF), jnp.maximum))
        mn = jnp.minimum(mn, _fold8(jnp.where(causal, acc, jnp.inf), jnp.minimum))
        return mx, mn

    mx8, mn8 = lax.fori_loop(0, nk, score_chunk,
                             (jnp.full((8, QB), NEG_INF, F32), jnp.full((8, QB), jnp.inf, F32)))
    row_max = jnp.max(mx8, axis=0, keepdims=True)
    row_min = jnp.min(mn8, axis=0, keepdims=True)

    @pl.when(nk % 2 == 1)
    def _():
        sc_ref[rows(nk), :] = jnp.full((CK, QB), NEG_INF, F32)

    n_pairs = (nk + 1) // 2

    def pair_rows(c):
        return pl.ds(pl.multiple_of(c * (2 * CK), 2 * CK), 2 * CK)

    def count(pred):
        def body(c, part):
            return part + _fold8(jnp.where(pred(sc_ref[pair_rows(c), :]), 1.0, 0.0), jnp.add)
        part = lax.fori_loop(0, n_pairs, body, jnp.zeros((8, QB), F32))
        return jnp.sum(part, axis=0, keepdims=True)

    @pl.when(qb * QB + QB <= INDEX_TOPK)
    def _():
        def body(c, carry):
            bias_ref[rows(c), :] = jnp.where(sc_ref[rows(c), :] > NEG_INF, 0.0, NEG_INF)
            return carry
        lax.fori_loop(0, nk, body, 0)

    @pl.when(qb * QB + QB > INDEX_TOPK)
    def _():
        def bisect(_, carry):
            lo, hi, c_lo, c_hi = carry
            mid = 0.5 * lo + 0.5 * jnp.minimum(hi, row_max)
            cnt = count(lambda x: x >= mid)
            ge = cnt >= kf
            return (jnp.where(ge, mid, lo), jnp.where(ge, hi, mid),
                    jnp.where(ge, cnt, c_lo), jnp.where(ge, c_hi, cnt))

        n_adm = (qpos + 1).astype(F32)
        lo, hi, c_lo, c_hi = lax.fori_loop(
            0, 18, bisect, (row_min, jnp.full((1, QB), jnp.inf, F32), n_adm, jnp.zeros((1, QB), F32)))

        def refine_cond(carry):
            it, _, _, _, done = carry
            return jnp.logical_and(it < nk * CK, jnp.min(done) < 0.5)

        def refine(carry):
            it, hi, c_hi, thr, done = carry

            def edges(c, part):
                up, dn = part
                x = sc_ref[pair_rows(c), :]
                return (jnp.maximum(up, _fold8(jnp.where(x < hi, x, NEG_INF), jnp.maximum)),
                        jnp.minimum(dn, _fold8(jnp.where(x >= lo, x, jnp.inf), jnp.minimum)))

            up8, dn8 = lax.fori_loop(0, n_pairs, edges,
                                     (jnp.full((8, QB), NEG_INF, F32), jnp.full((8, QB), jnp.inf, F32)))
            m_up = jnp.max(up8, axis=0, keepdims=True)
            m_dn = jnp.min(dn8, axis=0, keepdims=True)
            from_hi = c_hi == kf - 1.0
            from_lo = c_lo == kf
            thr = jnp.where(done > 0.5, thr, jnp.where(from_hi, m_up, m_dn))
            done = jnp.where(jnp.logical_or(from_hi, from_lo), 1.0, done)

            def step_down(args):
                hi, c_hi, thr, done = args
                open_ = done < 0.5
                cnt = count(lambda x: x >= m_up)
                hit = jnp.logical_and(open_, cnt >= kf)
                moved = jnp.logical_and(open_, cnt < kf)
                return (jnp.where(moved, m_up, hi), jnp.where(moved, cnt, c_hi), jnp.where(hit, m_up, thr),
                        jnp.where(hit, 1.0, done))

            hi, c_hi, thr, done = lax.cond(jnp.min(done) > 0.5, lambda args: args, step_down, (hi, c_hi, thr, done))
            return it + 1, hi, c_hi, thr, done

        _, _, _, thr, _ = lax.while_loop(
            refine_cond, refine, (jnp.int32(0), hi, c_hi, row_max, jnp.zeros((1, QB), F32)))

        def body(c, sel):
            keep = sc_ref[rows(c), :] >= thr
            bias_ref[rows(c), :] = jnp.where(keep, 0.0, NEG_INF)
            return sel + _fold8(jnp.where(keep, 1.0, 0.0), jnp.add)

        n_sel = jnp.sum(lax.fori_loop(0, nk, body, jnp.zeros((8, QB), F32)), axis=0, keepdims=True)

        @pl.when(jnp.max(n_sel) > kf + 0.5)
        def _():
            need = kf - count(lambda x: x > thr)
            r_i = lax.broadcasted_iota(jnp.int32, (CK, CK), 0)
            c_i = lax.broadcasted_iota(jnp.int32, (CK, CK), 1)
            lower = jnp.where(c_i <= r_i, 1.0, 0.0).astype(BF16)

            def body(c, seen):
                x = sc_ref[rows(c), :]
                eq = x == thr
                eq_f = jnp.where(eq, 1.0, 0.0)
                rank = _dot(lower, eq_f.astype(BF16)) + seen
                keep = jnp.logical_or(x > thr, jnp.logical_and(eq, rank <= need))
                bias_ref[rows(c), :] = jnp.where(keep, 0.0, NEG_INF)
                return seen + jnp.sum(eq_f, axis=0, keepdims=True)

            lax.fori_loop(0, nk, body, jnp.zeros((1, QB), F32))

    acc_ref[...] = jnp.zeros_like(acc_ref)
    q_wide = heads_on_lanes(qt_ref, HEAD_DIM)

    def attend_chunk(c, carry):
        m_old, l_old = carry
        kc = kv_ref[0, rows(c), :][:, :HEAD_DIM]
        vt = kvt_ref[0, c, HEAD_DIM:, :]
        bias = bias_ref[rows(c), :]
        logits = _dot(kc, q_wide) + jnp.concatenate([bias] * ATTN_HEADS, axis=1)
        m_new = jnp.maximum(m_old, _col_reduce(logits, jnp.maximum))
        alpha = jnp.exp(m_old - m_new)
        p = jnp.exp(logits - m_new)
        l_new = alpha * l_old + _col_reduce(p, jnp.add)
        acc_ref[...] = alpha * acc_ref[...] + _dot(vt, p.astype(BF16))
        return m_new, l_new

    _, l_all = lax.fori_loop(0, nk, attend_chunk, (jnp.full((1, ATTN_HEADS * QB), -1e30, F32),
                                                   jnp.zeros((1, ATTN_HEADS * QB), F32)))
    out_t = acc_ref[...] / l_all
    o_ref[0] = jnp.concatenate([out_t[:, head_lanes(hd)] for hd in range(ATTN_HEADS)], axis=0).T


def _dsa_attention_t(qt, iqt, iwt, kv, kvt, ik):
    B, _, S = qt.shape
    col = lambda b, j: (b, 0, j)
    per_b = lambda b, j: (b, 0, 0)
    return pl.pallas_call(
        _dsa_t_kernel,
        out_shape=jax.ShapeDtypeStruct((B, S, ATTN_WIDTH), F32),
        grid=(B, S // Q_BLOCK),
        in_specs=[
            pl.BlockSpec((1, ATTN_WIDTH, Q_BLOCK), col),
            pl.BlockSpec((1, IDX_HEADS * IDX_DIM, Q_BLOCK), col),
            pl.BlockSpec((1, LANES, Q_BLOCK), col),
            pl.BlockSpec((1, S, LANES), per_b),
            pl.BlockSpec((1, S // DSA_KEY_CHUNK, LANES, DSA_KEY_CHUNK), lambda b, j: (b, 0, 0, 0)),
            pl.BlockSpec((1, S, LANES), per_b),
        ],
        out_specs=pl.BlockSpec((1, Q_BLOCK, ATTN_WIDTH), lambda b, j: (b, j, 0)),
        scratch_shapes=[
            pltpu.VMEM((S, Q_BLOCK), F32),
            pltpu.VMEM((S, Q_BLOCK), F32),
            pltpu.VMEM((HEAD_DIM, ATTN_HEADS * Q_BLOCK), F32),
        ],
        compiler_params=_cparams("parallel", "parallel"),
        name="dsa_attention",
    )(qt, iqt, iwt, kv, kvt, ik)


def _hyb_out_kernel(ya_ref, bcu_ref, halo_ref, cw_ref, w_ref, x_ref, g_ref,
                    gain2_ref, sc2_ref, sh2_ref, wr_ref, br_ref, o_ref, h_ref, r_ref, rt_ref, cnt_ref, run_ref):
    j = pl.program_id(1)
    tm = ya_ref.shape[1]
    bcu = bcu_ref[0]
    bg = bcu[:, 0:512]
    z = bcu[:, 512:1024] * bcu[:, 1024:1536]
    halo = halo_ref[0]
    zh = halo[:, 512:1024] * halo[:, 1024:1536]
    zh = jnp.where(j > 0, zh, 0.0)
    row = lax.broadcasted_iota(jnp.int32, (tm, 1), 0)
    z1 = jnp.where(row >= 1, pltpu.roll(z, 1, 0), zh[7:8, :])
    z2 = jnp.where(row >= 2, pltpu.roll(z, 2, 0), jnp.where(row == 1, zh[7:8, :], zh[6:7, :]))
    cw = cw_ref[...]
    y_conv = bg * (z2 * cw[0:1, :] + z1 * cw[1:2, :] + z * cw[2:3, :])
    y = _dot(ya_ref[0].astype(BF16), w_ref[0:512, :]) + _dot(y_conv.astype(BF16), w_ref[512:1024, :])
    x_new = x_ref[0] + g_ref[0] * y
    o_ref[0] = x_new
    _route_tile(x_new, gain2_ref, sc2_ref, sh2_ref, wr_ref, br_ref, h_ref, r_ref, rt_ref, cnt_ref, run_ref)


def _hyb_out(y_attn, bcu, conv_w, w_out_bf, x, g1, route_args):
    B, S, D = x.shape
    tm = OUT_TILE
    row = lambda b, j: (b, j, 0)
    per_b = lambda b, j: (b, 0, 0)
    const2 = lambda b, j: (0, 0)
    halo = lambda b, j: (b, jnp.maximum(j * (tm // 8) - 1, 0), 0)
    r_in, r_shape, r_out, r_scratch = _route_specs(B, S, D, tm)
    return pl.pallas_call(
        _hyb_out_kernel,
        out_shape=(jax.ShapeDtypeStruct((B, S, D), F32),) + r_shape,
        grid=(B, S // tm),
        in_specs=[
            pl.BlockSpec((1, tm, 512), row),
            pl.BlockSpec((1, tm, 1536), row),
            pl.BlockSpec((1, 8, 1536), halo),
            pl.BlockSpec((CONV_K, CONV_WIDTH), const2),
            pl.BlockSpec((D, D), const2),
            pl.BlockSpec((1, tm, D), row),
            pl.BlockSpec((1, 1, D), per_b),
        ] + r_in,
        out_specs=(pl.BlockSpec((1, tm, D), row),) + r_out,
        scratch_shapes=r_scratch,
        compiler_params=_cparams("arbitrary", "arbitrary"),
        name="hybrid_out_proj",
    )(y_attn, bcu, bcu, conv_w, w_out_bf, x, g1, *route_args)


ML_COLS = 512 + 512 + 1024 + 1024 + LANES


def _ml_in_kernel(x_ref, g2_ref, y0_ref, y1_ref, r_ref, gain_ref, sc_ref, sh_ref, w_ref, bias_ref,
                  q_ref, k_ref, v_ref, og_ref, gt_ref, gtt_ref, xo_ref):
    x = _residual_tile(x_ref, (g2_ref, y0_ref, y1_ref, r_ref))
    xo_ref[0] = x
    h = _modulated_norm(x, gain_ref[...], sc_ref[0], sh_ref[0])
    hb = h.astype(BF16)
    gates = _dot(hb, w_ref[:, 0:LANES]) + bias_ref[...]
    gt_ref[0, 0] = gates
    gtt_ref[0, 0] = gates.T[:2 * ML_HEADS, :]
    k_ref[0, 0] = _dot(hb, w_ref[:, LANES:LANES + 512]).T.astype(BF16)
    q_ref[0] = (_dot(hb, w_ref[:, LANES + 512:LANES + 1024]) * (ML_QK_DIM ** -0.5)).astype(BF16)
    v_ref[0] = _dot(hb, w_ref[:, LANES + 1024:LANES + 2048]).astype(BF16)
    og_ref[0] = _dot(hb, w_ref[:, LANES + 2048:])


def _ml_in(x, pending, gain, sc, sh, w_pad, gate_bias):
    B, S, D = x.shape
    tm = TOKEN_TILE
    assert tm == ML_CHUNK
    row = lambda b, j: (b, j, 0)
    per_b = lambda b, j: (b, 0, 0)
    const2 = lambda b, j: (0, 0)
    return pl.pallas_call(
        _ml_in_kernel,
        out_shape=(
            jax.ShapeDtypeStruct((B, S, 512), BF16),
            jax.ShapeDtypeStruct((B, S // tm, 512, tm), BF16),
            jax.ShapeDtypeStruct((B, S, 1024), BF16),
            jax.ShapeDtypeStruct((B, S, 1024), F32),
            jax.ShapeDtypeStruct((B, S // tm, tm, LANES), F32),
            jax.ShapeDtypeStruct((B, S // tm, 2 * ML_HEADS, tm), F32),
            jax.ShapeDtypeStruct((B, S, D), F32),
        ),
        grid=(B, S // tm),
        in_specs=[pl.BlockSpec((1, tm, D), row)] + _pending_specs(D, tm) + [
            pl.BlockSpec((1, D), const2),
            pl.BlockSpec((1, 1, D), per_b),
            pl.BlockSpec((1, 1, D), per_b),
            pl.BlockSpec((D, ML_COLS), const2),
            pl.BlockSpec((1, LANES), const2),
        ],
        out_specs=(
            pl.BlockSpec((1, tm, 512), row),
            pl.BlockSpec((1, 1, 512, tm), lambda b, j: (b, j, 0, 0)),
            pl.BlockSpec((1, tm, 1024), row),
            pl.BlockSpec((1, tm, 1024), row),
            pl.BlockSpec((1, 1, tm, LANES), lambda b, j: (b, j, 0, 0)),
            pl.BlockSpec((1, 1, 2 * ML_HEADS, tm), lambda b, j: (b, j, 0, 0)),
            pl.BlockSpec((1, tm, D), row),
        ),
        compiler_params=_cparams("parallel", "parallel"),
        name="mlstm_in_proj",
    )(x, *pending, gain, sc, sh, w_pad, gate_bias)


def _log_sigmoid(f):
    return jnp.minimum(f, 0.0) - jnp.log1p(jnp.exp(-jnp.abs(f)))


def _split3(x):
    a = x.astype(BF16)
    r = x - a.astype(F32)
    b = r.astype(BF16)
    c = (r - b.astype(F32)).astype(BF16)
    return a, b, c


def _twice(a):
    return jnp.concatenate([a, a], axis=1)


def _mlstm_kernel(q_ref, kt_ref, v_ref, grow_ref, gcol_ref, gain_ref, o_ref, c_ref, m_ref):
    L = ML_CHUNK
    HP = ML_STEP_HEADS
    S = q_ref.shape[1]
    c_ref[...] = jnp.zeros_like(c_ref)
    m_ref[...] = jnp.zeros_like(m_ref)

    def chunk(c, carry):
        r0 = pl.multiple_of(c * L, L)
        r_i = lax.broadcasted_iota(jnp.int32, (L, L), 0)
        c_i = lax.broadcasted_iota(jnp.int32, (L, L), 1)
        tril = c_i <= r_i
        lower = jnp.where(tril, 1.0, 0.0).astype(BF16)
        upper = jnp.where(r_i <= c_i, 1.0, 0.0).astype(BF16)
        e_r = lax.broadcasted_iota(jnp.int32, (LANES, HP * LANES), 0)
        e_c = lax.broadcasted_iota(jnp.int32, (LANES, HP * LANES), 1)
        pick = jnp.where(e_r == HP + e_c // LANES, 1.0, 0.0).astype(BF16)
        rows = grow_ref[0, 0, c]
        cols = gcol_ref[0, 0, c]
        b_rows = sum(_dot(p, upper) for p in _split3(_log_sigmoid(rows)))
        b_cols = sum(_dot(lower, p) for p in _split3(_log_sigmoid(cols)))
        b_colr = sum(_dot(p, pick) for p in _split3(b_cols))
        lane = lax.broadcasted_iota(jnp.int32, (1, L), 1)
        b_last_all = jnp.sum(jnp.where(lane == L - 1, b_rows, 0.0), axis=-1, keepdims=True)
        ones_v = jnp.ones((L, ML_V_DIM), BF16)
        for hh in range(HP):
            q = q_ref[0, pl.ds(r0, L), hh * ML_QK_DIM:(hh + 1) * ML_QK_DIM]
            kt = kt_ref[0, c, hh * ML_QK_DIM:(hh + 1) * ML_QK_DIM, :]
            v = v_ref[0, pl.ds(r0, L), hh * ML_V_DIM:(hh + 1) * ML_V_DIM]
            vx = jnp.concatenate([v, ones_v], axis=1)
            i_row = rows[hh:hh + 1, :]
            b_row = b_rows[HP + hh:HP + hh + 1, :]
            b_last = b_last_all[HP + hh:HP + hh + 1, :]
            b_col = b_colr[:, hh * LANES:(hh + 1) * LANES]
            m_prev = m_ref[hh]
            ctn = c_ref[hh]

            dmat = jnp.where(tril, _twice(b_col) - b_row + i_row, NEG_INF)
            inter = b_col + m_prev
            m_t = jnp.maximum(inter, jnp.max(dmat, axis=-1, keepdims=True))
            w_intra = jnp.exp(dmat - _twice(m_t))
            w_inter = jnp.exp(inter - m_t)
            intra = (w_intra * _dot(q, kt)).astype(BF16)
            tot = _twice(w_inter) * _dot(q, ctn.astype(BF16)) + _dot(intra, vx)
            num = tot[:, :ML_V_DIM]
            den = tot[:, ML_V_DIM:]
            hc = num / jnp.maximum(jnp.abs(den), jnp.exp(-m_t))
            y = hc * lax.rsqrt(jnp.mean(hc * hc, axis=-1, keepdims=True) + NORM_EPS)
            o_ref[0, pl.ds(r0, L), hh * ML_V_DIM:(hh + 1) * ML_V_DIM] = (
                y * gain_ref[:, hh * ML_V_DIM:(hh + 1) * ML_V_DIM])

            g_row = b_last - b_row + i_row
            m_new = jnp.maximum(b_last + m_prev, jnp.max(g_row, axis=-1, keepdims=True))
            decay = jnp.exp(b_last + m_prev - m_new)
            kw = (kt.astype(F32) * jnp.exp(g_row - _twice(m_new))).astype(BF16)
            c_ref[hh] = _twice(decay) * ctn + _dot(kw, vx)
            m_ref[hh] = m_new
        return carry

    lax.fori_loop(0, S // L, chunk, 0)


def _mlstm(q, kt, v, g_rows, g_cols, out_gain):
    B, S, _ = q.shape
    assert ML_CHUNK == 2 * LANES
    nc = S // ML_CHUNK
    hp = ML_STEP_HEADS
    return pl.pallas_call(
        _mlstm_kernel,
        out_shape=jax.ShapeDtypeStruct((B, S, ML_HEADS * ML_V_DIM), F32),
        grid=(B, ML_HEADS // hp),
        in_specs=[
            pl.BlockSpec((1, S, hp * ML_QK_DIM), lambda b, p: (b, 0, p)),
            pl.BlockSpec((1, nc, hp * ML_QK_DIM, ML_CHUNK), lambda b, p: (b, 0, p, 0)),
            pl.BlockSpec((1, S, hp * ML_V_DIM), lambda b, p: (b, 0, p)),
            pl.BlockSpec((1, 1, nc, 2 * hp, ML_CHUNK), lambda b, p: (b, p, 0, 0, 0)),
            pl.BlockSpec((1, 1, nc, ML_CHUNK, LANES), lambda b, p: (b, p, 0, 0, 0)),
            pl.BlockSpec((1, hp * ML_V_DIM), lambda b, p: (0, p)),
        ],
        out_specs=pl.BlockSpec((1, S, hp * ML_V_DIM), lambda b, p: (b, 0, p)),
        scratch_shapes=[
            pltpu.VMEM((hp, ML_QK_DIM, ML_V_DIM + LANES), F32),
            pltpu.VMEM((hp, 1, LANES), F32),
        ],
        compiler_params=_cparams("parallel", "parallel"),
        name="mlstm_chunkwise",
    )(q, kt, v, g_rows, g_cols, out_gain)


def _ml_out_kernel(hh_ref, og_ref, w_ref, x_ref, g_ref,
                   gain2_ref, sc2_ref, sh2_ref, wr_ref, br_ref, o_ref, h_ref, r_ref, rt_ref, cnt_ref, run_ref):
    a = jax.nn.sigmoid(og_ref[0]) * hh_ref[0]
    x_new = x_ref[0] + g_ref[0] * _dot(a.astype(BF16), w_ref[...])
    o_ref[0] = x_new
    _route_tile(x_new, gain2_ref, sc2_ref, sh2_ref, wr_ref, br_ref, h_ref, r_ref, rt_ref, cnt_ref, run_ref)


def _ml_out(hh, og, w_out_bf, x, g1, route_args):
    B, S, D = x.shape
    tm = OUT_TILE
    row = lambda b, j: (b, j, 0)
    per_b = lambda b, j: (b, 0, 0)
    const2 = lambda b, j: (0, 0)
    r_in, r_shape, r_out, r_scratch = _route_specs(B, S, D, tm)
    return pl.pallas_call(
        _ml_out_kernel,
        out_shape=(jax.ShapeDtypeStruct((B, S, D), F32),) + r_shape,
        grid=(B, S // tm),
        in_specs=[
            pl.BlockSpec((1, tm, D), row),
            pl.BlockSpec((1, tm, D), row),
            pl.BlockSpec((D, D), const2),
            pl.BlockSpec((1, tm, D), row),
            pl.BlockSpec((1, 1, D), per_b),
        ] + r_in,
        out_specs=(pl.BlockSpec((1, tm, D), row),) + r_out,
        scratch_shapes=r_scratch,
        compiler_params=_cparams("arbitrary", "arbitrary"),
        name="mlstm_out_proj",
    )(hh, og, w_out_bf, x, g1, *route_args)


def _first_argmax(x, lane, width):
    mx = jnp.max(x, axis=-1, keepdims=True)
    idx = jnp.min(jnp.where(x == mx, lane, width), axis=-1, keepdims=True)
    return mx, idx


def _route_tile(x, gain_ref, sc_ref, sh_ref, w_ref, b_ref, h_ref, r_ref, rt_ref, cnt_ref, run_ref):
    tm = x.shape[0]

    @pl.when(jnp.logical_and(pl.program_id(0) == 0, pl.program_id(1) == 0))
    def _():
        run_ref[...] = jnp.zeros_like(run_ref)

    h = _modulated_norm(x, gain_ref[...], sc_ref[0], sh_ref[0])
    h_ref[0] = _pack_bf16_pairs(h)
    logits = _dot(h.astype(BF16), w_ref[...]) + b_ref[...]
    lane = lax.broadcasted_iota(jnp.int32, (1, LANES), 1)

    def pick(lgt):
        lg = jnp.where(lane < N_GROUPS, lgt, NEG_INF)
        g_max, g_sel = _first_argmax(lg, lane, LANES)
        pg = 1.0 / jnp.sum(jnp.exp(lg - g_max), axis=-1, keepdims=True)
        e_lane = lane - N_GROUPS
        in_grp = jnp.logical_and(e_lane >= g_sel * EXPERTS_PER_GROUP, e_lane < (g_sel + 1) * EXPERTS_PER_GROUP)
        le = jnp.where(in_grp, lgt, NEG_INF)
        v1, i1 = _first_argmax(le, lane, LANES)
        v2, i2 = _first_argmax(jnp.where(lane == i1, NEG_INF, le), lane, LANES)
        ratio = jnp.exp(v2 - v1)
        return i1 - N_GROUPS, i2 - N_GROUPS, pg / (1.0 + ratio), pg * ratio / (1.0 + ratio)

    e1, e2, w1, w2 = pick(logits)
    hot1 = lane == e1
    hot2 = lane == e2
    onehot = jnp.where(jnp.logical_or(hot1, hot2), 1.0, 0.0)
    r_i = lax.broadcasted_iota(jnp.int32, (tm, tm), 0)
    c_i = lax.broadcasted_iota(jnp.int32, (tm, tm), 1)
    before = jnp.where(c_i < r_i, 1.0, 0.0).astype(BF16)
    seen = _dot(before, onehot.astype(BF16)) + run_ref[...]
    rank1 = jnp.sum(jnp.where(hot1, seen, 0.0), axis=-1, keepdims=True)
    rank2 = jnp.sum(jnp.where(hot2, seen, 0.0), axis=-1, keepdims=True)
    run_ref[...] = run_ref[...] + jnp.sum(onehot, axis=0, keepdims=True)
    cnt_ref[...] = run_ref[...]

    out = jnp.where(lane == 0, e1.astype(F32), 0.0)
    out = jnp.where(lane == 1, e2.astype(F32), out)
    out = jnp.where(lane == 2, w1, out)
    out = jnp.where(lane == 3, w2, out)
    out = jnp.where(lane == 4, rank1, out)
    out = jnp.where(lane == 5, rank2, out)
    r_ref[0] = out
    rt_ref[0] = out.T[:8, :]


def _route_specs(B, S, D, tm):
    row = lambda b, j: (b, j, 0)
    per_b = lambda b, j: (b, 0, 0)
    const2 = lambda b, j: (0, 0)
    in_specs = [
        pl.BlockSpec((1, D), const2),
        pl.BlockSpec((1, 1, D), per_b),
        pl.BlockSpec((1, 1, D), per_b),
        pl.BlockSpec((D, LANES), const2),
        pl.BlockSpec((1, LANES), const2),
    ]
    out_shape = (
        jax.ShapeDtypeStruct((B, S, D // 2), jnp.int32),
        jax.ShapeDtypeStruct((B, S, LANES), F32),
        jax.ShapeDtypeStruct((B, 8, S), F32),
        jax.ShapeDtypeStruct((1, LANES), F32),
    )
    out_specs = (
        pl.BlockSpec((1, tm, D // 2), row),
        pl.BlockSpec((1, tm, LANES), row),
        pl.BlockSpec((1, 8, tm), lambda b, j: (b, 0, j)),
        pl.BlockSpec((1, LANES), const2),
    )
    return in_specs, out_shape, out_specs, [pltpu.VMEM((1, LANES), F32)]


def _experts_kernel(blk_e_ref, n_used_ref, next_e_ref, x_ref, wg_hbm, wu_hbm, wd_hbm, o_ref,
                    wg_f, wu_f, wd_f, wg_s, wu_s, wd_s, sem, *, layer):
    i = pl.program_id(0)
    used = i < n_used_ref[0]
    e = blk_e_ref[i]
    new_expert = jnp.logical_or(i == 0, e != blk_e_ref[jnp.maximum(i - 1, 0)])

    def fetch(expert):
        return (pltpu.make_async_copy(wg_hbm.at[layer, expert], wg_f, sem.at[0]),
                pltpu.make_async_copy(wu_hbm.at[layer, expert], wu_f, sem.at[1]),
                pltpu.make_async_copy(wd_hbm.at[layer, expert], wd_f, sem.at[2]))

    @pl.when(i == 0)
    def _():
        for cp in fetch(e):
            cp.start()

    @pl.when(jnp.logical_and(used, new_expert))
    def _():
        for cp in fetch(e):
            cp.wait()
        wg_s[...] = wg_f[...].astype(BF16)
        wu_s[...] = wu_f[...].astype(BF16)
        wd_s[...] = wd_f[...].astype(BF16)

        @pl.when(next_e_ref[i] >= 0)
        def _():
            for cp in fetch(next_e_ref[i]):
                cp.start()

    @pl.when(used)
    def _():
        x = _unpack_bf16_pairs(x_ref[...]).astype(BF16)
        a = _dot(x, wg_s[...])
        u = _dot(x, wu_s[...])
        act = a * jax.nn.sigmoid(a) * u
        o_ref[...] = _pack_bf16_pairs(_dot(act.astype(BF16), wd_s[...]))

    @pl.when(i >= n_used_ref[0])
    def _():
        o_ref[...] = jnp.zeros_like(o_ref)


def _experts(layer, blk_e, n_used, next_e, xs, w_gate, w_up, w_down):
    R = xs.shape[0]
    D = 2 * xs.shape[1]
    n_blk = R // MOE_BLOCK
    rows = lambda i, be, nu, ne: (i, 0)
    grid_spec = pltpu.PrefetchScalarGridSpec(
        num_scalar_prefetch=3,
        grid=(n_blk,),
        in_specs=[
            pl.BlockSpec((MOE_BLOCK, D // 2), rows),
            pl.BlockSpec(memory_space=pl.ANY),
            pl.BlockSpec(memory_space=pl.ANY),
            pl.BlockSpec(memory_space=pl.ANY),
        ],
        out_specs=pl.BlockSpec((MOE_BLOCK, D // 2), rows),
        scratch_shapes=[
            pltpu.VMEM((D, D_EXPERT), F32),
            pltpu.VMEM((D, D_EXPERT), F32),
            pltpu.VMEM((D_EXPERT, D), F32),
            pltpu.VMEM((D, D_EXPERT), BF16),
            pltpu.VMEM((D, D_EXPERT), BF16),
            pltpu.VMEM((D_EXPERT, D), BF16),
            pltpu.SemaphoreType.DMA((3,)),
        ],
    )
    return pl.pallas_call(
        functools.partial(_experts_kernel, layer=layer),
        out_shape=jax.ShapeDtypeStruct((R, D // 2), jnp.int32),
        grid_spec=grid_spec,
        compiler_params=_cparams("arbitrary"),
        name="moe_experts",
    )(blk_e, n_used, next_e, xs, w_gate, w_up, w_down)


def _residual_tile(x_ref, pending_refs):
    if not pending_refs:
        return x_ref[0]
    g_ref, y0_ref, y1_ref, r_ref = pending_refs
    r = r_ref[0]
    y = _unpack_bf16_pairs(y0_ref[0, 0]) * r[:, 2:3] + _unpack_bf16_pairs(y1_ref[0, 0]) * r[:, 3:4]
    return x_ref[0] + g_ref[0] * y


def _pending_specs(D, tm):
    return [
        pl.BlockSpec((1, 1, D), lambda b, j: (b, 0, 0)),
        pl.BlockSpec((1, 1, tm, D // 2), lambda b, j: (0, b, j, 0)),
        pl.BlockSpec((1, 1, tm, D // 2), lambda b, j: (1, b, j, 0)),
        pl.BlockSpec((1, tm, LANES), lambda b, j: (b, j, 0)),
    ]


def _combine_kernel(x_ref, g_ref, y0_ref, y1_ref, r_ref, o_ref):
    o_ref[0] = _residual_tile(x_ref, (g_ref, y0_ref, y1_ref, r_ref))


def _combine(x, pending):
    B, S, D = x.shape
    tm = TOKEN_TILE
    row = lambda b, j: (b, j, 0)
    return pl.pallas_call(
        _combine_kernel,
        out_shape=jax.ShapeDtypeStruct((B, S, D), F32),
        grid=(B, S // tm),
        in_specs=[pl.BlockSpec((1, tm, D), row)] + _pending_specs(D, tm),
        out_specs=pl.BlockSpec((1, tm, D), row),
        compiler_params=_cparams("parallel", "parallel"),
        name="moe_combine",
    )(x, *pending)


SC_CORES = 2
SC_SUBCORES = 16
SC_WORKERS = SC_CORES * SC_SUBCORES
SC_CHUNK = 64


def _sc_mesh():
    return plsc.VectorSubcoreMesh(core_axis_name="c", subcore_axis_name="s",
                                  num_cores=SC_CORES, num_subcores=SC_SUBCORES)


def _sc_scatter_rows(src, idx, n_out):
    T, W = src.shape
    per_w = T // SC_WORKERS
    nch = per_w // SC_CHUNK
    idx4 = idx.reshape(TOP_K, SC_WORKERS, nch, SC_CHUNK)

    @functools.partial(
        pl.kernel, mesh=_sc_mesh(),
        out_type=jax.ShapeDtypeStruct((n_out, W), src.dtype),
        scratch_types=[pltpu.VMEM((TOP_K, nch, SC_CHUNK), jnp.int32), pltpu.VMEM((SC_CHUNK, W), src.dtype)],
        name="sc_scatter_rows",
    )
    def body(src_hbm, idx_hbm, out_hbm, idx_v, rows_v):
        wid = lax.axis_index("s") * SC_CORES + lax.axis_index("c")
        for s in range(TOP_K):
            pltpu.sync_copy(idx_hbm.at[s, wid], idx_v.at[s])

        @pl.loop(0, nch)
        def _(i):
            pltpu.sync_copy(src_hbm.at[pl.ds(wid * per_w + i * SC_CHUNK, SC_CHUNK)], rows_v)
            for s in range(TOP_K):
                pltpu.sync_copy(rows_v, out_hbm.at[idx_v.at[s, i]])

    return body(src, idx4)


def _sc_gather_rows(table, idx):
    N = idx.shape[0]
    W = table.shape[1]
    per_w = N // SC_WORKERS
    nch = per_w // SC_CHUNK
    idx3 = idx.reshape(SC_WORKERS, nch, SC_CHUNK)

    @functools.partial(
        pl.kernel, mesh=_sc_mesh(),
        out_type=jax.ShapeDtypeStruct((N, W), table.dtype),
        scratch_types=[
            pltpu.VMEM((nch, SC_CHUNK), jnp.int32),
            pltpu.VMEM((2, SC_CHUNK, W), table.dtype),
            pltpu.SemaphoreType.DMA((2,)),
            pltpu.SemaphoreType.DMA((2,)),
        ],
        name="sc_gather_rows",
    )
    def body(table_hbm, idx_hbm, out_hbm, idx_v, rows_v, gather_sem, write_sem):
        wid = lax.axis_index("s") * SC_CORES + lax.axis_index("c")
        pltpu.sync_copy(idx_hbm.at[wid], idx_v)

        def gather(j, b):
            return pltpu.make_async_copy(table_hbm.at[idx_v.at[j]], rows_v.at[b], gather_sem.at[b])

        def write(j, b):
            return pltpu.make_async_copy(
                rows_v.at[b], out_hbm.at[pl.ds(wid * per_w + j * SC_CHUNK, SC_CHUNK)], write_sem.at[b])

        gather(0, 0).start()

        @pl.loop(0, nch, step=2)
        def _(i):
            for b in range(2):
                j = i + b

                @pl.when(j >= 1)
                def _():
                    write(j - 1, 1 - b).wait()

                @pl.when(j + 1 < nch)
                def _():
                    gather(j + 1, 1 - b).start()

                gather(j, b).wait()
                write(j, b).start()

        write(nch - 1, (nch - 1) % 2).wait()

    assert nch % 2 == 0
    return body(table, idx3)


def _moe_dispatch(route_t, counts, T):
    A = T * TOP_K
    counts = counts[0, :N_EXPERTS].astype(jnp.int32)
    blocks_per = (counts + MOE_BLOCK - 1) // MOE_BLOCK
    block_end = jnp.cumsum(blocks_per)
    block_start = block_end - blocks_per
    expert = jnp.swapaxes(route_t[:, :TOP_K, :], 0, 1).reshape(TOP_K, T).astype(jnp.int32)
    rank = jnp.swapaxes(route_t[:, 4:4 + TOP_K, :], 0, 1).reshape(TOP_K, T).astype(jnp.int32)
    onehot = expert[None] == jnp.arange(N_EXPERTS, dtype=jnp.int32)[:, None, None]
    start = jnp.sum(jnp.where(onehot, block_start[:, None, None], 0), axis=0)
    dest = start * MOE_BLOCK + rank
    n_blk = -(-A // MOE_BLOCK) + N_EXPERTS
    blk = jnp.arange(n_blk, dtype=jnp.int32)
    blk_e = jnp.minimum(jnp.sum(blk[:, None] >= block_end[None, :], axis=-1), N_EXPERTS - 1).astype(jnp.int32)
    n_used = block_end[-1]
    first = jnp.logical_and(blk < n_used, jnp.logical_or(blk == 0, blk_e != jnp.roll(blk_e, 1)))
    first_pos = jnp.where(first, blk, n_blk)
    next_pos = jnp.concatenate([lax.cummin(first_pos, axis=0, reverse=True)[1:], jnp.full((1,), n_blk, jnp.int32)])
    next_e = jnp.where(next_pos < n_blk, blk_e[jnp.minimum(next_pos, n_blk - 1)], -1).astype(jnp.int32)
    return dest, n_blk * MOE_BLOCK, blk_e, n_used.reshape(1).astype(jnp.int32), next_e


def _rope_tables(S):
    inv = 1.0 / (ROPE_THETA ** (jnp.arange(0, HEAD_DIM, 2, dtype=F32) / HEAD_DIM))
    ang = jnp.arange(S, dtype=F32)[:, None] * inv[None, :]
    cos, sin = jnp.cos(ang), jnp.sin(ang)
    cos_h = jnp.concatenate([cos, cos], axis=-1)
    sin_h = jnp.concatenate([-sin, sin], axis=-1)
    return jnp.tile(cos_h, (1, ATTN_HEADS)), jnp.tile(sin_h, (1, ATTN_HEADS))


def _pad_cols(w, width):
    return jnp.pad(w, ((0, 0), (0, width - w.shape[1])))


def kernel(x, c, ada_w, ada_b, norm_mix, norm_ffn, hy_w_in, hy_q_norm, hy_k_norm, hy_conv_w, hy_w_out, ml_w_in, ml_b_gates, ml_out_norm, ml_w_out, moe_w_group, moe_b_group, moe_w_expert, moe_b_expert, moe_w_gate, moe_w_up, moe_w_down):
    B, S, D = x.shape
    T = B * S
    cos_t, sin_t = _rope_tables(S)
    mod = _ada_modulation(c, ada_w, ada_b).reshape(DEPTH, B, 6, 1, D)
    r_i = np.arange(ATTN_WIDTH)
    grp = jnp.asarray((r_i[:, None] // HEAD_DIM) == (r_i[None, :] // HEAD_DIM), dtype=BF16)

    pending = ()
    for l in range(DEPTH):
        sh1, sc1, g1, sh2, sc2, g2 = [mod[l, :, i] for i in range(6)]
        gain1 = norm_mix[l].reshape(1, D)
        w_r = _pad_cols(jnp.concatenate([moe_w_group[l], moe_w_expert[l]], axis=1), LANES).astype(BF16)
        b_r = jnp.pad(jnp.concatenate([moe_b_group[l], moe_b_expert[l]]), (0, LANES - N_GROUPS - N_EXPERTS))
        route_args = (norm_ffn[l].reshape(1, D), sc2, sh2, w_r, b_r.reshape(1, LANES))
        j = l // 2
        if l % 2 == 0:
            w = hy_w_in[j]
            o = np.cumsum((0,) + (ATTN_WIDTH, HEAD_DIM, HEAD_DIM, IDX_HEADS * IDX_DIM, IDX_DIM, IDX_HEADS,
                                  CONV_WIDTH, CONV_WIDTH, CONV_WIDTH))
            wq, wk, wv, wiq, wik, wiw, wbg, wcg, wu = [w[:, o[i]:o[i + 1]] for i in range(9)]
            w_pad = jnp.concatenate(
                [wk, wv, _pad_cols(jnp.concatenate([wik, wiw], axis=1), LANES), wq, wiq, wbg, wcg, wu],
                axis=1).astype(BF16)
            qn_t = jnp.tile(hy_q_norm[j], ATTN_HEADS).reshape(1, ATTN_WIDTH)
            kn_t = jnp.tile(hy_k_norm[j], LANES // HEAD_DIM).reshape(1, LANES)
            outs = _hyb_in(x, pending, gain1, sc1, sh1, w_pad, cos_t, sin_t, qn_t, kn_t, grp)
            qt, iqt, bcu, kv, kvt, ik, iwt = outs[:7]
            if pending:
                x = outs[7]
            y_attn = _dsa_attention_t(qt, iqt, iwt, kv, kvt, ik)
            x, h2, route, route_t, counts =_hyb_out(y_attn, bcu, hy_conv_w[j], hy_w_out[j].astype(BF16), x, g1, route_args)
        else:
            w = ml_w_in[j]
            hq = ML_HEADS * ML_QK_DIM
            hv = ML_HEADS * ML_V_DIM
            wq, wk, wv = w[:, :hq], w[:, hq:2 * hq], w[:, 2 * hq:2 * hq + hv]
            wg = w[:, 2 * hq + hv:2 * hq + hv + 2 * ML_HEADS]
            wo = w[:, 2 * hq + hv + 2 * ML_HEADS:]
            w_pad = jnp.concatenate([_pad_cols(wg, LANES), wk, wq, wv, wo], axis=1).astype(BF16)
            gate_bias = jnp.pad(ml_b_gates[j], (0, LANES - 2 * ML_HEADS)).reshape(1, LANES)
            q, k, v, og, g_cols, g_rows, x = _ml_in(x, pending, gain1, sc1, sh1, w_pad, gate_bias)
            assert ML_STEP_HEADS == ML_HEADS
            hh = _mlstm(q, k, v, g_rows[:, None], g_cols[:, None], ml_out_norm[j].reshape(1, hv))
            x, h2, route, route_t, counts =_ml_out(hh, og, ml_w_out[j].astype(BF16), x, g1, route_args)

        dest, n_rows, blk_e, n_used, next_e = _moe_dispatch(route_t, counts, T)
        xs = _sc_scatter_rows(h2.reshape(T, D // 2), dest, n_rows)
        ys = _experts(l, blk_e, n_used, next_e, xs, moe_w_gate, moe_w_up, moe_w_down)
        y01 = _sc_gather_rows(ys, dest.reshape(TOP_K * T)).reshape(TOP_K, B, S, D // 2)
        pending = (g2, y01, y01, route)
    return _combine(x, pending)
```

```python
import functools

import numpy as np
import jax
import jax.numpy as jnp
from jax import lax
from jax.experimental import pallas as pl
from jax.experimental.pallas import tpu as pltpu
from jax.experimental.pallas import tpu_sc as plsc

F32 = jnp.float32
BF16 = jnp.bfloat16
HIGHEST = lax.Precision.HIGHEST

D_MODEL = 1024
DEPTH = 4
ATTN_HEADS = 8
HEAD_DIM = 64
ATTN_WIDTH = ATTN_HEADS * HEAD_DIM
IDX_HEADS = 8
IDX_DIM = 64
INDEX_TOPK = 256
Q_BLOCK = 256
ROPE_THETA = 10000.0
CONV_WIDTH = D_MODEL - ATTN_WIDTH
CONV_K = 3
ML_HEADS = 8
ML_QK_DIM = 64
ML_V_DIM = 128
N_GROUPS = 4
EXPERTS_PER_GROUP = 8
N_EXPERTS = N_GROUPS * EXPERTS_PER_GROUP
TOP_K = 2
D_EXPERT = 512
MOE_BLOCK = 512
NORM_EPS = 1e-6

LANES = 128
VMEM_LIMIT = 56 * 1024 * 1024
TOKEN_TILE = 256
OUT_TILE = 512
ML_CHUNK = 256
ML_STEP_HEADS = 8
NEG_INF = float("-inf")


def _cparams(*sem):
    return pltpu.CompilerParams(dimension_semantics=sem, vmem_limit_bytes=VMEM_LIMIT)


def _dot(a, b):
    return jnp.dot(a, b, preferred_element_type=F32)


def _pack_bf16_pairs(x):
    bits = lax.bitcast_convert_type(x.astype(BF16).astype(F32), jnp.uint32)
    half = bits.shape[1] // 2
    packed = (bits[:, :half] >> 16) | (bits[:, half:] & jnp.uint32(0xFFFF0000))
    return lax.bitcast_convert_type(packed, jnp.int32)


def _unpack_bf16_pairs(words):
    words = lax.bitcast_convert_type(words, jnp.uint32)
    return jnp.concatenate(
        [lax.bitcast_convert_type(words << 16, F32),
         lax.bitcast_convert_type(words & jnp.uint32(0xFFFF0000), F32)], axis=1)


def _split_dot(a_f32, b_bf16):
    hi = a_f32.astype(BF16)
    lo = (a_f32 - hi.astype(F32)).astype(BF16)
    return _dot(hi, b_bf16) + _dot(lo, b_bf16)


def _ada_kernel(c_ref, w_ref, b_ref, o_ref):
    c = c_ref[...]
    ca = c * jax.nn.sigmoid(c)
    o_ref[0] = jnp.dot(ca, w_ref[0], precision=HIGHEST, preferred_element_type=F32) + b_ref[0]


def _ada_modulation(c, ada_w, ada_b):
    B, D = c.shape
    n_col = ada_w.shape[-1] // D
    return pl.pallas_call(
        _ada_kernel,
        out_shape=jax.ShapeDtypeStruct((DEPTH, B, n_col * D), F32),
        grid=(DEPTH, n_col),
        in_specs=[
            pl.BlockSpec((B, D), lambda l, j: (0, 0)),
            pl.BlockSpec((1, D, D), lambda l, j: (l, 0, j)),
            pl.BlockSpec((1, 1, D), lambda l, j: (l, 0, j)),
        ],
        out_specs=pl.BlockSpec((1, B, D), lambda l, j: (l, 0, j)),
        compiler_params=_cparams("parallel", "parallel"),
        name="ada_modulation",
    )(c, ada_w, ada_b.reshape(DEPTH, 1, n_col * D))


def _modulated_norm(x, gain, scale, shift):
    y = x * lax.rsqrt(jnp.mean(x * x, axis=-1, keepdims=True) + NORM_EPS)
    return y * gain * (1.0 + scale) + shift


def _rope(x, cos, sin_signed, first_half):
    w = x.shape[-1]
    partner = jnp.where(first_half, pltpu.roll(x, w - HEAD_DIM // 2, 1), pltpu.roll(x, HEAD_DIM // 2, 1))
    return x * cos + partner * sin_signed


HYB_COLS = 5 * 512 + 2 * LANES


def _hyb_in_kernel(*refs, n_pending):
    x_ref, pending_refs, refs = refs[0], refs[1:1 + n_pending], refs[1 + n_pending:]
    (gain_ref, sc_ref, sh_ref, w_ref, cos_ref, sin_ref, qn_ref, kn_ref, grp_ref,
     qt_ref, iqt_ref, bcu_ref, kv_ref, kvt_ref, ik_ref, iwt_ref) = refs[:16]
    x = _residual_tile(x_ref, pending_refs)
    if n_pending:
        refs[16][0] = x
    h = _modulated_norm(x, gain_ref[...], sc_ref[0], sh_ref[0])
    hb = h.astype(BF16)
    p_small = _dot(hb, w_ref[:, 0:2 * LANES])
    p_q = _dot(hb, w_ref[:, 2 * LANES:2 * LANES + 512])
    p_iq = _dot(hb, w_ref[:, 2 * LANES + 512:2 * LANES + 1024])
    cos = cos_ref[...]
    sin = sin_ref[...]
    lane = lax.broadcasted_iota(jnp.int32, (1, ATTN_WIDTH), 1)
    first_half = (lane % HEAD_DIM) < (HEAD_DIM // 2)
    fh128 = first_half[:, :LANES]
    lane128 = lane[:, :LANES]

    kv = p_small[:, :LANES]
    is_k = lane128 < HEAD_DIM
    kk = jnp.where(is_k, kv, 0.0)
    ms_k = jnp.sum(kk * kk, axis=-1, keepdims=True) * (1.0 / HEAD_DIM)
    kn = kv * lax.rsqrt(ms_k + NORM_EPS) * kn_ref[...]
    kr = _rope(kn, cos[:, :LANES], sin[:, :LANES], fh128)
    kv = jnp.where(is_k, kr, kv)
    kv_ref[0] = kv.astype(BF16)
    feat = lax.broadcasted_iota(jnp.int32, (LANES, 1), 0)
    kvt_ref[0, 0] = jnp.where(feat < HEAD_DIM, 1.0, kv.T).astype(BF16)

    sm = p_small[:, LANES:]
    ikr = _rope(sm, cos[:, :LANES], sin[:, :LANES], fh128)
    ik_ref[0] = jnp.where(is_k, ikr, 0.0).astype(BF16)
    iwt_ref[0] = sm.T

    ms = _split_dot(p_q * p_q, grp_ref[...]) * (1.0 / HEAD_DIM)
    q = p_q * lax.rsqrt(ms + NORM_EPS) * qn_ref[...]
    qt_ref[0] = (_rope(q, cos, sin, first_half) * (HEAD_DIM ** -0.5)).T.astype(BF16)
    iqt_ref[0] = (_rope(p_iq, cos, sin, first_half) * (IDX_DIM ** -0.5)).T.astype(BF16)

    bcu_ref[0] = _dot(hb, w_ref[:, 2 * LANES + 1024:])


def _hyb_in(x, pending, gain, sc, sh, w_pad, cos_t, sin_t, qn_t, kn_t, grp):
    B, S, D = x.shape
    tm = TOKEN_TILE
    row = lambda b, j: (b, j, 0)
    per_b = lambda b, j: (b, 0, 0)
    const2 = lambda b, j: (0, 0)
    tab = lambda b, j: (j, 0)
    col = lambda b, j: (b, 0, j)
    assert tm == DSA_KEY_CHUNK
    x_out_shape = (jax.ShapeDtypeStruct((B, S, D), F32),) if pending else ()
    x_out_spec = (pl.BlockSpec((1, tm, D), row),) if pending else ()
    return pl.pallas_call(
        functools.partial(_hyb_in_kernel, n_pending=len(pending)),
        out_shape=(
            jax.ShapeDtypeStruct((B, 512, S), BF16),
            jax.ShapeDtypeStruct((B, 512, S), BF16),
            jax.ShapeDtypeStruct((B, S, 1536), F32),
            jax.ShapeDtypeStruct((B, S, LANES), BF16),
            jax.ShapeDtypeStruct((B, S // tm, LANES, tm), BF16),
            jax.ShapeDtypeStruct((B, S, LANES), BF16),
            jax.ShapeDtypeStruct((B, LANES, S), F32),
        ) + x_out_shape,
        grid=(B, S // tm),
        in_specs=[pl.BlockSpec((1, tm, D), row)] + (_pending_specs(D, tm) if pending else []) + [
            pl.BlockSpec((1, D), const2),
            pl.BlockSpec((1, 1, D), per_b),
            pl.BlockSpec((1, 1, D), per_b),
            pl.BlockSpec((D, HYB_COLS), const2),
            pl.BlockSpec((tm, 512), tab),
            pl.BlockSpec((tm, 512), tab),
            pl.BlockSpec((1, 512), const2),
            pl.BlockSpec((1, LANES), const2),
            pl.BlockSpec((512, 512), const2),
        ],
        out_specs=(
            pl.BlockSpec((1, 512, tm), col),
            pl.BlockSpec((1, 512, tm), col),
            pl.BlockSpec((1, tm, 1536), row),
            pl.BlockSpec((1, tm, LANES), row),
            pl.BlockSpec((1, 1, LANES, tm), lambda b, j: (b, j, 0, 0)),
            pl.BlockSpec((1, tm, LANES), row),
            pl.BlockSpec((1, LANES, tm), col),
        ) + x_out_spec,
        compiler_params=_cparams("parallel", "parallel"),
        name="hybrid_in_proj",
    )(x, *pending, gain, sc, sh, w_pad, cos_t, sin_t, qn_t, kn_t, grp)


DSA_KEY_CHUNK = 256
DSA_SUM_ROWS = 16


def _fold8(x, op):
    parts = x.reshape(x.shape[0] // 8, 8, x.shape[1])
    while parts.shape[0] > 1:
        half = parts.shape[0] // 2
        assert parts.shape[0] == 2 * half
        parts = op(parts[:half], parts[half:])
    return parts[0]


def _col_reduce(x, op):
    t = _fold8(x, op)
    for shift in (4, 2, 1):
        t = op(t, pltpu.roll(t, shift, 0))
    return t[0:1, :]


def _dsa_t_kernel(qt_ref, iqt_ref, iwt_ref, kv_ref, kvt_ref, ik_ref, o_ref, sc_ref, bias_ref, acc_ref):
    CK = DSA_KEY_CHUNK
    QB = Q_BLOCK
    qb = pl.program_id(1)
    nk = (qb * QB + QB + CK - 1) // CK
    kf = float(INDEX_TOPK)
    qpos = qb * QB + lax.broadcasted_iota(jnp.int32, (1, QB), 1)
    krow = lax.broadcasted_iota(jnp.int32, (CK, 1), 0)
    w_idx = iwt_ref[0, IDX_DIM:IDX_DIM + IDX_HEADS, :] * (IDX_HEADS ** -0.5)

    def rows(c):
        return pl.ds(pl.multiple_of(c * CK, CK), CK)

    def heads_on_lanes(ref, width):
        return jnp.concatenate([ref[0, hd * width:(hd + 1) * width, :] for hd in range(ref.shape[1] // width)], axis=1)

    def head_lanes(hd):
        return slice(hd * QB, (hd + 1) * QB)

    iq_wide = heads_on_lanes(iqt_ref, IDX_DIM)
    w_wide = jnp.concatenate([w_idx[hd:hd + 1, :] for hd in range(IDX_HEADS)], axis=1)

    def score_chunk(c, carry):
        mx, mn = carry
        ikc = ik_ref[0, rows(c), :][:, :IDX_DIM]
        s_all = jnp.maximum(_dot(ikc, iq_wide), 0.0) * w_wide
        acc = s_all[:, head_lanes(0)]
        for hd in range(1, IDX_HEADS):
            acc = acc + s_all[:, head_lanes(hd)]
        causal = (c * CK + krow) <= qpos
        sc_ref[rows(c), :] = jnp.where(causal, acc, NEG_INF)
        mx = jnp.maximum(mx, _fold8(jnp.where(causal, acc, NEG_INF), jnp.maximum))
        mn = jnp.minimum(mn, _fold8(jnp.where(causal, acc, jnp.inf), jnp.minimum))
        return mx, mn

    mx8, mn8 = lax.fori_loop(0, nk, score_chunk,
                             (jnp.full((8, QB), NEG_INF, F32), jnp.full((8, QB), jnp.inf, F32)))
    row_max = jnp.max(mx8, axis=0, keepdims=True)
    row_min = jnp.min(mn8, axis=0, keepdims=True)

    @pl.when(nk % 2 == 1)
    def _():
        sc_ref[rows(nk), :] = jnp.full((CK, QB), NEG_INF, F32)

    n_pairs = (nk + 1) // 2

    def pair_rows(c):
        return pl.ds(pl.multiple_of(c * (2 * CK), 2 * CK), 2 * CK)

    def count(pred):
        def body(c, part):
            return part + _fold8(jnp.where(pred(sc_ref[pair_rows(c), :]), 1.0, 0.0), jnp.add)
        part = lax.fori_loop(0, n_pairs, body, jnp.zeros((8, QB), F32))
        return jnp.sum(part, axis=0, keepdims=True)

    @pl.when(qb * QB + QB <= INDEX_TOPK)
    def _():
        def body(c, carry):
            bias_ref[rows(c), :] = jnp.where(sc_ref[rows(c), :] > NEG_INF, 0.0, NEG_INF)
            return carry
        lax.fori_loop(0, nk, body, 0)

    @pl.when(qb * QB + QB > INDEX_TOPK)
    def _():
        def bisect(_, carry):
            lo, hi, c_lo, c_hi = carry
            mid = 0.5 * lo + 0.5 * jnp.minimum(hi, row_max)
            cnt = count(lambda x: x >= mid)
            ge = cnt >= kf
            return (jnp.where(ge, mid, lo), jnp.where(ge, hi, mid),
                    jnp.where(ge, cnt, c_lo), jnp.where(ge, c_hi, cnt))

        n_adm = (qpos + 1).astype(F32)
        lo, hi, c_lo, c_hi = lax.fori_loop(
            0, 18, bisect, (row_min, jnp.full((1, QB), jnp.inf, F32), n_adm, jnp.zeros((1, QB), F32)))

        def refine_cond(carry):
            it, _, _, _, done = carry
            return jnp.logical_and(it < nk * CK, jnp.min(done) < 0.5)

        def refine(carry):
            it, hi, c_hi, thr, done = carry

            def edges(c, part):
                up, dn = part
                x = sc_ref[pair_rows(c), :]
                return (jnp.maximum(up, _fold8(jnp.where(x < hi, x, NEG_INF), jnp.maximum)),
                        jnp.minimum(dn, _fold8(jnp.where(x >= lo, x, jnp.inf), jnp.minimum)))

            up8, dn8 = lax.fori_loop(0, n_pairs, edges,
                                     (jnp.full((8, QB), NEG_INF, F32), jnp.full((8, QB), jnp.inf, F32)))
            m_up = jnp.max(up8, axis=0, keepdims=True)
            m_dn = jnp.min(dn8, axis=0, keepdims=True)
            from_hi = c_hi == kf - 1.0
            from_lo = c_lo == kf
            thr = jnp.where(done > 0.5, thr, jnp.where(from_hi, m_up, m_dn))
            done = jnp.where(jnp.logical_or(from_hi, from_lo), 1.0, done)

            def step_down(args):
                hi, c_hi, thr, done = args
                open_ = done < 0.5
                cnt = count(lambda x: x >= m_up)
                hit = jnp.logical_and(open_, cnt >= kf)
                moved = jnp.logical_and(open_, cnt < kf)
                return (jnp.where(moved, m_up, hi), jnp.where(moved, cnt, c_hi), jnp.where(hit, m_up, thr),
                        jnp.where(hit, 1.0, done))

            hi, c_hi, thr, done = lax.cond(jnp.min(done) > 0.5, lambda args: args, step_down, (hi, c_hi, thr, done))
            return it + 1, hi, c_hi, thr, done

        _, _, _, thr, _ = lax.while_loop(
            refine_cond, refine, (jnp.int32(0), hi, c_hi, row_max, jnp.zeros((1, QB), F32)))

        def body(c, sel):
            keep = sc_ref[rows(c), :] >= thr
            bias_ref[rows(c), :] = jnp.where(keep, 0.0, NEG_INF)
            return sel + _fold8(jnp.where(keep, 1.0, 0.0), jnp.add)

        n_sel = jnp.sum(lax.fori_loop(0, nk, body, jnp.zeros((8, QB), F32)), axis=0, keepdims=True)

        @pl.when(jnp.max(n_sel) > kf + 0.5)
        def _():
            need = kf - count(lambda x: x > thr)
            r_i = lax.broadcasted_iota(jnp.int32, (CK, CK), 0)
            c_i = lax.broadcasted_iota(jnp.int32, (CK, CK), 1)
            lower = jnp.where(c_i <= r_i, 1.0, 0.0).astype(BF16)

            def body(c, seen):
                x = sc_ref[rows(c), :]
                eq = x == thr
                eq_f = jnp.where(eq, 1.0, 0.0)
                rank = _dot(lower, eq_f.astype(BF16)) + seen
                keep = jnp.logical_or(x > thr, jnp.logical_and(eq, rank <= need))
                bias_ref[rows(c), :] = jnp.where(keep, 0.0, NEG_INF)
                return seen + jnp.sum(eq_f, axis=0, keepdims=True)

            lax.fori_loop(0, nk, body, jnp.zeros((1, QB), F32))

    acc_ref[...] = jnp.zeros_like(acc_ref)
    q_wide = heads_on_lanes(qt_ref, HEAD_DIM)

    def attend_chunk(c, m_old):
        kc = kv_ref[0, rows(c), :][:, :HEAD_DIM]
        bias = bias_ref[rows(c), :]
        logits = _dot(kc, q_wide) + jnp.concatenate([bias] * ATTN_HEADS, axis=1)
        m_new = jnp.maximum(m_old, _col_reduce(logits, jnp.maximum))
        alpha = jnp.exp(m_old - m_new)
        p = jnp.exp(logits - m_new)
        acc_ref[...] = alpha * acc_ref[...] + _dot(kvt_ref[0, c, HEAD_DIM - DSA_SUM_ROWS:, :], p.astype(BF16))
        return m_new

    lax.fori_loop(0, nk, attend_chunk, jnp.full((1, ATTN_HEADS * QB), -1e30, F32))
    out_t = acc_ref[DSA_SUM_ROWS:, :] / acc_ref[0:1, :]
    o_ref[0] = jnp.concatenate([out_t[:, head_lanes(hd)] for hd in range(ATTN_HEADS)], axis=0).T


def _dsa_attention_t(qt, iqt, iwt, kv, kvt, ik):
    B, _, S = qt.shape
    col = lambda b, j: (b, 0, j)
    per_b = lambda b, j: (b, 0, 0)
    return pl.pallas_call(
        _dsa_t_kernel,
        out_shape=jax.ShapeDtypeStruct((B, S, ATTN_WIDTH), F32),
        grid=(B, S // Q_BLOCK),
        in_specs=[
            pl.BlockSpec((1, ATTN_WIDTH, Q_BLOCK), col),
            pl.BlockSpec((1, IDX_HEADS * IDX_DIM, Q_BLOCK), col),
            pl.BlockSpec((1, LANES, Q_BLOCK), col),
            pl.BlockSpec((1, S, LANES), per_b),
            pl.BlockSpec((1, S // DSA_KEY_CHUNK, LANES, DSA_KEY_CHUNK), lambda b, j: (b, 0, 0, 0)),
            pl.BlockSpec((1, S, LANES), per_b),
        ],
        out_specs=pl.BlockSpec((1, Q_BLOCK, ATTN_WIDTH), lambda b, j: (b, j, 0)),
        scratch_shapes=[
            pltpu.VMEM((S, Q_BLOCK), F32),
            pltpu.VMEM((S, Q_BLOCK), F32),
            pltpu.VMEM((DSA_SUM_ROWS + HEAD_DIM, ATTN_HEADS * Q_BLOCK), F32),
        ],
        compiler_params=_cparams("parallel", "parallel"),
        name="dsa_attention",
    )(qt, iqt, iwt, kv, kvt, ik)


def _hyb_out_kernel(ya_ref, bcu_ref, halo_ref, cw_ref, w_ref, x_ref, g_ref,
                    gain2_ref, sc2_ref, sh2_ref, wr_ref, br_ref, o_ref, h_ref, r_ref, rt_ref, cnt_ref, run_ref):
    j = pl.program_id(1)
    tm = ya_ref.shape[1]
    bcu = bcu_ref[0]
    bg = bcu[:, 0:512]
    z = bcu[:, 512:1024] * bcu[:, 1024:1536]
    halo = halo_ref[0]
    zh = halo[:, 512:1024] * halo[:, 1024:1536]
    zh = jnp.where(j > 0, zh, 0.0)
    row = lax.broadcasted_iota(jnp.int32, (tm, 1), 0)
    z1 = jnp.where(row >= 1, pltpu.roll(z, 1, 0), zh[7:8, :])
    z2 = jnp.where(row >= 2, pltpu.roll(z, 2, 0), jnp.where(row == 1, zh[7:8, :], zh[6:7, :]))
    cw = cw_ref[...]
    y_conv = bg * (z2 * cw[0:1, :] + z1 * cw[1:2, :] + z * cw[2:3, :])
    y = _dot(ya_ref[0].astype(BF16), w_ref[0:512, :]) + _dot(y_conv.astype(BF16), w_ref[512:1024, :])
    x_new = x_ref[0] + g_ref[0] * y
    o_ref[0] = x_new
    _route_tile(x_new, gain2_ref, sc2_ref, sh2_ref, wr_ref, br_ref, h_ref, r_ref, rt_ref, cnt_ref, run_ref)


def _hyb_out(y_attn, bcu, conv_w, w_out_bf, x, g1, route_args):
    B, S, D = x.shape
    tm = OUT_TILE
    row = lambda b, j: (b, j, 0)
    per_b = lambda b, j: (b, 0, 0)
    const2 = lambda b, j: (0, 0)
    halo = lambda b, j: (b, jnp.maximum(j * (tm // 8) - 1, 0), 0)
    r_in, r_shape, r_out, r_scratch = _route_specs(B, S, D, tm)
    return pl.pallas_call(
        _hyb_out_kernel,
        out_shape=(jax.ShapeDtypeStruct((B, S, D), F32),) + r_shape,
        grid=(B, S // tm),
        in_specs=[
            pl.BlockSpec((1, tm, 512), row),
            pl.BlockSpec((1, tm, 1536), row),
            pl.BlockSpec((1, 8, 1536), halo),
            pl.BlockSpec((CONV_K, CONV_WIDTH), const2),
            pl.BlockSpec((D, D), const2),
            pl.BlockSpec((1, tm, D), row),
            pl.BlockSpec((1, 1, D), per_b),
        ] + r_in,
        out_specs=(pl.BlockSpec((1, tm, D), row),) + r_out,
        scratch_shapes=r_scratch,
        compiler_params=_cparams("arbitrary", "arbitrary"),
        name="hybrid_out_proj",
    )(y_attn, bcu, bcu, conv_w, w_out_bf, x, g1, *route_args)


ML_COLS = 512 + 512 + 1024 + 1024 + LANES


def _ml_in_kernel(x_ref, g2_ref, y0_ref, y1_ref, r_ref, gain_ref, sc_ref, sh_ref, w_ref, bias_ref,
                  q_ref, k_ref, v_ref, og_ref, gt_ref, gtt_ref, xo_ref):
    x = _residual_tile(x_ref, (g2_ref, y0_ref, y1_ref, r_ref))
    xo_ref[0] = x
    h = _modulated_norm(x, gain_ref[...], sc_ref[0], sh_ref[0])
    hb = h.astype(BF16)
    gates = _dot(hb, w_ref[:, 0:LANES]) + bias_ref[...]
    gt_ref[0, 0] = gates
    gtt_ref[0, 0] = gates.T[:2 * ML_HEADS, :]
    k_ref[0, 0] = _dot(hb, w_ref[:, LANES:LANES + 512]).T.astype(BF16)
    q_ref[0] = (_dot(hb, w_ref[:, LANES + 512:LANES + 1024]) * (ML_QK_DIM ** -0.5)).astype(BF16)
    v_ref[0] = _dot(hb, w_ref[:, LANES + 1024:LANES + 2048]).astype(BF16)
    og_ref[0] = _dot(hb, w_ref[:, LANES + 2048:])


def _ml_in(x, pending, gain, sc, sh, w_pad, gate_bias):
    B, S, D = x.shape
    tm = TOKEN_TILE
    assert tm == ML_CHUNK
    row = lambda b, j: (b, j, 0)
    per_b = lambda b, j: (b, 0, 0)
    const2 = lambda b, j: (0, 0)
    return pl.pallas_call(
        _ml_in_kernel,
        out_shape=(
            jax.ShapeDtypeStruct((B, S, 512), BF16),
            jax.ShapeDtypeStruct((B, S // tm, 512, tm), BF16),
            jax.ShapeDtypeStruct((B, S, 1024), BF16),
            jax.ShapeDtypeStruct((B, S, 1024), F32),
            jax.ShapeDtypeStruct((B, S // tm, tm, LANES), F32),
            jax.ShapeDtypeStruct((B, S // tm, 2 * ML_HEADS, tm), F32),
            jax.ShapeDtypeStruct((B, S, D), F32),
        ),
        grid=(B, S // tm),
        in_specs=[pl.BlockSpec((1, tm, D), row)] + _pending_specs(D, tm) + [
            pl.BlockSpec((1, D), const2),
            pl.BlockSpec((1, 1, D), per_b),
            pl.BlockSpec((1, 1, D), per_b),
            pl.BlockSpec((D, ML_COLS), const2),
            pl.BlockSpec((1, LANES), const2),
        ],
        out_specs=(
            pl.BlockSpec((1, tm, 512), row),
            pl.BlockSpec((1, 1, 512, tm), lambda b, j: (b, j, 0, 0)),
            pl.BlockSpec((1, tm, 1024), row),
            pl.BlockSpec((1, tm, 1024), row),
            pl.BlockSpec((1, 1, tm, LANES), lambda b, j: (b, j, 0, 0)),
            pl.BlockSpec((1, 1, 2 * ML_HEADS, tm), lambda b, j: (b, j, 0, 0)),
            pl.BlockSpec((1, tm, D), row),
        ),
        compiler_params=_cparams("parallel", "parallel"),
        name="mlstm_in_proj",
    )(x, *pending, gain, sc, sh, w_pad, gate_bias)


def _log_sigmoid(f):
    return jnp.minimum(f, 0.0) - jnp.log1p(jnp.exp(-jnp.abs(f)))


def _split3(x):
    a = x.astype(BF16)
    r = x - a.astype(F32)
    b = r.astype(BF16)
    c = (r - b.astype(F32)).astype(BF16)
    return a, b, c


def _twice(a):
    return jnp.concatenate([a, a], axis=1)


def _mlstm_kernel(q_ref, kt_ref, v_ref, grow_ref, gcol_ref, gain_ref, o_ref, c_ref, m_ref):
    L = ML_CHUNK
    HP = ML_STEP_HEADS
    S = q_ref.shape[1]
    c_ref[...] = jnp.zeros_like(c_ref)
    m_ref[...] = jnp.zeros_like(m_ref)

    def chunk(c, carry):
        r0 = pl.multiple_of(c * L, L)
        r_i = lax.broadcasted_iota(jnp.int32, (L, L), 0)
        c_i = lax.broadcasted_iota(jnp.int32, (L, L), 1)
        tril = c_i <= r_i
        lower = jnp.where(tril, 1.0, 0.0).astype(BF16)
        upper = jnp.where(r_i <= c_i, 1.0, 0.0).astype(BF16)
        e_r = lax.broadcasted_iota(jnp.int32, (LANES, HP * LANES), 0)
        e_c = lax.broadcasted_iota(jnp.int32, (LANES, HP * LANES), 1)
        pick = jnp.where(e_r == HP + e_c // LANES, 1.0, 0.0).astype(BF16)
        rows = grow_ref[0, 0, c]
        cols = gcol_ref[0, 0, c]
        b_rows = sum(_dot(p, upper) for p in _split3(_log_sigmoid(rows)))
        b_cols = sum(_dot(lower, p) for p in _split3(_log_sigmoid(cols)))
        b_colr = sum(_dot(p, pick) for p in _split3(b_cols))
        lane = lax.broadcasted_iota(jnp.int32, (1, L), 1)
        b_last_all = jnp.sum(jnp.where(lane == L - 1, b_rows, 0.0), axis=-1, keepdims=True)
        ones_v = jnp.ones((L, ML_V_DIM), BF16)
        for hh in range(HP):
            q = q_ref[0, pl.ds(r0, L), hh * ML_QK_DIM:(hh + 1) * ML_QK_DIM]
            kt = kt_ref[0, c, hh * ML_QK_DIM:(hh + 1) * ML_QK_DIM, :]
            v = v_ref[0, pl.ds(r0, L), hh * ML_V_DIM:(hh + 1) * ML_V_DIM]
            vx = jnp.concatenate([v, ones_v], axis=1)
            i_row = rows[hh:hh + 1, :]
            b_row = b_rows[HP + hh:HP + hh + 1, :]
            b_last = b_last_all[HP + hh:HP + hh + 1, :]
            b_col = b_colr[:, hh * LANES:(hh + 1) * LANES]
            m_prev = m_ref[hh]
            ctn = c_ref[hh]

            dmat = jnp.where(tril, _twice(b_col) - b_row + i_row, NEG_INF)
            inter = b_col + m_prev
            m_t = jnp.maximum(inter, jnp.max(dmat, axis=-1, keepdims=True))
            w_intra = jnp.exp(dmat - _twice(m_t))
            w_inter = jnp.exp(inter - m_t)
            intra = (w_intra * _dot(q, kt)).astype(BF16)
            tot = _twice(w_inter) * _dot(q, ctn.astype(BF16)) + _dot(intra, vx)
            num = tot[:, :ML_V_DIM]
            den = tot[:, ML_V_DIM:]
            hc = num / jnp.maximum(jnp.abs(den), jnp.exp(-m_t))
            y = hc * lax.rsqrt(jnp.mean(hc * hc, axis=-1, keepdims=True) + NORM_EPS)
            o_ref[0, pl.ds(r0, L), hh * ML_V_DIM:(hh + 1) * ML_V_DIM] = (
                y * gain_ref[:, hh * ML_V_DIM:(hh + 1) * ML_V_DIM])

            g_row = b_last - b_row + i_row
            m_new = jnp.maximum(b_last + m_prev, jnp.max(g_row, axis=-1, keepdims=True))
            decay = jnp.exp(b_last + m_prev - m_new)
            kw = (kt.astype(F32) * jnp.exp(g_row - _twice(m_new))).astype(BF16)
            c_ref[hh] = _twice(decay) * ctn + _dot(kw, vx)
            m_ref[hh] = m_new
        return carry

    lax.fori_loop(0, S // L, chunk, 0)


def _mlstm(q, kt, v, g_rows, g_cols, out_gain):
    B, S, _ = q.shape
    assert ML_CHUNK == 2 * LANES
    nc = S // ML_CHUNK
    hp = ML_STEP_HEADS
    return pl.pallas_call(
        _mlstm_kernel,
        out_shape=jax.ShapeDtypeStruct((B, S, ML_HEADS * ML_V_DIM), F32),
        grid=(B, ML_HEADS // hp),
        in_specs=[
            pl.BlockSpec((1, S, hp * ML_QK_DIM), lambda b, p: (b, 0, p)),
            pl.BlockSpec((1, nc, hp * ML_QK_DIM, ML_CHUNK), lambda b, p: (b, 0, p, 0)),
            pl.BlockSpec((1, S, hp * ML_V_DIM), lambda b, p: (b, 0, p)),
            pl.BlockSpec((1, 1, nc, 2 * hp, ML_CHUNK), lambda b, p: (b, p, 0, 0, 0)),
            pl.BlockSpec((1, 1, nc, ML_CHUNK, LANES), lambda b, p: (b, p, 0, 0, 0)),
            pl.BlockSpec((1, hp * ML_V_DIM), lambda b, p: (0, p)),
        ],
        out_specs=pl.BlockSpec((1, S, hp * ML_V_DIM), lambda b, p: (b, 0, p)),
        scratch_shapes=[
            pltpu.VMEM((hp, ML_QK_DIM, ML_V_DIM + LANES), F32),
            pltpu.VMEM((hp, 1, LANES), F32),
        ],
        compiler_params=_cparams("parallel", "parallel"),
        name="mlstm_chunkwise",
    )(q, kt, v, g_rows, g_cols, out_gain)


def _ml_out_kernel(hh_ref, og_ref, w_ref, x_ref, g_ref,
                   gain2_ref, sc2_ref, sh2_ref, wr_ref, br_ref, o_ref, h_ref, r_ref, rt_ref, cnt_ref, run_ref):
    a = jax.nn.sigmoid(og_ref[0]) * hh_ref[0]
    x_new = x_ref[0] + g_ref[0] * _dot(a.astype(BF16), w_ref[...])
    o_ref[0] = x_new
    _route_tile(x_new, gain2_ref, sc2_ref, sh2_ref, wr_ref, br_ref, h_ref, r_ref, rt_ref, cnt_ref, run_ref)


def _ml_out(hh, og, w_out_bf, x, g1, route_args):
    B, S, D = x.shape
    tm = OUT_TILE
    row = lambda b, j: (b, j, 0)
    per_b = lambda b, j: (b, 0, 0)
    const2 = lambda b, j: (0, 0)
    r_in, r_shape, r_out, r_scratch = _route_specs(B, S, D, tm)
    return pl.pallas_call(
        _ml_out_kernel,
        out_shape=(jax.ShapeDtypeStruct((B, S, D), F32),) + r_shape,
        grid=(B, S // tm),
        in_specs=[
            pl.BlockSpec((1, tm, D), row),
            pl.BlockSpec((1, tm, D), row),
            pl.BlockSpec((D, D), const2),
            pl.BlockSpec((1, tm, D), row),
            pl.BlockSpec((1, 1, D), per_b),
        ] + r_in,
        out_specs=(pl.BlockSpec((1, tm, D), row),) + r_out,
        scratch_shapes=r_scratch,
        compiler_params=_cparams("arbitrary", "arbitrary"),
        name="mlstm_out_proj",
    )(hh, og, w_out_bf, x, g1, *route_args)


def _first_argmax(x, lane, width):
    mx = jnp.max(x, axis=-1, keepdims=True)
    idx = jnp.min(jnp.where(x == mx, lane, width), axis=-1, keepdims=True)
    return mx, idx


def _route_tile(x, gain_ref, sc_ref, sh_ref, w_ref, b_ref, h_ref, r_ref, rt_ref, cnt_ref, run_ref):
    tm = x.shape[0]

    @pl.when(jnp.logical_and(pl.program_id(0) == 0, pl.program_id(1) == 0))
    def _():
        run_ref[...] = jnp.zeros_like(run_ref)

    h = _modulated_norm(x, gain_ref[...], sc_ref[0], sh_ref[0])
    h_ref[0] = _pack_bf16_pairs(h)
    logits = _dot(h.astype(BF16), w_ref[...]) + b_ref[...]
    lane = lax.broadcasted_iota(jnp.int32, (1, LANES), 1)

    def pick(lgt):
        lg = jnp.where(lane < N_GROUPS, lgt, NEG_INF)
        g_max, g_sel = _first_argmax(lg, lane, LANES)
        pg = 1.0 / jnp.sum(jnp.exp(lg - g_max), axis=-1, keepdims=True)
        e_lane = lane - N_GROUPS
        in_grp = jnp.logical_and(e_lane >= g_sel * EXPERTS_PER_GROUP, e_lane < (g_sel + 1) * EXPERTS_PER_GROUP)
        le = jnp.where(in_grp, lgt, NEG_INF)
        v1, i1 = _first_argmax(le, lane, LANES)
        v2, i2 = _first_argmax(jnp.where(lane == i1, NEG_INF, le), lane, LANES)
        ratio = jnp.exp(v2 - v1)
        return i1 - N_GROUPS, i2 - N_GROUPS, pg / (1.0 + ratio), pg * ratio / (1.0 + ratio)

    e1, e2, w1, w2 = pick(logits)
    hot1 = lane == e1
    hot2 = lane == e2
    onehot = jnp.where(jnp.logical_or(hot1, hot2), 1.0, 0.0)
    r_i = lax.broadcasted_iota(jnp.int32, (tm, tm), 0)
    c_i = lax.broadcasted_iota(jnp.int32, (tm, tm), 1)
    before = jnp.where(c_i < r_i, 1.0, 0.0).astype(BF16)
    seen = _dot(before, onehot.astype(BF16)) + run_ref[...]
    rank1 = jnp.sum(jnp.where(hot1, seen, 0.0), axis=-1, keepdims=True)
    rank2 = jnp.sum(jnp.where(hot2, seen, 0.0), axis=-1, keepdims=True)
    run_ref[...] = run_ref[...] + jnp.sum(onehot, axis=0, keepdims=True)
    cnt_ref[...] = run_ref[...]

    out = jnp.where(lane == 0, e1.astype(F32), 0.0)
    out = jnp.where(lane == 1, e2.astype(F32), out)
    out = jnp.where(lane == 2, w1, out)
    out = jnp.where(lane == 3, w2, out)
    out = jnp.where(lane == 4, rank1, out)
    out = jnp.where(lane == 5, rank2, out)
    r_ref[0] = out
    rt_ref[0] = out.T[:8, :]


def _route_specs(B, S, D, tm):
    row = lambda b, j: (b, j, 0)
    per_b = lambda b, j: (b, 0, 0)
    const2 = lambda b, j: (0, 0)
    in_specs = [
        pl.BlockSpec((1, D), const2),
        pl.BlockSpec((1, 1, D), per_b),
        pl.BlockSpec((1, 1, D), per_b),
        pl.BlockSpec((D, LANES), const2),
        pl.BlockSpec((1, LANES), const2),
    ]
    out_shape = (
        jax.ShapeDtypeStruct((B, S, D // 2), jnp.int32),
        jax.ShapeDtypeStruct((B, S, LANES), F32),
        jax.ShapeDtypeStruct((B, 8, S), F32),
        jax.ShapeDtypeStruct((1, LANES), F32),
    )
    out_specs = (
        pl.BlockSpec((1, tm, D // 2), row),
        pl.BlockSpec((1, tm, LANES), row),
        pl.BlockSpec((1, 8, tm), lambda b, j: (b, 0, j)),
        pl.BlockSpec((1, LANES), const2),
    )
    return in_specs, out_shape, out_specs, [pltpu.VMEM((1, LANES), F32)]


def _experts_kernel(blk_e_ref, n_used_ref, next_e_ref, x_ref, wg_hbm, wu_hbm, wd_hbm, o_ref,
                    wg_f, wu_f, wd_f, wg_s, wu_s, wd_s, sem, *, layer):
    i = pl.program_id(0)
    used = i < n_used_ref[0]
    e = blk_e_ref[i]
    new_expert = jnp.logical_or(i == 0, e != blk_e_ref[jnp.maximum(i - 1, 0)])

    def fetch(expert):
        return (pltpu.make_async_copy(wg_hbm.at[layer, expert], wg_f, sem.at[0]),
                pltpu.make_async_copy(wu_hbm.at[layer, expert], wu_f, sem.at[1]),
                pltpu.make_async_copy(wd_hbm.at[layer, expert], wd_f, sem.at[2]))

    @pl.when(i == 0)
    def _():
        for cp in fetch(e):
            cp.start()

    @pl.when(jnp.logical_and(used, new_expert))
    def _():
        for cp in fetch(e):
            cp.wait()
        wg_s[...] = wg_f[...].astype(BF16)
        wu_s[...] = wu_f[...].astype(BF16)
        wd_s[...] = wd_f[...].astype(BF16)

        @pl.when(next_e_ref[i] >= 0)
        def _():
            for cp in fetch(next_e_ref[i]):
                cp.start()

    @pl.when(used)
    def _():
        x = _unpack_bf16_pairs(x_ref[...]).astype(BF16)
        a = _dot(x, wg_s[...])
        u = _dot(x, wu_s[...])
        act = a * jax.nn.sigmoid(a) * u
        o_ref[...] = _pack_bf16_pairs(_dot(act.astype(BF16), wd_s[...]))

    @pl.when(i >= n_used_ref[0])
    def _():
        o_ref[...] = jnp.zeros_like(o_ref)


def _experts(layer, blk_e, n_used, next_e, xs, w_gate, w_up, w_down):
    R = xs.shape[0]
    D = 2 * xs.shape[1]
    n_blk = R // MOE_BLOCK
    rows = lambda i, be, nu, ne: (i, 0)
    grid_spec = pltpu.PrefetchScalarGridSpec(
        num_scalar_prefetch=3,
        grid=(n_blk,),
        in_specs=[
            pl.BlockSpec((MOE_BLOCK, D // 2), rows),
            pl.BlockSpec(memory_space=pl.ANY),
            pl.BlockSpec(memory_space=pl.ANY),
            pl.BlockSpec(memory_space=pl.ANY),
        ],
        out_specs=pl.BlockSpec((MOE_BLOCK, D // 2), rows),
        scratch_shapes=[
            pltpu.VMEM((D, D_EXPERT), F32),
            pltpu.VMEM((D, D_EXPERT), F32),
            pltpu.VMEM((D_EXPERT, D), F32),
            pltpu.VMEM((D, D_EXPERT), BF16),
            pltpu.VMEM((D, D_EXPERT), BF16),
            pltpu.VMEM((D_EXPERT, D), BF16),
            pltpu.SemaphoreType.DMA((3,)),
        ],
    )
    return pl.pallas_call(
        functools.partial(_experts_kernel, layer=layer),
        out_shape=jax.ShapeDtypeStruct((R, D // 2), jnp.int32),
        grid_spec=grid_spec,
        compiler_params=_cparams("arbitrary"),
        name="moe_experts",
    )(blk_e, n_used, next_e, xs, w_gate, w_up, w_down)


def _residual_tile(x_ref, pending_refs):
    if not pending_refs:
        return x_ref[0]
    g_ref, y0_ref, y1_ref, r_ref = pending_refs
    r = r_ref[0]
    y = _unpack_bf16_pairs(y0_ref[0, 0]) * r[:, 2:3] + _unpack_bf16_pairs(y1_ref[0, 0]) * r[:, 3:4]
    return x_ref[0] + g_ref[0] * y


def _pending_specs(D, tm):
    return [
        pl.BlockSpec((1, 1, D), lambda b, j: (b, 0, 0)),
        pl.BlockSpec((1, 1, tm, D // 2), lambda b, j: (0, b, j, 0)),
        pl.BlockSpec((1, 1, tm, D // 2), lambda b, j: (1, b, j, 0)),
        pl.BlockSpec((1, tm, LANES), lambda b, j: (b, j, 0)),
    ]


def _combine_kernel(x_ref, g_ref, y0_ref, y1_ref, r_ref, o_ref):
    o_ref[0] = _residual_tile(x_ref, (g_ref, y0_ref, y1_ref, r_ref))


def _combine(x, pending):
    B, S, D = x.shape
    tm = TOKEN_TILE
    row = lambda b, j: (b, j, 0)
    return pl.pallas_call(
        _combine_kernel,
        out_shape=jax.ShapeDtypeStruct((B, S, D), F32),
        grid=(B, S // tm),
        in_specs=[pl.BlockSpec((1, tm, D), row)] + _pending_specs(D, tm),
        out_specs=pl.BlockSpec((1, tm, D), row),
        compiler_params=_cparams("parallel", "parallel"),
        name="moe_combine",
    )(x, *pending)


SC_CORES = 2
SC_SUBCORES = 16
SC_WORKERS = SC_CORES * SC_SUBCORES
SC_CHUNK = 64


def _sc_mesh():
    return plsc.VectorSubcoreMesh(core_axis_name="c", subcore_axis_name="s",
                                  num_cores=SC_CORES, num_subcores=SC_SUBCORES)


def _sc_scatter_rows(src, idx, n_out):
    T, W = src.shape
    per_w = T // SC_WORKERS
    nch = per_w // SC_CHUNK
    idx4 = idx.reshape(TOP_K, SC_WORKERS, nch, SC_CHUNK)

    @functools.partial(
        pl.kernel, mesh=_sc_mesh(),
        out_type=jax.ShapeDtypeStruct((n_out, W), src.dtype),
        scratch_types=[pltpu.VMEM((TOP_K, nch, SC_CHUNK), jnp.int32), pltpu.VMEM((SC_CHUNK, W), src.dtype)],
        name="sc_scatter_rows",
    )
    def body(src_hbm, idx_hbm, out_hbm, idx_v, rows_v):
        wid = lax.axis_index("s") * SC_CORES + lax.axis_index("c")
        for s in range(TOP_K):
            pltpu.sync_copy(idx_hbm.at[s, wid], idx_v.at[s])

        @pl.loop(0, nch)
        def _(i):
            pltpu.sync_copy(src_hbm.at[pl.ds(wid * per_w + i * SC_CHUNK, SC_CHUNK)], rows_v)
            for s in range(TOP_K):
                pltpu.sync_copy(rows_v, out_hbm.at[idx_v.at[s, i]])

    return body(src, idx4)


def _sc_gather_rows(table, idx):
    N = idx.shape[0]
    W = table.shape[1]
    per_w = N // SC_WORKERS
    nch = per_w // SC_CHUNK
    idx3 = idx.reshape(SC_WORKERS, nch, SC_CHUNK)

    @functools.partial(
        pl.kernel, mesh=_sc_mesh(),
        out_type=jax.ShapeDtypeStruct((N, W), table.dtype),
        scratch_types=[
            pltpu.VMEM((nch, SC_CHUNK), jnp.int32),
            pltpu.VMEM((2, SC_CHUNK, W), table.dtype),
            pltpu.SemaphoreType.DMA((2,)),
            pltpu.SemaphoreType.DMA((2,)),
        ],
        name="sc_gather_rows",
    )
    def body(table_hbm, idx_hbm, out_hbm, idx_v, rows_v, gather_sem, write_sem):
        wid = lax.axis_index("s") * SC_CORES + lax.axis_index("c")
        pltpu.sync_copy(idx_hbm.at[wid], idx_v)

        def gather(j, b):
            return pltpu.make_async_copy(table_hbm.at[idx_v.at[j]], rows_v.at[b], gather_sem.at[b])

        def write(j, b):
            return pltpu.make_async_copy(
                rows_v.at[b], out_hbm.at[pl.ds(wid * per_w + j * SC_CHUNK, SC_CHUNK)], write_sem.at[b])

        gather(0, 0).start()

        @pl.loop(0, nch, step=2)
        def _(i):
            for b in range(2):
                j = i + b

                @pl.when(j >= 1)
                def _():
                    write(j - 1, 1 - b).wait()

                @pl.when(j + 1 < nch)
                def _():
                    gather(j + 1, 1 - b).start()

                gather(j, b).wait()
                write(j, b).start()

        write(nch - 1, (nch - 1) % 2).wait()

    assert nch % 2 == 0
    return body(table, idx3)


def _moe_dispatch(route_t, counts, T):
    A = T * TOP_K
    counts = counts[0, :N_EXPERTS].astype(jnp.int32)
    blocks_per = (counts + MOE_BLOCK - 1) // MOE_BLOCK
    block_end = jnp.cumsum(blocks_per)
    block_start = block_end - blocks_per
    expert = jnp.swapaxes(route_t[:, :TOP_K, :], 0, 1).reshape(TOP_K, T).astype(jnp.int32)
    rank = jnp.swapaxes(route_t[:, 4:4 + TOP_K, :], 0, 1).reshape(TOP_K, T).astype(jnp.int32)
    onehot = expert[None] == jnp.arange(N_EXPERTS, dtype=jnp.int32)[:, None, None]
    start = jnp.sum(jnp.where(onehot, block_start[:, None, None], 0), axis=0)
    dest = start * MOE_BLOCK + rank
    n_blk = -(-A // MOE_BLOCK) + N_EXPERTS
    blk = jnp.arange(n_blk, dtype=jnp.int32)
    blk_e = jnp.minimum(jnp.sum(blk[:, None] >= block_end[None, :], axis=-1), N_EXPERTS - 1).astype(jnp.int32)
    n_used = block_end[-1]
    first = jnp.logical_and(blk < n_used, jnp.logical_or(blk == 0, blk_e != jnp.roll(blk_e, 1)))
    first_pos = jnp.where(first, blk, n_blk)
    next_pos = jnp.concatenate([lax.cummin(first_pos, axis=0, reverse=True)[1:], jnp.full((1,), n_blk, jnp.int32)])
    next_e = jnp.where(next_pos < n_blk, blk_e[jnp.minimum(next_pos, n_blk - 1)], -1).astype(jnp.int32)
    return dest, n_blk * MOE_BLOCK, blk_e, n_used.reshape(1).astype(jnp.int32), next_e


def _rope_tables(S):
    inv = 1.0 / (ROPE_THETA ** (jnp.arange(0, HEAD_DIM, 2, dtype=F32) / HEAD_DIM))
    ang = jnp.arange(S, dtype=F32)[:, None] * inv[None, :]
    cos, sin = jnp.cos(ang), jnp.sin(ang)
    cos_h = jnp.concatenate([cos, cos], axis=-1)
    sin_h = jnp.concatenate([-sin, sin], axis=-1)
    return jnp.tile(cos_h, (1, ATTN_HEADS)), jnp.tile(sin_h, (1, ATTN_HEADS))


def _pad_cols(w, width):
    return jnp.pad(w, ((0, 0), (0, width - w.shape[1])))


def kernel(x, c, ada_w, ada_b, norm_mix, norm_ffn, hy_w_in, hy_q_norm, hy_k_norm, hy_conv_w, hy_w_out, ml_w_in, ml_b_gates, ml_out_norm, ml_w_out, moe_w_group, moe_b_group, moe_w_expert, moe_b_expert, moe_w_gate, moe_w_up, moe_w_down):
    B, S, D = x.shape
    T = B * S
    cos_t, sin_t = _rope_tables(S)
    mod = _ada_modulation(c, ada_w, ada_b).reshape(DEPTH, B, 6, 1, D)
    r_i = np.arange(ATTN_WIDTH)
    grp = jnp.asarray((r_i[:, None] // HEAD_DIM) == (r_i[None, :] // HEAD_DIM), dtype=BF16)

    pending = ()
    for l in range(DEPTH):
        sh1, sc1, g1, sh2, sc2, g2 = [mod[l, :, i] for i in range(6)]
        gain1 = norm_mix[l].reshape(1, D)
        w_r = _pad_cols(jnp.concatenate([moe_w_group[l], moe_w_expert[l]], axis=1), LANES).astype(BF16)
        b_r = jnp.pad(jnp.concatenate([moe_b_group[l], moe_b_expert[l]]), (0, LANES - N_GROUPS - N_EXPERTS))
        route_args = (norm_ffn[l].reshape(1, D), sc2, sh2, w_r, b_r.reshape(1, LANES))
        j = l // 2
        if l % 2 == 0:
            w = hy_w_in[j]
            o = np.cumsum((0,) + (ATTN_WIDTH, HEAD_DIM, HEAD_DIM, IDX_HEADS * IDX_DIM, IDX_DIM, IDX_HEADS,
                                  CONV_WIDTH, CONV_WIDTH, CONV_WIDTH))
            wq, wk, wv, wiq, wik, wiw, wbg, wcg, wu = [w[:, o[i]:o[i + 1]] for i in range(9)]
            w_pad = jnp.concatenate(
                [wk, wv, _pad_cols(jnp.concatenate([wik, wiw], axis=1), LANES), wq, wiq, wbg, wcg, wu],
                axis=1).astype(BF16)
            qn_t = jnp.tile(hy_q_norm[j], ATTN_HEADS).reshape(1, ATTN_WIDTH)
            kn_t = jnp.tile(hy_k_norm[j], LANES // HEAD_DIM).reshape(1, LANES)
            outs = _hyb_in(x, pending, gain1, sc1, sh1, w_pad, cos_t, sin_t, qn_t, kn_t, grp)
            qt, iqt, bcu, kv, kvt, ik, iwt = outs[:7]
            if pending:
                x = outs[7]
            y_attn = _dsa_attention_t(qt, iqt, iwt, kv, kvt, ik)
            x, h2, route, route_t, counts =_hyb_out(y_attn, bcu, hy_conv_w[j], hy_w_out[j].astype(BF16), x, g1, route_args)
        else:
            w = ml_w_in[j]
            hq = ML_HEADS * ML_QK_DIM
            hv = ML_HEADS * ML_V_DIM
            wq, wk, wv = w[:, :hq], w[:, hq:2 * hq], w[:, 2 * hq:2 * hq + hv]
            wg = w[:, 2 * hq + hv:2 * hq + hv + 2 * ML_HEADS]
            wo = w[:, 2 * hq + hv + 2 * ML_HEADS:]
            w_pad = jnp.concatenate([_pad_cols(wg, LANES), wk, wq, wv, wo], axis=1).astype(BF16)
            gate_bias = jnp.pad(ml_b_gates[j], (0, LANES - 2 * ML_HEADS)).reshape(1, LANES)
            q, k, v, og, g_cols, g_rows, x = _ml_in(x, pending, gain1, sc1, sh1, w_pad, gate_bias)
            assert ML_STEP_HEADS == ML_HEADS
            hh = _mlstm(q, k, v, g_rows[:, None], g_cols[:, None], ml_out_norm[j].reshape(1, hv))
            x, h2, route, route_t, counts =_ml_out(hh, og, ml_w_out[j].astype(BF16), x, g1, route_args)

        dest, n_rows, blk_e, n_used, next_e = _moe_dispatch(route_t, counts, T)
        xs = _sc_scatter_rows(h2.reshape(T, D // 2), dest, n_rows)
        ys = _experts(l, blk_e, n_used, next_e, xs, moe_w_gate, moe_w_up, moe_w_down)
        y01 = _sc_gather_rows(ys, dest.reshape(TOP_K * T)).reshape(TOP_K, B, S, D // 2)
        pending = (g2, y01, y01, route)
    return _combine(x, pending)
```

```python
import functools

import numpy as np
import jax
import jax.numpy as jnp
from jax import lax
from jax.experimental import pallas as pl
from jax.experimental.pallas import tpu as pltpu
from jax.experimental.pallas import tpu_sc as plsc

F32 = jnp.float32
BF16 = jnp.bfloat16
HIGHEST = lax.Precision.HIGHEST

D_MODEL = 1024
DEPTH = 4
ATTN_HEADS = 8
HEAD_DIM = 64
ATTN_WIDTH = ATTN_HEADS * HEAD_DIM
IDX_HEADS = 8
IDX_DIM = 64
INDEX_TOPK = 256
Q_BLOCK = 256
ROPE_THETA = 10000.0
CONV_WIDTH = D_MODEL - ATTN_WIDTH
CONV_K = 3
ML_HEADS = 8
ML_QK_DIM = 64
ML_V_DIM = 128
N_GROUPS = 4
EXPERTS_PER_GROUP = 8
N_EXPERTS = N_GROUPS * EXPERTS_PER_GROUP
TOP_K = 2
D_EXPERT = 512
MOE_BLOCK = 512
NORM_EPS = 1e-6

LANES = 128
VMEM_LIMIT = 56 * 1024 * 1024
TOKEN_TILE = 256
OUT_TILE = 512
ML_CHUNK = 256
ML_STEP_HEADS = 8
NEG_INF = float("-inf")


def _cparams(*sem):
    return pltpu.CompilerParams(dimension_semantics=sem, vmem_limit_bytes=VMEM_LIMIT)


def _dot(a, b):
    return jnp.dot(a, b, preferred_element_type=F32)


def _pack_bf16_pairs(x):
    bits = lax.bitcast_convert_type(x.astype(BF16).astype(F32), jnp.uint32)
    half = bits.shape[1] // 2
    packed = (bits[:, :half] >> 16) | (bits[:, half:] & jnp.uint32(0xFFFF0000))
    return lax.bitcast_convert_type(packed, jnp.int32)


def _unpack_bf16_pairs(words):
    words = lax.bitcast_convert_type(words, jnp.uint32)
    return jnp.concatenate(
        [lax.bitcast_convert_type(words << 16, F32),
         lax.bitcast_convert_type(words & jnp.uint32(0xFFFF0000), F32)], axis=1)


def _split_dot(a_f32, b_bf16):
    hi = a_f32.astype(BF16)
    lo = (a_f32 - hi.astype(F32)).astype(BF16)
    return _dot(hi, b_bf16) + _dot(lo, b_bf16)


def _ada_kernel(c_ref, w_ref, b_ref, o_ref):
    c = c_ref[...]
    ca = c * jax.nn.sigmoid(c)
    o_ref[0] = jnp.dot(ca, w_ref[0], precision=HIGHEST, preferred_element_type=F32) + b_ref[0]


def _ada_modulation(c, ada_w, ada_b):
    B, D = c.shape
    n_col = ada_w.shape[-1] // D
    return pl.pallas_call(
        _ada_kernel,
        out_shape=jax.ShapeDtypeStruct((DEPTH, B, n_col * D), F32),
        grid=(DEPTH, n_col),
        in_specs=[
            pl.BlockSpec((B, D), lambda l, j: (0, 0)),
            pl.BlockSpec((1, D, D), lambda l, j: (l, 0, j)),
            pl.BlockSpec((1, 1, D), lambda l, j: (l, 0, j)),
        ],
        out_specs=pl.BlockSpec((1, B, D), lambda l, j: (l, 0, j)),
        compiler_params=_cparams("parallel", "parallel"),
        name="ada_modulation",
    )(c, ada_w, ada_b.reshape(DEPTH, 1, n_col * D))


def _modulated_norm(x, gain, scale, shift):
    y = x * lax.rsqrt(jnp.mean(x * x, axis=-1, keepdims=True) + NORM_EPS)
    return y * gain * (1.0 + scale) + shift


def _rope(x, cos, sin_signed, first_half):
    w = x.shape[-1]
    partner = jnp.where(first_half, pltpu.roll(x, w - HEAD_DIM // 2, 1), pltpu.roll(x, HEAD_DIM // 2, 1))
    return x * cos + partner * sin_signed


HYB_COLS = 5 * 512 + 2 * LANES


def _hyb_in_kernel(*refs, n_pending):
    x_ref, pending_refs, refs = refs[0], refs[1:1 + n_pending], refs[1 + n_pending:]
    (gain_ref, sc_ref, sh_ref, w_ref, cos_ref, sin_ref, qn_ref, kn_ref, grp_ref,
     qt_ref, iqt_ref, bcu_ref, kv_ref, kvt_ref, ik_ref, iwt_ref) = refs[:16]
    x = _residual_tile(x_ref, pending_refs)
    if n_pending:
        refs[16][0] = x
    h = _modulated_norm(x, gain_ref[...], sc_ref[0], sh_ref[0])
    hb = h.astype(BF16)
    p_small = _dot(hb, w_ref[:, 0:2 * LANES])
    p_q = _dot(hb, w_ref[:, 2 * LANES:2 * LANES + 512])
    p_iq = _dot(hb, w_ref[:, 2 * LANES + 512:2 * LANES + 1024])
    cos = cos_ref[...]
    sin = sin_ref[...]
    lane = lax.broadcasted_iota(jnp.int32, (1, ATTN_WIDTH), 1)
    first_half = (lane % HEAD_DIM) < (HEAD_DIM // 2)
    fh128 = first_half[:, :LANES]
    lane128 = lane[:, :LANES]

    kv = p_small[:, :LANES]
    is_k = lane128 < HEAD_DIM
    kk = jnp.where(is_k, kv, 0.0)
    ms_k = jnp.sum(kk * kk, axis=-1, keepdims=True) * (1.0 / HEAD_DIM)
    kn = kv * lax.rsqrt(ms_k + NORM_EPS) * kn_ref[...]
    kr = _rope(kn, cos[:, :LANES], sin[:, :LANES], fh128)
    kv = jnp.where(is_k, kr, kv)
    kv_ref[0] = kv.astype(BF16)
    feat = lax.broadcasted_iota(jnp.int32, (LANES, 1), 0)
    kvt_ref[0, 0] = jnp.where(feat < HEAD_DIM, 1.0, kv.T).astype(BF16)

    sm = p_small[:, LANES:]
    ikr = _rope(sm, cos[:, :LANES], sin[:, :LANES], fh128)
    ik_ref[0] = jnp.where(is_k, ikr, 0.0).astype(BF16)
    iwt_ref[0] = sm.T

    ms = _split_dot(p_q * p_q, grp_ref[...]) * (1.0 / HEAD_DIM)
    q = p_q * lax.rsqrt(ms + NORM_EPS) * qn_ref[...]
    qt_ref[0] = (_rope(q, cos, sin, first_half) * (HEAD_DIM ** -0.5)).T.astype(BF16)
    iqt_ref[0] = (_rope(p_iq, cos, sin, first_half) * (IDX_DIM ** -0.5)).T.astype(BF16)

    bcu_ref[0] = _dot(hb, w_ref[:, 2 * LANES + 1024:])


def _hyb_in(x, pending, gain, sc, sh, w_pad, cos_t, sin_t, qn_t, kn_t, grp):
    B, S, D = x.shape
    tm = TOKEN_TILE
    row = lambda b, j: (b, j, 0)
    per_b = lambda b, j: (b, 0, 0)
    const2 = lambda b, j: (0, 0)
    tab = lambda b, j: (j, 0)
    col = lambda b, j: (b, 0, j)
    assert tm == DSA_KEY_CHUNK
    x_out_shape = (jax.ShapeDtypeStruct((B, S, D), F32),) if pending else ()
    x_out_spec = (pl.BlockSpec((1, tm, D), row),) if pending else ()
    return pl.pallas_call(
        functools.partial(_hyb_in_kernel, n_pending=len(pending)),
        out_shape=(
            jax.ShapeDtypeStruct((B, 512, S), BF16),
            jax.ShapeDtypeStruct((B, 512, S), BF16),
            jax.ShapeDtypeStruct((B, S, 1536), F32),
            jax.ShapeDtypeStruct((B, S, LANES), BF16),
            jax.ShapeDtypeStruct((B, S // tm, LANES, tm), BF16),
            jax.ShapeDtypeStruct((B, S, LANES), BF16),
            jax.ShapeDtypeStruct((B, LANES, S), F32),
        ) + x_out_shape,
        grid=(B, S // tm),
        in_specs=[pl.BlockSpec((1, tm, D), row)] + (_pending_specs(D, tm) if pending else []) + [
            pl.BlockSpec((1, D), const2),
            pl.BlockSpec((1, 1, D), per_b),
            pl.BlockSpec((1, 1, D), per_b),
            pl.BlockSpec((D, HYB_COLS), const2),
            pl.BlockSpec((tm, 512), tab),
            pl.BlockSpec((tm, 512), tab),
            pl.BlockSpec((1, 512), const2),
            pl.BlockSpec((1, LANES), const2),
            pl.BlockSpec((512, 512), const2),
        ],
        out_specs=(
            pl.BlockSpec((1, 512, tm), col),
            pl.BlockSpec((1, 512, tm), col),
            pl.BlockSpec((1, tm, 1536), row),
            pl.BlockSpec((1, tm, LANES), row),
            pl.BlockSpec((1, 1, LANES, tm), lambda b, j: (b, j, 0, 0)),
            pl.BlockSpec((1, tm, LANES), row),
            pl.BlockSpec((1, LANES, tm), col),
        ) + x_out_spec,
        compiler_params=_cparams("parallel", "parallel"),
        name="hybrid_in_proj",
    )(x, *pending, gain, sc, sh, w_pad, cos_t, sin_t, qn_t, kn_t, grp)


DSA_KEY_CHUNK = 256
DSA_SUM_ROWS = 16


def _fold8(x, op):
    parts = x.reshape(x.shape[0] // 8, 8, x.shape[1])
    while parts.shape[0] > 1:
        half = parts.shape[0] // 2
        assert parts.shape[0] == 2 * half
        parts = op(parts[:half], parts[half:])
    return parts[0]


def _col_reduce(x, op):
    t = _fold8(x, op)
    for shift in (4, 2, 1):
        t = op(t, pltpu.roll(t, shift, 0))
    return t[0:1, :]


def _dsa_t_kernel(qt_ref, iqt_ref, iwt_ref, kv_ref, kvt_ref, ik_ref, o_ref, sc_ref, bias_ref, acc_ref):
    CK = DSA_KEY_CHUNK
    QB = Q_BLOCK
    qb = pl.program_id(1)
    nk = (qb * QB + QB + CK - 1) // CK
    kf = float(INDEX_TOPK)
    qpos = qb * QB + lax.broadcasted_iota(jnp.int32, (1, QB), 1)
    krow = lax.broadcasted_iota(jnp.int32, (CK, 1), 0)
    w_idx = iwt_ref[0, IDX_DIM:IDX_DIM + IDX_HEADS, :] * (IDX_HEADS ** -0.5)

    def rows(c):
        return pl.ds(pl.multiple_of(c * CK, CK), CK)

    def heads_on_lanes(ref, width):
        return jnp.concatenate([ref[0, hd * width:(hd + 1) * width, :] for hd in range(ref.shape[1] // width)], axis=1)

    def head_lanes(hd):
        return slice(hd * QB, (hd + 1) * QB)

    iq_wide = heads_on_lanes(iqt_ref, IDX_DIM)
    w_wide = jnp.concatenate([w_idx[hd:hd + 1, :] for hd in range(IDX_HEADS)], axis=1)

    def score_chunk(c, carry):
        mx, mn = carry
        ikc = ik_ref[0, rows(c), :][:, :IDX_DIM]
        s_all = jnp.maximum(_dot(ikc, iq_wide), 0.0) * w_wide
        acc = s_all[:, head_lanes(0)]
        for hd in range(1, IDX_HEADS):
            acc = acc + s_all[:, head_lanes(hd)]
        causal = (c * CK + krow) <= qpos
        sc_ref[rows(c), :] = jnp.where(causal, acc, NEG_INF)
        mx = jnp.maximum(mx, _fold8(jnp.where(causal, acc, NEG_INF), jnp.maximum))
        mn = jnp.minimum(mn, _fold8(jnp.where(causal, acc, jnp.inf), jnp.minimum))
        return mx, mn

    mx8, mn8 = lax.fori_loop(0, nk, score_chunk,
                             (jnp.full((8, QB), NEG_INF, F32), jnp.full((8, QB), jnp.inf, F32)))
    row_max = jnp.max(mx8, axis=0, keepdims=True)
    row_min = jnp.min(mn8, axis=0, keepdims=True)

    @pl.when(nk % 2 == 1)
    def _():
        sc_ref[rows(nk), :] = jnp.full((CK, QB), NEG_INF, F32)

    n_pairs = (nk + 1) // 2

    def pair_rows(c):
        return pl.ds(pl.multiple_of(c * (2 * CK), 2 * CK), 2 * CK)

    def count(pred):
        def body(c, part):
            return part + _fold8(jnp.where(pred(sc_ref[pair_rows(c), :]), 1.0, 0.0), jnp.add)
        part = lax.fori_loop(0, n_pairs, body, jnp.zeros((8, QB), F32))
        return jnp.sum(part, axis=0, keepdims=True)

    @pl.when(qb * QB + QB <= INDEX_TOPK)
    def _():
        def body(c, carry):
            bias_ref[rows(c), :] = jnp.where(sc_ref[rows(c), :] > NEG_INF, 0.0, NEG_INF)
            return carry
        lax.fori_loop(0, nk, body, 0)

    @pl.when(qb * QB + QB > INDEX_TOPK)
    def _():
        def bisect(_, carry):
            lo, hi, c_lo, c_hi = carry
            mid = 0.5 * lo + 0.5 * jnp.minimum(hi, row_max)
            cnt = count(lambda x: x >= mid)
            ge = cnt >= kf
            return (jnp.where(ge, mid, lo), jnp.where(ge, hi, mid),
                    jnp.where(ge, cnt, c_lo), jnp.where(ge, c_hi, cnt))

        n_adm = (qpos + 1).astype(F32)
        lo, hi, c_lo, c_hi = lax.fori_loop(
            0, 18, bisect, (row_min, jnp.full((1, QB), jnp.inf, F32), n_adm, jnp.zeros((1, QB), F32)))

        def refine_cond(carry):
            it, _, _, _, done = carry
            return jnp.logical_and(it < nk * CK, jnp.min(done) < 0.5)

        def refine(carry):
            it, hi, c_hi, thr, done = carry

            def edges(c, part):
                up, dn = part
                x = sc_ref[pair_rows(c), :]
                return (jnp.maximum(up, _fold8(jnp.where(x < hi, x, NEG_INF), jnp.maximum)),
                        jnp.minimum(dn, _fold8(jnp.where(x >= lo, x, jnp.inf), jnp.minimum)))

            up8, dn8 = lax.fori_loop(0, n_pairs, edges,
                                     (jnp.full((8, QB), NEG_INF, F32), jnp.full((8, QB), jnp.inf, F32)))
            m_up = jnp.max(up8, axis=0, keepdims=True)
            m_dn = jnp.min(dn8, axis=0, keepdims=True)
            from_hi = c_hi == kf - 1.0
            from_lo = c_lo == kf
            thr = jnp.where(done > 0.5, thr, jnp.where(from_hi, m_up, m_dn))
            done = jnp.where(jnp.logical_or(from_hi, from_lo), 1.0, done)

            def step_down(args):
                hi, c_hi, thr, done = args
                open_ = done < 0.5
                cnt = count(lambda x: x >= m_up)
                hit = jnp.logical_and(open_, cnt >= kf)
                moved = jnp.logical_and(open_, cnt < kf)
                return (jnp.where(moved, m_up, hi), jnp.where(moved, cnt, c_hi), jnp.where(hit, m_up, thr),
                        jnp.where(hit, 1.0, done))

            hi, c_hi, thr, done = lax.cond(jnp.min(done) > 0.5, lambda args: args, step_down, (hi, c_hi, thr, done))
            return it + 1, hi, c_hi, thr, done

        _, _, _, thr, _ = lax.while_loop(
            refine_cond, refine, (jnp.int32(0), hi, c_hi, row_max, jnp.zeros((1, QB), F32)))

        def body(c, sel):
            keep = sc_ref[rows(c), :] >= thr
            bias_ref[rows(c), :] = jnp.where(keep, 0.0, NEG_INF)
            return sel + _fold8(jnp.where(keep, 1.0, 0.0), jnp.add)

        n_sel = jnp.sum(lax.fori_loop(0, nk, body, jnp.zeros((8, QB), F32)), axis=0, keepdims=True)

        @pl.when(jnp.max(n_sel) > kf + 0.5)
        def _():
            need = kf - count(lambda x: x > thr)
            r_i = lax.broadcasted_iota(jnp.int32, (CK, CK), 0)
            c_i = lax.broadcasted_iota(jnp.int32, (CK, CK), 1)
            lower = jnp.where(c_i <= r_i, 1.0, 0.0).astype(BF16)

            def body(c, seen):
                x = sc_ref[rows(c), :]
                eq = x == thr
                eq_f = jnp.where(eq, 1.0, 0.0)
                rank = _dot(lower, eq_f.astype(BF16)) + seen
                keep = jnp.logical_or(x > thr, jnp.logical_and(eq, rank <= need))
                bias_ref[rows(c), :] = jnp.where(keep, 0.0, NEG_INF)
                return seen + jnp.sum(eq_f, axis=0, keepdims=True)

            lax.fori_loop(0, nk, body, jnp.zeros((1, QB), F32))

    acc_ref[...] = jnp.zeros_like(acc_ref)
    q_wide = heads_on_lanes(qt_ref, HEAD_DIM)

    def attend_chunk(c, m_old):
        kc = kv_ref[0, rows(c), :][:, :HEAD_DIM]
        bias = bias_ref[rows(c), :]
        logits = _dot(kc, q_wide) + jnp.concatenate([bias] * ATTN_HEADS, axis=1)
        m_new = jnp.maximum(m_old, _col_reduce(logits, jnp.maximum))
        alpha = jnp.exp(m_old - m_new)
        p = jnp.exp(logits - m_new)
        acc_ref[...] = alpha * acc_ref[...] + _dot(kvt_ref[0, c, HEAD_DIM - DSA_SUM_ROWS:, :], p.astype(BF16))
        return m_new

    lax.fori_loop(0, nk, attend_chunk, jnp.full((1, ATTN_HEADS * QB), -1e30, F32))
    out_t = acc_ref[DSA_SUM_ROWS:, :] / acc_ref[0:1, :]
    o_ref[0] = jnp.concatenate([out_t[:, head_lanes(hd)] for hd in range(ATTN_HEADS)], axis=0).T


def _dsa_attention_t(qt, iqt, iwt, kv, kvt, ik):
    B, _, S = qt.shape
    col = lambda b, j: (b, 0, j)
    per_b = lambda b, j: (b, 0, 0)
    return pl.pallas_call(
        _dsa_t_kernel,
        out_shape=jax.ShapeDtypeStruct((B, S, ATTN_WIDTH), F32),
        grid=(B, S // Q_BLOCK),
        in_specs=[
            pl.BlockSpec((1, ATTN_WIDTH, Q_BLOCK), col),
            pl.BlockSpec((1, IDX_HEADS * IDX_DIM, Q_BLOCK), col),
            pl.BlockSpec((1, LANES, Q_BLOCK), col),
            pl.BlockSpec((1, S, LANES), per_b),
            pl.BlockSpec((1, S // DSA_KEY_CHUNK, LANES, DSA_KEY_CHUNK), lambda b, j: (b, 0, 0, 0)),
            pl.BlockSpec((1, S, LANES), per_b),
        ],
        out_specs=pl.BlockSpec((1, Q_BLOCK, ATTN_WIDTH), lambda b, j: (b, j, 0)),
        scratch_shapes=[
            pltpu.VMEM((S, Q_BLOCK), F32),
            pltpu.VMEM((S, Q_BLOCK), F32),
            pltpu.VMEM((DSA_SUM_ROWS + HEAD_DIM, ATTN_HEADS * Q_BLOCK), F32),
        ],
        compiler_params=_cparams("parallel", "parallel"),
        name="dsa_attention",
    )(qt, iqt, iwt, kv, kvt, ik)


def _hyb_out_kernel(ya_ref, bcu_ref, halo_ref, cw_ref, w_ref, x_ref, g_ref,
                    gain2_ref, sc2_ref, sh2_ref, wr_ref, br_ref, o_ref, h_ref, r_ref, rt_ref, cnt_ref, run_ref):
    j = pl.program_id(1)
    tm = ya_ref.shape[1]
    bcu = bcu_ref[0]
    bg = bcu[:, 0:512]
    z = bcu[:, 512:1024] * bcu[:, 1024:1536]
    halo = halo_ref[0]
    zh = halo[:, 512:1024] * halo[:, 1024:1536]
    zh = jnp.where(j > 0, zh, 0.0)
    row = lax.broadcasted_iota(jnp.int32, (tm, 1), 0)
    z1 = jnp.where(row >= 1, pltpu.roll(z, 1, 0), zh[7:8, :])
    z2 = jnp.where(row >= 2, pltpu.roll(z, 2, 0), jnp.where(row == 1, zh[7:8, :], zh[6:7, :]))
    cw = cw_ref[...]
    y_conv = bg * (z2 * cw[0:1, :] + z1 * cw[1:2, :] + z * cw[2:3, :])
    y = _dot(ya_ref[0].astype(BF16), w_ref[0:512, :]) + _dot(y_conv.astype(BF16), w_ref[512:1024, :])
    x_new = x_ref[0] + g_ref[0] * y
    o_ref[0] = x_new
    _route_tile(x_new, gain2_ref, sc2_ref, sh2_ref, wr_ref, br_ref, h_ref, r_ref, rt_ref, cnt_ref, run_ref)


def _hyb_out(y_attn, bcu, conv_w, w_out_bf, x, g1, route_args):
    B, S, D = x.shape
    tm = OUT_TILE
    row = lambda b, j: (b, j, 0)
    per_b = lambda b, j: (b, 0, 0)
    const2 = lambda b, j: (0, 0)
    halo = lambda b, j: (b, jnp.maximum(j * (tm // 8) - 1, 0), 0)
    r_in, r_shape, r_out, r_scratch = _route_specs(B, S, D, tm)
    return pl.pallas_call(
        _hyb_out_kernel,
        out_shape=(jax.ShapeDtypeStruct((B, S, D), F32),) + r_shape,
        grid=(B, S // tm),
        in_specs=[
            pl.BlockSpec((1, tm, 512), row),
            pl.BlockSpec((1, tm, 1536), row),
            pl.BlockSpec((1, 8, 1536), halo),
            pl.BlockSpec((CONV_K, CONV_WIDTH), const2),
            pl.BlockSpec((D, D), const2),
            pl.BlockSpec((1, tm, D), row),
            pl.BlockSpec((1, 1, D), per_b),
        ] + r_in,
        out_specs=(pl.BlockSpec((1, tm, D), row),) + r_out,
        scratch_shapes=r_scratch,
        compiler_params=_cparams("arbitrary", "arbitrary"),
        name="hybrid_out_proj",
    )(y_attn, bcu, bcu, conv_w, w_out_bf, x, g1, *route_args)


ML_COLS = 512 + 512 + 1024 + 1024 + LANES


def _ml_in_kernel(x_ref, g2_ref, y0_ref, y1_ref, r_ref, gain_ref, sc_ref, sh_ref, w_ref, bias_ref,
                  q_ref, k_ref, v_ref, og_ref, gt_ref, gtt_ref, xo_ref):
    x = _residual_tile(x_ref, (g2_ref, y0_ref, y1_ref, r_ref))
    xo_ref[0] = x
    h = _modulated_norm(x, gain_ref[...], sc_ref[0], sh_ref[0])
    hb = h.astype(BF16)
    gates = _dot(hb, w_ref[:, 0:LANES]) + bias_ref[...]
    gates_t = gates.T[:2 * ML_HEADS, :]
    keys_t = _dot(hb, w_ref[:, LANES:LANES + 512]).T.astype(BF16)
    for i in range(gt_ref.shape[1]):
        span = slice(i * ML_CHUNK, (i + 1) * ML_CHUNK)
        gt_ref[0, i] = gates[span, :]
        gtt_ref[0, i] = gates_t[:, span]
        k_ref[0, i] = keys_t[:, span]
    q_ref[0] = (_dot(hb, w_ref[:, LANES + 512:LANES + 1024]) * (ML_QK_DIM ** -0.5)).astype(BF16)
    v_ref[0] = _dot(hb, w_ref[:, LANES + 1024:LANES + 2048]).astype(BF16)
    og_ref[0] = _dot(hb, w_ref[:, LANES + 2048:])


def _ml_in(x, pending, gain, sc, sh, w_pad, gate_bias):
    B, S, D = x.shape
    tm = TOKEN_TILE
    L = ML_CHUNK
    assert tm % L == 0
    per_tile = tm // L
    chunked = lambda b, j: (b, j, 0, 0)
    row = lambda b, j: (b, j, 0)
    per_b = lambda b, j: (b, 0, 0)
    const2 = lambda b, j: (0, 0)
    return pl.pallas_call(
        _ml_in_kernel,
        out_shape=(
            jax.ShapeDtypeStruct((B, S, 512), BF16),
            jax.ShapeDtypeStruct((B, S // L, 512, L), BF16),
            jax.ShapeDtypeStruct((B, S, 1024), BF16),
            jax.ShapeDtypeStruct((B, S, 1024), F32),
            jax.ShapeDtypeStruct((B, S // L, L, LANES), F32),
            jax.ShapeDtypeStruct((B, S // L, 2 * ML_HEADS, L), F32),
            jax.ShapeDtypeStruct((B, S, D), F32),
        ),
        grid=(B, S // tm),
        in_specs=[pl.BlockSpec((1, tm, D), row)] + _pending_specs(D, tm) + [
            pl.BlockSpec((1, D), const2),
            pl.BlockSpec((1, 1, D), per_b),
            pl.BlockSpec((1, 1, D), per_b),
            pl.BlockSpec((D, ML_COLS), const2),
            pl.BlockSpec((1, LANES), const2),
        ],
        out_specs=(
            pl.BlockSpec((1, tm, 512), row),
            pl.BlockSpec((1, per_tile, 512, L), chunked),
            pl.BlockSpec((1, tm, 1024), row),
            pl.BlockSpec((1, tm, 1024), row),
            pl.BlockSpec((1, per_tile, L, LANES), chunked),
            pl.BlockSpec((1, per_tile, 2 * ML_HEADS, L), chunked),
            pl.BlockSpec((1, tm, D), row),
        ),
        compiler_params=_cparams("parallel", "parallel"),
        name="mlstm_in_proj",
    )(x, *pending, gain, sc, sh, w_pad, gate_bias)


def _log_sigmoid(f):
    return jnp.minimum(f, 0.0) - jnp.log1p(jnp.exp(-jnp.abs(f)))


def _split3(x):
    a = x.astype(BF16)
    r = x - a.astype(F32)
    b = r.astype(BF16)
    c = (r - b.astype(F32)).astype(BF16)
    return a, b, c


def _twice(a):
    return jnp.concatenate([a, a], axis=1)


def _over_lanes(a, width):
    return jnp.concatenate([a] * (width // LANES), axis=1)


def _mlstm_kernel(q_ref, kt_ref, v_ref, grow_ref, gcol_ref, gain_ref, o_ref, c_ref, m_ref):
    L = ML_CHUNK
    HP = ML_STEP_HEADS
    S = q_ref.shape[1]
    c_ref[...] = jnp.zeros_like(c_ref)
    m_ref[...] = jnp.zeros_like(m_ref)

    def chunk(c, carry):
        r0 = pl.multiple_of(c * L, L)
        r_i = lax.broadcasted_iota(jnp.int32, (L, L), 0)
        c_i = lax.broadcasted_iota(jnp.int32, (L, L), 1)
        tril = c_i <= r_i
        lower = jnp.where(tril, 1.0, 0.0).astype(BF16)
        upper = jnp.where(r_i <= c_i, 1.0, 0.0).astype(BF16)
        e_r = lax.broadcasted_iota(jnp.int32, (LANES, HP * LANES), 0)
        e_c = lax.broadcasted_iota(jnp.int32, (LANES, HP * LANES), 1)
        pick = jnp.where(e_r == HP + e_c // LANES, 1.0, 0.0).astype(BF16)
        rows = grow_ref[0, 0, c]
        cols = gcol_ref[0, 0, c]
        b_rows = sum(_dot(p, upper) for p in _split3(_log_sigmoid(rows)))
        b_cols = sum(_dot(lower, p) for p in _split3(_log_sigmoid(cols)))
        b_colr = sum(_dot(p, pick) for p in _split3(b_cols))
        lane = lax.broadcasted_iota(jnp.int32, (1, L), 1)
        b_last_all = jnp.sum(jnp.where(lane == L - 1, b_rows, 0.0), axis=-1, keepdims=True)
        ones_v = jnp.ones((L, ML_V_DIM), BF16)
        for hh in range(HP):
            q = q_ref[0, pl.ds(r0, L), hh * ML_QK_DIM:(hh + 1) * ML_QK_DIM]
            kt = kt_ref[0, c, hh * ML_QK_DIM:(hh + 1) * ML_QK_DIM, :]
            v = v_ref[0, pl.ds(r0, L), hh * ML_V_DIM:(hh + 1) * ML_V_DIM]
            vx = jnp.concatenate([v, ones_v], axis=1)
            i_row = rows[hh:hh + 1, :]
            b_row = b_rows[HP + hh:HP + hh + 1, :]
            b_last = b_last_all[HP + hh:HP + hh + 1, :]
            b_col = b_colr[:, hh * LANES:(hh + 1) * LANES]
            m_prev = m_ref[hh]
            ctn = c_ref[hh]

            dmat = jnp.where(tril, _over_lanes(b_col, L) - b_row + i_row, NEG_INF)
            inter = b_col + m_prev
            m_t = jnp.maximum(inter, jnp.max(dmat, axis=-1, keepdims=True))
            w_intra = jnp.exp(dmat - _over_lanes(m_t, L))
            w_inter = jnp.exp(inter - m_t)
            intra = (w_intra * _dot(q, kt)).astype(BF16)
            tot = _twice(w_inter) * _dot(q, ctn.astype(BF16)) + _dot(intra, vx)
            num = tot[:, :ML_V_DIM]
            den = tot[:, ML_V_DIM:]
            hc = num / jnp.maximum(jnp.abs(den), jnp.exp(-m_t))
            y = hc * lax.rsqrt(jnp.mean(hc * hc, axis=-1, keepdims=True) + NORM_EPS)
            o_ref[0, pl.ds(r0, L), hh * ML_V_DIM:(hh + 1) * ML_V_DIM] = (
                y * gain_ref[:, hh * ML_V_DIM:(hh + 1) * ML_V_DIM])

            g_row = b_last - b_row + i_row
            m_new = jnp.maximum(b_last + m_prev, jnp.max(g_row, axis=-1, keepdims=True))
            decay = jnp.exp(b_last + m_prev - m_new)
            kw = (kt.astype(F32) * jnp.exp(g_row - _over_lanes(m_new, L))).astype(BF16)
            c_ref[hh] = _twice(decay) * ctn + _dot(kw, vx)
            m_ref[hh] = m_new
        return carry

    lax.fori_loop(0, S // L, chunk, 0)


def _mlstm(q, kt, v, g_rows, g_cols, out_gain):
    B, S, _ = q.shape
    assert ML_CHUNK % LANES == 0
    nc = S // ML_CHUNK
    hp = ML_STEP_HEADS
    return pl.pallas_call(
        _mlstm_kernel,
        out_shape=jax.ShapeDtypeStruct((B, S, ML_HEADS * ML_V_DIM), F32),
        grid=(B, ML_HEADS // hp),
        in_specs=[
            pl.BlockSpec((1, S, hp * ML_QK_DIM), lambda b, p: (b, 0, p)),
            pl.BlockSpec((1, nc, hp * ML_QK_DIM, ML_CHUNK), lambda b, p: (b, 0, p, 0)),
            pl.BlockSpec((1, S, hp * ML_V_DIM), lambda b, p: (b, 0, p)),
            pl.BlockSpec((1, 1, nc, 2 * hp, ML_CHUNK), lambda b, p: (b, p, 0, 0, 0)),
            pl.BlockSpec((1, 1, nc, ML_CHUNK, LANES), lambda b, p: (b, p, 0, 0, 0)),
            pl.BlockSpec((1, hp * ML_V_DIM), lambda b, p: (0, p)),
        ],
        out_specs=pl.BlockSpec((1, S, hp * ML_V_DIM), lambda b, p: (b, 0, p)),
        scratch_shapes=[
            pltpu.VMEM((hp, ML_QK_DIM, ML_V_DIM + LANES), F32),
            pltpu.VMEM((hp, 1, LANES), F32),
        ],
        compiler_params=_cparams("parallel", "parallel"),
        name="mlstm_chunkwise",
    )(q, kt, v, g_rows, g_cols, out_gain)


def _ml_out_kernel(hh_ref, og_ref, w_ref, x_ref, g_ref,
                   gain2_ref, sc2_ref, sh2_ref, wr_ref, br_ref, o_ref, h_ref, r_ref, rt_ref, cnt_ref, run_ref):
    a = jax.nn.sigmoid(og_ref[0]) * hh_ref[0]
    x_new = x_ref[0] + g_ref[0] * _dot(a.astype(BF16), w_ref[...])
    o_ref[0] = x_new
    _route_tile(x_new, gain2_ref, sc2_ref, sh2_ref, wr_ref, br_ref, h_ref, r_ref, rt_ref, cnt_ref, run_ref)


def _ml_out(hh, og, w_out_bf, x, g1, route_args):
    B, S, D = x.shape
    tm = OUT_TILE
    row = lambda b, j: (b, j, 0)
    per_b = lambda b, j: (b, 0, 0)
    const2 = lambda b, j: (0, 0)
    r_in, r_shape, r_out, r_scratch = _route_specs(B, S, D, tm)
    return pl.pallas_call(
        _ml_out_kernel,
        out_shape=(jax.ShapeDtypeStruct((B, S, D), F32),) + r_shape,
        grid=(B, S // tm),
        in_specs=[
            pl.BlockSpec((1, tm, D), row),
            pl.BlockSpec((1, tm, D), row),
            pl.BlockSpec((D, D), const2),
            pl.BlockSpec((1, tm, D), row),
            pl.BlockSpec((1, 1, D), per_b),
        ] + r_in,
        out_specs=(pl.BlockSpec((1, tm, D), row),) + r_out,
        scratch_shapes=r_scratch,
        compiler_params=_cparams("arbitrary", "arbitrary"),
        name="mlstm_out_proj",
    )(hh, og, w_out_bf, x, g1, *route_args)


def _first_argmax(x, lane, width):
    mx = jnp.max(x, axis=-1, keepdims=True)
    idx = jnp.min(jnp.where(x == mx, lane, width), axis=-1, keepdims=True)
    return mx, idx


def _route_tile(x, gain_ref, sc_ref, sh_ref, w_ref, b_ref, h_ref, r_ref, rt_ref, cnt_ref, run_ref):
    tm = x.shape[0]

    @pl.when(jnp.logical_and(pl.program_id(0) == 0, pl.program_id(1) == 0))
    def _():
        run_ref[...] = jnp.zeros_like(run_ref)

    h = _modulated_norm(x, gain_ref[...], sc_ref[0], sh_ref[0])
    h_ref[0] = _pack_bf16_pairs(h)
    logits = _dot(h.astype(BF16), w_ref[...]) + b_ref[...]
    lane = lax.broadcasted_iota(jnp.int32, (1, LANES), 1)

    def pick(lgt):
        lg = jnp.where(lane < N_GROUPS, lgt, NEG_INF)
        g_max, g_sel = _first_argmax(lg, lane, LANES)
        pg = 1.0 / jnp.sum(jnp.exp(lg - g_max), axis=-1, keepdims=True)
        e_lane = lane - N_GROUPS
        in_grp = jnp.logical_and(e_lane >= g_sel * EXPERTS_PER_GROUP, e_lane < (g_sel + 1) * EXPERTS_PER_GROUP)
        le = jnp.where(in_grp, lgt, NEG_INF)
        v1, i1 = _first_argmax(le, lane, LANES)
        v2, i2 = _first_argmax(jnp.where(lane == i1, NEG_INF, le), lane, LANES)
        ratio = jnp.exp(v2 - v1)
        return i1 - N_GROUPS, i2 - N_GROUPS, pg / (1.0 + ratio), pg * ratio / (1.0 + ratio)

    e1, e2, w1, w2 = pick(logits)
    hot1 = lane == e1
    hot2 = lane == e2
    onehot = jnp.where(jnp.logical_or(hot1, hot2), 1.0, 0.0)
    r_i = lax.broadcasted_iota(jnp.int32, (tm, tm), 0)
    c_i = lax.broadcasted_iota(jnp.int32, (tm, tm), 1)
    before = jnp.where(c_i < r_i, 1.0, 0.0).astype(BF16)
    seen = _dot(before, onehot.astype(BF16)) + run_ref[...]
    rank1 = jnp.sum(jnp.where(hot1, seen, 0.0), axis=-1, keepdims=True)
    rank2 = jnp.sum(jnp.where(hot2, seen, 0.0), axis=-1, keepdims=True)
    run_ref[...] = run_ref[...] + jnp.sum(onehot, axis=0, keepdims=True)
    cnt_ref[...] = run_ref[...]

    out = jnp.where(lane == 0, e1.astype(F32), 0.0)
    out = jnp.where(lane == 1, e2.astype(F32), out)
    out = jnp.where(lane == 2, w1, out)
    out = jnp.where(lane == 3, w2, out)
    out = jnp.where(lane == 4, rank1, out)
    out = jnp.where(lane == 5, rank2, out)
    r_ref[0] = out
    rt_ref[0] = out.T[:8, :]


def _route_specs(B, S, D, tm):
    row = lambda b, j: (b, j, 0)
    per_b = lambda b, j: (b, 0, 0)
    const2 = lambda b, j: (0, 0)
    in_specs = [
        pl.BlockSpec((1, D), const2),
        pl.BlockSpec((1, 1, D), per_b),
        pl.BlockSpec((1, 1, D), per_b),
        pl.BlockSpec((D, LANES), const2),
        pl.BlockSpec((1, LANES), const2),
    ]
    out_shape = (
        jax.ShapeDtypeStruct((B, S, D // 2), jnp.int32),
        jax.ShapeDtypeStruct((B, S, LANES), F32),
        jax.ShapeDtypeStruct((B, 8, S), F32),
        jax.ShapeDtypeStruct((1, LANES), F32),
    )
    out_specs = (
        pl.BlockSpec((1, tm, D // 2), row),
        pl.BlockSpec((1, tm, LANES), row),
        pl.BlockSpec((1, 8, tm), lambda b, j: (b, 0, j)),
        pl.BlockSpec((1, LANES), const2),
    )
    return in_specs, out_shape, out_specs, [pltpu.VMEM((1, LANES), F32)]


def _experts_kernel(blk_e_ref, n_used_ref, next_e_ref, x_ref, wg_hbm, wu_hbm, wd_hbm, o_ref,
                    wg_f, wu_f, wd_f, wg_s, wu_s, wd_s, sem, *, layer):
    i = pl.program_id(0)
    used = i < n_used_ref[0]
    e = blk_e_ref[i]
    new_expert = jnp.logical_or(i == 0, e != blk_e_ref[jnp.maximum(i - 1, 0)])

    def fetch(expert):
        return (pltpu.make_async_copy(wg_hbm.at[layer, expert], wg_f, sem.at[0]),
                pltpu.make_async_copy(wu_hbm.at[layer, expert], wu_f, sem.at[1]),
                pltpu.make_async_copy(wd_hbm.at[layer, expert], wd_f, sem.at[2]))

    @pl.when(i == 0)
    def _():
        for cp in fetch(e):
            cp.start()

    @pl.when(jnp.logical_and(used, new_expert))
    def _():
        for cp in fetch(e):
            cp.wait()
        wg_s[...] = wg_f[...].astype(BF16)
        wu_s[...] = wu_f[...].astype(BF16)
        wd_s[...] = wd_f[...].astype(BF16)

        @pl.when(next_e_ref[i] >= 0)
        def _():
            for cp in fetch(next_e_ref[i]):
                cp.start()

    @pl.when(used)
    def _():
        x = _unpack_bf16_pairs(x_ref[...]).astype(BF16)
        a = _dot(x, wg_s[...])
        u = _dot(x, wu_s[...])
        act = a * jax.nn.sigmoid(a) * u
        o_ref[...] = _pack_bf16_pairs(_dot(act.astype(BF16), wd_s[...]))

    @pl.when(i >= n_used_ref[0])
    def _():
        o_ref[...] = jnp.zeros_like(o_ref)


def _experts(layer, blk_e, n_used, next_e, xs, w_gate, w_up, w_down):
    R = xs.shape[0]
    D = 2 * xs.shape[1]
    n_blk = R // MOE_BLOCK
    rows = lambda i, be, nu, ne: (i, 0)
    grid_spec = pltpu.PrefetchScalarGridSpec(
        num_scalar_prefetch=3,
        grid=(n_blk,),
        in_specs=[
            pl.BlockSpec((MOE_BLOCK, D // 2), rows),
            pl.BlockSpec(memory_space=pl.ANY),
            pl.BlockSpec(memory_space=pl.ANY),
            pl.BlockSpec(memory_space=pl.ANY),
        ],
        out_specs=pl.BlockSpec((MOE_BLOCK, D // 2), rows),
        scratch_shapes=[
            pltpu.VMEM((D, D_EXPERT), F32),
            pltpu.VMEM((D, D_EXPERT), F32),
            pltpu.VMEM((D_EXPERT, D), F32),
            pltpu.VMEM((D, D_EXPERT), BF16),
            pltpu.VMEM((D, D_EXPERT), BF16),
            pltpu.VMEM((D_EXPERT, D), BF16),
            pltpu.SemaphoreType.DMA((3,)),
        ],
    )
    return pl.pallas_call(
        functools.partial(_experts_kernel, layer=layer),
        out_shape=jax.ShapeDtypeStruct((R, D // 2), jnp.int32),
        grid_spec=grid_spec,
        compiler_params=_cparams("arbitrary"),
        name="moe_experts",
    )(blk_e, n_used, next_e, xs, w_gate, w_up, w_down)


def _residual_tile(x_ref, pending_refs):
    if not pending_refs:
        return x_ref[0]
    g_ref, y0_ref, y1_ref, r_ref = pending_refs
    r = r_ref[0]
    y = _unpack_bf16_pairs(y0_ref[0, 0]) * r[:, 2:3] + _unpack_bf16_pairs(y1_ref[0, 0]) * r[:, 3:4]
    return x_ref[0] + g_ref[0] * y


def _pending_specs(D, tm):
    return [
        pl.BlockSpec((1, 1, D), lambda b, j: (b, 0, 0)),
        pl.BlockSpec((1, 1, tm, D // 2), lambda b, j: (0, b, j, 0)),
        pl.BlockSpec((1, 1, tm, D // 2), lambda b, j: (1, b, j, 0)),
        pl.BlockSpec((1, tm, LANES), lambda b, j: (b, j, 0)),
    ]


def _combine_kernel(x_ref, g_ref, y0_ref, y1_ref, r_ref, o_ref):
    o_ref[0] = _residual_tile(x_ref, (g_ref, y0_ref, y1_ref, r_ref))


def _combine(x, pending):
    B, S, D = x.shape
    tm = OUT_TILE
    row = lambda b, j: (b, j, 0)
    return pl.pallas_call(
        _combine_kernel,
        out_shape=jax.ShapeDtypeStruct((B, S, D), F32),
        grid=(B, S // tm),
        in_specs=[pl.BlockSpec((1, tm, D), row)] + _pending_specs(D, tm),
        out_specs=pl.BlockSpec((1, tm, D), row),
        compiler_params=_cparams("parallel", "parallel"),
        name="moe_combine",
    )(x, *pending)


SC_CORES = 2
SC_SUBCORES = 16
SC_WORKERS = SC_CORES * SC_SUBCORES
SC_CHUNK = 64


def _sc_mesh():
    return plsc.VectorSubcoreMesh(core_axis_name="c", subcore_axis_name="s",
                                  num_cores=SC_CORES, num_subcores=SC_SUBCORES)


def _sc_scatter_rows(src, idx, n_out):
    T, W = src.shape
    per_w = T // SC_WORKERS
    nch = per_w // SC_CHUNK
    idx4 = idx.reshape(TOP_K, SC_WORKERS, nch, SC_CHUNK)

    @functools.partial(
        pl.kernel, mesh=_sc_mesh(),
        out_type=jax.ShapeDtypeStruct((n_out, W), src.dtype),
        scratch_types=[
            pltpu.VMEM((TOP_K, nch, SC_CHUNK), jnp.int32),
            pltpu.VMEM((2, SC_CHUNK, W), src.dtype),
            pltpu.SemaphoreType.DMA((2,)),
            pltpu.SemaphoreType.DMA((2 * TOP_K,)),
        ],
        name="sc_scatter_rows",
    )
    def body(src_hbm, idx_hbm, out_hbm, idx_v, rows_v, load_sem, scatter_sem):
        wid = lax.axis_index("s") * SC_CORES + lax.axis_index("c")
        for s in range(TOP_K):
            pltpu.sync_copy(idx_hbm.at[s, wid], idx_v.at[s])

        def load(j, b):
            return pltpu.make_async_copy(
                src_hbm.at[pl.ds(wid * per_w + j * SC_CHUNK, SC_CHUNK)], rows_v.at[b], load_sem.at[b])

        def scatter(s, j, b):
            return pltpu.make_async_copy(rows_v.at[b], out_hbm.at[idx_v.at[s, j]], scatter_sem.at[2 * s + b])

        load(0, 0).start()

        @pl.loop(0, nch, step=2)
        def _(i):
            for b in range(2):
                j = i + b

                @pl.when(j >= 1)
                def _():
                    for s in range(TOP_K):
                        scatter(s, j - 1, 1 - b).wait()

                @pl.when(j + 1 < nch)
                def _():
                    load(j + 1, 1 - b).start()

                load(j, b).wait()
                for s in range(TOP_K):
                    scatter(s, j, b).start()

        for s in range(TOP_K):
            scatter(s, nch - 1, (nch - 1) % 2).wait()

    assert nch % 2 == 0
    return body(src, idx4)


def _sc_gather_rows(table, idx):
    N = idx.shape[0]
    W = table.shape[1]
    per_w = N // SC_WORKERS
    nch = per_w // SC_CHUNK
    idx3 = idx.reshape(SC_WORKERS, nch, SC_CHUNK)

    @functools.partial(
        pl.kernel, mesh=_sc_mesh(),
        out_type=jax.ShapeDtypeStruct((N, W), table.dtype),
        scratch_types=[
            pltpu.VMEM((nch, SC_CHUNK), jnp.int32),
            pltpu.VMEM((2, SC_CHUNK, W), table.dtype),
            pltpu.SemaphoreType.DMA((2,)),
            pltpu.SemaphoreType.DMA((2,)),
        ],
        name="sc_gather_rows",
    )
    def body(table_hbm, idx_hbm, out_hbm, idx_v, rows_v, gather_sem, write_sem):
        wid = lax.axis_index("s") * SC_CORES + lax.axis_index("c")
        pltpu.sync_copy(idx_hbm.at[wid], idx_v)

        def gather(j, b):
            return pltpu.make_async_copy(table_hbm.at[idx_v.at[j]], rows_v.at[b], gather_sem.at[b])

        def write(j, b):
            return pltpu.make_async_copy(
                rows_v.at[b], out_hbm.at[pl.ds(wid * per_w + j * SC_CHUNK, SC_CHUNK)], write_sem.at[b])

        gather(0, 0).start()

        @pl.loop(0, nch, step=2)
        def _(i):
            for b in range(2):
                j = i + b

                @pl.when(j >= 1)
                def _():
                    write(j - 1, 1 - b).wait()

                @pl.when(j + 1 < nch)
                def _():
                    gather(j + 1, 1 - b).start()

                gather(j, b).wait()
                write(j, b).start()

        write(nch - 1, (nch - 1) % 2).wait()

    assert nch % 2 == 0
    return body(table, idx3)


def _moe_dispatch(route_t, counts, T):
    A = T * TOP_K
    counts = counts[0, :N_EXPERTS].astype(jnp.int32)
    blocks_per = (counts + MOE_BLOCK - 1) // MOE_BLOCK
    block_end = jnp.cumsum(blocks_per)
    block_start = block_end - blocks_per
    expert = jnp.swapaxes(route_t[:, :TOP_K, :], 0, 1).reshape(TOP_K, T).astype(jnp.int32)
    rank = jnp.swapaxes(route_t[:, 4:4 + TOP_K, :], 0, 1).reshape(TOP_K, T).astype(jnp.int32)
    onehot = expert[None] == jnp.arange(N_EXPERTS, dtype=jnp.int32)[:, None, None]
    start = jnp.sum(jnp.where(onehot, block_start[:, None, None], 0), axis=0)
    dest = start * MOE_BLOCK + rank
    n_blk = -(-A // MOE_BLOCK) + N_EXPERTS
    blk = jnp.arange(n_blk, dtype=jnp.int32)
    blk_e = jnp.minimum(jnp.sum(blk[:, None] >= block_end[None, :], axis=-1), N_EXPERTS - 1).astype(jnp.int32)
    n_used = block_end[-1]
    first = jnp.logical_and(blk < n_used, jnp.logical_or(blk == 0, blk_e != jnp.roll(blk_e, 1)))
    first_pos = jnp.where(first, blk, n_blk)
    next_pos = jnp.concatenate([lax.cummin(first_pos, axis=0, reverse=True)[1:], jnp.full((1,), n_blk, jnp.int32)])
    next_e = jnp.where(next_pos < n_blk, blk_e[jnp.minimum(next_pos, n_blk - 1)], -1).astype(jnp.int32)
    return dest, n_blk * MOE_BLOCK, blk_e, n_used.reshape(1).astype(jnp.int32), next_e


def _rope_tables(S):
    inv = 1.0 / (ROPE_THETA ** (jnp.arange(0, HEAD_DIM, 2, dtype=F32) / HEAD_DIM))
    ang = jnp.arange(S, dtype=F32)[:, None] * inv[None, :]
    cos, sin = jnp.cos(ang), jnp.sin(ang)
    cos_h = jnp.concatenate([cos, cos], axis=-1)
    sin_h = jnp.concatenate([-sin, sin], axis=-1)
    return jnp.tile(cos_h, (1, ATTN_HEADS)), jnp.tile(sin_h, (1, ATTN_HEADS))


def _pad_cols(w, width):
    return jnp.pad(w, ((0, 0), (0, width - w.shape[1])))


def kernel(x, c, ada_w, ada_b, norm_mix, norm_ffn, hy_w_in, hy_q_norm, hy_k_norm, hy_conv_w, hy_w_out, ml_w_in, ml_b_gates, ml_out_norm, ml_w_out, moe_w_group, moe_b_group, moe_w_expert, moe_b_expert, moe_w_gate, moe_w_up, moe_w_down):
    B, S, D = x.shape
    T = B * S
    cos_t, sin_t = _rope_tables(S)
    mod = _ada_modulation(c, ada_w, ada_b).reshape(DEPTH, B, 6, 1, D)
    r_i = np.arange(ATTN_WIDTH)
    grp = jnp.asarray((r_i[:, None] // HEAD_DIM) == (r_i[None, :] // HEAD_DIM), dtype=BF16)

    pending = ()
    for l in range(DEPTH):
        sh1, sc1, g1, sh2, sc2, g2 = [mod[l, :, i] for i in range(6)]
        gain1 = norm_mix[l].reshape(1, D)
        w_r = _pad_cols(jnp.concatenate([moe_w_group[l], moe_w_expert[l]], axis=1), LANES).astype(BF16)
        b_r = jnp.pad(jnp.concatenate([moe_b_group[l], moe_b_expert[l]]), (0, LANES - N_GROUPS - N_EXPERTS))
        route_args = (norm_ffn[l].reshape(1, D), sc2, sh2, w_r, b_r.reshape(1, LANES))
        j = l // 2
        if l % 2 == 0:
            w = hy_w_in[j]
            o = np.cumsum((0,) + (ATTN_WIDTH, HEAD_DIM, HEAD_DIM, IDX_HEADS * IDX_DIM, IDX_DIM, IDX_HEADS,
                                  CONV_WIDTH, CONV_WIDTH, CONV_WIDTH))
            wq, wk, wv, wiq, wik, wiw, wbg, wcg, wu = [w[:, o[i]:o[i + 1]] for i in range(9)]
            w_pad = jnp.concatenate(
                [wk, wv, _pad_cols(jnp.concatenate([wik, wiw], axis=1), LANES), wq, wiq, wbg, wcg, wu],
                axis=1).astype(BF16)
            qn_t = jnp.tile(hy_q_norm[j], ATTN_HEADS).reshape(1, ATTN_WIDTH)
            kn_t = jnp.tile(hy_k_norm[j], LANES // HEAD_DIM).reshape(1, LANES)
            outs = _hyb_in(x, pending, gain1, sc1, sh1, w_pad, cos_t, sin_t, qn_t, kn_t, grp)
            qt, iqt, bcu, kv, kvt, ik, iwt = outs[:7]
            if pending:
                x = outs[7]
            y_attn = _dsa_attention_t(qt, iqt, iwt, kv, kvt, ik)
            x, h2, route, route_t, counts =_hyb_out(y_attn, bcu, hy_conv_w[j], hy_w_out[j].astype(BF16), x, g1, route_args)
        else:
            w = ml_w_in[j]
            hq = ML_HEADS * ML_QK_DIM
            hv = ML_HEADS * ML_V_DIM
            wq, wk, wv = w[:, :hq], w[:, hq:2 * hq], w[:, 2 * hq:2 * hq + hv]
            wg = w[:, 2 * hq + hv:2 * hq + hv + 2 * ML_HEADS]
            wo = w[:, 2 * hq + hv + 2 * ML_HEADS:]
            w_pad = jnp.concatenate([_pad_cols(wg, LANES), wk, wq, wv, wo], axis=1).astype(BF16)
            gate_bias = jnp.pad(ml_b_gates[j], (0, LANES - 2 * ML_HEADS)).reshape(1, LANES)
            q, k, v, og, g_cols, g_rows, x = _ml_in(x, pending, gain1, sc1, sh1, w_pad, gate_bias)
            assert ML_STEP_HEADS == ML_HEADS
            hh = _mlstm(q, k, v, g_rows[:, None], g_cols[:, None], ml_out_norm[j].reshape(1, hv))
            x, h2, route, route_t, counts =_ml_out(hh, og, ml_w_out[j].astype(BF16), x, g1, route_args)

        dest, n_rows, blk_e, n_used, next_e = _moe_dispatch(route_t, counts, T)
        xs = _sc_scatter_rows(h2.reshape(T, D // 2), dest, n_rows)
        ys = _experts(l, blk_e, n_used, next_e, xs, moe_w_gate, moe_w_up, moe_w_down)
        y01 = _sc_gather_rows(ys, dest.reshape(TOP_K * T)).reshape(TOP_K, B, S, D // 2)
        pending = (g2, y01, y01, route)
    return _combine(x, pending)
```

```python
import functools

import numpy as np
import jax
import jax.numpy as jnp
from jax import lax
from jax.experimental import pallas as pl
from jax.experimental.pallas import tpu as pltpu
from jax.experimental.pallas import tpu_sc as plsc

F32 = jnp.float32
BF16 = jnp.bfloat16
HIGHEST = lax.Precision.HIGHEST

D_MODEL = 1024
DEPTH = 4
ATTN_HEADS = 8
HEAD_DIM = 64
ATTN_WIDTH = ATTN_HEADS * HEAD_DIM
IDX_HEADS = 8
IDX_DIM = 64
INDEX_TOPK = 256
Q_BLOCK = 256
ROPE_THETA = 10000.0
CONV_WIDTH = D_MODEL - ATTN_WIDTH
CONV_K = 3
ML_HEADS = 8
ML_QK_DIM = 64
ML_V_DIM = 128
N_GROUPS = 4
EXPERTS_PER_GROUP = 8
N_EXPERTS = N_GROUPS * EXPERTS_PER_GROUP
TOP_K = 2
D_EXPERT = 512
MOE_BLOCK = 512
NORM_EPS = 1e-6

LANES = 128
VMEM_LIMIT = 56 * 1024 * 1024
TOKEN_TILE = 512
OUT_TILE = 512
ML_CHUNK = 256
ML_STEP_HEADS = 8
NEG_INF = float("-inf")


def _cparams(*sem):
    return pltpu.CompilerParams(dimension_semantics=sem, vmem_limit_bytes=VMEM_LIMIT)


def _dot(a, b):
    return jnp.dot(a, b, preferred_element_type=F32)


def _pack_bf16_pairs(x):
    bits = lax.bitcast_convert_type(x.astype(BF16).astype(F32), jnp.uint32)
    half = bits.shape[1] // 2
    packed = (bits[:, :half] >> 16) | (bits[:, half:] & jnp.uint32(0xFFFF0000))
    return lax.bitcast_convert_type(packed, jnp.int32)


def _unpack_bf16_pairs(words):
    words = lax.bitcast_convert_type(words, jnp.uint32)
    return jnp.concatenate(
        [lax.bitcast_convert_type(words << 16, F32),
         lax.bitcast_convert_type(words & jnp.uint32(0xFFFF0000), F32)], axis=1)


def _split_dot(a_f32, b_bf16):
    hi = a_f32.astype(BF16)
    lo = (a_f32 - hi.astype(F32)).astype(BF16)
    return _dot(hi, b_bf16) + _dot(lo, b_bf16)


def _ada_kernel(c_ref, w_ref, b_ref, o_ref):
    c = c_ref[...]
    ca = c * jax.nn.sigmoid(c)
    o_ref[0] = jnp.dot(ca, w_ref[0], precision=HIGHEST, preferred_element_type=F32) + b_ref[0]


def _ada_modulation(c, ada_w, ada_b):
    B, D = c.shape
    n_col = ada_w.shape[-1] // D
    return pl.pallas_call(
        _ada_kernel,
        out_shape=jax.ShapeDtypeStruct((DEPTH, B, n_col * D), F32),
        grid=(DEPTH, n_col),
        in_specs=[
            pl.BlockSpec((B, D), lambda l, j: (0, 0)),
            pl.BlockSpec((1, D, D), lambda l, j: (l, 0, j)),
            pl.BlockSpec((1, 1, D), lambda l, j: (l, 0, j)),
        ],
        out_specs=pl.BlockSpec((1, B, D), lambda l, j: (l, 0, j)),
        compiler_params=_cparams("parallel", "parallel"),
        name="ada_modulation",
    )(c, ada_w, ada_b.reshape(DEPTH, 1, n_col * D))


def _modulated_norm(x, gain, scale, shift):
    y = x * lax.rsqrt(jnp.mean(x * x, axis=-1, keepdims=True) + NORM_EPS)
    return y * gain * (1.0 + scale) + shift


def _rope(x, cos, sin_signed, first_half):
    w = x.shape[-1]
    partner = jnp.where(first_half, pltpu.roll(x, w - HEAD_DIM // 2, 1), pltpu.roll(x, HEAD_DIM // 2, 1))
    return x * cos + partner * sin_signed


HYB_COLS = 5 * 512 + 2 * LANES


def _hyb_in_kernel(*refs, n_pending):
    x_ref, pending_refs, refs = refs[0], refs[1:1 + n_pending], refs[1 + n_pending:]
    (gain_ref, sc_ref, sh_ref, w_ref, cos_ref, sin_ref, qn_ref, kn_ref, grp_ref,
     qt_ref, iqt_ref, bcu_ref, kv_ref, kvt_ref, ik_ref, iwt_ref) = refs[:16]
    x = _residual_tile(x_ref, pending_refs)
    if n_pending:
        refs[16][0] = x
    h = _modulated_norm(x, gain_ref[...], sc_ref[0], sh_ref[0])
    hb = h.astype(BF16)
    p_small = _dot(hb, w_ref[:, 0:2 * LANES])
    p_q = _dot(hb, w_ref[:, 2 * LANES:2 * LANES + 512])
    p_iq = _dot(hb, w_ref[:, 2 * LANES + 512:2 * LANES + 1024])
    cos = cos_ref[...]
    sin = sin_ref[...]
    lane = lax.broadcasted_iota(jnp.int32, (1, ATTN_WIDTH), 1)
    first_half = (lane % HEAD_DIM) < (HEAD_DIM // 2)
    fh128 = first_half[:, :LANES]
    lane128 = lane[:, :LANES]

    kv = p_small[:, :LANES]
    is_k = lane128 < HEAD_DIM
    kk = jnp.where(is_k, kv, 0.0)
    ms_k = jnp.sum(kk * kk, axis=-1, keepdims=True) * (1.0 / HEAD_DIM)
    kn = kv * lax.rsqrt(ms_k + NORM_EPS) * kn_ref[...]
    kr = _rope(kn, cos[:, :LANES], sin[:, :LANES], fh128)
    kv = jnp.where(is_k, kr, kv)
    kv_ref[0] = kv.astype(BF16)
    feat = lax.broadcasted_iota(jnp.int32, (LANES, 1), 0)
    kv_t = jnp.where(feat < HEAD_DIM, 1.0, kv.T).astype(BF16)
    for i in range(kvt_ref.shape[1]):
        kvt_ref[0, i] = kv_t[:, i * DSA_KEY_CHUNK:(i + 1) * DSA_KEY_CHUNK]

    sm = p_small[:, LANES:]
    ikr = _rope(sm, cos[:, :LANES], sin[:, :LANES], fh128)
    ik_ref[0] = jnp.where(is_k, ikr, 0.0).astype(BF16)
    iwt_ref[0] = sm.T

    ms = _split_dot(p_q * p_q, grp_ref[...]) * (1.0 / HEAD_DIM)
    q = p_q * lax.rsqrt(ms + NORM_EPS) * qn_ref[...]
    qt_ref[0] = (_rope(q, cos, sin, first_half) * (HEAD_DIM ** -0.5)).T.astype(BF16)
    iqt_ref[0] = (_rope(p_iq, cos, sin, first_half) * (IDX_DIM ** -0.5)).T.astype(BF16)

    bcu_ref[0] = _dot(hb, w_ref[:, 2 * LANES + 1024:])


def _hyb_in(x, pending, gain, sc, sh, w_pad, cos_t, sin_t, qn_t, kn_t, grp):
    B, S, D = x.shape
    tm = TOKEN_TILE
    row = lambda b, j: (b, j, 0)
    per_b = lambda b, j: (b, 0, 0)
    const2 = lambda b, j: (0, 0)
    tab = lambda b, j: (j, 0)
    col = lambda b, j: (b, 0, j)
    assert tm % DSA_KEY_CHUNK == 0
    x_out_shape = (jax.ShapeDtypeStruct((B, S, D), F32),) if pending else ()
    x_out_spec = (pl.BlockSpec((1, tm, D), row),) if pending else ()
    return pl.pallas_call(
        functools.partial(_hyb_in_kernel, n_pending=len(pending)),
        out_shape=(
            jax.ShapeDtypeStruct((B, 512, S), BF16),
            jax.ShapeDtypeStruct((B, 512, S), BF16),
            jax.ShapeDtypeStruct((B, S, 1536), F32),
            jax.ShapeDtypeStruct((B, S, LANES), BF16),
            jax.ShapeDtypeStruct((B, S // DSA_KEY_CHUNK, LANES, DSA_KEY_CHUNK), BF16),
            jax.ShapeDtypeStruct((B, S, LANES), BF16),
            jax.ShapeDtypeStruct((B, LANES, S), F32),
        ) + x_out_shape,
        grid=(B, S // tm),
        in_specs=[pl.BlockSpec((1, tm, D), row)] + (_pending_specs(D, tm) if pending else []) + [
            pl.BlockSpec((1, D), const2),
            pl.BlockSpec((1, 1, D), per_b),
            pl.BlockSpec((1, 1, D), per_b),
            pl.BlockSpec((D, HYB_COLS), const2),
            pl.BlockSpec((tm, 512), tab),
            pl.BlockSpec((tm, 512), tab),
            pl.BlockSpec((1, 512), const2),
            pl.BlockSpec((1, LANES), const2),
            pl.BlockSpec((512, 512), const2),
        ],
        out_specs=(
            pl.BlockSpec((1, 512, tm), col),
            pl.BlockSpec((1, 512, tm), col),
            pl.BlockSpec((1, tm, 1536), row),
            pl.BlockSpec((1, tm, LANES), row),
            pl.BlockSpec((1, tm // DSA_KEY_CHUNK, LANES, DSA_KEY_CHUNK), lambda b, j: (b, j, 0, 0)),
            pl.BlockSpec((1, tm, LANES), row),
            pl.BlockSpec((1, LANES, tm), col),
        ) + x_out_spec,
        compiler_params=_cparams("parallel", "parallel"),
        name="hybrid_in_proj",
    )(x, *pending, gain, sc, sh, w_pad, cos_t, sin_t, qn_t, kn_t, grp)


DSA_KEY_CHUNK = 256
DSA_SUM_ROWS = 16


def _fold8(x, op):
    parts = x.reshape(x.shape[0] // 8, 8, x.shape[1])
    while parts.shape[0] > 1:
        half = parts.shape[0] // 2
        assert parts.shape[0] == 2 * half
        parts = op(parts[:half], parts[half:])
    return parts[0]


def _col_reduce(x, op):
    t = _fold8(x, op)
    for shift in (4, 2, 1):
        t = op(t, pltpu.roll(t, shift, 0))
    return t[0:1, :]


def _dsa_t_kernel(qt_ref, iqt_ref, iwt_ref, kv_ref, kvt_ref, ik_ref, o_ref, sc_ref, bias_ref, acc_ref):
    CK = DSA_KEY_CHUNK
    QB = Q_BLOCK
    qb = pl.program_id(1)
    nk = (qb * QB + QB + CK - 1) // CK
    kf = float(INDEX_TOPK)
    qpos = qb * QB + lax.broadcasted_iota(jnp.int32, (1, QB), 1)
    krow = lax.broadcasted_iota(jnp.int32, (CK, 1), 0)
    w_idx = iwt_ref[0, IDX_DIM:IDX_DIM + IDX_HEADS, :] * (IDX_HEADS ** -0.5)

    def rows(c):
        return pl.ds(pl.multiple_of(c * CK, CK), CK)

    def heads_on_lanes(ref, width):
        return jnp.concatenate([ref[0, hd * width:(hd + 1) * width, :] for hd in range(ref.shape[1] // width)], axis=1)

    def head_lanes(hd):
        return slice(hd * QB, (hd + 1) * QB)

    iq_wide = heads_on_lanes(iqt_ref, IDX_DIM)
    w_wide = jnp.concatenate([w_idx[hd:hd + 1, :] for hd in range(IDX_HEADS)], axis=1)

    def score_chunk(c, carry):
        mx, mn = carry
        ikc = ik_ref[0, rows(c), :][:, :IDX_DIM]
        s_all = jnp.maximum(_dot(ikc, iq_wide), 0.0) * w_wide
        acc = s_all[:, head_lanes(0)]
        for hd in range(1, IDX_HEADS):
            acc = acc + s_all[:, head_lanes(hd)]
        causal = (c * CK + krow) <= qpos
        sc_ref[rows(c), :] = jnp.where(causal, acc, NEG_INF)
        mx = jnp.maximum(mx, _fold8(jnp.where(causal, acc, NEG_INF), jnp.maximum))
        mn = jnp.minimum(mn, _fold8(jnp.where(causal, acc, jnp.inf), jnp.minimum))
        return mx, mn

    mx8, mn8 = lax.fori_loop(0, nk, score_chunk,
                             (jnp.full((8, QB), NEG_INF, F32), jnp.full((8, QB), jnp.inf, F32)))
    row_max = jnp.max(mx8, axis=0, keepdims=True)
    row_min = jnp.min(mn8, axis=0, keepdims=True)

    @pl.when(nk % 2 == 1)
    def _():
        sc_ref[rows(nk), :] = jnp.full((CK, QB), NEG_INF, F32)

    n_pairs = (nk + 1) // 2

    def pair_rows(c):
        return pl.ds(pl.multiple_of(c * (2 * CK), 2 * CK), 2 * CK)

    def count(pred):
        def body(c, part):
            return part + _fold8(jnp.where(pred(sc_ref[pair_rows(c), :]), 1.0, 0.0), jnp.add)
        part = lax.fori_loop(0, n_pairs, body, jnp.zeros((8, QB), F32))
        return jnp.sum(part, axis=0, keepdims=True)

    @pl.when(qb * QB + QB <= INDEX_TOPK)
    def _():
        def body(c, carry):
            bias_ref[rows(c), :] = jnp.where(sc_ref[rows(c), :] > NEG_INF, 0.0, NEG_INF)
            return carry
        lax.fori_loop(0, nk, body, 0)

    @pl.when(qb * QB + QB > INDEX_TOPK)
    def _():
        def bisect(_, carry):
            lo, hi, c_lo, c_hi = carry
            mid = 0.5 * lo + 0.5 * jnp.minimum(hi, row_max)
            cnt = count(lambda x: x >= mid)
            ge = cnt >= kf
            return (jnp.where(ge, mid, lo), jnp.where(ge, hi, mid),
                    jnp.where(ge, cnt, c_lo), jnp.where(ge, c_hi, cnt))

        n_adm = (qpos + 1).astype(F32)
        lo, hi, c_lo, c_hi = lax.fori_loop(
            0, 18, bisect, (row_min, jnp.full((1, QB), jnp.inf, F32), n_adm, jnp.zeros((1, QB), F32)))

        def refine_cond(carry):
            it, _, _, _, done = carry
            return jnp.logical_and(it < nk * CK, jnp.min(done) < 0.5)

        def refine(carry):
            it, hi, c_hi, thr, done = carry

            def edges(c, part):
                up, dn = part
                x = sc_ref[pair_rows(c), :]
                return (jnp.maximum(up, _fold8(jnp.where(x < hi, x, NEG_INF), jnp.maximum)),
                        jnp.minimum(dn, _fold8(jnp.where(x >= lo, x, jnp.inf), jnp.minimum)))

            up8, dn8 = lax.fori_loop(0, n_pairs, edges,
                                     (jnp.full((8, QB), NEG_INF, F32), jnp.full((8, QB), jnp.inf, F32)))
            m_up = jnp.max(up8, axis=0, keepdims=True)
            m_dn = jnp.min(dn8, axis=0, keepdims=True)
            from_hi = c_hi == kf - 1.0
            from_lo = c_lo == kf
            thr = jnp.where(done > 0.5, thr, jnp.where(from_hi, m_up, m_dn))
            done = jnp.where(jnp.logical_or(from_hi, from_lo), 1.0, done)

            def step_down(args):
                hi, c_hi, thr, done = args
                open_ = done < 0.5
                cnt = count(lambda x: x >= m_up)
                hit = jnp.logical_and(open_, cnt >= kf)
                moved = jnp.logical_and(open_, cnt < kf)
                return (jnp.where(moved, m_up, hi), jnp.where(moved, cnt, c_hi), jnp.where(hit, m_up, thr),
                        jnp.where(hit, 1.0, done))

            hi, c_hi, thr, done = lax.cond(jnp.min(done) > 0.5, lambda args: args, step_down, (hi, c_hi, thr, done))
            return it + 1, hi, c_hi, thr, done

        _, _, _, thr, _ = lax.while_loop(
            refine_cond, refine, (jnp.int32(0), hi, c_hi, row_max, jnp.zeros((1, QB), F32)))

        def body(c, sel):
            keep = sc_ref[rows(c), :] >= thr
            bias_ref[rows(c), :] = jnp.where(keep, 0.0, NEG_INF)
            return sel + _fold8(jnp.where(keep, 1.0, 0.0), jnp.add)

        n_sel = jnp.sum(lax.fori_loop(0, nk, body, jnp.zeros((8, QB), F32)), axis=0, keepdims=True)

        @pl.when(jnp.max(n_sel) > kf + 0.5)
        def _():
            need = kf - count(lambda x: x > thr)
            r_i = lax.broadcasted_iota(jnp.int32, (CK, CK), 0)
            c_i = lax.broadcasted_iota(jnp.int32, (CK, CK), 1)
            lower = jnp.where(c_i <= r_i, 1.0, 0.0).astype(BF16)

            def body(c, seen):
                x = sc_ref[rows(c), :]
                eq = x == thr
                eq_f = jnp.where(eq, 1.0, 0.0)
                rank = _dot(lower, eq_f.astype(BF16)) + seen
                keep = jnp.logical_or(x > thr, jnp.logical_and(eq, rank <= need))
                bias_ref[rows(c), :] = jnp.where(keep, 0.0, NEG_INF)
                return seen + jnp.sum(eq_f, axis=0, keepdims=True)

            lax.fori_loop(0, nk, body, jnp.zeros((1, QB), F32))

    acc_ref[...] = jnp.zeros_like(acc_ref)
    q_wide = heads_on_lanes(qt_ref, HEAD_DIM)

    def attend_chunk(c, m_old):
        kc = kv_ref[0, rows(c), :][:, :HEAD_DIM]
        bias = bias_ref[rows(c), :]
        logits = _dot(kc, q_wide) + jnp.concatenate([bias] * ATTN_HEADS, axis=1)
        m_new = jnp.maximum(m_old, _col_reduce(logits, jnp.maximum))
        alpha = jnp.exp(m_old - m_new)
        p = jnp.exp(logits - m_new)
        acc_ref[...] = alpha * acc_ref[...] + _dot(kvt_ref[0, c, HEAD_DIM - DSA_SUM_ROWS:, :], p.astype(BF16))
        return m_new

    lax.fori_loop(0, nk, attend_chunk, jnp.full((1, ATTN_HEADS * QB), -1e30, F32))
    out_t = acc_ref[DSA_SUM_ROWS:, :] / acc_ref[0:1, :]
    o_ref[0] = jnp.concatenate([out_t[:, head_lanes(hd)] for hd in range(ATTN_HEADS)], axis=0).T


def _dsa_attention_t(qt, iqt, iwt, kv, kvt, ik):
    B, _, S = qt.shape
    col = lambda b, j: (b, 0, j)
    per_b = lambda b, j: (b, 0, 0)
    return pl.pallas_call(
        _dsa_t_kernel,
        out_shape=jax.ShapeDtypeStruct((B, S, ATTN_WIDTH), F32),
        grid=(B, S // Q_BLOCK),
        in_specs=[
            pl.BlockSpec((1, ATTN_WIDTH, Q_BLOCK), col),
            pl.BlockSpec((1, IDX_HEADS * IDX_DIM, Q_BLOCK), col),
            pl.BlockSpec((1, LANES, Q_BLOCK), col),
            pl.BlockSpec((1, S, LANES), per_b),
            pl.BlockSpec((1, S // DSA_KEY_CHUNK, LANES, DSA_KEY_CHUNK), lambda b, j: (b, 0, 0, 0)),
            pl.BlockSpec((1, S, LANES), per_b),
        ],
        out_specs=pl.BlockSpec((1, Q_BLOCK, ATTN_WIDTH), lambda b, j: (b, j, 0)),
        scratch_shapes=[
            pltpu.VMEM((S, Q_BLOCK), F32),
            pltpu.VMEM((S, Q_BLOCK), F32),
            pltpu.VMEM((DSA_SUM_ROWS + HEAD_DIM, ATTN_HEADS * Q_BLOCK), F32),
        ],
        compiler_params=_cparams("parallel", "parallel"),
        name="dsa_attention",
    )(qt, iqt, iwt, kv, kvt, ik)


def _hyb_out_kernel(ya_ref, bcu_ref, halo_ref, cw_ref, w_ref, x_ref, g_ref,
                    gain2_ref, sc2_ref, sh2_ref, wr_ref, br_ref, o_ref, h_ref, r_ref, rt_ref, cnt_ref, run_ref):
    j = pl.program_id(1)
    tm = ya_ref.shape[1]
    bcu = bcu_ref[0]
    bg = bcu[:, 0:512]
    z = bcu[:, 512:1024] * bcu[:, 1024:1536]
    halo = halo_ref[0]
    zh = halo[:, 512:1024] * halo[:, 1024:1536]
    zh = jnp.where(j > 0, zh, 0.0)
    row = lax.broadcasted_iota(jnp.int32, (tm, 1), 0)
    z1 = jnp.where(row >= 1, pltpu.roll(z, 1, 0), zh[7:8, :])
    z2 = jnp.where(row >= 2, pltpu.roll(z, 2, 0), jnp.where(row == 1, zh[7:8, :], zh[6:7, :]))
    cw = cw_ref[...]
    y_conv = bg * (z2 * cw[0:1, :] + z1 * cw[1:2, :] + z * cw[2:3, :])
    y = _dot(ya_ref[0].astype(BF16), w_ref[0:512, :]) + _dot(y_conv.astype(BF16), w_ref[512:1024, :])
    x_new = x_ref[0] + g_ref[0] * y
    o_ref[0] = x_new
    _route_tile(x_new, gain2_ref, sc2_ref, sh2_ref, wr_ref, br_ref, h_ref, r_ref, rt_ref, cnt_ref, run_ref)


def _hyb_out(y_attn, bcu, conv_w, w_out_bf, x, g1, route_args):
    B, S, D = x.shape
    tm = OUT_TILE
    row = lambda b, j: (b, j, 0)
    per_b = lambda b, j: (b, 0, 0)
    const2 = lambda b, j: (0, 0)
    halo = lambda b, j: (b, jnp.maximum(j * (tm // 8) - 1, 0), 0)
    r_in, r_shape, r_out, r_scratch = _route_specs(B, S, D, tm)
    return pl.pallas_call(
        _hyb_out_kernel,
        out_shape=(jax.ShapeDtypeStruct((B, S, D), F32),) + r_shape,
        grid=(B, S // tm),
        in_specs=[
            pl.BlockSpec((1, tm, 512), row),
            pl.BlockSpec((1, tm, 1536), row),
            pl.BlockSpec((1, 8, 1536), halo),
            pl.BlockSpec((CONV_K, CONV_WIDTH), const2),
            pl.BlockSpec((D, D), const2),
            pl.BlockSpec((1, tm, D), row),
            pl.BlockSpec((1, 1, D), per_b),
        ] + r_in,
        out_specs=(pl.BlockSpec((1, tm, D), row),) + r_out,
        scratch_shapes=r_scratch,
        compiler_params=_cparams("arbitrary", "arbitrary"),
        name="hybrid_out_proj",
    )(y_attn, bcu, bcu, conv_w, w_out_bf, x, g1, *route_args)


ML_COLS = 512 + 512 + 1024 + 1024 + LANES


def _ml_in_kernel(x_ref, g2_ref, y0_ref, y1_ref, r_ref, gain_ref, sc_ref, sh_ref, w_ref, bias_ref,
                  q_ref, k_ref, v_ref, og_ref, gt_ref, gtt_ref, xo_ref):
    x = _residual_tile(x_ref, (g2_ref, y0_ref, y1_ref, r_ref))
    xo_ref[0] = x
    h = _modulated_norm(x, gain_ref[...], sc_ref[0], sh_ref[0])
    hb = h.astype(BF16)
    gates = _dot(hb, w_ref[:, 0:LANES]) + bias_ref[...]
    gates_t = gates.T[:2 * ML_HEADS, :]
    keys_t = _dot(hb, w_ref[:, LANES:LANES + 512]).T.astype(BF16)
    for i in range(gt_ref.shape[1]):
        span = slice(i * ML_CHUNK, (i + 1) * ML_CHUNK)
        gt_ref[0, i] = gates[span, :]
        gtt_ref[0, i] = gates_t[:, span]
        k_ref[0, i] = keys_t[:, span]
    q_ref[0] = (_dot(hb, w_ref[:, LANES + 512:LANES + 1024]) * (ML_QK_DIM ** -0.5)).astype(BF16)
    v_ref[0] = _dot(hb, w_ref[:, LANES + 1024:LANES + 2048]).astype(BF16)
    og_ref[0] = _dot(hb, w_ref[:, LANES + 2048:])


def _ml_in(x, pending, gain, sc, sh, w_pad, gate_bias):
    B, S, D = x.shape
    tm = TOKEN_TILE
    L = ML_CHUNK
    assert tm % L == 0
    per_tile = tm // L
    chunked = lambda b, j: (b, j, 0, 0)
    row = lambda b, j: (b, j, 0)
    per_b = lambda b, j: (b, 0, 0)
    const2 = lambda b, j: (0, 0)
    return pl.pallas_call(
        _ml_in_kernel,
        out_shape=(
            jax.ShapeDtypeStruct((B, S, 512), BF16),
            jax.ShapeDtypeStruct((B, S // L, 512, L), BF16),
            jax.ShapeDtypeStruct((B, S, 1024), BF16),
            jax.ShapeDtypeStruct((B, S, 1024), F32),
            jax.ShapeDtypeStruct((B, S // L, L, LANES), F32),
            jax.ShapeDtypeStruct((B, S // L, 2 * ML_HEADS, L), F32),
            jax.ShapeDtypeStruct((B, S, D), F32),
        ),
        grid=(B, S // tm),
        in_specs=[pl.BlockSpec((1, tm, D), row)] + _pending_specs(D, tm) + [
            pl.BlockSpec((1, D), const2),
            pl.BlockSpec((1, 1, D), per_b),
            pl.BlockSpec((1, 1, D), per_b),
            pl.BlockSpec((D, ML_COLS), const2),
            pl.BlockSpec((1, LANES), const2),
        ],
        out_specs=(
            pl.BlockSpec((1, tm, 512), row),
            pl.BlockSpec((1, per_tile, 512, L), chunked),
            pl.BlockSpec((1, tm, 1024), row),
            pl.BlockSpec((1, tm, 1024), row),
            pl.BlockSpec((1, per_tile, L, LANES), chunked),
            pl.BlockSpec((1, per_tile, 2 * ML_HEADS, L), chunked),
            pl.BlockSpec((1, tm, D), row),
        ),
        compiler_params=_cparams("parallel", "parallel"),
        name="mlstm_in_proj",
    )(x, *pending, gain, sc, sh, w_pad, gate_bias)


def _log_sigmoid(f):
    return jnp.minimum(f, 0.0) - jnp.log1p(jnp.exp(-jnp.abs(f)))


def _split3(x):
    a = x.astype(BF16)
    r = x - a.astype(F32)
    b = r.astype(BF16)
    c = (r - b.astype(F32)).astype(BF16)
    return a, b, c


def _twice(a):
    return jnp.concatenate([a, a], axis=1)


def _over_lanes(a, width):
    return jnp.concatenate([a] * (width // LANES), axis=1)


def _mlstm_kernel(q_ref, kt_ref, v_ref, grow_ref, gcol_ref, gain_ref, o_ref, c_ref, m_ref):
    L = ML_CHUNK
    HP = ML_STEP_HEADS
    S = q_ref.shape[1]
    c_ref[...] = jnp.zeros_like(c_ref)
    m_ref[...] = jnp.zeros_like(m_ref)

    def chunk(c, carry):
        r0 = pl.multiple_of(c * L, L)
        r_i = lax.broadcasted_iota(jnp.int32, (L, L), 0)
        c_i = lax.broadcasted_iota(jnp.int32, (L, L), 1)
        tril = c_i <= r_i
        lower = jnp.where(tril, 1.0, 0.0).astype(BF16)
        upper = jnp.where(r_i <= c_i, 1.0, 0.0).astype(BF16)
        e_r = lax.broadcasted_iota(jnp.int32, (LANES, HP * LANES), 0)
        e_c = lax.broadcasted_iota(jnp.int32, (LANES, HP * LANES), 1)
        pick = jnp.where(e_r == HP + e_c // LANES, 1.0, 0.0).astype(BF16)
        rows = grow_ref[0, 0, c]
        cols = gcol_ref[0, 0, c]
        b_rows = sum(_dot(p, upper) for p in _split3(_log_sigmoid(rows)))
        b_cols = sum(_dot(lower, p) for p in _split3(_log_sigmoid(cols)))
        b_colr = sum(_dot(p, pick) for p in _split3(b_cols))
        lane = lax.broadcasted_iota(jnp.int32, (1, L), 1)
        b_last_all = jnp.sum(jnp.where(lane == L - 1, b_rows, 0.0), axis=-1, keepdims=True)
        ones_v = jnp.ones((L, ML_V_DIM), BF16)
        for hh in range(HP):
            q = q_ref[0, pl.ds(r0, L), hh * ML_QK_DIM:(hh + 1) * ML_QK_DIM]
            kt = kt_ref[0, c, hh * ML_QK_DIM:(hh + 1) * ML_QK_DIM, :]
            v = v_ref[0, pl.ds(r0, L), hh * ML_V_DIM:(hh + 1) * ML_V_DIM]
            vx = jnp.concatenate([v, ones_v], axis=1)
            i_row = rows[hh:hh + 1, :]
            b_row = b_rows[HP + hh:HP + hh + 1, :]
            b_last = b_last_all[HP + hh:HP + hh + 1, :]
            b_col = b_colr[:, hh * LANES:(hh + 1) * LANES]
            m_prev = m_ref[hh]
            ctn = c_ref[hh]

            dmat = jnp.where(tril, _over_lanes(b_col, L) - b_row + i_row, NEG_INF)
            inter = b_col + m_prev
            m_t = jnp.maximum(inter, jnp.max(dmat, axis=-1, keepdims=True))
            w_intra = jnp.exp(dmat - _over_lanes(m_t, L))
            w_inter = jnp.exp(inter - m_t)
            intra = (w_intra * _dot(q, kt)).astype(BF16)
            tot = _twice(w_inter) * _dot(q, ctn.astype(BF16)) + _dot(intra, vx)
            num = tot[:, :ML_V_DIM]
            den = tot[:, ML_V_DIM:]
            hc = num / jnp.maximum(jnp.abs(den), jnp.exp(-m_t))
            y = hc * lax.rsqrt(jnp.mean(hc * hc, axis=-1, keepdims=True) + NORM_EPS)
            o_ref[0, pl.ds(r0, L), hh * ML_V_DIM:(hh + 1) * ML_V_DIM] = (
                y * gain_ref[:, hh * ML_V_DIM:(hh + 1) * ML_V_DIM])

            g_row = b_last - b_row + i_row
            m_new = jnp.maximum(b_last + m_prev, jnp.max(g_row, axis=-1, keepdims=True))
            decay = jnp.exp(b_last + m_prev - m_new)
            kw = (kt.astype(F32) * jnp.exp(g_row - _over_lanes(m_new, L))).astype(BF16)
            c_ref[hh] = _twice(decay) * ctn + _dot(kw, vx)
            m_ref[hh] = m_new
        return carry

    lax.fori_loop(0, S // L, chunk, 0)


def _mlstm(q, kt, v, g_rows, g_cols, out_gain):
    B, S, _ = q.shape
    assert ML_CHUNK % LANES == 0
    nc = S // ML_CHUNK
    hp = ML_STEP_HEADS
    return pl.pallas_call(
        _mlstm_kernel,
        out_shape=jax.ShapeDtypeStruct((B, S, ML_HEADS * ML_V_DIM), F32),
        grid=(B, ML_HEADS // hp),
        in_specs=[
            pl.BlockSpec((1, S, hp * ML_QK_DIM), lambda b, p: (b, 0, p)),
            pl.BlockSpec((1, nc, hp * ML_QK_DIM, ML_CHUNK), lambda b, p: (b, 0, p, 0)),
            pl.BlockSpec((1, S, hp * ML_V_DIM), lambda b, p: (b, 0, p)),
            pl.BlockSpec((1, 1, nc, 2 * hp, ML_CHUNK), lambda b, p: (b, p, 0, 0, 0)),
            pl.BlockSpec((1, 1, nc, ML_CHUNK, LANES), lambda b, p: (b, p, 0, 0, 0)),
            pl.BlockSpec((1, hp * ML_V_DIM), lambda b, p: (0, p)),
        ],
        out_specs=pl.BlockSpec((1, S, hp * ML_V_DIM), lambda b, p: (b, 0, p)),
        scratch_shapes=[
            pltpu.VMEM((hp, ML_QK_DIM, ML_V_DIM + LANES), F32),
            pltpu.VMEM((hp, 1, LANES), F32),
        ],
        compiler_params=_cparams("parallel", "parallel"),
        name="mlstm_chunkwise",
    )(q, kt, v, g_rows, g_cols, out_gain)


def _ml_out_kernel(hh_ref, og_ref, w_ref, x_ref, g_ref,
                   gain2_ref, sc2_ref, sh2_ref, wr_ref, br_ref, o_ref, h_ref, r_ref, rt_ref, cnt_ref, run_ref):
    a = jax.nn.sigmoid(og_ref[0]) * hh_ref[0]
    x_new = x_ref[0] + g_ref[0] * _dot(a.astype(BF16), w_ref[...])
    o_ref[0] = x_new
    _route_tile(x_new, gain2_ref, sc2_ref, sh2_ref, wr_ref, br_ref, h_ref, r_ref, rt_ref, cnt_ref, run_ref)


def _ml_out(hh, og, w_out_bf, x, g1, route_args):
    B, S, D = x.shape
    tm = OUT_TILE
    row = lambda b, j: (b, j, 0)
    per_b = lambda b, j: (b, 0, 0)
    const2 = lambda b, j: (0, 0)
    r_in, r_shape, r_out, r_scratch = _route_specs(B, S, D, tm)
    return pl.pallas_call(
        _ml_out_kernel,
        out_shape=(jax.ShapeDtypeStruct((B, S, D), F32),) + r_shape,
        grid=(B, S // tm),
        in_specs=[
            pl.BlockSpec((1, tm, D), row),
            pl.BlockSpec((1, tm, D), row),
            pl.BlockSpec((D, D), const2),
            pl.BlockSpec((1, tm, D), row),
            pl.BlockSpec((1, 1, D), per_b),
        ] + r_in,
        out_specs=(pl.BlockSpec((1, tm, D), row),) + r_out,
        scratch_shapes=r_scratch,
        compiler_params=_cparams("arbitrary", "arbitrary"),
        name="mlstm_out_proj",
    )(hh, og, w_out_bf, x, g1, *route_args)


def _first_argmax(x, lane, width):
    mx = jnp.max(x, axis=-1, keepdims=True)
    idx = jnp.min(jnp.where(x == mx, lane, width), axis=-1, keepdims=True)
    return mx, idx


def _route_tile(x, gain_ref, sc_ref, sh_ref, w_ref, b_ref, h_ref, r_ref, rt_ref, cnt_ref, run_ref):
    tm = x.shape[0]

    @pl.when(jnp.logical_and(pl.program_id(0) == 0, pl.program_id(1) == 0))
    def _():
        run_ref[...] = jnp.zeros_like(run_ref)

    h = _modulated_norm(x, gain_ref[...], sc_ref[0], sh_ref[0])
    h_ref[0] = _pack_bf16_pairs(h)
    logits = _dot(h.astype(BF16), w_ref[...]) + b_ref[...]
    lane = lax.broadcasted_iota(jnp.int32, (1, LANES), 1)

    def pick(lgt):
        lg = jnp.where(lane < N_GROUPS, lgt, NEG_INF)
        g_max, g_sel = _first_argmax(lg, lane, LANES)
        pg = 1.0 / jnp.sum(jnp.exp(lg - g_max), axis=-1, keepdims=True)
        e_lane = lane - N_GROUPS
        in_grp = jnp.logical_and(e_lane >= g_sel * EXPERTS_PER_GROUP, e_lane < (g_sel + 1) * EXPERTS_PER_GROUP)
        le = jnp.where(in_grp, lgt, NEG_INF)
        v1, i1 = _first_argmax(le, lane, LANES)
        v2, i2 = _first_argmax(jnp.where(lane == i1, NEG_INF, le), lane, LANES)
        ratio = jnp.exp(v2 - v1)
        return i1 - N_GROUPS, i2 - N_GROUPS, pg / (1.0 + ratio), pg * ratio / (1.0 + ratio)

    e1, e2, w1, w2 = pick(logits)
    hot1 = lane == e1
    hot2 = lane == e2
    onehot = jnp.where(jnp.logical_or(hot1, hot2), 1.0, 0.0)
    r_i = lax.broadcasted_iota(jnp.int32, (tm, tm), 0)
    c_i = lax.broadcasted_iota(jnp.int32, (tm, tm), 1)
    before = jnp.where(c_i < r_i, 1.0, 0.0).astype(BF16)
    seen = _dot(before, onehot.astype(BF16)) + run_ref[...]
    rank1 = jnp.sum(jnp.where(hot1, seen, 0.0), axis=-1, keepdims=True)
    rank2 = jnp.sum(jnp.where(hot2, seen, 0.0), axis=-1, keepdims=True)
    run_ref[...] = run_ref[...] + jnp.sum(onehot, axis=0, keepdims=True)
    cnt_ref[...] = run_ref[...]

    out = jnp.where(lane == 0, e1.astype(F32), 0.0)
    out = jnp.where(lane == 1, e2.astype(F32), out)
    out = jnp.where(lane == 2, w1, out)
    out = jnp.where(lane == 3, w2, out)
    out = jnp.where(lane == 4, rank1, out)
    out = jnp.where(lane == 5, rank2, out)
    r_ref[0] = out
    rt_ref[0] = out.T[:8, :]


def _route_specs(B, S, D, tm):
    row = lambda b, j: (b, j, 0)
    per_b = lambda b, j: (b, 0, 0)
    const2 = lambda b, j: (0, 0)
    in_specs = [
        pl.BlockSpec((1, D), const2),
        pl.BlockSpec((1, 1, D), per_b),
        pl.BlockSpec((1, 1, D), per_b),
        pl.BlockSpec((D, LANES), const2),
        pl.BlockSpec((1, LANES), const2),
    ]
    out_shape = (
        jax.ShapeDtypeStruct((B, S, D // 2), jnp.int32),
        jax.ShapeDtypeStruct((B, S, LANES), F32),
        jax.ShapeDtypeStruct((B, 8, S), F32),
        jax.ShapeDtypeStruct((1, LANES), F32),
    )
    out_specs = (
        pl.BlockSpec((1, tm, D // 2), row),
        pl.BlockSpec((1, tm, LANES), row),
        pl.BlockSpec((1, 8, tm), lambda b, j: (b, 0, j)),
        pl.BlockSpec((1, LANES), const2),
    )
    return in_specs, out_shape, out_specs, [pltpu.VMEM((1, LANES), F32)]


def _experts_kernel(blk_e_ref, n_used_ref, next_e_ref, x_ref, wg_hbm, wu_hbm, wd_hbm, o_ref,
                    wg_f, wu_f, wd_f, wg_s, wu_s, wd_s, sem, *, layer):
    i = pl.program_id(0)
    used = i < n_used_ref[0]
    e = blk_e_ref[i]
    new_expert = jnp.logical_or(i == 0, e != blk_e_ref[jnp.maximum(i - 1, 0)])

    def fetch(expert):
        return (pltpu.make_async_copy(wg_hbm.at[layer, expert], wg_f, sem.at[0]),
                pltpu.make_async_copy(wu_hbm.at[layer, expert], wu_f, sem.at[1]),
                pltpu.make_async_copy(wd_hbm.at[layer, expert], wd_f, sem.at[2]))

    @pl.when(i == 0)
    def _():
        for cp in fetch(e):
            cp.start()

    @pl.when(jnp.logical_and(used, new_expert))
    def _():
        for cp in fetch(e):
            cp.wait()
        wg_s[...] = wg_f[...].astype(BF16)
        wu_s[...] = wu_f[...].astype(BF16)
        wd_s[...] = wd_f[...].astype(BF16)

        @pl.when(next_e_ref[i] >= 0)
        def _():
            for cp in fetch(next_e_ref[i]):
                cp.start()

    @pl.when(used)
    def _():
        x = _unpack_bf16_pairs(x_ref[...]).astype(BF16)
        a = _dot(x, wg_s[...])
        u = _dot(x, wu_s[...])
        act = a * jax.nn.sigmoid(a) * u
        o_ref[...] = _pack_bf16_pairs(_dot(act.astype(BF16), wd_s[...]))

    @pl.when(i >= n_used_ref[0])
    def _():
        o_ref[...] = jnp.zeros_like(o_ref)


def _experts(layer, blk_e, n_used, next_e, xs, w_gate, w_up, w_down):
    R = xs.shape[0]
    D = 2 * xs.shape[1]
    n_blk = R // MOE_BLOCK
    rows = lambda i, be, nu, ne: (i, 0)
    grid_spec = pltpu.PrefetchScalarGridSpec(
        num_scalar_prefetch=3,
        grid=(n_blk,),
        in_specs=[
            pl.BlockSpec((MOE_BLOCK, D // 2), rows),
            pl.BlockSpec(memory_space=pl.ANY),
            pl.BlockSpec(memory_space=pl.ANY),
            pl.BlockSpec(memory_space=pl.ANY),
        ],
        out_specs=pl.BlockSpec((MOE_BLOCK, D // 2), rows),
        scratch_shapes=[
            pltpu.VMEM((D, D_EXPERT), F32),
            pltpu.VMEM((D, D_EXPERT), F32),
            pltpu.VMEM((D_EXPERT, D), F32),
            pltpu.VMEM((D, D_EXPERT), BF16),
            pltpu.VMEM((D, D_EXPERT), BF16),
            pltpu.VMEM((D_EXPERT, D), BF16),
            pltpu.SemaphoreType.DMA((3,)),
        ],
    )
    return pl.pallas_call(
        functools.partial(_experts_kernel, layer=layer),
        out_shape=jax.ShapeDtypeStruct((R, D // 2), jnp.int32),
        grid_spec=grid_spec,
        compiler_params=_cparams("arbitrary"),
        name="moe_experts",
    )(blk_e, n_used, next_e, xs, w_gate, w_up, w_down)


def _residual_tile(x_ref, pending_refs):
    if not pending_refs:
        return x_ref[0]
    g_ref, y0_ref, y1_ref, r_ref = pending_refs
    r = r_ref[0]
    y = _unpack_bf16_pairs(y0_ref[0, 0]) * r[:, 2:3] + _unpack_bf16_pairs(y1_ref[0, 0]) * r[:, 3:4]
    return x_ref[0] + g_ref[0] * y


def _pending_specs(D, tm):
    return [
        pl.BlockSpec((1, 1, D), lambda b, j: (b, 0, 0)),
        pl.BlockSpec((1, 1, tm, D // 2), lambda b, j: (0, b, j, 0)),
        pl.BlockSpec((1, 1, tm, D // 2), lambda b, j: (1, b, j, 0)),
        pl.BlockSpec((1, tm, LANES), lambda b, j: (b, j, 0)),
    ]


def _combine_kernel(x_ref, g_ref, y0_ref, y1_ref, r_ref, o_ref):
    o_ref[0] = _residual_tile(x_ref, (g_ref, y0_ref, y1_ref, r_ref))


def _combine(x, pending):
    B, S, D = x.shape
    tm = OUT_TILE
    row = lambda b, j: (b, j, 0)
    return pl.pallas_call(
        _combine_kernel,
        out_shape=jax.ShapeDtypeStruct((B, S, D), F32),
        grid=(B, S // tm),
        in_specs=[pl.BlockSpec((1, tm, D), row)] + _pending_specs(D, tm),
        out_specs=pl.BlockSpec((1, tm, D), row),
        compiler_params=_cparams("parallel", "parallel"),
        name="moe_combine",
    )(x, *pending)


SC_CORES = 2
SC_SUBCORES = 16
SC_WORKERS = SC_CORES * SC_SUBCORES
SC_CHUNK = 64


def _sc_mesh():
    return plsc.VectorSubcoreMesh(core_axis_name="c", subcore_axis_name="s",
                                  num_cores=SC_CORES, num_subcores=SC_SUBCORES)


def _sc_scatter_rows(src, idx, n_out):
    T, W = src.shape
    per_w = T // SC_WORKERS
    nch = per_w // SC_CHUNK
    idx4 = idx.reshape(TOP_K, SC_WORKERS, nch, SC_CHUNK)

    @functools.partial(
        pl.kernel, mesh=_sc_mesh(),
        out_type=jax.ShapeDtypeStruct((n_out, W), src.dtype),
        scratch_types=[
            pltpu.VMEM((TOP_K, nch, SC_CHUNK), jnp.int32),
            pltpu.VMEM((2, SC_CHUNK, W), src.dtype),
            pltpu.SemaphoreType.DMA((2,)),
            pltpu.SemaphoreType.DMA((2 * TOP_K,)),
        ],
        name="sc_scatter_rows",
    )
    def body(src_hbm, idx_hbm, out_hbm, idx_v, rows_v, load_sem, scatter_sem):
        wid = lax.axis_index("s") * SC_CORES + lax.axis_index("c")
        for s in range(TOP_K):
            pltpu.sync_copy(idx_hbm.at[s, wid], idx_v.at[s])

        def load(j, b):
            return pltpu.make_async_copy(
                src_hbm.at[pl.ds(wid * per_w + j * SC_CHUNK, SC_CHUNK)], rows_v.at[b], load_sem.at[b])

        def scatter(s, j, b):
            return pltpu.make_async_copy(rows_v.at[b], out_hbm.at[idx_v.at[s, j]], scatter_sem.at[2 * s + b])

        load(0, 0).start()

        @pl.loop(0, nch, step=2)
        def _(i):
            for b in range(2):
                j = i + b

                @pl.when(j >= 1)
                def _():
                    for s in range(TOP_K):
                        scatter(s, j - 1, 1 - b).wait()

                @pl.when(j + 1 < nch)
                def _():
                    load(j + 1, 1 - b).start()

                load(j, b).wait()
                for s in range(TOP_K):
                    scatter(s, j, b).start()

        for s in range(TOP_K):
            scatter(s, nch - 1, (nch - 1) % 2).wait()

    assert nch % 2 == 0
    return body(src, idx4)


def _sc_gather_rows(table, idx):
    N = idx.shape[0]
    W = table.shape[1]
    per_w = N // SC_WORKERS
    nch = per_w // SC_CHUNK
    idx3 = idx.reshape(SC_WORKERS, nch, SC_CHUNK)

    @functools.partial(
        pl.kernel, mesh=_sc_mesh(),
        out_type=jax.ShapeDtypeStruct((N, W), table.dtype),
        scratch_types=[
            pltpu.VMEM((nch, SC_CHUNK), jnp.int32),
            pltpu.VMEM((2, SC_CHUNK, W), table.dtype),
            pltpu.SemaphoreType.DMA((2,)),
            pltpu.SemaphoreType.DMA((2,)),
        ],
        name="sc_gather_rows",
    )
    def body(table_hbm, idx_hbm, out_hbm, idx_v, rows_v, gather_sem, write_sem):
        wid = lax.axis_index("s") * SC_CORES + lax.axis_index("c")
        pltpu.sync_copy(idx_hbm.at[wid], idx_v)

        def gather(j, b):
            return pltpu.make_async_copy(table_hbm.at[idx_v.at[j]], rows_v.at[b], gather_sem.at[b])

        def write(j, b):
            return pltpu.make_async_copy(
                rows_v.at[b], out_hbm.at[pl.ds(wid * per_w + j * SC_CHUNK, SC_CHUNK)], write_sem.at[b])

        gather(0, 0).start()

        @pl.loop(0, nch, step=2)
        def _(i):
            for b in range(2):
                j = i + b

                @pl.when(j >= 1)
                def _():
                    write(j - 1, 1 - b).wait()

                @pl.when(j + 1 < nch)
                def _():
                    gather(j + 1, 1 - b).start()

                gather(j, b).wait()
                write(j, b).start()

        write(nch - 1, (nch - 1) % 2).wait()

    assert nch % 2 == 0
    return body(table, idx3)


def _moe_dispatch(route_t, counts, T):
    A = T * TOP_K
    counts = counts[0, :N_EXPERTS].astype(jnp.int32)
    blocks_per = (counts + MOE_BLOCK - 1) // MOE_BLOCK
    block_end = jnp.cumsum(blocks_per)
    block_start = block_end - blocks_per
    expert = jnp.swapaxes(route_t[:, :TOP_K, :], 0, 1).reshape(TOP_K, T).astype(jnp.int32)
    rank = jnp.swapaxes(route_t[:, 4:4 + TOP_K, :], 0, 1).reshape(TOP_K, T).astype(jnp.int32)
    onehot = expert[None] == jnp.arange(N_EXPERTS, dtype=jnp.int32)[:, None, None]
    start = jnp.sum(jnp.where(onehot, block_start[:, None, None], 0), axis=0)
    dest = start * MOE_BLOCK + rank
    n_blk = -(-A // MOE_BLOCK) + N_EXPERTS
    blk = jnp.arange(n_blk, dtype=jnp.int32)
    blk_e = jnp.minimum(jnp.sum(blk[:, None] >= block_end[None, :], axis=-1), N_EXPERTS - 1).astype(jnp.int32)
    n_used = block_end[-1]
    first = jnp.logical_and(blk < n_used, jnp.logical_or(blk == 0, blk_e != jnp.roll(blk_e, 1)))
    first_pos = jnp.where(first, blk, n_blk)
    next_pos = jnp.concatenate([lax.cummin(first_pos, axis=0, reverse=True)[1:], jnp.full((1,), n_blk, jnp.int32)])
    next_e = jnp.where(next_pos < n_blk, blk_e[jnp.minimum(next_pos, n_blk - 1)], -1).astype(jnp.int32)
    return dest, n_blk * MOE_BLOCK, blk_e, n_used.reshape(1).astype(jnp.int32), next_e


def _rope_tables(S):
    inv = 1.0 / (ROPE_THETA ** (jnp.arange(0, HEAD_DIM, 2, dtype=F32) / HEAD_DIM))
    ang = jnp.arange(S, dtype=F32)[:, None] * inv[None, :]
    cos, sin = jnp.cos(ang), jnp.sin(ang)
    cos_h = jnp.concatenate([cos, cos], axis=-1)
    sin_h = jnp.concatenate([-sin, sin], axis=-1)
    return jnp.tile(cos_h, (1, ATTN_HEADS)), jnp.tile(sin_h, (1, ATTN_HEADS))


def _pad_cols(w, width):
    return jnp.pad(w, ((0, 0), (0, width - w.shape[1])))


def kernel(x, c, ada_w, ada_b, norm_mix, norm_ffn, hy_w_in, hy_q_norm, hy_k_norm, hy_conv_w, hy_w_out, ml_w_in, ml_b_gates, ml_out_norm, ml_w_out, moe_w_group, moe_b_group, moe_w_expert, moe_b_expert, moe_w_gate, moe_w_up, moe_w_down):
    B, S, D = x.shape
    T = B * S
    cos_t, sin_t = _rope_tables(S)
    mod = _ada_modulation(c, ada_w, ada_b).reshape(DEPTH, B, 6, 1, D)
    r_i = np.arange(ATTN_WIDTH)
    grp = jnp.asarray((r_i[:, None] // HEAD_DIM) == (r_i[None, :] // HEAD_DIM), dtype=BF16)

    pending = ()
    for l in range(DEPTH):
        sh1, sc1, g1, sh2, sc2, g2 = [mod[l, :, i] for i in range(6)]
        gain1 = norm_mix[l].reshape(1, D)
        w_r = _pad_cols(jnp.concatenate([moe_w_group[l], moe_w_expert[l]], axis=1), LANES).astype(BF16)
        b_r = jnp.pad(jnp.concatenate([moe_b_group[l], moe_b_expert[l]]), (0, LANES - N_GROUPS - N_EXPERTS))
        route_args = (norm_ffn[l].reshape(1, D), sc2, sh2, w_r, b_r.reshape(1, LANES))
        j = l // 2
        if l % 2 == 0:
            w = hy_w_in[j]
            o = np.cumsum((0,) + (ATTN_WIDTH, HEAD_DIM, HEAD_DIM, IDX_HEADS * IDX_DIM, IDX_DIM, IDX_HEADS,
                                  CONV_WIDTH, CONV_WIDTH, CONV_WIDTH))
            wq, wk, wv, wiq, wik, wiw, wbg, wcg, wu = [w[:, o[i]:o[i + 1]] for i in range(9)]
            w_pad = jnp.concatenate(
                [wk, wv, _pad_cols(jnp.concatenate([wik, wiw], axis=1), LANES), wq, wiq, wbg, wcg, wu],
                axis=1).astype(BF16)
            qn_t = jnp.tile(hy_q_norm[j], ATTN_HEADS).reshape(1, ATTN_WIDTH)
            kn_t = jnp.tile(hy_k_norm[j], LANES // HEAD_DIM).reshape(1, LANES)
            outs = _hyb_in(x, pending, gain1, sc1, sh1, w_pad, cos_t, sin_t, qn_t, kn_t, grp)
            qt, iqt, bcu, kv, kvt, ik, iwt = outs[:7]
            if pending:
                x = outs[7]
            y_attn = _dsa_attention_t(qt, iqt, iwt, kv, kvt, ik)
            x, h2, route, route_t, counts =_hyb_out(y_attn, bcu, hy_conv_w[j], hy_w_out[j].astype(BF16), x, g1, route_args)
        else:
            w = ml_w_in[j]
            hq = ML_HEADS * ML_QK_DIM
            hv = ML_HEADS * ML_V_DIM
            wq, wk, wv = w[:, :hq], w[:, hq:2 * hq], w[:, 2 * hq:2 * hq + hv]
            wg = w[:, 2 * hq + hv:2 * hq + hv + 2 * ML_HEADS]
            wo = w[:, 2 * hq + hv + 2 * ML_HEADS:]
            w_pad = jnp.concatenate([_pad_cols(wg, LANES), wk, wq, wv, wo], axis=1).astype(BF16)
            gate_bias = jnp.pad(ml_b_gates[j], (0, LANES - 2 * ML_HEADS)).reshape(1, LANES)
            q, k, v, og, g_cols, g_rows, x = _ml_in(x, pending, gain1, sc1, sh1, w_pad, gate_bias)
            assert ML_STEP_HEADS == ML_HEADS
            hh = _mlstm(q, k, v, g_rows[:, None], g_cols[:, None], ml_out_norm[j].reshape(1, hv))
            x, h2, route, route_t, counts =_ml_out(hh, og, ml_w_out[j].astype(BF16), x, g1, route_args)

        dest, n_rows, blk_e, n_used, next_e = _moe_dispatch(route_t, counts, T)
        xs = _sc_scatter_rows(h2.reshape(T, D // 2), dest, n_rows)
        ys = _experts(l, blk_e, n_used, next_e, xs, moe_w_gate, moe_w_up, moe_w_down)
        y01 = _sc_gather_rows(ys, dest.reshape(TOP_K * T)).reshape(TOP_K, B, S, D // 2)
        pending = (g2, y01, y01, route)
    return _combine(x, pending)
```

```python
import functools

import numpy as np
import jax
import jax.numpy as jnp
from jax import lax
from jax.experimental import pallas as pl
from jax.experimental.pallas import tpu as pltpu
from jax.experimental.pallas import tpu_sc as plsc

F32 = jnp.float32
BF16 = jnp.bfloat16
HIGHEST = lax.Precision.HIGHEST

D_MODEL = 1024
DEPTH = 4
ATTN_HEADS = 8
HEAD_DIM = 64
ATTN_WIDTH = ATTN_HEADS * HEAD_DIM
IDX_HEADS = 8
IDX_DIM = 64
INDEX_TOPK = 256
Q_BLOCK = 256
ROPE_THETA = 10000.0
CONV_WIDTH = D_MODEL - ATTN_WIDTH
CONV_K = 3
ML_HEADS = 8
ML_QK_DIM = 64
ML_V_DIM = 128
N_GROUPS = 4
EXPERTS_PER_GROUP = 8
N_EXPERTS = N_GROUPS * EXPERTS_PER_GROUP
TOP_K = 2
D_EXPERT = 512
MOE_BLOCK = 512
NORM_EPS = 1e-6

LANES = 128
VMEM_LIMIT = 56 * 1024 * 1024
TOKEN_TILE = 512
OUT_TILE = 1024
ML_CHUNK = 256
ML_STEP_HEADS = 8
NEG_INF = float("-inf")


def _cparams(*sem):
    return pltpu.CompilerParams(dimension_semantics=sem, vmem_limit_bytes=VMEM_LIMIT)


def _dot(a, b):
    return jnp.dot(a, b, preferred_element_type=F32)


def _pack_bf16_pairs(x):
    bits = lax.bitcast_convert_type(x.astype(BF16).astype(F32), jnp.uint32)
    half = bits.shape[1] // 2
    packed = (bits[:, :half] >> 16) | (bits[:, half:] & jnp.uint32(0xFFFF0000))
    return lax.bitcast_convert_type(packed, jnp.int32)


def _unpack_bf16_pairs(words):
    words = lax.bitcast_convert_type(words, jnp.uint32)
    return jnp.concatenate(
        [lax.bitcast_convert_type(words << 16, F32),
         lax.bitcast_convert_type(words & jnp.uint32(0xFFFF0000), F32)], axis=1)


def _split_dot(a_f32, b_bf16):
    hi = a_f32.astype(BF16)
    lo = (a_f32 - hi.astype(F32)).astype(BF16)
    return _dot(hi, b_bf16) + _dot(lo, b_bf16)


def _ada_kernel(c_ref, w_ref, b_ref, o_ref):
    c = c_ref[...]
    ca = c * jax.nn.sigmoid(c)
    o_ref[0] = jnp.dot(ca, w_ref[0], precision=HIGHEST, preferred_element_type=F32) + b_ref[0]


def _ada_modulation(c, ada_w, ada_b):
    B, D = c.shape
    n_col = ada_w.shape[-1] // D
    return pl.pallas_call(
        _ada_kernel,
        out_shape=jax.ShapeDtypeStruct((DEPTH, B, n_col * D), F32),
        grid=(DEPTH, n_col),
        in_specs=[
            pl.BlockSpec((B, D), lambda l, j: (0, 0)),
            pl.BlockSpec((1, D, D), lambda l, j: (l, 0, j)),
            pl.BlockSpec((1, 1, D), lambda l, j: (l, 0, j)),
        ],
        out_specs=pl.BlockSpec((1, B, D), lambda l, j: (l, 0, j)),
        compiler_params=_cparams("parallel", "parallel"),
        name="ada_modulation",
    )(c, ada_w, ada_b.reshape(DEPTH, 1, n_col * D))


def _modulated_norm(x, gain, scale, shift):
    y = x * lax.rsqrt(jnp.mean(x * x, axis=-1, keepdims=True) + NORM_EPS)
    return y * gain * (1.0 + scale) + shift


def _rope(x, cos, sin_signed, first_half):
    w = x.shape[-1]
    partner = jnp.where(first_half, pltpu.roll(x, w - HEAD_DIM // 2, 1), pltpu.roll(x, HEAD_DIM // 2, 1))
    return x * cos + partner * sin_signed


HYB_COLS = 5 * 512 + 2 * LANES


def _hyb_in_kernel(*refs, n_pending):
    x_ref, pending_refs, refs = refs[0], refs[1:1 + n_pending], refs[1 + n_pending:]
    (gain_ref, sc_ref, sh_ref, w_ref, cos_ref, sin_ref, qn_ref, kn_ref, grp_ref,
     qt_ref, iqt_ref, bcu_ref, kv_ref, kvt_ref, ik_ref, iwt_ref) = refs[:16]
    x = _residual_tile(x_ref, pending_refs)
    if n_pending:
        refs[16][0] = x
    h = _modulated_norm(x, gain_ref[...], sc_ref[0], sh_ref[0])
    hb = h.astype(BF16)
    p_small = _dot(hb, w_ref[:, 0:2 * LANES])
    p_q = _dot(hb, w_ref[:, 2 * LANES:2 * LANES + 512])
    p_iq = _dot(hb, w_ref[:, 2 * LANES + 512:2 * LANES + 1024])
    cos = cos_ref[...]
    sin = sin_ref[...]
    lane = lax.broadcasted_iota(jnp.int32, (1, ATTN_WIDTH), 1)
    first_half = (lane % HEAD_DIM) < (HEAD_DIM // 2)
    fh128 = first_half[:, :LANES]
    lane128 = lane[:, :LANES]

    kv = p_small[:, :LANES]
    is_k = lane128 < HEAD_DIM
    kk = jnp.where(is_k, kv, 0.0)
    ms_k = jnp.sum(kk * kk, axis=-1, keepdims=True) * (1.0 / HEAD_DIM)
    kn = kv * lax.rsqrt(ms_k + NORM_EPS) * kn_ref[...]
    kr = _rope(kn, cos[:, :LANES], sin[:, :LANES], fh128)
    kv = jnp.where(is_k, kr, kv)
    kv_ref[0] = kv.astype(BF16)
    feat = lax.broadcasted_iota(jnp.int32, (LANES, 1), 0)
    kv_t = jnp.where(feat < HEAD_DIM, 1.0, kv.T).astype(BF16)
    for i in range(kvt_ref.shape[1]):
        kvt_ref[0, i] = kv_t[:, i * DSA_KEY_CHUNK:(i + 1) * DSA_KEY_CHUNK]

    sm = p_small[:, LANES:]
    ikr = _rope(sm, cos[:, :LANES], sin[:, :LANES], fh128)
    ik_ref[0] = jnp.where(is_k, ikr, 0.0).astype(BF16)
    iwt_ref[0] = sm.T

    ms = _split_dot(p_q * p_q, grp_ref[...]) * (1.0 / HEAD_DIM)
    q = p_q * lax.rsqrt(ms + NORM_EPS) * qn_ref[...]
    qt_ref[0] = (_rope(q, cos, sin, first_half) * (HEAD_DIM ** -0.5)).T.astype(BF16)
    iqt_ref[0] = (_rope(p_iq, cos, sin, first_half) * (IDX_DIM ** -0.5)).T.astype(BF16)

    bcu_ref[0] = _dot(hb, w_ref[:, 2 * LANES + 1024:])


def _hyb_in(x, pending, gain, sc, sh, w_pad, cos_t, sin_t, qn_t, kn_t, grp):
    B, S, D = x.shape
    tm = TOKEN_TILE
    row = lambda b, j: (b, j, 0)
    per_b = lambda b, j: (b, 0, 0)
    const2 = lambda b, j: (0, 0)
    tab = lambda b, j: (j, 0)
    col = lambda b, j: (b, 0, j)
    assert tm % DSA_KEY_CHUNK == 0
    x_out_shape = (jax.ShapeDtypeStruct((B, S, D), F32),) if pending else ()
    x_out_spec = (pl.BlockSpec((1, tm, D), row),) if pending else ()
    return pl.pallas_call(
        functools.partial(_hyb_in_kernel, n_pending=len(pending)),
        out_shape=(
            jax.ShapeDtypeStruct((B, 512, S), BF16),
            jax.ShapeDtypeStruct((B, 512, S), BF16),
            jax.ShapeDtypeStruct((B, S, 1536), F32),
            jax.ShapeDtypeStruct((B, S, LANES), BF16),
            jax.ShapeDtypeStruct((B, S // DSA_KEY_CHUNK, LANES, DSA_KEY_CHUNK), BF16),
            jax.ShapeDtypeStruct((B, S, LANES), BF16),
            jax.ShapeDtypeStruct((B, LANES, S), F32),
        ) + x_out_shape,
        grid=(B, S // tm),
        in_specs=[pl.BlockSpec((1, tm, D), row)] + (_pending_specs(D, tm) if pending else []) + [
            pl.BlockSpec((1, D), const2),
            pl.BlockSpec((1, 1, D), per_b),
            pl.BlockSpec((1, 1, D), per_b),
            pl.BlockSpec((D, HYB_COLS), const2),
            pl.BlockSpec((tm, 512), tab),
            pl.BlockSpec((tm, 512), tab),
            pl.BlockSpec((1, 512), const2),
            pl.BlockSpec((1, LANES), const2),
            pl.BlockSpec((512, 512), const2),
        ],
        out_specs=(
            pl.BlockSpec((1, 512, tm), col),
            pl.BlockSpec((1, 512, tm), col),
            pl.BlockSpec((1, tm, 1536), row),
            pl.BlockSpec((1, tm, LANES), row),
            pl.BlockSpec((1, tm // DSA_KEY_CHUNK, LANES, DSA_KEY_CHUNK), lambda b, j: (b, j, 0, 0)),
            pl.BlockSpec((1, tm, LANES), row),
            pl.BlockSpec((1, LANES, tm), col),
        ) + x_out_spec,
        compiler_params=_cparams("parallel", "parallel"),
        name="hybrid_in_proj",
    )(x, *pending, gain, sc, sh, w_pad, cos_t, sin_t, qn_t, kn_t, grp)


DSA_KEY_CHUNK = 256
DSA_SUM_ROWS = 16


def _fold8(x, op):
    parts = x.reshape(x.shape[0] // 8, 8, x.shape[1])
    while parts.shape[0] > 1:
        half = parts.shape[0] // 2
        assert parts.shape[0] == 2 * half
        parts = op(parts[:half], parts[half:])
    return parts[0]


def _col_reduce(x, op):
    t = _fold8(x, op)
    for shift in (4, 2, 1):
        t = op(t, pltpu.roll(t, shift, 0))
    return t[0:1, :]


def _dsa_t_kernel(qt_ref, iqt_ref, iwt_ref, kv_ref, kvt_ref, ik_ref, o_ref, sc_ref, bias_ref, acc_ref):
    CK = DSA_KEY_CHUNK
    QB = Q_BLOCK
    qb = pl.program_id(1)
    nk = (qb * QB + QB + CK - 1) // CK
    kf = float(INDEX_TOPK)
    qpos = qb * QB + lax.broadcasted_iota(jnp.int32, (1, QB), 1)
    krow = lax.broadcasted_iota(jnp.int32, (CK, 1), 0)
    w_idx = iwt_ref[0, IDX_DIM:IDX_DIM + IDX_HEADS, :] * (IDX_HEADS ** -0.5)

    def rows(c):
        return pl.ds(pl.multiple_of(c * CK, CK), CK)

    def heads_on_lanes(ref, width):
        return jnp.concatenate([ref[0, hd * width:(hd + 1) * width, :] for hd in range(ref.shape[1] // width)], axis=1)

    def head_lanes(hd):
        return slice(hd * QB, (hd + 1) * QB)

    iq_wide = heads_on_lanes(iqt_ref, IDX_DIM)
    w_wide = jnp.concatenate([w_idx[hd:hd + 1, :] for hd in range(IDX_HEADS)], axis=1)

    def score_chunk(c, carry):
        mx, mn = carry
        ikc = ik_ref[0, rows(c), :][:, :IDX_DIM]
        s_all = jnp.maximum(_dot(ikc, iq_wide), 0.0) * w_wide
        acc = s_all[:, head_lanes(0)]
        for hd in range(1, IDX_HEADS):
            acc = acc + s_all[:, head_lanes(hd)]
        causal = (c * CK + krow) <= qpos
        sc_ref[rows(c), :] = jnp.where(causal, acc, NEG_INF)
        mx = jnp.maximum(mx, _fold8(jnp.where(causal, acc, NEG_INF), jnp.maximum))
        mn = jnp.minimum(mn, _fold8(jnp.where(causal, acc, jnp.inf), jnp.minimum))
        return mx, mn

    mx8, mn8 = lax.fori_loop(0, nk, score_chunk,
                             (jnp.full((8, QB), NEG_INF, F32), jnp.full((8, QB), jnp.inf, F32)))
    row_max = jnp.max(mx8, axis=0, keepdims=True)
    row_min = jnp.min(mn8, axis=0, keepdims=True)

    @pl.when(nk % 2 == 1)
    def _():
        sc_ref[rows(nk), :] = jnp.full((CK, QB), NEG_INF, F32)

    n_pairs = (nk + 1) // 2

    def pair_rows(c):
        return pl.ds(pl.multiple_of(c * (2 * CK), 2 * CK), 2 * CK)

    def count(pred):
        def body(c, part):
            return part + _fold8(jnp.where(pred(sc_ref[pair_rows(c), :]), 1.0, 0.0), jnp.add)
        part = lax.fori_loop(0, n_pairs, body, jnp.zeros((8, QB), F32))
        return jnp.sum(part, axis=0, keepdims=True)

    @pl.when(qb * QB + QB <= INDEX_TOPK)
    def _():
        def body(c, carry):
            bias_ref[rows(c), :] = jnp.where(sc_ref[rows(c), :] > NEG_INF, 0.0, NEG_INF)
            return carry
        lax.fori_loop(0, nk, body, 0)

    @pl.when(qb * QB + QB > INDEX_TOPK)
    def _():
        def bisect(_, carry):
            lo, hi, c_lo, c_hi = carry
            mid = 0.5 * lo + 0.5 * jnp.minimum(hi, row_max)
            cnt = count(lambda x: x >= mid)
            ge = cnt >= kf
            return (jnp.where(ge, mid, lo), jnp.where(ge, hi, mid),
                    jnp.where(ge, cnt, c_lo), jnp.where(ge, c_hi, cnt))

        n_adm = (qpos + 1).astype(F32)
        lo, hi, c_lo, c_hi = lax.fori_loop(
            0, 18, bisect, (row_min, jnp.full((1, QB), jnp.inf, F32), n_adm, jnp.zeros((1, QB), F32)))

        def refine_cond(carry):
            it, _, _, _, done = carry
            return jnp.logical_and(it < nk * CK, jnp.min(done) < 0.5)

        def refine(carry):
            it, hi, c_hi, thr, done = carry

            def edges(c, part):
                up, dn = part
                x = sc_ref[pair_rows(c), :]
                return (jnp.maximum(up, _fold8(jnp.where(x < hi, x, NEG_INF), jnp.maximum)),
                        jnp.minimum(dn, _fold8(jnp.where(x >= lo, x, jnp.inf), jnp.minimum)))

            up8, dn8 = lax.fori_loop(0, n_pairs, edges,
                                     (jnp.full((8, QB), NEG_INF, F32), jnp.full((8, QB), jnp.inf, F32)))
            m_up = jnp.max(up8, axis=0, keepdims=True)
            m_dn = jnp.min(dn8, axis=0, keepdims=True)
            from_hi = c_hi == kf - 1.0
            from_lo = c_lo == kf
            thr = jnp.where(done > 0.5, thr, jnp.where(from_hi, m_up, m_dn))
            done = jnp.where(jnp.logical_or(from_hi, from_lo), 1.0, done)

            def step_down(args):
                hi, c_hi, thr, done = args
                open_ = done < 0.5
                cnt = count(lambda x: x >= m_up)
                hit = jnp.logical_and(open_, cnt >= kf)
                moved = jnp.logical_and(open_, cnt < kf)
                return (jnp.where(moved, m_up, hi), jnp.where(moved, cnt, c_hi), jnp.where(hit, m_up, thr),
                        jnp.where(hit, 1.0, done))

            hi, c_hi, thr, done = lax.cond(jnp.min(done) > 0.5, lambda args: args, step_down, (hi, c_hi, thr, done))
            return it + 1, hi, c_hi, thr, done

        _, _, _, thr, _ = lax.while_loop(
            refine_cond, refine, (jnp.int32(0), hi, c_hi, row_max, jnp.zeros((1, QB), F32)))

        def body(c, sel):
            keep = sc_ref[rows(c), :] >= thr
            bias_ref[rows(c), :] = jnp.where(keep, 0.0, NEG_INF)
            return sel + _fold8(jnp.where(keep, 1.0, 0.0), jnp.add)

        n_sel = jnp.sum(lax.fori_loop(0, nk, body, jnp.zeros((8, QB), F32)), axis=0, keepdims=True)

        @pl.when(jnp.max(n_sel) > kf + 0.5)
        def _():
            need = kf - count(lambda x: x > thr)
            r_i = lax.broadcasted_iota(jnp.int32, (CK, CK), 0)
            c_i = lax.broadcasted_iota(jnp.int32, (CK, CK), 1)
            lower = jnp.where(c_i <= r_i, 1.0, 0.0).astype(BF16)

            def body(c, seen):
                x = sc_ref[rows(c), :]
                eq = x == thr
                eq_f = jnp.where(eq, 1.0, 0.0)
                rank = _dot(lower, eq_f.astype(BF16)) + seen
                keep = jnp.logical_or(x > thr, jnp.logical_and(eq, rank <= need))
                bias_ref[rows(c), :] = jnp.where(keep, 0.0, NEG_INF)
                return seen + jnp.sum(eq_f, axis=0, keepdims=True)

            lax.fori_loop(0, nk, body, jnp.zeros((1, QB), F32))

    acc_ref[...] = jnp.zeros_like(acc_ref)
    q_wide = heads_on_lanes(qt_ref, HEAD_DIM)

    def attend_chunk(c, m_old):
        kc = kv_ref[0, rows(c), :][:, :HEAD_DIM]
        bias = bias_ref[rows(c), :]
        logits = _dot(kc, q_wide) + jnp.concatenate([bias] * ATTN_HEADS, axis=1)
        m_new = jnp.maximum(m_old, _col_reduce(logits, jnp.maximum))
        alpha = jnp.exp(m_old - m_new)
        p = jnp.exp(logits - m_new)
        acc_ref[...] = alpha * acc_ref[...] + _dot(kvt_ref[0, c, HEAD_DIM - DSA_SUM_ROWS:, :], p.astype(BF16))
        return m_new

    lax.fori_loop(0, nk, attend_chunk, jnp.full((1, ATTN_HEADS * QB), -1e30, F32))
    out_t = acc_ref[DSA_SUM_ROWS:, :] / acc_ref[0:1, :]
    o_ref[0] = jnp.concatenate([out_t[:, head_lanes(hd)] for hd in range(ATTN_HEADS)], axis=0).T


def _dsa_attention_t(qt, iqt, iwt, kv, kvt, ik):
    B, _, S = qt.shape
    col = lambda b, j: (b, 0, j)
    per_b = lambda b, j: (b, 0, 0)
    return pl.pallas_call(
        _dsa_t_kernel,
        out_shape=jax.ShapeDtypeStruct((B, S, ATTN_WIDTH), F32),
        grid=(B, S // Q_BLOCK),
        in_specs=[
            pl.BlockSpec((1, ATTN_WIDTH, Q_BLOCK), col),
            pl.BlockSpec((1, IDX_HEADS * IDX_DIM, Q_BLOCK), col),
            pl.BlockSpec((1, LANES, Q_BLOCK), col),
            pl.BlockSpec((1, S, LANES), per_b),
            pl.BlockSpec((1, S // DSA_KEY_CHUNK, LANES, DSA_KEY_CHUNK), lambda b, j: (b, 0, 0, 0)),
            pl.BlockSpec((1, S, LANES), per_b),
        ],
        out_specs=pl.BlockSpec((1, Q_BLOCK, ATTN_WIDTH), lambda b, j: (b, j, 0)),
        scratch_shapes=[
            pltpu.VMEM((S, Q_BLOCK), F32),
            pltpu.VMEM((S, Q_BLOCK), F32),
            pltpu.VMEM((DSA_SUM_ROWS + HEAD_DIM, ATTN_HEADS * Q_BLOCK), F32),
        ],
        compiler_params=_cparams("parallel", "parallel"),
        name="dsa_attention",
    )(qt, iqt, iwt, kv, kvt, ik)


def _hyb_out_kernel(ya_ref, bcu_ref, halo_ref, cw_ref, w_ref, x_ref, g_ref,
                    gain2_ref, sc2_ref, sh2_ref, wr_ref, br_ref, o_ref, h_ref, r_ref, rt_ref, cnt_ref, run_ref):
    j = pl.program_id(1)
    tm = ya_ref.shape[1]
    bcu = bcu_ref[0]
    bg = bcu[:, 0:512]
    z = bcu[:, 512:1024] * bcu[:, 1024:1536]
    halo = halo_ref[0]
    zh = halo[:, 512:1024] * halo[:, 1024:1536]
    zh = jnp.where(j > 0, zh, 0.0)
    row = lax.broadcasted_iota(jnp.int32, (tm, 1), 0)
    z1 = jnp.where(row >= 1, pltpu.roll(z, 1, 0), zh[7:8, :])
    z2 = jnp.where(row >= 2, pltpu.roll(z, 2, 0), jnp.where(row == 1, zh[7:8, :], zh[6:7, :]))
    cw = cw_ref[...]
    y_conv = bg * (z2 * cw[0:1, :] + z1 * cw[1:2, :] + z * cw[2:3, :])
    y = _dot(ya_ref[0].astype(BF16), w_ref[0:512, :]) + _dot(y_conv.astype(BF16), w_ref[512:1024, :])
    x_new = x_ref[0] + g_ref[0] * y
    o_ref[0] = x_new
    _route_tile(x_new, gain2_ref, sc2_ref, sh2_ref, wr_ref, br_ref, h_ref, r_ref, rt_ref, cnt_ref, run_ref)


def _hyb_out(y_attn, bcu, conv_w, w_out_bf, x, g1, route_args):
    B, S, D = x.shape
    tm = OUT_TILE
    row = lambda b, j: (b, j, 0)
    per_b = lambda b, j: (b, 0, 0)
    const2 = lambda b, j: (0, 0)
    halo = lambda b, j: (b, jnp.maximum(j * (tm // 8) - 1, 0), 0)
    r_in, r_shape, r_out, r_scratch = _route_specs(B, S, D, tm)
    return pl.pallas_call(
        _hyb_out_kernel,
        out_shape=(jax.ShapeDtypeStruct((B, S, D), F32),) + r_shape,
        grid=(B, S // tm),
        in_specs=[
            pl.BlockSpec((1, tm, 512), row),
            pl.BlockSpec((1, tm, 1536), row),
            pl.BlockSpec((1, 8, 1536), halo),
            pl.BlockSpec((CONV_K, CONV_WIDTH), const2),
            pl.BlockSpec((D, D), const2),
            pl.BlockSpec((1, tm, D), row),
            pl.BlockSpec((1, 1, D), per_b),
        ] + r_in,
        out_specs=(pl.BlockSpec((1, tm, D), row),) + r_out,
        scratch_shapes=r_scratch,
        compiler_params=_cparams("arbitrary", "arbitrary"),
        name="hybrid_out_proj",
    )(y_attn, bcu, bcu, conv_w, w_out_bf, x, g1, *route_args)


ML_COLS = 512 + 512 + 1024 + 1024 + LANES


def _ml_in_kernel(x_ref, g2_ref, y0_ref, y1_ref, r_ref, gain_ref, sc_ref, sh_ref, w_ref, bias_ref,
                  q_ref, k_ref, v_ref, og_ref, gt_ref, gtt_ref, xo_ref):
    x = _residual_tile(x_ref, (g2_ref, y0_ref, y1_ref, r_ref))
    xo_ref[0] = x
    h = _modulated_norm(x, gain_ref[...], sc_ref[0], sh_ref[0])
    hb = h.astype(BF16)
    gates = _dot(hb, w_ref[:, 0:LANES]) + bias_ref[...]
    gates_t = gates.T[:2 * ML_HEADS, :]
    keys_t = _dot(hb, w_ref[:, LANES:LANES + 512]).T.astype(BF16)
    for i in range(gt_ref.shape[1]):
        span = slice(i * ML_CHUNK, (i + 1) * ML_CHUNK)
        gt_ref[0, i] = gates[span, :]
        gtt_ref[0, i] = gates_t[:, span]
        k_ref[0, i] = keys_t[:, span]
    q_ref[0] = (_dot(hb, w_ref[:, LANES + 512:LANES + 1024]) * (ML_QK_DIM ** -0.5)).astype(BF16)
    v_ref[0] = _dot(hb, w_ref[:, LANES + 1024:LANES + 2048]).astype(BF16)
    og_ref[0] = _dot(hb, w_ref[:, LANES + 2048:])


def _ml_in(x, pending, gain, sc, sh, w_pad, gate_bias):
    B, S, D = x.shape
    tm = TOKEN_TILE
    L = ML_CHUNK
    assert tm % L == 0
    per_tile = tm // L
    chunked = lambda b, j: (b, j, 0, 0)
    row = lambda b, j: (b, j, 0)
    per_b = lambda b, j: (b, 0, 0)
    const2 = lambda b, j: (0, 0)
    return pl.pallas_call(
        _ml_in_kernel,
        out_shape=(
            jax.ShapeDtypeStruct((B, S, 512), BF16),
            jax.ShapeDtypeStruct((B, S // L, 512, L), BF16),
            jax.ShapeDtypeStruct((B, S, 1024), BF16),
            jax.ShapeDtypeStruct((B, S, 1024), F32),
            jax.ShapeDtypeStruct((B, S // L, L, LANES), F32),
            jax.ShapeDtypeStruct((B, S // L, 2 * ML_HEADS, L), F32),
            jax.ShapeDtypeStruct((B, S, D), F32),
        ),
        grid=(B, S // tm),
        in_specs=[pl.BlockSpec((1, tm, D), row)] + _pending_specs(D, tm) + [
            pl.BlockSpec((1, D), const2),
            pl.BlockSpec((1, 1, D), per_b),
            pl.BlockSpec((1, 1, D), per_b),
            pl.BlockSpec((D, ML_COLS), const2),
            pl.BlockSpec((1, LANES), const2),
        ],
        out_specs=(
            pl.BlockSpec((1, tm, 512), row),
            pl.BlockSpec((1, per_tile, 512, L), chunked),
            pl.BlockSpec((1, tm, 1024), row),
            pl.BlockSpec((1, tm, 1024), row),
            pl.BlockSpec((1, per_tile, L, LANES), chunked),
            pl.BlockSpec((1, per_tile, 2 * ML_HEADS, L), chunked),
            pl.BlockSpec((1, tm, D), row),
        ),
        compiler_params=_cparams("parallel", "parallel"),
        name="mlstm_in_proj",
    )(x, *pending, gain, sc, sh, w_pad, gate_bias)


def _log_sigmoid(f):
    return jnp.minimum(f, 0.0) - jnp.log1p(jnp.exp(-jnp.abs(f)))


def _split3(x):
    a = x.astype(BF16)
    r = x - a.astype(F32)
    b = r.astype(BF16)
    c = (r - b.astype(F32)).astype(BF16)
    return a, b, c


def _twice(a):
    return jnp.concatenate([a, a], axis=1)


def _over_lanes(a, width):
    return jnp.concatenate([a] * (width // LANES), axis=1)


def _mlstm_kernel(q_ref, kt_ref, v_ref, grow_ref, gcol_ref, gain_ref, o_ref, c_ref, m_ref):
    L = ML_CHUNK
    HP = ML_STEP_HEADS
    S = q_ref.shape[1]
    c_ref[...] = jnp.zeros_like(c_ref)
    m_ref[...] = jnp.zeros_like(m_ref)

    def chunk(c, carry):
        r0 = pl.multiple_of(c * L, L)
        r_i = lax.broadcasted_iota(jnp.int32, (L, L), 0)
        c_i = lax.broadcasted_iota(jnp.int32, (L, L), 1)
        tril = c_i <= r_i
        lower = jnp.where(tril, 1.0, 0.0).astype(BF16)
        upper = jnp.where(r_i <= c_i, 1.0, 0.0).astype(BF16)
        e_r = lax.broadcasted_iota(jnp.int32, (LANES, HP * LANES), 0)
        e_c = lax.broadcasted_iota(jnp.int32, (LANES, HP * LANES), 1)
        pick = jnp.where(e_r == HP + e_c // LANES, 1.0, 0.0).astype(BF16)
        rows = grow_ref[0, 0, c]
        cols = gcol_ref[0, 0, c]
        b_rows = sum(_dot(p, upper) for p in _split3(_log_sigmoid(rows)))
        b_cols = sum(_dot(lower, p) for p in _split3(_log_sigmoid(cols)))
        b_colr = sum(_dot(p, pick) for p in _split3(b_cols))
        lane = lax.broadcasted_iota(jnp.int32, (1, L), 1)
        b_last_all = jnp.sum(jnp.where(lane == L - 1, b_rows, 0.0), axis=-1, keepdims=True)
        ones_v = jnp.ones((L, ML_V_DIM), BF16)
        for hh in range(HP):
            q = q_ref[0, pl.ds(r0, L), hh * ML_QK_DIM:(hh + 1) * ML_QK_DIM]
            kt = kt_ref[0, c, hh * ML_QK_DIM:(hh + 1) * ML_QK_DIM, :]
            v = v_ref[0, pl.ds(r0, L), hh * ML_V_DIM:(hh + 1) * ML_V_DIM]
            vx = jnp.concatenate([v, ones_v], axis=1)
            i_row = rows[hh:hh + 1, :]
            b_row = b_rows[HP + hh:HP + hh + 1, :]
            b_last = b_last_all[HP + hh:HP + hh + 1, :]
            b_col = b_colr[:, hh * LANES:(hh + 1) * LANES]
            m_prev = m_ref[hh]
            ctn = c_ref[hh]

            dmat = jnp.where(tril, _over_lanes(b_col, L) - b_row + i_row, NEG_INF)
            inter = b_col + m_prev
            m_t = jnp.maximum(inter, jnp.max(dmat, axis=-1, keepdims=True))
            w_intra = jnp.exp(dmat - _over_lanes(m_t, L))
            w_inter = jnp.exp(inter - m_t)
            intra = (w_intra * _dot(q, kt)).astype(BF16)
            tot = _twice(w_inter) * _dot(q, ctn.astype(BF16)) + _dot(intra, vx)
            num = tot[:, :ML_V_DIM]
            den = tot[:, ML_V_DIM:]
            hc = num / jnp.maximum(jnp.abs(den), jnp.exp(-m_t))
            y = hc * lax.rsqrt(jnp.mean(hc * hc, axis=-1, keepdims=True) + NORM_EPS)
            o_ref[0, pl.ds(r0, L), hh * ML_V_DIM:(hh + 1) * ML_V_DIM] = (
                y * gain_ref[:, hh * ML_V_DIM:(hh + 1) * ML_V_DIM])

            g_row = b_last - b_row + i_row
            m_new = jnp.maximum(b_last + m_prev, jnp.max(g_row, axis=-1, keepdims=True))
            decay = jnp.exp(b_last + m_prev - m_new)
            kw = (kt.astype(F32) * jnp.exp(g_row - _over_lanes(m_new, L))).astype(BF16)
            c_ref[hh] = _twice(decay) * ctn + _dot(kw, vx)
            m_ref[hh] = m_new
        return carry

    lax.fori_loop(0, S // L, chunk, 0)


def _mlstm(q, kt, v, g_rows, g_cols, out_gain):
    B, S, _ = q.shape
    assert ML_CHUNK % LANES == 0
    nc = S // ML_CHUNK
    hp = ML_STEP_HEADS
    return pl.pallas_call(
        _mlstm_kernel,
        out_shape=jax.ShapeDtypeStruct((B, S, ML_HEADS * ML_V_DIM), F32),
        grid=(B, ML_HEADS // hp),
        in_specs=[
            pl.BlockSpec((1, S, hp * ML_QK_DIM), lambda b, p: (b, 0, p)),
            pl.BlockSpec((1, nc, hp * ML_QK_DIM, ML_CHUNK), lambda b, p: (b, 0, p, 0)),
            pl.BlockSpec((1, S, hp * ML_V_DIM), lambda b, p: (b, 0, p)),
            pl.BlockSpec((1, 1, nc, 2 * hp, ML_CHUNK), lambda b, p: (b, p, 0, 0, 0)),
            pl.BlockSpec((1, 1, nc, ML_CHUNK, LANES), lambda b, p: (b, p, 0, 0, 0)),
            pl.BlockSpec((1, hp * ML_V_DIM), lambda b, p: (0, p)),
        ],
        out_specs=pl.BlockSpec((1, S, hp * ML_V_DIM), lambda b, p: (b, 0, p)),
        scratch_shapes=[
            pltpu.VMEM((hp, ML_QK_DIM, ML_V_DIM + LANES), F32),
            pltpu.VMEM((hp, 1, LANES), F32),
        ],
        compiler_params=_cparams("parallel", "parallel"),
        name="mlstm_chunkwise",
    )(q, kt, v, g_rows, g_cols, out_gain)


def _ml_out_kernel(hh_ref, og_ref, w_ref, x_ref, g_ref,
                   gain2_ref, sc2_ref, sh2_ref, wr_ref, br_ref, o_ref, h_ref, r_ref, rt_ref, cnt_ref, run_ref):
    a = jax.nn.sigmoid(og_ref[0]) * hh_ref[0]
    x_new = x_ref[0] + g_ref[0] * _dot(a.astype(BF16), w_ref[...])
    o_ref[0] = x_new
    _route_tile(x_new, gain2_ref, sc2_ref, sh2_ref, wr_ref, br_ref, h_ref, r_ref, rt_ref, cnt_ref, run_ref)


def _ml_out(hh, og, w_out_bf, x, g1, route_args):
    B, S, D = x.shape
    tm = OUT_TILE
    row = lambda b, j: (b, j, 0)
    per_b = lambda b, j: (b, 0, 0)
    const2 = lambda b, j: (0, 0)
    r_in, r_shape, r_out, r_scratch = _route_specs(B, S, D, tm)
    return pl.pallas_call(
        _ml_out_kernel,
        out_shape=(jax.ShapeDtypeStruct((B, S, D), F32),) + r_shape,
        grid=(B, S // tm),
        in_specs=[
            pl.BlockSpec((1, tm, D), row),
            pl.BlockSpec((1, tm, D), row),
            pl.BlockSpec((D, D), const2),
            pl.BlockSpec((1, tm, D), row),
            pl.BlockSpec((1, 1, D), per_b),
        ] + r_in,
        out_specs=(pl.BlockSpec((1, tm, D), row),) + r_out,
        scratch_shapes=r_scratch,
        compiler_params=_cparams("arbitrary", "arbitrary"),
        name="mlstm_out_proj",
    )(hh, og, w_out_bf, x, g1, *route_args)


def _first_argmax(x, lane, width):
    mx = jnp.max(x, axis=-1, keepdims=True)
    idx = jnp.min(jnp.where(x == mx, lane, width), axis=-1, keepdims=True)
    return mx, idx


def _route_tile(x, gain_ref, sc_ref, sh_ref, w_ref, b_ref, h_ref, r_ref, rt_ref, cnt_ref, run_ref):
    tm = x.shape[0]

    @pl.when(jnp.logical_and(pl.program_id(0) == 0, pl.program_id(1) == 0))
    def _():
        run_ref[...] = jnp.zeros_like(run_ref)

    h = _modulated_norm(x, gain_ref[...], sc_ref[0], sh_ref[0])
    h_ref[0] = _pack_bf16_pairs(h)
    logits = _dot(h.astype(BF16), w_ref[...]) + b_ref[...]
    lane = lax.broadcasted_iota(jnp.int32, (1, LANES), 1)

    def pick(lgt):
        lg = jnp.where(lane < N_GROUPS, lgt, NEG_INF)
        g_max, g_sel = _first_argmax(lg, lane, LANES)
        pg = 1.0 / jnp.sum(jnp.exp(lg - g_max), axis=-1, keepdims=True)
        e_lane = lane - N_GROUPS
        in_grp = jnp.logical_and(e_lane >= g_sel * EXPERTS_PER_GROUP, e_lane < (g_sel + 1) * EXPERTS_PER_GROUP)
        le = jnp.where(in_grp, lgt, NEG_INF)
        v1, i1 = _first_argmax(le, lane, LANES)
        v2, i2 = _first_argmax(jnp.where(lane == i1, NEG_INF, le), lane, LANES)
        ratio = jnp.exp(v2 - v1)
        return i1 - N_GROUPS, i2 - N_GROUPS, pg / (1.0 + ratio), pg * ratio / (1.0 + ratio)

    e1, e2, w1, w2 = pick(logits)
    hot1 = lane == e1
    hot2 = lane == e2
    onehot = jnp.where(jnp.logical_or(hot1, hot2), 1.0, 0.0)
    r_i = lax.broadcasted_iota(jnp.int32, (tm, tm), 0)
    c_i = lax.broadcasted_iota(jnp.int32, (tm, tm), 1)
    before = jnp.where(c_i < r_i, 1.0, 0.0).astype(BF16)
    seen = _dot(before, onehot.astype(BF16)) + run_ref[...]
    rank1 = jnp.sum(jnp.where(hot1, seen, 0.0), axis=-1, keepdims=True)
    rank2 = jnp.sum(jnp.where(hot2, seen, 0.0), axis=-1, keepdims=True)
    run_ref[...] = run_ref[...] + jnp.sum(onehot, axis=0, keepdims=True)
    cnt_ref[...] = run_ref[...]

    out = jnp.where(lane == 0, e1.astype(F32), 0.0)
    out = jnp.where(lane == 1, e2.astype(F32), out)
    out = jnp.where(lane == 2, w1, out)
    out = jnp.where(lane == 3, w2, out)
    out = jnp.where(lane == 4, rank1, out)
    out = jnp.where(lane == 5, rank2, out)
    r_ref[0] = out
    rt_ref[0] = out.T[:8, :]


def _route_specs(B, S, D, tm):
    row = lambda b, j: (b, j, 0)
    per_b = lambda b, j: (b, 0, 0)
    const2 = lambda b, j: (0, 0)
    in_specs = [
        pl.BlockSpec((1, D), const2),
        pl.BlockSpec((1, 1, D), per_b),
        pl.BlockSpec((1, 1, D), per_b),
        pl.BlockSpec((D, LANES), const2),
        pl.BlockSpec((1, LANES), const2),
    ]
    out_shape = (
        jax.ShapeDtypeStruct((B, S, D // 2), jnp.int32),
        jax.ShapeDtypeStruct((B, S, LANES), F32),
        jax.ShapeDtypeStruct((B, 8, S), F32),
        jax.ShapeDtypeStruct((1, LANES), F32),
    )
    out_specs = (
        pl.BlockSpec((1, tm, D // 2), row),
        pl.BlockSpec((1, tm, LANES), row),
        pl.BlockSpec((1, 8, tm), lambda b, j: (b, 0, j)),
        pl.BlockSpec((1, LANES), const2),
    )
    return in_specs, out_shape, out_specs, [pltpu.VMEM((1, LANES), F32)]


def _experts_kernel(blk_e_ref, n_used_ref, next_e_ref, x_ref, wg_hbm, wu_hbm, wd_hbm, o_ref,
                    wg_f, wu_f, wd_f, wg_s, wu_s, wd_s, sem, *, layer):
    i = pl.program_id(0)
    used = i < n_used_ref[0]
    e = blk_e_ref[i]
    new_expert = jnp.logical_or(i == 0, e != blk_e_ref[jnp.maximum(i - 1, 0)])

    def fetch(expert):
        return (pltpu.make_async_copy(wg_hbm.at[layer, expert], wg_f, sem.at[0]),
                pltpu.make_async_copy(wu_hbm.at[layer, expert], wu_f, sem.at[1]),
                pltpu.make_async_copy(wd_hbm.at[layer, expert], wd_f, sem.at[2]))

    @pl.when(i == 0)
    def _():
        for cp in fetch(e):
            cp.start()

    @pl.when(jnp.logical_and(used, new_expert))
    def _():
        for cp in fetch(e):
            cp.wait()
        wg_s[...] = wg_f[...].astype(BF16)
        wu_s[...] = wu_f[...].astype(BF16)
        wd_s[...] = wd_f[...].astype(BF16)

        @pl.when(next_e_ref[i] >= 0)
        def _():
            for cp in fetch(next_e_ref[i]):
                cp.start()

    @pl.when(used)
    def _():
        x = _unpack_bf16_pairs(x_ref[...]).astype(BF16)
        a = _dot(x, wg_s[...])
        u = _dot(x, wu_s[...])
        act = a * jax.nn.sigmoid(a) * u
        o_ref[...] = _pack_bf16_pairs(_dot(act.astype(BF16), wd_s[...]))

    @pl.when(i >= n_used_ref[0])
    def _():
        o_ref[...] = jnp.zeros_like(o_ref)


def _experts(layer, blk_e, n_used, next_e, xs, w_gate, w_up, w_down):
    R = xs.shape[0]
    D = 2 * xs.shape[1]
    n_blk = R // MOE_BLOCK
    rows = lambda i, be, nu, ne: (i, 0)
    grid_spec = pltpu.PrefetchScalarGridSpec(
        num_scalar_prefetch=3,
        grid=(n_blk,),
        in_specs=[
            pl.BlockSpec((MOE_BLOCK, D // 2), rows),
            pl.BlockSpec(memory_space=pl.ANY),
            pl.BlockSpec(memory_space=pl.ANY),
            pl.BlockSpec(memory_space=pl.ANY),
        ],
        out_specs=pl.BlockSpec((MOE_BLOCK, D // 2), rows),
        scratch_shapes=[
            pltpu.VMEM((D, D_EXPERT), F32),
            pltpu.VMEM((D, D_EXPERT), F32),
            pltpu.VMEM((D_EXPERT, D), F32),
            pltpu.VMEM((D, D_EXPERT), BF16),
            pltpu.VMEM((D, D_EXPERT), BF16),
            pltpu.VMEM((D_EXPERT, D), BF16),
            pltpu.SemaphoreType.DMA((3,)),
        ],
    )
    return pl.pallas_call(
        functools.partial(_experts_kernel, layer=layer),
        out_shape=jax.ShapeDtypeStruct((R, D // 2), jnp.int32),
        grid_spec=grid_spec,
        compiler_params=_cparams("arbitrary"),
        name="moe_experts",
    )(blk_e, n_used, next_e, xs, w_gate, w_up, w_down)


def _residual_tile(x_ref, pending_refs):
    if not pending_refs:
        return x_ref[0]
    g_ref, y0_ref, y1_ref, r_ref = pending_refs
    r = r_ref[0]
    y = _unpack_bf16_pairs(y0_ref[0, 0]) * r[:, 2:3] + _unpack_bf16_pairs(y1_ref[0, 0]) * r[:, 3:4]
    return x_ref[0] + g_ref[0] * y


def _pending_specs(D, tm):
    return [
        pl.BlockSpec((1, 1, D), lambda b, j: (b, 0, 0)),
        pl.BlockSpec((1, 1, tm, D // 2), lambda b, j: (0, b, j, 0)),
        pl.BlockSpec((1, 1, tm, D // 2), lambda b, j: (1, b, j, 0)),
        pl.BlockSpec((1, tm, LANES), lambda b, j: (b, j, 0)),
    ]


def _combine_kernel(x_ref, g_ref, y0_ref, y1_ref, r_ref, o_ref):
    o_ref[0] = _residual_tile(x_ref, (g_ref, y0_ref, y1_ref, r_ref))


def _combine(x, pending):
    B, S, D = x.shape
    tm = OUT_TILE
    row = lambda b, j: (b, j, 0)
    return pl.pallas_call(
        _combine_kernel,
        out_shape=jax.ShapeDtypeStruct((B, S, D), F32),
        grid=(B, S // tm),
        in_specs=[pl.BlockSpec((1, tm, D), row)] + _pending_specs(D, tm),
        out_specs=pl.BlockSpec((1, tm, D), row),
        compiler_params=_cparams("parallel", "parallel"),
        name="moe_combine",
    )(x, *pending)


SC_CORES = 2
SC_SUBCORES = 16
SC_WORKERS = SC_CORES * SC_SUBCORES
SC_CHUNK = 64


def _sc_mesh():
    return plsc.VectorSubcoreMesh(core_axis_name="c", subcore_axis_name="s",
                                  num_cores=SC_CORES, num_subcores=SC_SUBCORES)


def _sc_scatter_rows(src, idx, n_out):
    T, W = src.shape
    per_w = T // SC_WORKERS
    nch = per_w // SC_CHUNK
    idx4 = idx.reshape(TOP_K, SC_WORKERS, nch, SC_CHUNK)

    @functools.partial(
        pl.kernel, mesh=_sc_mesh(),
        out_type=jax.ShapeDtypeStruct((n_out, W), src.dtype),
        scratch_types=[
            pltpu.VMEM((TOP_K, nch, SC_CHUNK), jnp.int32),
            pltpu.VMEM((2, SC_CHUNK, W), src.dtype),
            pltpu.SemaphoreType.DMA((2,)),
            pltpu.SemaphoreType.DMA((2 * TOP_K,)),
        ],
        name="sc_scatter_rows",
    )
    def body(src_hbm, idx_hbm, out_hbm, idx_v, rows_v, load_sem, scatter_sem):
        wid = lax.axis_index("s") * SC_CORES + lax.axis_index("c")
        for s in range(TOP_K):
            pltpu.sync_copy(idx_hbm.at[s, wid], idx_v.at[s])

        def load(j, b):
            return pltpu.make_async_copy(
                src_hbm.at[pl.ds(wid * per_w + j * SC_CHUNK, SC_CHUNK)], rows_v.at[b], load_sem.at[b])

        def scatter(s, j, b):
            return pltpu.make_async_copy(rows_v.at[b], out_hbm.at[idx_v.at[s, j]], scatter_sem.at[2 * s + b])

        load(0, 0).start()

        @pl.loop(0, nch, step=2)
        def _(i):
            for b in range(2):
                j = i + b

                @pl.when(j >= 1)
                def _():
                    for s in range(TOP_K):
                        scatter(s, j - 1, 1 - b).wait()

                @pl.when(j + 1 < nch)
                def _():
                    load(j + 1, 1 - b).start()

                load(j, b).wait()
                for s in range(TOP_K):
                    scatter(s, j, b).start()

        for s in range(TOP_K):
            scatter(s, nch - 1, (nch - 1) % 2).wait()

    assert nch % 2 == 0
    return body(src, idx4)


def _sc_gather_rows(table, idx):
    N = idx.shape[0]
    W = table.shape[1]
    per_w = N // SC_WORKERS
    nch = per_w // SC_CHUNK
    idx3 = idx.reshape(SC_WORKERS, nch, SC_CHUNK)

    @functools.partial(
        pl.kernel, mesh=_sc_mesh(),
        out_type=jax.ShapeDtypeStruct((N, W), table.dtype),
        scratch_types=[
            pltpu.VMEM((nch, SC_CHUNK), jnp.int32),
            pltpu.VMEM((2, SC_CHUNK, W), table.dtype),
            pltpu.SemaphoreType.DMA((2,)),
            pltpu.SemaphoreType.DMA((2,)),
        ],
        name="sc_gather_rows",
    )
    def body(table_hbm, idx_hbm, out_hbm, idx_v, rows_v, gather_sem, write_sem):
        wid = lax.axis_index("s") * SC_CORES + lax.axis_index("c")
        pltpu.sync_copy(idx_hbm.at[wid], idx_v)

        def gather(j, b):
            return pltpu.make_async_copy(table_hbm.at[idx_v.at[j]], rows_v.at[b], gather_sem.at[b])

        def write(j, b):
            return pltpu.make_async_copy(
                rows_v.at[b], out_hbm.at[pl.ds(wid * per_w + j * SC_CHUNK, SC_CHUNK)], write_sem.at[b])

        gather(0, 0).start()

        @pl.loop(0, nch, step=2)
        def _(i):
            for b in range(2):
                j = i + b

                @pl.when(j >= 1)
                def _():
                    write(j - 1, 1 - b).wait()

                @pl.when(j + 1 < nch)
                def _():
                    gather(j + 1, 1 - b).start()

                gather(j, b).wait()
                write(j, b).start()

        write(nch - 1, (nch - 1) % 2).wait()

    assert nch % 2 == 0
    return body(table, idx3)


def _moe_dispatch(route_t, counts, T):
    A = T * TOP_K
    counts = counts[0, :N_EXPERTS].astype(jnp.int32)
    blocks_per = (counts + MOE_BLOCK - 1) // MOE_BLOCK
    block_end = jnp.cumsum(blocks_per)
    block_start = block_end - blocks_per
    expert = jnp.swapaxes(route_t[:, :TOP_K, :], 0, 1).reshape(TOP_K, T).astype(jnp.int32)
    rank = jnp.swapaxes(route_t[:, 4:4 + TOP_K, :], 0, 1).reshape(TOP_K, T).astype(jnp.int32)
    onehot = expert[None] == jnp.arange(N_EXPERTS, dtype=jnp.int32)[:, None, None]
    start = jnp.sum(jnp.where(onehot, block_start[:, None, None], 0), axis=0)
    dest = start * MOE_BLOCK + rank
    n_blk = -(-A // MOE_BLOCK) + N_EXPERTS
    blk = jnp.arange(n_blk, dtype=jnp.int32)
    blk_e = jnp.minimum(jnp.sum(blk[:, None] >= block_end[None, :], axis=-1), N_EXPERTS - 1).astype(jnp.int32)
    n_used = block_end[-1]
    first = jnp.logical_and(blk < n_used, jnp.logical_or(blk == 0, blk_e != jnp.roll(blk_e, 1)))
    first_pos = jnp.where(first, blk, n_blk)
    next_pos = jnp.concatenate([lax.cummin(first_pos, axis=0, reverse=True)[1:], jnp.full((1,), n_blk, jnp.int32)])
    next_e = jnp.where(next_pos < n_blk, blk_e[jnp.minimum(next_pos, n_blk - 1)], -1).astype(jnp.int32)
    return dest, n_blk * MOE_BLOCK, blk_e, n_used.reshape(1).astype(jnp.int32), next_e


def _rope_tables(S):
    inv = 1.0 / (ROPE_THETA ** (jnp.arange(0, HEAD_DIM, 2, dtype=F32) / HEAD_DIM))
    ang = jnp.arange(S, dtype=F32)[:, None] * inv[None, :]
    cos, sin = jnp.cos(ang), jnp.sin(ang)
    cos_h = jnp.concatenate([cos, cos], axis=-1)
    sin_h = jnp.concatenate([-sin, sin], axis=-1)
    return jnp.tile(cos_h, (1, ATTN_HEADS)), jnp.tile(sin_h, (1, ATTN_HEADS))


def _pad_cols(w, width):
    return jnp.pad(w, ((0, 0), (0, width - w.shape[1])))


def kernel(x, c, ada_w, ada_b, norm_mix, norm_ffn, hy_w_in, hy_q_norm, hy_k_norm, hy_conv_w, hy_w_out, ml_w_in, ml_b_gates, ml_out_norm, ml_w_out, moe_w_group, moe_b_group, moe_w_expert, moe_b_expert, moe_w_gate, moe_w_up, moe_w_down):
    B, S, D = x.shape
    T = B * S
    cos_t, sin_t = _rope_tables(S)
    mod = _ada_modulation(c, ada_w, ada_b).reshape(DEPTH, B, 6, 1, D)
    r_i = np.arange(ATTN_WIDTH)
    grp = jnp.asarray((r_i[:, None] // HEAD_DIM) == (r_i[None, :] // HEAD_DIM), dtype=BF16)

    pending = ()
    for l in range(DEPTH):
        sh1, sc1, g1, sh2, sc2, g2 = [mod[l, :, i] for i in range(6)]
        gain1 = norm_mix[l].reshape(1, D)
        w_r = _pad_cols(jnp.concatenate([moe_w_group[l], moe_w_expert[l]], axis=1), LANES).astype(BF16)
        b_r = jnp.pad(jnp.concatenate([moe_b_group[l], moe_b_expert[l]]), (0, LANES - N_GROUPS - N_EXPERTS))
        route_args = (norm_ffn[l].reshape(1, D), sc2, sh2, w_r, b_r.reshape(1, LANES))
        j = l // 2
        if l % 2 == 0:
            w = hy_w_in[j]
            o = np.cumsum((0,) + (ATTN_WIDTH, HEAD_DIM, HEAD_DIM, IDX_HEADS * IDX_DIM, IDX_DIM, IDX_HEADS,
                                  CONV_WIDTH, CONV_WIDTH, CONV_WIDTH))
            wq, wk, wv, wiq, wik, wiw, wbg, wcg, wu = [w[:, o[i]:o[i + 1]] for i in range(9)]
            w_pad = jnp.concatenate(
                [wk, wv, _pad_cols(jnp.concatenate([wik, wiw], axis=1), LANES), wq, wiq, wbg, wcg, wu],
                axis=1).astype(BF16)
            qn_t = jnp.tile(hy_q_norm[j], ATTN_HEADS).reshape(1, ATTN_WIDTH)
            kn_t = jnp.tile(hy_k_norm[j], LANES // HEAD_DIM).reshape(1, LANES)
            outs = _hyb_in(x, pending, gain1, sc1, sh1, w_pad, cos_t, sin_t, qn_t, kn_t, grp)
            qt, iqt, bcu, kv, kvt, ik, iwt = outs[:7]
            if pending:
                x = outs[7]
            y_attn = _dsa_attention_t(qt, iqt, iwt, kv, kvt, ik)
            x, h2, route, route_t, counts =_hyb_out(y_attn, bcu, hy_conv_w[j], hy_w_out[j].astype(BF16), x, g1, route_args)
        else:
            w = ml_w_in[j]
            hq = ML_HEADS * ML_QK_DIM
            hv = ML_HEADS * ML_V_DIM
            wq, wk, wv = w[:, :hq], w[:, hq:2 * hq], w[:, 2 * hq:2 * hq + hv]
            wg = w[:, 2 * hq + hv:2 * hq + hv + 2 * ML_HEADS]
            wo = w[:, 2 * hq + hv + 2 * ML_HEADS:]
            w_pad = jnp.concatenate([_pad_cols(wg, LANES), wk, wq, wv, wo], axis=1).astype(BF16)
            gate_bias = jnp.pad(ml_b_gates[j], (0, LANES - 2 * ML_HEADS)).reshape(1, LANES)
            q, k, v, og, g_cols, g_rows, x = _ml_in(x, pending, gain1, sc1, sh1, w_pad, gate_bias)
            assert ML_STEP_HEADS == ML_HEADS
            hh = _mlstm(q, k, v, g_rows[:, None], g_cols[:, None], ml_out_norm[j].reshape(1, hv))
            x, h2, route, route_t, counts =_ml_out(hh, og, ml_w_out[j].astype(BF16), x, g1, route_args)

        dest, n_rows, blk_e, n_used, next_e = _moe_dispatch(route_t, counts, T)
        xs = _sc_scatter_rows(h2.reshape(T, D // 2), dest, n_rows)
        ys = _experts(l, blk_e, n_used, next_e, xs, moe_w_gate, moe_w_up, moe_w_down)
        y01 = _sc_gather_rows(ys, dest.reshape(TOP_K * T)).reshape(TOP_K, B, S, D // 2)
        pending = (g2, y01, y01, route)
    return _combine(x, pending)
```

```python
import functools

import numpy as np
import jax
import jax.numpy as jnp
from jax import lax
from jax.experimental import pallas as pl
from jax.experimental.pallas import tpu as pltpu
from jax.experimental.pallas import tpu_sc as plsc

F32 = jnp.float32
BF16 = jnp.bfloat16
HIGHEST = lax.Precision.HIGHEST

D_MODEL = 1024
DEPTH = 4
ATTN_HEADS = 8
HEAD_DIM = 64
ATTN_WIDTH = ATTN_HEADS * HEAD_DIM
IDX_HEADS = 8
IDX_DIM = 64
INDEX_TOPK = 256
Q_BLOCK = 256
ROPE_THETA = 10000.0
CONV_WIDTH = D_MODEL - ATTN_WIDTH
CONV_K = 3
ML_HEADS = 8
ML_QK_DIM = 64
ML_V_DIM = 128
N_GROUPS = 4
EXPERTS_PER_GROUP = 8
N_EXPERTS = N_GROUPS * EXPERTS_PER_GROUP
TOP_K = 2
D_EXPERT = 512
MOE_BLOCK = 512
NORM_EPS = 1e-6

LANES = 128
VMEM_LIMIT = 56 * 1024 * 1024
TOKEN_TILE = 512
OUT_TILE = 1024
ML_CHUNK = 256
ML_STEP_HEADS = 8
NEG_INF = float("-inf")


def _cparams(*sem):
    return pltpu.CompilerParams(dimension_semantics=sem, vmem_limit_bytes=VMEM_LIMIT)


def _dot(a, b):
    return jnp.dot(a, b, preferred_element_type=F32)


def _pack_bf16_pairs(x):
    bits = lax.bitcast_convert_type(x.astype(BF16).astype(F32), jnp.uint32)
    half = bits.shape[1] // 2
    packed = (bits[:, :half] >> 16) | (bits[:, half:] & jnp.uint32(0xFFFF0000))
    return lax.bitcast_convert_type(packed, jnp.int32)


def _unpack_bf16_pairs(words):
    words = lax.bitcast_convert_type(words, jnp.uint32)
    return jnp.concatenate(
        [lax.bitcast_convert_type(words << 16, F32),
         lax.bitcast_convert_type(words & jnp.uint32(0xFFFF0000), F32)], axis=1)


def _split_dot(a_f32, b_bf16):
    hi = a_f32.astype(BF16)
    lo = (a_f32 - hi.astype(F32)).astype(BF16)
    return _dot(hi, b_bf16) + _dot(lo, b_bf16)


def _ada_kernel(c_ref, w_ref, b_ref, o_ref):
    c = c_ref[...]
    ca = c * jax.nn.sigmoid(c)
    o_ref[0] = jnp.dot(ca, w_ref[0], precision=HIGHEST, preferred_element_type=F32) + b_ref[0]


def _ada_modulation(c, ada_w, ada_b):
    B, D = c.shape
    n_col = ada_w.shape[-1] // D
    return pl.pallas_call(
        _ada_kernel,
        out_shape=jax.ShapeDtypeStruct((DEPTH, B, n_col * D), F32),
        grid=(DEPTH, n_col),
        in_specs=[
            pl.BlockSpec((B, D), lambda l, j: (0, 0)),
            pl.BlockSpec((1, D, D), lambda l, j: (l, 0, j)),
            pl.BlockSpec((1, 1, D), lambda l, j: (l, 0, j)),
        ],
        out_specs=pl.BlockSpec((1, B, D), lambda l, j: (l, 0, j)),
        compiler_params=_cparams("parallel", "parallel"),
        name="ada_modulation",
    )(c, ada_w, ada_b.reshape(DEPTH, 1, n_col * D))


def _modulated_norm(x, gain, scale, shift):
    y = x * lax.rsqrt(jnp.mean(x * x, axis=-1, keepdims=True) + NORM_EPS)
    return y * gain * (1.0 + scale) + shift


def _rope(x, cos, sin_signed, first_half):
    w = x.shape[-1]
    partner = jnp.where(first_half, pltpu.roll(x, w - HEAD_DIM // 2, 1), pltpu.roll(x, HEAD_DIM // 2, 1))
    return x * cos + partner * sin_signed


HYB_COLS = 5 * 512 + 2 * LANES


def _hyb_in_kernel(*refs, n_pending):
    x_ref, pending_refs, refs = refs[0], refs[1:1 + n_pending], refs[1 + n_pending:]
    (gain_ref, sc_ref, sh_ref, w_ref, cos_ref, sin_ref, qn_ref, kn_ref, grp_ref,
     qt_ref, iqt_ref, bcu_ref, kv_ref, kvt_ref, ik_ref, iwt_ref) = refs[:16]
    x = _residual_tile(x_ref, pending_refs)
    if n_pending:
        refs[16][0] = x
    h = _modulated_norm(x, gain_ref[...], sc_ref[0], sh_ref[0])
    hb = h.astype(BF16)
    p_small = _dot(hb, w_ref[:, 0:2 * LANES])
    p_q = _dot(hb, w_ref[:, 2 * LANES:2 * LANES + 512])
    p_iq = _dot(hb, w_ref[:, 2 * LANES + 512:2 * LANES + 1024])
    cos = cos_ref[...]
    sin = sin_ref[...]
    lane = lax.broadcasted_iota(jnp.int32, (1, ATTN_WIDTH), 1)
    first_half = (lane % HEAD_DIM) < (HEAD_DIM // 2)
    fh128 = first_half[:, :LANES]
    lane128 = lane[:, :LANES]

    kv = p_small[:, :LANES]
    is_k = lane128 < HEAD_DIM
    kk = jnp.where(is_k, kv, 0.0)
    ms_k = jnp.sum(kk * kk, axis=-1, keepdims=True) * (1.0 / HEAD_DIM)
    kn = kv * lax.rsqrt(ms_k + NORM_EPS) * kn_ref[...]
    kr = _rope(kn, cos[:, :LANES], sin[:, :LANES], fh128)
    kv = jnp.where(is_k, kr, kv)
    kv_ref[0] = kv.astype(BF16)
    feat = lax.broadcasted_iota(jnp.int32, (LANES, 1), 0)
    kv_t = jnp.where(feat < HEAD_DIM, 1.0, kv.T).astype(BF16)
    for i in range(kvt_ref.shape[1]):
        kvt_ref[0, i] = kv_t[:, i * DSA_KEY_CHUNK:(i + 1) * DSA_KEY_CHUNK]

    sm = p_small[:, LANES:]
    ikr = _rope(sm, cos[:, :LANES], sin[:, :LANES], fh128)
    ik_ref[0] = jnp.where(is_k, ikr, 0.0).astype(BF16)
    iwt_ref[0] = sm.T

    ms = _split_dot(p_q * p_q, grp_ref[...]) * (1.0 / HEAD_DIM)
    q = p_q * lax.rsqrt(ms + NORM_EPS) * qn_ref[...]
    qt_ref[0] = (_rope(q, cos, sin, first_half) * (HEAD_DIM ** -0.5)).T.astype(BF16)
    iqt_ref[0] = (_rope(p_iq, cos, sin, first_half) * (IDX_DIM ** -0.5)).T.astype(BF16)

    bcu_ref[0] = _dot(hb, w_ref[:, 2 * LANES + 1024:])


def _hyb_in(x, pending, gain, sc, sh, w_pad, cos_t, sin_t, qn_t, kn_t, grp):
    B, S, D = x.shape
    tm = TOKEN_TILE
    row = lambda b, j: (b, j, 0)
    per_b = lambda b, j: (b, 0, 0)
    const2 = lambda b, j: (0, 0)
    tab = lambda b, j: (j, 0)
    col = lambda b, j: (b, 0, j)
    assert tm % DSA_KEY_CHUNK == 0
    x_out_shape = (jax.ShapeDtypeStruct((B, S, D), F32),) if pending else ()
    x_out_spec = (pl.BlockSpec((1, tm, D), row),) if pending else ()
    return pl.pallas_call(
        functools.partial(_hyb_in_kernel, n_pending=len(pending)),
        out_shape=(
            jax.ShapeDtypeStruct((B, 512, S), BF16),
            jax.ShapeDtypeStruct((B, 512, S), BF16),
            jax.ShapeDtypeStruct((B, S, 1536), F32),
            jax.ShapeDtypeStruct((B, S, LANES), BF16),
            jax.ShapeDtypeStruct((B, S // DSA_KEY_CHUNK, LANES, DSA_KEY_CHUNK), BF16),
            jax.ShapeDtypeStruct((B, S, LANES), BF16),
            jax.ShapeDtypeStruct((B, LANES, S), F32),
        ) + x_out_shape,
        grid=(B, S // tm),
        in_specs=[pl.BlockSpec((1, tm, D), row)] + (_pending_specs(D, tm) if pending else []) + [
            pl.BlockSpec((1, D), const2),
            pl.BlockSpec((1, 1, D), per_b),
            pl.BlockSpec((1, 1, D), per_b),
            pl.BlockSpec((D, HYB_COLS), const2),
            pl.BlockSpec((tm, 512), tab),
            pl.BlockSpec((tm, 512), tab),
            pl.BlockSpec((1, 512), const2),
            pl.BlockSpec((1, LANES), const2),
            pl.BlockSpec((512, 512), const2),
        ],
        out_specs=(
            pl.BlockSpec((1, 512, tm), col),
            pl.BlockSpec((1, 512, tm), col),
            pl.BlockSpec((1, tm, 1536), row),
            pl.BlockSpec((1, tm, LANES), row),
            pl.BlockSpec((1, tm // DSA_KEY_CHUNK, LANES, DSA_KEY_CHUNK), lambda b, j: (b, j, 0, 0)),
            pl.BlockSpec((1, tm, LANES), row),
            pl.BlockSpec((1, LANES, tm), col),
        ) + x_out_spec,
        compiler_params=_cparams("parallel", "parallel"),
        name="hybrid_in_proj",
    )(x, *pending, gain, sc, sh, w_pad, cos_t, sin_t, qn_t, kn_t, grp)


DSA_KEY_CHUNK = 512
DSA_SUM_ROWS = 16


def _fold8(x, op):
    parts = x.reshape(x.shape[0] // 8, 8, x.shape[1])
    while parts.shape[0] > 1:
        half = parts.shape[0] // 2
        assert parts.shape[0] == 2 * half
        parts = op(parts[:half], parts[half:])
    return parts[0]


def _col_reduce(x, op):
    t = _fold8(x, op)
    for shift in (4, 2, 1):
        t = op(t, pltpu.roll(t, shift, 0))
    return t[0:1, :]


def _dsa_t_kernel(qt_ref, iqt_ref, iwt_ref, kv_ref, kvt_ref, ik_ref, o_ref, sc_ref, bias_ref, acc_ref):
    CK = DSA_KEY_CHUNK
    QB = Q_BLOCK
    qb = pl.program_id(1)
    nk = (qb * QB + QB + CK - 1) // CK
    kf = float(INDEX_TOPK)
    qpos = qb * QB + lax.broadcasted_iota(jnp.int32, (1, QB), 1)
    krow = lax.broadcasted_iota(jnp.int32, (CK, 1), 0)
    w_idx = iwt_ref[0, IDX_DIM:IDX_DIM + IDX_HEADS, :] * (IDX_HEADS ** -0.5)

    def rows(c):
        return pl.ds(pl.multiple_of(c * CK, CK), CK)

    def heads_on_lanes(ref, width):
        return jnp.concatenate([ref[0, hd * width:(hd + 1) * width, :] for hd in range(ref.shape[1] // width)], axis=1)

    def head_lanes(hd):
        return slice(hd * QB, (hd + 1) * QB)

    iq_wide = heads_on_lanes(iqt_ref, IDX_DIM)
    w_wide = jnp.concatenate([w_idx[hd:hd + 1, :] for hd in range(IDX_HEADS)], axis=1)

    def score_chunk(c, carry):
        mx, mn = carry
        ikc = ik_ref[0, rows(c), :][:, :IDX_DIM]
        s_all = jnp.maximum(_dot(ikc, iq_wide), 0.0) * w_wide
        acc = s_all[:, head_lanes(0)]
        for hd in range(1, IDX_HEADS):
            acc = acc + s_all[:, head_lanes(hd)]
        causal = (c * CK + krow) <= qpos
        sc_ref[rows(c), :] = jnp.where(causal, acc, NEG_INF)
        mx = jnp.maximum(mx, _fold8(jnp.where(causal, acc, NEG_INF), jnp.maximum))
        mn = jnp.minimum(mn, _fold8(jnp.where(causal, acc, jnp.inf), jnp.minimum))
        return mx, mn

    mx8, mn8 = lax.fori_loop(0, nk, score_chunk,
                             (jnp.full((8, QB), NEG_INF, F32), jnp.full((8, QB), jnp.inf, F32)))
    row_max = jnp.max(mx8, axis=0, keepdims=True)
    row_min = jnp.min(mn8, axis=0, keepdims=True)

    def count(pred):
        def body(c, part):
            return part + _fold8(jnp.where(pred(sc_ref[rows(c), :]), 1.0, 0.0), jnp.add)
        part = lax.fori_loop(0, nk, body, jnp.zeros((8, QB), F32))
        return jnp.sum(part, axis=0, keepdims=True)

    @pl.when(qb * QB + QB <= INDEX_TOPK)
    def _():
        def body(c, carry):
            bias_ref[rows(c), :] = jnp.where(sc_ref[rows(c), :] > NEG_INF, 0.0, NEG_INF)
            return carry
        lax.fori_loop(0, nk, body, 0)

    @pl.when(qb * QB + QB > INDEX_TOPK)
    def _():
        def bisect(_, carry):
            lo, hi, c_lo, c_hi = carry
            mid = 0.5 * lo + 0.5 * jnp.minimum(hi, row_max)
            cnt = count(lambda x: x >= mid)
            ge = cnt >= kf
            return (jnp.where(ge, mid, lo), jnp.where(ge, hi, mid),
                    jnp.where(ge, cnt, c_lo), jnp.where(ge, c_hi, cnt))

        n_adm = (qpos + 1).astype(F32)
        lo, hi, c_lo, c_hi = lax.fori_loop(
            0, 18, bisect, (row_min, jnp.full((1, QB), jnp.inf, F32), n_adm, jnp.zeros((1, QB), F32)))

        def refine_cond(carry):
            it, _, _, _, done = carry
            return jnp.logical_and(it < nk * CK, jnp.min(done) < 0.5)

        def refine(carry):
            it, hi, c_hi, thr, done = carry

            def edges(c, part):
                up, dn = part
                x = sc_ref[rows(c), :]
                return (jnp.maximum(up, _fold8(jnp.where(x < hi, x, NEG_INF), jnp.maximum)),
                        jnp.minimum(dn, _fold8(jnp.where(x >= lo, x, jnp.inf), jnp.minimum)))

            up8, dn8 = lax.fori_loop(0, nk, edges,
                                     (jnp.full((8, QB), NEG_INF, F32), jnp.full((8, QB), jnp.inf, F32)))
            m_up = jnp.max(up8, axis=0, keepdims=True)
            m_dn = jnp.min(dn8, axis=0, keepdims=True)
            from_hi = c_hi == kf - 1.0
            from_lo = c_lo == kf
            thr = jnp.where(done > 0.5, thr, jnp.where(from_hi, m_up, m_dn))
            done = jnp.where(jnp.logical_or(from_hi, from_lo), 1.0, done)

            def step_down(args):
                hi, c_hi, thr, done = args
                open_ = done < 0.5
                cnt = count(lambda x: x >= m_up)
                hit = jnp.logical_and(open_, cnt >= kf)
                moved = jnp.logical_and(open_, cnt < kf)
                return (jnp.where(moved, m_up, hi), jnp.where(moved, cnt, c_hi), jnp.where(hit, m_up, thr),
                        jnp.where(hit, 1.0, done))

            hi, c_hi, thr, done = lax.cond(jnp.min(done) > 0.5, lambda args: args, step_down, (hi, c_hi, thr, done))
            return it + 1, hi, c_hi, thr, done

        _, _, _, thr, _ = lax.while_loop(
            refine_cond, refine, (jnp.int32(0), hi, c_hi, row_max, jnp.zeros((1, QB), F32)))

        def body(c, sel):
            keep = sc_ref[rows(c), :] >= thr
            bias_ref[rows(c), :] = jnp.where(keep, 0.0, NEG_INF)
            return sel + _fold8(jnp.where(keep, 1.0, 0.0), jnp.add)

        n_sel = jnp.sum(lax.fori_loop(0, nk, body, jnp.zeros((8, QB), F32)), axis=0, keepdims=True)

        @pl.when(jnp.max(n_sel) > kf + 0.5)
        def _():
            need = kf - count(lambda x: x > thr)
            r_i = lax.broadcasted_iota(jnp.int32, (CK, CK), 0)
            c_i = lax.broadcasted_iota(jnp.int32, (CK, CK), 1)
            lower = jnp.where(c_i <= r_i, 1.0, 0.0).astype(BF16)

            def body(c, seen):
                x = sc_ref[rows(c), :]
                eq = x == thr
                eq_f = jnp.where(eq, 1.0, 0.0)
                rank = _dot(lower, eq_f.astype(BF16)) + seen
                keep = jnp.logical_or(x > thr, jnp.logical_and(eq, rank <= need))
                bias_ref[rows(c), :] = jnp.where(keep, 0.0, NEG_INF)
                return seen + jnp.sum(eq_f, axis=0, keepdims=True)

            lax.fori_loop(0, nk, body, jnp.zeros((1, QB), F32))

    acc_ref[...] = jnp.zeros_like(acc_ref)
    q_wide = heads_on_lanes(qt_ref, HEAD_DIM)

    def attend_chunk(c, m_old):
        kc = kv_ref[0, rows(c), :][:, :HEAD_DIM]
        bias = bias_ref[rows(c), :]
        logits = _dot(kc, q_wide) + jnp.concatenate([bias] * ATTN_HEADS, axis=1)
        m_new = jnp.maximum(m_old, _col_reduce(logits, jnp.maximum))
        alpha = jnp.exp(m_old - m_new)
        p = jnp.exp(logits - m_new)
        acc_ref[...] = alpha * acc_ref[...] + _dot(kvt_ref[0, c, HEAD_DIM - DSA_SUM_ROWS:, :], p.astype(BF16))
        return m_new

    lax.fori_loop(0, nk, attend_chunk, jnp.full((1, ATTN_HEADS * QB), -1e30, F32))
    out_t = acc_ref[DSA_SUM_ROWS:, :] / acc_ref[0:1, :]
    o_ref[0] = jnp.concatenate([out_t[:, head_lanes(hd)] for hd in range(ATTN_HEADS)], axis=0).T


def _dsa_attention_t(qt, iqt, iwt, kv, kvt, ik):
    B, _, S = qt.shape
    col = lambda b, j: (b, 0, j)
    per_b = lambda b, j: (b, 0, 0)
    return pl.pallas_call(
        _dsa_t_kernel,
        out_shape=jax.ShapeDtypeStruct((B, S, ATTN_WIDTH), F32),
        grid=(B, S // Q_BLOCK),
        in_specs=[
            pl.BlockSpec((1, ATTN_WIDTH, Q_BLOCK), col),
            pl.BlockSpec((1, IDX_HEADS * IDX_DIM, Q_BLOCK), col),
            pl.BlockSpec((1, LANES, Q_BLOCK), col),
            pl.BlockSpec((1, S, LANES), per_b),
            pl.BlockSpec((1, S // DSA_KEY_CHUNK, LANES, DSA_KEY_CHUNK), lambda b, j: (b, 0, 0, 0)),
            pl.BlockSpec((1, S, LANES), per_b),
        ],
        out_specs=pl.BlockSpec((1, Q_BLOCK, ATTN_WIDTH), lambda b, j: (b, j, 0)),
        scratch_shapes=[
            pltpu.VMEM((S, Q_BLOCK), F32),
            pltpu.VMEM((S, Q_BLOCK), F32),
            pltpu.VMEM((DSA_SUM_ROWS + HEAD_DIM, ATTN_HEADS * Q_BLOCK), F32),
        ],
        compiler_params=_cparams("parallel", "parallel"),
        name="dsa_attention",
    )(qt, iqt, iwt, kv, kvt, ik)


def _hyb_out_kernel(ya_ref, bcu_ref, halo_ref, cw_ref, w_ref, x_ref, g_ref,
                    gain2_ref, sc2_ref, sh2_ref, wr_ref, br_ref, o_ref, h_ref, r_ref, rt_ref, cnt_ref, run_ref):
    j = pl.program_id(1)
    tm = ya_ref.shape[1]
    bcu = bcu_ref[0]
    bg = bcu[:, 0:512]
    z = bcu[:, 512:1024] * bcu[:, 1024:1536]
    halo = halo_ref[0]
    zh = halo[:, 512:1024] * halo[:, 1024:1536]
    zh = jnp.where(j > 0, zh, 0.0)
    row = lax.broadcasted_iota(jnp.int32, (tm, 1), 0)
    z1 = jnp.where(row >= 1, pltpu.roll(z, 1, 0), zh[7:8, :])
    z2 = jnp.where(row >= 2, pltpu.roll(z, 2, 0), jnp.where(row == 1, zh[7:8, :], zh[6:7, :]))
    cw = cw_ref[...]
    y_conv = bg * (z2 * cw[0:1, :] + z1 * cw[1:2, :] + z * cw[2:3, :])
    y = _dot(ya_ref[0].astype(BF16), w_ref[0:512, :]) + _dot(y_conv.astype(BF16), w_ref[512:1024, :])
    x_new = x_ref[0] + g_ref[0] * y
    o_ref[0] = x_new
    _route_tile(x_new, gain2_ref, sc2_ref, sh2_ref, wr_ref, br_ref, h_ref, r_ref, rt_ref, cnt_ref, run_ref)


def _hyb_out(y_attn, bcu, conv_w, w_out_bf, x, g1, route_args):
    B, S, D = x.shape
    tm = OUT_TILE
    row = lambda b, j: (b, j, 0)
    per_b = lambda b, j: (b, 0, 0)
    const2 = lambda b, j: (0, 0)
    halo = lambda b, j: (b, jnp.maximum(j * (tm // 8) - 1, 0), 0)
    r_in, r_shape, r_out, r_scratch = _route_specs(B, S, D, tm)
    return pl.pallas_call(
        _hyb_out_kernel,
        out_shape=(jax.ShapeDtypeStruct((B, S, D), F32),) + r_shape,
        grid=(B, S // tm),
        in_specs=[
            pl.BlockSpec((1, tm, 512), row),
            pl.BlockSpec((1, tm, 1536), row),
            pl.BlockSpec((1, 8, 1536), halo),
            pl.BlockSpec((CONV_K, CONV_WIDTH), const2),
            pl.BlockSpec((D, D), const2),
            pl.BlockSpec((1, tm, D), row),
            pl.BlockSpec((1, 1, D), per_b),
        ] + r_in,
        out_specs=(pl.BlockSpec((1, tm, D), row),) + r_out,
        scratch_shapes=r_scratch,
        compiler_params=_cparams("arbitrary", "arbitrary"),
        name="hybrid_out_proj",
    )(y_attn, bcu, bcu, conv_w, w_out_bf, x, g1, *route_args)


ML_COLS = 512 + 512 + 1024 + 1024 + LANES


def _ml_in_kernel(x_ref, g2_ref, y0_ref, y1_ref, r_ref, gain_ref, sc_ref, sh_ref, w_ref, bias_ref,
                  q_ref, k_ref, v_ref, og_ref, gt_ref, gtt_ref, xo_ref):
    x = _residual_tile(x_ref, (g2_ref, y0_ref, y1_ref, r_ref))
    xo_ref[0] = x
    h = _modulated_norm(x, gain_ref[...], sc_ref[0], sh_ref[0])
    hb = h.astype(BF16)
    gates = _dot(hb, w_ref[:, 0:LANES]) + bias_ref[...]
    gates_t = gates.T[:2 * ML_HEADS, :]
    keys_t = _dot(hb, w_ref[:, LANES:LANES + 512]).T.astype(BF16)
    for i in range(gt_ref.shape[1]):
        span = slice(i * ML_CHUNK, (i + 1) * ML_CHUNK)
        gt_ref[0, i] = gates[span, :]
        gtt_ref[0, i] = gates_t[:, span]
        k_ref[0, i] = keys_t[:, span]
    q_ref[0] = (_dot(hb, w_ref[:, LANES + 512:LANES + 1024]) * (ML_QK_DIM ** -0.5)).astype(BF16)
    v_ref[0] = _dot(hb, w_ref[:, LANES + 1024:LANES + 2048]).astype(BF16)
    og_ref[0] = _dot(hb, w_ref[:, LANES + 2048:])


def _ml_in(x, pending, gain, sc, sh, w_pad, gate_bias):
    B, S, D = x.shape
    tm = TOKEN_TILE
    L = ML_CHUNK
    assert tm % L == 0
    per_tile = tm // L
    chunked = lambda b, j: (b, j, 0, 0)
    row = lambda b, j: (b, j, 0)
    per_b = lambda b, j: (b, 0, 0)
    const2 = lambda b, j: (0, 0)
    return pl.pallas_call(
        _ml_in_kernel,
        out_shape=(
            jax.ShapeDtypeStruct((B, S, 512), BF16),
            jax.ShapeDtypeStruct((B, S // L, 512, L), BF16),
            jax.ShapeDtypeStruct((B, S, 1024), BF16),
            jax.ShapeDtypeStruct((B, S, 1024), F32),
            jax.ShapeDtypeStruct((B, S // L, L, LANES), F32),
            jax.ShapeDtypeStruct((B, S // L, 2 * ML_HEADS, L), F32),
            jax.ShapeDtypeStruct((B, S, D), F32),
        ),
        grid=(B, S // tm),
        in_specs=[pl.BlockSpec((1, tm, D), row)] + _pending_specs(D, tm) + [
            pl.BlockSpec((1, D), const2),
            pl.BlockSpec((1, 1, D), per_b),
            pl.BlockSpec((1, 1, D), per_b),
            pl.BlockSpec((D, ML_COLS), const2),
            pl.BlockSpec((1, LANES), const2),
        ],
        out_specs=(
            pl.BlockSpec((1, tm, 512), row),
            pl.BlockSpec((1, per_tile, 512, L), chunked),
            pl.BlockSpec((1, tm, 1024), row),
            pl.BlockSpec((1, tm, 1024), row),
            pl.BlockSpec((1, per_tile, L, LANES), chunked),
            pl.BlockSpec((1, per_tile, 2 * ML_HEADS, L), chunked),
            pl.BlockSpec((1, tm, D), row),
        ),
        compiler_params=_cparams("parallel", "parallel"),
        name="mlstm_in_proj",
    )(x, *pending, gain, sc, sh, w_pad, gate_bias)


def _log_sigmoid(f):
    return jnp.minimum(f, 0.0) - jnp.log1p(jnp.exp(-jnp.abs(f)))


def _split3(x):
    a = x.astype(BF16)
    r = x - a.astype(F32)
    b = r.astype(BF16)
    c = (r - b.astype(F32)).astype(BF16)
    return a, b, c


def _twice(a):
    return jnp.concatenate([a, a], axis=1)


def _over_lanes(a, width):
    return jnp.concatenate([a] * (width // LANES), axis=1)


def _mlstm_kernel(q_ref, kt_ref, v_ref, grow_ref, gcol_ref, gain_ref, o_ref, c_ref, m_ref):
    L = ML_CHUNK
    HP = ML_STEP_HEADS
    S = q_ref.shape[1]
    c_ref[...] = jnp.zeros_like(c_ref)
    m_ref[...] = jnp.zeros_like(m_ref)

    def chunk(c, carry):
        r0 = pl.multiple_of(c * L, L)
        r_i = lax.broadcasted_iota(jnp.int32, (L, L), 0)
        c_i = lax.broadcasted_iota(jnp.int32, (L, L), 1)
        tril = c_i <= r_i
        lower = jnp.where(tril, 1.0, 0.0).astype(BF16)
        upper = jnp.where(r_i <= c_i, 1.0, 0.0).astype(BF16)
        e_r = lax.broadcasted_iota(jnp.int32, (LANES, HP * LANES), 0)
        e_c = lax.broadcasted_iota(jnp.int32, (LANES, HP * LANES), 1)
        pick = jnp.where(e_r == HP + e_c // LANES, 1.0, 0.0).astype(BF16)
        rows = grow_ref[0, 0, c]
        cols = gcol_ref[0, 0, c]
        b_rows = sum(_dot(p, upper) for p in _split3(_log_sigmoid(rows)))
        b_cols = sum(_dot(lower, p) for p in _split3(_log_sigmoid(cols)))
        b_colr = sum(_dot(p, pick) for p in _split3(b_cols))
        lane = lax.broadcasted_iota(jnp.int32, (1, L), 1)
        b_last_all = jnp.sum(jnp.where(lane == L - 1, b_rows, 0.0), axis=-1, keepdims=True)
        ones_v = jnp.ones((L, ML_V_DIM), BF16)
        for hh in range(HP):
            q = q_ref[0, pl.ds(r0, L), hh * ML_QK_DIM:(hh + 1) * ML_QK_DIM]
            kt = kt_ref[0, c, hh * ML_QK_DIM:(hh + 1) * ML_QK_DIM, :]
            v = v_ref[0, pl.ds(r0, L), hh * ML_V_DIM:(hh + 1) * ML_V_DIM]
            vx = jnp.concatenate([v, ones_v], axis=1)
            i_row = rows[hh:hh + 1, :]
            b_row = b_rows[HP + hh:HP + hh + 1, :]
            b_last = b_last_all[HP + hh:HP + hh + 1, :]
            b_col = b_colr[:, hh * LANES:(hh + 1) * LANES]
            m_prev = m_ref[hh]
            ctn = c_ref[hh]

            dmat = jnp.where(tril, _over_lanes(b_col, L) - b_row + i_row, NEG_INF)
            inter = b_col + m_prev
            m_t = jnp.maximum(inter, jnp.max(dmat, axis=-1, keepdims=True))
            w_intra = jnp.exp(dmat - _over_lanes(m_t, L))
            w_inter = jnp.exp(inter - m_t)
            intra = (w_intra * _dot(q, kt)).astype(BF16)
            tot = _twice(w_inter) * _dot(q, ctn.astype(BF16)) + _dot(intra, vx)
            num = tot[:, :ML_V_DIM]
            den = tot[:, ML_V_DIM:]
            hc = num / jnp.maximum(jnp.abs(den), jnp.exp(-m_t))
            y = hc * lax.rsqrt(jnp.mean(hc * hc, axis=-1, keepdims=True) + NORM_EPS)
            o_ref[0, pl.ds(r0, L), hh * ML_V_DIM:(hh + 1) * ML_V_DIM] = (
                y * gain_ref[:, hh * ML_V_DIM:(hh + 1) * ML_V_DIM])

            g_row = b_last - b_row + i_row
            m_new = jnp.maximum(b_last + m_prev, jnp.max(g_row, axis=-1, keepdims=True))
            decay = jnp.exp(b_last + m_prev - m_new)
            kw = (kt.astype(F32) * jnp.exp(g_row - _over_lanes(m_new, L))).astype(BF16)
            c_ref[hh] = _twice(decay) * ctn + _dot(kw, vx)
            m_ref[hh] = m_new
        return carry

    lax.fori_loop(0, S // L, chunk, 0)


def _mlstm(q, kt, v, g_rows, g_cols, out_gain):
    B, S, _ = q.shape
    assert ML_CHUNK % LANES == 0
    nc = S // ML_CHUNK
    hp = ML_STEP_HEADS
    return pl.pallas_call(
        _mlstm_kernel,
        out_shape=jax.ShapeDtypeStruct((B, S, ML_HEADS * ML_V_DIM), F32),
        grid=(B, ML_HEADS // hp),
        in_specs=[
            pl.BlockSpec((1, S, hp * ML_QK_DIM), lambda b, p: (b, 0, p)),
            pl.BlockSpec((1, nc, hp * ML_QK_DIM, ML_CHUNK), lambda b, p: (b, 0, p, 0)),
            pl.BlockSpec((1, S, hp * ML_V_DIM), lambda b, p: (b, 0, p)),
            pl.BlockSpec((1, 1, nc, 2 * hp, ML_CHUNK), lambda b, p: (b, p, 0, 0, 0)),
            pl.BlockSpec((1, 1, nc, ML_CHUNK, LANES), lambda b, p: (b, p, 0, 0, 0)),
            pl.BlockSpec((1, hp * ML_V_DIM), lambda b, p: (0, p)),
        ],
        out_specs=pl.BlockSpec((1, S, hp * ML_V_DIM), lambda b, p: (b, 0, p)),
        scratch_shapes=[
            pltpu.VMEM((hp, ML_QK_DIM, ML_V_DIM + LANES), F32),
            pltpu.VMEM((hp, 1, LANES), F32),
        ],
        compiler_params=_cparams("parallel", "parallel"),
        name="mlstm_chunkwise",
    )(q, kt, v, g_rows, g_cols, out_gain)


def _ml_out_kernel(hh_ref, og_ref, w_ref, x_ref, g_ref,
                   gain2_ref, sc2_ref, sh2_ref, wr_ref, br_ref, o_ref, h_ref, r_ref, rt_ref, cnt_ref, run_ref):
    a = jax.nn.sigmoid(og_ref[0]) * hh_ref[0]
    x_new = x_ref[0] + g_ref[0] * _dot(a.astype(BF16), w_ref[...])
    o_ref[0] = x_new
    _route_tile(x_new, gain2_ref, sc2_ref, sh2_ref, wr_ref, br_ref, h_ref, r_ref, rt_ref, cnt_ref, run_ref)


def _ml_out(hh, og, w_out_bf, x, g1, route_args):
    B, S, D = x.shape
    tm = OUT_TILE
    row = lambda b, j: (b, j, 0)
    per_b = lambda b, j: (b, 0, 0)
    const2 = lambda b, j: (0, 0)
    r_in, r_shape, r_out, r_scratch = _route_specs(B, S, D, tm)
    return pl.pallas_call(
        _ml_out_kernel,
        out_shape=(jax.ShapeDtypeStruct((B, S, D), F32),) + r_shape,
        grid=(B, S // tm),
        in_specs=[
            pl.BlockSpec((1, tm, D), row),
            pl.BlockSpec((1, tm, D), row),
            pl.BlockSpec((D, D), const2),
            pl.BlockSpec((1, tm, D), row),
            pl.BlockSpec((1, 1, D), per_b),
        ] + r_in,
        out_specs=(pl.BlockSpec((1, tm, D), row),) + r_out,
        scratch_shapes=r_scratch,
        compiler_params=_cparams("arbitrary", "arbitrary"),
        name="mlstm_out_proj",
    )(hh, og, w_out_bf, x, g1, *route_args)


def _first_argmax(x, lane, width):
    mx = jnp.max(x, axis=-1, keepdims=True)
    idx = jnp.min(jnp.where(x == mx, lane, width), axis=-1, keepdims=True)
    return mx, idx


def _route_tile(x, gain_ref, sc_ref, sh_ref, w_ref, b_ref, h_ref, r_ref, rt_ref, cnt_ref, run_ref):
    tm = x.shape[0]

    @pl.when(jnp.logical_and(pl.program_id(0) == 0, pl.program_id(1) == 0))
    def _():
        run_ref[...] = jnp.zeros_like(run_ref)

    h = _modulated_norm(x, gain_ref[...], sc_ref[0], sh_ref[0])
    h_ref[0] = _pack_bf16_pairs(h)
    logits = _dot(h.astype(BF16), w_ref[...]) + b_ref[...]
    lane = lax.broadcasted_iota(jnp.int32, (1, LANES), 1)

    def pick(lgt):
        lg = jnp.where(lane < N_GROUPS, lgt, NEG_INF)
        g_max, g_sel = _first_argmax(lg, lane, LANES)
        pg = 1.0 / jnp.sum(jnp.exp(lg - g_max), axis=-1, keepdims=True)
        e_lane = lane - N_GROUPS
        in_grp = jnp.logical_and(e_lane >= g_sel * EXPERTS_PER_GROUP, e_lane < (g_sel + 1) * EXPERTS_PER_GROUP)
        le = jnp.where(in_grp, lgt, NEG_INF)
        v1, i1 = _first_argmax(le, lane, LANES)
        v2, i2 = _first_argmax(jnp.where(lane == i1, NEG_INF, le), lane, LANES)
        ratio = jnp.exp(v2 - v1)
        return i1 - N_GROUPS, i2 - N_GROUPS, pg / (1.0 + ratio), pg * ratio / (1.0 + ratio)

    e1, e2, w1, w2 = pick(logits)
    hot1 = lane == e1
    hot2 = lane == e2
    onehot = jnp.where(jnp.logical_or(hot1, hot2), 1.0, 0.0)
    r_i = lax.broadcasted_iota(jnp.int32, (tm, tm), 0)
    c_i = lax.broadcasted_iota(jnp.int32, (tm, tm), 1)
    before = jnp.where(c_i < r_i, 1.0, 0.0).astype(BF16)
    seen = _dot(before, onehot.astype(BF16)) + run_ref[...]
    rank1 = jnp.sum(jnp.where(hot1, seen, 0.0), axis=-1, keepdims=True)
    rank2 = jnp.sum(jnp.where(hot2, seen, 0.0), axis=-1, keepdims=True)
    run_ref[...] = run_ref[...] + jnp.sum(onehot, axis=0, keepdims=True)
    cnt_ref[...] = run_ref[...]

    out = jnp.where(lane == 0, e1.astype(F32), 0.0)
    out = jnp.where(lane == 1, e2.astype(F32), out)
    out = jnp.where(lane == 2, w1, out)
    out = jnp.where(lane == 3, w2, out)
    out = jnp.where(lane == 4, rank1, out)
    out = jnp.where(lane == 5, rank2, out)
    r_ref[0] = out
    rt_ref[0] = out.T[:8, :]


def _route_specs(B, S, D, tm):
    row = lambda b, j: (b, j, 0)
    per_b = lambda b, j: (b, 0, 0)
    const2 = lambda b, j: (0, 0)
    in_specs = [
        pl.BlockSpec((1, D), const2),
        pl.BlockSpec((1, 1, D), per_b),
        pl.BlockSpec((1, 1, D), per_b),
        pl.BlockSpec((D, LANES), const2),
        pl.BlockSpec((1, LANES), const2),
    ]
    out_shape = (
        jax.ShapeDtypeStruct((B, S, D // 2), jnp.int32),
        jax.ShapeDtypeStruct((B, S, LANES), F32),
        jax.ShapeDtypeStruct((B, 8, S), F32),
        jax.ShapeDtypeStruct((1, LANES), F32),
    )
    out_specs = (
        pl.BlockSpec((1, tm, D // 2), row),
        pl.BlockSpec((1, tm, LANES), row),
        pl.BlockSpec((1, 8, tm), lambda b, j: (b, 0, j)),
        pl.BlockSpec((1, LANES), const2),
    )
    return in_specs, out_shape, out_specs, [pltpu.VMEM((1, LANES), F32)]


def _experts_kernel(blk_e_ref, n_used_ref, next_e_ref, x_ref, wg_hbm, wu_hbm, wd_hbm, o_ref,
                    wg_f, wu_f, wd_f, wg_s, wu_s, wd_s, sem, *, layer):
    i = pl.program_id(0)
    used = i < n_used_ref[0]
    e = blk_e_ref[i]
    new_expert = jnp.logical_or(i == 0, e != blk_e_ref[jnp.maximum(i - 1, 0)])

    def fetch(expert):
        return (pltpu.make_async_copy(wg_hbm.at[layer, expert], wg_f, sem.at[0]),
                pltpu.make_async_copy(wu_hbm.at[layer, expert], wu_f, sem.at[1]),
                pltpu.make_async_copy(wd_hbm.at[layer, expert], wd_f, sem.at[2]))

    @pl.when(i == 0)
    def _():
        for cp in fetch(e):
            cp.start()

    @pl.when(jnp.logical_and(used, new_expert))
    def _():
        for cp in fetch(e):
            cp.wait()
        wg_s[...] = wg_f[...].astype(BF16)
        wu_s[...] = wu_f[...].astype(BF16)
        wd_s[...] = wd_f[...].astype(BF16)

        @pl.when(next_e_ref[i] >= 0)
        def _():
            for cp in fetch(next_e_ref[i]):
                cp.start()

    @pl.when(used)
    def _():
        x = _unpack_bf16_pairs(x_ref[...]).astype(BF16)
        a = _dot(x, wg_s[...])
        u = _dot(x, wu_s[...])
        act = a * jax.nn.sigmoid(a) * u
        o_ref[...] = _pack_bf16_pairs(_dot(act.astype(BF16), wd_s[...]))

    @pl.when(i >= n_used_ref[0])
    def _():
        o_ref[...] = jnp.zeros_like(o_ref)


def _experts(layer, blk_e, n_used, next_e, xs, w_gate, w_up, w_down):
    R = xs.shape[0]
    D = 2 * xs.shape[1]
    n_blk = R // MOE_BLOCK
    rows = lambda i, be, nu, ne: (i, 0)
    grid_spec = pltpu.PrefetchScalarGridSpec(
        num_scalar_prefetch=3,
        grid=(n_blk,),
        in_specs=[
            pl.BlockSpec((MOE_BLOCK, D // 2), rows),
            pl.BlockSpec(memory_space=pl.ANY),
            pl.BlockSpec(memory_space=pl.ANY),
            pl.BlockSpec(memory_space=pl.ANY),
        ],
        out_specs=pl.BlockSpec((MOE_BLOCK, D // 2), rows),
        scratch_shapes=[
            pltpu.VMEM((D, D_EXPERT), F32),
            pltpu.VMEM((D, D_EXPERT), F32),
            pltpu.VMEM((D_EXPERT, D), F32),
            pltpu.VMEM((D, D_EXPERT), BF16),
            pltpu.VMEM((D, D_EXPERT), BF16),
            pltpu.VMEM((D_EXPERT, D), BF16),
            pltpu.SemaphoreType.DMA((3,)),
        ],
    )
    return pl.pallas_call(
        functools.partial(_experts_kernel, layer=layer),
        out_shape=jax.ShapeDtypeStruct((R, D // 2), jnp.int32),
        grid_spec=grid_spec,
        compiler_params=_cparams("arbitrary"),
        name="moe_experts",
    )(blk_e, n_used, next_e, xs, w_gate, w_up, w_down)


def _residual_tile(x_ref, pending_refs):
    if not pending_refs:
        return x_ref[0]
    g_ref, y0_ref, y1_ref, r_ref = pending_refs
    r = r_ref[0]
    y = _unpack_bf16_pairs(y0_ref[0, 0]) * r[:, 2:3] + _unpack_bf16_pairs(y1_ref[0, 0]) * r[:, 3:4]
    return x_ref[0] + g_ref[0] * y


def _pending_specs(D, tm):
    return [
        pl.BlockSpec((1, 1, D), lambda b, j: (b, 0, 0)),
        pl.BlockSpec((1, 1, tm, D // 2), lambda b, j: (0, b, j, 0)),
        pl.BlockSpec((1, 1, tm, D // 2), lambda b, j: (1, b, j, 0)),
        pl.BlockSpec((1, tm, LANES), lambda b, j: (b, j, 0)),
    ]


def _combine_kernel(x_ref, g_ref, y0_ref, y1_ref, r_ref, o_ref):
    o_ref[0] = _residual_tile(x_ref, (g_ref, y0_ref, y1_ref, r_ref))


def _combine(x, pending):
    B, S, D = x.shape
    tm = OUT_TILE
    row = lambda b, j: (b, j, 0)
    return pl.pallas_call(
        _combine_kernel,
        out_shape=jax.ShapeDtypeStruct((B, S, D), F32),
        grid=(B, S // tm),
        in_specs=[pl.BlockSpec((1, tm, D), row)] + _pending_specs(D, tm),
        out_specs=pl.BlockSpec((1, tm, D), row),
        compiler_params=_cparams("parallel", "parallel"),
        name="moe_combine",
    )(x, *pending)


SC_CORES = 2
SC_SUBCORES = 16
SC_WORKERS = SC_CORES * SC_SUBCORES
SC_CHUNK = 64


def _sc_mesh():
    return plsc.VectorSubcoreMesh(core_axis_name="c", subcore_axis_name="s",
                                  num_cores=SC_CORES, num_subcores=SC_SUBCORES)


def _sc_scatter_rows(src, idx, n_out):
    T, W = src.shape
    per_w = T // SC_WORKERS
    nch = per_w // SC_CHUNK
    idx4 = idx.reshape(TOP_K, SC_WORKERS, nch, SC_CHUNK)

    @functools.partial(
        pl.kernel, mesh=_sc_mesh(),
        out_type=jax.ShapeDtypeStruct((n_out, W), src.dtype),
        scratch_types=[
            pltpu.VMEM((TOP_K, nch, SC_CHUNK), jnp.int32),
            pltpu.VMEM((2, SC_CHUNK, W), src.dtype),
            pltpu.SemaphoreType.DMA((2,)),
            pltpu.SemaphoreType.DMA((2 * TOP_K,)),
        ],
        name="sc_scatter_rows",
    )
    def body(src_hbm, idx_hbm, out_hbm, idx_v, rows_v, load_sem, scatter_sem):
        wid = lax.axis_index("s") * SC_CORES + lax.axis_index("c")
        for s in range(TOP_K):
            pltpu.sync_copy(idx_hbm.at[s, wid], idx_v.at[s])

        def load(j, b):
            return pltpu.make_async_copy(
                src_hbm.at[pl.ds(wid * per_w + j * SC_CHUNK, SC_CHUNK)], rows_v.at[b], load_sem.at[b])

        def scatter(s, j, b):
            return pltpu.make_async_copy(rows_v.at[b], out_hbm.at[idx_v.at[s, j]], scatter_sem.at[2 * s + b])

        load(0, 0).start()

        @pl.loop(0, nch, step=2)
        def _(i):
            for b in range(2):
                j = i + b

                @pl.when(j >= 1)
                def _():
                    for s in range(TOP_K):
                        scatter(s, j - 1, 1 - b).wait()

                @pl.when(j + 1 < nch)
                def _():
                    load(j + 1, 1 - b).start()

                load(j, b).wait()
                for s in range(TOP_K):
                    scatter(s, j, b).start()

        for s in range(TOP_K):
            scatter(s, nch - 1, (nch - 1) % 2).wait()

    assert nch % 2 == 0
    return body(src, idx4)


def _sc_gather_rows(table, idx):
    N = idx.shape[0]
    W = table.shape[1]
    per_w = N // SC_WORKERS
    nch = per_w // SC_CHUNK
    idx3 = idx.reshape(SC_WORKERS, nch, SC_CHUNK)

    @functools.partial(
        pl.kernel, mesh=_sc_mesh(),
        out_type=jax.ShapeDtypeStruct((N, W), table.dtype),
        scratch_types=[
            pltpu.VMEM((nch, SC_CHUNK), jnp.int32),
            pltpu.VMEM((2, SC_CHUNK, W), table.dtype),
            pltpu.SemaphoreType.DMA((2,)),
            pltpu.SemaphoreType.DMA((2,)),
        ],
        name="sc_gather_rows",
    )
    def body(table_hbm, idx_hbm, out_hbm, idx_v, rows_v, gather_sem, write_sem):
        wid = lax.axis_index("s") * SC_CORES + lax.axis_index("c")
        pltpu.sync_copy(idx_hbm.at[wid], idx_v)

        def gather(j, b):
            return pltpu.make_async_copy(table_hbm.at[idx_v.at[j]], rows_v.at[b], gather_sem.at[b])

        def write(j, b):
            return pltpu.make_async_copy(
                rows_v.at[b], out_hbm.at[pl.ds(wid * per_w + j * SC_CHUNK, SC_CHUNK)], write_sem.at[b])

        gather(0, 0).start()

        @pl.loop(0, nch, step=2)
        def _(i):
            for b in range(2):
                j = i + b

                @pl.when(j >= 1)
                def _():
                    write(j - 1, 1 - b).wait()

                @pl.when(j + 1 < nch)
                def _():
                    gather(j + 1, 1 - b).start()

                gather(j, b).wait()
                write(j, b).start()

        write(nch - 1, (nch - 1) % 2).wait()

    assert nch % 2 == 0
    return body(table, idx3)


def _moe_dispatch(route_t, counts, T):
    A = T * TOP_K
    counts = counts[0, :N_EXPERTS].astype(jnp.int32)
    blocks_per = (counts + MOE_BLOCK - 1) // MOE_BLOCK
    block_end = jnp.cumsum(blocks_per)
    block_start = block_end - blocks_per
    expert = jnp.swapaxes(route_t[:, :TOP_K, :], 0, 1).reshape(TOP_K, T).astype(jnp.int32)
    rank = jnp.swapaxes(route_t[:, 4:4 + TOP_K, :], 0, 1).reshape(TOP_K, T).astype(jnp.int32)
    onehot = expert[None] == jnp.arange(N_EXPERTS, dtype=jnp.int32)[:, None, None]
    start = jnp.sum(jnp.where(onehot, block_start[:, None, None], 0), axis=0)
    dest = start * MOE_BLOCK + rank
    n_blk = -(-A // MOE_BLOCK) + N_EXPERTS
    blk = jnp.arange(n_blk, dtype=jnp.int32)
    blk_e = jnp.minimum(jnp.sum(blk[:, None] >= block_end[None, :], axis=-1), N_EXPERTS - 1).astype(jnp.int32)
    n_used = block_end[-1]
    first = jnp.logical_and(blk < n_used, jnp.logical_or(blk == 0, blk_e != jnp.roll(blk_e, 1)))
    first_pos = jnp.where(first, blk, n_blk)
    next_pos = jnp.concatenate([lax.cummin(first_pos, axis=0, reverse=True)[1:], jnp.full((1,), n_blk, jnp.int32)])
    next_e = jnp.where(next_pos < n_blk, blk_e[jnp.minimum(next_pos, n_blk - 1)], -1).astype(jnp.int32)
    return dest, n_blk * MOE_BLOCK, blk_e, n_used.reshape(1).astype(jnp.int32), next_e


def _rope_tables(S):
    inv = 1.0 / (ROPE_THETA ** (jnp.arange(0, HEAD_DIM, 2, dtype=F32) / HEAD_DIM))
    ang = jnp.arange(S, dtype=F32)[:, None] * inv[None, :]
    cos, sin = jnp.cos(ang), jnp.sin(ang)
    cos_h = jnp.concatenate([cos, cos], axis=-1)
    sin_h = jnp.concatenate([-sin, sin], axis=-1)
    return jnp.tile(cos_h, (1, ATTN_HEADS)), jnp.tile(sin_h, (1, ATTN_HEADS))


def _pad_cols(w, width):
    return jnp.pad(w, ((0, 0), (0, width - w.shape[1])))


def kernel(x, c, ada_w, ada_b, norm_mix, norm_ffn, hy_w_in, hy_q_norm, hy_k_norm, hy_conv_w, hy_w_out, ml_w_in, ml_b_gates, ml_out_norm, ml_w_out, moe_w_group, moe_b_group, moe_w_expert, moe_b_expert, moe_w_gate, moe_w_up, moe_w_down):
    B, S, D = x.shape
    T = B * S
    cos_t, sin_t = _rope_tables(S)
    mod = _ada_modulation(c, ada_w, ada_b).reshape(DEPTH, B, 6, 1, D)
    r_i = np.arange(ATTN_WIDTH)
    grp = jnp.asarray((r_i[:, None] // HEAD_DIM) == (r_i[None, :] // HEAD_DIM), dtype=BF16)

    pending = ()
    for l in range(DEPTH):
        sh1, sc1, g1, sh2, sc2, g2 = [mod[l, :, i] for i in range(6)]
        gain1 = norm_mix[l].reshape(1, D)
        w_r = _pad_cols(jnp.concatenate([moe_w_group[l], moe_w_expert[l]], axis=1), LANES).astype(BF16)
        b_r = jnp.pad(jnp.concatenate([moe_b_group[l], moe_b_expert[l]]), (0, LANES - N_GROUPS - N_EXPERTS))
        route_args = (norm_ffn[l].reshape(1, D), sc2, sh2, w_r, b_r.reshape(1, LANES))
        j = l // 2
        if l % 2 == 0:
            w = hy_w_in[j]
            o = np.cumsum((0,) + (ATTN_WIDTH, HEAD_DIM, HEAD_DIM, IDX_HEADS * IDX_DIM, IDX_DIM, IDX_HEADS,
                                  CONV_WIDTH, CONV_WIDTH, CONV_WIDTH))
            wq, wk, wv, wiq, wik, wiw, wbg, wcg, wu = [w[:, o[i]:o[i + 1]] for i in range(9)]
            w_pad = jnp.concatenate(
                [wk, wv, _pad_cols(jnp.concatenate([wik, wiw], axis=1), LANES), wq, wiq, wbg, wcg, wu],
                axis=1).astype(BF16)
            qn_t = jnp.tile(hy_q_norm[j], ATTN_HEADS).reshape(1, ATTN_WIDTH)
            kn_t = jnp.tile(hy_k_norm[j], LANES // HEAD_DIM).reshape(1, LANES)
            outs = _hyb_in(x, pending, gain1, sc1, sh1, w_pad, cos_t, sin_t, qn_t, kn_t, grp)
            qt, iqt, bcu, kv, kvt, ik, iwt = outs[:7]
            if pending:
                x = outs[7]
            y_attn = _dsa_attention_t(qt, iqt, iwt, kv, kvt, ik)
            x, h2, route, route_t, counts =_hyb_out(y_attn, bcu, hy_conv_w[j], hy_w_out[j].astype(BF16), x, g1, route_args)
        else:
            w = ml_w_in[j]
            hq = ML_HEADS * ML_QK_DIM
            hv = ML_HEADS * ML_V_DIM
            wq, wk, wv = w[:, :hq], w[:, hq:2 * hq], w[:, 2 * hq:2 * hq + hv]
            wg = w[:, 2 * hq + hv:2 * hq + hv + 2 * ML_HEADS]
            wo = w[:, 2 * hq + hv + 2 * ML_HEADS:]
            w_pad = jnp.concatenate([_pad_cols(wg, LANES), wk, wq, wv, wo], axis=1).astype(BF16)
            gate_bias = jnp.pad(ml_b_gates[j], (0, LANES - 2 * ML_HEADS)).reshape(1, LANES)
            q, k, v, og, g_cols, g_rows, x = _ml_in(x, pending, gain1, sc1, sh1, w_pad, gate_bias)
            assert ML_STEP_HEADS == ML_HEADS
            hh = _mlstm(q, k, v, g_rows[:, None], g_cols[:, None], ml_out_norm[j].reshape(1, hv))
            x, h2, route, route_t, counts =_ml_out(hh, og, ml_w_out[j].astype(BF16), x, g1, route_args)

        dest, n_rows, blk_e, n_used, next_e = _moe_dispatch(route_t, counts, T)
        xs = _sc_scatter_rows(h2.reshape(T, D // 2), dest, n_rows)
        ys = _experts(l, blk_e, n_used, next_e, xs, moe_w_gate, moe_w_up, moe_w_down)
        y01 = _sc_gather_rows(ys, dest.reshape(TOP_K * T)).reshape(TOP_K, B, S, D // 2)
        pending = (g2, y01, y01, route)
    return _combine(x, pending)
```

```python
import functools

import numpy as np
import jax
import jax.numpy as jnp
from jax import lax
from jax.experimental import pallas as pl
from jax.experimental.pallas import tpu as pltpu
from jax.experimental.pallas import tpu_sc as plsc

F32 = jnp.float32
BF16 = jnp.bfloat16
HIGHEST = lax.Precision.HIGHEST

D_MODEL = 1024
DEPTH = 4
ATTN_HEADS = 8
HEAD_DIM = 64
ATTN_WIDTH = ATTN_HEADS * HEAD_DIM
IDX_HEADS = 8
IDX_DIM = 64
INDEX_TOPK = 256
Q_BLOCK = 256
ROPE_THETA = 10000.0
CONV_WIDTH = D_MODEL - ATTN_WIDTH
CONV_K = 3
ML_HEADS = 8
ML_QK_DIM = 64
ML_V_DIM = 128
N_GROUPS = 4
EXPERTS_PER_GROUP = 8
N_EXPERTS = N_GROUPS * EXPERTS_PER_GROUP
TOP_K = 2
D_EXPERT = 512
MOE_BLOCK = 512
NORM_EPS = 1e-6

LANES = 128
VMEM_LIMIT = 56 * 1024 * 1024
TOKEN_TILE = 512
OUT_TILE = 1024
ML_CHUNK = 256
ML_STEP_HEADS = 8
NEG_INF = float("-inf")


def _cparams(*sem):
    return pltpu.CompilerParams(dimension_semantics=sem, vmem_limit_bytes=VMEM_LIMIT)


def _dot(a, b):
    return jnp.dot(a, b, preferred_element_type=F32)


def _pack_bf16_pairs(x):
    bits = lax.bitcast_convert_type(x.astype(BF16).astype(F32), jnp.uint32)
    half = bits.shape[1] // 2
    packed = (bits[:, :half] >> 16) | (bits[:, half:] & jnp.uint32(0xFFFF0000))
    return lax.bitcast_convert_type(packed, jnp.int32)


def _unpack_bf16_pairs(words):
    words = lax.bitcast_convert_type(words, jnp.uint32)
    return jnp.concatenate(
        [lax.bitcast_convert_type(words << 16, F32),
         lax.bitcast_convert_type(words & jnp.uint32(0xFFFF0000), F32)], axis=1)


def _split_dot(a_f32, b_bf16):
    hi = a_f32.astype(BF16)
    lo = (a_f32 - hi.astype(F32)).astype(BF16)
    return _dot(hi, b_bf16) + _dot(lo, b_bf16)


def _ada_kernel(c_ref, w_ref, b_ref, o_ref):
    c = c_ref[...]
    ca = c * jax.nn.sigmoid(c)
    o_ref[0] = jnp.dot(ca, w_ref[0], precision=HIGHEST, preferred_element_type=F32) + b_ref[0]


def _ada_modulation(c, ada_w, ada_b):
    B, D = c.shape
    n_col = ada_w.shape[-1] // D
    return pl.pallas_call(
        _ada_kernel,
        out_shape=jax.ShapeDtypeStruct((DEPTH, B, n_col * D), F32),
        grid=(DEPTH, n_col),
        in_specs=[
            pl.BlockSpec((B, D), lambda l, j: (0, 0)),
            pl.BlockSpec((1, D, D), lambda l, j: (l, 0, j)),
            pl.BlockSpec((1, 1, D), lambda l, j: (l, 0, j)),
        ],
        out_specs=pl.BlockSpec((1, B, D), lambda l, j: (l, 0, j)),
        compiler_params=_cparams("parallel", "parallel"),
        name="ada_modulation",
    )(c, ada_w, ada_b.reshape(DEPTH, 1, n_col * D))


def _modulated_norm(x, gain, scale, shift):
    y = x * lax.rsqrt(jnp.mean(x * x, axis=-1, keepdims=True) + NORM_EPS)
    return y * gain * (1.0 + scale) + shift


def _rope(x, cos, sin_signed, first_half):
    w = x.shape[-1]
    partner = jnp.where(first_half, pltpu.roll(x, w - HEAD_DIM // 2, 1), pltpu.roll(x, HEAD_DIM // 2, 1))
    return x * cos + partner * sin_signed


HYB_COLS = 5 * 512 + 2 * LANES


def _hyb_in_kernel(*refs, n_pending):
    x_ref, pending_refs, refs = refs[0], refs[1:1 + n_pending], refs[1 + n_pending:]
    (gain_ref, sc_ref, sh_ref, w_ref, cos_ref, sin_ref, qn_ref, kn_ref, grp_ref,
     qt_ref, iqt_ref, bcu_ref, kv_ref, kvt_ref, ik_ref, iwt_ref) = refs[:16]
    x = _residual_tile(x_ref, pending_refs)
    if n_pending:
        refs[16][0] = x
    h = _modulated_norm(x, gain_ref[...], sc_ref[0], sh_ref[0])
    hb = h.astype(BF16)
    p_small = _dot(hb, w_ref[:, 0:2 * LANES])
    p_q = _dot(hb, w_ref[:, 2 * LANES:2 * LANES + 512])
    p_iq = _dot(hb, w_ref[:, 2 * LANES + 512:2 * LANES + 1024])
    cos = cos_ref[...]
    sin = sin_ref[...]
    lane = lax.broadcasted_iota(jnp.int32, (1, ATTN_WIDTH), 1)
    first_half = (lane % HEAD_DIM) < (HEAD_DIM // 2)
    fh128 = first_half[:, :LANES]
    lane128 = lane[:, :LANES]

    kv = p_small[:, :LANES]
    is_k = lane128 < HEAD_DIM
    kk = jnp.where(is_k, kv, 0.0)
    ms_k = jnp.sum(kk * kk, axis=-1, keepdims=True) * (1.0 / HEAD_DIM)
    kn = kv * lax.rsqrt(ms_k + NORM_EPS) * kn_ref[...]
    kr = _rope(kn, cos[:, :LANES], sin[:, :LANES], fh128)
    kv = jnp.where(is_k, kr, kv)
    kv_ref[0] = kv.astype(BF16)
    feat = lax.broadcasted_iota(jnp.int32, (LANES, 1), 0)
    kv_t = jnp.where(feat < HEAD_DIM, 1.0, kv.T).astype(BF16)
    for i in range(kvt_ref.shape[1]):
        kvt_ref[0, i] = kv_t[:, i * DSA_KEY_CHUNK:(i + 1) * DSA_KEY_CHUNK]

    sm = p_small[:, LANES:]
    ikr = _rope(sm, cos[:, :LANES], sin[:, :LANES], fh128)
    ik_ref[0] = jnp.where(is_k, ikr, 0.0).astype(BF16)
    iwt_ref[0] = sm.T

    ms = _split_dot(p_q * p_q, grp_ref[...]) * (1.0 / HEAD_DIM)
    q = p_q * lax.rsqrt(ms + NORM_EPS) * qn_ref[...]
    qt_ref[0] = (_rope(q, cos, sin, first_half) * (HEAD_DIM ** -0.5)).T.astype(BF16)
    iqt_ref[0] = (_rope(p_iq, cos, sin, first_half) * (IDX_DIM ** -0.5)).T.astype(BF16)

    bcu_ref[0] = _dot(hb, w_ref[:, 2 * LANES + 1024:])


def _hyb_in(x, pending, gain, sc, sh, w_pad, cos_t, sin_t, qn_t, kn_t, grp):
    B, S, D = x.shape
    tm = TOKEN_TILE
    row = lambda b, j: (b, j, 0)
    per_b = lambda b, j: (b, 0, 0)
    const2 = lambda b, j: (0, 0)
    tab = lambda b, j: (j, 0)
    col = lambda b, j: (b, 0, j)
    assert tm % DSA_KEY_CHUNK == 0
    x_out_shape = (jax.ShapeDtypeStruct((B, S, D), F32),) if pending else ()
    x_out_spec = (pl.BlockSpec((1, tm, D), row),) if pending else ()
    return pl.pallas_call(
        functools.partial(_hyb_in_kernel, n_pending=len(pending)),
        out_shape=(
            jax.ShapeDtypeStruct((B, 512, S), BF16),
            jax.ShapeDtypeStruct((B, 512, S), BF16),
            jax.ShapeDtypeStruct((B, S, 1536), F32),
            jax.ShapeDtypeStruct((B, S, LANES), BF16),
            jax.ShapeDtypeStruct((B, S // DSA_KEY_CHUNK, LANES, DSA_KEY_CHUNK), BF16),
            jax.ShapeDtypeStruct((B, S, LANES), BF16),
            jax.ShapeDtypeStruct((B, LANES, S), F32),
        ) + x_out_shape,
        grid=(B, S // tm),
        in_specs=[pl.BlockSpec((1, tm, D), row)] + (_pending_specs(D, tm) if pending else []) + [
            pl.BlockSpec((1, D), const2),
            pl.BlockSpec((1, 1, D), per_b),
            pl.BlockSpec((1, 1, D), per_b),
            pl.BlockSpec((D, HYB_COLS), const2),
            pl.BlockSpec((tm, 512), tab),
            pl.BlockSpec((tm, 512), tab),
            pl.BlockSpec((1, 512), const2),
            pl.BlockSpec((1, LANES), const2),
            pl.BlockSpec((512, 512), const2),
        ],
        out_specs=(
            pl.BlockSpec((1, 512, tm), col),
            pl.BlockSpec((1, 512, tm), col),
            pl.BlockSpec((1, tm, 1536), row),
            pl.BlockSpec((1, tm, LANES), row),
            pl.BlockSpec((1, tm // DSA_KEY_CHUNK, LANES, DSA_KEY_CHUNK), lambda b, j: (b, j, 0, 0)),
            pl.BlockSpec((1, tm, LANES), row),
            pl.BlockSpec((1, LANES, tm), col),
        ) + x_out_spec,
        compiler_params=_cparams("parallel", "parallel"),
        name="hybrid_in_proj",
    )(x, *pending, gain, sc, sh, w_pad, cos_t, sin_t, qn_t, kn_t, grp)


DSA_KEY_CHUNK = 512
DSA_SUM_ROWS = 16


def _fold8(x, op):
    parts = x.reshape(x.shape[0] // 8, 8, x.shape[1])
    while parts.shape[0] > 1:
        half = parts.shape[0] // 2
        assert parts.shape[0] == 2 * half
        parts = op(parts[:half], parts[half:])
    return parts[0]


def _col_reduce(x, op):
    t = _fold8(x, op)
    for shift in (4, 2, 1):
        t = op(t, pltpu.roll(t, shift, 0))
    return t[0:1, :]


def _dsa_t_kernel(qt_ref, iqt_ref, iwt_ref, kv_ref, kvt_ref, ik_ref, o_ref, sc_ref, bias_ref, acc_ref):
    CK = DSA_KEY_CHUNK
    QB = Q_BLOCK
    qb = pl.program_id(1)
    nk = (qb * QB + QB + CK - 1) // CK
    kf = float(INDEX_TOPK)
    qpos = qb * QB + lax.broadcasted_iota(jnp.int32, (1, QB), 1)
    krow = lax.broadcasted_iota(jnp.int32, (CK, 1), 0)
    w_idx = iwt_ref[0, IDX_DIM:IDX_DIM + IDX_HEADS, :] * (IDX_HEADS ** -0.5)

    def rows(c):
        return pl.ds(pl.multiple_of(c * CK, CK), CK)

    def heads_on_lanes(ref, width):
        return jnp.concatenate([ref[0, hd * width:(hd + 1) * width, :] for hd in range(ref.shape[1] // width)], axis=1)

    def head_lanes(hd):
        return slice(hd * QB, (hd + 1) * QB)

    iq_wide = heads_on_lanes(iqt_ref, IDX_DIM)
    w_wide = jnp.concatenate([w_idx[hd:hd + 1, :] for hd in range(IDX_HEADS)], axis=1)

    def score_chunk(c, carry):
        mx, mn = carry
        ikc = ik_ref[0, rows(c), :][:, :IDX_DIM]
        s_all = jnp.maximum(_dot(ikc, iq_wide), 0.0) * w_wide
        acc = s_all[:, head_lanes(0)]
        for hd in range(1, IDX_HEADS):
            acc = acc + s_all[:, head_lanes(hd)]
        causal = (c * CK + krow) <= qpos
        sc_ref[rows(c), :] = jnp.where(causal, acc, NEG_INF)
        mx = jnp.maximum(mx, _fold8(jnp.where(causal, acc, NEG_INF), jnp.maximum))
        mn = jnp.minimum(mn, _fold8(jnp.where(causal, acc, jnp.inf), jnp.minimum))
        return mx, mn

    mx8, mn8 = lax.fori_loop(0, nk, score_chunk,
                             (jnp.full((8, QB), NEG_INF, F32), jnp.full((8, QB), jnp.inf, F32)))
    row_max = jnp.max(mx8, axis=0, keepdims=True)
    row_min = jnp.min(mn8, axis=0, keepdims=True)

    def count(pred):
        def body(c, part):
            return part + _fold8(jnp.where(pred(sc_ref[rows(c), :]), 1.0, 0.0), jnp.add)
        part = lax.fori_loop(0, nk, body, jnp.zeros((8, QB), F32))
        return jnp.sum(part, axis=0, keepdims=True)

    @pl.when(qb * QB + QB <= INDEX_TOPK)
    def _():
        def body(c, carry):
            bias_ref[rows(c), :] = jnp.where(sc_ref[rows(c), :] > NEG_INF, 0.0, NEG_INF)
            return carry
        lax.fori_loop(0, nk, body, 0)

    @pl.when(qb * QB + QB > INDEX_TOPK)
    def _():
        def bisect(_, carry):
            lo, hi, c_lo, c_hi = carry
            mid = 0.5 * lo + 0.5 * jnp.minimum(hi, row_max)
            cnt = count(lambda x: x >= mid)
            ge = cnt >= kf
            return (jnp.where(ge, mid, lo), jnp.where(ge, hi, mid),
                    jnp.where(ge, cnt, c_lo), jnp.where(ge, c_hi, cnt))

        n_adm = (qpos + 1).astype(F32)
        lo, hi, c_lo, c_hi = lax.fori_loop(
            0, 18, bisect, (row_min, jnp.full((1, QB), jnp.inf, F32), n_adm, jnp.zeros((1, QB), F32)))

        def refine_cond(carry):
            it, _, _, _, done = carry
            return jnp.logical_and(it < nk * CK, jnp.min(done) < 0.5)

        def refine(carry):
            it, hi, c_hi, thr, done = carry

            def edges(c, part):
                up, dn = part
                x = sc_ref[rows(c), :]
                return (jnp.maximum(up, _fold8(jnp.where(x < hi, x, NEG_INF), jnp.maximum)),
                        jnp.minimum(dn, _fold8(jnp.where(x >= lo, x, jnp.inf), jnp.minimum)))

            up8, dn8 = lax.fori_loop(0, nk, edges,
                                     (jnp.full((8, QB), NEG_INF, F32), jnp.full((8, QB), jnp.inf, F32)))
            m_up = jnp.max(up8, axis=0, keepdims=True)
            m_dn = jnp.min(dn8, axis=0, keepdims=True)
            from_hi = c_hi == kf - 1.0
            from_lo = c_lo == kf
            thr = jnp.where(done > 0.5, thr, jnp.where(from_hi, m_up, m_dn))
            done = jnp.where(jnp.logical_or(from_hi, from_lo), 1.0, done)

            def step_down(args):
                hi, c_hi, thr, done = args
                open_ = done < 0.5
                cnt = count(lambda x: x >= m_up)
                hit = jnp.logical_and(open_, cnt >= kf)
                moved = jnp.logical_and(open_, cnt < kf)
                return (jnp.where(moved, m_up, hi), jnp.where(moved, cnt, c_hi), jnp.where(hit, m_up, thr),
                        jnp.where(hit, 1.0, done))

            hi, c_hi, thr, done = lax.cond(jnp.min(done) > 0.5, lambda args: args, step_down, (hi, c_hi, thr, done))
            return it + 1, hi, c_hi, thr, done

        _, _, _, thr, _ = lax.while_loop(
            refine_cond, refine, (jnp.int32(0), hi, c_hi, row_max, jnp.zeros((1, QB), F32)))

        def body(c, sel):
            keep = sc_ref[rows(c), :] >= thr
            bias_ref[rows(c), :] = jnp.where(keep, 0.0, NEG_INF)
            return sel + _fold8(jnp.where(keep, 1.0, 0.0), jnp.add)

        n_sel = jnp.sum(lax.fori_loop(0, nk, body, jnp.zeros((8, QB), F32)), axis=0, keepdims=True)

        @pl.when(jnp.max(n_sel) > kf + 0.5)
        def _():
            need = kf - count(lambda x: x > thr)
            r_i = lax.broadcasted_iota(jnp.int32, (CK, CK), 0)
            c_i = lax.broadcasted_iota(jnp.int32, (CK, CK), 1)
            lower = jnp.where(c_i <= r_i, 1.0, 0.0).astype(BF16)

            def body(c, seen):
                x = sc_ref[rows(c), :]
                eq = x == thr
                eq_f = jnp.where(eq, 1.0, 0.0)
                rank = _dot(lower, eq_f.astype(BF16)) + seen
                keep = jnp.logical_or(x > thr, jnp.logical_and(eq, rank <= need))
                bias_ref[rows(c), :] = jnp.where(keep, 0.0, NEG_INF)
                return seen + jnp.sum(eq_f, axis=0, keepdims=True)

            lax.fori_loop(0, nk, body, jnp.zeros((1, QB), F32))

    acc_ref[...] = jnp.zeros_like(acc_ref)
    q_wide = heads_on_lanes(qt_ref, HEAD_DIM)

    def attend_chunk(c, m_old):
        kc = kv_ref[0, rows(c), :][:, :HEAD_DIM]
        bias = bias_ref[rows(c), :]
        logits = _dot(kc, q_wide) + jnp.concatenate([bias] * ATTN_HEADS, axis=1)
        m_new = jnp.maximum(m_old, _col_reduce(logits, jnp.maximum))
        alpha = jnp.exp(m_old - m_new)
        p = jnp.exp(logits - m_new)
        acc_ref[...] = alpha * acc_ref[...] + _dot(kvt_ref[0, c, HEAD_DIM - DSA_SUM_ROWS:, :], p.astype(BF16))
        return m_new

    lax.fori_loop(0, nk, attend_chunk, jnp.full((1, ATTN_HEADS * QB), -1e30, F32))
    out_t = acc_ref[DSA_SUM_ROWS:, :] / acc_ref[0:1, :]
    o_ref[0] = jnp.concatenate([out_t[:, head_lanes(hd)] for hd in range(ATTN_HEADS)], axis=0).T


def _dsa_attention_t(qt, iqt, iwt, kv, kvt, ik):
    B, _, S = qt.shape
    col = lambda b, j: (b, 0, j)
    per_b = lambda b, j: (b, 0, 0)
    return pl.pallas_call(
        _dsa_t_kernel,
        out_shape=jax.ShapeDtypeStruct((B, S, ATTN_WIDTH), F32),
        grid=(B, S // Q_BLOCK),
        in_specs=[
            pl.BlockSpec((1, ATTN_WIDTH, Q_BLOCK), col),
            pl.BlockSpec((1, IDX_HEADS * IDX_DIM, Q_BLOCK), col),
            pl.BlockSpec((1, LANES, Q_BLOCK), col),
            pl.BlockSpec((1, S, LANES), per_b),
            pl.BlockSpec((1, S // DSA_KEY_CHUNK, LANES, DSA_KEY_CHUNK), lambda b, j: (b, 0, 0, 0)),
            pl.BlockSpec((1, S, LANES), per_b),
        ],
        out_specs=pl.BlockSpec((1, Q_BLOCK, ATTN_WIDTH), lambda b, j: (b, j, 0)),
        scratch_shapes=[
            pltpu.VMEM((S, Q_BLOCK), F32),
            pltpu.VMEM((S, Q_BLOCK), F32),
            pltpu.VMEM((DSA_SUM_ROWS + HEAD_DIM, ATTN_HEADS * Q_BLOCK), F32),
        ],
        compiler_params=_cparams("parallel", "parallel"),
        name="dsa_attention",
    )(qt, iqt, iwt, kv, kvt, ik)


def _hyb_out_kernel(ya_ref, bcu_ref, halo_ref, cw_ref, w_ref, x_ref, g_ref,
                    gain2_ref, sc2_ref, sh2_ref, wr_ref, br_ref, o_ref, h_ref, r_ref, rt_ref, cnt_ref, run_ref, tri_ref):
    j = pl.program_id(1)
    tm = ya_ref.shape[1]
    bcu = bcu_ref[0]
    bg = bcu[:, 0:512]
    z = bcu[:, 512:1024] * bcu[:, 1024:1536]
    halo = halo_ref[0]
    zh = halo[:, 512:1024] * halo[:, 1024:1536]
    zh = jnp.where(j > 0, zh, 0.0)
    row = lax.broadcasted_iota(jnp.int32, (tm, 1), 0)
    z1 = jnp.where(row >= 1, pltpu.roll(z, 1, 0), zh[7:8, :])
    z2 = jnp.where(row >= 2, pltpu.roll(z, 2, 0), jnp.where(row == 1, zh[7:8, :], zh[6:7, :]))
    cw = cw_ref[...]
    y_conv = bg * (z2 * cw[0:1, :] + z1 * cw[1:2, :] + z * cw[2:3, :])
    y = _dot(ya_ref[0].astype(BF16), w_ref[0:512, :]) + _dot(y_conv.astype(BF16), w_ref[512:1024, :])
    x_new = x_ref[0] + g_ref[0] * y
    o_ref[0] = x_new
    _route_tile(x_new, gain2_ref, sc2_ref, sh2_ref, wr_ref, br_ref, h_ref, r_ref, rt_ref, cnt_ref, run_ref, tri_ref)


def _hyb_out(y_attn, bcu, conv_w, w_out_bf, x, g1, route_args):
    B, S, D = x.shape
    tm = OUT_TILE
    row = lambda b, j: (b, j, 0)
    per_b = lambda b, j: (b, 0, 0)
    const2 = lambda b, j: (0, 0)
    halo = lambda b, j: (b, jnp.maximum(j * (tm // 8) - 1, 0), 0)
    r_in, r_shape, r_out, r_scratch = _route_specs(B, S, D, tm)
    return pl.pallas_call(
        _hyb_out_kernel,
        out_shape=(jax.ShapeDtypeStruct((B, S, D), F32),) + r_shape,
        grid=(B, S // tm),
        in_specs=[
            pl.BlockSpec((1, tm, 512), row),
            pl.BlockSpec((1, tm, 1536), row),
            pl.BlockSpec((1, 8, 1536), halo),
            pl.BlockSpec((CONV_K, CONV_WIDTH), const2),
            pl.BlockSpec((D, D), const2),
            pl.BlockSpec((1, tm, D), row),
            pl.BlockSpec((1, 1, D), per_b),
        ] + r_in,
        out_specs=(pl.BlockSpec((1, tm, D), row),) + r_out,
        scratch_shapes=r_scratch,
        compiler_params=_cparams("arbitrary", "arbitrary"),
        name="hybrid_out_proj",
    )(y_attn, bcu, bcu, conv_w, w_out_bf, x, g1, *route_args)


ML_COLS = 512 + 512 + 1024 + 1024 + LANES


def _ml_in_kernel(x_ref, g2_ref, y0_ref, y1_ref, r_ref, gain_ref, sc_ref, sh_ref, w_ref, bias_ref,
                  q_ref, k_ref, v_ref, og_ref, gt_ref, gtt_ref, xo_ref):
    x = _residual_tile(x_ref, (g2_ref, y0_ref, y1_ref, r_ref))
    xo_ref[0] = x
    h = _modulated_norm(x, gain_ref[...], sc_ref[0], sh_ref[0])
    hb = h.astype(BF16)
    gates = _dot(hb, w_ref[:, 0:LANES]) + bias_ref[...]
    gates_t = gates.T[:2 * ML_HEADS, :]
    keys_t = _dot(hb, w_ref[:, LANES:LANES + 512]).T.astype(BF16)
    for i in range(gt_ref.shape[1]):
        span = slice(i * ML_CHUNK, (i + 1) * ML_CHUNK)
        gt_ref[0, i] = gates[span, :]
        gtt_ref[0, i] = gates_t[:, span]
        k_ref[0, i] = keys_t[:, span]
    q_ref[0] = (_dot(hb, w_ref[:, LANES + 512:LANES + 1024]) * (ML_QK_DIM ** -0.5)).astype(BF16)
    v_ref[0] = _dot(hb, w_ref[:, LANES + 1024:LANES + 2048]).astype(BF16)
    og_ref[0] = _dot(hb, w_ref[:, LANES + 2048:])


def _ml_in(x, pending, gain, sc, sh, w_pad, gate_bias):
    B, S, D = x.shape
    tm = TOKEN_TILE
    L = ML_CHUNK
    assert tm % L == 0
    per_tile = tm // L
    chunked = lambda b, j: (b, j, 0, 0)
    row = lambda b, j: (b, j, 0)
    per_b = lambda b, j: (b, 0, 0)
    const2 = lambda b, j: (0, 0)
    return pl.pallas_call(
        _ml_in_kernel,
        out_shape=(
            jax.ShapeDtypeStruct((B, S, 512), BF16),
            jax.ShapeDtypeStruct((B, S // L, 512, L), BF16),
            jax.ShapeDtypeStruct((B, S, 1024), BF16),
            jax.ShapeDtypeStruct((B, S, 1024), F32),
            jax.ShapeDtypeStruct((B, S // L, L, LANES), F32),
            jax.ShapeDtypeStruct((B, S // L, 2 * ML_HEADS, L), F32),
            jax.ShapeDtypeStruct((B, S, D), F32),
        ),
        grid=(B, S // tm),
        in_specs=[pl.BlockSpec((1, tm, D), row)] + _pending_specs(D, tm) + [
            pl.BlockSpec((1, D), const2),
            pl.BlockSpec((1, 1, D), per_b),
            pl.BlockSpec((1, 1, D), per_b),
            pl.BlockSpec((D, ML_COLS), const2),
            pl.BlockSpec((1, LANES), const2),
        ],
        out_specs=(
            pl.BlockSpec((1, tm, 512), row),
            pl.BlockSpec((1, per_tile, 512, L), chunked),
            pl.BlockSpec((1, tm, 1024), row),
            pl.BlockSpec((1, tm, 1024), row),
            pl.BlockSpec((1, per_tile, L, LANES), chunked),
            pl.BlockSpec((1, per_tile, 2 * ML_HEADS, L), chunked),
            pl.BlockSpec((1, tm, D), row),
        ),
        compiler_params=_cparams("parallel", "parallel"),
        name="mlstm_in_proj",
    )(x, *pending, gain, sc, sh, w_pad, gate_bias)


def _log_sigmoid(f):
    return jnp.minimum(f, 0.0) - jnp.log1p(jnp.exp(-jnp.abs(f)))


def _split3(x):
    a = x.astype(BF16)
    r = x - a.astype(F32)
    b = r.astype(BF16)
    c = (r - b.astype(F32)).astype(BF16)
    return a, b, c


def _twice(a):
    return jnp.concatenate([a, a], axis=1)


def _over_lanes(a, width):
    return jnp.concatenate([a] * (width // LANES), axis=1)


def _mlstm_kernel(q_ref, kt_ref, v_ref, grow_ref, gcol_ref, gain_ref, o_ref, c_ref, m_ref):
    L = ML_CHUNK
    HP = ML_STEP_HEADS
    S = q_ref.shape[1]
    c_ref[...] = jnp.zeros_like(c_ref)
    m_ref[...] = jnp.zeros_like(m_ref)

    def chunk(c, carry):
        r0 = pl.multiple_of(c * L, L)
        r_i = lax.broadcasted_iota(jnp.int32, (L, L), 0)
        c_i = lax.broadcasted_iota(jnp.int32, (L, L), 1)
        tril = c_i <= r_i
        lower = jnp.where(tril, 1.0, 0.0).astype(BF16)
        upper = jnp.where(r_i <= c_i, 1.0, 0.0).astype(BF16)
        e_r = lax.broadcasted_iota(jnp.int32, (LANES, HP * LANES), 0)
        e_c = lax.broadcasted_iota(jnp.int32, (LANES, HP * LANES), 1)
        pick = jnp.where(e_r == HP + e_c // LANES, 1.0, 0.0).astype(BF16)
        rows = grow_ref[0, 0, c]
        cols = gcol_ref[0, 0, c]
        b_rows = sum(_dot(p, upper) for p in _split3(_log_sigmoid(rows)))
        b_cols = sum(_dot(lower, p) for p in _split3(_log_sigmoid(cols)))
        b_colr = sum(_dot(p, pick) for p in _split3(b_cols))
        lane = lax.broadcasted_iota(jnp.int32, (1, L), 1)
        b_last_all = jnp.sum(jnp.where(lane == L - 1, b_rows, 0.0), axis=-1, keepdims=True)
        ones_v = jnp.ones((L, ML_V_DIM), BF16)
        for hh in range(HP):
            q = q_ref[0, pl.ds(r0, L), hh * ML_QK_DIM:(hh + 1) * ML_QK_DIM]
            kt = kt_ref[0, c, hh * ML_QK_DIM:(hh + 1) * ML_QK_DIM, :]
            v = v_ref[0, pl.ds(r0, L), hh * ML_V_DIM:(hh + 1) * ML_V_DIM]
            vx = jnp.concatenate([v, ones_v], axis=1)
            i_row = rows[hh:hh + 1, :]
            b_row = b_rows[HP + hh:HP + hh + 1, :]
            b_last = b_last_all[HP + hh:HP + hh + 1, :]
            b_col = b_colr[:, hh * LANES:(hh + 1) * LANES]
            m_prev = m_ref[hh]
            ctn = c_ref[hh]

            dmat = jnp.where(tril, _over_lanes(b_col, L) - b_row + i_row, NEG_INF)
            inter = b_col + m_prev
            m_t = jnp.maximum(inter, jnp.max(dmat, axis=-1, keepdims=True))
            w_intra = jnp.exp(dmat - _over_lanes(m_t, L))
            w_inter = jnp.exp(inter - m_t)
            intra = (w_intra * _dot(q, kt)).astype(BF16)
            tot = _twice(w_inter) * _dot(q, ctn.astype(BF16)) + _dot(intra, vx)
            num = tot[:, :ML_V_DIM]
            den = tot[:, ML_V_DIM:]
            hc = num / jnp.maximum(jnp.abs(den), jnp.exp(-m_t))
            y = hc * lax.rsqrt(jnp.mean(hc * hc, axis=-1, keepdims=True) + NORM_EPS)
            o_ref[0, pl.ds(r0, L), hh * ML_V_DIM:(hh + 1) * ML_V_DIM] = (
                y * gain_ref[:, hh * ML_V_DIM:(hh + 1) * ML_V_DIM])

            g_row = b_last - b_row + i_row
            m_new = jnp.maximum(b_last + m_prev, jnp.max(g_row, axis=-1, keepdims=True))
            decay = jnp.exp(b_last + m_prev - m_new)
            kw = (kt.astype(F32) * jnp.exp(g_row - _over_lanes(m_new, L))).astype(BF16)
            c_ref[hh] = _twice(decay) * ctn + _dot(kw, vx)
            m_ref[hh] = m_new
        return carry

    lax.fori_loop(0, S // L, chunk, 0)


def _mlstm(q, kt, v, g_rows, g_cols, out_gain):
    B, S, _ = q.shape
    assert ML_CHUNK % LANES == 0
    nc = S // ML_CHUNK
    hp = ML_STEP_HEADS
    return pl.pallas_call(
        _mlstm_kernel,
        out_shape=jax.ShapeDtypeStruct((B, S, ML_HEADS * ML_V_DIM), F32),
        grid=(B, ML_HEADS // hp),
        in_specs=[
            pl.BlockSpec((1, S, hp * ML_QK_DIM), lambda b, p: (b, 0, p)),
            pl.BlockSpec((1, nc, hp * ML_QK_DIM, ML_CHUNK), lambda b, p: (b, 0, p, 0)),
            pl.BlockSpec((1, S, hp * ML_V_DIM), lambda b, p: (b, 0, p)),
            pl.BlockSpec((1, 1, nc, 2 * hp, ML_CHUNK), lambda b, p: (b, p, 0, 0, 0)),
            pl.BlockSpec((1, 1, nc, ML_CHUNK, LANES), lambda b, p: (b, p, 0, 0, 0)),
            pl.BlockSpec((1, hp * ML_V_DIM), lambda b, p: (0, p)),
        ],
        out_specs=pl.BlockSpec((1, S, hp * ML_V_DIM), lambda b, p: (b, 0, p)),
        scratch_shapes=[
            pltpu.VMEM((hp, ML_QK_DIM, ML_V_DIM + LANES), F32),
            pltpu.VMEM((hp, 1, LANES), F32),
        ],
        compiler_params=_cparams("parallel", "parallel"),
        name="mlstm_chunkwise",
    )(q, kt, v, g_rows, g_cols, out_gain)


def _ml_out_kernel(hh_ref, og_ref, w_ref, x_ref, g_ref,
                   gain2_ref, sc2_ref, sh2_ref, wr_ref, br_ref, o_ref, h_ref, r_ref, rt_ref, cnt_ref, run_ref, tri_ref):
    a = jax.nn.sigmoid(og_ref[0]) * hh_ref[0]
    x_new = x_ref[0] + g_ref[0] * _dot(a.astype(BF16), w_ref[...])
    o_ref[0] = x_new
    _route_tile(x_new, gain2_ref, sc2_ref, sh2_ref, wr_ref, br_ref, h_ref, r_ref, rt_ref, cnt_ref, run_ref, tri_ref)


def _ml_out(hh, og, w_out_bf, x, g1, route_args):
    B, S, D = x.shape
    tm = OUT_TILE
    row = lambda b, j: (b, j, 0)
    per_b = lambda b, j: (b, 0, 0)
    const2 = lambda b, j: (0, 0)
    r_in, r_shape, r_out, r_scratch = _route_specs(B, S, D, tm)
    return pl.pallas_call(
        _ml_out_kernel,
        out_shape=(jax.ShapeDtypeStruct((B, S, D), F32),) + r_shape,
        grid=(B, S // tm),
        in_specs=[
            pl.BlockSpec((1, tm, D), row),
            pl.BlockSpec((1, tm, D), row),
            pl.BlockSpec((D, D), const2),
            pl.BlockSpec((1, tm, D), row),
            pl.BlockSpec((1, 1, D), per_b),
        ] + r_in,
        out_specs=(pl.BlockSpec((1, tm, D), row),) + r_out,
        scratch_shapes=r_scratch,
        compiler_params=_cparams("arbitrary", "arbitrary"),
        name="mlstm_out_proj",
    )(hh, og, w_out_bf, x, g1, *route_args)


def _first_argmax(x, lane, width):
    mx = jnp.max(x, axis=-1, keepdims=True)
    idx = jnp.min(jnp.where(x == mx, lane, width), axis=-1, keepdims=True)
    return mx, idx


def _route_tile(x, gain_ref, sc_ref, sh_ref, w_ref, b_ref, h_ref, r_ref, rt_ref, cnt_ref, run_ref, tri_ref):
    tm = x.shape[0]

    @pl.when(jnp.logical_and(pl.program_id(0) == 0, pl.program_id(1) == 0))
    def _():
        run_ref[...] = jnp.zeros_like(run_ref)
        r_i = lax.broadcasted_iota(jnp.int32, (tm, tm), 0)
        c_i = lax.broadcasted_iota(jnp.int32, (tm, tm), 1)
        tri_ref[...] = jnp.where(c_i < r_i, 1.0, 0.0).astype(BF16)

    h = _modulated_norm(x, gain_ref[...], sc_ref[0], sh_ref[0])
    h_ref[0] = _pack_bf16_pairs(h)
    logits = _dot(h.astype(BF16), w_ref[...]) + b_ref[...]
    lane = lax.broadcasted_iota(jnp.int32, (1, LANES), 1)

    def pick(lgt):
        lg = jnp.where(lane < N_GROUPS, lgt, NEG_INF)
        g_max, g_sel = _first_argmax(lg, lane, LANES)
        pg = 1.0 / jnp.sum(jnp.exp(lg - g_max), axis=-1, keepdims=True)
        e_lane = lane - N_GROUPS
        in_grp = jnp.logical_and(e_lane >= g_sel * EXPERTS_PER_GROUP, e_lane < (g_sel + 1) * EXPERTS_PER_GROUP)
        le = jnp.where(in_grp, lgt, NEG_INF)
        v1, i1 = _first_argmax(le, lane, LANES)
        v2, i2 = _first_argmax(jnp.where(lane == i1, NEG_INF, le), lane, LANES)
        ratio = jnp.exp(v2 - v1)
        return i1 - N_GROUPS, i2 - N_GROUPS, pg / (1.0 + ratio), pg * ratio / (1.0 + ratio)

    e1, e2, w1, w2 = pick(logits)
    hot1 = lane == e1
    hot2 = lane == e2
    onehot = jnp.where(jnp.logical_or(hot1, hot2), 1.0, 0.0)
    seen = _dot(tri_ref[...], onehot.astype(BF16)) + run_ref[...]
    rank1 = jnp.sum(jnp.where(hot1, seen, 0.0), axis=-1, keepdims=True)
    rank2 = jnp.sum(jnp.where(hot2, seen, 0.0), axis=-1, keepdims=True)
    run_ref[...] = run_ref[...] + jnp.sum(onehot, axis=0, keepdims=True)
    cnt_ref[...] = run_ref[...]

    out = jnp.where(lane == 0, e1.astype(F32), 0.0)
    out = jnp.where(lane == 1, e2.astype(F32), out)
    out = jnp.where(lane == 2, w1, out)
    out = jnp.where(lane == 3, w2, out)
    out = jnp.where(lane == 4, rank1, out)
    out = jnp.where(lane == 5, rank2, out)
    r_ref[0] = out
    rt_ref[0] = out.T[:8, :]


def _route_specs(B, S, D, tm):
    row = lambda b, j: (b, j, 0)
    per_b = lambda b, j: (b, 0, 0)
    const2 = lambda b, j: (0, 0)
    in_specs = [
        pl.BlockSpec((1, D), const2),
        pl.BlockSpec((1, 1, D), per_b),
        pl.BlockSpec((1, 1, D), per_b),
        pl.BlockSpec((D, LANES), const2),
        pl.BlockSpec((1, LANES), const2),
    ]
    out_shape = (
        jax.ShapeDtypeStruct((B, S, D // 2), jnp.int32),
        jax.ShapeDtypeStruct((B, S, LANES), F32),
        jax.ShapeDtypeStruct((B, 8, S), F32),
        jax.ShapeDtypeStruct((1, LANES), F32),
    )
    out_specs = (
        pl.BlockSpec((1, tm, D // 2), row),
        pl.BlockSpec((1, tm, LANES), row),
        pl.BlockSpec((1, 8, tm), lambda b, j: (b, 0, j)),
        pl.BlockSpec((1, LANES), const2),
    )
    return in_specs, out_shape, out_specs, [pltpu.VMEM((1, LANES), F32), pltpu.VMEM((tm, tm), BF16)]


def _experts_kernel(blk_e_ref, n_used_ref, next_e_ref, x_ref, wg_hbm, wu_hbm, wd_hbm, o_ref,
                    wg_f, wu_f, wd_f, wg_s, wu_s, wd_s, sem, *, layer):
    i = pl.program_id(0)
    used = i < n_used_ref[0]
    e = blk_e_ref[i]
    new_expert = jnp.logical_or(i == 0, e != blk_e_ref[jnp.maximum(i - 1, 0)])

    def fetch(expert):
        return (pltpu.make_async_copy(wg_hbm.at[layer, expert], wg_f, sem.at[0]),
                pltpu.make_async_copy(wu_hbm.at[layer, expert], wu_f, sem.at[1]),
                pltpu.make_async_copy(wd_hbm.at[layer, expert], wd_f, sem.at[2]))

    @pl.when(i == 0)
    def _():
        for cp in fetch(e):
            cp.start()

    @pl.when(jnp.logical_and(used, new_expert))
    def _():
        for cp in fetch(e):
            cp.wait()
        wg_s[...] = wg_f[...].astype(BF16)
        wu_s[...] = wu_f[...].astype(BF16)
        wd_s[...] = wd_f[...].astype(BF16)

        @pl.when(next_e_ref[i] >= 0)
        def _():
            for cp in fetch(next_e_ref[i]):
                cp.start()

    @pl.when(used)
    def _():
        x = _unpack_bf16_pairs(x_ref[...]).astype(BF16)
        a = _dot(x, wg_s[...])
        u = _dot(x, wu_s[...])
        act = a * jax.nn.sigmoid(a) * u
        o_ref[...] = _pack_bf16_pairs(_dot(act.astype(BF16), wd_s[...]))

    @pl.when(i >= n_used_ref[0])
    def _():
        o_ref[...] = jnp.zeros_like(o_ref)


def _experts(layer, blk_e, n_used, next_e, xs, w_gate, w_up, w_down):
    R = xs.shape[0]
    D = 2 * xs.shape[1]
    n_blk = R // MOE_BLOCK
    rows = lambda i, be, nu, ne: (i, 0)
    grid_spec = pltpu.PrefetchScalarGridSpec(
        num_scalar_prefetch=3,
        grid=(n_blk,),
        in_specs=[
            pl.BlockSpec((MOE_BLOCK, D // 2), rows),
            pl.BlockSpec(memory_space=pl.ANY),
            pl.BlockSpec(memory_space=pl.ANY),
            pl.BlockSpec(memory_space=pl.ANY),
        ],
        out_specs=pl.BlockSpec((MOE_BLOCK, D // 2), rows),
        scratch_shapes=[
            pltpu.VMEM((D, D_EXPERT), F32),
            pltpu.VMEM((D, D_EXPERT), F32),
            pltpu.VMEM((D_EXPERT, D), F32),
            pltpu.VMEM((D, D_EXPERT), BF16),
            pltpu.VMEM((D, D_EXPERT), BF16),
            pltpu.VMEM((D_EXPERT, D), BF16),
            pltpu.SemaphoreType.DMA((3,)),
        ],
    )
    return pl.pallas_call(
        functools.partial(_experts_kernel, layer=layer),
        out_shape=jax.ShapeDtypeStruct((R, D // 2), jnp.int32),
        grid_spec=grid_spec,
        compiler_params=_cparams("arbitrary"),
        name="moe_experts",
    )(blk_e, n_used, next_e, xs, w_gate, w_up, w_down)


def _residual_tile(x_ref, pending_refs):
    if not pending_refs:
        return x_ref[0]
    g_ref, y0_ref, y1_ref, r_ref = pending_refs
    r = r_ref[0]
    y = _unpack_bf16_pairs(y0_ref[0, 0]) * r[:, 2:3] + _unpack_bf16_pairs(y1_ref[0, 0]) * r[:, 3:4]
    return x_ref[0] + g_ref[0] * y


def _pending_specs(D, tm):
    return [
        pl.BlockSpec((1, 1, D), lambda b, j: (b, 0, 0)),
        pl.BlockSpec((1, 1, tm, D // 2), lambda b, j: (0, b, j, 0)),
        pl.BlockSpec((1, 1, tm, D // 2), lambda b, j: (1, b, j, 0)),
        pl.BlockSpec((1, tm, LANES), lambda b, j: (b, j, 0)),
    ]


def _combine_kernel(x_ref, g_ref, y0_ref, y1_ref, r_ref, o_ref):
    o_ref[0] = _residual_tile(x_ref, (g_ref, y0_ref, y1_ref, r_ref))


def _combine(x, pending):
    B, S, D = x.shape
    tm = OUT_TILE
    row = lambda b, j: (b, j, 0)
    return pl.pallas_call(
        _combine_kernel,
        out_shape=jax.ShapeDtypeStruct((B, S, D), F32),
        grid=(B, S // tm),
        in_specs=[pl.BlockSpec((1, tm, D), row)] + _pending_specs(D, tm),
        out_specs=pl.BlockSpec((1, tm, D), row),
        compiler_params=_cparams("parallel", "parallel"),
        name="moe_combine",
    )(x, *pending)


SC_CORES = 2
SC_SUBCORES = 16
SC_WORKERS = SC_CORES * SC_SUBCORES
SC_CHUNK = 64


def _sc_mesh():
    return plsc.VectorSubcoreMesh(core_axis_name="c", subcore_axis_name="s",
                                  num_cores=SC_CORES, num_subcores=SC_SUBCORES)


def _sc_scatter_rows(src, idx, n_out):
    T, W = src.shape
    per_w = T // SC_WORKERS
    nch = per_w // SC_CHUNK
    idx4 = idx.reshape(TOP_K, SC_WORKERS, nch, SC_CHUNK)

    @functools.partial(
        pl.kernel, mesh=_sc_mesh(),
        out_type=jax.ShapeDtypeStruct((n_out, W), src.dtype),
        scratch_types=[
            pltpu.VMEM((TOP_K, nch, SC_CHUNK), jnp.int32),
            pltpu.VMEM((2, SC_CHUNK, W), src.dtype),
            pltpu.SemaphoreType.DMA((2,)),
            pltpu.SemaphoreType.DMA((2 * TOP_K,)),
        ],
        name="sc_scatter_rows",
    )
    def body(src_hbm, idx_hbm, out_hbm, idx_v, rows_v, load_sem, scatter_sem):
        wid = lax.axis_index("s") * SC_CORES + lax.axis_index("c")
        for s in range(TOP_K):
            pltpu.sync_copy(idx_hbm.at[s, wid], idx_v.at[s])

        def load(j, b):
            return pltpu.make_async_copy(
                src_hbm.at[pl.ds(wid * per_w + j * SC_CHUNK, SC_CHUNK)], rows_v.at[b], load_sem.at[b])

        def scatter(s, j, b):
            return pltpu.make_async_copy(rows_v.at[b], out_hbm.at[idx_v.at[s, j]], scatter_sem.at[2 * s + b])

        load(0, 0).start()

        @pl.loop(0, nch, step=2)
        def _(i):
            for b in range(2):
                j = i + b

                @pl.when(j >= 1)
                def _():
                    for s in range(TOP_K):
                        scatter(s, j - 1, 1 - b).wait()

                @pl.when(j + 1 < nch)
                def _():
                    load(j + 1, 1 - b).start()

                load(j, b).wait()
                for s in range(TOP_K):
                    scatter(s, j, b).start()

        for s in range(TOP_K):
            scatter(s, nch - 1, (nch - 1) % 2).wait()

    assert nch % 2 == 0
    return body(src, idx4)


def _sc_gather_rows(table, idx):
    N = idx.shape[0]
    W = table.shape[1]
    per_w = N // SC_WORKERS
    nch = per_w // SC_CHUNK
    idx3 = idx.reshape(SC_WORKERS, nch, SC_CHUNK)

    @functools.partial(
        pl.kernel, mesh=_sc_mesh(),
        out_type=jax.ShapeDtypeStruct((N, W), table.dtype),
        scratch_types=[
            pltpu.VMEM((nch, SC_CHUNK), jnp.int32),
            pltpu.VMEM((2, SC_CHUNK, W), table.dtype),
            pltpu.SemaphoreType.DMA((2,)),
            pltpu.SemaphoreType.DMA((2,)),
        ],
        name="sc_gather_rows",
    )
    def body(table_hbm, idx_hbm, out_hbm, idx_v, rows_v, gather_sem, write_sem):
        wid = lax.axis_index("s") * SC_CORES + lax.axis_index("c")
        pltpu.sync_copy(idx_hbm.at[wid], idx_v)

        def gather(j, b):
            return pltpu.make_async_copy(table_hbm.at[idx_v.at[j]], rows_v.at[b], gather_sem.at[b])

        def write(j, b):
            return pltpu.make_async_copy(
                rows_v.at[b], out_hbm.at[pl.ds(wid * per_w + j * SC_CHUNK, SC_CHUNK)], write_sem.at[b])

        gather(0, 0).start()

        @pl.loop(0, nch, step=2)
        def _(i):
            for b in range(2):
                j = i + b

                @pl.when(j >= 1)
                def _():
                    write(j - 1, 1 - b).wait()

                @pl.when(j + 1 < nch)
                def _():
                    gather(j + 1, 1 - b).start()

                gather(j, b).wait()
                write(j, b).start()

        write(nch - 1, (nch - 1) % 2).wait()

    assert nch % 2 == 0
    return body(table, idx3)


def _moe_dispatch(route_t, counts, T):
    A = T * TOP_K
    counts = counts[0, :N_EXPERTS].astype(jnp.int32)
    blocks_per = (counts + MOE_BLOCK - 1) // MOE_BLOCK
    block_end = jnp.cumsum(blocks_per)
    block_start = block_end - blocks_per
    expert = jnp.swapaxes(route_t[:, :TOP_K, :], 0, 1).reshape(TOP_K, T).astype(jnp.int32)
    rank = jnp.swapaxes(route_t[:, 4:4 + TOP_K, :], 0, 1).reshape(TOP_K, T).astype(jnp.int32)
    onehot = expert[None] == jnp.arange(N_EXPERTS, dtype=jnp.int32)[:, None, None]
    start = jnp.sum(jnp.where(onehot, block_start[:, None, None], 0), axis=0)
    dest = start * MOE_BLOCK + rank
    n_blk = -(-A // MOE_BLOCK) + N_EXPERTS
    blk = jnp.arange(n_blk, dtype=jnp.int32)
    blk_e = jnp.minimum(jnp.sum(blk[:, None] >= block_end[None, :], axis=-1), N_EXPERTS - 1).astype(jnp.int32)
    n_used = block_end[-1]
    first = jnp.logical_and(blk < n_used, jnp.logical_or(blk == 0, blk_e != jnp.roll(blk_e, 1)))
    first_pos = jnp.where(first, blk, n_blk)
    next_pos = jnp.concatenate([lax.cummin(first_pos, axis=0, reverse=True)[1:], jnp.full((1,), n_blk, jnp.int32)])
    next_e = jnp.where(next_pos < n_blk, blk_e[jnp.minimum(next_pos, n_blk - 1)], -1).astype(jnp.int32)
    return dest, n_blk * MOE_BLOCK, blk_e, n_used.reshape(1).astype(jnp.int32), next_e


def _rope_tables(S):
    inv = 1.0 / (ROPE_THETA ** (jnp.arange(0, HEAD_DIM, 2, dtype=F32) / HEAD_DIM))
    ang = jnp.arange(S, dtype=F32)[:, None] * inv[None, :]
    cos, sin = jnp.cos(ang), jnp.sin(ang)
    cos_h = jnp.concatenate([cos, cos], axis=-1)
    sin_h = jnp.concatenate([-sin, sin], axis=-1)
    return jnp.tile(cos_h, (1, ATTN_HEADS)), jnp.tile(sin_h, (1, ATTN_HEADS))


def _pad_cols(w, width):
    return jnp.pad(w, ((0, 0), (0, width - w.shape[1])))


def kernel(x, c, ada_w, ada_b, norm_mix, norm_ffn, hy_w_in, hy_q_norm, hy_k_norm, hy_conv_w, hy_w_out, ml_w_in, ml_b_gates, ml_out_norm, ml_w_out, moe_w_group, moe_b_group, moe_w_expert, moe_b_expert, moe_w_gate, moe_w_up, moe_w_down):
    B, S, D = x.shape
    T = B * S
    cos_t, sin_t = _rope_tables(S)
    mod = _ada_modulation(c, ada_w, ada_b).reshape(DEPTH, B, 6, 1, D)
    r_i = np.arange(ATTN_WIDTH)
    grp = jnp.asarray((r_i[:, None] // HEAD_DIM) == (r_i[None, :] // HEAD_DIM), dtype=BF16)

    pending = ()
    for l in range(DEPTH):
        sh1, sc1, g1, sh2, sc2, g2 = [mod[l, :, i] for i in range(6)]
        gain1 = norm_mix[l].reshape(1, D)
        w_r = _pad_cols(jnp.concatenate([moe_w_group[l], moe_w_expert[l]], axis=1), LANES).astype(BF16)
        b_r = jnp.pad(jnp.concatenate([moe_b_group[l], moe_b_expert[l]]), (0, LANES - N_GROUPS - N_EXPERTS))
        route_args = (norm_ffn[l].reshape(1, D), sc2, sh2, w_r, b_r.reshape(1, LANES))
        j = l // 2
        if l % 2 == 0:
            w = hy_w_in[j]
            o = np.cumsum((0,) + (ATTN_WIDTH, HEAD_DIM, HEAD_DIM, IDX_HEADS * IDX_DIM, IDX_DIM, IDX_HEADS,
                                  CONV_WIDTH, CONV_WIDTH, CONV_WIDTH))
            wq, wk, wv, wiq, wik, wiw, wbg, wcg, wu = [w[:, o[i]:o[i + 1]] for i in range(9)]
            w_pad = jnp.concatenate(
                [wk, wv, _pad_cols(jnp.concatenate([wik, wiw], axis=1), LANES), wq, wiq, wbg, wcg, wu],
                axis=1).astype(BF16)
            qn_t = jnp.tile(hy_q_norm[j], ATTN_HEADS).reshape(1, ATTN_WIDTH)
            kn_t = jnp.tile(hy_k_norm[j], LANES // HEAD_DIM).reshape(1, LANES)
            outs = _hyb_in(x, pending, gain1, sc1, sh1, w_pad, cos_t, sin_t, qn_t, kn_t, grp)
            qt, iqt, bcu, kv, kvt, ik, iwt = outs[:7]
            if pending:
                x = outs[7]
            y_attn = _dsa_attention_t(qt, iqt, iwt, kv, kvt, ik)
            x, h2, route, route_t, counts =_hyb_out(y_attn, bcu, hy_conv_w[j], hy_w_out[j].astype(BF16), x, g1, route_args)
        else:
            w = ml_w_in[j]
            hq = ML_HEADS * ML_QK_DIM
            hv = ML_HEADS * ML_V_DIM
            wq, wk, wv = w[:, :hq], w[:, hq:2 * hq], w[:, 2 * hq:2 * hq + hv]
            wg = w[:, 2 * hq + hv:2 * hq + hv + 2 * ML_HEADS]
            wo = w[:, 2 * hq + hv + 2 * ML_HEADS:]
            w_pad = jnp.concatenate([_pad_cols(wg, LANES), wk, wq, wv, wo], axis=1).astype(BF16)
            gate_bias = jnp.pad(ml_b_gates[j], (0, LANES - 2 * ML_HEADS)).reshape(1, LANES)
            q, k, v, og, g_cols, g_rows, x = _ml_in(x, pending, gain1, sc1, sh1, w_pad, gate_bias)
            assert ML_STEP_HEADS == ML_HEADS
            hh = _mlstm(q, k, v, g_rows[:, None], g_cols[:, None], ml_out_norm[j].reshape(1, hv))
            x, h2, route, route_t, counts =_ml_out(hh, og, ml_w_out[j].astype(BF16), x, g1, route_args)

        dest, n_rows, blk_e, n_used, next_e = _moe_dispatch(route_t, counts, T)
        xs = _sc_scatter_rows(h2.reshape(T, D // 2), dest, n_rows)
        ys = _experts(l, blk_e, n_used, next_e, xs, moe_w_gate, moe_w_up, moe_w_down)
        y01 = _sc_gather_rows(ys, dest.reshape(TOP_K * T)).reshape(TOP_K, B, S, D // 2)
        pending = (g2, y01, y01, route)
    return _combine(x, pending)
```

```python
import functools

import numpy as np
import jax
import jax.numpy as jnp
from jax import lax
from jax.experimental import pallas as pl
from jax.experimental.pallas import tpu as pltpu
from jax.experimental.pallas import tpu_sc as plsc

F32 = jnp.float32
BF16 = jnp.bfloat16
HIGHEST = lax.Precision.HIGHEST

D_MODEL = 1024
DEPTH = 4
ATTN_HEADS = 8
HEAD_DIM = 64
ATTN_WIDTH = ATTN_HEADS * HEAD_DIM
IDX_HEADS = 8
IDX_DIM = 64
INDEX_TOPK = 256
Q_BLOCK = 256
ROPE_THETA = 10000.0
CONV_WIDTH = D_MODEL - ATTN_WIDTH
CONV_K = 3
ML_HEADS = 8
ML_QK_DIM = 64
ML_V_DIM = 128
N_GROUPS = 4
EXPERTS_PER_GROUP = 8
N_EXPERTS = N_GROUPS * EXPERTS_PER_GROUP
TOP_K = 2
D_EXPERT = 512
MOE_BLOCK = 512
NORM_EPS = 1e-6

LANES = 128
VMEM_LIMIT = 56 * 1024 * 1024
TOKEN_TILE = 512
OUT_TILE = 1024
ML_CHUNK = 256
ML_STEP_HEADS = 8
NEG_INF = float("-inf")


def _cparams(*sem):
    return pltpu.CompilerParams(dimension_semantics=sem, vmem_limit_bytes=VMEM_LIMIT)


def _dot(a, b):
    return jnp.dot(a, b, preferred_element_type=F32)


def _pack_bf16_pairs(x):
    bits = lax.bitcast_convert_type(x.astype(BF16).astype(F32), jnp.uint32)
    half = bits.shape[1] // 2
    packed = (bits[:, :half] >> 16) | (bits[:, half:] & jnp.uint32(0xFFFF0000))
    return lax.bitcast_convert_type(packed, jnp.int32)


def _unpack_bf16_pairs(words):
    words = lax.bitcast_convert_type(words, jnp.uint32)
    return jnp.concatenate(
        [lax.bitcast_convert_type(words << 16, F32),
         lax.bitcast_convert_type(words & jnp.uint32(0xFFFF0000), F32)], axis=1)


def _split_dot(a_f32, b_bf16):
    hi = a_f32.astype(BF16)
    lo = (a_f32 - hi.astype(F32)).astype(BF16)
    return _dot(hi, b_bf16) + _dot(lo, b_bf16)


def _ada_kernel(c_ref, w_ref, b_ref, o_ref):
    c = c_ref[...]
    ca = c * jax.nn.sigmoid(c)
    o_ref[0] = jnp.dot(ca, w_ref[0], precision=HIGHEST, preferred_element_type=F32) + b_ref[0]


def _ada_modulation(c, ada_w, ada_b):
    B, D = c.shape
    n_col = ada_w.shape[-1] // D
    return pl.pallas_call(
        _ada_kernel,
        out_shape=jax.ShapeDtypeStruct((DEPTH, B, n_col * D), F32),
        grid=(DEPTH, n_col),
        in_specs=[
            pl.BlockSpec((B, D), lambda l, j: (0, 0)),
            pl.BlockSpec((1, D, D), lambda l, j: (l, 0, j)),
            pl.BlockSpec((1, 1, D), lambda l, j: (l, 0, j)),
        ],
        out_specs=pl.BlockSpec((1, B, D), lambda l, j: (l, 0, j)),
        compiler_params=_cparams("parallel", "parallel"),
        name="ada_modulation",
    )(c, ada_w, ada_b.reshape(DEPTH, 1, n_col * D))


def _modulated_norm(x, gain, scale, shift):
    y = x * lax.rsqrt(jnp.mean(x * x, axis=-1, keepdims=True) + NORM_EPS)
    return y * gain * (1.0 + scale) + shift


def _rope(x, cos, sin_signed, first_half):
    w = x.shape[-1]
    partner = jnp.where(first_half, pltpu.roll(x, w - HEAD_DIM // 2, 1), pltpu.roll(x, HEAD_DIM // 2, 1))
    return x * cos + partner * sin_signed


def _hyb_in_kernel(*refs, n_pending):
    x_ref, pending_refs, refs = refs[0], refs[1:1 + n_pending], refs[1 + n_pending:]
    (gain_ref, sc_ref, sh_ref, w_small_ref, w_q_ref, w_iq_ref, w_bcu_ref, cos_ref, sin_ref, qn_ref, kn_ref, grp_ref,
     qt_ref, iqt_ref, bcu_ref, kv_ref, kvt_ref, ik_ref, iwt_ref) = refs[:19]
    x = _residual_tile(x_ref, pending_refs)
    if n_pending:
        refs[19][0] = x
    h = _modulated_norm(x, gain_ref[...], sc_ref[0], sh_ref[0])
    hb = h.astype(BF16)
    p_small = _dot(hb, w_small_ref[...])
    p_q = _dot(hb, w_q_ref[...])
    p_iq = _dot(hb, w_iq_ref[...])
    cos = cos_ref[...]
    sin = sin_ref[...]
    lane = lax.broadcasted_iota(jnp.int32, (1, ATTN_WIDTH), 1)
    first_half = (lane % HEAD_DIM) < (HEAD_DIM // 2)
    fh128 = first_half[:, :LANES]
    lane128 = lane[:, :LANES]

    kv = p_small[:, :LANES]
    is_k = lane128 < HEAD_DIM
    kk = jnp.where(is_k, kv, 0.0)
    ms_k = jnp.sum(kk * kk, axis=-1, keepdims=True) * (1.0 / HEAD_DIM)
    kn = kv * lax.rsqrt(ms_k + NORM_EPS) * kn_ref[...]
    kr = _rope(kn, cos[:, :LANES], sin[:, :LANES], fh128)
    kv = jnp.where(is_k, kr, kv)
    kv_ref[0] = kv.astype(BF16)
    feat = lax.broadcasted_iota(jnp.int32, (LANES, 1), 0)
    kv_t = jnp.where(feat < HEAD_DIM, 1.0, kv.T).astype(BF16)
    for i in range(kvt_ref.shape[1]):
        kvt_ref[0, i] = kv_t[:, i * DSA_KEY_CHUNK:(i + 1) * DSA_KEY_CHUNK]

    sm = p_small[:, LANES:]
    ikr = _rope(sm, cos[:, :LANES], sin[:, :LANES], fh128)
    ik_ref[0] = jnp.where(is_k, ikr, 0.0).astype(BF16)
    iwt_ref[0] = sm.T

    ms = _split_dot(p_q * p_q, grp_ref[...]) * (1.0 / HEAD_DIM)
    q = p_q * lax.rsqrt(ms + NORM_EPS) * qn_ref[...]
    qt_ref[0] = (_rope(q, cos, sin, first_half) * (HEAD_DIM ** -0.5)).T.astype(BF16)
    iqt_ref[0] = (_rope(p_iq, cos, sin, first_half) * (IDX_DIM ** -0.5)).T.astype(BF16)

    bcu_ref[0] = _dot(hb, w_bcu_ref[...])


def _hyb_in(x, pending, gain, sc, sh, weights, cos_t, sin_t, qn_t, kn_t, grp):
    B, S, D = x.shape
    tm = TOKEN_TILE
    row = lambda b, j: (b, j, 0)
    per_b = lambda b, j: (b, 0, 0)
    const2 = lambda b, j: (0, 0)
    tab = lambda b, j: (j, 0)
    col = lambda b, j: (b, 0, j)
    assert tm % DSA_KEY_CHUNK == 0
    x_out_shape = (jax.ShapeDtypeStruct((B, S, D), F32),) if pending else ()
    x_out_spec = (pl.BlockSpec((1, tm, D), row),) if pending else ()
    return pl.pallas_call(
        functools.partial(_hyb_in_kernel, n_pending=len(pending)),
        out_shape=(
            jax.ShapeDtypeStruct((B, 512, S), BF16),
            jax.ShapeDtypeStruct((B, 512, S), BF16),
            jax.ShapeDtypeStruct((B, S, 1536), F32),
            jax.ShapeDtypeStruct((B, S, LANES), BF16),
            jax.ShapeDtypeStruct((B, S // DSA_KEY_CHUNK, LANES, DSA_KEY_CHUNK), BF16),
            jax.ShapeDtypeStruct((B, S, LANES), BF16),
            jax.ShapeDtypeStruct((B, LANES, S), F32),
        ) + x_out_shape,
        grid=(B, S // tm),
        in_specs=[pl.BlockSpec((1, tm, D), row)] + (_pending_specs(D, tm) if pending else []) + [
            pl.BlockSpec((1, D), const2),
            pl.BlockSpec((1, 1, D), per_b),
            pl.BlockSpec((1, 1, D), per_b),
            pl.BlockSpec((D, 2 * LANES), const2),
            pl.BlockSpec((D, 512), const2),
            pl.BlockSpec((D, 512), const2),
            pl.BlockSpec((D, 1536), const2),
            pl.BlockSpec((tm, 512), tab),
            pl.BlockSpec((tm, 512), tab),
            pl.BlockSpec((1, 512), const2),
            pl.BlockSpec((1, LANES), const2),
            pl.BlockSpec((512, 512), const2),
        ],
        out_specs=(
            pl.BlockSpec((1, 512, tm), col),
            pl.BlockSpec((1, 512, tm), col),
            pl.BlockSpec((1, tm, 1536), row),
            pl.BlockSpec((1, tm, LANES), row),
            pl.BlockSpec((1, tm // DSA_KEY_CHUNK, LANES, DSA_KEY_CHUNK), lambda b, j: (b, j, 0, 0)),
            pl.BlockSpec((1, tm, LANES), row),
            pl.BlockSpec((1, LANES, tm), col),
        ) + x_out_spec,
        compiler_params=_cparams("parallel", "parallel"),
        name="hybrid_in_proj",
    )(x, *pending, gain, sc, sh, *weights, cos_t, sin_t, qn_t, kn_t, grp)


DSA_KEY_CHUNK = 512
DSA_SUM_ROWS = 16


def _fold8(x, op):
    parts = x.reshape(x.shape[0] // 8, 8, x.shape[1])
    while parts.shape[0] > 1:
        half = parts.shape[0] // 2
        assert parts.shape[0] == 2 * half
        parts = op(parts[:half], parts[half:])
    return parts[0]


def _col_reduce(x, op):
    t = _fold8(x, op)
    for shift in (4, 2, 1):
        t = op(t, pltpu.roll(t, shift, 0))
    return t[0:1, :]


def _dsa_t_kernel(qt_ref, iqt_ref, iwt_ref, kv_ref, kvt_ref, ik_ref, o_ref, sc_ref, bias_ref, acc_ref):
    CK = DSA_KEY_CHUNK
    QB = Q_BLOCK
    qb = pl.program_id(1)
    nk = (qb * QB + QB + CK - 1) // CK
    kf = float(INDEX_TOPK)
    qpos = qb * QB + lax.broadcasted_iota(jnp.int32, (1, QB), 1)
    krow = lax.broadcasted_iota(jnp.int32, (CK, 1), 0)
    w_idx = iwt_ref[0, IDX_DIM:IDX_DIM + IDX_HEADS, :] * (IDX_HEADS ** -0.5)

    def rows(c):
        return pl.ds(pl.multiple_of(c * CK, CK), CK)

    def heads_on_lanes(ref, width):
        return jnp.concatenate([ref[0, hd * width:(hd + 1) * width, :] for hd in range(ref.shape[1] // width)], axis=1)

    def head_lanes(hd):
        return slice(hd * QB, (hd + 1) * QB)

    iq_wide = heads_on_lanes(iqt_ref, IDX_DIM)
    w_wide = jnp.concatenate([w_idx[hd:hd + 1, :] for hd in range(IDX_HEADS)], axis=1)

    def score_chunk(c, carry):
        mx, mn = carry
        ikc = ik_ref[0, rows(c), :][:, :IDX_DIM]
        s_all = jnp.maximum(_dot(ikc, iq_wide), 0.0) * w_wide
        acc = s_all[:, head_lanes(0)]
        for hd in range(1, IDX_HEADS):
            acc = acc + s_all[:, head_lanes(hd)]
        causal = (c * CK + krow) <= qpos
        sc_ref[rows(c), :] = jnp.where(causal, acc, NEG_INF)
        mx = jnp.maximum(mx, _fold8(jnp.where(causal, acc, NEG_INF), jnp.maximum))
        mn = jnp.minimum(mn, _fold8(jnp.where(causal, acc, jnp.inf), jnp.minimum))
        return mx, mn

    mx8, mn8 = lax.fori_loop(0, nk, score_chunk,
                             (jnp.full((8, QB), NEG_INF, F32), jnp.full((8, QB), jnp.inf, F32)))
    row_max = jnp.max(mx8, axis=0, keepdims=True)
    row_min = jnp.min(mn8, axis=0, keepdims=True)

    def count(pred):
        def body(c, part):
            return part + _fold8(jnp.where(pred(sc_ref[rows(c), :]), 1.0, 0.0), jnp.add)
        part = lax.fori_loop(0, nk, body, jnp.zeros((8, QB), F32))
        return jnp.sum(part, axis=0, keepdims=True)

    @pl.when(qb * QB + QB <= INDEX_TOPK)
    def _():
        def body(c, carry):
            bias_ref[rows(c), :] = jnp.where(sc_ref[rows(c), :] > NEG_INF, 0.0, NEG_INF)
            return carry
        lax.fori_loop(0, nk, body, 0)

    @pl.when(qb * QB + QB > INDEX_TOPK)
    def _():
        def bisect(_, carry):
            lo, hi, c_lo, c_hi = carry
            mid = 0.5 * lo + 0.5 * jnp.minimum(hi, row_max)
            cnt = count(lambda x: x >= mid)
            ge = cnt >= kf
            return (jnp.where(ge, mid, lo), jnp.where(ge, hi, mid),
                    jnp.where(ge, cnt, c_lo), jnp.where(ge, c_hi, cnt))

        n_adm = (qpos + 1).astype(F32)
        lo, hi, c_lo, c_hi = lax.fori_loop(
            0, 18, bisect, (row_min, jnp.full((1, QB), jnp.inf, F32), n_adm, jnp.zeros((1, QB), F32)))

        def refine_cond(carry):
            it, _, _, _, done = carry
            return jnp.logical_and(it < nk * CK, jnp.min(done) < 0.5)

        def refine(carry):
            it, hi, c_hi, thr, done = carry

            def edges(c, part):
                up, dn = part
                x = sc_ref[rows(c), :]
                return (jnp.maximum(up, _fold8(jnp.where(x < hi, x, NEG_INF), jnp.maximum)),
                        jnp.minimum(dn, _fold8(jnp.where(x >= lo, x, jnp.inf), jnp.minimum)))

            up8, dn8 = lax.fori_loop(0, nk, edges,
                                     (jnp.full((8, QB), NEG_INF, F32), jnp.full((8, QB), jnp.inf, F32)))
            m_up = jnp.max(up8, axis=0, keepdims=True)
            m_dn = jnp.min(dn8, axis=0, keepdims=True)
            from_hi = c_hi == kf - 1.0
            from_lo = c_lo == kf
            thr = jnp.where(done > 0.5, thr, jnp.where(from_hi, m_up, m_dn))
            done = jnp.where(jnp.logical_or(from_hi, from_lo), 1.0, done)

            def step_down(args):
                hi, c_hi, thr, done = args
                open_ = done < 0.5
                cnt = count(lambda x: x >= m_up)
                hit = jnp.logical_and(open_, cnt >= kf)
                moved = jnp.logical_and(open_, cnt < kf)
                return (jnp.where(moved, m_up, hi), jnp.where(moved, cnt, c_hi), jnp.where(hit, m_up, thr),
                        jnp.where(hit, 1.0, done))

            hi, c_hi, thr, done = lax.cond(jnp.min(done) > 0.5, lambda args: args, step_down, (hi, c_hi, thr, done))
            return it + 1, hi, c_hi, thr, done

        _, _, _, thr, _ = lax.while_loop(
            refine_cond, refine, (jnp.int32(0), hi, c_hi, row_max, jnp.zeros((1, QB), F32)))

        def body(c, sel):
            keep = sc_ref[rows(c), :] >= thr
            bias_ref[rows(c), :] = jnp.where(keep, 0.0, NEG_INF)
            return sel + _fold8(jnp.where(keep, 1.0, 0.0), jnp.add)

        n_sel = jnp.sum(lax.fori_loop(0, nk, body, jnp.zeros((8, QB), F32)), axis=0, keepdims=True)

        @pl.when(jnp.max(n_sel) > kf + 0.5)
        def _():
            need = kf - count(lambda x: x > thr)
            r_i = lax.broadcasted_iota(jnp.int32, (CK, CK), 0)
            c_i = lax.broadcasted_iota(jnp.int32, (CK, CK), 1)
            lower = jnp.where(c_i <= r_i, 1.0, 0.0).astype(BF16)

            def body(c, seen):
                x = sc_ref[rows(c), :]
                eq = x == thr
                eq_f = jnp.where(eq, 1.0, 0.0)
                rank = _dot(lower, eq_f.astype(BF16)) + seen
                keep = jnp.logical_or(x > thr, jnp.logical_and(eq, rank <= need))
                bias_ref[rows(c), :] = jnp.where(keep, 0.0, NEG_INF)
                return seen + jnp.sum(eq_f, axis=0, keepdims=True)

            lax.fori_loop(0, nk, body, jnp.zeros((1, QB), F32))

    acc_ref[...] = jnp.zeros_like(acc_ref)
    q_wide = heads_on_lanes(qt_ref, HEAD_DIM)

    def attend_chunk(c, m_old):
        kc = kv_ref[0, rows(c), :][:, :HEAD_DIM]
        bias = bias_ref[rows(c), :]
        logits = _dot(kc, q_wide) + jnp.concatenate([bias] * ATTN_HEADS, axis=1)
        m_new = jnp.maximum(m_old, _col_reduce(logits, jnp.maximum))
        alpha = jnp.exp(m_old - m_new)
        p = jnp.exp(logits - m_new)
        acc_ref[...] = alpha * acc_ref[...] + _dot(kvt_ref[0, c, HEAD_DIM - DSA_SUM_ROWS:, :], p.astype(BF16))
        return m_new

    lax.fori_loop(0, nk, attend_chunk, jnp.full((1, ATTN_HEADS * QB), -1e30, F32))
    out_t = acc_ref[DSA_SUM_ROWS:, :] / acc_ref[0:1, :]
    o_ref[0] = jnp.concatenate([out_t[:, head_lanes(hd)] for hd in range(ATTN_HEADS)], axis=0).T


def _dsa_attention_t(qt, iqt, iwt, kv, kvt, ik):
    B, _, S = qt.shape
    col = lambda b, j: (b, 0, j)
    per_b = lambda b, j: (b, 0, 0)
    return pl.pallas_call(
        _dsa_t_kernel,
        out_shape=jax.ShapeDtypeStruct((B, S, ATTN_WIDTH), F32),
        grid=(B, S // Q_BLOCK),
        in_specs=[
            pl.BlockSpec((1, ATTN_WIDTH, Q_BLOCK), col),
            pl.BlockSpec((1, IDX_HEADS * IDX_DIM, Q_BLOCK), col),
            pl.BlockSpec((1, LANES, Q_BLOCK), col),
            pl.BlockSpec((1, S, LANES), per_b),
            pl.BlockSpec((1, S // DSA_KEY_CHUNK, LANES, DSA_KEY_CHUNK), lambda b, j: (b, 0, 0, 0)),
            pl.BlockSpec((1, S, LANES), per_b),
        ],
        out_specs=pl.BlockSpec((1, Q_BLOCK, ATTN_WIDTH), lambda b, j: (b, j, 0)),
        scratch_shapes=[
            pltpu.VMEM((S, Q_BLOCK), F32),
            pltpu.VMEM((S, Q_BLOCK), F32),
            pltpu.VMEM((DSA_SUM_ROWS + HEAD_DIM, ATTN_HEADS * Q_BLOCK), F32),
        ],
        compiler_params=_cparams("parallel", "parallel"),
        name="dsa_attention",
    )(qt, iqt, iwt, kv, kvt, ik)


def _hyb_out_kernel(ya_ref, bcu_ref, halo_ref, cw_ref, w_ref, x_ref, g_ref,
                    gain2_ref, sc2_ref, sh2_ref, wr_ref, br_ref, o_ref, h_ref, r_ref, rt_ref, cnt_ref, run_ref, tri_ref):
    j = pl.program_id(1)
    tm = ya_ref.shape[1]
    bcu = bcu_ref[0]
    bg = bcu[:, 0:512]
    z = bcu[:, 512:1024] * bcu[:, 1024:1536]
    halo = halo_ref[0]
    zh = halo[:, 512:1024] * halo[:, 1024:1536]
    zh = jnp.where(j > 0, zh, 0.0)
    row = lax.broadcasted_iota(jnp.int32, (tm, 1), 0)
    z1 = jnp.where(row >= 1, pltpu.roll(z, 1, 0), zh[7:8, :])
    z2 = jnp.where(row >= 2, pltpu.roll(z, 2, 0), jnp.where(row == 1, zh[7:8, :], zh[6:7, :]))
    cw = cw_ref[...]
    y_conv = bg * (z2 * cw[0:1, :] + z1 * cw[1:2, :] + z * cw[2:3, :])
    y = _dot(ya_ref[0].astype(BF16), w_ref[0:512, :]) + _dot(y_conv.astype(BF16), w_ref[512:1024, :])
    x_new = x_ref[0] + g_ref[0] * y
    o_ref[0] = x_new
    _route_tile(x_new, gain2_ref, sc2_ref, sh2_ref, wr_ref, br_ref, h_ref, r_ref, rt_ref, cnt_ref, run_ref, tri_ref)


def _hyb_out(y_attn, bcu, conv_w, w_out_bf, x, g1, route_args):
    B, S, D = x.shape
    tm = OUT_TILE
    row = lambda b, j: (b, j, 0)
    per_b = lambda b, j: (b, 0, 0)
    const2 = lambda b, j: (0, 0)
    halo = lambda b, j: (b, jnp.maximum(j * (tm // 8) - 1, 0), 0)
    r_in, r_shape, r_out, r_scratch = _route_specs(B, S, D, tm)
    return pl.pallas_call(
        _hyb_out_kernel,
        out_shape=(jax.ShapeDtypeStruct((B, S, D), F32),) + r_shape,
        grid=(B, S // tm),
        in_specs=[
            pl.BlockSpec((1, tm, 512), row),
            pl.BlockSpec((1, tm, 1536), row),
            pl.BlockSpec((1, 8, 1536), halo),
            pl.BlockSpec((CONV_K, CONV_WIDTH), const2),
            pl.BlockSpec((D, D), const2),
            pl.BlockSpec((1, tm, D), row),
            pl.BlockSpec((1, 1, D), per_b),
        ] + r_in,
        out_specs=(pl.BlockSpec((1, tm, D), row),) + r_out,
        scratch_shapes=r_scratch,
        compiler_params=_cparams("arbitrary", "arbitrary"),
        name="hybrid_out_proj",
    )(y_attn, bcu, bcu, conv_w, w_out_bf, x, g1, *route_args)


def _ml_in_kernel(x_ref, g2_ref, y0_ref, y1_ref, r_ref, gain_ref, sc_ref, sh_ref,
                  w_g_ref, w_k_ref, w_q_ref, w_v_ref, w_og_ref, bias_ref,
                  q_ref, k_ref, v_ref, og_ref, gt_ref, gtt_ref, xo_ref):
    x = _residual_tile(x_ref, (g2_ref, y0_ref, y1_ref, r_ref))
    xo_ref[0] = x
    h = _modulated_norm(x, gain_ref[...], sc_ref[0], sh_ref[0])
    hb = h.astype(BF16)
    gates = _dot(hb, w_g_ref[...]) + bias_ref[...]
    gates_t = gates.T[:2 * ML_HEADS, :]
    keys_t = _dot(hb, w_k_ref[...]).T.astype(BF16)
    for i in range(gt_ref.shape[1]):
        span = slice(i * ML_CHUNK, (i + 1) * ML_CHUNK)
        gt_ref[0, i] = gates[span, :]
        gtt_ref[0, i] = gates_t[:, span]
        k_ref[0, i] = keys_t[:, span]
    q_ref[0] = (_dot(hb, w_q_ref[...]) * (ML_QK_DIM ** -0.5)).astype(BF16)
    v_ref[0] = _dot(hb, w_v_ref[...]).astype(BF16)
    og_ref[0] = _dot(hb, w_og_ref[...])


def _ml_in(x, pending, gain, sc, sh, weights, gate_bias):
    B, S, D = x.shape
    tm = TOKEN_TILE
    L = ML_CHUNK
    assert tm % L == 0
    per_tile = tm // L
    chunked = lambda b, j: (b, j, 0, 0)
    row = lambda b, j: (b, j, 0)
    per_b = lambda b, j: (b, 0, 0)
    const2 = lambda b, j: (0, 0)
    return pl.pallas_call(
        _ml_in_kernel,
        out_shape=(
            jax.ShapeDtypeStruct((B, S, 512), BF16),
            jax.ShapeDtypeStruct((B, S // L, 512, L), BF16),
            jax.ShapeDtypeStruct((B, S, 1024), BF16),
            jax.ShapeDtypeStruct((B, S, 1024), F32),
            jax.ShapeDtypeStruct((B, S // L, L, LANES), F32),
            jax.ShapeDtypeStruct((B, S // L, 2 * ML_HEADS, L), F32),
            jax.ShapeDtypeStruct((B, S, D), F32),
        ),
        grid=(B, S // tm),
        in_specs=[pl.BlockSpec((1, tm, D), row)] + _pending_specs(D, tm) + [
            pl.BlockSpec((1, D), const2),
            pl.BlockSpec((1, 1, D), per_b),
            pl.BlockSpec((1, 1, D), per_b),
            pl.BlockSpec((D, LANES), const2),
            pl.BlockSpec((D, 512), const2),
            pl.BlockSpec((D, 512), const2),
            pl.BlockSpec((D, 1024), const2),
            pl.BlockSpec((D, 1024), const2),
            pl.BlockSpec((1, LANES), const2),
        ],
        out_specs=(
            pl.BlockSpec((1, tm, 512), row),
            pl.BlockSpec((1, per_tile, 512, L), chunked),
            pl.BlockSpec((1, tm, 1024), row),
            pl.BlockSpec((1, tm, 1024), row),
            pl.BlockSpec((1, per_tile, L, LANES), chunked),
            pl.BlockSpec((1, per_tile, 2 * ML_HEADS, L), chunked),
            pl.BlockSpec((1, tm, D), row),
        ),
        compiler_params=_cparams("parallel", "parallel"),
        name="mlstm_in_proj",
    )(x, *pending, gain, sc, sh, *weights, gate_bias)


def _log_sigmoid(f):
    return jnp.minimum(f, 0.0) - jnp.log1p(jnp.exp(-jnp.abs(f)))


def _split3(x):
    a = x.astype(BF16)
    r = x - a.astype(F32)
    b = r.astype(BF16)
    c = (r - b.astype(F32)).astype(BF16)
    return a, b, c


def _twice(a):
    return jnp.concatenate([a, a], axis=1)


def _over_lanes(a, width):
    return jnp.concatenate([a] * (width // LANES), axis=1)


def _mlstm_kernel(q_ref, kt_ref, v_ref, grow_ref, gcol_ref, gain_ref, o_ref, c_ref, m_ref):
    L = ML_CHUNK
    HP = ML_STEP_HEADS
    S = q_ref.shape[1]
    c_ref[...] = jnp.zeros_like(c_ref)
    m_ref[...] = jnp.zeros_like(m_ref)

    def chunk(c, carry):
        r0 = pl.multiple_of(c * L, L)
        r_i = lax.broadcasted_iota(jnp.int32, (L, L), 0)
        c_i = lax.broadcasted_iota(jnp.int32, (L, L), 1)
        tril = c_i <= r_i
        lower = jnp.where(tril, 1.0, 0.0).astype(BF16)
        upper = jnp.where(r_i <= c_i, 1.0, 0.0).astype(BF16)
        e_r = lax.broadcasted_iota(jnp.int32, (LANES, HP * LANES), 0)
        e_c = lax.broadcasted_iota(jnp.int32, (LANES, HP * LANES), 1)
        pick = jnp.where(e_r == HP + e_c // LANES, 1.0, 0.0).astype(BF16)
        rows = grow_ref[0, 0, c]
        cols = gcol_ref[0, 0, c]
        b_rows = sum(_dot(p, upper) for p in _split3(_log_sigmoid(rows)))
        b_cols = sum(_dot(lower, p) for p in _split3(_log_sigmoid(cols)))
        b_colr = sum(_dot(p, pick) for p in _split3(b_cols))
        lane = lax.broadcasted_iota(jnp.int32, (1, L), 1)
        b_last_all = jnp.sum(jnp.where(lane == L - 1, b_rows, 0.0), axis=-1, keepdims=True)
        ones_v = jnp.ones((L, ML_V_DIM), BF16)
        for hh in range(HP):
            q = q_ref[0, pl.ds(r0, L), hh * ML_QK_DIM:(hh + 1) * ML_QK_DIM]
            kt = kt_ref[0, c, hh * ML_QK_DIM:(hh + 1) * ML_QK_DIM, :]
            v = v_ref[0, pl.ds(r0, L), hh * ML_V_DIM:(hh + 1) * ML_V_DIM]
            vx = jnp.concatenate([v, ones_v], axis=1)
            i_row = rows[hh:hh + 1, :]
            b_row = b_rows[HP + hh:HP + hh + 1, :]
            b_last = b_last_all[HP + hh:HP + hh + 1, :]
            b_col = b_colr[:, hh * LANES:(hh + 1) * LANES]
            m_prev = m_ref[hh]
            ctn = c_ref[hh]

            dmat = jnp.where(tril, _over_lanes(b_col, L) - b_row + i_row, NEG_INF)
            inter = b_col + m_prev
            m_t = jnp.maximum(inter, jnp.max(dmat, axis=-1, keepdims=True))
            w_intra = jnp.exp(dmat - _over_lanes(m_t, L))
            w_inter = jnp.exp(inter - m_t)
            intra = (w_intra * _dot(q, kt)).astype(BF16)
            tot = _twice(w_inter) * _dot(q, ctn.astype(BF16)) + _dot(intra, vx)
            num = tot[:, :ML_V_DIM]
            den = tot[:, ML_V_DIM:]
            hc = num / jnp.maximum(jnp.abs(den), jnp.exp(-m_t))
            y = hc * lax.rsqrt(jnp.mean(hc * hc, axis=-1, keepdims=True) + NORM_EPS)
            o_ref[0, pl.ds(r0, L), hh * ML_V_DIM:(hh + 1) * ML_V_DIM] = (
                y * gain_ref[:, hh * ML_V_DIM:(hh + 1) * ML_V_DIM])

            g_row = b_last - b_row + i_row
            m_new = jnp.maximum(b_last + m_prev, jnp.max(g_row, axis=-1, keepdims=True))
            decay = jnp.exp(b_last + m_prev - m_new)
            kw = (kt.astype(F32) * jnp.exp(g_row - _over_lanes(m_new, L))).astype(BF16)
            c_ref[hh] = _twice(decay) * ctn + _dot(kw, vx)
            m_ref[hh] = m_new
        return carry

    lax.fori_loop(0, S // L, chunk, 0)


def _mlstm(q, kt, v, g_rows, g_cols, out_gain):
    B, S, _ = q.shape
    assert ML_CHUNK % LANES == 0
    nc = S // ML_CHUNK
    hp = ML_STEP_HEADS
    return pl.pallas_call(
        _mlstm_kernel,
        out_shape=jax.ShapeDtypeStruct((B, S, ML_HEADS * ML_V_DIM), F32),
        grid=(B, ML_HEADS // hp),
        in_specs=[
            pl.BlockSpec((1, S, hp * ML_QK_DIM), lambda b, p: (b, 0, p)),
            pl.BlockSpec((1, nc, hp * ML_QK_DIM, ML_CHUNK), lambda b, p: (b, 0, p, 0)),
            pl.BlockSpec((1, S, hp * ML_V_DIM), lambda b, p: (b, 0, p)),
            pl.BlockSpec((1, 1, nc, 2 * hp, ML_CHUNK), lambda b, p: (b, p, 0, 0, 0)),
            pl.BlockSpec((1, 1, nc, ML_CHUNK, LANES), lambda b, p: (b, p, 0, 0, 0)),
            pl.BlockSpec((1, hp * ML_V_DIM), lambda b, p: (0, p)),
        ],
        out_specs=pl.BlockSpec((1, S, hp * ML_V_DIM), lambda b, p: (b, 0, p)),
        scratch_shapes=[
            pltpu.VMEM((hp, ML_QK_DIM, ML_V_DIM + LANES), F32),
            pltpu.VMEM((hp, 1, LANES), F32),
        ],
        compiler_params=_cparams("parallel", "parallel"),
        name="mlstm_chunkwise",
    )(q, kt, v, g_rows, g_cols, out_gain)


def _ml_out_kernel(hh_ref, og_ref, w_ref, x_ref, g_ref,
                   gain2_ref, sc2_ref, sh2_ref, wr_ref, br_ref, o_ref, h_ref, r_ref, rt_ref, cnt_ref, run_ref, tri_ref):
    a = jax.nn.sigmoid(og_ref[0]) * hh_ref[0]
    x_new = x_ref[0] + g_ref[0] * _dot(a.astype(BF16), w_ref[...])
    o_ref[0] = x_new
    _route_tile(x_new, gain2_ref, sc2_ref, sh2_ref, wr_ref, br_ref, h_ref, r_ref, rt_ref, cnt_ref, run_ref, tri_ref)


def _ml_out(hh, og, w_out_bf, x, g1, route_args):
    B, S, D = x.shape
    tm = OUT_TILE
    row = lambda b, j: (b, j, 0)
    per_b = lambda b, j: (b, 0, 0)
    const2 = lambda b, j: (0, 0)
    r_in, r_shape, r_out, r_scratch = _route_specs(B, S, D, tm)
    return pl.pallas_call(
        _ml_out_kernel,
        out_shape=(jax.ShapeDtypeStruct((B, S, D), F32),) + r_shape,
        grid=(B, S // tm),
        in_specs=[
            pl.BlockSpec((1, tm, D), row),
            pl.BlockSpec((1, tm, D), row),
            pl.BlockSpec((D, D), const2),
            pl.BlockSpec((1, tm, D), row),
            pl.BlockSpec((1, 1, D), per_b),
        ] + r_in,
        out_specs=(pl.BlockSpec((1, tm, D), row),) + r_out,
        scratch_shapes=r_scratch,
        compiler_params=_cparams("arbitrary", "arbitrary"),
        name="mlstm_out_proj",
    )(hh, og, w_out_bf, x, g1, *route_args)


def _first_argmax(x, lane, width):
    mx = jnp.max(x, axis=-1, keepdims=True)
    idx = jnp.min(jnp.where(x == mx, lane, width), axis=-1, keepdims=True)
    return mx, idx


def _route_tile(x, gain_ref, sc_ref, sh_ref, w_ref, b_ref, h_ref, r_ref, rt_ref, cnt_ref, run_ref, tri_ref):
    tm = x.shape[0]

    @pl.when(jnp.logical_and(pl.program_id(0) == 0, pl.program_id(1) == 0))
    def _():
        run_ref[...] = jnp.zeros_like(run_ref)
        r_i = lax.broadcasted_iota(jnp.int32, (tm, tm), 0)
        c_i = lax.broadcasted_iota(jnp.int32, (tm, tm), 1)
        tri_ref[...] = jnp.where(c_i < r_i, 1.0, 0.0).astype(BF16)

    h = _modulated_norm(x, gain_ref[...], sc_ref[0], sh_ref[0])
    h_ref[0] = _pack_bf16_pairs(h)
    logits = _dot(h.astype(BF16), w_ref[...]) + b_ref[...]
    lane = lax.broadcasted_iota(jnp.int32, (1, LANES), 1)

    def pick(lgt):
        lg = jnp.where(lane < N_GROUPS, lgt, NEG_INF)
        g_max, g_sel = _first_argmax(lg, lane, LANES)
        pg = 1.0 / jnp.sum(jnp.exp(lg - g_max), axis=-1, keepdims=True)
        e_lane = lane - N_GROUPS
        in_grp = jnp.logical_and(e_lane >= g_sel * EXPERTS_PER_GROUP, e_lane < (g_sel + 1) * EXPERTS_PER_GROUP)
        le = jnp.where(in_grp, lgt, NEG_INF)
        v1, i1 = _first_argmax(le, lane, LANES)
        v2, i2 = _first_argmax(jnp.where(lane == i1, NEG_INF, le), lane, LANES)
        ratio = jnp.exp(v2 - v1)
        return i1 - N_GROUPS, i2 - N_GROUPS, pg / (1.0 + ratio), pg * ratio / (1.0 + ratio)

    e1, e2, w1, w2 = pick(logits)
    hot1 = lane == e1
    hot2 = lane == e2
    onehot = jnp.where(jnp.logical_or(hot1, hot2), 1.0, 0.0)
    seen = _dot(tri_ref[...], onehot.astype(BF16)) + run_ref[...]
    rank1 = jnp.sum(jnp.where(hot1, seen, 0.0), axis=-1, keepdims=True)
    rank2 = jnp.sum(jnp.where(hot2, seen, 0.0), axis=-1, keepdims=True)
    run_ref[...] = run_ref[...] + jnp.sum(onehot, axis=0, keepdims=True)
    cnt_ref[...] = run_ref[...]

    out = jnp.where(lane == 0, e1.astype(F32), 0.0)
    out = jnp.where(lane == 1, e2.astype(F32), out)
    out = jnp.where(lane == 2, w1, out)
    out = jnp.where(lane == 3, w2, out)
    out = jnp.where(lane == 4, rank1, out)
    out = jnp.where(lane == 5, rank2, out)
    r_ref[0] = out
    rt_ref[0] = out.T[:8, :]


def _route_specs(B, S, D, tm):
    row = lambda b, j: (b, j, 0)
    per_b = lambda b, j: (b, 0, 0)
    const2 = lambda b, j: (0, 0)
    in_specs = [
        pl.BlockSpec((1, D), const2),
        pl.BlockSpec((1, 1, D), per_b),
        pl.BlockSpec((1, 1, D), per_b),
        pl.BlockSpec((D, LANES), const2),
        pl.BlockSpec((1, LANES), const2),
    ]
    out_shape = (
        jax.ShapeDtypeStruct((B, S, D // 2), jnp.int32),
        jax.ShapeDtypeStruct((B, S, LANES), F32),
        jax.ShapeDtypeStruct((B, 8, S), F32),
        jax.ShapeDtypeStruct((1, LANES), F32),
    )
    out_specs = (
        pl.BlockSpec((1, tm, D // 2), row),
        pl.BlockSpec((1, tm, LANES), row),
        pl.BlockSpec((1, 8, tm), lambda b, j: (b, 0, j)),
        pl.BlockSpec((1, LANES), const2),
    )
    return in_specs, out_shape, out_specs, [pltpu.VMEM((1, LANES), F32), pltpu.VMEM((tm, tm), BF16)]


def _experts_kernel(blk_e_ref, n_used_ref, next_e_ref, x_ref, wg_hbm, wu_hbm, wd_hbm, o_ref,
                    wg_f, wu_f, wd_f, wg_s, wu_s, wd_s, sem, *, layer):
    i = pl.program_id(0)
    used = i < n_used_ref[0]
    e = blk_e_ref[i]
    new_expert = jnp.logical_or(i == 0, e != blk_e_ref[jnp.maximum(i - 1, 0)])

    def fetch(expert):
        return (pltpu.make_async_copy(wg_hbm.at[layer, expert], wg_f, sem.at[0]),
                pltpu.make_async_copy(wu_hbm.at[layer, expert], wu_f, sem.at[1]),
                pltpu.make_async_copy(wd_hbm.at[layer, expert], wd_f, sem.at[2]))

    @pl.when(i == 0)
    def _():
        for cp in fetch(e):
            cp.start()

    @pl.when(jnp.logical_and(used, new_expert))
    def _():
        for cp in fetch(e):
            cp.wait()
        wg_s[...] = wg_f[...].astype(BF16)
        wu_s[...] = wu_f[...].astype(BF16)
        wd_s[...] = wd_f[...].astype(BF16)

        @pl.when(next_e_ref[i] >= 0)
        def _():
            for cp in fetch(next_e_ref[i]):
                cp.start()

    @pl.when(used)
    def _():
        x = _unpack_bf16_pairs(x_ref[...]).astype(BF16)
        a = _dot(x, wg_s[...])
        u = _dot(x, wu_s[...])
        act = a * jax.nn.sigmoid(a) * u
        o_ref[...] = _pack_bf16_pairs(_dot(act.astype(BF16), wd_s[...]))

    @pl.when(i >= n_used_ref[0])
    def _():
        o_ref[...] = jnp.zeros_like(o_ref)


def _experts(layer, blk_e, n_used, next_e, xs, w_gate, w_up, w_down):
    R = xs.shape[0]
    D = 2 * xs.shape[1]
    n_blk = R // MOE_BLOCK
    rows = lambda i, be, nu, ne: (i, 0)
    grid_spec = pltpu.PrefetchScalarGridSpec(
        num_scalar_prefetch=3,
        grid=(n_blk,),
        in_specs=[
            pl.BlockSpec((MOE_BLOCK, D // 2), rows),
            pl.BlockSpec(memory_space=pl.ANY),
            pl.BlockSpec(memory_space=pl.ANY),
            pl.BlockSpec(memory_space=pl.ANY),
        ],
        out_specs=pl.BlockSpec((MOE_BLOCK, D // 2), rows),
        scratch_shapes=[
            pltpu.VMEM((D, D_EXPERT), F32),
            pltpu.VMEM((D, D_EXPERT), F32),
            pltpu.VMEM((D_EXPERT, D), F32),
            pltpu.VMEM((D, D_EXPERT), BF16),
            pltpu.VMEM((D, D_EXPERT), BF16),
            pltpu.VMEM((D_EXPERT, D), BF16),
            pltpu.SemaphoreType.DMA((3,)),
        ],
    )
    return pl.pallas_call(
        functools.partial(_experts_kernel, layer=layer),
        out_shape=jax.ShapeDtypeStruct((R, D // 2), jnp.int32),
        grid_spec=grid_spec,
        compiler_params=_cparams("arbitrary"),
        name="moe_experts",
    )(blk_e, n_used, next_e, xs, w_gate, w_up, w_down)


def _residual_tile(x_ref, pending_refs):
    if not pending_refs:
        return x_ref[0]
    g_ref, y0_ref, y1_ref, r_ref = pending_refs
    r = r_ref[0]
    y = _unpack_bf16_pairs(y0_ref[0, 0]) * r[:, 2:3] + _unpack_bf16_pairs(y1_ref[0, 0]) * r[:, 3:4]
    return x_ref[0] + g_ref[0] * y


def _pending_specs(D, tm):
    return [
        pl.BlockSpec((1, 1, D), lambda b, j: (b, 0, 0)),
        pl.BlockSpec((1, 1, tm, D // 2), lambda b, j: (0, b, j, 0)),
        pl.BlockSpec((1, 1, tm, D // 2), lambda b, j: (1, b, j, 0)),
        pl.BlockSpec((1, tm, LANES), lambda b, j: (b, j, 0)),
    ]


def _combine_kernel(x_ref, g_ref, y0_ref, y1_ref, r_ref, o_ref):
    o_ref[0] = _residual_tile(x_ref, (g_ref, y0_ref, y1_ref, r_ref))


def _combine(x, pending):
    B, S, D = x.shape
    tm = OUT_TILE
    row = lambda b, j: (b, j, 0)
    return pl.pallas_call(
        _combine_kernel,
        out_shape=jax.ShapeDtypeStruct((B, S, D), F32),
        grid=(B, S // tm),
        in_specs=[pl.BlockSpec((1, tm, D), row)] + _pending_specs(D, tm),
        out_specs=pl.BlockSpec((1, tm, D), row),
        compiler_params=_cparams("parallel", "parallel"),
        name="moe_combine",
    )(x, *pending)


SC_CORES = 2
SC_SUBCORES = 16
SC_WORKERS = SC_CORES * SC_SUBCORES
SC_CHUNK = 64


def _sc_mesh():
    return plsc.VectorSubcoreMesh(core_axis_name="c", subcore_axis_name="s",
                                  num_cores=SC_CORES, num_subcores=SC_SUBCORES)


def _sc_scatter_rows(src, idx, n_out):
    T, W = src.shape
    per_w = T // SC_WORKERS
    nch = per_w // SC_CHUNK
    idx4 = idx.reshape(TOP_K, SC_WORKERS, nch, SC_CHUNK)

    @functools.partial(
        pl.kernel, mesh=_sc_mesh(),
        out_type=jax.ShapeDtypeStruct((n_out, W), src.dtype),
        scratch_types=[
            pltpu.VMEM((TOP_K, nch, SC_CHUNK), jnp.int32),
            pltpu.VMEM((2, SC_CHUNK, W), src.dtype),
            pltpu.SemaphoreType.DMA((2,)),
            pltpu.SemaphoreType.DMA((2 * TOP_K,)),
        ],
        name="sc_scatter_rows",
    )
    def body(src_hbm, idx_hbm, out_hbm, idx_v, rows_v, load_sem, scatter_sem):
        wid = lax.axis_index("s") * SC_CORES + lax.axis_index("c")
        for s in range(TOP_K):
            pltpu.sync_copy(idx_hbm.at[s, wid], idx_v.at[s])

        def load(j, b):
            return pltpu.make_async_copy(
                src_hbm.at[pl.ds(wid * per_w + j * SC_CHUNK, SC_CHUNK)], rows_v.at[b], load_sem.at[b])

        def scatter(s, j, b):
            return pltpu.make_async_copy(rows_v.at[b], out_hbm.at[idx_v.at[s, j]], scatter_sem.at[2 * s + b])

        load(0, 0).start()

        @pl.loop(0, nch, step=2)
        def _(i):
            for b in range(2):
                j = i + b

                @pl.when(j >= 1)
                def _():
                    for s in range(TOP_K):
                        scatter(s, j - 1, 1 - b).wait()

                @pl.when(j + 1 < nch)
                def _():
                    load(j + 1, 1 - b).start()

                load(j, b).wait()
                for s in range(TOP_K):
                    scatter(s, j, b).start()

        for s in range(TOP_K):
            scatter(s, nch - 1, (nch - 1) % 2).wait()

    assert nch % 2 == 0
    return body(src, idx4)


def _sc_gather_rows(table, idx):
    N = idx.shape[0]
    W = table.shape[1]
    per_w = N // SC_WORKERS
    nch = per_w // SC_CHUNK
    idx3 = idx.reshape(SC_WORKERS, nch, SC_CHUNK)

    @functools.partial(
        pl.kernel, mesh=_sc_mesh(),
        out_type=jax.ShapeDtypeStruct((N, W), table.dtype),
        scratch_types=[
            pltpu.VMEM((nch, SC_CHUNK), jnp.int32),
            pltpu.VMEM((2, SC_CHUNK, W), table.dtype),
            pltpu.SemaphoreType.DMA((2,)),
            pltpu.SemaphoreType.DMA((2,)),
        ],
        name="sc_gather_rows",
    )
    def body(table_hbm, idx_hbm, out_hbm, idx_v, rows_v, gather_sem, write_sem):
        wid = lax.axis_index("s") * SC_CORES + lax.axis_index("c")
        pltpu.sync_copy(idx_hbm.at[wid], idx_v)

        def gather(j, b):
            return pltpu.make_async_copy(table_hbm.at[idx_v.at[j]], rows_v.at[b], gather_sem.at[b])

        def write(j, b):
            return pltpu.make_async_copy(
                rows_v.at[b], out_hbm.at[pl.ds(wid * per_w + j * SC_CHUNK, SC_CHUNK)], write_sem.at[b])

        gather(0, 0).start()

        @pl.loop(0, nch, step=2)
        def _(i):
            for b in range(2):
                j = i + b

                @pl.when(j >= 1)
                def _():
                    write(j - 1, 1 - b).wait()

                @pl.when(j + 1 < nch)
                def _():
                    gather(j + 1, 1 - b).start()

                gather(j, b).wait()
                write(j, b).start()

        write(nch - 1, (nch - 1) % 2).wait()

    assert nch % 2 == 0
    return body(table, idx3)


def _moe_dispatch(route_t, counts, T):
    A = T * TOP_K
    counts = counts[0, :N_EXPERTS].astype(jnp.int32)
    blocks_per = (counts + MOE_BLOCK - 1) // MOE_BLOCK
    block_end = jnp.cumsum(blocks_per)
    block_start = block_end - blocks_per
    expert = jnp.swapaxes(route_t[:, :TOP_K, :], 0, 1).reshape(TOP_K, T).astype(jnp.int32)
    rank = jnp.swapaxes(route_t[:, 4:4 + TOP_K, :], 0, 1).reshape(TOP_K, T).astype(jnp.int32)
    onehot = expert[None] == jnp.arange(N_EXPERTS, dtype=jnp.int32)[:, None, None]
    start = jnp.sum(jnp.where(onehot, block_start[:, None, None], 0), axis=0)
    dest = start * MOE_BLOCK + rank
    n_blk = -(-A // MOE_BLOCK) + N_EXPERTS
    blk = jnp.arange(n_blk, dtype=jnp.int32)
    blk_e = jnp.minimum(jnp.sum(blk[:, None] >= block_end[None, :], axis=-1), N_EXPERTS - 1).astype(jnp.int32)
    n_used = block_end[-1]
    first = jnp.logical_and(blk < n_used, jnp.logical_or(blk == 0, blk_e != jnp.roll(blk_e, 1)))
    first_pos = jnp.where(first, blk, n_blk)
    next_pos = jnp.concatenate([lax.cummin(first_pos, axis=0, reverse=True)[1:], jnp.full((1,), n_blk, jnp.int32)])
    next_e = jnp.where(next_pos < n_blk, blk_e[jnp.minimum(next_pos, n_blk - 1)], -1).astype(jnp.int32)
    return dest, n_blk * MOE_BLOCK, blk_e, n_used.reshape(1).astype(jnp.int32), next_e


def _rope_tables(S):
    inv = 1.0 / (ROPE_THETA ** (jnp.arange(0, HEAD_DIM, 2, dtype=F32) / HEAD_DIM))
    ang = jnp.arange(S, dtype=F32)[:, None] * inv[None, :]
    cos, sin = jnp.cos(ang), jnp.sin(ang)
    cos_h = jnp.concatenate([cos, cos], axis=-1)
    sin_h = jnp.concatenate([-sin, sin], axis=-1)
    return jnp.tile(cos_h, (1, ATTN_HEADS)), jnp.tile(sin_h, (1, ATTN_HEADS))


def _pad_cols(w, width):
    return jnp.pad(w, ((0, 0), (0, width - w.shape[1])))


def kernel(x, c, ada_w, ada_b, norm_mix, norm_ffn, hy_w_in, hy_q_norm, hy_k_norm, hy_conv_w, hy_w_out, ml_w_in, ml_b_gates, ml_out_norm, ml_w_out, moe_w_group, moe_b_group, moe_w_expert, moe_b_expert, moe_w_gate, moe_w_up, moe_w_down):
    B, S, D = x.shape
    T = B * S
    cos_t, sin_t = _rope_tables(S)
    mod = _ada_modulation(c, ada_w, ada_b).reshape(DEPTH, B, 6, 1, D)
    r_i = np.arange(ATTN_WIDTH)
    grp = jnp.asarray((r_i[:, None] // HEAD_DIM) == (r_i[None, :] // HEAD_DIM), dtype=BF16)

    pending = ()
    for l in range(DEPTH):
        sh1, sc1, g1, sh2, sc2, g2 = [mod[l, :, i] for i in range(6)]
        gain1 = norm_mix[l].reshape(1, D)
        w_r = _pad_cols(jnp.concatenate([moe_w_group[l], moe_w_expert[l]], axis=1), LANES).astype(BF16)
        b_r = jnp.pad(jnp.concatenate([moe_b_group[l], moe_b_expert[l]]), (0, LANES - N_GROUPS - N_EXPERTS))
        route_args = (norm_ffn[l].reshape(1, D), sc2, sh2, w_r, b_r.reshape(1, LANES))
        j = l // 2
        if l % 2 == 0:
            w = hy_w_in[j]
            o = np.cumsum((0,) + (ATTN_WIDTH, HEAD_DIM, HEAD_DIM, IDX_HEADS * IDX_DIM, IDX_DIM, IDX_HEADS,
                                  CONV_WIDTH, CONV_WIDTH, CONV_WIDTH))
            w_small = jnp.concatenate([w[:, o[1]:o[3]], _pad_cols(w[:, o[4]:o[6]], LANES)], axis=1)
            weights = [g.astype(BF16) for g in (w_small, w[:, o[0]:o[1]], w[:, o[3]:o[4]], w[:, o[6]:o[9]])]
            qn_t = jnp.tile(hy_q_norm[j], ATTN_HEADS).reshape(1, ATTN_WIDTH)
            kn_t = jnp.tile(hy_k_norm[j], LANES // HEAD_DIM).reshape(1, LANES)
            outs = _hyb_in(x, pending, gain1, sc1, sh1, weights, cos_t, sin_t, qn_t, kn_t, grp)
            qt, iqt, bcu, kv, kvt, ik, iwt = outs[:7]
            if pending:
                x = outs[7]
            y_attn = _dsa_attention_t(qt, iqt, iwt, kv, kvt, ik)
            x, h2, route, route_t, counts = _hyb_out(y_attn, bcu, hy_conv_w[j], hy_w_out[j].astype(BF16), x, g1, route_args)
        else:
            w = ml_w_in[j]
            hq = ML_HEADS * ML_QK_DIM
            hv = ML_HEADS * ML_V_DIM
            wq, wk, wv = w[:, :hq], w[:, hq:2 * hq], w[:, 2 * hq:2 * hq + hv]
            wg = w[:, 2 * hq + hv:2 * hq + hv + 2 * ML_HEADS]
            wo = w[:, 2 * hq + hv + 2 * ML_HEADS:]
            weights = [g.astype(BF16) for g in (_pad_cols(wg, LANES), wk, wq, wv, wo)]
            gate_bias = jnp.pad(ml_b_gates[j], (0, LANES - 2 * ML_HEADS)).reshape(1, LANES)
            q, k, v, og, g_cols, g_rows, x = _ml_in(x, pending, gain1, sc1, sh1, weights, gate_bias)
            assert ML_STEP_HEADS == ML_HEADS
            hh = _mlstm(q, k, v, g_rows[:, None], g_cols[:, None], ml_out_norm[j].reshape(1, hv))
            x, h2, route, route_t, counts = _ml_out(hh, og, ml_w_out[j].astype(BF16), x, g1, route_args)

        dest, n_rows, blk_e, n_used, next_e = _moe_dispatch(route_t, counts, T)
        xs = _sc_scatter_rows(h2.reshape(T, D // 2), dest, n_rows)
        ys = _experts(l, blk_e, n_used, next_e, xs, moe_w_gate, moe_w_up, moe_w_down)
        y01 = _sc_gather_rows(ys, dest.reshape(TOP_K * T)).reshape(TOP_K, B, S, D // 2)
        pending = (g2, y01, y01, route)
    return _combine(x, pending)
```

```python
import functools

import numpy as np
import jax
import jax.numpy as jnp
from jax import lax
from jax.experimental import pallas as pl
from jax.experimental.pallas import tpu as pltpu
from jax.experimental.pallas import tpu_sc as plsc

F32 = jnp.float32
BF16 = jnp.bfloat16
HIGHEST = lax.Precision.HIGHEST

D_MODEL = 1024
DEPTH = 4
ATTN_HEADS = 8
HEAD_DIM = 64
ATTN_WIDTH = ATTN_HEADS * HEAD_DIM
IDX_HEADS = 8
IDX_DIM = 64
INDEX_TOPK = 256
Q_BLOCK = 256
ROPE_THETA = 10000.0
CONV_WIDTH = D_MODEL - ATTN_WIDTH
CONV_K = 3
ML_HEADS = 8
ML_QK_DIM = 64
ML_V_DIM = 128
N_GROUPS = 4
EXPERTS_PER_GROUP = 8
N_EXPERTS = N_GROUPS * EXPERTS_PER_GROUP
TOP_K = 2
D_EXPERT = 512
MOE_BLOCK = 512
NORM_EPS = 1e-6

LANES = 128
VMEM_LIMIT = 56 * 1024 * 1024
TOKEN_TILE = 512
OUT_TILE = 1024
ML_CHUNK = 256
ML_STEP_HEADS = 8
NEG_INF = float("-inf")


def _cparams(*sem):
    return pltpu.CompilerParams(dimension_semantics=sem, vmem_limit_bytes=VMEM_LIMIT)


def _dot(a, b):
    return jnp.dot(a, b, preferred_element_type=F32)


def _pack_bf16_pairs(x):
    bits = lax.bitcast_convert_type(x.astype(BF16).astype(F32), jnp.uint32)
    half = bits.shape[1] // 2
    packed = (bits[:, :half] >> 16) | (bits[:, half:] & jnp.uint32(0xFFFF0000))
    return lax.bitcast_convert_type(packed, jnp.int32)


def _unpack_bf16_pairs(words):
    words = lax.bitcast_convert_type(words, jnp.uint32)
    return jnp.concatenate(
        [lax.bitcast_convert_type(words << 16, F32),
         lax.bitcast_convert_type(words & jnp.uint32(0xFFFF0000), F32)], axis=1)


def _split_dot(a_f32, b_bf16):
    hi = a_f32.astype(BF16)
    lo = (a_f32 - hi.astype(F32)).astype(BF16)
    return _dot(hi, b_bf16) + _dot(lo, b_bf16)


def _ada_kernel(c_ref, w_ref, b_ref, o_ref):
    c = c_ref[...]
    ca = c * jax.nn.sigmoid(c)
    o_ref[0] = jnp.dot(ca, w_ref[0], precision=HIGHEST, preferred_element_type=F32) + b_ref[0]


def _ada_modulation(c, ada_w, ada_b):
    B, D = c.shape
    n_col = ada_w.shape[-1] // D
    return pl.pallas_call(
        _ada_kernel,
        out_shape=jax.ShapeDtypeStruct((DEPTH, B, n_col * D), F32),
        grid=(DEPTH, n_col),
        in_specs=[
            pl.BlockSpec((B, D), lambda l, j: (0, 0)),
            pl.BlockSpec((1, D, D), lambda l, j: (l, 0, j)),
            pl.BlockSpec((1, 1, D), lambda l, j: (l, 0, j)),
        ],
        out_specs=pl.BlockSpec((1, B, D), lambda l, j: (l, 0, j)),
        compiler_params=_cparams("parallel", "parallel"),
        name="ada_modulation",
    )(c, ada_w, ada_b.reshape(DEPTH, 1, n_col * D))


def _modulated_norm(x, gain, scale, shift):
    y = x * lax.rsqrt(jnp.mean(x * x, axis=-1, keepdims=True) + NORM_EPS)
    return y * gain * (1.0 + scale) + shift


def _rope(x, cos, sin_signed, first_half):
    w = x.shape[-1]
    partner = jnp.where(first_half, pltpu.roll(x, w - HEAD_DIM // 2, 1), pltpu.roll(x, HEAD_DIM // 2, 1))
    return x * cos + partner * sin_signed


HYB_COLS = 5 * 512 + 2 * LANES


def _hyb_in_kernel(*refs, n_pending):
    x_ref, pending_refs, refs = refs[0], refs[1:1 + n_pending], refs[1 + n_pending:]
    (gain_ref, sc_ref, sh_ref, w_ref, cos_ref, sin_ref, qn_ref, kn_ref, grp_ref,
     qt_ref, iqt_ref, bcu_ref, kv_ref, kvt_ref, ik_ref, iwt_ref) = refs[:16]
    x = _residual_tile(x_ref, pending_refs)
    if n_pending:
        refs[16][0] = x
    h = _modulated_norm(x, gain_ref[...], sc_ref[0], sh_ref[0])
    hb = h.astype(BF16)
    p_small = _dot(hb, w_ref[:, 0:2 * LANES])
    p_q = _dot(hb, w_ref[:, 2 * LANES:2 * LANES + 512])
    p_iq = _dot(hb, w_ref[:, 2 * LANES + 512:2 * LANES + 1024])
    cos = cos_ref[...]
    sin = sin_ref[...]
    lane = lax.broadcasted_iota(jnp.int32, (1, ATTN_WIDTH), 1)
    first_half = (lane % HEAD_DIM) < (HEAD_DIM // 2)
    fh128 = first_half[:, :LANES]
    lane128 = lane[:, :LANES]

    kv = p_small[:, :LANES]
    is_k = lane128 < HEAD_DIM
    kk = jnp.where(is_k, kv, 0.0)
    ms_k = jnp.sum(kk * kk, axis=-1, keepdims=True) * (1.0 / HEAD_DIM)
    kn = kv * lax.rsqrt(ms_k + NORM_EPS) * kn_ref[...]
    kr = _rope(kn, cos[:, :LANES], sin[:, :LANES], fh128)
    kv = jnp.where(is_k, kr, kv)
    kv_ref[0] = kv.astype(BF16)
    feat = lax.broadcasted_iota(jnp.int32, (LANES, 1), 0)
    kv_t = jnp.where(feat < HEAD_DIM, 1.0, kv.T).astype(BF16)
    for i in range(kvt_ref.shape[1]):
        kvt_ref[0, i] = kv_t[:, i * DSA_KEY_CHUNK:(i + 1) * DSA_KEY_CHUNK]

    sm = p_small[:, LANES:]
    ikr = _rope(sm, cos[:, :LANES], sin[:, :LANES], fh128)
    ik_ref[0] = jnp.where(is_k, ikr, 0.0).astype(BF16)
    iwt_ref[0] = sm.T

    ms = _split_dot(p_q * p_q, grp_ref[...]) * (1.0 / HEAD_DIM)
    q = p_q * lax.rsqrt(ms + NORM_EPS) * qn_ref[...]
    qt_ref[0] = (_rope(q, cos, sin, first_half) * (HEAD_DIM ** -0.5)).T.astype(BF16)
    iqt_ref[0] = (_rope(p_iq, cos, sin, first_half) * (IDX_DIM ** -0.5)).T.astype(BF16)

    bcu_ref[0] = _dot(hb, w_ref[:, 2 * LANES + 1024:])


def _hyb_in(x, pending, gain, sc, sh, w_pad, cos_t, sin_t, qn_t, kn_t, grp):
    B, S, D = x.shape
    tm = TOKEN_TILE
    row = lambda b, j: (b, j, 0)
    per_b = lambda b, j: (b, 0, 0)
    const2 = lambda b, j: (0, 0)
    tab = lambda b, j: (j, 0)
    col = lambda b, j: (b, 0, j)
    assert tm % DSA_KEY_CHUNK == 0
    x_out_shape = (jax.ShapeDtypeStruct((B, S, D), F32),) if pending else ()
    x_out_spec = (pl.BlockSpec((1, tm, D), row),) if pending else ()
    return pl.pallas_call(
        functools.partial(_hyb_in_kernel, n_pending=len(pending)),
        out_shape=(
            jax.ShapeDtypeStruct((B, 512, S), BF16),
            jax.ShapeDtypeStruct((B, 512, S), BF16),
            jax.ShapeDtypeStruct((B, S, 1536), F32),
            jax.ShapeDtypeStruct((B, S, LANES), BF16),
            jax.ShapeDtypeStruct((B, S // DSA_KEY_CHUNK, LANES, DSA_KEY_CHUNK), BF16),
            jax.ShapeDtypeStruct((B, S, LANES), BF16),
            jax.ShapeDtypeStruct((B, LANES, S), F32),
        ) + x_out_shape,
        grid=(B, S // tm),
        in_specs=[pl.BlockSpec((1, tm, D), row)] + (_pending_specs(D, tm) if pending else []) + [
            pl.BlockSpec((1, D), const2),
            pl.BlockSpec((1, 1, D), per_b),
            pl.BlockSpec((1, 1, D), per_b),
            pl.BlockSpec((D, HYB_COLS), const2),
            pl.BlockSpec((tm, 512), tab),
            pl.BlockSpec((tm, 512), tab),
            pl.BlockSpec((1, 512), const2),
            pl.BlockSpec((1, LANES), const2),
            pl.BlockSpec((512, 512), const2),
        ],
        out_specs=(
            pl.BlockSpec((1, 512, tm), col),
            pl.BlockSpec((1, 512, tm), col),
            pl.BlockSpec((1, tm, 1536), row),
            pl.BlockSpec((1, tm, LANES), row),
            pl.BlockSpec((1, tm // DSA_KEY_CHUNK, LANES, DSA_KEY_CHUNK), lambda b, j: (b, j, 0, 0)),
            pl.BlockSpec((1, tm, LANES), row),
            pl.BlockSpec((1, LANES, tm), col),
        ) + x_out_spec,
        compiler_params=_cparams("parallel", "parallel"),
        name="hybrid_in_proj",
    )(x, *pending, gain, sc, sh, w_pad, cos_t, sin_t, qn_t, kn_t, grp)


DSA_KEY_CHUNK = 512
DSA_SUM_ROWS = 16


def _fold8(x, op):
    parts = x.reshape(x.shape[0] // 8, 8, x.shape[1])
    while parts.shape[0] > 1:
        half = parts.shape[0] // 2
        assert parts.shape[0] == 2 * half
        parts = op(parts[:half], parts[half:])
    return parts[0]


def _col_reduce(x, op):
    t = _fold8(x, op)
    for shift in (4, 2, 1):
        t = op(t, pltpu.roll(t, shift, 0))
    return t[0:1, :]


def _dsa_t_kernel(qt_ref, iqt_ref, iwt_ref, kv_ref, kvt_ref, ik_ref, o_ref, sc_ref, bias_ref, acc_ref):
    CK = DSA_KEY_CHUNK
    QB = Q_BLOCK
    qb = pl.program_id(1)
    nk = (qb * QB + QB + CK - 1) // CK
    kf = float(INDEX_TOPK)
    qpos = qb * QB + lax.broadcasted_iota(jnp.int32, (1, QB), 1)
    krow = lax.broadcasted_iota(jnp.int32, (CK, 1), 0)
    w_idx = iwt_ref[0, IDX_DIM:IDX_DIM + IDX_HEADS, :] * (IDX_HEADS ** -0.5)

    def rows(c):
        return pl.ds(pl.multiple_of(c * CK, CK), CK)

    def heads_on_lanes(ref, width):
        return jnp.concatenate([ref[0, hd * width:(hd + 1) * width, :] for hd in range(ref.shape[1] // width)], axis=1)

    def head_lanes(hd):
        return slice(hd * QB, (hd + 1) * QB)

    iq_wide = heads_on_lanes(iqt_ref, IDX_DIM)
    w_wide = jnp.concatenate([w_idx[hd:hd + 1, :] for hd in range(IDX_HEADS)], axis=1)

    def score_chunk(c, carry):
        mx, mn = carry
        ikc = ik_ref[0, rows(c), :][:, :IDX_DIM]
        s_all = jnp.maximum(_dot(ikc, iq_wide), 0.0) * w_wide
        acc = s_all[:, head_lanes(0)]
        for hd in range(1, IDX_HEADS):
            acc = acc + s_all[:, head_lanes(hd)]
        causal = (c * CK + krow) <= qpos
        sc_ref[rows(c), :] = jnp.where(causal, acc, NEG_INF)
        mx = jnp.maximum(mx, _fold8(jnp.where(causal, acc, NEG_INF), jnp.maximum))
        mn = jnp.minimum(mn, _fold8(jnp.where(causal, acc, jnp.inf), jnp.minimum))
        return mx, mn

    mx8, mn8 = lax.fori_loop(0, nk, score_chunk,
                             (jnp.full((8, QB), NEG_INF, F32), jnp.full((8, QB), jnp.inf, F32)))
    row_max = jnp.max(mx8, axis=0, keepdims=True)
    row_min = jnp.min(mn8, axis=0, keepdims=True)

    def count(pred):
        def body(c, part):
            return part + _fold8(jnp.where(pred(sc_ref[rows(c), :]), 1.0, 0.0), jnp.add)
        part = lax.fori_loop(0, nk, body, jnp.zeros((8, QB), F32))
        return jnp.sum(part, axis=0, keepdims=True)

    @pl.when(qb * QB + QB <= INDEX_TOPK)
    def _():
        def body(c, carry):
            bias_ref[rows(c), :] = jnp.where(sc_ref[rows(c), :] > NEG_INF, 0.0, NEG_INF)
            return carry
        lax.fori_loop(0, nk, body, 0)

    @pl.when(qb * QB + QB > INDEX_TOPK)
    def _():
        def bisect(_, carry):
            lo, hi, c_lo, c_hi = carry
            mid = 0.5 * lo + 0.5 * jnp.minimum(hi, row_max)
            cnt = count(lambda x: x >= mid)
            ge = cnt >= kf
            return (jnp.where(ge, mid, lo), jnp.where(ge, hi, mid),
                    jnp.where(ge, cnt, c_lo), jnp.where(ge, c_hi, cnt))

        n_adm = (qpos + 1).astype(F32)
        lo, hi, c_lo, c_hi = lax.fori_loop(
            0, 18, bisect, (row_min, jnp.full((1, QB), jnp.inf, F32), n_adm, jnp.zeros((1, QB), F32)))

        def refine_cond(carry):
            it, _, _, _, done = carry
            return jnp.logical_and(it < nk * CK, jnp.min(done) < 0.5)

        def refine(carry):
            it, hi, c_hi, thr, done = carry

            def edges(c, part):
                up, dn = part
                x = sc_ref[rows(c), :]
                return (jnp.maximum(up, _fold8(jnp.where(x < hi, x, NEG_INF), jnp.maximum)),
                        jnp.minimum(dn, _fold8(jnp.where(x >= lo, x, jnp.inf), jnp.minimum)))

            up8, dn8 = lax.fori_loop(0, nk, edges,
                                     (jnp.full((8, QB), NEG_INF, F32), jnp.full((8, QB), jnp.inf, F32)))
            m_up = jnp.max(up8, axis=0, keepdims=True)
            m_dn = jnp.min(dn8, axis=0, keepdims=True)
            from_hi = c_hi == kf - 1.0
            from_lo = c_lo == kf
            thr = jnp.where(done > 0.5, thr, jnp.where(from_hi, m_up, m_dn))
            done = jnp.where(jnp.logical_or(from_hi, from_lo), 1.0, done)

            def step_down(args):
                hi, c_hi, thr, done = args
                open_ = done < 0.5
                cnt = count(lambda x: x >= m_up)
                hit = jnp.logical_and(open_, cnt >= kf)
                moved = jnp.logical_and(open_, cnt < kf)
                return (jnp.where(moved, m_up, hi), jnp.where(moved, cnt, c_hi), jnp.where(hit, m_up, thr),
                        jnp.where(hit, 1.0, done))

            hi, c_hi, thr, done = lax.cond(jnp.min(done) > 0.5, lambda args: args, step_down, (hi, c_hi, thr, done))
            return it + 1, hi, c_hi, thr, done

        _, _, _, thr, _ = lax.while_loop(
            refine_cond, refine, (jnp.int32(0), hi, c_hi, row_max, jnp.zeros((1, QB), F32)))

        def body(c, sel):
            keep = sc_ref[rows(c), :] >= thr
            bias_ref[rows(c), :] = jnp.where(keep, 0.0, NEG_INF)
            return sel + _fold8(jnp.where(keep, 1.0, 0.0), jnp.add)

        n_sel = jnp.sum(lax.fori_loop(0, nk, body, jnp.zeros((8, QB), F32)), axis=0, keepdims=True)

        @pl.when(jnp.max(n_sel) > kf + 0.5)
        def _():
            need = kf - count(lambda x: x > thr)
            r_i = lax.broadcasted_iota(jnp.int32, (CK, CK), 0)
            c_i = lax.broadcasted_iota(jnp.int32, (CK, CK), 1)
            lower = jnp.where(c_i <= r_i, 1.0, 0.0).astype(BF16)

            def body(c, seen):
                x = sc_ref[rows(c), :]
                eq = x == thr
                eq_f = jnp.where(eq, 1.0, 0.0)
                rank = _dot(lower, eq_f.astype(BF16)) + seen
                keep = jnp.logical_or(x > thr, jnp.logical_and(eq, rank <= need))
                bias_ref[rows(c), :] = jnp.where(keep, 0.0, NEG_INF)
                return seen + jnp.sum(eq_f, axis=0, keepdims=True)

            lax.fori_loop(0, nk, body, jnp.zeros((1, QB), F32))

    acc_ref[...] = jnp.zeros_like(acc_ref)
    q_wide = heads_on_lanes(qt_ref, HEAD_DIM)

    def attend_chunk(c, m_old):
        kc = kv_ref[0, rows(c), :][:, :HEAD_DIM]
        bias = bias_ref[rows(c), :]
        logits = _dot(kc, q_wide) + jnp.concatenate([bias] * ATTN_HEADS, axis=1)
        m_new = jnp.maximum(m_old, _col_reduce(logits, jnp.maximum))
        alpha = jnp.exp(m_old - m_new)
        p = jnp.exp(logits - m_new)
        acc_ref[...] = alpha * acc_ref[...] + _dot(kvt_ref[0, c, HEAD_DIM - DSA_SUM_ROWS:, :], p.astype(BF16))
        return m_new

    lax.fori_loop(0, nk, attend_chunk, jnp.full((1, ATTN_HEADS * QB), -1e30, F32))
    out_t = acc_ref[DSA_SUM_ROWS:, :] / acc_ref[0:1, :]
    o_ref[0] = jnp.concatenate([out_t[:, head_lanes(hd)] for hd in range(ATTN_HEADS)], axis=0).T


def _dsa_attention_t(qt, iqt, iwt, kv, kvt, ik):
    B, _, S = qt.shape
    col = lambda b, j: (b, 0, j)
    per_b = lambda b, j: (b, 0, 0)
    return pl.pallas_call(
        _dsa_t_kernel,
        out_shape=jax.ShapeDtypeStruct((B, S, ATTN_WIDTH), F32),
        grid=(B, S // Q_BLOCK),
        in_specs=[
            pl.BlockSpec((1, ATTN_WIDTH, Q_BLOCK), col),
            pl.BlockSpec((1, IDX_HEADS * IDX_DIM, Q_BLOCK), col),
            pl.BlockSpec((1, LANES, Q_BLOCK), col),
            pl.BlockSpec((1, S, LANES), per_b),
            pl.BlockSpec((1, S // DSA_KEY_CHUNK, LANES, DSA_KEY_CHUNK), lambda b, j: (b, 0, 0, 0)),
            pl.BlockSpec((1, S, LANES), per_b),
        ],
        out_specs=pl.BlockSpec((1, Q_BLOCK, ATTN_WIDTH), lambda b, j: (b, j, 0)),
        scratch_shapes=[
            pltpu.VMEM((S, Q_BLOCK), F32),
            pltpu.VMEM((S, Q_BLOCK), F32),
            pltpu.VMEM((DSA_SUM_ROWS + HEAD_DIM, ATTN_HEADS * Q_BLOCK), F32),
        ],
        compiler_params=_cparams("parallel", "parallel"),
        name="dsa_attention",
    )(qt, iqt, iwt, kv, kvt, ik)


def _hyb_out_kernel(ya_ref, bcu_ref, halo_ref, cw_ref, w_ref, x_ref, g_ref,
                    gain2_ref, sc2_ref, sh2_ref, wr_ref, br_ref, o_ref, h_ref, r_ref, rt_ref, cnt_ref, run_ref, tri_ref):
    j = pl.program_id(1)
    tm = ya_ref.shape[1]
    bcu = bcu_ref[0]
    bg = bcu[:, 0:512]
    z = bcu[:, 512:1024] * bcu[:, 1024:1536]
    halo = halo_ref[0]
    zh = halo[:, 512:1024] * halo[:, 1024:1536]
    zh = jnp.where(j > 0, zh, 0.0)
    row = lax.broadcasted_iota(jnp.int32, (tm, 1), 0)
    z1 = jnp.where(row >= 1, pltpu.roll(z, 1, 0), zh[7:8, :])
    z2 = jnp.where(row >= 2, pltpu.roll(z, 2, 0), jnp.where(row == 1, zh[7:8, :], zh[6:7, :]))
    cw = cw_ref[...]
    y_conv = bg * (z2 * cw[0:1, :] + z1 * cw[1:2, :] + z * cw[2:3, :])
    y = _dot(ya_ref[0].astype(BF16), w_ref[0:512, :]) + _dot(y_conv.astype(BF16), w_ref[512:1024, :])
    x_new = x_ref[0] + g_ref[0] * y
    o_ref[0] = x_new
    _route_tile(x_new, gain2_ref, sc2_ref, sh2_ref, wr_ref, br_ref, h_ref, r_ref, rt_ref, cnt_ref, run_ref, tri_ref)


def _hyb_out(y_attn, bcu, conv_w, w_out_bf, x, g1, route_args):
    B, S, D = x.shape
    tm = OUT_TILE
    row = lambda b, j: (b, j, 0)
    per_b = lambda b, j: (b, 0, 0)
    const2 = lambda b, j: (0, 0)
    halo = lambda b, j: (b, jnp.maximum(j * (tm // 8) - 1, 0), 0)
    r_in, r_shape, r_out, r_scratch = _route_specs(B, S, D, tm)
    return pl.pallas_call(
        _hyb_out_kernel,
        out_shape=(jax.ShapeDtypeStruct((B, S, D), F32),) + r_shape,
        grid=(B, S // tm),
        in_specs=[
            pl.BlockSpec((1, tm, 512), row),
            pl.BlockSpec((1, tm, 1536), row),
            pl.BlockSpec((1, 8, 1536), halo),
            pl.BlockSpec((CONV_K, CONV_WIDTH), const2),
            pl.BlockSpec((D, D), const2),
            pl.BlockSpec((1, tm, D), row),
            pl.BlockSpec((1, 1, D), per_b),
        ] + r_in,
        out_specs=(pl.BlockSpec((1, tm, D), row),) + r_out,
        scratch_shapes=r_scratch,
        compiler_params=_cparams("arbitrary", "arbitrary"),
        name="hybrid_out_proj",
    )(y_attn, bcu, bcu, conv_w, w_out_bf, x, g1, *route_args)


ML_COLS = 512 + 512 + 1024 + 1024 + LANES


def _ml_in_kernel(x_ref, g2_ref, y0_ref, y1_ref, r_ref, gain_ref, sc_ref, sh_ref, w_ref, bias_ref,
                  q_ref, k_ref, v_ref, og_ref, gt_ref, gtt_ref, xo_ref):
    x = _residual_tile(x_ref, (g2_ref, y0_ref, y1_ref, r_ref))
    xo_ref[0] = x
    h = _modulated_norm(x, gain_ref[...], sc_ref[0], sh_ref[0])
    hb = h.astype(BF16)
    gates = _dot(hb, w_ref[:, 0:LANES]) + bias_ref[...]
    gates_t = gates.T[:2 * ML_HEADS, :]
    keys_t = _dot(hb, w_ref[:, LANES:LANES + 512]).T.astype(BF16)
    for i in range(gt_ref.shape[1]):
        span = slice(i * ML_CHUNK, (i + 1) * ML_CHUNK)
        gt_ref[0, i] = gates[span, :]
        gtt_ref[0, i] = gates_t[:, span]
        k_ref[0, i] = keys_t[:, span]
    q_ref[0] = (_dot(hb, w_ref[:, LANES + 512:LANES + 1024]) * (ML_QK_DIM ** -0.5)).astype(BF16)
    v_ref[0] = _dot(hb, w_ref[:, LANES + 1024:LANES + 2048]).astype(BF16)
    og_ref[0] = _dot(hb, w_ref[:, LANES + 2048:])


def _ml_in(x, pending, gain, sc, sh, w_pad, gate_bias):
    B, S, D = x.shape
    tm = TOKEN_TILE
    L = ML_CHUNK
    assert tm % L == 0
    per_tile = tm // L
    chunked = lambda b, j: (b, j, 0, 0)
    row = lambda b, j: (b, j, 0)
    per_b = lambda b, j: (b, 0, 0)
    const2 = lambda b, j: (0, 0)
    return pl.pallas_call(
        _ml_in_kernel,
        out_shape=(
            jax.ShapeDtypeStruct((B, S, 512), BF16),
            jax.ShapeDtypeStruct((B, S // L, 512, L), BF16),
            jax.ShapeDtypeStruct((B, S, 1024), BF16),
            jax.ShapeDtypeStruct((B, S, 1024), F32),
            jax.ShapeDtypeStruct((B, S // L, L, LANES), F32),
            jax.ShapeDtypeStruct((B, S // L, 2 * ML_HEADS, L), F32),
            jax.ShapeDtypeStruct((B, S, D), F32),
        ),
        grid=(B, S // tm),
        in_specs=[pl.BlockSpec((1, tm, D), row)] + _pending_specs(D, tm) + [
            pl.BlockSpec((1, D), const2),
            pl.BlockSpec((1, 1, D), per_b),
            pl.BlockSpec((1, 1, D), per_b),
            pl.BlockSpec((D, ML_COLS), const2),
            pl.BlockSpec((1, LANES), const2),
        ],
        out_specs=(
            pl.BlockSpec((1, tm, 512), row),
            pl.BlockSpec((1, per_tile, 512, L), chunked),
            pl.BlockSpec((1, tm, 1024), row),
            pl.BlockSpec((1, tm, 1024), row),
            pl.BlockSpec((1, per_tile, L, LANES), chunked),
            pl.BlockSpec((1, per_tile, 2 * ML_HEADS, L), chunked),
            pl.BlockSpec((1, tm, D), row),
        ),
        compiler_params=_cparams("parallel", "parallel"),
        name="mlstm_in_proj",
    )(x, *pending, gain, sc, sh, w_pad, gate_bias)


def _log_sigmoid(f):
    return jnp.minimum(f, 0.0) - jnp.log1p(jnp.exp(-jnp.abs(f)))


def _split3(x):
    a = x.astype(BF16)
    r = x - a.astype(F32)
    b = r.astype(BF16)
    c = (r - b.astype(F32)).astype(BF16)
    return a, b, c


def _twice(a):
    return jnp.concatenate([a, a], axis=1)


def _over_lanes(a, width):
    return jnp.concatenate([a] * (width // LANES), axis=1)


def _mlstm_kernel(q_ref, kt_ref, v_ref, grow_ref, gcol_ref, gain_ref, o_ref, c_ref, m_ref):
    L = ML_CHUNK
    HP = ML_STEP_HEADS
    S = q_ref.shape[1]
    c_ref[...] = jnp.zeros_like(c_ref)
    m_ref[...] = jnp.zeros_like(m_ref)

    def chunk(c, carry):
        r0 = pl.multiple_of(c * L, L)
        r_i = lax.broadcasted_iota(jnp.int32, (L, L), 0)
        c_i = lax.broadcasted_iota(jnp.int32, (L, L), 1)
        tril = c_i <= r_i
        lower = jnp.where(tril, 1.0, 0.0).astype(BF16)
        upper = jnp.where(r_i <= c_i, 1.0, 0.0).astype(BF16)
        e_r = lax.broadcasted_iota(jnp.int32, (LANES, HP * LANES), 0)
        e_c = lax.broadcasted_iota(jnp.int32, (LANES, HP * LANES), 1)
        pick = jnp.where(e_r == HP + e_c // LANES, 1.0, 0.0).astype(BF16)
        rows = grow_ref[0, 0, c]
        cols = gcol_ref[0, 0, c]
        b_rows = sum(_dot(p, upper) for p in _split3(_log_sigmoid(rows)))
        b_cols = sum(_dot(lower, p) for p in _split3(_log_sigmoid(cols)))
        b_colr = sum(_dot(p, pick) for p in _split3(b_cols))
        lane = lax.broadcasted_iota(jnp.int32, (1, L), 1)
        b_last_all = jnp.sum(jnp.where(lane == L - 1, b_rows, 0.0), axis=-1, keepdims=True)
        ones_v = jnp.ones((L, ML_V_DIM), BF16)
        for hh in range(HP):
            q = q_ref[0, pl.ds(r0, L), hh * ML_QK_DIM:(hh + 1) * ML_QK_DIM]
            kt = kt_ref[0, c, hh * ML_QK_DIM:(hh + 1) * ML_QK_DIM, :]
            v = v_ref[0, pl.ds(r0, L), hh * ML_V_DIM:(hh + 1) * ML_V_DIM]
            vx = jnp.concatenate([v, ones_v], axis=1)
            i_row = rows[hh:hh + 1, :]
            b_row = b_rows[HP + hh:HP + hh + 1, :]
            b_last = b_last_all[HP + hh:HP + hh + 1, :]
            b_col = b_colr[:, hh * LANES:(hh + 1) * LANES]
            m_prev = m_ref[hh]
            ctn = c_ref[hh]

            dmat = jnp.where(tril, _over_lanes(b_col, L) - b_row + i_row, NEG_INF)
            inter = b_col + m_prev
            m_t = jnp.maximum(inter, jnp.max(dmat, axis=-1, keepdims=True))
            w_intra = jnp.exp(dmat - _over_lanes(m_t, L))
            w_inter = jnp.exp(inter - m_t)
            intra = (w_intra * _dot(q, kt)).astype(BF16)
            tot = _twice(w_inter) * _dot(q, ctn.astype(BF16)) + _dot(intra, vx)
            num = tot[:, :ML_V_DIM]
            den = tot[:, ML_V_DIM:]
            hc = num / jnp.maximum(jnp.abs(den), jnp.exp(-m_t))
            y = hc * lax.rsqrt(jnp.mean(hc * hc, axis=-1, keepdims=True) + NORM_EPS)
            o_ref[0, pl.ds(r0, L), hh * ML_V_DIM:(hh + 1) * ML_V_DIM] = (
                y * gain_ref[:, hh * ML_V_DIM:(hh + 1) * ML_V_DIM])

            g_row = b_last - b_row + i_row
            m_new = jnp.maximum(b_last + m_prev, jnp.max(g_row, axis=-1, keepdims=True))
            decay = jnp.exp(b_last + m_prev - m_new)
            kw = (kt.astype(F32) * jnp.exp(g_row - _over_lanes(m_new, L))).astype(BF16)
            c_ref[hh] = _twice(decay) * ctn + _dot(kw, vx)
            m_ref[hh] = m_new
        return carry

    lax.fori_loop(0, S // L, chunk, 0)


def _mlstm(q, kt, v, g_rows, g_cols, out_gain):
    B, S, _ = q.shape
    assert ML_CHUNK % LANES == 0
    nc = S // ML_CHUNK
    hp = ML_STEP_HEADS
    return pl.pallas_call(
        _mlstm_kernel,
        out_shape=jax.ShapeDtypeStruct((B, S, ML_HEADS * ML_V_DIM), F32),
        grid=(B, ML_HEADS // hp),
        in_specs=[
            pl.BlockSpec((1, S, hp * ML_QK_DIM), lambda b, p: (b, 0, p)),
            pl.BlockSpec((1, nc, hp * ML_QK_DIM, ML_CHUNK), lambda b, p: (b, 0, p, 0)),
            pl.BlockSpec((1, S, hp * ML_V_DIM), lambda b, p: (b, 0, p)),
            pl.BlockSpec((1, 1, nc, 2 * hp, ML_CHUNK), lambda b, p: (b, p, 0, 0, 0)),
            pl.BlockSpec((1, 1, nc, ML_CHUNK, LANES), lambda b, p: (b, p, 0, 0, 0)),
            pl.BlockSpec((1, hp * ML_V_DIM), lambda b, p: (0, p)),
        ],
        out_specs=pl.BlockSpec((1, S, hp * ML_V_DIM), lambda b, p: (b, 0, p)),
        scratch_shapes=[
            pltpu.VMEM((hp, ML_QK_DIM, ML_V_DIM + LANES), F32),
            pltpu.VMEM((hp, 1, LANES), F32),
        ],
        compiler_params=_cparams("parallel", "parallel"),
        name="mlstm_chunkwise",
    )(q, kt, v, g_rows, g_cols, out_gain)


def _ml_out_kernel(hh_ref, og_ref, w_ref, x_ref, g_ref,
                   gain2_ref, sc2_ref, sh2_ref, wr_ref, br_ref, o_ref, h_ref, r_ref, rt_ref, cnt_ref, run_ref, tri_ref):
    a = jax.nn.sigmoid(og_ref[0]) * hh_ref[0]
    x_new = x_ref[0] + g_ref[0] * _dot(a.astype(BF16), w_ref[...])
    o_ref[0] = x_new
    _route_tile(x_new, gain2_ref, sc2_ref, sh2_ref, wr_ref, br_ref, h_ref, r_ref, rt_ref, cnt_ref, run_ref, tri_ref)


def _ml_out(hh, og, w_out_bf, x, g1, route_args):
    B, S, D = x.shape
    tm = OUT_TILE
    row = lambda b, j: (b, j, 0)
    per_b = lambda b, j: (b, 0, 0)
    const2 = lambda b, j: (0, 0)
    r_in, r_shape, r_out, r_scratch = _route_specs(B, S, D, tm)
    return pl.pallas_call(
        _ml_out_kernel,
        out_shape=(jax.ShapeDtypeStruct((B, S, D), F32),) + r_shape,
        grid=(B, S // tm),
        in_specs=[
            pl.BlockSpec((1, tm, D), row),
            pl.BlockSpec((1, tm, D), row),
            pl.BlockSpec((D, D), const2),
            pl.BlockSpec((1, tm, D), row),
            pl.BlockSpec((1, 1, D), per_b),
        ] + r_in,
        out_specs=(pl.BlockSpec((1, tm, D), row),) + r_out,
        scratch_shapes=r_scratch,
        compiler_params=_cparams("arbitrary", "arbitrary"),
        name="mlstm_out_proj",
    )(hh, og, w_out_bf, x, g1, *route_args)


def _first_argmax(x, lane, width):
    mx = jnp.max(x, axis=-1, keepdims=True)
    idx = jnp.min(jnp.where(x == mx, lane, width), axis=-1, keepdims=True)
    return mx, idx


def _route_tile(x, gain_ref, sc_ref, sh_ref, w_ref, b_ref, h_ref, r_ref, rt_ref, cnt_ref, run_ref, tri_ref):
    tm = x.shape[0]

    @pl.when(jnp.logical_and(pl.program_id(0) == 0, pl.program_id(1) == 0))
    def _():
        run_ref[...] = jnp.zeros_like(run_ref)
        r_i = lax.broadcasted_iota(jnp.int32, (tm, tm), 0)
        c_i = lax.broadcasted_iota(jnp.int32, (tm, tm), 1)
        tri_ref[...] = jnp.where(c_i < r_i, 1.0, 0.0).astype(BF16)

    h = _modulated_norm(x, gain_ref[...], sc_ref[0], sh_ref[0])
    h_ref[0] = _pack_bf16_pairs(h)
    logits = _dot(h.astype(BF16), w_ref[...]) + b_ref[...]
    lane = lax.broadcasted_iota(jnp.int32, (1, LANES), 1)

    def pick(lgt):
        lg = jnp.where(lane < N_GROUPS, lgt, NEG_INF)
        g_max, g_sel = _first_argmax(lg, lane, LANES)
        pg = 1.0 / jnp.sum(jnp.exp(lg - g_max), axis=-1, keepdims=True)
        e_lane = lane - N_GROUPS
        in_grp = jnp.logical_and(e_lane >= g_sel * EXPERTS_PER_GROUP, e_lane < (g_sel + 1) * EXPERTS_PER_GROUP)
        le = jnp.where(in_grp, lgt, NEG_INF)
        v1, i1 = _first_argmax(le, lane, LANES)
        v2, i2 = _first_argmax(jnp.where(lane == i1, NEG_INF, le), lane, LANES)
        ratio = jnp.exp(v2 - v1)
        return i1 - N_GROUPS, i2 - N_GROUPS, pg / (1.0 + ratio), pg * ratio / (1.0 + ratio)

    e1, e2, w1, w2 = pick(logits)
    hot1 = lane == e1
    hot2 = lane == e2
    onehot = jnp.where(jnp.logical_or(hot1, hot2), 1.0, 0.0)
    seen = _dot(tri_ref[...], onehot.astype(BF16)) + run_ref[...]
    rank1 = jnp.sum(jnp.where(hot1, seen, 0.0), axis=-1, keepdims=True)
    rank2 = jnp.sum(jnp.where(hot2, seen, 0.0), axis=-1, keepdims=True)
    run_ref[...] = run_ref[...] + jnp.sum(onehot, axis=0, keepdims=True)
    cnt_ref[...] = run_ref[...]

    out = jnp.where(lane == 0, e1.astype(F32), 0.0)
    out = jnp.where(lane == 1, e2.astype(F32), out)
    out = jnp.where(lane == 2, w1, out)
    out = jnp.where(lane == 3, w2, out)
    out = jnp.where(lane == 4, rank1, out)
    out = jnp.where(lane == 5, rank2, out)
    r_ref[0] = out
    rt_ref[0] = out.T[:8, :]


def _route_specs(B, S, D, tm):
    row = lambda b, j: (b, j, 0)
    per_b = lambda b, j: (b, 0, 0)
    const2 = lambda b, j: (0, 0)
    in_specs = [
        pl.BlockSpec((1, D), const2),
        pl.BlockSpec((1, 1, D), per_b),
        pl.BlockSpec((1, 1, D), per_b),
        pl.BlockSpec((D, LANES), const2),
        pl.BlockSpec((1, LANES), const2),
    ]
    out_shape = (
        jax.ShapeDtypeStruct((B, S, D // 2), jnp.int32),
        jax.ShapeDtypeStruct((B, S, LANES), F32),
        jax.ShapeDtypeStruct((B, 8, S), F32),
        jax.ShapeDtypeStruct((1, LANES), F32),
    )
    out_specs = (
        pl.BlockSpec((1, tm, D // 2), row),
        pl.BlockSpec((1, tm, LANES), row),
        pl.BlockSpec((1, 8, tm), lambda b, j: (b, 0, j)),
        pl.BlockSpec((1, LANES), const2),
    )
    return in_specs, out_shape, out_specs, [pltpu.VMEM((1, LANES), F32), pltpu.VMEM((tm, tm), BF16)]


def _experts_kernel(blk_e_ref, n_used_ref, next_e_ref, x_ref, wg_hbm, wu_hbm, wd_hbm, o_ref,
                    wg_f, wu_f, wd_f, wg_s, wu_s, wd_s, sem, *, layer):
    i = pl.program_id(0)
    used = i < n_used_ref[0]
    e = blk_e_ref[i]
    new_expert = jnp.logical_or(i == 0, e != blk_e_ref[jnp.maximum(i - 1, 0)])

    def fetch(expert):
        return (pltpu.make_async_copy(wg_hbm.at[layer, expert], wg_f, sem.at[0]),
                pltpu.make_async_copy(wu_hbm.at[layer, expert], wu_f, sem.at[1]),
                pltpu.make_async_copy(wd_hbm.at[layer, expert], wd_f, sem.at[2]))

    @pl.when(i == 0)
    def _():
        for cp in fetch(e):
            cp.start()

    @pl.when(jnp.logical_and(used, new_expert))
    def _():
        for cp in fetch(e):
            cp.wait()
        wg_s[...] = wg_f[...].astype(BF16)
        wu_s[...] = wu_f[...].astype(BF16)
        wd_s[...] = wd_f[...].astype(BF16)

        @pl.when(next_e_ref[i] >= 0)
        def _():
            for cp in fetch(next_e_ref[i]):
                cp.start(priority=1)

    @pl.when(used)
    def _():
        x = _unpack_bf16_pairs(x_ref[...]).astype(BF16)
        a = _dot(x, wg_s[...])
        u = _dot(x, wu_s[...])
        act = a * jax.nn.sigmoid(a) * u
        o_ref[...] = _pack_bf16_pairs(_dot(act.astype(BF16), wd_s[...]))

    @pl.when(i >= n_used_ref[0])
    def _():
        o_ref[...] = jnp.zeros_like(o_ref)


def _experts(layer, blk_e, n_used, next_e, xs, w_gate, w_up, w_down):
    R = xs.shape[0]
    D = 2 * xs.shape[1]
    n_blk = R // MOE_BLOCK
    rows = lambda i, be, nu, ne: (i, 0)
    grid_spec = pltpu.PrefetchScalarGridSpec(
        num_scalar_prefetch=3,
        grid=(n_blk,),
        in_specs=[
            pl.BlockSpec((MOE_BLOCK, D // 2), rows),
            pl.BlockSpec(memory_space=pl.ANY),
            pl.BlockSpec(memory_space=pl.ANY),
            pl.BlockSpec(memory_space=pl.ANY),
        ],
        out_specs=pl.BlockSpec((MOE_BLOCK, D // 2), rows),
        scratch_shapes=[
            pltpu.VMEM((D, D_EXPERT), F32),
            pltpu.VMEM((D, D_EXPERT), F32),
            pltpu.VMEM((D_EXPERT, D), F32),
            pltpu.VMEM((D, D_EXPERT), BF16),
            pltpu.VMEM((D, D_EXPERT), BF16),
            pltpu.VMEM((D_EXPERT, D), BF16),
            pltpu.SemaphoreType.DMA((3,)),
        ],
    )
    return pl.pallas_call(
        functools.partial(_experts_kernel, layer=layer),
        out_shape=jax.ShapeDtypeStruct((R, D // 2), jnp.int32),
        grid_spec=grid_spec,
        compiler_params=_cparams("arbitrary"),
        name="moe_experts",
    )(blk_e, n_used, next_e, xs, w_gate, w_up, w_down)


def _residual_tile(x_ref, pending_refs):
    if not pending_refs:
        return x_ref[0]
    g_ref, y0_ref, y1_ref, r_ref = pending_refs
    r = r_ref[0]
    y = _unpack_bf16_pairs(y0_ref[0, 0]) * r[:, 2:3] + _unpack_bf16_pairs(y1_ref[0, 0]) * r[:, 3:4]
    return x_ref[0] + g_ref[0] * y


def _pending_specs(D, tm):
    return [
        pl.BlockSpec((1, 1, D), lambda b, j: (b, 0, 0)),
        pl.BlockSpec((1, 1, tm, D // 2), lambda b, j: (0, b, j, 0)),
        pl.BlockSpec((1, 1, tm, D // 2), lambda b, j: (1, b, j, 0)),
        pl.BlockSpec((1, tm, LANES), lambda b, j: (b, j, 0)),
    ]


def _combine_kernel(x_ref, g_ref, y0_ref, y1_ref, r_ref, o_ref):
    o_ref[0] = _residual_tile(x_ref, (g_ref, y0_ref, y1_ref, r_ref))


def _combine(x, pending):
    B, S, D = x.shape
    tm = OUT_TILE
    row = lambda b, j: (b, j, 0)
    return pl.pallas_call(
        _combine_kernel,
        out_shape=jax.ShapeDtypeStruct((B, S, D), F32),
        grid=(B, S // tm),
        in_specs=[pl.BlockSpec((1, tm, D), row)] + _pending_specs(D, tm),
        out_specs=pl.BlockSpec((1, tm, D), row),
        compiler_params=_cparams("parallel", "parallel"),
        name="moe_combine",
    )(x, *pending)


SC_CORES = 2
SC_SUBCORES = 16
SC_WORKERS = SC_CORES * SC_SUBCORES
SC_CHUNK = 64


def _sc_mesh():
    return plsc.VectorSubcoreMesh(core_axis_name="c", subcore_axis_name="s",
                                  num_cores=SC_CORES, num_subcores=SC_SUBCORES)


def _sc_scatter_rows(src, idx, n_out):
    T, W = src.shape
    per_w = T // SC_WORKERS
    nch = per_w // SC_CHUNK
    idx4 = idx.reshape(TOP_K, SC_WORKERS, nch, SC_CHUNK)

    @functools.partial(
        pl.kernel, mesh=_sc_mesh(),
        out_type=jax.ShapeDtypeStruct((n_out, W), src.dtype),
        scratch_types=[
            pltpu.VMEM((TOP_K, nch, SC_CHUNK), jnp.int32),
            pltpu.VMEM((2, SC_CHUNK, W), src.dtype),
            pltpu.SemaphoreType.DMA((2,)),
            pltpu.SemaphoreType.DMA((2 * TOP_K,)),
        ],
        name="sc_scatter_rows",
    )
    def body(src_hbm, idx_hbm, out_hbm, idx_v, rows_v, load_sem, scatter_sem):
        wid = lax.axis_index("s") * SC_CORES + lax.axis_index("c")
        for s in range(TOP_K):
            pltpu.sync_copy(idx_hbm.at[s, wid], idx_v.at[s])

        def load(j, b):
            return pltpu.make_async_copy(
                src_hbm.at[pl.ds(wid * per_w + j * SC_CHUNK, SC_CHUNK)], rows_v.at[b], load_sem.at[b])

        def scatter(s, j, b):
            return pltpu.make_async_copy(rows_v.at[b], out_hbm.at[idx_v.at[s, j]], scatter_sem.at[2 * s + b])

        load(0, 0).start()

        @pl.loop(0, nch, step=2)
        def _(i):
            for b in range(2):
                j = i + b

                @pl.when(j >= 1)
                def _():
                    for s in range(TOP_K):
                        scatter(s, j - 1, 1 - b).wait()

                @pl.when(j + 1 < nch)
                def _():
                    load(j + 1, 1 - b).start()

                load(j, b).wait()
                for s in range(TOP_K):
                    scatter(s, j, b).start()

        for s in range(TOP_K):
            scatter(s, nch - 1, (nch - 1) % 2).wait()

    assert nch % 2 == 0
    return body(src, idx4)


def _sc_gather_rows(table, idx):
    N = idx.shape[0]
    W = table.shape[1]
    per_w = N // SC_WORKERS
    nch = per_w // SC_CHUNK
    idx3 = idx.reshape(SC_WORKERS, nch, SC_CHUNK)

    @functools.partial(
        pl.kernel, mesh=_sc_mesh(),
        out_type=jax.ShapeDtypeStruct((N, W), table.dtype),
        scratch_types=[
            pltpu.VMEM((nch, SC_CHUNK), jnp.int32),
            pltpu.VMEM((2, SC_CHUNK, W), table.dtype),
            pltpu.SemaphoreType.DMA((2,)),
            pltpu.SemaphoreType.DMA((2,)),
        ],
        name="sc_gather_rows",
    )
    def body(table_hbm, idx_hbm, out_hbm, idx_v, rows_v, gather_sem, write_sem):
        wid = lax.axis_index("s") * SC_CORES + lax.axis_index("c")
        pltpu.sync_copy(idx_hbm.at[wid], idx_v)

        def gather(j, b):
            return pltpu.make_async_copy(table_hbm.at[idx_v.at[j]], rows_v.at[b], gather_sem.at[b])

        def write(j, b):
            return pltpu.make_async_copy(
                rows_v.at[b], out_hbm.at[pl.ds(wid * per_w + j * SC_CHUNK, SC_CHUNK)], write_sem.at[b])

        gather(0, 0).start()

        @pl.loop(0, nch, step=2)
        def _(i):
            for b in range(2):
                j = i + b

                @pl.when(j >= 1)
                def _():
                    write(j - 1, 1 - b).wait()

                @pl.when(j + 1 < nch)
                def _():
                    gather(j + 1, 1 - b).start()

                gather(j, b).wait()
                write(j, b).start()

        write(nch - 1, (nch - 1) % 2).wait()

    assert nch % 2 == 0
    return body(table, idx3)


def _moe_dispatch(route_t, counts, T):
    A = T * TOP_K
    counts = counts[0, :N_EXPERTS].astype(jnp.int32)
    blocks_per = (counts + MOE_BLOCK - 1) // MOE_BLOCK
    block_end = jnp.cumsum(blocks_per)
    block_start = block_end - blocks_per
    expert = jnp.swapaxes(route_t[:, :TOP_K, :], 0, 1).reshape(TOP_K, T).astype(jnp.int32)
    rank = jnp.swapaxes(route_t[:, 4:4 + TOP_K, :], 0, 1).reshape(TOP_K, T).astype(jnp.int32)
    onehot = expert[None] == jnp.arange(N_EXPERTS, dtype=jnp.int32)[:, None, None]
    start = jnp.sum(jnp.where(onehot, block_start[:, None, None], 0), axis=0)
    dest = start * MOE_BLOCK + rank
    n_blk = -(-A // MOE_BLOCK) + N_EXPERTS
    blk = jnp.arange(n_blk, dtype=jnp.int32)
    blk_e = jnp.minimum(jnp.sum(blk[:, None] >= block_end[None, :], axis=-1), N_EXPERTS - 1).astype(jnp.int32)
    n_used = block_end[-1]
    first = jnp.logical_and(blk < n_used, jnp.logical_or(blk == 0, blk_e != jnp.roll(blk_e, 1)))
    first_pos = jnp.where(first, blk, n_blk)
    next_pos = jnp.concatenate([lax.cummin(first_pos, axis=0, reverse=True)[1:], jnp.full((1,), n_blk, jnp.int32)])
    next_e = jnp.where(next_pos < n_blk, blk_e[jnp.minimum(next_pos, n_blk - 1)], -1).astype(jnp.int32)
    return dest, n_blk * MOE_BLOCK, blk_e, n_used.reshape(1).astype(jnp.int32), next_e


def _rope_tables(S):
    inv = 1.0 / (ROPE_THETA ** (jnp.arange(0, HEAD_DIM, 2, dtype=F32) / HEAD_DIM))
    ang = jnp.arange(S, dtype=F32)[:, None] * inv[None, :]
    cos, sin = jnp.cos(ang), jnp.sin(ang)
    cos_h = jnp.concatenate([cos, cos], axis=-1)
    sin_h = jnp.concatenate([-sin, sin], axis=-1)
    return jnp.tile(cos_h, (1, ATTN_HEADS)), jnp.tile(sin_h, (1, ATTN_HEADS))


def _pad_cols(w, width):
    return jnp.pad(w, ((0, 0), (0, width - w.shape[1])))


def kernel(x, c, ada_w, ada_b, norm_mix, norm_ffn, hy_w_in, hy_q_norm, hy_k_norm, hy_conv_w, hy_w_out, ml_w_in, ml_b_gates, ml_out_norm, ml_w_out, moe_w_group, moe_b_group, moe_w_expert, moe_b_expert, moe_w_gate, moe_w_up, moe_w_down):
    B, S, D = x.shape
    T = B * S
    cos_t, sin_t = _rope_tables(S)
    mod = _ada_modulation(c, ada_w, ada_b).reshape(DEPTH, B, 6, 1, D)
    r_i = np.arange(ATTN_WIDTH)
    grp = jnp.asarray((r_i[:, None] // HEAD_DIM) == (r_i[None, :] // HEAD_DIM), dtype=BF16)

    pending = ()
    for l in range(DEPTH):
        sh1, sc1, g1, sh2, sc2, g2 = [mod[l, :, i] for i in range(6)]
        gain1 = norm_mix[l].reshape(1, D)
        w_r = _pad_cols(jnp.concatenate([moe_w_group[l], moe_w_expert[l]], axis=1), LANES).astype(BF16)
        b_r = jnp.pad(jnp.concatenate([moe_b_group[l], moe_b_expert[l]]), (0, LANES - N_GROUPS - N_EXPERTS))
        route_args = (norm_ffn[l].reshape(1, D), sc2, sh2, w_r, b_r.reshape(1, LANES))
        j = l // 2
        if l % 2 == 0:
            w = hy_w_in[j]
            o = np.cumsum((0,) + (ATTN_WIDTH, HEAD_DIM, HEAD_DIM, IDX_HEADS * IDX_DIM, IDX_DIM, IDX_HEADS,
                                  CONV_WIDTH, CONV_WIDTH, CONV_WIDTH))
            wq, wk, wv, wiq, wik, wiw, wbg, wcg, wu = [w[:, o[i]:o[i + 1]] for i in range(9)]
            w_pad = jnp.concatenate(
                [wk, wv, _pad_cols(jnp.concatenate([wik, wiw], axis=1), LANES), wq, wiq, wbg, wcg, wu],
                axis=1).astype(BF16)
            qn_t = jnp.tile(hy_q_norm[j], ATTN_HEADS).reshape(1, ATTN_WIDTH)
            kn_t = jnp.tile(hy_k_norm[j], LANES // HEAD_DIM).reshape(1, LANES)
            outs = _hyb_in(x, pending, gain1, sc1, sh1, w_pad, cos_t, sin_t, qn_t, kn_t, grp)
            qt, iqt, bcu, kv, kvt, ik, iwt = outs[:7]
            if pending:
                x = outs[7]
            y_attn = _dsa_attention_t(qt, iqt, iwt, kv, kvt, ik)
            x, h2, route, route_t, counts =_hyb_out(y_attn, bcu, hy_conv_w[j], hy_w_out[j].astype(BF16), x, g1, route_args)
        else:
            w = ml_w_in[j]
            hq = ML_HEADS * ML_QK_DIM
            hv = ML_HEADS * ML_V_DIM
            wq, wk, wv = w[:, :hq], w[:, hq:2 * hq], w[:, 2 * hq:2 * hq + hv]
            wg = w[:, 2 * hq + hv:2 * hq + hv + 2 * ML_HEADS]
            wo = w[:, 2 * hq + hv + 2 * ML_HEADS:]
            w_pad = jnp.concatenate([_pad_cols(wg, LANES), wk, wq, wv, wo], axis=1).astype(BF16)
            gate_bias = jnp.pad(ml_b_gates[j], (0, LANES - 2 * ML_HEADS)).reshape(1, LANES)
            q, k, v, og, g_cols, g_rows, x = _ml_in(x, pending, gain1, sc1, sh1, w_pad, gate_bias)
            assert ML_STEP_HEADS == ML_HEADS
            hh = _mlstm(q, k, v, g_rows[:, None], g_cols[:, None], ml_out_norm[j].reshape(1, hv))
            x, h2, route, route_t, counts =_ml_out(hh, og, ml_w_out[j].astype(BF16), x, g1, route_args)

        dest, n_rows, blk_e, n_used, next_e = _moe_dispatch(route_t, counts, T)
        xs = _sc_scatter_rows(h2.reshape(T, D // 2), dest, n_rows)
        ys = _experts(l, blk_e, n_used, next_e, xs, moe_w_gate, moe_w_up, moe_w_down)
        y01 = _sc_gather_rows(ys, dest.reshape(TOP_K * T)).reshape(TOP_K, B, S, D // 2)
        pending = (g2, y01, y01, route)
    return _combine(x, pending)
```

```python
import functools

import numpy as np
import jax
import jax.numpy as jnp
from jax import lax
from jax.experimental import pallas as pl
from jax.experimental.pallas import tpu as pltpu
from jax.experimental.pallas import tpu_sc as plsc

F32 = jnp.float32
BF16 = jnp.bfloat16
HIGHEST = lax.Precision.HIGHEST

D_MODEL = 1024
DEPTH = 4
ATTN_HEADS = 8
HEAD_DIM = 64
ATTN_WIDTH = ATTN_HEADS * HEAD_DIM
IDX_HEADS = 8
IDX_DIM = 64
INDEX_TOPK = 256
Q_BLOCK = 256
ROPE_THETA = 10000.0
CONV_WIDTH = D_MODEL - ATTN_WIDTH
CONV_K = 3
ML_HEADS = 8
ML_QK_DIM = 64
ML_V_DIM = 128
N_GROUPS = 4
EXPERTS_PER_GROUP = 8
N_EXPERTS = N_GROUPS * EXPERTS_PER_GROUP
TOP_K = 2
D_EXPERT = 512
MOE_BLOCK = 512
NORM_EPS = 1e-6

LANES = 128
VMEM_LIMIT = 56 * 1024 * 1024
TOKEN_TILE = 512
OUT_TILE = 1024
ML_CHUNK = 256
ML_STEP_HEADS = 8
NEG_INF = float("-inf")
LOG2_E = 1.4426950408889634


def _cparams(*sem):
    return pltpu.CompilerParams(dimension_semantics=sem, vmem_limit_bytes=VMEM_LIMIT)


def _dot(a, b):
    return jnp.dot(a, b, preferred_element_type=F32)


def _pack_bf16_pairs(x):
    bits = lax.bitcast_convert_type(x.astype(BF16).astype(F32), jnp.uint32)
    half = bits.shape[1] // 2
    packed = (bits[:, :half] >> 16) | (bits[:, half:] & jnp.uint32(0xFFFF0000))
    return lax.bitcast_convert_type(packed, jnp.int32)


def _unpack_bf16_pairs(words):
    words = lax.bitcast_convert_type(words, jnp.uint32)
    return jnp.concatenate(
        [lax.bitcast_convert_type(words << 16, F32),
         lax.bitcast_convert_type(words & jnp.uint32(0xFFFF0000), F32)], axis=1)


def _split_dot(a_f32, b_bf16):
    hi = a_f32.astype(BF16)
    lo = (a_f32 - hi.astype(F32)).astype(BF16)
    return _dot(hi, b_bf16) + _dot(lo, b_bf16)


def _ada_kernel(c_ref, w_ref, b_ref, o_ref):
    c = c_ref[...]
    ca = c * jax.nn.sigmoid(c)
    o_ref[0] = jnp.dot(ca, w_ref[0], precision=HIGHEST, preferred_element_type=F32) + b_ref[0]


def _ada_modulation(c, ada_w, ada_b):
    B, D = c.shape
    n_col = ada_w.shape[-1] // D
    return pl.pallas_call(
        _ada_kernel,
        out_shape=jax.ShapeDtypeStruct((DEPTH, B, n_col * D), F32),
        grid=(DEPTH, n_col),
        in_specs=[
            pl.BlockSpec((B, D), lambda l, j: (0, 0)),
            pl.BlockSpec((1, D, D), lambda l, j: (l, 0, j)),
            pl.BlockSpec((1, 1, D), lambda l, j: (l, 0, j)),
        ],
        out_specs=pl.BlockSpec((1, B, D), lambda l, j: (l, 0, j)),
        compiler_params=_cparams("parallel", "parallel"),
        name="ada_modulation",
    )(c, ada_w, ada_b.reshape(DEPTH, 1, n_col * D))


def _modulated_norm(x, gain, scale, shift):
    y = x * lax.rsqrt(jnp.mean(x * x, axis=-1, keepdims=True) + NORM_EPS)
    return y * gain * (1.0 + scale) + shift


def _rope(x, cos, sin_signed, first_half):
    w = x.shape[-1]
    partner = jnp.where(first_half, pltpu.roll(x, w - HEAD_DIM // 2, 1), pltpu.roll(x, HEAD_DIM // 2, 1))
    return x * cos + partner * sin_signed


HYB_COLS = 5 * 512 + 2 * LANES


def _hyb_in_kernel(*refs, n_pending):
    x_ref, pending_refs, refs = refs[0], refs[1:1 + n_pending], refs[1 + n_pending:]
    (gain_ref, sc_ref, sh_ref, w_ref, cos_ref, sin_ref, qn_ref, kn_ref, grp_ref,
     qt_ref, iqt_ref, bcu_ref, kv_ref, kvt_ref, ik_ref, iwt_ref) = refs[:16]
    x = _residual_tile(x_ref, pending_refs)
    if n_pending:
        refs[16][0] = x
    h = _modulated_norm(x, gain_ref[...], sc_ref[0], sh_ref[0])
    hb = h.astype(BF16)
    p_small = _dot(hb, w_ref[:, 0:2 * LANES])
    p_q = _dot(hb, w_ref[:, 2 * LANES:2 * LANES + 512])
    p_iq = _dot(hb, w_ref[:, 2 * LANES + 512:2 * LANES + 1024])
    cos = cos_ref[...]
    sin = sin_ref[...]
    lane = lax.broadcasted_iota(jnp.int32, (1, ATTN_WIDTH), 1)
    first_half = (lane % HEAD_DIM) < (HEAD_DIM // 2)
    fh128 = first_half[:, :LANES]
    lane128 = lane[:, :LANES]

    kv = p_small[:, :LANES]
    is_k = lane128 < HEAD_DIM
    kk = jnp.where(is_k, kv, 0.0)
    ms_k = jnp.sum(kk * kk, axis=-1, keepdims=True) * (1.0 / HEAD_DIM)
    kn = kv * lax.rsqrt(ms_k + NORM_EPS) * kn_ref[...]
    kr = _rope(kn, cos[:, :LANES], sin[:, :LANES], fh128)
    kv = jnp.where(is_k, kr, kv)
    kv_ref[0] = kv.astype(BF16)
    feat = lax.broadcasted_iota(jnp.int32, (LANES, 1), 0)
    kv_t = jnp.where(feat < HEAD_DIM, 1.0, kv.T).astype(BF16)
    for i in range(kvt_ref.shape[1]):
        kvt_ref[0, i] = kv_t[:, i * DSA_KEY_CHUNK:(i + 1) * DSA_KEY_CHUNK]

    sm = p_small[:, LANES:]
    ikr = _rope(sm, cos[:, :LANES], sin[:, :LANES], fh128)
    ik_ref[0] = jnp.where(is_k, ikr, 0.0).astype(BF16)
    iwt_ref[0] = sm.T

    ms = _split_dot(p_q * p_q, grp_ref[...]) * (1.0 / HEAD_DIM)
    q = p_q * lax.rsqrt(ms + NORM_EPS) * qn_ref[...]
    qt_ref[0] = (_rope(q, cos, sin, first_half) * (HEAD_DIM ** -0.5 * LOG2_E)).T.astype(BF16)
    iqt_ref[0] = (_rope(p_iq, cos, sin, first_half) * (IDX_DIM ** -0.5)).T.astype(BF16)

    bcu_ref[0] = _dot(hb, w_ref[:, 2 * LANES + 1024:])


def _hyb_in(x, pending, gain, sc, sh, w_pad, cos_t, sin_t, qn_t, kn_t, grp):
    B, S, D = x.shape
    tm = TOKEN_TILE
    row = lambda b, j: (b, j, 0)
    per_b = lambda b, j: (b, 0, 0)
    const2 = lambda b, j: (0, 0)
    tab = lambda b, j: (j, 0)
    col = lambda b, j: (b, 0, j)
    assert tm % DSA_KEY_CHUNK == 0
    x_out_shape = (jax.ShapeDtypeStruct((B, S, D), F32),) if pending else ()
    x_out_spec = (pl.BlockSpec((1, tm, D), row),) if pending else ()
    return pl.pallas_call(
        functools.partial(_hyb_in_kernel, n_pending=len(pending)),
        out_shape=(
            jax.ShapeDtypeStruct((B, 512, S), BF16),
            jax.ShapeDtypeStruct((B, 512, S), BF16),
            jax.ShapeDtypeStruct((B, S, 1536), F32),
            jax.ShapeDtypeStruct((B, S, LANES), BF16),
            jax.ShapeDtypeStruct((B, S // DSA_KEY_CHUNK, LANES, DSA_KEY_CHUNK), BF16),
            jax.ShapeDtypeStruct((B, S, LANES), BF16),
            jax.ShapeDtypeStruct((B, LANES, S), F32),
        ) + x_out_shape,
        grid=(B, S // tm),
        in_specs=[pl.BlockSpec((1, tm, D), row)] + (_pending_specs(D, tm) if pending else []) + [
            pl.BlockSpec((1, D), const2),
            pl.BlockSpec((1, 1, D), per_b),
            pl.BlockSpec((1, 1, D), per_b),
            pl.BlockSpec((D, HYB_COLS), const2),
            pl.BlockSpec((tm, 512), tab),
            pl.BlockSpec((tm, 512), tab),
            pl.BlockSpec((1, 512), const2),
            pl.BlockSpec((1, LANES), const2),
            pl.BlockSpec((512, 512), const2),
        ],
        out_specs=(
            pl.BlockSpec((1, 512, tm), col),
            pl.BlockSpec((1, 512, tm), col),
            pl.BlockSpec((1, tm, 1536), row),
            pl.BlockSpec((1, tm, LANES), row),
            pl.BlockSpec((1, tm // DSA_KEY_CHUNK, LANES, DSA_KEY_CHUNK), lambda b, j: (b, j, 0, 0)),
            pl.BlockSpec((1, tm, LANES), row),
            pl.BlockSpec((1, LANES, tm), col),
        ) + x_out_spec,
        compiler_params=_cparams("parallel", "parallel"),
        name="hybrid_in_proj",
    )(x, *pending, gain, sc, sh, w_pad, cos_t, sin_t, qn_t, kn_t, grp)


DSA_KEY_CHUNK = 512
DSA_SUM_ROWS = 16


def _fold8(x, op):
    parts = x.reshape(x.shape[0] // 8, 8, x.shape[1])
    while parts.shape[0] > 1:
        half = parts.shape[0] // 2
        assert parts.shape[0] == 2 * half
        parts = op(parts[:half], parts[half:])
    return parts[0]


def _col_reduce(x, op):
    t = _fold8(x, op)
    for shift in (4, 2, 1):
        t = op(t, pltpu.roll(t, shift, 0))
    return t[0:1, :]


def _dsa_t_kernel(qt_ref, iqt_ref, iwt_ref, kv_ref, kvt_ref, ik_ref, o_ref, sc_ref, bias_ref, acc_ref):
    CK = DSA_KEY_CHUNK
    QB = Q_BLOCK
    qb = pl.program_id(1)
    nk = (qb * QB + QB + CK - 1) // CK
    kf = float(INDEX_TOPK)
    qpos = qb * QB + lax.broadcasted_iota(jnp.int32, (1, QB), 1)
    krow = lax.broadcasted_iota(jnp.int32, (CK, 1), 0)
    w_idx = iwt_ref[0, IDX_DIM:IDX_DIM + IDX_HEADS, :] * (IDX_HEADS ** -0.5)

    def rows(c):
        return pl.ds(pl.multiple_of(c * CK, CK), CK)

    def heads_on_lanes(ref, width):
        return jnp.concatenate([ref[0, hd * width:(hd + 1) * width, :] for hd in range(ref.shape[1] // width)], axis=1)

    def head_lanes(hd):
        return slice(hd * QB, (hd + 1) * QB)

    iq_wide = heads_on_lanes(iqt_ref, IDX_DIM)
    w_wide = jnp.concatenate([w_idx[hd:hd + 1, :] for hd in range(IDX_HEADS)], axis=1)

    def score_chunk(c, carry):
        mx, mn = carry
        ikc = ik_ref[0, rows(c), :][:, :IDX_DIM]
        s_all = jnp.maximum(_dot(ikc, iq_wide), 0.0) * w_wide
        acc = s_all[:, head_lanes(0)]
        for hd in range(1, IDX_HEADS):
            acc = acc + s_all[:, head_lanes(hd)]
        causal = (c * CK + krow) <= qpos
        sc_ref[rows(c), :] = jnp.where(causal, acc, NEG_INF)
        mx = jnp.maximum(mx, _fold8(jnp.where(causal, acc, NEG_INF), jnp.maximum))
        mn = jnp.minimum(mn, _fold8(jnp.where(causal, acc, jnp.inf), jnp.minimum))
        return mx, mn

    mx8, mn8 = lax.fori_loop(0, nk, score_chunk,
                             (jnp.full((8, QB), NEG_INF, F32), jnp.full((8, QB), jnp.inf, F32)))
    row_max = jnp.max(mx8, axis=0, keepdims=True)
    row_min = jnp.min(mn8, axis=0, keepdims=True)

    def count(pred):
        def body(c, part):
            return part + _fold8(jnp.where(pred(sc_ref[rows(c), :]), 1.0, 0.0), jnp.add)
        part = lax.fori_loop(0, nk, body, jnp.zeros((8, QB), F32))
        return jnp.sum(part, axis=0, keepdims=True)

    @pl.when(qb * QB + QB <= INDEX_TOPK)
    def _():
        def body(c, carry):
            bias_ref[rows(c), :] = jnp.where(sc_ref[rows(c), :] > NEG_INF, 0.0, NEG_INF)
            return carry
        lax.fori_loop(0, nk, body, 0)

    @pl.when(qb * QB + QB > INDEX_TOPK)
    def _():
        def bisect(_, carry):
            lo, hi, c_lo, c_hi = carry
            mid = 0.5 * lo + 0.5 * jnp.minimum(hi, row_max)
            cnt = count(lambda x: x >= mid)
            ge = cnt >= kf
            return (jnp.where(ge, mid, lo), jnp.where(ge, hi, mid),
                    jnp.where(ge, cnt, c_lo), jnp.where(ge, c_hi, cnt))

        n_adm = (qpos + 1).astype(F32)
        lo, hi, c_lo, c_hi = lax.fori_loop(
            0, 18, bisect, (row_min, jnp.full((1, QB), jnp.inf, F32), n_adm, jnp.zeros((1, QB), F32)))

        def refine_cond(carry):
            it, _, _, _, done = carry
            return jnp.logical_and(it < nk * CK, jnp.min(done) < 0.5)

        def refine(carry):
            it, hi, c_hi, thr, done = carry

            def edges(c, part):
                up, dn = part
                x = sc_ref[rows(c), :]
                return (jnp.maximum(up, _fold8(jnp.where(x < hi, x, NEG_INF), jnp.maximum)),
                        jnp.minimum(dn, _fold8(jnp.where(x >= lo, x, jnp.inf), jnp.minimum)))

            up8, dn8 = lax.fori_loop(0, nk, edges,
                                     (jnp.full((8, QB), NEG_INF, F32), jnp.full((8, QB), jnp.inf, F32)))
            m_up = jnp.max(up8, axis=0, keepdims=True)
            m_dn = jnp.min(dn8, axis=0, keepdims=True)
            from_hi = c_hi == kf - 1.0
            from_lo = c_lo == kf
            thr = jnp.where(done > 0.5, thr, jnp.where(from_hi, m_up, m_dn))
            done = jnp.where(jnp.logical_or(from_hi, from_lo), 1.0, done)

            def step_down(args):
                hi, c_hi, thr, done = args
                open_ = done < 0.5
                cnt = count(lambda x: x >= m_up)
                hit = jnp.logical_and(open_, cnt >= kf)
                moved = jnp.logical_and(open_, cnt < kf)
                return (jnp.where(moved, m_up, hi), jnp.where(moved, cnt, c_hi), jnp.where(hit, m_up, thr),
                        jnp.where(hit, 1.0, done))

            hi, c_hi, thr, done = lax.cond(jnp.min(done) > 0.5, lambda args: args, step_down, (hi, c_hi, thr, done))
            return it + 1, hi, c_hi, thr, done

        _, _, _, thr, _ = lax.while_loop(
            refine_cond, refine, (jnp.int32(0), hi, c_hi, row_max, jnp.zeros((1, QB), F32)))

        def body(c, sel):
            keep = sc_ref[rows(c), :] >= thr
            bias_ref[rows(c), :] = jnp.where(keep, 0.0, NEG_INF)
            return sel + _fold8(jnp.where(keep, 1.0, 0.0), jnp.add)

        n_sel = jnp.sum(lax.fori_loop(0, nk, body, jnp.zeros((8, QB), F32)), axis=0, keepdims=True)

        @pl.when(jnp.max(n_sel) > kf + 0.5)
        def _():
            need = kf - count(lambda x: x > thr)
            r_i = lax.broadcasted_iota(jnp.int32, (CK, CK), 0)
            c_i = lax.broadcasted_iota(jnp.int32, (CK, CK), 1)
            lower = jnp.where(c_i <= r_i, 1.0, 0.0).astype(BF16)

            def body(c, seen):
                x = sc_ref[rows(c), :]
                eq = x == thr
                eq_f = jnp.where(eq, 1.0, 0.0)
                rank = _dot(lower, eq_f.astype(BF16)) + seen
                keep = jnp.logical_or(x > thr, jnp.logical_and(eq, rank <= need))
                bias_ref[rows(c), :] = jnp.where(keep, 0.0, NEG_INF)
                return seen + jnp.sum(eq_f, axis=0, keepdims=True)

            lax.fori_loop(0, nk, body, jnp.zeros((1, QB), F32))

    acc_ref[...] = jnp.zeros_like(acc_ref)
    q_wide = heads_on_lanes(qt_ref, HEAD_DIM)

    def attend_chunk(c, m_old):
        kc = kv_ref[0, rows(c), :][:, :HEAD_DIM]
        bias = bias_ref[rows(c), :]
        logits = _dot(kc, q_wide) + jnp.concatenate([bias] * ATTN_HEADS, axis=1)
        m_new = jnp.maximum(m_old, _col_reduce(logits, jnp.maximum))
        alpha = jnp.exp2(m_old - m_new)
        p = jnp.exp2(logits - m_new)
        acc_ref[...] = alpha * acc_ref[...] + _dot(kvt_ref[0, c, HEAD_DIM - DSA_SUM_ROWS:, :], p.astype(BF16))
        return m_new

    lax.fori_loop(0, nk, attend_chunk, jnp.full((1, ATTN_HEADS * QB), -1e30, F32))
    out_t = acc_ref[DSA_SUM_ROWS:, :] / acc_ref[0:1, :]
    o_ref[0] = jnp.concatenate([out_t[:, head_lanes(hd)] for hd in range(ATTN_HEADS)], axis=0).T


def _dsa_attention_t(qt, iqt, iwt, kv, kvt, ik):
    B, _, S = qt.shape
    col = lambda b, j: (b, 0, j)
    per_b = lambda b, j: (b, 0, 0)
    return pl.pallas_call(
        _dsa_t_kernel,
        out_shape=jax.ShapeDtypeStruct((B, S, ATTN_WIDTH), F32),
        grid=(B, S // Q_BLOCK),
        in_specs=[
            pl.BlockSpec((1, ATTN_WIDTH, Q_BLOCK), col),
            pl.BlockSpec((1, IDX_HEADS * IDX_DIM, Q_BLOCK), col),
            pl.BlockSpec((1, LANES, Q_BLOCK), col),
            pl.BlockSpec((1, S, LANES), per_b),
            pl.BlockSpec((1, S // DSA_KEY_CHUNK, LANES, DSA_KEY_CHUNK), lambda b, j: (b, 0, 0, 0)),
            pl.BlockSpec((1, S, LANES), per_b),
        ],
        out_specs=pl.BlockSpec((1, Q_BLOCK, ATTN_WIDTH), lambda b, j: (b, j, 0)),
        scratch_shapes=[
            pltpu.VMEM((S, Q_BLOCK), F32),
            pltpu.VMEM((S, Q_BLOCK), F32),
            pltpu.VMEM((DSA_SUM_ROWS + HEAD_DIM, ATTN_HEADS * Q_BLOCK), F32),
        ],
        compiler_params=_cparams("parallel", "parallel"),
        name="dsa_attention",
    )(qt, iqt, iwt, kv, kvt, ik)


def _hyb_out_kernel(ya_ref, bcu_ref, halo_ref, cw_ref, w_ref, x_ref, g_ref,
                    gain2_ref, sc2_ref, sh2_ref, wr_ref, br_ref, o_ref, h_ref, r_ref, rt_ref, cnt_ref, run_ref, tri_ref):
    j = pl.program_id(1)
    tm = ya_ref.shape[1]
    bcu = bcu_ref[0]
    bg = bcu[:, 0:512]
    z = bcu[:, 512:1024] * bcu[:, 1024:1536]
    halo = halo_ref[0]
    zh = halo[:, 512:1024] * halo[:, 1024:1536]
    zh = jnp.where(j > 0, zh, 0.0)
    row = lax.broadcasted_iota(jnp.int32, (tm, 1), 0)
    z1 = jnp.where(row >= 1, pltpu.roll(z, 1, 0), zh[7:8, :])
    z2 = jnp.where(row >= 2, pltpu.roll(z, 2, 0), jnp.where(row == 1, zh[7:8, :], zh[6:7, :]))
    cw = cw_ref[...]
    y_conv = bg * (z2 * cw[0:1, :] + z1 * cw[1:2, :] + z * cw[2:3, :])
    y = _dot(ya_ref[0].astype(BF16), w_ref[0:512, :]) + _dot(y_conv.astype(BF16), w_ref[512:1024, :])
    x_new = x_ref[0] + g_ref[0] * y
    o_ref[0] = x_new
    _route_tile(x_new, gain2_ref, sc2_ref, sh2_ref, wr_ref, br_ref, h_ref, r_ref, rt_ref, cnt_ref, run_ref, tri_ref)


def _hyb_out(y_attn, bcu, conv_w, w_out_bf, x, g1, route_args):
    B, S, D = x.shape
    tm = OUT_TILE
    row = lambda b, j: (b, j, 0)
    per_b = lambda b, j: (b, 0, 0)
    const2 = lambda b, j: (0, 0)
    halo = lambda b, j: (b, jnp.maximum(j * (tm // 8) - 1, 0), 0)
    r_in, r_shape, r_out, r_scratch = _route_specs(B, S, D, tm)
    return pl.pallas_call(
        _hyb_out_kernel,
        out_shape=(jax.ShapeDtypeStruct((B, S, D), F32),) + r_shape,
        grid=(B, S // tm),
        in_specs=[
            pl.BlockSpec((1, tm, 512), row),
            pl.BlockSpec((1, tm, 1536), row),
            pl.BlockSpec((1, 8, 1536), halo),
            pl.BlockSpec((CONV_K, CONV_WIDTH), const2),
            pl.BlockSpec((D, D), const2),
            pl.BlockSpec((1, tm, D), row),
            pl.BlockSpec((1, 1, D), per_b),
        ] + r_in,
        out_specs=(pl.BlockSpec((1, tm, D), row),) + r_out,
        scratch_shapes=r_scratch,
        compiler_params=_cparams("arbitrary", "arbitrary"),
        name="hybrid_out_proj",
    )(y_attn, bcu, bcu, conv_w, w_out_bf, x, g1, *route_args)


ML_COLS = 512 + 512 + 1024 + 1024 + LANES


def _ml_in_kernel(x_ref, g2_ref, y0_ref, y1_ref, r_ref, gain_ref, sc_ref, sh_ref, w_ref, bias_ref,
                  q_ref, k_ref, v_ref, og_ref, gt_ref, gtt_ref, xo_ref):
    x = _residual_tile(x_ref, (g2_ref, y0_ref, y1_ref, r_ref))
    xo_ref[0] = x
    h = _modulated_norm(x, gain_ref[...], sc_ref[0], sh_ref[0])
    hb = h.astype(BF16)
    gates = _dot(hb, w_ref[:, 0:LANES]) + bias_ref[...]
    gates_t = gates.T[:2 * ML_HEADS, :]
    keys_t = _dot(hb, w_ref[:, LANES:LANES + 512]).T.astype(BF16)
    for i in range(gt_ref.shape[1]):
        span = slice(i * ML_CHUNK, (i + 1) * ML_CHUNK)
        gt_ref[0, i] = gates[span, :]
        gtt_ref[0, i] = gates_t[:, span]
        k_ref[0, i] = keys_t[:, span]
    q_ref[0] = (_dot(hb, w_ref[:, LANES + 512:LANES + 1024]) * (ML_QK_DIM ** -0.5)).astype(BF16)
    v_ref[0] = _dot(hb, w_ref[:, LANES + 1024:LANES + 2048]).astype(BF16)
    og_ref[0] = _dot(hb, w_ref[:, LANES + 2048:])


def _ml_in(x, pending, gain, sc, sh, w_pad, gate_bias):
    B, S, D = x.shape
    tm = TOKEN_TILE
    L = ML_CHUNK
    assert tm % L == 0
    per_tile = tm // L
    chunked = lambda b, j: (b, j, 0, 0)
    row = lambda b, j: (b, j, 0)
    per_b = lambda b, j: (b, 0, 0)
    const2 = lambda b, j: (0, 0)
    return pl.pallas_call(
        _ml_in_kernel,
        out_shape=(
            jax.ShapeDtypeStruct((B, S, 512), BF16),
            jax.ShapeDtypeStruct((B, S // L, 512, L), BF16),
            jax.ShapeDtypeStruct((B, S, 1024), BF16),
            jax.ShapeDtypeStruct((B, S, 1024), F32),
            jax.ShapeDtypeStruct((B, S // L, L, LANES), F32),
            jax.ShapeDtypeStruct((B, S // L, 2 * ML_HEADS, L), F32),
            jax.ShapeDtypeStruct((B, S, D), F32),
        ),
        grid=(B, S // tm),
        in_specs=[pl.BlockSpec((1, tm, D), row)] + _pending_specs(D, tm) + [
            pl.BlockSpec((1, D), const2),
            pl.BlockSpec((1, 1, D), per_b),
            pl.BlockSpec((1, 1, D), per_b),
            pl.BlockSpec((D, ML_COLS), const2),
            pl.BlockSpec((1, LANES), const2),
        ],
        out_specs=(
            pl.BlockSpec((1, tm, 512), row),
            pl.BlockSpec((1, per_tile, 512, L), chunked),
            pl.BlockSpec((1, tm, 1024), row),
            pl.BlockSpec((1, tm, 1024), row),
            pl.BlockSpec((1, per_tile, L, LANES), chunked),
            pl.BlockSpec((1, per_tile, 2 * ML_HEADS, L), chunked),
            pl.BlockSpec((1, tm, D), row),
        ),
        compiler_params=_cparams("parallel", "parallel"),
        name="mlstm_in_proj",
    )(x, *pending, gain, sc, sh, w_pad, gate_bias)


def _log_sigmoid(f):
    return jnp.minimum(f, 0.0) - jnp.log1p(jnp.exp(-jnp.abs(f)))


def _split3(x):
    a = x.astype(BF16)
    r = x - a.astype(F32)
    b = r.astype(BF16)
    c = (r - b.astype(F32)).astype(BF16)
    return a, b, c


def _twice(a):
    return jnp.concatenate([a, a], axis=1)


def _over_lanes(a, width):
    return jnp.concatenate([a] * (width // LANES), axis=1)


def _mlstm_kernel(q_ref, kt_ref, v_ref, grow_ref, gcol_ref, gain_ref, o_ref, c_ref, m_ref):
    L = ML_CHUNK
    HP = ML_STEP_HEADS
    S = q_ref.shape[1]
    c_ref[...] = jnp.zeros_like(c_ref)
    m_ref[...] = jnp.zeros_like(m_ref)

    def chunk(c, carry):
        r0 = pl.multiple_of(c * L, L)
        r_i = lax.broadcasted_iota(jnp.int32, (L, L), 0)
        c_i = lax.broadcasted_iota(jnp.int32, (L, L), 1)
        tril = c_i <= r_i
        lower = jnp.where(tril, 1.0, 0.0).astype(BF16)
        upper = jnp.where(r_i <= c_i, 1.0, 0.0).astype(BF16)
        e_r = lax.broadcasted_iota(jnp.int32, (LANES, HP * LANES), 0)
        e_c = lax.broadcasted_iota(jnp.int32, (LANES, HP * LANES), 1)
        pick = jnp.where(e_r == HP + e_c // LANES, 1.0, 0.0).astype(BF16)
        rows = grow_ref[0, 0, c]
        cols = gcol_ref[0, 0, c]
        b_rows = sum(_dot(p, upper) for p in _split3(_log_sigmoid(rows)))
        b_cols = sum(_dot(lower, p) for p in _split3(_log_sigmoid(cols)))
        b_colr = sum(_dot(p, pick) for p in _split3(b_cols))
        lane = lax.broadcasted_iota(jnp.int32, (1, L), 1)
        b_last_all = jnp.sum(jnp.where(lane == L - 1, b_rows, 0.0), axis=-1, keepdims=True)
        ones_v = jnp.ones((L, ML_V_DIM), BF16)
        for hh in range(HP):
            q = q_ref[0, pl.ds(r0, L), hh * ML_QK_DIM:(hh + 1) * ML_QK_DIM]
            kt = kt_ref[0, c, hh * ML_QK_DIM:(hh + 1) * ML_QK_DIM, :]
            v = v_ref[0, pl.ds(r0, L), hh * ML_V_DIM:(hh + 1) * ML_V_DIM]
            vx = jnp.concatenate([v, ones_v], axis=1)
            i_row = rows[hh:hh + 1, :]
            b_row = b_rows[HP + hh:HP + hh + 1, :]
            b_last = b_last_all[HP + hh:HP + hh + 1, :]
            b_col = b_colr[:, hh * LANES:(hh + 1) * LANES]
            m_prev = m_ref[hh]
            ctn = c_ref[hh]

            dmat = jnp.where(tril, _over_lanes(b_col, L) - b_row + i_row, NEG_INF)
            inter = b_col + m_prev
            m_t = jnp.maximum(inter, jnp.max(dmat, axis=-1, keepdims=True))
            w_intra = jnp.exp(dmat - _over_lanes(m_t, L))
            w_inter = jnp.exp(inter - m_t)
            intra = (w_intra * _dot(q, kt)).astype(BF16)
            tot = _twice(w_inter) * _dot(q, ctn.astype(BF16)) + _dot(intra, vx)
            num = tot[:, :ML_V_DIM]
            den = tot[:, ML_V_DIM:]
            hc = num / jnp.maximum(jnp.abs(den), jnp.exp(-m_t))
            y = hc * lax.rsqrt(jnp.mean(hc * hc, axis=-1, keepdims=True) + NORM_EPS)
            o_ref[0, pl.ds(r0, L), hh * ML_V_DIM:(hh + 1) * ML_V_DIM] = (
                y * gain_ref[:, hh * ML_V_DIM:(hh + 1) * ML_V_DIM])

            g_row = b_last - b_row + i_row
            m_new = jnp.maximum(b_last + m_prev, jnp.max(g_row, axis=-1, keepdims=True))
            decay = jnp.exp(b_last + m_prev - m_new)
            kw = (kt.astype(F32) * jnp.exp(g_row - _over_lanes(m_new, L))).astype(BF16)
            c_ref[hh] = _twice(decay) * ctn + _dot(kw, vx)
            m_ref[hh] = m_new
        return carry

    lax.fori_loop(0, S // L, chunk, 0)


def _mlstm(q, kt, v, g_rows, g_cols, out_gain):
    B, S, _ = q.shape
    assert ML_CHUNK % LANES == 0
    nc = S // ML_CHUNK
    hp = ML_STEP_HEADS
    return pl.pallas_call(
        _mlstm_kernel,
        out_shape=jax.ShapeDtypeStruct((B, S, ML_HEADS * ML_V_DIM), F32),
        grid=(B, ML_HEADS // hp),
        in_specs=[
            pl.BlockSpec((1, S, hp * ML_QK_DIM), lambda b, p: (b, 0, p)),
            pl.BlockSpec((1, nc, hp * ML_QK_DIM, ML_CHUNK), lambda b, p: (b, 0, p, 0)),
            pl.BlockSpec((1, S, hp * ML_V_DIM), lambda b, p: (b, 0, p)),
            pl.BlockSpec((1, 1, nc, 2 * hp, ML_CHUNK), lambda b, p: (b, p, 0, 0, 0)),
            pl.BlockSpec((1, 1, nc, ML_CHUNK, LANES), lambda b, p: (b, p, 0, 0, 0)),
            pl.BlockSpec((1, hp * ML_V_DIM), lambda b, p: (0, p)),
        ],
        out_specs=pl.BlockSpec((1, S, hp * ML_V_DIM), lambda b, p: (b, 0, p)),
        scratch_shapes=[
            pltpu.VMEM((hp, ML_QK_DIM, ML_V_DIM + LANES), F32),
            pltpu.VMEM((hp, 1, LANES), F32),
        ],
        compiler_params=_cparams("parallel", "parallel"),
        name="mlstm_chunkwise",
    )(q, kt, v, g_rows, g_cols, out_gain)


def _ml_out_kernel(hh_ref, og_ref, w_ref, x_ref, g_ref,
                   gain2_ref, sc2_ref, sh2_ref, wr_ref, br_ref, o_ref, h_ref, r_ref, rt_ref, cnt_ref, run_ref, tri_ref):
    a = jax.nn.sigmoid(og_ref[0]) * hh_ref[0]
    x_new = x_ref[0] + g_ref[0] * _dot(a.astype(BF16), w_ref[...])
    o_ref[0] = x_new
    _route_tile(x_new, gain2_ref, sc2_ref, sh2_ref, wr_ref, br_ref, h_ref, r_ref, rt_ref, cnt_ref, run_ref, tri_ref)


def _ml_out(hh, og, w_out_bf, x, g1, route_args):
    B, S, D = x.shape
    tm = OUT_TILE
    row = lambda b, j: (b, j, 0)
    per_b = lambda b, j: (b, 0, 0)
    const2 = lambda b, j: (0, 0)
    r_in, r_shape, r_out, r_scratch = _route_specs(B, S, D, tm)
    return pl.pallas_call(
        _ml_out_kernel,
        out_shape=(jax.ShapeDtypeStruct((B, S, D), F32),) + r_shape,
        grid=(B, S // tm),
        in_specs=[
            pl.BlockSpec((1, tm, D), row),
            pl.BlockSpec((1, tm, D), row),
            pl.BlockSpec((D, D), const2),
            pl.BlockSpec((1, tm, D), row),
            pl.BlockSpec((1, 1, D), per_b),
        ] + r_in,
        out_specs=(pl.BlockSpec((1, tm, D), row),) + r_out,
        scratch_shapes=r_scratch,
        compiler_params=_cparams("arbitrary", "arbitrary"),
        name="mlstm_out_proj",
    )(hh, og, w_out_bf, x, g1, *route_args)


def _first_argmax(x, lane, width):
    mx = jnp.max(x, axis=-1, keepdims=True)
    idx = jnp.min(jnp.where(x == mx, lane, width), axis=-1, keepdims=True)
    return mx, idx


def _route_tile(x, gain_ref, sc_ref, sh_ref, w_ref, b_ref, h_ref, r_ref, rt_ref, cnt_ref, run_ref, tri_ref):
    tm = x.shape[0]

    @pl.when(jnp.logical_and(pl.program_id(0) == 0, pl.program_id(1) == 0))
    def _():
        run_ref[...] = jnp.zeros_like(run_ref)
        r_i = lax.broadcasted_iota(jnp.int32, (tm, tm), 0)
        c_i = lax.broadcasted_iota(jnp.int32, (tm, tm), 1)
        tri_ref[...] = jnp.where(c_i < r_i, 1.0, 0.0).astype(BF16)

    h = _modulated_norm(x, gain_ref[...], sc_ref[0], sh_ref[0])
    h_ref[0] = _pack_bf16_pairs(h)
    logits = _dot(h.astype(BF16), w_ref[...]) + b_ref[...]
    lane = lax.broadcasted_iota(jnp.int32, (1, LANES), 1)

    def pick(lgt):
        lg = jnp.where(lane < N_GROUPS, lgt, NEG_INF)
        g_max, g_sel = _first_argmax(lg, lane, LANES)
        pg = 1.0 / jnp.sum(jnp.exp(lg - g_max), axis=-1, keepdims=True)
        e_lane = lane - N_GROUPS
        in_grp = jnp.logical_and(e_lane >= g_sel * EXPERTS_PER_GROUP, e_lane < (g_sel + 1) * EXPERTS_PER_GROUP)
        le = jnp.where(in_grp, lgt, NEG_INF)
        v1, i1 = _first_argmax(le, lane, LANES)
        v2, i2 = _first_argmax(jnp.where(lane == i1, NEG_INF, le), lane, LANES)
        ratio = jnp.exp(v2 - v1)
        return i1 - N_GROUPS, i2 - N_GROUPS, pg / (1.0 + ratio), pg * ratio / (1.0 + ratio)

    e1, e2, w1, w2 = pick(logits)
    hot1 = lane == e1
    hot2 = lane == e2
    onehot = jnp.where(jnp.logical_or(hot1, hot2), 1.0, 0.0)
    seen = _dot(tri_ref[...], onehot.astype(BF16)) + run_ref[...]
    rank1 = jnp.sum(jnp.where(hot1, seen, 0.0), axis=-1, keepdims=True)
    rank2 = jnp.sum(jnp.where(hot2, seen, 0.0), axis=-1, keepdims=True)
    run_ref[...] = run_ref[...] + jnp.sum(onehot, axis=0, keepdims=True)
    cnt_ref[...] = run_ref[...]

    out = jnp.where(lane == 0, e1.astype(F32), 0.0)
    out = jnp.where(lane == 1, e2.astype(F32), out)
    out = jnp.where(lane == 2, w1, out)
    out = jnp.where(lane == 3, w2, out)
    out = jnp.where(lane == 4, rank1, out)
    out = jnp.where(lane == 5, rank2, out)
    r_ref[0] = out
    rt_ref[0] = out.T[:8, :]


def _route_specs(B, S, D, tm):
    row = lambda b, j: (b, j, 0)
    per_b = lambda b, j: (b, 0, 0)
    const2 = lambda b, j: (0, 0)
    in_specs = [
        pl.BlockSpec((1, D), const2),
        pl.BlockSpec((1, 1, D), per_b),
        pl.BlockSpec((1, 1, D), per_b),
        pl.BlockSpec((D, LANES), const2),
        pl.BlockSpec((1, LANES), const2),
    ]
    out_shape = (
        jax.ShapeDtypeStruct((B, S, D // 2), jnp.int32),
        jax.ShapeDtypeStruct((B, S, LANES), F32),
        jax.ShapeDtypeStruct((B, 8, S), F32),
        jax.ShapeDtypeStruct((1, LANES), F32),
    )
    out_specs = (
        pl.BlockSpec((1, tm, D // 2), row),
        pl.BlockSpec((1, tm, LANES), row),
        pl.BlockSpec((1, 8, tm), lambda b, j: (b, 0, j)),
        pl.BlockSpec((1, LANES), const2),
    )
    return in_specs, out_shape, out_specs, [pltpu.VMEM((1, LANES), F32), pltpu.VMEM((tm, tm), BF16)]


def _experts_kernel(blk_e_ref, n_used_ref, next_e_ref, x_ref, wg_hbm, wu_hbm, wd_hbm, o_ref,
                    wg_f, wu_f, wd_f, wg_s, wu_s, wd_s, sem, *, layer):
    i = pl.program_id(0)
    used = i < n_used_ref[0]
    e = blk_e_ref[i]
    new_expert = jnp.logical_or(i == 0, e != blk_e_ref[jnp.maximum(i - 1, 0)])

    def fetch(expert):
        return (pltpu.make_async_copy(wg_hbm.at[layer, expert], wg_f, sem.at[0]),
                pltpu.make_async_copy(wu_hbm.at[layer, expert], wu_f, sem.at[1]),
                pltpu.make_async_copy(wd_hbm.at[layer, expert], wd_f, sem.at[2]))

    @pl.when(i == 0)
    def _():
        for cp in fetch(e):
            cp.start()

    @pl.when(jnp.logical_and(used, new_expert))
    def _():
        for cp in fetch(e):
            cp.wait()
        wg_s[...] = wg_f[...].astype(BF16)
        wu_s[...] = wu_f[...].astype(BF16)
        wd_s[...] = wd_f[...].astype(BF16)

        @pl.when(next_e_ref[i] >= 0)
        def _():
            for cp in fetch(next_e_ref[i]):
                cp.start()

    @pl.when(used)
    def _():
        x = _unpack_bf16_pairs(x_ref[...]).astype(BF16)
        a = _dot(x, wg_s[...])
        u = _dot(x, wu_s[...])
        act = a * jax.nn.sigmoid(a) * u
        o_ref[...] = _pack_bf16_pairs(_dot(act.astype(BF16), wd_s[...]))

    @pl.when(i >= n_used_ref[0])
    def _():
        o_ref[...] = jnp.zeros_like(o_ref)


def _experts(layer, blk_e, n_used, next_e, xs, w_gate, w_up, w_down):
    R = xs.shape[0]
    D = 2 * xs.shape[1]
    n_blk = R // MOE_BLOCK
    rows = lambda i, be, nu, ne: (i, 0)
    grid_spec = pltpu.PrefetchScalarGridSpec(
        num_scalar_prefetch=3,
        grid=(n_blk,),
        in_specs=[
            pl.BlockSpec((MOE_BLOCK, D // 2), rows),
            pl.BlockSpec(memory_space=pl.ANY),
            pl.BlockSpec(memory_space=pl.ANY),
            pl.BlockSpec(memory_space=pl.ANY),
        ],
        out_specs=pl.BlockSpec((MOE_BLOCK, D // 2), rows),
        scratch_shapes=[
            pltpu.VMEM((D, D_EXPERT), F32),
            pltpu.VMEM((D, D_EXPERT), F32),
            pltpu.VMEM((D_EXPERT, D), F32),
            pltpu.VMEM((D, D_EXPERT), BF16),
            pltpu.VMEM((D, D_EXPERT), BF16),
            pltpu.VMEM((D_EXPERT, D), BF16),
            pltpu.SemaphoreType.DMA((3,)),
        ],
    )
    return pl.pallas_call(
        functools.partial(_experts_kernel, layer=layer),
        out_shape=jax.ShapeDtypeStruct((R, D // 2), jnp.int32),
        grid_spec=grid_spec,
        compiler_params=_cparams("arbitrary"),
        name="moe_experts",
    )(blk_e, n_used, next_e, xs, w_gate, w_up, w_down)


def _residual_tile(x_ref, pending_refs):
    if not pending_refs:
        return x_ref[0]
    g_ref, y0_ref, y1_ref, r_ref = pending_refs
    r = r_ref[0]
    y = _unpack_bf16_pairs(y0_ref[0, 0]) * r[:, 2:3] + _unpack_bf16_pairs(y1_ref[0, 0]) * r[:, 3:4]
    return x_ref[0] + g_ref[0] * y


def _pending_specs(D, tm):
    return [
        pl.BlockSpec((1, 1, D), lambda b, j: (b, 0, 0)),
        pl.BlockSpec((1, 1, tm, D // 2), lambda b, j: (0, b, j, 0)),
        pl.BlockSpec((1, 1, tm, D // 2), lambda b, j: (1, b, j, 0)),
        pl.BlockSpec((1, tm, LANES), lambda b, j: (b, j, 0)),
    ]


def _combine_kernel(x_ref, g_ref, y0_ref, y1_ref, r_ref, o_ref):
    o_ref[0] = _residual_tile(x_ref, (g_ref, y0_ref, y1_ref, r_ref))


def _combine(x, pending):
    B, S, D = x.shape
    tm = OUT_TILE
    row = lambda b, j: (b, j, 0)
    return pl.pallas_call(
        _combine_kernel,
        out_shape=jax.ShapeDtypeStruct((B, S, D), F32),
        grid=(B, S // tm),
        in_specs=[pl.BlockSpec((1, tm, D), row)] + _pending_specs(D, tm),
        out_specs=pl.BlockSpec((1, tm, D), row),
        compiler_params=_cparams("parallel", "parallel"),
        name="moe_combine",
    )(x, *pending)


SC_CORES = 2
SC_SUBCORES = 16
SC_WORKERS = SC_CORES * SC_SUBCORES
SC_CHUNK = 64


def _sc_mesh():
    return plsc.VectorSubcoreMesh(core_axis_name="c", subcore_axis_name="s",
                                  num_cores=SC_CORES, num_subcores=SC_SUBCORES)


def _sc_scatter_rows(src, idx, n_out):
    T, W = src.shape
    per_w = T // SC_WORKERS
    nch = per_w // SC_CHUNK
    idx4 = idx.reshape(TOP_K, SC_WORKERS, nch, SC_CHUNK)

    @functools.partial(
        pl.kernel, mesh=_sc_mesh(),
        out_type=jax.ShapeDtypeStruct((n_out, W), src.dtype),
        scratch_types=[
            pltpu.VMEM((TOP_K, nch, SC_CHUNK), jnp.int32),
            pltpu.VMEM((2, SC_CHUNK, W), src.dtype),
            pltpu.SemaphoreType.DMA((2,)),
            pltpu.SemaphoreType.DMA((2 * TOP_K,)),
        ],
        name="sc_scatter_rows",
    )
    def body(src_hbm, idx_hbm, out_hbm, idx_v, rows_v, load_sem, scatter_sem):
        wid = lax.axis_index("s") * SC_CORES + lax.axis_index("c")
        for s in range(TOP_K):
            pltpu.sync_copy(idx_hbm.at[s, wid], idx_v.at[s])

        def load(j, b):
            return pltpu.make_async_copy(
                src_hbm.at[pl.ds(wid * per_w + j * SC_CHUNK, SC_CHUNK)], rows_v.at[b], load_sem.at[b])

        def scatter(s, j, b):
            return pltpu.make_async_copy(rows_v.at[b], out_hbm.at[idx_v.at[s, j]], scatter_sem.at[2 * s + b])

        load(0, 0).start()

        @pl.loop(0, nch, step=2)
        def _(i):
            for b in range(2):
                j = i + b

                @pl.when(j >= 1)
                def _():
                    for s in range(TOP_K):
                        scatter(s, j - 1, 1 - b).wait()

                @pl.when(j + 1 < nch)
                def _():
                    load(j + 1, 1 - b).start()

                load(j, b).wait()
                for s in range(TOP_K):
                    scatter(s, j, b).start()

        for s in range(TOP_K):
            scatter(s, nch - 1, (nch - 1) % 2).wait()

    assert nch % 2 == 0
    return body(src, idx4)


def _sc_gather_rows(table, idx):
    N = idx.shape[0]
    W = table.shape[1]
    per_w = N // SC_WORKERS
    nch = per_w // SC_CHUNK
    idx3 = idx.reshape(SC_WORKERS, nch, SC_CHUNK)

    @functools.partial(
        pl.kernel, mesh=_sc_mesh(),
        out_type=jax.ShapeDtypeStruct((N, W), table.dtype),
        scratch_types=[
            pltpu.VMEM((nch, SC_CHUNK), jnp.int32),
            pltpu.VMEM((2, SC_CHUNK, W), table.dtype),
            pltpu.SemaphoreType.DMA((2,)),
            pltpu.SemaphoreType.DMA((2,)),
        ],
        name="sc_gather_rows",
    )
    def body(table_hbm, idx_hbm, out_hbm, idx_v, rows_v, gather_sem, write_sem):
        wid = lax.axis_index("s") * SC_CORES + lax.axis_index("c")
        pltpu.sync_copy(idx_hbm.at[wid], idx_v)

        def gather(j, b):
            return pltpu.make_async_copy(table_hbm.at[idx_v.at[j]], rows_v.at[b], gather_sem.at[b])

        def write(j, b):
            return pltpu.make_async_copy(
                rows_v.at[b], out_hbm.at[pl.ds(wid * per_w + j * SC_CHUNK, SC_CHUNK)], write_sem.at[b])

        gather(0, 0).start()

        @pl.loop(0, nch, step=2)
        def _(i):
            for b in range(2):
                j = i + b

                @pl.when(j >= 1)
                def _():
                    write(j - 1, 1 - b).wait()

                @pl.when(j + 1 < nch)
                def _():
                    gather(j + 1, 1 - b).start()

                gather(j, b).wait()
                write(j, b).start()

        write(nch - 1, (nch - 1) % 2).wait()

    assert nch % 2 == 0
    return body(table, idx3)


def _moe_dispatch(route_t, counts, T):
    A = T * TOP_K
    counts = counts[0, :N_EXPERTS].astype(jnp.int32)
    blocks_per = (counts + MOE_BLOCK - 1) // MOE_BLOCK
    block_end = jnp.cumsum(blocks_per)
    block_start = block_end - blocks_per
    expert = jnp.swapaxes(route_t[:, :TOP_K, :], 0, 1).reshape(TOP_K, T).astype(jnp.int32)
    rank = jnp.swapaxes(route_t[:, 4:4 + TOP_K, :], 0, 1).reshape(TOP_K, T).astype(jnp.int32)
    onehot = expert[None] == jnp.arange(N_EXPERTS, dtype=jnp.int32)[:, None, None]
    start = jnp.sum(jnp.where(onehot, block_start[:, None, None], 0), axis=0)
    dest = start * MOE_BLOCK + rank
    n_blk = -(-A // MOE_BLOCK) + N_EXPERTS
    blk = jnp.arange(n_blk, dtype=jnp.int32)
    blk_e = jnp.minimum(jnp.sum(blk[:, None] >= block_end[None, :], axis=-1), N_EXPERTS - 1).astype(jnp.int32)
    n_used = block_end[-1]
    first = jnp.logical_and(blk < n_used, jnp.logical_or(blk == 0, blk_e != jnp.roll(blk_e, 1)))
    first_pos = jnp.where(first, blk, n_blk)
    next_pos = jnp.concatenate([lax.cummin(first_pos, axis=0, reverse=True)[1:], jnp.full((1,), n_blk, jnp.int32)])
    next_e = jnp.where(next_pos < n_blk, blk_e[jnp.minimum(next_pos, n_blk - 1)], -1).astype(jnp.int32)
    return dest, n_blk * MOE_BLOCK, blk_e, n_used.reshape(1).astype(jnp.int32), next_e


def _rope_tables(S):
    inv = 1.0 / (ROPE_THETA ** (jnp.arange(0, HEAD_DIM, 2, dtype=F32) / HEAD_DIM))
    ang = jnp.arange(S, dtype=F32)[:, None] * inv[None, :]
    cos, sin = jnp.cos(ang), jnp.sin(ang)
    cos_h = jnp.concatenate([cos, cos], axis=-1)
    sin_h = jnp.concatenate([-sin, sin], axis=-1)
    return jnp.tile(cos_h, (1, ATTN_HEADS)), jnp.tile(sin_h, (1, ATTN_HEADS))


def _pad_cols(w, width):
    return jnp.pad(w, ((0, 0), (0, width - w.shape[1])))


def kernel(x, c, ada_w, ada_b, norm_mix, norm_ffn, hy_w_in, hy_q_norm, hy_k_norm, hy_conv_w, hy_w_out, ml_w_in, ml_b_gates, ml_out_norm, ml_w_out, moe_w_group, moe_b_group, moe_w_expert, moe_b_expert, moe_w_gate, moe_w_up, moe_w_down):
    B, S, D = x.shape
    T = B * S
    cos_t, sin_t = _rope_tables(S)
    mod = _ada_modulation(c, ada_w, ada_b).reshape(DEPTH, B, 6, 1, D)
    r_i = np.arange(ATTN_WIDTH)
    grp = jnp.asarray((r_i[:, None] // HEAD_DIM) == (r_i[None, :] // HEAD_DIM), dtype=BF16)

    pending = ()
    for l in range(DEPTH):
        sh1, sc1, g1, sh2, sc2, g2 = [mod[l, :, i] for i in range(6)]
        gain1 = norm_mix[l].reshape(1, D)
        w_r = _pad_cols(jnp.concatenate([moe_w_group[l], moe_w_expert[l]], axis=1), LANES).astype(BF16)
        b_r = jnp.pad(jnp.concatenate([moe_b_group[l], moe_b_expert[l]]), (0, LANES - N_GROUPS - N_EXPERTS))
        route_args = (norm_ffn[l].reshape(1, D), sc2, sh2, w_r, b_r.reshape(1, LANES))
        j = l // 2
        if l % 2 == 0:
            w = hy_w_in[j]
            o = np.cumsum((0,) + (ATTN_WIDTH, HEAD_DIM, HEAD_DIM, IDX_HEADS * IDX_DIM, IDX_DIM, IDX_HEADS,
                                  CONV_WIDTH, CONV_WIDTH, CONV_WIDTH))
            wq, wk, wv, wiq, wik, wiw, wbg, wcg, wu = [w[:, o[i]:o[i + 1]] for i in range(9)]
            w_pad = jnp.concatenate(
                [wk, wv, _pad_cols(jnp.concatenate([wik, wiw], axis=1), LANES), wq, wiq, wbg, wcg, wu],
                axis=1).astype(BF16)
            qn_t = jnp.tile(hy_q_norm[j], ATTN_HEADS).reshape(1, ATTN_WIDTH)
            kn_t = jnp.tile(hy_k_norm[j], LANES // HEAD_DIM).reshape(1, LANES)
            outs = _hyb_in(x, pending, gain1, sc1, sh1, w_pad, cos_t, sin_t, qn_t, kn_t, grp)
            qt, iqt, bcu, kv, kvt, ik, iwt = outs[:7]
            if pending:
                x = outs[7]
            y_attn = _dsa_attention_t(qt, iqt, iwt, kv, kvt, ik)
            x, h2, route, route_t, counts =_hyb_out(y_attn, bcu, hy_conv_w[j], hy_w_out[j].astype(BF16), x, g1, route_args)
        else:
            w = ml_w_in[j]
            hq = ML_HEADS * ML_QK_DIM
            hv = ML_HEADS * ML_V_DIM
            wq, wk, wv = w[:, :hq], w[:, hq:2 * hq], w[:, 2 * hq:2 * hq + hv]
            wg = w[:, 2 * hq + hv:2 * hq + hv + 2 * ML_HEADS]
            wo = w[:, 2 * hq + hv + 2 * ML_HEADS:]
            w_pad = jnp.concatenate([_pad_cols(wg, LANES), wk, wq, wv, wo], axis=1).astype(BF16)
            gate_bias = jnp.pad(ml_b_gates[j], (0, LANES - 2 * ML_HEADS)).reshape(1, LANES)
            q, k, v, og, g_cols, g_rows, x = _ml_in(x, pending, gain1, sc1, sh1, w_pad, gate_bias)
            assert ML_STEP_HEADS == ML_HEADS
            hh = _mlstm(q, k, v, g_rows[:, None], g_cols[:, None], ml_out_norm[j].reshape(1, hv))
            x, h2, route, route_t, counts =_ml_out(hh, og, ml_w_out[j].astype(BF16), x, g1, route_args)

        dest, n_rows, blk_e, n_used, next_e = _moe_dispatch(route_t, counts, T)
        xs = _sc_scatter_rows(h2.reshape(T, D // 2), dest, n_rows)
        ys = _experts(l, blk_e, n_used, next_e, xs, moe_w_gate, moe_w_up, moe_w_down)
        y01 = _sc_gather_rows(ys, dest.reshape(TOP_K * T)).reshape(TOP_K, B, S, D // 2)
        pending = (g2, y01, y01, route)
    return _combine(x, pending)
```
